```python
import jax, jax.numpy as jnp
from jax import lax
import numpy as np

D_MODEL = 1024
BATCH = 8
SEQ = 4096
DEPTH = 2

MIX_WIDTH = 512
N_BRANCH = 4
MLA_HEADS = 8
QK_NOPE = 64
QK_ROPE = 32
V_HEAD = 64
Q_LORA = 384
KV_LORA = 256
ROPE_THETA = 10000.0
Q_BLOCK = 128
POOL_WINDOWS = (2, 4, 8, 16)
POOL_GROUP = MIX_WIDTH // 4
SSD_HEADS = 8
SSD_HEADDIM = 64
SSD_GROUPS = 2
SSD_STATE = 64
SSD_CHUNK = 128
CONV_WIDTH = 4
SSD_XBC = SSD_HEADS * SSD_HEADDIM + 2 * SSD_GROUPS * SSD_STATE
LRU_BLOCKS = 8
LRU_BLOCK = MIX_WIDTH // LRU_BLOCKS
LRU_C = 8.0
D_FF = 4 * D_MODEL
PLE_DIM = 256
EPS = 1e-6

SPLIT_SIZES = (Q_LORA, KV_LORA, QK_ROPE,
               MIX_WIDTH,
               MIX_WIDTH, SSD_XBC, SSD_HEADS,
               MIX_WIDTH, MIX_WIDTH,
               N_BRANCH * D_MODEL)
IN_COLS = sum(SPLIT_SIZES)

kernel_name = "hybrid_gated_mla_pool_ssd_rglru_block"


def _split_points():
    pts, acc = [], 0
    for s in SPLIT_SIZES[:-1]:
        acc += s
        pts.append(acc)
    return pts


def rmsnorm(x, g):
    x32 = x.astype(jnp.float32)
    y = x32 * lax.rsqrt(jnp.mean(x32 * x32, axis=-1, keepdims=True) + EPS)
    return (y * g.astype(jnp.float32)).astype(x.dtype)


def causal_dwconv(x, w, b):
    c = x.shape[-1]
    y = lax.conv_general_dilated(x, w[:, None, :].astype(x.dtype), window_strides=(1,),
                                 padding=[(CONV_WIDTH - 1, 0)],
                                 dimension_numbers=('NWC', 'WIO', 'NWC'),
                                 feature_group_count=c)
    return y + b.astype(x.dtype)


def rope_tables(positions):
    inv = 1.0 / (ROPE_THETA ** (jnp.arange(0, QK_ROPE, 2, dtype=jnp.float32) / QK_ROPE))
    ang = positions.astype(jnp.float32)[..., None] * inv
    return jnp.cos(ang), jnp.sin(ang)


def apply_rope(x, cos, sin):
    x32 = x.astype(jnp.float32)
    x1, x2 = jnp.split(x32, 2, axis=-1)
    out = jnp.concatenate([x1 * cos - x2 * sin, x2 * cos + x1 * sin], axis=-1)
    return out.astype(x.dtype)


def mla_mixer(c_q, c_kv, k_r, cos, sin, q_norm, w_uq, kv_norm, w_ukv):
    b, s, _ = c_q.shape
    q = (rmsnorm(c_q, q_norm) @ w_uq).reshape(b, s, MLA_HEADS, QK_NOPE + QK_ROPE)
    q_nope = q[..., :QK_NOPE]
    q_rope = apply_rope(q[..., QK_NOPE:], cos[:, :, None], sin[:, :, None])
    kv = (rmsnorm(c_kv, kv_norm) @ w_ukv).reshape(b, s, MLA_HEADS, QK_NOPE + V_HEAD)
    k_nope, v = kv[..., :QK_NOPE], kv[..., QK_NOPE:]
    k_rope = apply_rope(k_r, cos, sin)
    scale = (QK_NOPE + QK_ROPE) ** -0.5
    outs = []
    for blk in range(s // Q_BLOCK):
        q0, kend = blk * Q_BLOCK, (blk + 1) * Q_BLOCK
        sc = (jnp.einsum('bqhd,bkhd->bhqk', q_nope[:, q0:kend], k_nope[:, :kend])
              + jnp.einsum('bqhd,bkd->bhqk', q_rope[:, q0:kend], k_rope[:, :kend]))
        sc = sc.astype(jnp.float32) * scale
        qi = q0 + jnp.arange(Q_BLOCK)[:, None]
        ki = jnp.arange(kend)[None, :]
        sc = jnp.where(ki <= qi, sc, -jnp.inf)
        pr = jax.nn.softmax(sc, axis=-1).astype(v.dtype)
        outs.append(jnp.einsum('bhqk,bkhd->bqhd', pr, v[:, :kend]))
    o = jnp.concatenate(outs, axis=1)
    return o.reshape(b, s, MLA_HEADS * V_HEAD)


def pool_mixer(u, w_pool, pool_scale):
    b, s, _ = u.shape
    u32 = u.astype(jnp.float32)
    maxw = max(POOL_WINDOWS)
    cs = jnp.pad(jnp.cumsum(u32, axis=1), ((0, 0), (maxw, 0), (0, 0)))
    t = jnp.arange(s)
    groups = []
    for g, w in enumerate(POOL_WINDOWS):
        sl = slice(g * POOL_GROUP, (g + 1) * POOL_GROUP)
        win_sum = cs[:, maxw:, sl] - cs[:, maxw - w:maxw - w + s, sl]
        count = jnp.minimum(t + 1, w).astype(jnp.float32)[None, :, None]
        groups.append(win_sum / count - u32[..., sl])
    d = jnp.stack(groups, axis=2).astype(u.dtype)
    y = jnp.einsum('bsgc,gcd->bsgd', d, w_pool).reshape(b, s, MIX_WIDTH)
    return y * pool_scale


def segsum(a):
    t = a.shape[-1]
    cs = jnp.cumsum(a, axis=-1)
    diff = cs[..., :, None] - cs[..., None, :]
    mask = jnp.tril(jnp.ones((t, t), dtype=bool))
    return jnp.where(mask, diff, -jnp.inf)


def ssd_mixer(z, xbc, dt, conv_w, conv_b, dt_bias, a_log, d_skip, norm_g):
    b, s, _ = z.shape
    nc, lc, g, r, n, hp = s // SSD_CHUNK, SSD_CHUNK, SSD_GROUPS, SSD_HEADS // SSD_GROUPS, SSD_STATE, SSD_HEADDIM
    xbc = jax.nn.silu(causal_dwconv(xbc, conv_w, conv_b)).astype(jnp.float32)
    xs = xbc[..., :MIX_WIDTH]
    bm = xbc[..., MIX_WIDTH:MIX_WIDTH + g * n].reshape(b, nc, lc, g, n)
    cm = xbc[..., MIX_WIDTH + g * n:].reshape(b, nc, lc, g, n)
    dt = jax.nn.softplus(dt.astype(jnp.float32) + dt_bias.astype(jnp.float32))
    a_head = -jnp.exp(a_log.astype(jnp.float32))
    x = xs.reshape(b, nc, lc, g, r, hp)
    xdt = x * dt.reshape(b, nc, lc, g, r)[..., None]
    a = (dt * a_head).reshape(b, nc, lc, g, r).transpose(0, 3, 4, 1, 2)
    a_cs = jnp.cumsum(a, axis=-1)
    lmat = jnp.exp(segsum(a))
    cb = jnp.einsum('bclgn,bcsgn->bgcls', cm, bm)
    y_diag = jnp.einsum('bgrcls,bcsgrp->bclgrp', cb[:, :, None] * lmat, xdt)
    decay_states = jnp.exp(a_cs[..., -1:] - a_cs)
    states = jnp.einsum('bclgn,bgrcl,bclgrp->bcgrpn', bm, decay_states, xdt)
    states = jnp.concatenate([jnp.zeros_like(states[:, :1]), states], axis=1)
    chunk_a = jnp.pad(a_cs[..., -1], ((0, 0), (0, 0), (0, 0), (1, 0)))
    decay_chunk = jnp.exp(segsum(chunk_a))
    states = jnp.einsum('bgrzc,bcgrpn->bzgrpn', decay_chunk, states)[:, :-1]
    y_off = jnp.einsum('bclgn,bcgrpn,bgrcl->bclgrp', cm, states, jnp.exp(a_cs))
    y = (y_diag + y_off).reshape(b, s, SSD_HEADS, hp) \
        + xs.reshape(b, s, SSD_HEADS, hp) * d_skip.astype(jnp.float32)[:, None]
    y = y.reshape(b, s, MIX_WIDTH) * jax.nn.silu(z.astype(jnp.float32))
    return rmsnorm(y, norm_g).astype(z.dtype)


def rglru_mixer(gate_in, x_in, conv_w, conv_b, w_a, b_a, w_i, b_i, lam):
    b, s, _ = x_in.shape
    gate = jax.nn.gelu(gate_in)
    xc = causal_dwconv(x_in, conv_w, conv_b)
    xb = xc.reshape(b, s, LRU_BLOCKS, LRU_BLOCK)
    r_t = jax.nn.sigmoid((jnp.einsum('bshi,hij->bshj', xb, w_a).reshape(b, s, MIX_WIDTH) + b_a).astype(jnp.float32))
    i_t = jax.nn.sigmoid((jnp.einsum('bshi,hij->bshj', xb, w_i).reshape(b, s, MIX_WIDTH) + b_i).astype(jnp.float32))
    log_a = -LRU_C * r_t * jax.nn.softplus(-lam.astype(jnp.float32))
    a_t = jnp.exp(log_a)
    mult = jnp.sqrt(-jnp.expm1(2.0 * log_a))
    u = xc.astype(jnp.float32) * i_t * mult

    def combine(lhs, rhs):
        a1, b1 = lhs
        a2, b2 = rhs
        return a1 * a2, a2 * b1 + b2

    _, h = lax.associative_scan(combine, (a_t, u), axis=1)
    return h.astype(x_in.dtype) * gate


def _fwd_setup_inputs(seed: int = 0) -> dict:
    key = jax.random.key(seed)
    ks = iter(jax.random.split(key, 48))
    f32 = jnp.float32

    def nrm(shape, fan_in):
        return jax.random.normal(next(ks), shape, f32) * (fan_in ** -0.5)

    def gain(shape):
        return 1.0 + 0.05 * jax.random.normal(next(ks), shape, f32)

    def small(shape):
        return 0.01 * jax.random.normal(next(ks), shape, f32)

    L = DEPTH
    x = jax.random.normal(next(ks), (BATCH, SEQ, D_MODEL), f32)
    p = jax.random.normal(next(ks), (DEPTH, BATCH, SEQ, PLE_DIM), f32)
    offs = jax.random.randint(next(ks), (BATCH, 1), 0, 1024, dtype=jnp.int32)
    positions = (offs + jnp.arange(SEQ, dtype=jnp.int32)[None, :]).astype(jnp.int32)
    dt0 = jnp.exp(jax.random.uniform(next(ks), (L, SSD_HEADS), f32, np.log(1e-3), np.log(1e-1)))
    dt_bias = dt0 + jnp.log(-jnp.expm1(-dt0))
    a_log = jnp.log(jax.random.uniform(next(ks), (L, SSD_HEADS), f32, 1.0, 16.0))
    a_pow = jax.random.uniform(next(ks), (L, MIX_WIDTH), f32, 0.9, 0.999) ** (1.0 / LRU_C)
    lam = jnp.log(a_pow) - jnp.log1p(-a_pow)
    return {
        "x": x,
        "p": p,
        "positions": positions,
        "g_mix": gain((L, D_MODEL)),
        "w_in": nrm((L, D_MODEL, IN_COLS), D_MODEL),
        "q_norm": gain((L, Q_LORA)),
        "w_uq": nrm((L, Q_LORA, MLA_HEADS * (QK_NOPE + QK_ROPE)), Q_LORA),
        "kv_norm": gain((L, KV_LORA)),
        "w_ukv": nrm((L, KV_LORA, MLA_HEADS * (QK_NOPE + V_HEAD)), KV_LORA),
        "w_pool": nrm((L, 4, POOL_GROUP, POOL_GROUP), POOL_GROUP),
        "pool_scale": 1.0 + 0.1 * jax.random.normal(next(ks), (L, MIX_WIDTH), f32),
        "ssd_conv_w": nrm((L, CONV_WIDTH, SSD_XBC), CONV_WIDTH),
        "ssd_conv_b": small((L, SSD_XBC)),
        "ssd_dt_bias": dt_bias,
        "ssd_a_log": a_log,
        "ssd_d": gain((L, SSD_HEADS)),
        "ssd_norm": gain((L, MIX_WIDTH)),
        "lru_conv_w": nrm((L, CONV_WIDTH, MIX_WIDTH), CONV_WIDTH),
        "lru_conv_b": small((L, MIX_WIDTH)),
        "lru_w_a": nrm((L, LRU_BLOCKS, LRU_BLOCK, LRU_BLOCK), LRU_BLOCK),
        "lru_b_a": small((L, MIX_WIDTH)),
        "lru_w_i": nrm((L, LRU_BLOCKS, LRU_BLOCK, LRU_BLOCK), LRU_BLOCK),
        "lru_b_i": small((L, MIX_WIDTH)),
        "lru_lambda": lam,
        "w_branch": nrm((L, N_BRANCH, MIX_WIDTH, D_MODEL), MIX_WIDTH),
        "w_out": nrm((L, D_MODEL, D_MODEL), D_MODEL),
        "g_mlp": gain((L, D_MODEL)),
        "w_ff1": nrm((L, D_MODEL, D_FF), D_MODEL),
        "w_ff2": nrm((L, D_FF, D_MODEL), D_FF),
        "g_ple": gain((L, D_MODEL)),
        "w_ple_gate": nrm((L, D_MODEL, D_MODEL), D_MODEL),
        "w_ple": nrm((L, PLE_DIM, D_MODEL), PLE_DIM),
        "g_final": gain((D_MODEL,)),
    }


def _fwd_reference(x, p, positions, g_mix, w_in, q_norm, w_uq, kv_norm, w_ukv, w_pool, pool_scale,
              ssd_conv_w, ssd_conv_b, ssd_dt_bias, ssd_a_log, ssd_d, ssd_norm,
              lru_conv_w, lru_conv_b, lru_w_a, lru_b_a, lru_w_i, lru_b_i, lru_lambda,
              w_branch, w_out, g_mlp, w_ff1, w_ff2, g_ple, w_ple_gate, w_ple, g_final):
    b, s, _ = x.shape
    cos, sin = rope_tables(positions)
    pts = _split_points()
    for l in range(DEPTH):
        h = rmsnorm(x, g_mix[l])
        u = h @ w_in[l]
        c_q, c_kv, k_r, u_pool, z, xbc, dt, lru_g, lru_x, gates = jnp.split(u, pts, axis=-1)
        y_a = mla_mixer(c_q, c_kv, k_r, cos, sin, q_norm[l], w_uq[l], kv_norm[l], w_ukv[l])
        y_b = pool_mixer(u_pool, w_pool[l], pool_scale[l])
        y_c = ssd_mixer(z, xbc, dt, ssd_conv_w[l], ssd_conv_b[l], ssd_dt_bias[l], ssd_a_log[l],
                        ssd_d[l], ssd_norm[l])
        y_d = rglru_mixer(lru_g, lru_x, lru_conv_w[l], lru_conv_b[l], lru_w_a[l], lru_b_a[l],
                          lru_w_i[l], lru_b_i[l], lru_lambda[l])
        gates = jax.nn.sigmoid(gates.reshape(b, s, N_BRANCH, D_MODEL))
        merged = (gates[:, :, 0] * (y_a @ w_branch[l, 0])
                  + gates[:, :, 1] * (y_b @ w_branch[l, 1])
                  + gates[:, :, 2] * (y_c @ w_branch[l, 2])
                  + gates[:, :, 3] * (y_d @ w_branch[l, 3]))
        x = x + merged @ w_out[l]
        h2 = rmsnorm(x, g_mlp[l])
        x = x + jnp.square(jax.nn.relu(h2 @ w_ff1[l])) @ w_ff2[l]
        ple_gate = jax.nn.sigmoid(rmsnorm(x, g_ple[l]) @ w_ple_gate[l])
        x = x + (p[l] @ w_ple[l]) * ple_gate
    return rmsnorm(x, g_final)


import jax as _jax
import jax.numpy as _jnp

TWIN_FORMAT = 'train_step'
FWD_PARAMS = ['x', 'p', 'positions', 'g_mix', 'w_in', 'q_norm', 'w_uq', 'kv_norm', 'w_ukv', 'w_pool', 'pool_scale', 'ssd_conv_w', 'ssd_conv_b', 'ssd_dt_bias', 'ssd_a_log', 'ssd_d', 'ssd_norm', 'lru_conv_w', 'lru_conv_b', 'lru_w_a', 'lru_b_a', 'lru_w_i', 'lru_b_i', 'lru_lambda', 'w_branch', 'w_out', 'g_mlp', 'w_ff1', 'w_ff2', 'g_ple', 'w_ple_gate', 'w_ple', 'g_final']
TWIN_WEIGHTS = ['g_mix', 'w_in', 'q_norm', 'w_uq', 'kv_norm', 'w_ukv', 'w_pool', 'pool_scale', 'ssd_conv_w', 'ssd_conv_b', 'ssd_dt_bias', 'ssd_a_log', 'ssd_d', 'ssd_norm', 'lru_conv_w', 'lru_conv_b', 'lru_w_a', 'lru_b_a', 'lru_w_i', 'lru_b_i', 'lru_lambda', 'w_branch', 'w_out', 'g_mlp', 'w_ff1', 'w_ff2', 'g_ple', 'w_ple_gate', 'w_ple', 'g_final']
TWIN_DIFF_INPUT = 'x'
TWIN_INPUTS = ['x', 'p', 'positions', 'g_mix', 'w_in', 'q_norm', 'w_uq', 'kv_norm', 'w_ukv', 'w_pool', 'pool_scale', 'ssd_conv_w', 'ssd_conv_b', 'ssd_dt_bias', 'ssd_a_log', 'ssd_d', 'ssd_norm', 'lru_conv_w', 'lru_conv_b', 'lru_w_a', 'lru_b_a', 'lru_w_i', 'lru_b_i', 'lru_lambda', 'w_branch', 'w_out', 'g_mlp', 'w_ff1', 'w_ff2', 'g_ple', 'w_ple_gate', 'w_ple', 'g_final', 'loss_target', 'm_g_mix', 'm_w_in', 'm_q_norm', 'm_w_uq', 'm_kv_norm', 'm_w_ukv', 'm_w_pool', 'm_pool_scale', 'm_ssd_conv_w', 'm_ssd_conv_b', 'm_ssd_dt_bias', 'm_ssd_a_log', 'm_ssd_d', 'm_ssd_norm', 'm_lru_conv_w', 'm_lru_conv_b', 'm_lru_w_a', 'm_lru_b_a', 'm_lru_w_i', 'm_lru_b_i', 'm_lru_lambda', 'm_w_branch', 'm_w_out', 'm_g_mlp', 'm_w_ff1', 'm_w_ff2', 'm_g_ple', 'm_w_ple_gate', 'm_w_ple', 'm_g_final', 'v_g_mix', 'v_w_in', 'v_q_norm', 'v_w_uq', 'v_kv_norm', 'v_w_ukv', 'v_w_pool', 'v_pool_scale', 'v_ssd_conv_w', 'v_ssd_conv_b', 'v_ssd_dt_bias', 'v_ssd_a_log', 'v_ssd_d', 'v_ssd_norm', 'v_lru_conv_w', 'v_lru_conv_b', 'v_lru_w_a', 'v_lru_b_a', 'v_lru_w_i', 'v_lru_b_i', 'v_lru_lambda', 'v_w_branch', 'v_w_out', 'v_g_mlp', 'v_w_ff1', 'v_w_ff2', 'v_g_ple', 'v_w_ple_gate', 'v_w_ple', 'v_g_final']
TWIN_OUTPUTS = ['loss', 'grad_x', 'grad_g_mix', 'grad_w_in', 'grad_q_norm', 'grad_w_uq', 'grad_kv_norm', 'grad_w_ukv', 'grad_w_pool', 'grad_pool_scale', 'grad_ssd_conv_w', 'grad_ssd_conv_b', 'grad_ssd_dt_bias', 'grad_ssd_a_log', 'grad_ssd_d', 'grad_ssd_norm', 'grad_lru_conv_w', 'grad_lru_conv_b', 'grad_lru_w_a', 'grad_lru_b_a', 'grad_lru_w_i', 'grad_lru_b_i', 'grad_lru_lambda', 'grad_w_branch', 'grad_w_out', 'grad_g_mlp', 'grad_w_ff1', 'grad_w_ff2', 'grad_g_ple', 'grad_w_ple_gate', 'grad_w_ple', 'grad_g_final', 'delta_g_mix', 'delta_w_in', 'delta_q_norm', 'delta_w_uq', 'delta_kv_norm', 'delta_w_ukv', 'delta_w_pool', 'delta_pool_scale', 'delta_ssd_conv_w', 'delta_ssd_conv_b', 'delta_ssd_dt_bias', 'delta_ssd_a_log', 'delta_ssd_d', 'delta_ssd_norm', 'delta_lru_conv_w', 'delta_lru_conv_b', 'delta_lru_w_a', 'delta_lru_b_a', 'delta_lru_w_i', 'delta_lru_b_i', 'delta_lru_lambda', 'delta_w_branch', 'delta_w_out', 'delta_g_mlp', 'delta_w_ff1', 'delta_w_ff2', 'delta_g_ple', 'delta_w_ple_gate', 'delta_w_ple', 'delta_g_final', 'new_m_g_mix', 'new_m_w_in', 'new_m_q_norm', 'new_m_w_uq', 'new_m_kv_norm', 'new_m_w_ukv', 'new_m_w_pool', 'new_m_pool_scale', 'new_m_ssd_conv_w', 'new_m_ssd_conv_b', 'new_m_ssd_dt_bias', 'new_m_ssd_a_log', 'new_m_ssd_d', 'new_m_ssd_norm', 'new_m_lru_conv_w', 'new_m_lru_conv_b', 'new_m_lru_w_a', 'new_m_lru_b_a', 'new_m_lru_w_i', 'new_m_lru_b_i', 'new_m_lru_lambda', 'new_m_w_branch', 'new_m_w_out', 'new_m_g_mlp', 'new_m_w_ff1', 'new_m_w_ff2', 'new_m_g_ple', 'new_m_w_ple_gate', 'new_m_w_ple', 'new_m_g_final', 'new_v_g_mix', 'new_v_w_in', 'new_v_q_norm', 'new_v_w_uq', 'new_v_kv_norm', 'new_v_w_ukv', 'new_v_w_pool', 'new_v_pool_scale', 'new_v_ssd_conv_w', 'new_v_ssd_conv_b', 'new_v_ssd_dt_bias', 'new_v_ssd_a_log', 'new_v_ssd_d', 'new_v_ssd_norm', 'new_v_lru_conv_w', 'new_v_lru_conv_b', 'new_v_lru_w_a', 'new_v_lru_b_a', 'new_v_lru_w_i', 'new_v_lru_b_i', 'new_v_lru_lambda', 'new_v_w_branch', 'new_v_w_out', 'new_v_g_mlp', 'new_v_w_ff1', 'new_v_w_ff2', 'new_v_g_ple', 'new_v_w_ple_gate', 'new_v_w_ple', 'new_v_g_final']
TWIN_LEAF_KINDS = {'loss': 'loss', 'grad_x': 'grad_x', 'grad_g_mix': 'grad_w', 'grad_w_in': 'grad_w', 'grad_q_norm': 'grad_w', 'grad_w_uq': 'grad_w', 'grad_kv_norm': 'grad_w', 'grad_w_ukv': 'grad_w', 'grad_w_pool': 'grad_w', 'grad_pool_scale': 'grad_w', 'grad_ssd_conv_w': 'grad_w', 'grad_ssd_conv_b': 'grad_w', 'grad_ssd_dt_bias': 'grad_w', 'grad_ssd_a_log': 'grad_w', 'grad_ssd_d': 'grad_w', 'grad_ssd_norm': 'grad_w', 'grad_lru_conv_w': 'grad_w', 'grad_lru_conv_b': 'grad_w', 'grad_lru_w_a': 'grad_w', 'grad_lru_b_a': 'grad_w', 'grad_lru_w_i': 'grad_w', 'grad_lru_b_i': 'grad_w', 'grad_lru_lambda': 'grad_w', 'grad_w_branch': 'grad_w', 'grad_w_out': 'grad_w', 'grad_g_mlp': 'grad_w', 'grad_w_ff1': 'grad_w', 'grad_w_ff2': 'grad_w', 'grad_g_ple': 'grad_w', 'grad_w_ple_gate': 'grad_w', 'grad_w_ple': 'grad_w', 'grad_g_final': 'grad_w', 'delta_g_mix': 'delta_w', 'delta_w_in': 'delta_w', 'delta_q_norm': 'delta_w', 'delta_w_uq': 'delta_w', 'delta_kv_norm': 'delta_w', 'delta_w_ukv': 'delta_w', 'delta_w_pool': 'delta_w', 'delta_pool_scale': 'delta_w', 'delta_ssd_conv_w': 'delta_w', 'delta_ssd_conv_b': 'delta_w', 'delta_ssd_dt_bias': 'delta_w', 'delta_ssd_a_log': 'delta_w', 'delta_ssd_d': 'delta_w', 'delta_ssd_norm': 'delta_w', 'delta_lru_conv_w': 'delta_w', 'delta_lru_conv_b': 'delta_w', 'delta_lru_w_a': 'delta_w', 'delta_lru_b_a': 'delta_w', 'delta_lru_w_i': 'delta_w', 'delta_lru_b_i': 'delta_w', 'delta_lru_lambda': 'delta_w', 'delta_w_branch': 'delta_w', 'delta_w_out': 'delta_w', 'delta_g_mlp': 'delta_w', 'delta_w_ff1': 'delta_w', 'delta_w_ff2': 'delta_w', 'delta_g_ple': 'delta_w', 'delta_w_ple_gate': 'delta_w', 'delta_w_ple': 'delta_w', 'delta_g_final': 'delta_w', 'new_m_g_mix': 'new_m', 'new_m_w_in': 'new_m', 'new_m_q_norm': 'new_m', 'new_m_w_uq': 'new_m', 'new_m_kv_norm': 'new_m', 'new_m_w_ukv': 'new_m', 'new_m_w_pool': 'new_m', 'new_m_pool_scale': 'new_m', 'new_m_ssd_conv_w': 'new_m', 'new_m_ssd_conv_b': 'new_m', 'new_m_ssd_dt_bias': 'new_m', 'new_m_ssd_a_log': 'new_m', 'new_m_ssd_d': 'new_m', 'new_m_ssd_norm': 'new_m', 'new_m_lru_conv_w': 'new_m', 'new_m_lru_conv_b': 'new_m', 'new_m_lru_w_a': 'new_m', 'new_m_lru_b_a': 'new_m', 'new_m_lru_w_i': 'new_m', 'new_m_lru_b_i': 'new_m', 'new_m_lru_lambda': 'new_m', 'new_m_w_branch': 'new_m', 'new_m_w_out': 'new_m', 'new_m_g_mlp': 'new_m', 'new_m_w_ff1': 'new_m', 'new_m_w_ff2': 'new_m', 'new_m_g_ple': 'new_m', 'new_m_w_ple_gate': 'new_m', 'new_m_w_ple': 'new_m', 'new_m_g_final': 'new_m', 'new_v_g_mix': 'new_v', 'new_v_w_in': 'new_v', 'new_v_q_norm': 'new_v', 'new_v_w_uq': 'new_v', 'new_v_kv_norm': 'new_v', 'new_v_w_ukv': 'new_v', 'new_v_w_pool': 'new_v', 'new_v_pool_scale': 'new_v', 'new_v_ssd_conv_w': 'new_v', 'new_v_ssd_conv_b': 'new_v', 'new_v_ssd_dt_bias': 'new_v', 'new_v_ssd_a_log': 'new_v', 'new_v_ssd_d': 'new_v', 'new_v_ssd_norm': 'new_v', 'new_v_lru_conv_w': 'new_v', 'new_v_lru_conv_b': 'new_v', 'new_v_lru_w_a': 'new_v', 'new_v_lru_b_a': 'new_v', 'new_v_lru_w_i': 'new_v', 'new_v_lru_b_i': 'new_v', 'new_v_lru_lambda': 'new_v', 'new_v_w_branch': 'new_v', 'new_v_w_out': 'new_v', 'new_v_g_mlp': 'new_v', 'new_v_w_ff1': 'new_v', 'new_v_w_ff2': 'new_v', 'new_v_g_ple': 'new_v', 'new_v_w_ple_gate': 'new_v', 'new_v_w_ple': 'new_v', 'new_v_g_final': 'new_v'}


def _forward(args):
    return _fwd_reference(*[args[k] for k in FWD_PARAMS])


def _output_shape():
    out = _jax.eval_shape(lambda: _forward(_fwd_setup_inputs(0)))
    return out.shape, out.dtype

N_MICROBATCH = 1
ADAM_LR = 0.001
ADAM_B1 = 0.9
ADAM_B2 = 0.999
ADAM_EPS = 1e-08
ADAM_WD = 0.01
ADAM_STEP = 10
PER_EXAMPLE_BATCH_AXIS = {'x': 0, 'p': 1, 'positions': 0, 'loss_target': 0}
SHARED_INPUTS = []
_WEIGHT_DTYPES = {'g_mix': _jnp.float32, 'w_in': _jnp.float32, 'q_norm': _jnp.float32, 'w_uq': _jnp.float32, 'kv_norm': _jnp.float32, 'w_ukv': _jnp.float32, 'w_pool': _jnp.float32, 'pool_scale': _jnp.float32, 'ssd_conv_w': _jnp.float32, 'ssd_conv_b': _jnp.float32, 'ssd_dt_bias': _jnp.float32, 'ssd_a_log': _jnp.float32, 'ssd_d': _jnp.float32, 'ssd_norm': _jnp.float32, 'lru_conv_w': _jnp.float32, 'lru_conv_b': _jnp.float32, 'lru_w_a': _jnp.float32, 'lru_b_a': _jnp.float32, 'lru_w_i': _jnp.float32, 'lru_b_i': _jnp.float32, 'lru_lambda': _jnp.float32, 'w_branch': _jnp.float32, 'w_out': _jnp.float32, 'g_mlp': _jnp.float32, 'w_ff1': _jnp.float32, 'w_ff2': _jnp.float32, 'g_ple': _jnp.float32, 'w_ple_gate': _jnp.float32, 'w_ple': _jnp.float32, 'g_final': _jnp.float32}
MOMENT_SCALE = {'g_mix': 1.543551e-01, 'w_in': 6.017087e-02, 'q_norm': 2.144066e-02, 'w_uq': 1.481839e-02, 'kv_norm': 5.071526e-02, 'w_ukv': 2.629432e-02, 'w_pool': 8.780861e-02, 'pool_scale': 8.678261e-02, 'ssd_conv_w': 9.435441e-02, 'ssd_conv_b': 1.477405e-01, 'ssd_dt_bias': 1.693697e-01, 'ssd_a_log': 4.299212e-01, 'ssd_d': 6.306496e-01, 'ssd_norm': 1.083345e-01, 'lru_conv_w': 1.359060e-01, 'lru_conv_b': 7.550284e-01, 'lru_w_a': 2.768488e-02, 'lru_b_a': 3.272189e-02, 'lru_w_i': 5.422859e-02, 'lru_b_i': 5.927585e-02, 'lru_lambda': 7.524072e-02, 'w_branch': 6.335687e-02, 'w_out': 1.259063e-01, 'g_mlp': 1.430241e-01, 'w_ff1': 6.891385e-02, 'w_ff2': 1.965091e-01, 'g_ple': 2.153216e-02, 'w_ple_gate': 2.138495e-02, 'w_ple': 4.939558e-02, 'g_final': 3.235808e+01}


def _to_microbatches(a, axis):
    t = _jnp.moveaxis(a, axis, 0)
    t = t.reshape((N_MICROBATCH, t.shape[0] // N_MICROBATCH) + t.shape[1:])
    return _jnp.moveaxis(t, 1, axis + 1)


def setup_inputs(seed: int = 0) -> dict:
    inp = _fwd_setup_inputs(seed)
    key = _jax.random.fold_in(_jax.random.key(seed), 7919)
    shape, _ = _output_shape()
    out = dict(inp)
    out["loss_target"] = _jax.random.normal(_jax.random.fold_in(key, 0), shape, _jnp.float32)
    for i, name in enumerate(TWIN_WEIGHTS):
        w = inp[name].astype(_jnp.float32)
        if MOMENT_SCALE is None:
            s = _jnp.sqrt(_jnp.mean(_jnp.square(w)) + 1e-30)
        else:
            s = MOMENT_SCALE[name]
        km, kv = _jax.random.split(_jax.random.fold_in(key, i + 1))
        out[name] = w
        out["m_" + name] = s * _jax.random.normal(km, w.shape, _jnp.float32)
        out["v_" + name] = (s * s) * _jax.random.uniform(kv, w.shape, _jnp.float32, 0.5, 1.5)
    if N_MICROBATCH > 1:
        for name, axis in PER_EXAMPLE_BATCH_AXIS.items():
            out[name] = _to_microbatches(out[name], axis)
    return {'x': out['x'], 'p': out['p'], 'positions': out['positions'], 'g_mix': out['g_mix'], 'w_in': out['w_in'], 'q_norm': out['q_norm'], 'w_uq': out['w_uq'], 'kv_norm': out['kv_norm'], 'w_ukv': out['w_ukv'], 'w_pool': out['w_pool'], 'pool_scale': out['pool_scale'], 'ssd_conv_w': out['ssd_conv_w'], 'ssd_conv_b': out['ssd_conv_b'], 'ssd_dt_bias': out['ssd_dt_bias'], 'ssd_a_log': out['ssd_a_log'], 'ssd_d': out['ssd_d'], 'ssd_norm': out['ssd_norm'], 'lru_conv_w': out['lru_conv_w'], 'lru_conv_b': out['lru_conv_b'], 'lru_w_a': out['lru_w_a'], 'lru_b_a': out['lru_b_a'], 'lru_w_i': out['lru_w_i'], 'lru_b_i': out['lru_b_i'], 'lru_lambda': out['lru_lambda'], 'w_branch': out['w_branch'], 'w_out': out['w_out'], 'g_mlp': out['g_mlp'], 'w_ff1': out['w_ff1'], 'w_ff2': out['w_ff2'], 'g_ple': out['g_ple'], 'w_ple_gate': out['w_ple_gate'], 'w_ple': out['w_ple'], 'g_final': out['g_final'], 'loss_target': out['loss_target'], 'm_g_mix': out['m_g_mix'], 'm_w_in': out['m_w_in'], 'm_q_norm': out['m_q_norm'], 'm_w_uq': out['m_w_uq'], 'm_kv_norm': out['m_kv_norm'], 'm_w_ukv': out['m_w_ukv'], 'm_w_pool': out['m_w_pool'], 'm_pool_scale': out['m_pool_scale'], 'm_ssd_conv_w': out['m_ssd_conv_w'], 'm_ssd_conv_b': out['m_ssd_conv_b'], 'm_ssd_dt_bias': out['m_ssd_dt_bias'], 'm_ssd_a_log': out['m_ssd_a_log'], 'm_ssd_d': out['m_ssd_d'], 'm_ssd_norm': out['m_ssd_norm'], 'm_lru_conv_w': out['m_lru_conv_w'], 'm_lru_conv_b': out['m_lru_conv_b'], 'm_lru_w_a': out['m_lru_w_a'], 'm_lru_b_a': out['m_lru_b_a'], 'm_lru_w_i': out['m_lru_w_i'], 'm_lru_b_i': out['m_lru_b_i'], 'm_lru_lambda': out['m_lru_lambda'], 'm_w_branch': out['m_w_branch'], 'm_w_out': out['m_w_out'], 'm_g_mlp': out['m_g_mlp'], 'm_w_ff1': out['m_w_ff1'], 'm_w_ff2': out['m_w_ff2'], 'm_g_ple': out['m_g_ple'], 'm_w_ple_gate': out['m_w_ple_gate'], 'm_w_ple': out['m_w_ple'], 'm_g_final': out['m_g_final'], 'v_g_mix': out['v_g_mix'], 'v_w_in': out['v_w_in'], 'v_q_norm': out['v_q_norm'], 'v_w_uq': out['v_w_uq'], 'v_kv_norm': out['v_kv_norm'], 'v_w_ukv': out['v_w_ukv'], 'v_w_pool': out['v_w_pool'], 'v_pool_scale': out['v_pool_scale'], 'v_ssd_conv_w': out['v_ssd_conv_w'], 'v_ssd_conv_b': out['v_ssd_conv_b'], 'v_ssd_dt_bias': out['v_ssd_dt_bias'], 'v_ssd_a_log': out['v_ssd_a_log'], 'v_ssd_d': out['v_ssd_d'], 'v_ssd_norm': out['v_ssd_norm'], 'v_lru_conv_w': out['v_lru_conv_w'], 'v_lru_conv_b': out['v_lru_conv_b'], 'v_lru_w_a': out['v_lru_w_a'], 'v_lru_b_a': out['v_lru_b_a'], 'v_lru_w_i': out['v_lru_w_i'], 'v_lru_b_i': out['v_lru_b_i'], 'v_lru_lambda': out['v_lru_lambda'], 'v_w_branch': out['v_w_branch'], 'v_w_out': out['v_w_out'], 'v_g_mlp': out['v_g_mlp'], 'v_w_ff1': out['v_w_ff1'], 'v_w_ff2': out['v_w_ff2'], 'v_g_ple': out['v_g_ple'], 'v_w_ple_gate': out['v_w_ple_gate'], 'v_w_ple': out['v_w_ple'], 'v_g_final': out['v_g_final']}


def _loss(weights, diff, rest, loss_target):
    with _jax.named_scope("forward"):
        args = {**rest, TWIN_DIFF_INPUT: diff, **{k: w.astype(_WEIGHT_DTYPES[k]) for k, w in weights.items()}}
        y = _forward(args)
    with _jax.named_scope("loss_head"):
        err = _jnp.square(y.astype(_jnp.float32) - loss_target)
        return 0.5 * _jnp.sum(_jnp.mean(err, axis=-1)) if err.ndim else 0.5 * err


def _adamw(w, g, m, v):
    m = ADAM_B1 * m + (1.0 - ADAM_B1) * g
    v = ADAM_B2 * v + (1.0 - ADAM_B2) * _jnp.square(g)
    m_hat = m / (1.0 - ADAM_B1 ** ADAM_STEP)
    v_hat = v / (1.0 - ADAM_B2 ** ADAM_STEP)
    delta = -ADAM_LR * (m_hat / (_jnp.sqrt(v_hat) + ADAM_EPS) + ADAM_WD * w)
    return delta, m, v


def reference(x, p, positions, g_mix, w_in, q_norm, w_uq, kv_norm, w_ukv, w_pool, pool_scale, ssd_conv_w, ssd_conv_b, ssd_dt_bias, ssd_a_log, ssd_d, ssd_norm, lru_conv_w, lru_conv_b, lru_w_a, lru_b_a, lru_w_i, lru_b_i, lru_lambda, w_branch, w_out, g_mlp, w_ff1, w_ff2, g_ple, w_ple_gate, w_ple, g_final, loss_target, m_g_mix, m_w_in, m_q_norm, m_w_uq, m_kv_norm, m_w_ukv, m_w_pool, m_pool_scale, m_ssd_conv_w, m_ssd_conv_b, m_ssd_dt_bias, m_ssd_a_log, m_ssd_d, m_ssd_norm, m_lru_conv_w, m_lru_conv_b, m_lru_w_a, m_lru_b_a, m_lru_w_i, m_lru_b_i, m_lru_lambda, m_w_branch, m_w_out, m_g_mlp, m_w_ff1, m_w_ff2, m_g_ple, m_w_ple_gate, m_w_ple, m_g_final, v_g_mix, v_w_in, v_q_norm, v_w_uq, v_kv_norm, v_w_ukv, v_w_pool, v_pool_scale, v_ssd_conv_w, v_ssd_conv_b, v_ssd_dt_bias, v_ssd_a_log, v_ssd_d, v_ssd_norm, v_lru_conv_w, v_lru_conv_b, v_lru_w_a, v_lru_b_a, v_lru_w_i, v_lru_b_i, v_lru_lambda, v_w_branch, v_w_out, v_g_mlp, v_w_ff1, v_w_ff2, v_g_ple, v_w_ple_gate, v_w_ple, v_g_final):
    given = dict(x=x, p=p, positions=positions, g_mix=g_mix, w_in=w_in, q_norm=q_norm, w_uq=w_uq, kv_norm=kv_norm, w_ukv=w_ukv, w_pool=w_pool, pool_scale=pool_scale, ssd_conv_w=ssd_conv_w, ssd_conv_b=ssd_conv_b, ssd_dt_bias=ssd_dt_bias, ssd_a_log=ssd_a_log, ssd_d=ssd_d, ssd_norm=ssd_norm, lru_conv_w=lru_conv_w, lru_conv_b=lru_conv_b, lru_w_a=lru_w_a, lru_b_a=lru_b_a, lru_w_i=lru_w_i, lru_b_i=lru_b_i, lru_lambda=lru_lambda, w_branch=w_branch, w_out=w_out, g_mlp=g_mlp, w_ff1=w_ff1, w_ff2=w_ff2, g_ple=g_ple, w_ple_gate=w_ple_gate, w_ple=w_ple, g_final=g_final, loss_target=loss_target, m_g_mix=m_g_mix, m_w_in=m_w_in, m_q_norm=m_q_norm, m_w_uq=m_w_uq, m_kv_norm=m_kv_norm, m_w_ukv=m_w_ukv, m_w_pool=m_w_pool, m_pool_scale=m_pool_scale, m_ssd_conv_w=m_ssd_conv_w, m_ssd_conv_b=m_ssd_conv_b, m_ssd_dt_bias=m_ssd_dt_bias, m_ssd_a_log=m_ssd_a_log, m_ssd_d=m_ssd_d, m_ssd_norm=m_ssd_norm, m_lru_conv_w=m_lru_conv_w, m_lru_conv_b=m_lru_conv_b, m_lru_w_a=m_lru_w_a, m_lru_b_a=m_lru_b_a, m_lru_w_i=m_lru_w_i, m_lru_b_i=m_lru_b_i, m_lru_lambda=m_lru_lambda, m_w_branch=m_w_branch, m_w_out=m_w_out, m_g_mlp=m_g_mlp, m_w_ff1=m_w_ff1, m_w_ff2=m_w_ff2, m_g_ple=m_g_ple, m_w_ple_gate=m_w_ple_gate, m_w_ple=m_w_ple, m_g_final=m_g_final, v_g_mix=v_g_mix, v_w_in=v_w_in, v_q_norm=v_q_norm, v_w_uq=v_w_uq, v_kv_norm=v_kv_norm, v_w_ukv=v_w_ukv, v_w_pool=v_w_pool, v_pool_scale=v_pool_scale, v_ssd_conv_w=v_ssd_conv_w, v_ssd_conv_b=v_ssd_conv_b, v_ssd_dt_bias=v_ssd_dt_bias, v_ssd_a_log=v_ssd_a_log, v_ssd_d=v_ssd_d, v_ssd_norm=v_ssd_norm, v_lru_conv_w=v_lru_conv_w, v_lru_conv_b=v_lru_conv_b, v_lru_w_a=v_lru_w_a, v_lru_b_a=v_lru_b_a, v_lru_w_i=v_lru_w_i, v_lru_b_i=v_lru_b_i, v_lru_lambda=v_lru_lambda, v_w_branch=v_w_branch, v_w_out=v_w_out, v_g_mlp=v_g_mlp, v_w_ff1=v_w_ff1, v_w_ff2=v_w_ff2, v_g_ple=v_g_ple, v_w_ple_gate=v_w_ple_gate, v_w_ple=v_w_ple, v_g_final=v_g_final)
    weights = {n: given[n] for n in TWIN_WEIGHTS}
    shared = {n: given[n] for n in SHARED_INPUTS}
    per_example = {n: given[n] for n in ['x', 'p', 'positions']}
    grad_fn = _jax.value_and_grad(_loss, argnums=(0, 1))

    def one_microbatch(ex, loss_target):
        ex = dict(ex)
        diff = ex.pop(TWIN_DIFF_INPUT)
        return grad_fn(weights, diff, {**shared, **ex}, loss_target)

    if N_MICROBATCH == 1:
        loss, (grad_w, grad_x) = one_microbatch(per_example, given["loss_target"])
    else:
        def body(carry, xs):
            loss_sum, grad_sum = carry
            l_k, (gw_k, gx_k) = one_microbatch(xs[0], xs[1])
            with _jax.named_scope("update"):
                return (loss_sum + l_k, _jax.tree.map(_jnp.add, grad_sum, gw_k)), gx_k

        init = (_jnp.zeros((), _jnp.float32), _jax.tree.map(_jnp.zeros_like, weights))
        (loss, grad_w), grad_x = _jax.lax.scan(body, init, (per_example, given["loss_target"]))
    with _jax.named_scope("update"):
        delta_w, new_m, new_v = {}, {}, {}
        for n in TWIN_WEIGHTS:
            delta_w[n], new_m[n], new_v[n] = _adamw(weights[n], grad_w[n], given["m_" + n], given["v_" + n])
    return (loss, grad_x, *[grad_w[n] for n in TWIN_WEIGHTS], *[delta_w[n] for n in TWIN_WEIGHTS],
            *[new_m[n] for n in TWIN_WEIGHTS], *[new_v[n] for n in TWIN_WEIGHTS])
```

```python
import functools

import jax
import jax.numpy as jnp
from jax import lax
from jax.experimental import pallas as pl
from jax.experimental.pallas import tpu as pltpu

f32 = jnp.float32
bf16 = jnp.bfloat16

D_MODEL = 1024
MIX = 512
N_HEADS = 8
QK_NOPE, QK_ROPE, V_HEAD = 64, 32, 64
Q_LORA, KV_LORA = 384, 256
ROPE_THETA = 10000.0
POOL_WINDOWS = (2, 4, 8, 16)
SSD_CHUNK = 128
SSD_XBC = 768
CONV_W = 4
LRU_C = 8.0
D_FF = 4096
EPS = 1e-6
IN_COLS = 7592
ADAM_LR, ADAM_B1, ADAM_B2, ADAM_EPS, ADAM_WD, ADAM_STEP = 0.001, 0.9, 0.999, 1e-08, 0.01, 10

LANE = 128
HALO = 8
POOL_HALO = 16
VMEM_LIMIT = 56 * 1024 * 1024
N_DEV = 8
AXES = ("x", "y", "c")

U_COLS = 8192
U_GATES, U_POOL, U_Z, U_LG, U_LX, U_XBC, U_CQ, U_KR, U_CKV, U_DT = (
    (0, 4096), (4096, 512), (4608, 512), (5120, 512), (5632, 512), (6144, 768),
    (6912, 384), (7296, 128), (7424, 256), (7680, 128))
KR_LANE = 64


def _cp(sem):
    return pltpu.CompilerParams(dimension_semantics=sem, vmem_limit_bytes=VMEM_LIMIT)


def _pick(dim, pref):
    if dim <= pref:
        return dim
    t = pref
    while t >= LANE:
        if dim % t == 0:
            return t
        t -= LANE
    t = pref
    while dim % t:
        t -= 8
    return t


@functools.partial(jax.custom_vjp, nondiff_argnums=(1,))
def shift_down(x, k):
    row = lax.broadcasted_iota(jnp.int32, x.shape, 0)
    return jnp.where(row >= k, pltpu.roll(x, k, 0), 0.0)


def _shift_down_fwd(x, k):
    return shift_down(x, k), None


def _shift_down_bwd(k, _, g):
    r = g.shape[0]
    row = lax.broadcasted_iota(jnp.int32, g.shape, 0)
    return (jnp.where(row < r - k, pltpu.roll(g, r - k, 0), 0.0),)


shift_down.defvjp(_shift_down_fwd, _shift_down_bwd)


@functools.partial(jax.custom_vjp, nondiff_argnums=(1,))
def lane_roll(x, s):
    return pltpu.roll(x, s, 1)


def _lane_roll_fwd(x, s):
    return lane_roll(x, s), None


def _lane_roll_bwd(s, _, g):
    return (pltpu.roll(g, (g.shape[1] - s) % g.shape[1], 1),)


lane_roll.defvjp(_lane_roll_fwd, _lane_roll_bwd)


def _tile_spec(tm, width, cb, n=None):
    if n is None:
        return pl.BlockSpec((tm, width), lambda i: (i, cb))
    return pl.BlockSpec((tm, width), lambda i: (n - 1 - i, cb))


def _const_spec(shape):
    nd = len(shape)
    return pl.BlockSpec(shape, lambda i: (0,) * nd)


def seq_fwd(name, f, params, tiles, carries, outs, tm):
    rows = tiles[0][0].shape[0]
    n = rows // tm
    np_, nt, no, nc = len(params), len(tiles), len(outs), len(carries)

    def body(*refs):
        p_refs = refs[:np_]
        t_refs = refs[np_:np_ + nt]
        o_refs = refs[np_ + nt:np_ + nt + no]
        s_refs = refs[np_ + nt + no:np_ + nt + no + nc]
        c_refs = refs[np_ + nt + no + nc:]
        i = pl.program_id(0)

        @pl.when(i == 0)
        def _():
            for c in c_refs:
                c[...] = jnp.zeros_like(c)

        cvals = [c[...] for c in c_refs]
        for s, c in zip(s_refs, cvals):
            s[0] = c
        o, newc = f(i, [r[...] for r in p_refs], cvals, [r[...].astype(f32) for r in t_refs])
        for r, v in zip(o_refs, o):
            r[...] = v.astype(r.dtype)
        for r, v in zip(c_refs, newc):
            r[...] = v

    in_specs = [_const_spec(p.shape) for p in params] + [_tile_spec(tm, w, cb) for (_, w, cb) in tiles]
    out_specs = [_tile_spec(tm, w, 0) for (w, _) in outs]
    out_specs += [pl.BlockSpec((1,) + tuple(c), lambda i, nd=len(c): (i,) + (0,) * nd) for c in carries]
    out_shape = [jax.ShapeDtypeStruct((rows, w), dt) for (w, dt) in outs]
    out_shape += [jax.ShapeDtypeStruct((n,) + tuple(c), f32) for c in carries]
    res = pl.pallas_call(
        body, name=name, grid=(n,), in_specs=in_specs, out_specs=out_specs, out_shape=out_shape,
        scratch_shapes=[pltpu.VMEM(tuple(c), f32) for c in carries],
        compiler_params=_cp(("arbitrary",)),
    )(*params, *[t[0] for t in tiles])
    return list(res[:no]), list(res[no:])


def seq_bwd(name, f, params, tiles, diff, saved, douts, gdtypes, tm, add_to=None):
    rows = tiles[0][0].shape[0]
    n = rows // tm
    np_, nt, nc, nd = len(params), len(tiles), len(saved), len(douts)
    didx = [k for k, d in enumerate(diff) if d]
    ng = len(didx)
    has_add = add_to is not None

    def body(*refs):
        p_refs = refs[:np_]
        t_refs = refs[np_:np_ + nt]
        s_refs = refs[np_ + nt:np_ + nt + nc]
        d_refs = refs[np_ + nt + nc:np_ + nt + nc + nd]
        pos = np_ + nt + nc + nd
        a_ref = refs[pos] if has_add else None
        pos += 1 if has_add else 0
        dp_refs = refs[pos:pos + np_]
        dt_refs = refs[pos + np_:pos + np_ + ng]
        dc_refs = refs[pos + np_ + ng:]
        i = pl.program_id(0)
        step = n - 1 - i

        @pl.when(i == 0)
        def _():
            for r in dp_refs:
                r[...] = jnp.zeros_like(r)
            for r in dc_refs:
                r[...] = jnp.zeros_like(r)

        pvals = [r[...] for r in p_refs]
        cvals = [r[0] for r in s_refs]
        xvals = [r[...].astype(f32) for r in t_refs]

        def fn(p, c, xd):
            x = list(xvals)
            for k, v in zip(didx, xd):
                x[k] = v
            return f(step, p, c, x)

        _, vjp = jax.vjp(fn, pvals, cvals, [xvals[k] for k in didx])
        dp, dc, dx = vjp(([r[...].astype(f32) for r in d_refs], [r[...] for r in dc_refs]))
        for r, v in zip(dp_refs, dp):
            r[...] += v
        for r, v in zip(dc_refs, dc):
            r[...] = v
        for k, (r, v) in enumerate(zip(dt_refs, dx)):
            if has_add and k == add_to[0]:
                v = v + a_ref[...].astype(f32)
            r[...] = v.astype(r.dtype)

    in_specs = [_const_spec(p.shape) for p in params] + [_tile_spec(tm, w, cb, n) for (_, w, cb) in tiles]
    in_specs += [pl.BlockSpec((1,) + tuple(s.shape[1:]), lambda i, nd_=s.ndim - 1: (n - 1 - i,) + (0,) * nd_) for s in saved]
    in_specs += [_tile_spec(tm, d.shape[1], 0, n) for d in douts]
    args = list(params) + [t[0] for t in tiles] + list(saved) + list(douts)
    if has_add:
        in_specs.append(_tile_spec(tm, add_to[1].shape[1], 0, n))
        args.append(add_to[1])
    out_specs = [_const_spec(p.shape) for p in params] + [_tile_spec(tm, tiles[k][1], 0, n) for k in didx]
    out_shape = [jax.ShapeDtypeStruct(p.shape, f32) for p in params]
    out_shape += [jax.ShapeDtypeStruct((rows, tiles[k][1]), dt) for k, dt in zip(didx, gdtypes)]
    res = pl.pallas_call(
        body, name=name, grid=(n,), in_specs=in_specs, out_specs=out_specs, out_shape=out_shape,
        scratch_shapes=[pltpu.VMEM(tuple(s.shape[1:]), f32) for s in saved],
        compiler_params=_cp(("arbitrary",)),
    )(*args)
    return list(res[:np_]), list(res[np_:])


def matmul(name, a, b, *, ta=False, tb=False, outs=(f32,), epi=None, extras=(), tm=512, tn=1024, tk=1024):
    m, k = (a.shape[1], a.shape[0]) if ta else a.shape
    n = b.shape[0] if tb else b.shape[1]
    tm, tn, tk = _pick(m, tm), _pick(n, tn), _pick(k, tk)
    nk = k // tk
    ne = len(extras)
    dims = (((0 if ta else 1,), (1 if tb else 0,)), ((), ()))

    def body(*refs):
        a_ref, b_ref = refs[0], refs[1]
        e_refs = refs[2:2 + ne]
        o_refs = refs[2 + ne:-1]
        acc = refs[-1]
        kk = pl.program_id(2)

        @pl.when(kk == 0)
        def _():
            acc[...] = jnp.zeros_like(acc)

        acc[...] += lax.dot_general(a_ref[...].astype(bf16), b_ref[...].astype(bf16), dims, preferred_element_type=f32)

        @pl.when(kk == nk - 1)
        def _():
            res = (acc[...],) if epi is None else epi(acc[...], *[e[...] for e in e_refs])
            for r, v in zip(o_refs, res):
                r[...] = v.astype(r.dtype)

    a_spec = pl.BlockSpec((tk, tm), lambda i, j, q: (q, i)) if ta else pl.BlockSpec((tm, tk), lambda i, j, q: (i, q))
    b_spec = pl.BlockSpec((tn, tk), lambda i, j, q: (j, q)) if tb else pl.BlockSpec((tk, tn), lambda i, j, q: (q, j))
    assert all(off % tn == 0 for (_, off) in extras)
    e_specs = [pl.BlockSpec((tm, tn), lambda i, j, q, off=off // tn: (i, off + j)) for (_, off) in extras]
    res = pl.pallas_call(
        body, name=name, grid=(m // tm, n // tn, nk), in_specs=[a_spec, b_spec] + e_specs,
        out_specs=[pl.BlockSpec((tm, tn), lambda i, j, q: (i, j)) for _ in outs],
        out_shape=[jax.ShapeDtypeStruct((m, n), dt) for dt in outs],
        scratch_shapes=[pltpu.VMEM((tm, tn), f32)],
        compiler_params=_cp(("parallel", "parallel", "arbitrary")),
    )(a, b, *[e[0] for e in extras])
    return res[0] if len(outs) == 1 else tuple(res)


ATT_SCALE = (QK_NOPE + QK_ROPE) ** -0.5
NT = (((1,), (1,)), ((), ()))
TN = (((0,), (0,)), ((), ()))


def _causal(tq, tk):
    return lax.broadcasted_iota(jnp.int32, (tq, tk), 0) >= lax.broadcasted_iota(jnp.int32, (tq, tk), 1)


def attn_fwd(q, k, v, t):
    rows = q.shape[0]
    n = rows // t

    def body(q_ref, k_ref, v_ref, o_ref, lse_ref, m_s, l_s, acc_s):
        i, j = pl.program_id(1), pl.program_id(2)

        @pl.when(j == 0)
        def _():
            m_s[...] = jnp.full_like(m_s, -jnp.inf)
            l_s[...] = jnp.zeros_like(l_s)
            acc_s[...] = jnp.zeros_like(acc_s)

        @pl.when(j <= i)
        def _():
            s = lax.dot_general(q_ref[...], k_ref[...], NT, preferred_element_type=f32) * ATT_SCALE
            s = jnp.where(jnp.logical_or(j < i, _causal(t, t)), s, -jnp.inf)
            m_prev = m_s[...]
            m_new = jnp.maximum(m_prev, jnp.max(s, axis=1, keepdims=True))
            alpha = jnp.exp(m_prev - m_new)
            p = jnp.exp(s - m_new[:, :1])
            l_s[...] = alpha * l_s[...] + jnp.sum(p, axis=1, keepdims=True)
            acc_s[...] = alpha * acc_s[...] + jnp.dot(p.astype(bf16), v_ref[...], preferred_element_type=f32)
            m_s[...] = m_new

        @pl.when(j == n - 1)
        def _():
            o_ref[...] = (acc_s[...] / l_s[...]).astype(o_ref.dtype)
            lse_ref[...] = m_s[...] + jnp.log(l_s[...])

    qs = pl.BlockSpec((t, LANE), lambda h, i, j: (i, h))
    ks = pl.BlockSpec((t, LANE), lambda h, i, j: (jnp.minimum(j, i), h))
    return pl.pallas_call(
        body, name="attn_fwd", grid=(N_HEADS, n, n), in_specs=[qs, ks, ks], out_specs=[qs, qs],
        out_shape=[jax.ShapeDtypeStruct((rows, N_HEADS * LANE), bf16), jax.ShapeDtypeStruct((rows, N_HEADS * LANE), f32)],
        scratch_shapes=[pltpu.VMEM((t, LANE), f32)] * 3,
        compiler_params=_cp(("parallel", "parallel", "arbitrary")),
    )(q, k, v)


def _attn_p_ds(q, k, v, do, o, lse, diag):
    s = lax.dot_general(q, k, NT, preferred_element_type=f32) * ATT_SCALE
    p = jnp.exp(s - lse[:, :1])
    p = jnp.where(jnp.logical_or(jnp.logical_not(diag), _causal(*s.shape)), p, 0.0)
    dp = lax.dot_general(do, v, NT, preferred_element_type=f32)
    delta = jnp.sum(do.astype(f32) * o.astype(f32), axis=1, keepdims=True)
    ds = p * (dp - delta) * ATT_SCALE
    return p, ds


def attn_bwd_dq(q, k, v, do, o, lse, t):
    rows = q.shape[0]
    n = rows // t

    def body(q_ref, k_ref, v_ref, do_ref, o_ref, lse_ref, dq_ref, acc_s):
        i, j = pl.program_id(1), pl.program_id(2)

        @pl.when(j == 0)
        def _():
            acc_s[...] = jnp.zeros_like(acc_s)

        @pl.when(j <= i)
        def _():
            _, ds = _attn_p_ds(q_ref[...], k_ref[...], v_ref[...], do_ref[...], o_ref[...], lse_ref[...], j == i)
            acc_s[...] += jnp.dot(ds.astype(bf16), k_ref[...], preferred_element_type=f32)

        @pl.when(j == n - 1)
        def _():
            dq_ref[...] = acc_s[...]

    qs = pl.BlockSpec((t, LANE), lambda h, i, j: (i, h))
    ks = pl.BlockSpec((t, LANE), lambda h, i, j: (jnp.minimum(j, i), h))
    return pl.pallas_call(
        body, name="attn_bwd_dq", grid=(N_HEADS, n, n), in_specs=[qs, ks, ks, qs, qs, qs], out_specs=qs,
        out_shape=jax.ShapeDtypeStruct((rows, N_HEADS * LANE), f32),
        scratch_shapes=[pltpu.VMEM((t, LANE), f32)],
        compiler_params=_cp(("parallel", "parallel", "arbitrary")),
    )(q, k, v, do, o, lse)


def attn_bwd_dkv(q, k, v, do, o, lse, t):
    rows = q.shape[0]
    n = rows // t

    def body(q_ref, k_ref, v_ref, do_ref, o_ref, lse_ref, dk_ref, dv_ref, dk_s, dv_s):
        j, i = pl.program_id(1), pl.program_id(2)

        @pl.when(i == 0)
        def _():
            dk_s[...] = jnp.zeros_like(dk_s)
            dv_s[...] = jnp.zeros_like(dv_s)

        @pl.when(i >= j)
        def _():
            p, ds = _attn_p_ds(q_ref[...], k_ref[...], v_ref[...], do_ref[...], o_ref[...], lse_ref[...], j == i)
            dv_s[...] += lax.dot_general(p.astype(bf16), do_ref[...], TN, preferred_element_type=f32)
            dk_s[...] += lax.dot_general(ds.astype(bf16), q_ref[...], TN, preferred_element_type=f32)

        @pl.when(i == n - 1)
        def _():
            dk_ref[...] = dk_s[...]
            dv_ref[...] = dv_s[...]

    qs = pl.BlockSpec((t, LANE), lambda h, j, i: (jnp.maximum(i, j), h))
    ks = pl.BlockSpec((t, LANE), lambda h, j, i: (j, h))
    return pl.pallas_call(
        body, name="attn_bwd_dkv", grid=(N_HEADS, n, n), in_specs=[qs, ks, ks, qs, qs, qs], out_specs=[ks, ks],
        out_shape=[jax.ShapeDtypeStruct((rows, N_HEADS * LANE), f32)] * 2,
        scratch_shapes=[pltpu.VMEM((t, LANE), f32)] * 2,
        compiler_params=_cp(("parallel", "parallel", "arbitrary")),
    )(q, k, v, do, o, lse)


def _steps(tm):
    k, out = 1, []
    while k < tm:
        out.append(k)
        k *= 2
    return out


def scan_fwd(a, u, tm):
    rows, ch = a.shape
    n = rows // tm

    def body(a_ref, u_ref, h_ref, h_s):
        @pl.when(pl.program_id(0) == 0)
        def _():
            h_s[...] = jnp.zeros_like(h_s)

        av, bv = a_ref[...], u_ref[...]
        row = lax.broadcasted_iota(jnp.int32, av.shape, 0)
        for k in _steps(tm):
            a_sh = jnp.where(row >= k, pltpu.roll(av, k, 0), 1.0)
            b_sh = jnp.where(row >= k, pltpu.roll(bv, k, 0), 0.0)
            bv = av * b_sh + bv
            av = av * a_sh
        h = bv + av * h_s[HALO - 1:HALO, :]
        h_ref[...] = h
        h_s[...] = h[tm - HALO:, :]

    spec = pl.BlockSpec((tm, ch), lambda i: (i, 0))
    return pl.pallas_call(
        body, name="lru_scan_fwd", grid=(n,), in_specs=[spec, spec], out_specs=spec,
        out_shape=jax.ShapeDtypeStruct((rows, ch), f32), scratch_shapes=[pltpu.VMEM((HALO, ch), f32)],
        compiler_params=_cp(("arbitrary",)),
    )(a, u)


def scan_bwd(a, h, dh, tm):
    rows, ch = a.shape
    n = rows // tm
    per = tm // HALO

    def body(a_ref, h_ref, hp_ref, dh_ref, da_ref, du_ref, g_s, a_s):
        i = pl.program_id(0)
        step = n - 1 - i

        @pl.when(i == 0)
        def _():
            g_s[...] = jnp.zeros_like(g_s)
            a_s[...] = jnp.zeros_like(a_s)

        a0 = a_ref[...]
        row = lax.broadcasted_iota(jnp.int32, a0.shape, 0)
        av = jnp.where(row < tm - 1, pltpu.roll(a0, tm - 1, 0), a_s[0:1, :])
        bv = dh_ref[...]
        for k in _steps(tm):
            a_sh = jnp.where(row < tm - k, pltpu.roll(av, tm - k, 0), 1.0)
            b_sh = jnp.where(row < tm - k, pltpu.roll(bv, tm - k, 0), 0.0)
            bv = bv + av * b_sh
            av = av * a_sh
        g = bv + av * g_s[0:1, :]
        h_last = jnp.where(step > 0, hp_ref[HALO - 1:HALO, :], 0.0)
        h_prev = jnp.where(row >= 1, pltpu.roll(h_ref[...], 1, 0), h_last)
        du_ref[...] = g
        da_ref[...] = g * h_prev
        g_s[...] = g[0:HALO, :]
        a_s[...] = a0[0:HALO, :]

    spec = pl.BlockSpec((tm, ch), lambda i: (n - 1 - i, 0))
    hp_spec = pl.BlockSpec((HALO, ch), lambda i: (jnp.maximum((n - 1 - i) * per - 1, 0), 0))
    return pl.pallas_call(
        body, name="lru_scan_bwd", grid=(n,), in_specs=[spec, spec, hp_spec, spec], out_specs=[spec, spec],
        out_shape=[jax.ShapeDtypeStruct((rows, ch), f32)] * 2,
        scratch_shapes=[pltpu.VMEM((HALO, ch), f32)] * 2,
        compiler_params=_cp(("arbitrary",)),
    )(a, h, h, dh)


def _rms(x, g):
    return x * lax.rsqrt(jnp.mean(x * x, axis=-1, keepdims=True) + EPS) * g


def f_rms(step, p, c, x):
    return [_rms(x[0], p[0])], []


def _rope(x, cosf, sinf):
    lane = lax.broadcasted_iota(jnp.int32, x.shape, 1)
    sw = jnp.where(lane < KR_LANE + QK_ROPE // 2, lane_roll(x, LANE - QK_ROPE // 2), lane_roll(x, QK_ROPE // 2))
    return x * cosf + sw * sinf


def f_prep(step, p, c, x):
    q, kn, kr, cosf, sinf = x
    kr_rot = _rope(kr, cosf, sinf)
    qr = [_rope(q[:, LANE * h:LANE * (h + 1)], cosf, sinf) for h in range(N_HEADS)]
    kk = [kn[:, LANE * h:LANE * (h + 1)] + kr_rot for h in range(N_HEADS)]
    return [jnp.concatenate(qr, axis=1), jnp.concatenate(kk, axis=1)], []


def _conv(tail, x, w, b):
    xf = jnp.concatenate([tail, x], axis=0)
    acc = b + w[CONV_W - 1:CONV_W, :] * xf
    for k in range(CONV_W - 1):
        acc = acc + w[k:k + 1, :] * shift_down(xf, CONV_W - 1 - k)
    return acc[HALO:, :]


def f_pool(step, p, c, x):
    wp, sc = p
    (tail,) = c
    (u,) = x
    tm = u.shape[0]
    xf = jnp.concatenate([tail, u], axis=0)
    sums, s, w = [], xf, 1
    while w < POOL_WINDOWS[-1]:
        s = s + shift_down(s, w)
        w *= 2
        sums.append(s)
    t = step * tm + lax.broadcasted_iota(jnp.int32, (tm, 1), 0)
    ys = []
    for g, (w, s) in enumerate(zip(POOL_WINDOWS, sums)):
        sl = slice(LANE * g, LANE * (g + 1))
        cnt = jnp.minimum(t + 1, w).astype(f32)
        d = s[POOL_HALO:, sl] / cnt - u[:, sl]
        ys.append(jnp.dot(d.astype(bf16), wp[LANE * g:LANE * (g + 1), :].astype(bf16), preferred_element_type=f32))
    return [jnp.concatenate(ys, axis=1) * sc], [u[tm - POOL_HALO:, :]]


def _expand_heads(v):
    hrow = lax.broadcasted_iota(jnp.int32, (LANE, MIX), 0)
    col = lax.broadcasted_iota(jnp.int32, (LANE, MIX), 1)
    e = jnp.logical_and(col >= hrow * 64, col < hrow * 64 + 64).astype(f32)
    return jnp.dot(v, e, precision=lax.Precision.HIGHEST, preferred_element_type=f32)


def f_ssd(step, p, c, x):
    conv_w, conv_b, dtb, alog, dsk, ng = p
    tail, s_in = c[0], c[1:]
    z, xbc, dt = x
    ln = z.shape[0]
    xc = jax.nn.silu(_conv(tail, xbc, conv_w, conv_b))
    xs, bb, cc = xc[:, :MIX], xc[:, MIX:MIX + LANE], xc[:, MIX + LANE:]
    dtv = jax.nn.softplus(dt + dtb[0:1, :])
    a_head = -jnp.exp(alog)
    a = dtv * a_head[0:1, :]
    ri = lax.broadcasted_iota(jnp.int32, (ln, ln), 0)
    ci = lax.broadcasted_iota(jnp.int32, (ln, ln), 1)
    tril = (ri >= ci).astype(f32)
    triu = (ri <= ci).astype(f32)
    hi = lax.Precision.HIGHEST
    a_cs = jnp.dot(tril, a, precision=hi, preferred_element_type=f32)
    a_cs_t = lax.dot_general(a, triu, TN, precision=hi, preferred_element_type=f32)
    dt_e = _expand_heads(dtv)
    a_e = dt_e * _expand_heads(a_head)[0:1, :]
    a_cs_e = jnp.dot(tril, a_e, precision=hi, preferred_element_type=f32)
    tot_e = jnp.sum(a_e, axis=0, keepdims=True)
    xdt = xs * dt_e
    lane = lax.broadcasted_iota(jnp.int32, (1, LANE), 1)
    half = [(lane < 64).astype(f32), (lane >= 64).astype(f32)]
    hrow = lax.broadcasted_iota(jnp.int32, (LANE, 1), 0)
    cg = [(cc * half[g]).astype(bf16) for g in range(2)]
    bg = [(bb * half[g]).astype(bf16) for g in range(2)]
    cb = [lax.dot_general(cg[g], bg[g], NT, preferred_element_type=f32) for g in range(2)]
    ys, s_out = [], []
    for j in range(4):
        g = j // 2
        blk = slice(LANE * j, LANE * (j + 1))
        xj = xdt[:, blk]
        yj = jnp.zeros((ln, LANE), f32)
        for hh in range(2):
            h = 2 * j + hh
            col = jnp.sum(a_cs * (lane == h).astype(f32), axis=1, keepdims=True)
            rowv = jnp.sum(a_cs_t * (hrow == h).astype(f32), axis=0, keepdims=True)
            lmat = jnp.exp(jnp.where(ri >= ci, col - rowv, -jnp.inf))
            yj = yj + jnp.dot((cb[g] * lmat).astype(bf16), (xj * half[hh]).astype(bf16), preferred_element_type=f32)
        acs = a_cs_e[:, blk]
        tot = tot_e[:, blk]
        yj = yj + jnp.exp(acs) * jnp.dot(cg[g], s_in[j].astype(bf16), preferred_element_type=f32)
        s_new = jnp.exp(tot) * s_in[j] + lax.dot_general(bg[g], (xj * jnp.exp(tot - acs)).astype(bf16), TN,
                                                         preferred_element_type=f32)
        ys.append(yj)
        s_out.append(s_new)
    y = jnp.concatenate(ys, axis=1) + xs * _expand_heads(dsk)[0:1, :]
    y = y * jax.nn.silu(z)
    return [_rms(y, ng)], [xbc[ln - HALO:, :]] + s_out


def _neg_expm1(y):
    series = -y * (1.0 + y * (0.5 + y * (1.0 / 6 + y * (1.0 / 24 + y * (1.0 / 120)))))
    return jnp.where(y > -0.05, series, 1.0 - jnp.exp(y))


def f_lru_pre(step, p, c, x):
    cw, cb_, wa, ba, wi, bi, lam = p
    (tail,) = c
    (lx,) = x
    tm = lx.shape[0]
    xc = _conv(tail, lx, cw, cb_)
    xb = xc.astype(bf16)
    r = jax.nn.sigmoid(jnp.dot(xb, wa.astype(bf16), preferred_element_type=f32) + ba)
    it = jax.nn.sigmoid(jnp.dot(xb, wi.astype(bf16), preferred_element_type=f32) + bi)
    log_a = -LRU_C * r * jax.nn.softplus(-lam)
    mult = jnp.sqrt(_neg_expm1(2.0 * log_a))
    return [jnp.exp(log_a), xc * it * mult], [lx[tm - HALO:, :]]


def f_lru_post(step, p, c, x):
    h, g = x
    return [h * jax.nn.gelu(g)], []


def loss_head(x, tgt, g, tm):
    rows, d = x.shape
    n = rows // tm

    def body(x_ref, t_ref, g_ref, loss_ref, dx_ref, dg_ref):
        @pl.when(pl.program_id(0) == 0)
        def _():
            loss_ref[...] = jnp.zeros_like(loss_ref)
            dg_ref[...] = jnp.zeros_like(dg_ref)

        def fn(gv, xv):
            err = _rms(xv, gv) - t_ref[...]
            return 0.5 * jnp.sum(jnp.mean(err * err, axis=-1, keepdims=True))

        val, (dg, dx) = jax.value_and_grad(fn, argnums=(0, 1))(g_ref[...], x_ref[...])
        loss_ref[...] += val
        dg_ref[...] += dg
        dx_ref[...] = dx

    spec = pl.BlockSpec((tm, d), lambda i: (i, 0))
    return pl.pallas_call(
        body, name="loss_head", grid=(n,), in_specs=[spec, spec, _const_spec((1, d))],
        out_specs=[_const_spec((8, LANE)), spec, _const_spec((1, d))],
        out_shape=[jax.ShapeDtypeStruct((8, LANE), f32), jax.ShapeDtypeStruct((rows, d), f32),
                   jax.ShapeDtypeStruct((1, d), f32)],
        compiler_params=_cp(("arbitrary",)),
    )(x, tgt, g)


def ew(name, fn, ins, outs, tm):
    rows = ins[0][0].shape[0]
    ni = len(ins)

    def body(*refs):
        res = fn(*[r[...].astype(f32) for r in refs[:ni]])
        for r, v in zip(refs[ni:], res):
            r[...] = v.astype(r.dtype)

    return pl.pallas_call(
        body, name=name, grid=(rows // tm,), in_specs=[_tile_spec(tm, w, cb) for (_, w, cb) in ins],
        out_specs=[_tile_spec(tm, w, 0) for (w, _) in outs],
        out_shape=[jax.ShapeDtypeStruct((rows, w), dt) for (w, dt) in outs],
        compiler_params=_cp(("parallel",)),
    )(*[t[0] for t in ins])


def _peers():
    x, y, c = lax.axis_index("x"), lax.axis_index("y"), lax.axis_index("c")
    me = 4 * x + 2 * y + c
    out = []
    for k in range(1, N_DEV):
        px = 1 - x if k & 4 else x
        py = 1 - y if k & 2 else y
        pc = 1 - c if k & 1 else c
        out.append(((px, py, pc), 4 * px + 2 * py + pc))
    return me, out


def _exchange(name, src, gather):
    shape = src.shape if gather else src.shape[1:]

    def body(src_ref, out_ref, send_sems, recv_sems, loc_sem):
        me, peers = _peers()
        mine = src_ref if gather else src_ref.at[me]
        local = pltpu.make_async_copy(mine, out_ref.at[me], loc_sem)
        local.start()
        sends = []
        for k, (dev, pid) in enumerate(peers):
            cp = pltpu.make_async_remote_copy(
                src_ref=src_ref if gather else src_ref.at[pid], dst_ref=out_ref.at[me],
                send_sem=send_sems.at[k], recv_sem=recv_sems.at[k], device_id=dev, device_id_type=pl.DeviceIdType.MESH)
            cp.start()
            sends.append(cp)
        for k, (dev, pid) in enumerate(peers):
            pltpu.make_async_remote_copy(
                src_ref=mine, dst_ref=out_ref.at[pid], send_sem=send_sems.at[k], recv_sem=recv_sems.at[k],
                device_id=dev, device_id_type=pl.DeviceIdType.MESH).wait_recv()
        for cp in sends:
            cp.wait_send()
        local.wait()

    return pl.pallas_call(
        body, name=name, in_specs=[pl.BlockSpec(memory_space=pl.ANY)], out_specs=pl.BlockSpec(memory_space=pl.ANY),
        out_shape=jax.ShapeDtypeStruct((N_DEV,) + tuple(shape), src.dtype),
        scratch_shapes=[pltpu.SemaphoreType.DMA((N_DEV - 1,)), pltpu.SemaphoreType.DMA((N_DEV - 1,)),
                        pltpu.SemaphoreType.DMA],
        compiler_params=pltpu.CompilerParams(has_side_effects=True),
    )(src)


def all_gather(name, src):
    return _exchange(name, src, True)


def all_to_all(name, src):
    return _exchange(name, src, False)


def adamw(name, parts, w, m, v):
    shape = w.shape
    c = shape[-1]
    r = 1
    for s in shape[:-1]:
        r *= s
    tr = _pick(r, 256) if r % 8 == 0 else r
    parts2, w2, m2, v2 = parts.reshape(N_DEV, r, c), w.reshape(r, c), m.reshape(r, c), v.reshape(r, c)

    def body(p_ref, w_ref, m_ref, v_ref, g_ref, d_ref, nm_ref, nv_ref):
        g = p_ref[0].astype(f32)
        for i in range(1, N_DEV):
            g = g + p_ref[i].astype(f32)
        mn = ADAM_B1 * m_ref[...] + (1.0 - ADAM_B1) * g
        vn = ADAM_B2 * v_ref[...] + (1.0 - ADAM_B2) * jnp.square(g)
        m_hat = mn / (1.0 - ADAM_B1 ** ADAM_STEP)
        v_hat = vn / (1.0 - ADAM_B2 ** ADAM_STEP)
        g_ref[...] = g
        d_ref[...] = -ADAM_LR * (m_hat / (jnp.sqrt(v_hat) + ADAM_EPS) + ADAM_WD * w_ref[...])
        nm_ref[...] = mn
        nv_ref[...] = vn

    spec = pl.BlockSpec((tr, c), lambda i: (i, 0))
    res = pl.pallas_call(
        body, name=name, grid=(r // tr,), in_specs=[pl.BlockSpec((N_DEV, tr, c), lambda i: (0, i, 0)), spec, spec, spec],
        out_specs=[spec] * 4, out_shape=[jax.ShapeDtypeStruct((r, c), f32)] * 4,
        compiler_params=_cp(("parallel",)),
    )(parts2, w2, m2, v2)
    return [a.reshape(shape) for a in res]


_IN_SPLITS = dict(cq=(0, 384), ckv=(384, 640), kr=(640, 672), pool=(672, 1184), z=(1184, 1696), xbc=(1696, 2464),
                  dt=(2464, 2472), lg=(2472, 2984), lx=(2984, 3496), gates=(3496, 7592))


def _pad_w_in(w):
    s = lambda n: w[:, _IN_SPLITS[n][0]:_IN_SPLITS[n][1]]
    z = lambda n: jnp.zeros((w.shape[0], n), w.dtype)
    return jnp.concatenate([s("gates"), s("pool"), s("z"), s("lg"), s("lx"), s("xbc"), s("cq"), z(KR_LANE), s("kr"),
                            z(LANE - KR_LANE - QK_ROPE), s("ckv"), s("dt"), z(LANE - 8), z(U_COLS - U_DT[0] - LANE)], axis=1)


def _unpad_w_in(g):
    c = lambda o, n: g[:, o:o + n]
    return jnp.concatenate([c(U_CQ[0], 384), c(U_CKV[0], 256), c(U_KR[0] + KR_LANE, QK_ROPE), c(U_POOL[0], 512),
                            c(U_Z[0], 512), c(U_XBC[0], 768), c(U_DT[0], 8), c(U_LG[0], 512), c(U_LX[0], 512),
                            c(0, 4096)], axis=1)


def _head_pad_cols(w, per, lo, hi):
    k = w.shape[0]
    w = w.reshape(k, N_HEADS, per)[:, :, lo:hi]
    return jnp.pad(w, ((0, 0), (0, 0), (0, LANE - (hi - lo)))).reshape(k, N_HEADS * LANE)


def _head_unpad_cols(g, n):
    k = g.shape[0]
    return g.reshape(k, N_HEADS, LANE)[:, :, :n]


def _block_diag(w):
    out = jnp.zeros((MIX, MIX), w.dtype)
    for i in range(8):
        out = lax.dynamic_update_slice(out, w[i], (64 * i, 64 * i))
    return out


def _block_diag_inv(g):
    return jnp.stack([g[64 * i:64 * (i + 1), 64 * i:64 * (i + 1)] for i in range(8)])


def _head8(v):
    return jnp.zeros((8, LANE), f32).at[0, :8].set(v)


def _layer_weights(fw, l):
    w_ukv = fw["w_ukv"][l]
    wb = fw["w_branch"][l]
    wb0 = jnp.pad(wb[0].reshape(N_HEADS, V_HEAD, D_MODEL), ((0, 0), (0, LANE - V_HEAD), (0, 0))).reshape(N_HEADS * LANE, D_MODEL)
    return dict(
        w_in=_pad_w_in(fw["w_in"][l]),
        w_uq=_head_pad_cols(fw["w_uq"][l], QK_NOPE + QK_ROPE, 0, QK_NOPE + QK_ROPE),
        w_uk=_head_pad_cols(w_ukv, QK_NOPE + V_HEAD, 0, QK_NOPE),
        w_uv=_head_pad_cols(w_ukv, QK_NOPE + V_HEAD, QK_NOPE, QK_NOPE + V_HEAD),
        wb=[wb0, wb[1], wb[2], wb[3]],
        w_out=fw["w_out"][l], w_ff1=fw["w_ff1"][l], w_ff2=fw["w_ff2"][l], w_pg=fw["w_ple_gate"][l], w_ple=fw["w_ple"][l],
    )


def _layer_params(sp, l):
    row = lambda n: sp[n][l][None, :]
    return dict(
        g_mix=row("g_mix"), q_norm=row("q_norm"), kv_norm=row("kv_norm"),
        pool=[sp["w_pool"][l].reshape(4 * LANE, LANE), row("pool_scale")],
        ssd=[sp["ssd_conv_w"][l], row("ssd_conv_b"), _head8(sp["ssd_dt_bias"][l]), _head8(sp["ssd_a_log"][l]),
             _head8(sp["ssd_d"][l]), row("ssd_norm")],
        lru=[sp["lru_conv_w"][l], row("lru_conv_b"), _block_diag(sp["lru_w_a"][l]), row("lru_b_a"),
             _block_diag(sp["lru_w_i"][l]), row("lru_b_i"), row("lru_lambda")],
        g_mlp=row("g_mlp"), g_ple=row("g_ple"),
    )


_sig = jax.nn.sigmoid
_SSD_CARRY = [(HALO, SSD_XBC)] + [(LANE, LANE)] * 4


def _tiles(rows):
    return dict(tm=_pick(rows, 512), ta=_pick(rows, 512), tp=_pick(rows, 256), tl=_pick(rows, 256), ts=_pick(rows, 256))


def _mixer_tiles(u):
    return dict(
        cq=(u, 384, U_CQ[0] // 384), ckv=(u, 256, U_CKV[0] // 256), kr=(u, LANE, U_KR[0] // LANE),
        pool=(u, MIX, U_POOL[0] // MIX), z=(u, MIX, U_Z[0] // MIX), xbc=(u, SSD_XBC, U_XBC[0] // SSD_XBC),
        dt=(u, LANE, U_DT[0] // LANE), lg=(u, MIX, U_LG[0] // MIX), lx=(u, MIX, U_LX[0] // MIX))


def _layer_fwd(x, p_bf, w, pr, cosf, sinf, tag):
    rows = x.shape[0]
    ts = _tiles(rows)
    tm = ts["tm"]
    nm = lambda s: f"{s}_{tag}"
    r = dict(x=x)
    (h,), _ = seq_fwd(nm("rms_in"), f_rms, [pr["g_mix"]], [(x, D_MODEL, 0)], [], [(D_MODEL, bf16)], tm)
    u = matmul(nm("w_in"), h, w["w_in"])
    mt = _mixer_tiles(u)
    (cqn,), _ = seq_fwd(nm("rms_q"), f_rms, [pr["q_norm"]], [mt["cq"]], [], [(Q_LORA, bf16)], tm)
    (ckvn,), _ = seq_fwd(nm("rms_kv"), f_rms, [pr["kv_norm"]], [mt["ckv"]], [], [(KV_LORA, bf16)], tm)
    q = matmul(nm("w_uq"), cqn, w["w_uq"])
    kn = matmul(nm("w_uk"), ckvn, w["w_uk"])
    vb = matmul(nm("w_uv"), ckvn, w["w_uv"], outs=(bf16,))
    hw = N_HEADS * LANE
    (qr, kr), _ = seq_fwd(nm("mla_prep"), f_prep, [], [(q, hw, 0), (kn, hw, 0), mt["kr"], (cosf, LANE, 0), (sinf, LANE, 0)],
                          [], [(hw, bf16), (hw, bf16)], tm)
    o, lse = attn_fwd(qr, kr, vb, ts["ta"])
    (yb,), pool_saved = seq_fwd(nm("pool"), f_pool, pr["pool"], [mt["pool"]], [(POOL_HALO, MIX)], [(MIX, bf16)], ts["tp"])
    (yc,), ssd_saved = seq_fwd(nm("ssd"), f_ssd, pr["ssd"], [mt["z"], mt["xbc"], mt["dt"]], _SSD_CARRY, [(MIX, bf16)], SSD_CHUNK)
    (la, lu), lru_saved = seq_fwd(nm("lru_pre"), f_lru_pre, pr["lru"], [mt["lx"]], [(HALO, MIX)], [(MIX, f32), (MIX, f32)], ts["tl"])
    hh = scan_fwd(la, lu, ts["ts"])
    (yd,), _ = seq_fwd(nm("lru_post"), f_lru_post, [], [(hh, MIX, 0), mt["lg"]], [], [(MIX, bf16)], tm)
    ys = [o, yb, yc, yd]
    pres = []
    m = None
    for n in range(4):
        last = n == 3
        if n == 0:
            epi = lambda acc, g: (_sig(g) * acc, acc)
            extras = [(u, 0)]
        else:
            epi = lambda acc, g, prev: (prev + _sig(g) * acc, acc)
            extras = [(u, D_MODEL * n), (m, 0)]
        m, pre = matmul(nm(f"branch{n}"), ys[n], w["wb"][n], outs=(bf16 if last else f32, f32), epi=epi, extras=extras)
        pres.append(pre)
    x1 = matmul(nm("w_out"), m, w["w_out"], epi=lambda acc, xr: (acc + xr,), extras=[(x, 0)])
    (h2,), _ = seq_fwd(nm("rms_mlp"), f_rms, [pr["g_mlp"]], [(x1, D_MODEL, 0)], [], [(D_MODEL, bf16)], tm)
    a1, act = matmul(nm("ff1"), h2, w["w_ff1"], outs=(f32, bf16), epi=lambda acc: (acc, jnp.square(jnp.maximum(acc, 0.0))))
    x2 = matmul(nm("ff2"), act, w["w_ff2"], epi=lambda acc, xr: (acc + xr,), extras=[(x1, 0)])
    (h3,), _ = seq_fwd(nm("rms_ple"), f_rms, [pr["g_ple"]], [(x2, D_MODEL, 0)], [], [(D_MODEL, bf16)], tm)
    gl = matmul(nm("ple_gate"), h3, w["w_pg"])
    x3, pe = matmul(nm("ple"), p_bf, w["w_ple"], outs=(f32, f32), epi=lambda acc, g, xr: (xr + acc * _sig(g), acc),
                    extras=[(gl, 0), (x2, 0)])
    r.update(h=h, u=u, cqn=cqn, ckvn=ckvn, q=q, kn=kn, vb=vb, qr=qr, kr=kr, o=o, lse=lse, ys=ys, pres=pres, m=m, x1=x1,
             h2=h2, a1=a1, act=act, x2=x2, h3=h3, gl=gl, pe=pe, p_bf=p_bf, pool_saved=pool_saved, ssd_saved=ssd_saved,
             lru_saved=lru_saved, la=la, hh=hh)
    return x3, r


def _gate_bwd(d, g, pre):
    s = _sig(g)
    return d * s, d * pre * s * (1.0 - s)


def _layer_bwd(dx3, r, w, pr, cosf, sinf, tag):
    rows = dx3.shape[0]
    ts = _tiles(rows)
    tm = ts["tm"]
    nm = lambda s: f"{s}_{tag}"
    u = r["u"]
    mt = _mixer_tiles(u)
    hw = N_HEADS * LANE
    g = {}
    full = lambda a: (a, a.shape[1], 0)
    dpe, dgl = ew(nm("ple_bwd"), _gate_bwd, [full(dx3), full(r["gl"]), full(r["pe"])], [(D_MODEL, bf16)] * 2, tm)
    g["w_ple"] = matmul(nm("d_w_ple"), r["p_bf"], dpe, ta=True)
    g["w_pg"] = matmul(nm("d_w_pg"), r["h3"], dgl, ta=True)
    dh3 = matmul(nm("d_h3"), dgl, w["w_pg"], tb=True)
    (g["g_ple"],), (dx2,) = seq_bwd(nm("rms_ple_bwd"), f_rms, [pr["g_ple"]], [full(r["x2"])], [True], [], [dh3], [f32], tm,
                                    add_to=(0, dx3))
    da1 = matmul(nm("d_act"), dx2, w["w_ff2"], tb=True, outs=(bf16,),
                 epi=lambda acc, a: (acc * 2.0 * jnp.maximum(a, 0.0),), extras=[(r["a1"], 0)])
    g["w_ff2"] = matmul(nm("d_w_ff2"), r["act"], dx2, ta=True)
    g["w_ff1"] = matmul(nm("d_w_ff1"), r["h2"], da1, ta=True)
    dh2 = matmul(nm("d_h2"), da1, w["w_ff1"], tb=True)
    (g["g_mlp"],), (dx1,) = seq_bwd(nm("rms_mlp_bwd"), f_rms, [pr["g_mlp"]], [full(r["x1"])], [True], [], [dh2], [f32], tm,
                                    add_to=(0, dx2))
    dm = matmul(nm("d_merged"), dx1, w["w_out"], tb=True)
    g["w_out"] = matmul(nm("d_w_out"), r["m"], dx1, ta=True)
    dgates, dys, g["wb"] = [], [], []
    for n in range(4):
        dpre, dgate = ew(nm(f"gate_bwd{n}"), _gate_bwd, [full(dm), (u, D_MODEL, n), full(r["pres"][n])],
                         [(D_MODEL, bf16)] * 2, tm)
        dgates.append(dgate)
        g["wb"].append(matmul(nm(f"d_w_branch{n}"), r["ys"][n], dpre, ta=True))
        dys.append(matmul(nm(f"d_y{n}"), dpre, w["wb"][n], tb=True, outs=(bf16 if n == 0 else f32,)))
    dqr = attn_bwd_dq(r["qr"], r["kr"], r["vb"], dys[0], r["o"], r["lse"], ts["ta"])
    dkr_, dv = attn_bwd_dkv(r["qr"], r["kr"], r["vb"], dys[0], r["o"], r["lse"], ts["ta"])
    _, (dq, dkn, dkrope) = seq_bwd(nm("mla_prep_bwd"), f_prep, [],
                                   [full(r["q"]), full(r["kn"]), mt["kr"], full(cosf), full(sinf)],
                                   [True, True, True, False, False], [], [dqr, dkr_], [bf16] * 3, tm)
    g["w_uq"] = matmul(nm("d_w_uq"), r["cqn"], dq, ta=True)
    g["w_uk"] = matmul(nm("d_w_uk"), r["ckvn"], dkn, ta=True)
    g["w_uv"] = matmul(nm("d_w_uv"), r["ckvn"], dv, ta=True)
    dcqn = matmul(nm("d_cqn"), dq, w["w_uq"], tb=True)
    dckvn = matmul(nm("d_ckvn_k"), dkn, w["w_uk"], tb=True)
    dckvn = matmul(nm("d_ckvn_v"), dv, w["w_uv"], tb=True, epi=lambda acc, prev: (acc + prev,), extras=[(dckvn, 0)])
    (g["q_norm"],), (dcq,) = seq_bwd(nm("rms_q_bwd"), f_rms, [pr["q_norm"]], [mt["cq"]], [True], [], [dcqn], [bf16], tm)
    (g["kv_norm"],), (dckv,) = seq_bwd(nm("rms_kv_bwd"), f_rms, [pr["kv_norm"]], [mt["ckv"]], [True], [], [dckvn], [bf16], tm)
    g["pool"], (dpool,) = seq_bwd(nm("pool_bwd"), f_pool, pr["pool"], [mt["pool"]], [True], r["pool_saved"], [dys[1]],
                                  [bf16], ts["tp"])
    g["ssd"], (dz, dxbc, ddt) = seq_bwd(nm("ssd_bwd"), f_ssd, pr["ssd"], [mt["z"], mt["xbc"], mt["dt"]], [True] * 3,
                                        r["ssd_saved"], [dys[2]], [bf16] * 3, SSD_CHUNK)
    _, (dhh, dlg) = seq_bwd(nm("lru_post_bwd"), f_lru_post, [], [full(r["hh"]), mt["lg"]], [True, True], [], [dys[3]],
                            [f32, bf16], tm)
    da, du = scan_bwd(r["la"], r["hh"], dhh, ts["ts"])
    g["lru"], (dlx,) = seq_bwd(nm("lru_pre_bwd"), f_lru_pre, pr["lru"], [mt["lx"]], [True], r["lru_saved"], [da, du],
                               [bf16], ts["tl"])
    du_p = jnp.concatenate(dgates + [dpool, dz, dlg, dlx, dxbc, dcq, dkrope, dckv, ddt,
                                     jnp.zeros((rows, U_COLS - U_DT[0] - LANE), bf16)], axis=1)
    dh = matmul(nm("d_h"), du_p, w["w_in"], tb=True)
    g["w_in"] = matmul(nm("d_w_in"), r["h"], du_p, ta=True)
    (g["g_mix"],), (dx,) = seq_bwd(nm("rms_in_bwd"), f_rms, [pr["g_mix"]], [full(r["x"])], [True], [], [dh], [f32], tm,
                                   add_to=(0, dx1))
    return dx, g


def _layer_grads_to_reference_layout(g):
    dk = _head_unpad_cols(g["w_uk"], QK_NOPE)
    dv = _head_unpad_cols(g["w_uv"], V_HEAD)
    wb0 = g["wb"][0].reshape(N_HEADS, LANE, D_MODEL)[:, :V_HEAD].reshape(MIX, D_MODEL)
    ssd, lru, pool = g["ssd"], g["lru"], g["pool"]
    return dict(
        g_mix=g["g_mix"][0], w_in=_unpad_w_in(g["w_in"]), q_norm=g["q_norm"][0],
        w_uq=_head_unpad_cols(g["w_uq"], QK_NOPE + QK_ROPE).reshape(Q_LORA, -1), kv_norm=g["kv_norm"][0],
        w_ukv=jnp.concatenate([dk, dv], axis=2).reshape(KV_LORA, -1),
        w_pool=pool[0].reshape(4, LANE, LANE), pool_scale=pool[1][0],
        ssd_conv_w=ssd[0], ssd_conv_b=ssd[1][0], ssd_dt_bias=ssd[2][0, :8], ssd_a_log=ssd[3][0, :8], ssd_d=ssd[4][0, :8],
        ssd_norm=ssd[5][0],
        lru_conv_w=lru[0], lru_conv_b=lru[1][0], lru_w_a=_block_diag_inv(lru[2]), lru_b_a=lru[3][0],
        lru_w_i=_block_diag_inv(lru[4]), lru_b_i=lru[5][0], lru_lambda=lru[6][0],
        w_branch=jnp.stack([wb0, g["wb"][1], g["wb"][2], g["wb"][3]]), w_out=g["w_out"], g_mlp=g["g_mlp"][0],
        w_ff1=g["w_ff1"], w_ff2=g["w_ff2"], g_ple=g["g_ple"][0], w_ple_gate=g["w_pg"], w_ple=g["w_ple"],
    )


def _rope_tables(positions):
    inv = 1.0 / (ROPE_THETA ** (jnp.arange(0, QK_ROPE, 2, dtype=f32) / QK_ROPE))
    ang = positions.astype(f32)[:, None] * inv
    cos, sin = jnp.cos(ang), jnp.sin(ang)
    rows = positions.shape[0]
    pad = jnp.zeros((rows, LANE - KR_LANE - QK_ROPE), f32)
    cosf = jnp.concatenate([jnp.ones((rows, KR_LANE), f32), cos, cos, pad], axis=1)
    sinf = jnp.concatenate([jnp.zeros((rows, KR_LANE), f32), -sin, sin, pad], axis=1)
    return cosf, sinf


WEIGHTS = ['g_mix', 'w_in', 'q_norm', 'w_uq', 'kv_norm', 'w_ukv', 'w_pool', 'pool_scale', 'ssd_conv_w', 'ssd_conv_b',
           'ssd_dt_bias', 'ssd_a_log', 'ssd_d', 'ssd_norm', 'lru_conv_w', 'lru_conv_b', 'lru_w_a', 'lru_b_a', 'lru_w_i',
           'lru_b_i', 'lru_lambda', 'w_branch', 'w_out', 'g_mlp', 'w_ff1', 'w_ff2', 'g_ple', 'w_ple_gate', 'w_ple', 'g_final']
SHARDED = dict(w_in=2, w_uq=2, w_ukv=2, ssd_conv_w=2, lru_conv_w=2, w_branch=3, w_out=1, w_ff1=2, w_ff2=1,
               w_ple_gate=1, w_ple=2)
F32_PAYLOAD = ("ssd_conv_w", "lru_conv_w")
DEPTH = 2


def local_step(x, p, positions, tgt, fw, sp):
    cosf, sinf = _rope_tables(positions)
    ws = [_layer_weights(fw, l) for l in range(DEPTH)]
    prs = [_layer_params(sp, l) for l in range(DEPTH)]
    res = []
    for l in range(DEPTH):
        x, r = _layer_fwd(x, p[l].astype(bf16), ws[l], prs[l], cosf, sinf, f"l{l}")
        res.append(r)
    loss8, dx, dgf = loss_head(x, tgt, sp["g_final"][None, :], _pick(x.shape[0], 512))
    layer_grads = [None] * DEPTH
    for l in reversed(range(DEPTH)):
        dx, g = _layer_bwd(dx, res[l], ws[l], prs[l], cosf, sinf, f"l{l}")
        layer_grads[l] = _layer_grads_to_reference_layout(g)
    grads = {n: jnp.stack([layer_grads[l][n] for l in range(DEPTH)]) for n in WEIGHTS if n != "g_final"}
    grads["g_final"] = dgf[0]
    return loss8[0, 0], dx, grads


def _gather_full(name, w):
    ax = SHARDED[name]
    payload = w if name in F32_PAYLOAD else w.astype(bf16)
    got = all_gather(f"ag_{name}", payload)
    got = jnp.moveaxis(got, 0, ax)
    shape = list(w.shape)
    shape[ax] *= N_DEV
    return got.reshape(shape)


def _scatter_blocks(name, g):
    ax = SHARDED[name]
    shape = list(g.shape)
    shape[ax:ax + 1] = [N_DEV, shape[ax] // N_DEV]
    g = jnp.moveaxis(g.reshape(shape), ax, 0)
    return g if name in F32_PAYLOAD else g.astype(bf16)


def kernel(x, p, positions, g_mix, w_in, q_norm, w_uq, kv_norm, w_ukv, w_pool, pool_scale, ssd_conv_w, ssd_conv_b,
           ssd_dt_bias, ssd_a_log, ssd_d, ssd_norm, lru_conv_w, lru_conv_b, lru_w_a, lru_b_a, lru_w_i, lru_b_i,
           lru_lambda, w_branch, w_out, g_mlp, w_ff1, w_ff2, g_ple, w_ple_gate, w_ple, g_final, loss_target, m_g_mix,
           m_w_in, m_q_norm, m_w_uq, m_kv_norm, m_w_ukv, m_w_pool, m_pool_scale, m_ssd_conv_w, m_ssd_conv_b,
           m_ssd_dt_bias, m_ssd_a_log, m_ssd_d, m_ssd_norm, m_lru_conv_w, m_lru_conv_b, m_lru_w_a, m_lru_b_a,
           m_lru_w_i, m_lru_b_i, m_lru_lambda, m_w_branch, m_w_out, m_g_mlp, m_w_ff1, m_w_ff2, m_g_ple, m_w_ple_gate,
           m_w_ple, m_g_final, v_g_mix, v_w_in, v_q_norm, v_w_uq, v_kv_norm, v_w_ukv, v_w_pool, v_pool_scale,
           v_ssd_conv_w, v_ssd_conv_b, v_ssd_dt_bias, v_ssd_a_log, v_ssd_d, v_ssd_norm, v_lru_conv_w, v_lru_conv_b,
           v_lru_w_a, v_lru_b_a, v_lru_w_i, v_lru_b_i, v_lru_lambda, v_w_branch, v_w_out, v_g_mlp, v_w_ff1, v_w_ff2,
           v_g_ple, v_w_ple_gate, v_w_ple, v_g_final):
    given = dict(locals())
    wts = {n: given[n] for n in WEIGHTS}
    full = {n: (_gather_full(n, wts[n]) if n in SHARDED else wts[n]) for n in WEIGHTS}
    loss, grad_x, grads = local_step(x[0], p[:, 0], positions[0], loss_target[0], full, full)

    rep = [n for n in WEIGHTS if n not in SHARDED]
    flat = jnp.concatenate([grads[n].reshape(-1) for n in rep])
    pad = (-flat.shape[0]) % (8 * LANE)
    flat = jnp.pad(flat, (0, pad)).reshape(-1, LANE)
    got = all_gather("ag_small_grads", flat).reshape(N_DEV, -1)
    parts, off = {}, 0
    for n in rep:
        size = grads[n].size
        parts[n] = got[:, off:off + size].reshape((N_DEV,) + grads[n].shape)
        off += size
    for n in SHARDED:
        parts[n] = all_to_all(f"a2a_{n}", _scatter_blocks(n, grads[n]))

    outs = {n: adamw(f"adamw_{n}", parts[n], wts[n], given["m_" + n], given["v_" + n]) for n in WEIGHTS}
    loss = lax.psum(loss, AXES)
    return (loss, grad_x[None], *[outs[n][0] for n in WEIGHTS], *[outs[n][1] for n in WEIGHTS],
            *[outs[n][2] for n in WEIGHTS], *[outs[n][3] for n in WEIGHTS])
```

```python
import functools

import jax
import jax.numpy as jnp
from jax import lax
from jax.experimental import pallas as pl
from jax.experimental.pallas import tpu as pltpu

f32 = jnp.float32
bf16 = jnp.bfloat16

D_MODEL = 1024
MIX = 512
N_HEADS = 8
QK_NOPE, QK_ROPE, V_HEAD = 64, 32, 64
Q_LORA, KV_LORA = 384, 256
ROPE_THETA = 10000.0
POOL_WINDOWS = (2, 4, 8, 16)
SSD_CHUNK = 128
SSD_XBC = 768
CONV_W = 4
LRU_C = 8.0
D_FF = 4096
EPS = 1e-6
IN_COLS = 7592
ADAM_LR, ADAM_B1, ADAM_B2, ADAM_EPS, ADAM_WD, ADAM_STEP = 0.001, 0.9, 0.999, 1e-08, 0.01, 10

LANE = 128
HALO = 8
POOL_HALO = 16
VMEM_LIMIT = 56 * 1024 * 1024
N_DEV = 8
AXES = ("x", "y", "c")

U_COLS = 8192
U_GATES, U_POOL, U_Z, U_LG, U_LX, U_XBC, U_CQ, U_KR, U_CKV, U_DT = (
    (0, 4096), (4096, 512), (4608, 512), (5120, 512), (5632, 512), (6144, 768),
    (6912, 384), (7296, 128), (7424, 256), (7680, 128))
KR_LANE = 64


def _cp(sem):
    return pltpu.CompilerParams(dimension_semantics=sem, vmem_limit_bytes=VMEM_LIMIT)


def _pick(dim, pref):
    if dim <= pref:
        return dim
    t = pref
    while t >= LANE:
        if dim % t == 0:
            return t
        t -= LANE
    t = pref
    while dim % t:
        t -= 8
    return t


@functools.partial(jax.custom_vjp, nondiff_argnums=(1,))
def shift_down(x, k):
    row = lax.broadcasted_iota(jnp.int32, x.shape, 0)
    return jnp.where(row >= k, pltpu.roll(x, k, 0), 0.0)


def _shift_down_fwd(x, k):
    return shift_down(x, k), None


def _shift_down_bwd(k, _, g):
    r = g.shape[0]
    row = lax.broadcasted_iota(jnp.int32, g.shape, 0)
    return (jnp.where(row < r - k, pltpu.roll(g, r - k, 0), 0.0),)


shift_down.defvjp(_shift_down_fwd, _shift_down_bwd)


@functools.partial(jax.custom_vjp, nondiff_argnums=(1,))
def lane_roll(x, s):
    return pltpu.roll(x, s, 1)


def _lane_roll_fwd(x, s):
    return lane_roll(x, s), None


def _lane_roll_bwd(s, _, g):
    return (pltpu.roll(g, (g.shape[1] - s) % g.shape[1], 1),)


lane_roll.defvjp(_lane_roll_fwd, _lane_roll_bwd)


def _tile_spec(tm, width, cb, n=None):
    if n is None:
        return pl.BlockSpec((tm, width), lambda i: (i, cb))
    return pl.BlockSpec((tm, width), lambda i: (n - 1 - i, cb))


def _const_spec(shape):
    nd = len(shape)
    return pl.BlockSpec(shape, lambda i: (0,) * nd)


def seq_fwd(name, f, params, tiles, carries, outs, tm):
    rows = tiles[0][0].shape[0]
    n = rows // tm
    np_, nt, no, nc = len(params), len(tiles), len(outs), len(carries)

    def body(*refs):
        p_refs = refs[:np_]
        t_refs = refs[np_:np_ + nt]
        o_refs = refs[np_ + nt:np_ + nt + no]
        s_refs = refs[np_ + nt + no:np_ + nt + no + nc]
        c_refs = refs[np_ + nt + no + nc:]
        i = pl.program_id(0)

        @pl.when(i == 0)
        def _():
            for c in c_refs:
                c[...] = jnp.zeros_like(c)

        cvals = [c[...] for c in c_refs]
        for s, c in zip(s_refs, cvals):
            s[0] = c
        o, newc = f(i, [r[...] for r in p_refs], cvals, [r[...].astype(f32) for r in t_refs])
        for r, v in zip(o_refs, o):
            r[...] = v.astype(r.dtype)
        for r, v in zip(c_refs, newc):
            r[...] = v

    in_specs = [_const_spec(p.shape) for p in params] + [_tile_spec(tm, w, cb) for (_, w, cb) in tiles]
    out_specs = [_tile_spec(tm, w, 0) for (w, _) in outs]
    out_specs += [pl.BlockSpec((1,) + tuple(c), lambda i, nd=len(c): (i,) + (0,) * nd) for c in carries]
    out_shape = [jax.ShapeDtypeStruct((rows, w), dt) for (w, dt) in outs]
    out_shape += [jax.ShapeDtypeStruct((n,) + tuple(c), f32) for c in carries]
    res = pl.pallas_call(
        body, name=name, grid=(n,), in_specs=in_specs, out_specs=out_specs, out_shape=out_shape,
        scratch_shapes=[pltpu.VMEM(tuple(c), f32) for c in carries],
        compiler_params=_cp(("arbitrary",)),
    )(*params, *[t[0] for t in tiles])
    return list(res[:no]), list(res[no:])


def seq_bwd(name, f, params, tiles, diff, saved, douts, gdtypes, tm, add_to=None):
    rows = tiles[0][0].shape[0]
    n = rows // tm
    np_, nt, nc, nd = len(params), len(tiles), len(saved), len(douts)
    didx = [k for k, d in enumerate(diff) if d]
    ng = len(didx)
    has_add = add_to is not None

    def body(*refs):
        p_refs = refs[:np_]
        t_refs = refs[np_:np_ + nt]
        s_refs = refs[np_ + nt:np_ + nt + nc]
        d_refs = refs[np_ + nt + nc:np_ + nt + nc + nd]
        pos = np_ + nt + nc + nd
        a_ref = refs[pos] if has_add else None
        pos += 1 if has_add else 0
        dp_refs = refs[pos:pos + np_]
        dt_refs = refs[pos + np_:pos + np_ + ng]
        dc_refs = refs[pos + np_ + ng:]
        i = pl.program_id(0)
        step = n - 1 - i

        @pl.when(i == 0)
        def _():
            for r in dp_refs:
                r[...] = jnp.zeros_like(r)
            for r in dc_refs:
                r[...] = jnp.zeros_like(r)

        pvals = [r[...] for r in p_refs]
        cvals = [r[0] for r in s_refs]
        xvals = [r[...].astype(f32) for r in t_refs]

        def fn(p, c, xd):
            x = list(xvals)
            for k, v in zip(didx, xd):
                x[k] = v
            return f(step, p, c, x)

        _, vjp = jax.vjp(fn, pvals, cvals, [xvals[k] for k in didx])
        dp, dc, dx = vjp(([r[...].astype(f32) for r in d_refs], [r[...] for r in dc_refs]))
        for r, v in zip(dp_refs, dp):
            r[...] += v
        for r, v in zip(dc_refs, dc):
            r[...] = v
        for k, (r, v) in enumerate(zip(dt_refs, dx)):
            if has_add and k == add_to[0]:
                v = v + a_ref[...].astype(f32)
            r[...] = v.astype(r.dtype)

    in_specs = [_const_spec(p.shape) for p in params] + [_tile_spec(tm, w, cb, n) for (_, w, cb) in tiles]
    in_specs += [pl.BlockSpec((1,) + tuple(s.shape[1:]), lambda i, nd_=s.ndim - 1: (n - 1 - i,) + (0,) * nd_) for s in saved]
    in_specs += [_tile_spec(tm, d.shape[1], 0, n) for d in douts]
    args = list(params) + [t[0] for t in tiles] + list(saved) + list(douts)
    if has_add:
        in_specs.append(_tile_spec(tm, add_to[1].shape[1], 0, n))
        args.append(add_to[1])
    out_specs = [_const_spec(p.shape) for p in params] + [_tile_spec(tm, tiles[k][1], 0, n) for k in didx]
    out_shape = [jax.ShapeDtypeStruct(p.shape, f32) for p in params]
    out_shape += [jax.ShapeDtypeStruct((rows, tiles[k][1]), dt) for k, dt in zip(didx, gdtypes)]
    res = pl.pallas_call(
        body, name=name, grid=(n,), in_specs=in_specs, out_specs=out_specs, out_shape=out_shape,
        scratch_shapes=[pltpu.VMEM(tuple(s.shape[1:]), f32) for s in saved],
        compiler_params=_cp(("arbitrary",)),
    )(*args)
    return list(res[:np_]), list(res[np_:])


def matmul(name, a, b, *, ta=False, tb=False, outs=(f32,), epi=None, extras=(), deps=(), tm=512, tn=1024, tk=1024):
    m, k = (a.shape[1], a.shape[0]) if ta else a.shape
    n = b.shape[0] if tb else b.shape[1]
    tm, tn, tk = _pick(m, tm), _pick(n, tn), _pick(k, tk)
    nk = k // tk
    ne = len(extras)
    dims = (((0 if ta else 1,), (1 if tb else 0,)), ((), ()))

    def body(*refs):
        a_ref, b_ref = refs[0], refs[1]
        e_refs = refs[2:2 + ne]
        o_refs = refs[2 + ne + len(deps):-1]
        acc = refs[-1]
        kk = pl.program_id(2)

        @pl.when(kk == 0)
        def _():
            acc[...] = jnp.zeros_like(acc)

        acc[...] += lax.dot_general(a_ref[...].astype(bf16), b_ref[...].astype(bf16), dims, preferred_element_type=f32)

        @pl.when(kk == nk - 1)
        def _():
            res = (acc[...],) if epi is None else epi(acc[...], *[e[...] for e in e_refs])
            for r, v in zip(o_refs, res):
                r[...] = v.astype(r.dtype)

    a_spec = pl.BlockSpec((tk, tm), lambda i, j, q: (q, i)) if ta else pl.BlockSpec((tm, tk), lambda i, j, q: (i, q))
    b_spec = pl.BlockSpec((tn, tk), lambda i, j, q: (j, q)) if tb else pl.BlockSpec((tk, tn), lambda i, j, q: (q, j))
    assert all(off % tn == 0 for (_, off) in extras)
    e_specs = [pl.BlockSpec((tm, tn), lambda i, j, q, off=off // tn: (i, off + j)) for (_, off) in extras]
    res = pl.pallas_call(
        body, name=name, grid=(m // tm, n // tn, nk),
        in_specs=[a_spec, b_spec] + e_specs + [pl.BlockSpec(memory_space=pl.ANY) for _ in deps],
        out_specs=[pl.BlockSpec((tm, tn), lambda i, j, q: (i, j)) for _ in outs],
        out_shape=[jax.ShapeDtypeStruct((m, n), dt) for dt in outs],
        scratch_shapes=[pltpu.VMEM((tm, tn), f32)],
        compiler_params=_cp(("parallel", "parallel", "arbitrary")),
    )(a, b, *[e[0] for e in extras], *deps)
    return res[0] if len(outs) == 1 else tuple(res)


ATT_SCALE = (QK_NOPE + QK_ROPE) ** -0.5
NT = (((1,), (1,)), ((), ()))
TN = (((0,), (0,)), ((), ()))


def _causal(tq, tk):
    return lax.broadcasted_iota(jnp.int32, (tq, tk), 0) >= lax.broadcasted_iota(jnp.int32, (tq, tk), 1)


def attn_fwd(q, k, v, t):
    rows = q.shape[0]
    n = rows // t

    def body(q_ref, k_ref, v_ref, o_ref, lse_ref, m_s, l_s, acc_s):
        i, j = pl.program_id(1), pl.program_id(2)

        @pl.when(j == 0)
        def _():
            m_s[...] = jnp.full_like(m_s, -jnp.inf)
            l_s[...] = jnp.zeros_like(l_s)
            acc_s[...] = jnp.zeros_like(acc_s)

        @pl.when(j <= i)
        def _():
            s = lax.dot_general(q_ref[...], k_ref[...], NT, preferred_element_type=f32) * ATT_SCALE
            s = jnp.where(jnp.logical_or(j < i, _causal(t, t)), s, -jnp.inf)
            m_prev = m_s[...]
            m_new = jnp.maximum(m_prev, jnp.max(s, axis=1, keepdims=True))
            alpha = jnp.exp(m_prev - m_new)
            p = jnp.exp(s - m_new[:, :1])
            l_s[...] = alpha * l_s[...] + jnp.sum(p, axis=1, keepdims=True)
            acc_s[...] = alpha * acc_s[...] + jnp.dot(p.astype(bf16), v_ref[...], preferred_element_type=f32)
            m_s[...] = m_new

        @pl.when(j == n - 1)
        def _():
            o_ref[...] = (acc_s[...] / l_s[...]).astype(o_ref.dtype)
            lse_ref[...] = m_s[...] + jnp.log(l_s[...])

    qs = pl.BlockSpec((t, LANE), lambda h, i, j: (i, h))
    ks = pl.BlockSpec((t, LANE), lambda h, i, j: (jnp.minimum(j, i), h))
    return pl.pallas_call(
        body, name="attn_fwd", grid=(N_HEADS, n, n), in_specs=[qs, ks, ks], out_specs=[qs, qs],
        out_shape=[jax.ShapeDtypeStruct((rows, N_HEADS * LANE), bf16), jax.ShapeDtypeStruct((rows, N_HEADS * LANE), f32)],
        scratch_shapes=[pltpu.VMEM((t, LANE), f32)] * 3,
        compiler_params=_cp(("parallel", "parallel", "arbitrary")),
    )(q, k, v)


def _attn_p_ds(q, k, v, do, o, lse, diag):
    s = lax.dot_general(q, k, NT, preferred_element_type=f32) * ATT_SCALE
    p = jnp.exp(s - lse[:, :1])
    p = jnp.where(jnp.logical_or(jnp.logical_not(diag), _causal(*s.shape)), p, 0.0)
    dp = lax.dot_general(do, v, NT, preferred_element_type=f32)
    delta = jnp.sum(do.astype(f32) * o.astype(f32), axis=1, keepdims=True)
    ds = p * (dp - delta) * ATT_SCALE
    return p, ds


def attn_bwd_dq(q, k, v, do, o, lse, t):
    rows = q.shape[0]
    n = rows // t

    def body(q_ref, k_ref, v_ref, do_ref, o_ref, lse_ref, dq_ref, acc_s):
        i, j = pl.program_id(1), pl.program_id(2)

        @pl.when(j == 0)
        def _():
            acc_s[...] = jnp.zeros_like(acc_s)

        @pl.when(j <= i)
        def _():
            _, ds = _attn_p_ds(q_ref[...], k_ref[...], v_ref[...], do_ref[...], o_ref[...], lse_ref[...], j == i)
            acc_s[...] += jnp.dot(ds.astype(bf16), k_ref[...], preferred_element_type=f32)

        @pl.when(j == n - 1)
        def _():
            dq_ref[...] = acc_s[...]

    qs = pl.BlockSpec((t, LANE), lambda h, i, j: (i, h))
    ks = pl.BlockSpec((t, LANE), lambda h, i, j: (jnp.minimum(j, i), h))
    return pl.pallas_call(
        body, name="attn_bwd_dq", grid=(N_HEADS, n, n), in_specs=[qs, ks, ks, qs, qs, qs], out_specs=qs,
        out_shape=jax.ShapeDtypeStruct((rows, N_HEADS * LANE), f32),
        scratch_shapes=[pltpu.VMEM((t, LANE), f32)],
        compiler_params=_cp(("parallel", "parallel", "arbitrary")),
    )(q, k, v, do, o, lse)


def attn_bwd_dkv(q, k, v, do, o, lse, t):
    rows = q.shape[0]
    n = rows // t

    def body(q_ref, k_ref, v_ref, do_ref, o_ref, lse_ref, dk_ref, dv_ref, dk_s, dv_s):
        j, i = pl.program_id(1), pl.program_id(2)

        @pl.when(i == 0)
        def _():
            dk_s[...] = jnp.zeros_like(dk_s)
            dv_s[...] = jnp.zeros_like(dv_s)

        @pl.when(i >= j)
        def _():
            p, ds = _attn_p_ds(q_ref[...], k_ref[...], v_ref[...], do_ref[...], o_ref[...], lse_ref[...], j == i)
            dv_s[...] += lax.dot_general(p.astype(bf16), do_ref[...], TN, preferred_element_type=f32)
            dk_s[...] += lax.dot_general(ds.astype(bf16), q_ref[...], TN, preferred_element_type=f32)

        @pl.when(i == n - 1)
        def _():
            dk_ref[...] = dk_s[...]
            dv_ref[...] = dv_s[...]

    qs = pl.BlockSpec((t, LANE), lambda h, j, i: (jnp.maximum(i, j), h))
    ks = pl.BlockSpec((t, LANE), lambda h, j, i: (j, h))
    return pl.pallas_call(
        body, name="attn_bwd_dkv", grid=(N_HEADS, n, n), in_specs=[qs, ks, ks, qs, qs, qs], out_specs=[ks, ks],
        out_shape=[jax.ShapeDtypeStruct((rows, N_HEADS * LANE), f32)] * 2,
        scratch_shapes=[pltpu.VMEM((t, LANE), f32)] * 2,
        compiler_params=_cp(("parallel", "parallel", "arbitrary")),
    )(q, k, v, do, o, lse)


def _steps(tm):
    k, out = 1, []
    while k < tm:
        out.append(k)
        k *= 2
    return out


def scan_fwd(a, u, tm):
    rows, ch = a.shape
    n = rows // tm

    def body(a_ref, u_ref, h_ref, h_s):
        @pl.when(pl.program_id(0) == 0)
        def _():
            h_s[...] = jnp.zeros_like(h_s)

        av, bv = a_ref[...], u_ref[...]
        row = lax.broadcasted_iota(jnp.int32, av.shape, 0)
        for k in _steps(tm):
            a_sh = jnp.where(row >= k, pltpu.roll(av, k, 0), 1.0)
            b_sh = jnp.where(row >= k, pltpu.roll(bv, k, 0), 0.0)
            bv = av * b_sh + bv
            av = av * a_sh
        h = bv + av * h_s[HALO - 1:HALO, :]
        h_ref[...] = h
        h_s[...] = h[tm - HALO:, :]

    spec = pl.BlockSpec((tm, ch), lambda i: (i, 0))
    return pl.pallas_call(
        body, name="lru_scan_fwd", grid=(n,), in_specs=[spec, spec], out_specs=spec,
        out_shape=jax.ShapeDtypeStruct((rows, ch), f32), scratch_shapes=[pltpu.VMEM((HALO, ch), f32)],
        compiler_params=_cp(("arbitrary",)),
    )(a, u)


def scan_bwd(a, h, dh, tm):
    rows, ch = a.shape
    n = rows // tm
    per = tm // HALO

    def body(a_ref, h_ref, hp_ref, dh_ref, da_ref, du_ref, g_s, a_s):
        i = pl.program_id(0)
        step = n - 1 - i

        @pl.when(i == 0)
        def _():
            g_s[...] = jnp.zeros_like(g_s)
            a_s[...] = jnp.zeros_like(a_s)

        a0 = a_ref[...]
        row = lax.broadcasted_iota(jnp.int32, a0.shape, 0)
        av = jnp.where(row < tm - 1, pltpu.roll(a0, tm - 1, 0), a_s[0:1, :])
        bv = dh_ref[...]
        for k in _steps(tm):
            a_sh = jnp.where(row < tm - k, pltpu.roll(av, tm - k, 0), 1.0)
            b_sh = jnp.where(row < tm - k, pltpu.roll(bv, tm - k, 0), 0.0)
            bv = bv + av * b_sh
            av = av * a_sh
        g = bv + av * g_s[0:1, :]
        h_last = jnp.where(step > 0, hp_ref[HALO - 1:HALO, :], 0.0)
        h_prev = jnp.where(row >= 1, pltpu.roll(h_ref[...], 1, 0), h_last)
        du_ref[...] = g
        da_ref[...] = g * h_prev
        g_s[...] = g[0:HALO, :]
        a_s[...] = a0[0:HALO, :]

    spec = pl.BlockSpec((tm, ch), lambda i: (n - 1 - i, 0))
    hp_spec = pl.BlockSpec((HALO, ch), lambda i: (jnp.maximum((n - 1 - i) * per - 1, 0), 0))
    return pl.pallas_call(
        body, name="lru_scan_bwd", grid=(n,), in_specs=[spec, spec, hp_spec, spec], out_specs=[spec, spec],
        out_shape=[jax.ShapeDtypeStruct((rows, ch), f32)] * 2,
        scratch_shapes=[pltpu.VMEM((HALO, ch), f32)] * 2,
        compiler_params=_cp(("arbitrary",)),
    )(a, h, h, dh)


def _rms(x, g):
    return x * lax.rsqrt(jnp.mean(x * x, axis=-1, keepdims=True) + EPS) * g


def f_rms(step, p, c, x):
    return [_rms(x[0], p[0])], []


def _rope(x, cosf, sinf):
    lane = lax.broadcasted_iota(jnp.int32, x.shape, 1)
    sw = jnp.where(lane < KR_LANE + QK_ROPE // 2, lane_roll(x, LANE - QK_ROPE // 2), lane_roll(x, QK_ROPE // 2))
    return x * cosf + sw * sinf


def f_prep(step, p, c, x):
    q, kn, kr, cosf, sinf = x
    kr_rot = _rope(kr, cosf, sinf)
    qr = [_rope(q[:, LANE * h:LANE * (h + 1)], cosf, sinf) for h in range(N_HEADS)]
    kk = [kn[:, LANE * h:LANE * (h + 1)] + kr_rot for h in range(N_HEADS)]
    return [jnp.concatenate(qr, axis=1), jnp.concatenate(kk, axis=1)], []


def _conv(tail, x, w, b):
    xf = jnp.concatenate([tail, x], axis=0)
    acc = b + w[CONV_W - 1:CONV_W, :] * xf
    for k in range(CONV_W - 1):
        acc = acc + w[k:k + 1, :] * shift_down(xf, CONV_W - 1 - k)
    return acc[HALO:, :]


def f_pool(step, p, c, x):
    wp, sc = p
    (tail,) = c
    (u,) = x
    tm = u.shape[0]
    xf = jnp.concatenate([tail, u], axis=0)
    sums, s, w = [], xf, 1
    while w < POOL_WINDOWS[-1]:
        s = s + shift_down(s, w)
        w *= 2
        sums.append(s)
    t = step * tm + lax.broadcasted_iota(jnp.int32, (tm, 1), 0)
    ys = []
    for g, (w, s) in enumerate(zip(POOL_WINDOWS, sums)):
        sl = slice(LANE * g, LANE * (g + 1))
        cnt = jnp.minimum(t + 1, w).astype(f32)
        d = s[POOL_HALO:, sl] / cnt - u[:, sl]
        ys.append(jnp.dot(d.astype(bf16), wp[LANE * g:LANE * (g + 1), :].astype(bf16), preferred_element_type=f32))
    return [jnp.concatenate(ys, axis=1) * sc], [u[tm - POOL_HALO:, :]]


def _expand_heads(v):
    hrow = lax.broadcasted_iota(jnp.int32, (LANE, MIX), 0)
    col = lax.broadcasted_iota(jnp.int32, (LANE, MIX), 1)
    e = jnp.logical_and(col >= hrow * 64, col < hrow * 64 + 64).astype(f32)
    return jnp.dot(v, e, precision=lax.Precision.HIGHEST, preferred_element_type=f32)


def f_ssd(step, p, c, x):
    conv_w, conv_b, dtb, alog, dsk, ng = p
    tail, s_in = c[0], c[1:]
    z, xbc, dt = x
    ln = z.shape[0]
    xc = jax.nn.silu(_conv(tail, xbc, conv_w, conv_b))
    xs, bb, cc = xc[:, :MIX], xc[:, MIX:MIX + LANE], xc[:, MIX + LANE:]
    dtv = jax.nn.softplus(dt + dtb[0:1, :])
    a_head = -jnp.exp(alog)
    a = dtv * a_head[0:1, :]
    ri = lax.broadcasted_iota(jnp.int32, (ln, ln), 0)
    ci = lax.broadcasted_iota(jnp.int32, (ln, ln), 1)
    tril = (ri >= ci).astype(f32)
    triu = (ri <= ci).astype(f32)
    hi = lax.Precision.HIGHEST
    a_cs = jnp.dot(tril, a, precision=hi, preferred_element_type=f32)
    a_cs_t = lax.dot_general(a, triu, TN, precision=hi, preferred_element_type=f32)
    dt_e = _expand_heads(dtv)
    a_e = dt_e * _expand_heads(a_head)[0:1, :]
    a_cs_e = jnp.dot(tril, a_e, precision=hi, preferred_element_type=f32)
    tot_e = jnp.sum(a_e, axis=0, keepdims=True)
    xdt = xs * dt_e
    lane = lax.broadcasted_iota(jnp.int32, (1, LANE), 1)
    half = [(lane < 64).astype(f32), (lane >= 64).astype(f32)]
    hrow = lax.broadcasted_iota(jnp.int32, (LANE, 1), 0)
    cg = [(cc * half[g]).astype(bf16) for g in range(2)]
    bg = [(bb * half[g]).astype(bf16) for g in range(2)]
    cb = [lax.dot_general(cg[g], bg[g], NT, preferred_element_type=f32) for g in range(2)]
    ys, s_out = [], []
    for j in range(4):
        g = j // 2
        blk = slice(LANE * j, LANE * (j + 1))
        xj = xdt[:, blk]
        yj = jnp.zeros((ln, LANE), f32)
        for hh in range(2):
            h = 2 * j + hh
            col = jnp.sum(a_cs * (lane == h).astype(f32), axis=1, keepdims=True)
            rowv = jnp.sum(a_cs_t * (hrow == h).astype(f32), axis=0, keepdims=True)
            lmat = jnp.exp(jnp.where(ri >= ci, col - rowv, -jnp.inf))
            yj = yj + jnp.dot((cb[g] * lmat).astype(bf16), (xj * half[hh]).astype(bf16), preferred_element_type=f32)
        acs = a_cs_e[:, blk]
        tot = tot_e[:, blk]
        yj = yj + jnp.exp(acs) * jnp.dot(cg[g], s_in[j].astype(bf16), preferred_element_type=f32)
        s_new = jnp.exp(tot) * s_in[j] + lax.dot_general(bg[g], (xj * jnp.exp(tot - acs)).astype(bf16), TN,
                                                         preferred_element_type=f32)
        ys.append(yj)
        s_out.append(s_new)
    y = jnp.concatenate(ys, axis=1) + xs * _expand_heads(dsk)[0:1, :]
    y = y * jax.nn.silu(z)
    return [_rms(y, ng)], [xbc[ln - HALO:, :]] + s_out


def _neg_expm1(y):
    series = -y * (1.0 + y * (0.5 + y * (1.0 / 6 + y * (1.0 / 24 + y * (1.0 / 120)))))
    return jnp.where(y > -0.05, series, 1.0 - jnp.exp(y))


def f_lru_pre(step, p, c, x):
    cw, cb_, wa, ba, wi, bi, lam = p
    (tail,) = c
    (lx,) = x
    tm = lx.shape[0]
    xc = _conv(tail, lx, cw, cb_)
    xb = xc.astype(bf16)
    r = jax.nn.sigmoid(jnp.dot(xb, wa.astype(bf16), preferred_element_type=f32) + ba)
    it = jax.nn.sigmoid(jnp.dot(xb, wi.astype(bf16), preferred_element_type=f32) + bi)
    log_a = -LRU_C * r * jax.nn.softplus(-lam)
    mult = jnp.sqrt(_neg_expm1(2.0 * log_a))
    return [jnp.exp(log_a), xc * it * mult], [lx[tm - HALO:, :]]


def f_lru_post(step, p, c, x):
    h, g = x
    return [h * jax.nn.gelu(g)], []


def loss_head(x, tgt, g, tm):
    rows, d = x.shape
    n = rows // tm

    def body(x_ref, t_ref, g_ref, loss_ref, dx_ref, dg_ref):
        @pl.when(pl.program_id(0) == 0)
        def _():
            loss_ref[...] = jnp.zeros_like(loss_ref)
            dg_ref[...] = jnp.zeros_like(dg_ref)

        def fn(gv, xv):
            err = _rms(xv, gv) - t_ref[...]
            return 0.5 * jnp.sum(jnp.mean(err * err, axis=-1, keepdims=True))

        val, (dg, dx) = jax.value_and_grad(fn, argnums=(0, 1))(g_ref[...], x_ref[...])
        loss_ref[...] += val
        dg_ref[...] += dg
        dx_ref[...] = dx

    spec = pl.BlockSpec((tm, d), lambda i: (i, 0))
    return pl.pallas_call(
        body, name="loss_head", grid=(n,), in_specs=[spec, spec, _const_spec((1, d))],
        out_specs=[_const_spec((8, LANE)), spec, _const_spec((1, d))],
        out_shape=[jax.ShapeDtypeStruct((8, LANE), f32), jax.ShapeDtypeStruct((rows, d), f32),
                   jax.ShapeDtypeStruct((1, d), f32)],
        compiler_params=_cp(("arbitrary",)),
    )(x, tgt, g)


def ew(name, fn, ins, outs, tm):
    rows = ins[0][0].shape[0]
    ni = len(ins)

    def body(*refs):
        res = fn(*[r[...].astype(f32) for r in refs[:ni]])
        for r, v in zip(refs[ni:], res):
            r[...] = v.astype(r.dtype)

    return pl.pallas_call(
        body, name=name, grid=(rows // tm,), in_specs=[_tile_spec(tm, w, cb) for (_, w, cb) in ins],
        out_specs=[_tile_spec(tm, w, 0) for (w, _) in outs],
        out_shape=[jax.ShapeDtypeStruct((rows, w), dt) for (w, dt) in outs],
        compiler_params=_cp(("parallel",)),
    )(*[t[0] for t in ins])


def _peers():
    x, y, c = lax.axis_index("x"), lax.axis_index("y"), lax.axis_index("c")
    me = 4 * x + 2 * y + c
    out = []
    for k in range(1, N_DEV):
        px = 1 - x if k & 4 else x
        py = 1 - y if k & 2 else y
        pc = 1 - c if k & 1 else c
        out.append(((px, py, pc), 4 * px + 2 * py + pc))
    return me, out


_HBM = pl.BlockSpec(memory_space=pltpu.HBM)
_SEM = pl.BlockSpec(memory_space=pltpu.SEMAPHORE)
_EFFECT = pltpu.SideEffectType.DATAFLOW_SIDE_EFFECTING


def _remote(src_ref, land_ref, gather, me, pid, dev, send_sems, recv_sems, k, recv_side):
    return pltpu.make_async_remote_copy(
        src_ref=src_ref if gather else src_ref.at[pid], dst_ref=land_ref.at[pid if recv_side else me],
        send_sem=send_sems.at[k], recv_sem=recv_sems.at[k], device_id=dev, device_id_type=pl.DeviceIdType.MESH)


def exchange_start(name, srcs, gather, deps=()):
    n, nd = len(srcs), len(deps)
    shapes = [(s.shape if gather else s.shape[1:]) for s in srcs]
    lands = [lax.empty((N_DEV,) + tuple(sh), s.dtype) for s, sh in zip(srcs, shapes)]

    def body(*refs):
        src_refs, land_refs = refs[:n], refs[n:2 * n]
        send_sems, recv_sems = refs[2 * n + nd], refs[2 * n + nd + 1]
        token = refs[-1]
        me, peers = _peers()
        for k, (dev, pid) in enumerate(peers):
            for s_ref, l_ref in zip(src_refs, land_refs):
                _remote(s_ref, l_ref, gather, me, pid, dev, send_sems, recv_sems, k, False).start()
        token[...] = jnp.zeros_like(token)

    hbm = lambda a: pltpu.with_memory_space_constraint(a, pltpu.HBM)
    res = pl.pallas_call(
        body, name=name,
        out_shape=(pltpu.SemaphoreType.DMA((N_DEV - 1,)), pltpu.SemaphoreType.DMA((N_DEV - 1,)),
                   *[pltpu.HBM(a.shape, a.dtype) for a in list(srcs) + lands], jax.ShapeDtypeStruct((8, LANE), f32)),
        in_specs=[_HBM] * (2 * n) + [pl.BlockSpec(memory_space=pl.ANY)] * nd,
        out_specs=(_SEM, _SEM, *([_HBM] * (2 * n)), pl.BlockSpec(memory_space=pltpu.VMEM)),
        input_output_aliases={i: 2 + i for i in range(2 * n)},
        compiler_params=pltpu.CompilerParams(has_side_effects=_EFFECT),
    )(*[hbm(a) for a in list(srcs) + lands], *deps)
    return dict(sems=res[:2], srcs=list(res[2:2 + n]), lands=list(res[2 + n:2 + 2 * n]), token=res[-1], gather=gather)


def exchange_wait(name, h, afters):
    n, gather = len(h["srcs"]), h["gather"]

    def body(*refs):
        src_refs, land_refs = refs[:n], refs[n:2 * n]
        send_sems, recv_sems = refs[2 * n], refs[2 * n + 1]
        me, peers = _peers()
        for k, (dev, pid) in enumerate(peers):
            for s_ref, l_ref in zip(src_refs, land_refs):
                _remote(s_ref, l_ref, gather, me, pid, dev, send_sems, recv_sems, k, True).wait_recv()
        for k, (dev, pid) in enumerate(peers):
            for s_ref, l_ref in zip(src_refs, land_refs):
                _remote(s_ref, l_ref, gather, me, pid, dev, send_sems, recv_sems, k, False).wait_send()

    arrs = h["srcs"] + h["lands"]
    res = pl.pallas_call(
        body, name=name, out_shape=tuple(pltpu.HBM(a.shape, a.dtype) for a in arrs),
        in_specs=[_HBM] * (2 * n) + [_SEM, _SEM] + [pl.BlockSpec(memory_space=pl.ANY)] * len(afters),
        out_specs=tuple([_HBM] * (2 * n)), input_output_aliases={i: i for i in range(2 * n)},
        compiler_params=pltpu.CompilerParams(has_side_effects=_EFFECT),
    )(*arrs, *h["sems"], *afters)
    me = 4 * lax.axis_index("x") + 2 * lax.axis_index("y") + lax.axis_index("c")
    out = []
    for src, land in zip(res[:n], res[n:]):
        mine = src if gather else lax.dynamic_index_in_dim(src, me, 0, keepdims=False)
        out.append(lax.dynamic_update_index_in_dim(land, mine, me, 0))
    return out


def adamw(name, parts, w, m, v):
    nl = len(parts)
    shape = w.shape[1:]
    c = shape[-1]
    r = 1
    for s in shape[:-1]:
        r *= s
    tr = _pick(r, 256) if r % 8 == 0 else r
    nb = r // tr
    parts2 = [p.reshape(N_DEV, r, c) for p in parts]
    w2, m2, v2 = (a.reshape(nl, r, c) for a in (w, m, v))

    def body(*refs):
        p_refs = refs[:nl]
        w_ref, m_ref, v_ref, g_ref, d_ref, nm_ref, nv_ref = refs[nl:]
        layer = pl.program_id(0)
        for ll, p_ref in enumerate(p_refs):
            @pl.when(layer == ll)
            def _(p_ref=p_ref):
                g = p_ref[0].astype(f32)
                for i in range(1, N_DEV):
                    g = g + p_ref[i].astype(f32)
                mn = ADAM_B1 * m_ref[0] + (1.0 - ADAM_B1) * g
                vn = ADAM_B2 * v_ref[0] + (1.0 - ADAM_B2) * jnp.square(g)
                m_hat = mn / (1.0 - ADAM_B1 ** ADAM_STEP)
                v_hat = vn / (1.0 - ADAM_B2 ** ADAM_STEP)
                g_ref[0] = g
                d_ref[0] = -ADAM_LR * (m_hat / (jnp.sqrt(v_hat) + ADAM_EPS) + ADAM_WD * w_ref[0])
                nm_ref[0] = mn
                nv_ref[0] = vn

    def p_spec(ll):
        return pl.BlockSpec((N_DEV, tr, c), lambda l, i: (0, jnp.where(l == ll, i, jnp.where(l > ll, nb - 1, 0)), 0))

    spec = pl.BlockSpec((1, tr, c), lambda l, i: (l, i, 0))
    res = pl.pallas_call(
        body, name=name, grid=(nl, nb), in_specs=[p_spec(ll) for ll in range(nl)] + [spec, spec, spec],
        out_specs=[spec] * 4, out_shape=[jax.ShapeDtypeStruct((nl, r, c), f32)] * 4,
        compiler_params=_cp(("arbitrary", "arbitrary")),
    )(*parts2, w2, m2, v2)
    return [a.reshape(w.shape) for a in res]


_IN_SPLITS = dict(cq=(0, 384), ckv=(384, 640), kr=(640, 672), pool=(672, 1184), z=(1184, 1696), xbc=(1696, 2464),
                  dt=(2464, 2472), lg=(2472, 2984), lx=(2984, 3496), gates=(3496, 7592))


def _pad_w_in(w):
    s = lambda n: w[:, _IN_SPLITS[n][0]:_IN_SPLITS[n][1]]
    z = lambda n: jnp.zeros((w.shape[0], n), w.dtype)
    return jnp.concatenate([s("gates"), s("pool"), s("z"), s("lg"), s("lx"), s("xbc"), s("cq"), z(KR_LANE), s("kr"),
                            z(LANE - KR_LANE - QK_ROPE), s("ckv"), s("dt"), z(LANE - 8), z(U_COLS - U_DT[0] - LANE)], axis=1)


def _unpad_w_in(g):
    c = lambda o, n: g[:, o:o + n]
    return jnp.concatenate([c(U_CQ[0], 384), c(U_CKV[0], 256), c(U_KR[0] + KR_LANE, QK_ROPE), c(U_POOL[0], 512),
                            c(U_Z[0], 512), c(U_XBC[0], 768), c(U_DT[0], 8), c(U_LG[0], 512), c(U_LX[0], 512),
                            c(0, 4096)], axis=1)


def _head_pad_cols(w, per, lo, hi):
    k = w.shape[0]
    w = w.reshape(k, N_HEADS, per)[:, :, lo:hi]
    return jnp.pad(w, ((0, 0), (0, 0), (0, LANE - (hi - lo)))).reshape(k, N_HEADS * LANE)


def _head_unpad_cols(g, n):
    k = g.shape[0]
    return g.reshape(k, N_HEADS, LANE)[:, :, :n]


def _block_diag(w):
    out = jnp.zeros((MIX, MIX), w.dtype)
    for i in range(8):
        out = lax.dynamic_update_slice(out, w[i], (64 * i, 64 * i))
    return out


def _block_diag_inv(g):
    return jnp.stack([g[64 * i:64 * (i + 1), 64 * i:64 * (i + 1)] for i in range(8)])


def _head8(v):
    return jnp.zeros((8, LANE), f32).at[0, :8].set(v)


GROUPS = dict(A=("w_in",), B=("w_uq", "w_ukv", "ssd_conv_w", "lru_conv_w", "w_branch", "w_out"),
              C=("w_ff1", "w_ff2", "w_ple_gate", "w_ple"))


def _kernel_weights(grp, fw):
    if grp == "A":
        return dict(w_in=_pad_w_in(fw["w_in"]))
    if grp == "C":
        return dict(w_ff1=fw["w_ff1"], w_ff2=fw["w_ff2"], w_pg=fw["w_ple_gate"], w_ple=fw["w_ple"])
    wb = fw["w_branch"]
    wb0 = jnp.pad(wb[0].reshape(N_HEADS, V_HEAD, D_MODEL), ((0, 0), (0, LANE - V_HEAD), (0, 0))).reshape(N_HEADS * LANE, D_MODEL)
    return dict(
        w_uq=_head_pad_cols(fw["w_uq"], QK_NOPE + QK_ROPE, 0, QK_NOPE + QK_ROPE),
        w_uk=_head_pad_cols(fw["w_ukv"], QK_NOPE + V_HEAD, 0, QK_NOPE),
        w_uv=_head_pad_cols(fw["w_ukv"], QK_NOPE + V_HEAD, QK_NOPE, QK_NOPE + V_HEAD),
        wb=[wb0, wb[1], wb[2], wb[3]], w_out=fw["w_out"], ssd_conv_w=fw["ssd_conv_w"], lru_conv_w=fw["lru_conv_w"])


def _layer_params(sp, l):
    row = lambda n: sp[n][l][None, :]
    return dict(
        g_mix=row("g_mix"), q_norm=row("q_norm"), kv_norm=row("kv_norm"),
        pool=[sp["w_pool"][l].reshape(4 * LANE, LANE), row("pool_scale")],
        ssd=[None, row("ssd_conv_b"), _head8(sp["ssd_dt_bias"][l]), _head8(sp["ssd_a_log"][l]),
             _head8(sp["ssd_d"][l]), row("ssd_norm")],
        lru=[None, row("lru_conv_b"), _block_diag(sp["lru_w_a"][l]), row("lru_b_a"),
             _block_diag(sp["lru_w_i"][l]), row("lru_b_i"), row("lru_lambda")],
        g_mlp=row("g_mlp"), g_ple=row("g_ple"),
    )


_sig = jax.nn.sigmoid
_SSD_CARRY = [(HALO, SSD_XBC)] + [(LANE, LANE)] * 4


def _tiles(rows):
    return dict(tm=_pick(rows, 512), ta=_pick(rows, 512), tp=_pick(rows, 256), tl=_pick(rows, 256), ts=_pick(rows, 256))


def _mixer_tiles(u):
    return dict(
        cq=(u, 384, U_CQ[0] // 384), ckv=(u, 256, U_CKV[0] // 256), kr=(u, LANE, U_KR[0] // LANE),
        pool=(u, MIX, U_POOL[0] // MIX), z=(u, MIX, U_Z[0] // MIX), xbc=(u, SSD_XBC, U_XBC[0] // SSD_XBC),
        dt=(u, LANE, U_DT[0] // LANE), lg=(u, MIX, U_LG[0] // MIX), lx=(u, MIX, U_LX[0] // MIX))


def _layer_fwd(x, p_bf, ctx, l, pr, cosf, sinf):
    rows = x.shape[0]
    ts = _tiles(rows)
    tm = ts["tm"]
    nm = lambda s: f"{s}_l{l}"
    r = dict(x=x)
    (h,), _ = seq_fwd(nm("rms_in"), f_rms, [pr["g_mix"]], [(x, D_MODEL, 0)], [], [(D_MODEL, bf16)], tm)
    w = dict(_kernel_weights("A", ctx.weights(l, "A", h)))
    u = matmul(nm("w_in"), h, w["w_in"])
    mt = _mixer_tiles(u)
    (cqn,), _ = seq_fwd(nm("rms_q"), f_rms, [pr["q_norm"]], [mt["cq"]], [], [(Q_LORA, bf16)], tm)
    (ckvn,), _ = seq_fwd(nm("rms_kv"), f_rms, [pr["kv_norm"]], [mt["ckv"]], [], [(KV_LORA, bf16)], tm)
    (yb,), pool_saved = seq_fwd(nm("pool"), f_pool, pr["pool"], [mt["pool"]], [(POOL_HALO, MIX)], [(MIX, bf16)], ts["tp"])
    w.update(_kernel_weights("B", ctx.weights(l, "B", yb)))
    pr = dict(pr, ssd=[w["ssd_conv_w"]] + pr["ssd"][1:], lru=[w["lru_conv_w"]] + pr["lru"][1:])
    q = matmul(nm("w_uq"), cqn, w["w_uq"])
    kn = matmul(nm("w_uk"), ckvn, w["w_uk"])
    vb = matmul(nm("w_uv"), ckvn, w["w_uv"], outs=(bf16,))
    hw = N_HEADS * LANE
    (qr, kr), _ = seq_fwd(nm("mla_prep"), f_prep, [], [(q, hw, 0), (kn, hw, 0), mt["kr"], (cosf, LANE, 0), (sinf, LANE, 0)],
                          [], [(hw, bf16), (hw, bf16)], tm)
    o, lse = attn_fwd(qr, kr, vb, ts["ta"])
    (yc,), ssd_saved = seq_fwd(nm("ssd"), f_ssd, pr["ssd"], [mt["z"], mt["xbc"], mt["dt"]], _SSD_CARRY, [(MIX, bf16)], SSD_CHUNK)
    (la, lu), lru_saved = seq_fwd(nm("lru_pre"), f_lru_pre, pr["lru"], [mt["lx"]], [(HALO, MIX)], [(MIX, f32), (MIX, f32)], ts["tl"])
    hh = scan_fwd(la, lu, ts["ts"])
    (yd,), _ = seq_fwd(nm("lru_post"), f_lru_post, [], [(hh, MIX, 0), mt["lg"]], [], [(MIX, bf16)], tm)
    ys = [o, yb, yc, yd]
    pres = []
    m = None
    for n in range(4):
        last = n == 3
        if n == 0:
            epi = lambda acc, g: (_sig(g) * acc, acc)
            extras = [(u, 0)]
        else:
            epi = lambda acc, g, prev: (prev + _sig(g) * acc, acc)
            extras = [(u, D_MODEL * n), (m, 0)]
        m, pre = matmul(nm(f"branch{n}"), ys[n], w["wb"][n], outs=(bf16 if last else f32, f32), epi=epi, extras=extras)
        pres.append(pre)
    x1 = matmul(nm("w_out"), m, w["w_out"], epi=lambda acc, xr: (acc + xr,), extras=[(x, 0)])
    (h2,), _ = seq_fwd(nm("rms_mlp"), f_rms, [pr["g_mlp"]], [(x1, D_MODEL, 0)], [], [(D_MODEL, bf16)], tm)
    w.update(_kernel_weights("C", ctx.weights(l, "C", h2)))
    a1, act = matmul(nm("ff1"), h2, w["w_ff1"], outs=(f32, bf16), epi=lambda acc: (acc, jnp.square(jnp.maximum(acc, 0.0))))
    x2 = matmul(nm("ff2"), act, w["w_ff2"], epi=lambda acc, xr: (acc + xr,), extras=[(x1, 0)])
    (h3,), _ = seq_fwd(nm("rms_ple"), f_rms, [pr["g_ple"]], [(x2, D_MODEL, 0)], [], [(D_MODEL, bf16)], tm)
    gl = matmul(nm("ple_gate"), h3, w["w_pg"])
    x3, pe = matmul(nm("ple"), p_bf, w["w_ple"], outs=(f32, f32), epi=lambda acc, g, xr: (xr + acc * _sig(g), acc),
                    extras=[(gl, 0), (x2, 0)])
    r.update(h=h, u=u, cqn=cqn, ckvn=ckvn, q=q, kn=kn, vb=vb, qr=qr, kr=kr, o=o, lse=lse, ys=ys, pres=pres, m=m, x1=x1,
             h2=h2, a1=a1, act=act, x2=x2, h3=h3, gl=gl, pe=pe, p_bf=p_bf, pool_saved=pool_saved, ssd_saved=ssd_saved,
             lru_saved=lru_saved, la=la, hh=hh, w=w, pr=pr)
    return x3, r


def _gate_bwd(d, g, pre):
    s = _sig(g)
    return d * s, d * pre * s * (1.0 - s)


def _layer_bwd(dx3, r, ctx, l, cosf, sinf, tok, extra_small):
    rows = dx3.shape[0]
    ts = _tiles(rows)
    tm = ts["tm"]
    nm = lambda s: f"{s}_l{l}"
    u, w, pr = r["u"], r["w"], r["pr"]
    mt = _mixer_tiles(u)
    g = {}
    full = lambda a: (a, a.shape[1], 0)
    dpe, dgl = ew(nm("ple_bwd"), _gate_bwd, [full(dx3), full(r["gl"]), full(r["pe"])], [(D_MODEL, bf16)] * 2, tm)
    g["w_ple"] = matmul(nm("d_w_ple"), r["p_bf"], dpe, ta=True, deps=[tok] if tok is not None else [])
    g["w_pg"] = matmul(nm("d_w_pg"), r["h3"], dgl, ta=True)
    dh3 = matmul(nm("d_h3"), dgl, w["w_pg"], tb=True)
    (g["g_ple"],), (dx2,) = seq_bwd(nm("rms_ple_bwd"), f_rms, [pr["g_ple"]], [full(r["x2"])], [True], [], [dh3], [f32], tm,
                                    add_to=(0, dx3))
    da1 = matmul(nm("d_act"), dx2, w["w_ff2"], tb=True, outs=(bf16,),
                 epi=lambda acc, a: (acc * 2.0 * jnp.maximum(a, 0.0),), extras=[(r["a1"], 0)])
    g["w_ff2"] = matmul(nm("d_w_ff2"), r["act"], dx2, ta=True)
    g["w_ff1"] = matmul(nm("d_w_ff1"), r["h2"], da1, ta=True)
    tok = ctx.grads(l, "C", dict(w_ff1=g["w_ff1"], w_ff2=g["w_ff2"], w_ple_gate=g["w_pg"], w_ple=g["w_ple"]))
    dh2 = matmul(nm("d_h2"), da1, w["w_ff1"], tb=True, deps=[tok])
    (g["g_mlp"],), (dx1,) = seq_bwd(nm("rms_mlp_bwd"), f_rms, [pr["g_mlp"]], [full(r["x1"])], [True], [], [dh2], [f32], tm,
                                    add_to=(0, dx2))
    dm = matmul(nm("d_merged"), dx1, w["w_out"], tb=True)
    g["w_out"] = matmul(nm("d_w_out"), r["m"], dx1, ta=True)
    dgates, dys, g["wb"] = [], [], []
    for n in range(4):
        dpre, dgate = ew(nm(f"gate_bwd{n}"), _gate_bwd, [full(dm), (u, D_MODEL, n), full(r["pres"][n])],
                         [(D_MODEL, bf16)] * 2, tm)
        dgates.append(dgate)
        g["wb"].append(matmul(nm(f"d_w_branch{n}"), r["ys"][n], dpre, ta=True))
        dys.append(matmul(nm(f"d_y{n}"), dpre, w["wb"][n], tb=True, outs=(bf16 if n == 0 else f32,)))
    dqr = attn_bwd_dq(r["qr"], r["kr"], r["vb"], dys[0], r["o"], r["lse"], ts["ta"])
    dkr_, dv = attn_bwd_dkv(r["qr"], r["kr"], r["vb"], dys[0], r["o"], r["lse"], ts["ta"])
    _, (dq, dkn, dkrope) = seq_bwd(nm("mla_prep_bwd"), f_prep, [],
                                   [full(r["q"]), full(r["kn"]), mt["kr"], full(cosf), full(sinf)],
                                   [True, True, True, False, False], [], [dqr, dkr_], [bf16] * 3, tm)
    g["w_uq"] = matmul(nm("d_w_uq"), r["cqn"], dq, ta=True)
    g["w_uk"] = matmul(nm("d_w_uk"), r["ckvn"], dkn, ta=True)
    g["w_uv"] = matmul(nm("d_w_uv"), r["ckvn"], dv, ta=True)
    dcqn = matmul(nm("d_cqn"), dq, w["w_uq"], tb=True)
    dckvn = matmul(nm("d_ckvn_k"), dkn, w["w_uk"], tb=True)
    dckvn = matmul(nm("d_ckvn_v"), dv, w["w_uv"], tb=True, epi=lambda acc, prev: (acc + prev,), extras=[(dckvn, 0)])
    (g["q_norm"],), (dcq,) = seq_bwd(nm("rms_q_bwd"), f_rms, [pr["q_norm"]], [mt["cq"]], [True], [], [dcqn], [bf16], tm)
    (g["kv_norm"],), (dckv,) = seq_bwd(nm("rms_kv_bwd"), f_rms, [pr["kv_norm"]], [mt["ckv"]], [True], [], [dckvn], [bf16], tm)
    g["pool"], (dpool,) = seq_bwd(nm("pool_bwd"), f_pool, pr["pool"], [mt["pool"]], [True], r["pool_saved"], [dys[1]],
                                  [bf16], ts["tp"])
    g["ssd"], (dz, dxbc, ddt) = seq_bwd(nm("ssd_bwd"), f_ssd, pr["ssd"], [mt["z"], mt["xbc"], mt["dt"]], [True] * 3,
                                        r["ssd_saved"], [dys[2]], [bf16] * 3, SSD_CHUNK)
    _, (dhh, dlg) = seq_bwd(nm("lru_post_bwd"), f_lru_post, [], [full(r["hh"]), mt["lg"]], [True, True], [], [dys[3]],
                            [f32, bf16], tm)
    da, du = scan_bwd(r["la"], r["hh"], dhh, ts["ts"])
    g["lru"], (dlx,) = seq_bwd(nm("lru_pre_bwd"), f_lru_pre, pr["lru"], [mt["lx"]], [True], r["lru_saved"], [da, du],
                               [bf16], ts["tl"])
    dk = _head_unpad_cols(g["w_uk"], QK_NOPE)
    dv_ = _head_unpad_cols(g["w_uv"], V_HEAD)
    wb0 = g["wb"][0].reshape(N_HEADS, LANE, D_MODEL)[:, :V_HEAD].reshape(MIX, D_MODEL)
    ssd, lru, pool = g["ssd"], g["lru"], g["pool"]
    tok = ctx.grads(l, "B", dict(
        w_uq=_head_unpad_cols(g["w_uq"], QK_NOPE + QK_ROPE).reshape(Q_LORA, -1),
        w_ukv=jnp.concatenate([dk, dv_], axis=2).reshape(KV_LORA, -1), ssd_conv_w=ssd[0], lru_conv_w=lru[0],
        w_branch=jnp.stack([wb0, g["wb"][1], g["wb"][2], g["wb"][3]]), w_out=g["w_out"]))
    du_p = jnp.concatenate(dgates + [dpool, dz, dlg, dlx, dxbc, dcq, dkrope, dckv, ddt,
                                     jnp.zeros((rows, U_COLS - U_DT[0] - LANE), bf16)], axis=1)
    g_w_in = matmul(nm("d_w_in"), r["h"], du_p, ta=True, deps=[tok])
    tok = ctx.grads(l, "A", dict(w_in=_unpad_w_in(g_w_in)))
    dh = matmul(nm("d_h"), du_p, w["w_in"], tb=True, deps=[tok])
    (g_mix,), (dx,) = seq_bwd(nm("rms_in_bwd"), f_rms, [pr["g_mix"]], [full(r["x"])], [True], [], [dh], [f32], tm,
                              add_to=(0, dx1))
    small = dict(
        g_mix=g_mix[0], q_norm=g["q_norm"][0], kv_norm=g["kv_norm"][0],
        w_pool=pool[0].reshape(4, LANE, LANE), pool_scale=pool[1][0],
        ssd_conv_b=ssd[1][0], ssd_dt_bias=ssd[2][0, :8], ssd_a_log=ssd[3][0, :8], ssd_d=ssd[4][0, :8], ssd_norm=ssd[5][0],
        lru_conv_b=lru[1][0], lru_w_a=_block_diag_inv(lru[2]), lru_b_a=lru[3][0], lru_w_i=_block_diag_inv(lru[4]),
        lru_b_i=lru[5][0], lru_lambda=lru[6][0], g_mlp=g["g_mlp"][0], g_ple=g["g_ple"][0], **extra_small)
    tok = ctx.small(l, small)
    return dx, tok


def _rope_tables(positions):
    inv = 1.0 / (ROPE_THETA ** (jnp.arange(0, QK_ROPE, 2, dtype=f32) / QK_ROPE))
    ang = positions.astype(f32)[:, None] * inv
    cos, sin = jnp.cos(ang), jnp.sin(ang)
    rows = positions.shape[0]
    pad = jnp.zeros((rows, LANE - KR_LANE - QK_ROPE), f32)
    cosf = jnp.concatenate([jnp.ones((rows, KR_LANE), f32), cos, cos, pad], axis=1)
    sinf = jnp.concatenate([jnp.zeros((rows, KR_LANE), f32), -sin, sin, pad], axis=1)
    return cosf, sinf


WEIGHTS = ['g_mix', 'w_in', 'q_norm', 'w_uq', 'kv_norm', 'w_ukv', 'w_pool', 'pool_scale', 'ssd_conv_w', 'ssd_conv_b',
           'ssd_dt_bias', 'ssd_a_log', 'ssd_d', 'ssd_norm', 'lru_conv_w', 'lru_conv_b', 'lru_w_a', 'lru_b_a', 'lru_w_i',
           'lru_b_i', 'lru_lambda', 'w_branch', 'w_out', 'g_mlp', 'w_ff1', 'w_ff2', 'g_ple', 'w_ple_gate', 'w_ple', 'g_final']
SHARDED = dict(w_in=2, w_uq=2, w_ukv=2, ssd_conv_w=2, lru_conv_w=2, w_branch=3, w_out=1, w_ff1=2, w_ff2=1,
               w_ple_gate=1, w_ple=2)
F32_PAYLOAD = ("ssd_conv_w", "lru_conv_w")
DEPTH = 2


SMALL = [n for n in WEIGHTS if n not in SHARDED and n != "g_final"]


def local_step(x, p, positions, tgt, sp, ctx):
    cosf, sinf = _rope_tables(positions)
    res = []
    for l in range(DEPTH):
        x, r = _layer_fwd(x, p[l].astype(bf16), ctx, l, _layer_params(sp, l), cosf, sinf)
        res.append(r)
    loss8, dx, dgf = loss_head(x, tgt, sp["g_final"][None, :], _pick(x.shape[0], 512))
    tok = None
    for l in reversed(range(DEPTH)):
        dx, tok = _layer_bwd(dx, res[l], ctx, l, cosf, sinf, tok, dict(g_final=dgf[0]) if l == DEPTH - 1 else {})
    return loss8[0, 0], dx


def _payload(name, w):
    return w if name in F32_PAYLOAD else w.astype(bf16)


def _blocks(name, g):
    ax = SHARDED[name] - 1
    shape = list(g.shape)
    shape[ax:ax + 1] = [N_DEV, shape[ax] // N_DEV]
    return _payload(name, jnp.moveaxis(g.reshape(shape), ax, 0))


class _Exchanges:
    def __init__(self, wts):
        self.wts = wts
        self.ag, self.rs, self.sm = {}, {}, {}
        tok = None
        for l in range(DEPTH):
            for grp, names in GROUPS.items():
                h = exchange_start(f"ag_start_{grp}{l}", [_payload(n, wts[n][l]) for n in names], True,
                                   deps=[] if tok is None else [tok])
                tok = h["token"]
                self.ag[(l, grp)] = h
        self.all_started = tok

    def weights(self, l, grp, after):
        first = (l, grp) == (0, "A")
        got = exchange_wait(f"ag_wait_{grp}{l}", self.ag[(l, grp)], [after, self.all_started] if first else [after])
        out = {}
        for n, a in zip(GROUPS[grp], got):
            ax = SHARDED[n] - 1
            shape = list(self.wts[n].shape[1:])
            shape[ax] *= N_DEV
            out[n] = jnp.moveaxis(a, 0, ax).reshape(shape)
        return out

    def grads(self, l, grp, g):
        h = exchange_start(f"rs_start_{grp}{l}", [_blocks(n, g[n]) for n in GROUPS[grp]], False)
        self.rs[(l, grp)] = h
        return h["token"]

    def small(self, l, g):
        names = [n for n in SMALL + ["g_final"] if n in g]
        flat = jnp.concatenate([g[n].reshape(-1) for n in names])
        flat = jnp.pad(flat, (0, (-flat.shape[0]) % (8 * LANE))).reshape(-1, LANE)
        h = exchange_start(f"small_start_{l}", [flat], True)
        self.sm[l] = (h, [(n, g[n].shape) for n in names])
        return h["token"]

    def collect(self, groups, after):
        parts = {}
        for grp in groups:
            for l in reversed(range(DEPTH)):
                got = exchange_wait(f"rs_wait_{grp}{l}", self.rs[(l, grp)], [after])
                for n, a in zip(GROUPS[grp], got):
                    parts.setdefault(n, [None] * DEPTH)[l] = a
        return parts

    def collect_small(self, after):
        parts = {}
        for l in reversed(range(DEPTH)):
            h, layout = self.sm[l]
            (got,) = exchange_wait(f"small_wait_{l}", h, [after])
            got = got.reshape(N_DEV, -1)
            off = 0
            for n, shape in layout:
                size = 1
                for d in shape:
                    size *= d
                part = got[:, off:off + size].reshape((N_DEV,) + tuple(shape))
                off += size
                if n == "g_final":
                    parts[n] = [part]
                else:
                    parts.setdefault(n, [None] * DEPTH)[l] = part
        return parts


def kernel(x, p, positions, g_mix, w_in, q_norm, w_uq, kv_norm, w_ukv, w_pool, pool_scale, ssd_conv_w, ssd_conv_b,
           ssd_dt_bias, ssd_a_log, ssd_d, ssd_norm, lru_conv_w, lru_conv_b, lru_w_a, lru_b_a, lru_w_i, lru_b_i,
           lru_lambda, w_branch, w_out, g_mlp, w_ff1, w_ff2, g_ple, w_ple_gate, w_ple, g_final, loss_target, m_g_mix,
           m_w_in, m_q_norm, m_w_uq, m_kv_norm, m_w_ukv, m_w_pool, m_pool_scale, m_ssd_conv_w, m_ssd_conv_b,
           m_ssd_dt_bias, m_ssd_a_log, m_ssd_d, m_ssd_norm, m_lru_conv_w, m_lru_conv_b, m_lru_w_a, m_lru_b_a,
           m_lru_w_i, m_lru_b_i, m_lru_lambda, m_w_branch, m_w_out, m_g_mlp, m_w_ff1, m_w_ff2, m_g_ple, m_w_ple_gate,
           m_w_ple, m_g_final, v_g_mix, v_w_in, v_q_norm, v_w_uq, v_kv_norm, v_w_ukv, v_w_pool, v_pool_scale,
           v_ssd_conv_w, v_ssd_conv_b, v_ssd_dt_bias, v_ssd_a_log, v_ssd_d, v_ssd_norm, v_lru_conv_w, v_lru_conv_b,
           v_lru_w_a, v_lru_b_a, v_lru_w_i, v_lru_b_i, v_lru_lambda, v_w_branch, v_w_out, v_g_mlp, v_w_ff1, v_w_ff2,
           v_g_ple, v_w_ple_gate, v_w_ple, v_g_final):
    given = dict(locals())
    wts = {n: given[n] for n in WEIGHTS}
    ctx = _Exchanges(wts)
    loss, grad_x = local_step(x[0], p[:, 0], positions[0], loss_target[0], wts, ctx)

    def update(parts):
        out = {}
        for n, eight in parts.items():
            w, m, v = wts[n], given["m_" + n], given["v_" + n]
            if n == "g_final":
                out[n] = [a[0] for a in adamw(f"adamw_{n}", eight, w[None], m[None], v[None])]
            else:
                out[n] = adamw(f"adamw_{n}", eight, w, m, v)
        return out

    outs = update(ctx.collect(("C", "B"), grad_x))
    late = outs["w_ff1"][1]
    outs.update(update(ctx.collect(("A",), late)))
    outs.update(update(ctx.collect_small(late)))
    loss = lax.psum(loss, AXES)
    return (loss, grad_x[None], *[outs[n][0] for n in WEIGHTS], *[outs[n][1] for n in WEIGHTS],
            *[outs[n][2] for n in WEIGHTS], *[outs[n][3] for n in WEIGHTS])
```

```python
import functools

import jax
import jax.numpy as jnp
from jax import lax
from jax.experimental import pallas as pl
from jax.experimental.pallas import tpu as pltpu

f32 = jnp.float32
bf16 = jnp.bfloat16

D_MODEL = 1024
MIX = 512
N_HEADS = 8
QK_NOPE, QK_ROPE, V_HEAD = 64, 32, 64
Q_LORA, KV_LORA = 384, 256
ROPE_THETA = 10000.0
POOL_WINDOWS = (2, 4, 8, 16)
SSD_CHUNK = 128
SSD_XBC = 768
CONV_W = 4
LRU_C = 8.0
D_FF = 4096
EPS = 1e-6
IN_COLS = 7592
ADAM_LR, ADAM_B1, ADAM_B2, ADAM_EPS, ADAM_WD, ADAM_STEP = 0.001, 0.9, 0.999, 1e-08, 0.01, 10

LANE = 128
HALO = 8
POOL_HALO = 16
VMEM_LIMIT = 56 * 1024 * 1024
N_DEV = 8
AXES = ("x", "y", "c")

U_COLS = 8192
U_GATES, U_POOL, U_Z, U_LG, U_LX, U_XBC, U_CQ, U_KR, U_CKV, U_DT = (
    (0, 4096), (4096, 512), (4608, 512), (5120, 512), (5632, 512), (6144, 768),
    (6912, 384), (7296, 128), (7424, 256), (7680, 128))
KR_LANE = 64


def _cp(sem):
    return pltpu.CompilerParams(dimension_semantics=sem, vmem_limit_bytes=VMEM_LIMIT)


def _pick(dim, pref):
    if dim <= pref:
        return dim
    t = pref
    while t >= LANE:
        if dim % t == 0:
            return t
        t -= LANE
    t = pref
    while dim % t:
        t -= 8
    return t


@functools.partial(jax.custom_vjp, nondiff_argnums=(1,))
def shift_down(x, k):
    row = lax.broadcasted_iota(jnp.int32, x.shape, 0)
    return jnp.where(row >= k, pltpu.roll(x, k, 0), 0.0)


def _shift_down_fwd(x, k):
    return shift_down(x, k), None


def _shift_down_bwd(k, _, g):
    r = g.shape[0]
    row = lax.broadcasted_iota(jnp.int32, g.shape, 0)
    return (jnp.where(row < r - k, pltpu.roll(g, r - k, 0), 0.0),)


shift_down.defvjp(_shift_down_fwd, _shift_down_bwd)


@functools.partial(jax.custom_vjp, nondiff_argnums=(1,))
def lane_roll(x, s):
    return pltpu.roll(x, s, 1)


def _lane_roll_fwd(x, s):
    return lane_roll(x, s), None


def _lane_roll_bwd(s, _, g):
    return (pltpu.roll(g, (g.shape[1] - s) % g.shape[1], 1),)


lane_roll.defvjp(_lane_roll_fwd, _lane_roll_bwd)


def _tile_spec(tm, width, cb, n=None):
    if n is None:
        return pl.BlockSpec((tm, width), lambda i: (i, cb))
    return pl.BlockSpec((tm, width), lambda i: (n - 1 - i, cb))


def _const_spec(shape):
    nd = len(shape)
    return pl.BlockSpec(shape, lambda i: (0,) * nd)


def seq_fwd(name, f, params, tiles, carries, outs, tm):
    rows = tiles[0][0].shape[0]
    n = rows // tm
    np_, nt, no, nc = len(params), len(tiles), len(outs), len(carries)

    def body(*refs):
        p_refs = refs[:np_]
        t_refs = refs[np_:np_ + nt]
        o_refs = refs[np_ + nt:np_ + nt + no]
        s_refs = refs[np_ + nt + no:np_ + nt + no + nc]
        c_refs = refs[np_ + nt + no + nc:]
        i = pl.program_id(0)

        @pl.when(i == 0)
        def _():
            for c in c_refs:
                c[...] = jnp.zeros_like(c)

        cvals = [c[...] for c in c_refs]
        for s, c in zip(s_refs, cvals):
            s[0] = c
        o, newc = f(i, [r[...] for r in p_refs], cvals, [r[...].astype(f32) for r in t_refs])
        for r, v in zip(o_refs, o):
            r[...] = v.astype(r.dtype)
        for r, v in zip(c_refs, newc):
            r[...] = v

    in_specs = [_const_spec(p.shape) for p in params] + [_tile_spec(tm, w, cb) for (_, w, cb) in tiles]
    out_specs = [_tile_spec(tm, w, 0) for (w, _) in outs]
    out_specs += [pl.BlockSpec((1,) + tuple(c), lambda i, nd=len(c): (i,) + (0,) * nd) for c in carries]
    out_shape = [jax.ShapeDtypeStruct((rows, w), dt) for (w, dt) in outs]
    out_shape += [jax.ShapeDtypeStruct((n,) + tuple(c), f32) for c in carries]
    res = pl.pallas_call(
        body, name=name, grid=(n,), in_specs=in_specs, out_specs=out_specs, out_shape=out_shape,
        scratch_shapes=[pltpu.VMEM(tuple(c), f32) for c in carries],
        compiler_params=_cp(("arbitrary",)),
    )(*params, *[t[0] for t in tiles])
    return list(res[:no]), list(res[no:])


def seq_bwd(name, f, params, tiles, diff, saved, douts, gdtypes, tm, add_to=None):
    rows = tiles[0][0].shape[0]
    n = rows // tm
    np_, nt, nc, nd = len(params), len(tiles), len(saved), len(douts)
    didx = [k for k, d in enumerate(diff) if d]
    ng = len(didx)
    has_add = add_to is not None

    def body(*refs):
        p_refs = refs[:np_]
        t_refs = refs[np_:np_ + nt]
        s_refs = refs[np_ + nt:np_ + nt + nc]
        d_refs = refs[np_ + nt + nc:np_ + nt + nc + nd]
        pos = np_ + nt + nc + nd
        a_ref = refs[pos] if has_add else None
        pos += 1 if has_add else 0
        dp_refs = refs[pos:pos + np_]
        dt_refs = refs[pos + np_:pos + np_ + ng]
        dc_refs = refs[pos + np_ + ng:]
        i = pl.program_id(0)
        step = n - 1 - i

        @pl.when(i == 0)
        def _():
            for r in dp_refs:
                r[...] = jnp.zeros_like(r)
            for r in dc_refs:
                r[...] = jnp.zeros_like(r)

        pvals = [r[...] for r in p_refs]
        cvals = [r[0] for r in s_refs]
        xvals = [r[...].astype(f32) for r in t_refs]

        def fn(p, c, xd):
            x = list(xvals)
            for k, v in zip(didx, xd):
                x[k] = v
            return f(step, p, c, x)

        _, vjp = jax.vjp(fn, pvals, cvals, [xvals[k] for k in didx])
        dp, dc, dx = vjp(([r[...].astype(f32) for r in d_refs], [r[...] for r in dc_refs]))
        for r, v in zip(dp_refs, dp):
            r[...] += v
        for r, v in zip(dc_refs, dc):
            r[...] = v
        for k, (r, v) in enumerate(zip(dt_refs, dx)):
            if has_add and k == add_to[0]:
                v = v + a_ref[...].astype(f32)
            r[...] = v.astype(r.dtype)

    in_specs = [_const_spec(p.shape) for p in params] + [_tile_spec(tm, w, cb, n) for (_, w, cb) in tiles]
    in_specs += [pl.BlockSpec((1,) + tuple(s.shape[1:]), lambda i, nd_=s.ndim - 1: (n - 1 - i,) + (0,) * nd_) for s in saved]
    in_specs += [_tile_spec(tm, d.shape[1], 0, n) for d in douts]
    args = list(params) + [t[0] for t in tiles] + list(saved) + list(douts)
    if has_add:
        in_specs.append(_tile_spec(tm, add_to[1].shape[1], 0, n))
        args.append(add_to[1])
    out_specs = [_const_spec(p.shape) for p in params] + [_tile_spec(tm, tiles[k][1], 0, n) for k in didx]
    out_shape = [jax.ShapeDtypeStruct(p.shape, f32) for p in params]
    out_shape += [jax.ShapeDtypeStruct((rows, tiles[k][1]), dt) for k, dt in zip(didx, gdtypes)]
    res = pl.pallas_call(
        body, name=name, grid=(n,), in_specs=in_specs, out_specs=out_specs, out_shape=out_shape,
        scratch_shapes=[pltpu.VMEM(tuple(s.shape[1:]), f32) for s in saved],
        compiler_params=_cp(("arbitrary",)),
    )(*args)
    return list(res[:np_]), list(res[np_:])


def matmul(name, a, b, *, ta=False, tb=False, outs=(f32,), epi=None, extras=(), deps=(), tm=512, tn=1024, tk=1024):
    m, k = (a.shape[1], a.shape[0]) if ta else a.shape
    n = b.shape[0] if tb else b.shape[1]
    tm, tn, tk = _pick(m, tm), _pick(n, tn), _pick(k, tk)
    nk = k // tk
    ne = len(extras)
    dims = (((0 if ta else 1,), (1 if tb else 0,)), ((), ()))

    def body(*refs):
        a_ref, b_ref = refs[0], refs[1]
        e_refs = refs[2:2 + ne]
        o_refs = refs[2 + ne + len(deps):2 + ne + len(deps) + len(outs)]
        kk = pl.program_id(2)
        part = lax.dot_general(a_ref[...].astype(bf16), b_ref[...].astype(bf16), dims, preferred_element_type=f32)

        def finish(total):
            res = (total,) if epi is None else epi(total, *[e[...] for e in e_refs])
            for r, v in zip(o_refs, res):
                r[...] = v.astype(r.dtype)

        if nk == 1:
            finish(part)
            return
        acc = refs[-1]

        @pl.when(kk == 0)
        def _():
            acc[...] = part

        @pl.when(jnp.logical_and(kk > 0, kk < nk - 1))
        def _():
            acc[...] += part

        @pl.when(kk == nk - 1)
        def _():
            finish(acc[...] + part)

    a_spec = pl.BlockSpec((tk, tm), lambda i, j, q: (q, i)) if ta else pl.BlockSpec((tm, tk), lambda i, j, q: (i, q))
    b_spec = pl.BlockSpec((tn, tk), lambda i, j, q: (j, q)) if tb else pl.BlockSpec((tk, tn), lambda i, j, q: (q, j))
    assert all(off % tn == 0 for (_, off) in extras)
    e_specs = [pl.BlockSpec((tm, tn), lambda i, j, q, off=off // tn: (i, off + j)) for (_, off) in extras]
    res = pl.pallas_call(
        body, name=name, grid=(m // tm, n // tn, nk),
        in_specs=[a_spec, b_spec] + e_specs + [pl.BlockSpec(memory_space=pl.ANY) for _ in deps],
        out_specs=[pl.BlockSpec((tm, tn), lambda i, j, q: (i, j)) for _ in outs],
        out_shape=[jax.ShapeDtypeStruct((m, n), dt) for dt in outs],
        scratch_shapes=[pltpu.VMEM((tm, tn), f32)] if nk > 1 else [],
        compiler_params=_cp(("parallel", "parallel", "arbitrary")),
    )(a, b, *[e[0] for e in extras], *deps)
    return res[0] if len(outs) == 1 else tuple(res)


ATT_SCALE = (QK_NOPE + QK_ROPE) ** -0.5
LN2 = 0.6931471805599453
ATT_C = ATT_SCALE / LN2
NT = (((1,), (1,)), ((), ()))
TN = (((0,), (0,)), ((), ()))


def _causal(tq, tk):
    return lax.broadcasted_iota(jnp.int32, (tq, tk), 0) >= lax.broadcasted_iota(jnp.int32, (tq, tk), 1)


def _tri_pairs(n, by_column):
    if by_column:
        pairs = [(i, j) for j in range(n) for i in range(j, n)]
    else:
        pairs = [(i, j) for i in range(n) for j in range(i + 1)]
    return (jnp.asarray([a for a, _ in pairs], jnp.int32), jnp.asarray([b for _, b in pairs], jnp.int32))


HEADS_PER_STEP = 4
HEAD_PAIR = HEADS_PER_STEP * LANE


def attn_fwd(q, k, v, t):
    rows = q.shape[0]
    n = rows // t
    it, jt = _tri_pairs(n, False)

    def body(it_ref, jt_ref, q_ref, k_ref, v_ref, o_ref, lse_ref, m_s, l_s, acc_s):
        s_id = pl.program_id(1)
        i, j = it_ref[s_id], jt_ref[s_id]

        @pl.when(j == 0)
        def _():
            m_s[...] = jnp.full_like(m_s, -jnp.inf)
            l_s[...] = jnp.zeros_like(l_s)
            acc_s[...] = jnp.zeros_like(acc_s)

        def step(diag):
            for hh in range(HEADS_PER_STEP):
                sl = slice(LANE * hh, LANE * (hh + 1))
                s = lax.dot_general(q_ref[:, sl], k_ref[:, sl], NT, preferred_element_type=f32)
                if diag:
                    s = jnp.where(_causal(t, t), s, -jnp.inf)
                m_prev = m_s[:, sl]
                m_new = jnp.maximum(m_prev, jnp.max(s, axis=1, keepdims=True))
                alpha = jnp.exp2(m_prev - m_new)
                p = jnp.exp2(s - m_new[:, :1])
                l_s[:, sl] = alpha * l_s[:, sl] + jnp.sum(p, axis=1, keepdims=True)
                acc_s[:, sl] = alpha * acc_s[:, sl] + jnp.dot(p.astype(bf16), v_ref[:, sl], preferred_element_type=f32)
                m_s[:, sl] = m_new

        pl.when(j < i)(lambda: step(False))

        @pl.when(j == i)
        def _():
            step(True)
            o_ref[...] = (acc_s[...] / l_s[...]).astype(o_ref.dtype)
            lse_ref[...] = m_s[...] + jnp.log2(l_s[...])

    qs = pl.BlockSpec((t, HEAD_PAIR), lambda h, s, it_, jt_: (it_[s], h))
    ks = pl.BlockSpec((t, HEAD_PAIR), lambda h, s, it_, jt_: (jt_[s], h))
    hw = N_HEADS * LANE
    return pl.pallas_call(
        body, name="attn_fwd",
        grid_spec=pltpu.PrefetchScalarGridSpec(
            num_scalar_prefetch=2, grid=(hw // HEAD_PAIR, it.shape[0]), in_specs=[qs, ks, ks], out_specs=[qs, qs],
            scratch_shapes=[pltpu.VMEM((t, HEAD_PAIR), f32)] * 3),
        out_shape=[jax.ShapeDtypeStruct((rows, hw), bf16), jax.ShapeDtypeStruct((rows, hw), f32)],
        compiler_params=_cp(("parallel", "arbitrary")),
    )(it, jt, q, k, v)


def attn_bwd(q, k, v, do, o, lse, t):
    rows = q.shape[0]
    n = rows // t
    it, jt = _tri_pairs(n, True)

    def body(it_ref, jt_ref, q_ref, k_ref, v_ref, do_ref, o_ref, lse_ref, dq_ref, dk_ref, dv_ref, dk_s, dv_s):
        s_id = pl.program_id(1)
        i, j = it_ref[s_id], jt_ref[s_id]

        @pl.when(s_id == 0)
        def _():
            dq_ref[...] = jnp.zeros_like(dq_ref)

        @pl.when(i == j)
        def _():
            dk_s[...] = jnp.zeros_like(dk_s)
            dv_s[...] = jnp.zeros_like(dv_s)

        q_rows = pl.ds(pl.multiple_of(i * t, t), t)

        def step(diag):
            for hh in range(HEADS_PER_STEP):
                sl = slice(LANE * hh, LANE * (hh + 1))
                qh, kh, vh, doh = q_ref[:, sl], k_ref[:, sl], v_ref[:, sl], do_ref[:, sl]
                s = lax.dot_general(qh, kh, NT, preferred_element_type=f32)
                p = jnp.exp2(s - lse_ref[:, sl][:, :1])
                if diag:
                    p = jnp.where(_causal(t, t), p, 0.0)
                dp = lax.dot_general(doh, vh, NT, preferred_element_type=f32)
                delta = jnp.sum(doh.astype(f32) * o_ref[:, sl].astype(f32), axis=1, keepdims=True)
                ds = (p * (dp - delta) * LN2).astype(bf16)
                dv_s[:, sl] += lax.dot_general(p.astype(bf16), doh, TN, preferred_element_type=f32)
                dk_s[:, sl] += lax.dot_general(ds, qh, TN, preferred_element_type=f32)
                dq_ref[q_rows, sl] += jnp.dot(ds, kh, preferred_element_type=f32)

        pl.when(i > j)(lambda: step(False))
        pl.when(i == j)(lambda: step(True))

        @pl.when(i == n - 1)
        def _():
            dk_ref[...] = dk_s[...]
            dv_ref[...] = dv_s[...]

    qs = pl.BlockSpec((t, HEAD_PAIR), lambda h, s, it_, jt_: (it_[s], h))
    ks = pl.BlockSpec((t, HEAD_PAIR), lambda h, s, it_, jt_: (jt_[s], h))
    dqs = pl.BlockSpec((rows, HEAD_PAIR), lambda h, s, it_, jt_: (0, h))
    hw = N_HEADS * LANE
    return pl.pallas_call(
        body, name="attn_bwd",
        grid_spec=pltpu.PrefetchScalarGridSpec(
            num_scalar_prefetch=2, grid=(hw // HEAD_PAIR, it.shape[0]), in_specs=[qs, ks, ks, qs, qs, qs],
            out_specs=[dqs, ks, ks], scratch_shapes=[pltpu.VMEM((t, HEAD_PAIR), f32)] * 2),
        out_shape=[jax.ShapeDtypeStruct((rows, hw), f32)] * 3,
        compiler_params=_cp(("parallel", "arbitrary")),
    )(it, jt, q, k, v, do, o, lse)


def _steps(tm):
    k, out = 1, []
    while k < tm:
        out.append(k)
        k *= 2
    return out


def scan_fwd(a, u, tm):
    rows, ch = a.shape
    n = rows // tm

    def body(a_ref, u_ref, h_ref, h_s):
        @pl.when(pl.program_id(0) == 0)
        def _():
            h_s[...] = jnp.zeros_like(h_s)

        av, bv = a_ref[...], u_ref[...]
        row = lax.broadcasted_iota(jnp.int32, av.shape, 0)
        for k in _steps(tm):
            a_sh = jnp.where(row >= k, pltpu.roll(av, k, 0), 1.0)
            b_sh = jnp.where(row >= k, pltpu.roll(bv, k, 0), 0.0)
            bv = av * b_sh + bv
            av = av * a_sh
        h = bv + av * h_s[HALO - 1:HALO, :]
        h_ref[...] = h
        h_s[...] = h[tm - HALO:, :]

    spec = pl.BlockSpec((tm, ch), lambda i: (i, 0))
    return pl.pallas_call(
        body, name="lru_scan_fwd", grid=(n,), in_specs=[spec, spec], out_specs=spec,
        out_shape=jax.ShapeDtypeStruct((rows, ch), f32), scratch_shapes=[pltpu.VMEM((HALO, ch), f32)],
        compiler_params=_cp(("arbitrary",)),
    )(a, u)


def scan_bwd(a, h, dh, tm):
    rows, ch = a.shape
    n = rows // tm
    per = tm // HALO

    def body(a_ref, h_ref, hp_ref, dh_ref, da_ref, du_ref, g_s, a_s):
        i = pl.program_id(0)
        step = n - 1 - i

        @pl.when(i == 0)
        def _():
            g_s[...] = jnp.zeros_like(g_s)
            a_s[...] = jnp.zeros_like(a_s)

        a0 = a_ref[...]
        row = lax.broadcasted_iota(jnp.int32, a0.shape, 0)
        av = jnp.where(row < tm - 1, pltpu.roll(a0, tm - 1, 0), a_s[0:1, :])
        bv = dh_ref[...]
        for k in _steps(tm):
            a_sh = jnp.where(row < tm - k, pltpu.roll(av, tm - k, 0), 1.0)
            b_sh = jnp.where(row < tm - k, pltpu.roll(bv, tm - k, 0), 0.0)
            bv = bv + av * b_sh
            av = av * a_sh
        g = bv + av * g_s[0:1, :]
        h_last = jnp.where(step > 0, hp_ref[HALO - 1:HALO, :], 0.0)
        h_prev = jnp.where(row >= 1, pltpu.roll(h_ref[...], 1, 0), h_last)
        du_ref[...] = g
        da_ref[...] = g * h_prev
        g_s[...] = g[0:HALO, :]
        a_s[...] = a0[0:HALO, :]

    spec = pl.BlockSpec((tm, ch), lambda i: (n - 1 - i, 0))
    hp_spec = pl.BlockSpec((HALO, ch), lambda i: (jnp.maximum((n - 1 - i) * per - 1, 0), 0))
    return pl.pallas_call(
        body, name="lru_scan_bwd", grid=(n,), in_specs=[spec, spec, hp_spec, spec], out_specs=[spec, spec],
        out_shape=[jax.ShapeDtypeStruct((rows, ch), f32)] * 2,
        scratch_shapes=[pltpu.VMEM((HALO, ch), f32)] * 2,
        compiler_params=_cp(("arbitrary",)),
    )(a, h, h, dh)


def _rms(x, g):
    return x * lax.rsqrt(jnp.mean(x * x, axis=-1, keepdims=True) + EPS) * g


def f_rms(step, p, c, x):
    return [_rms(x[0], p[0])], []


def _rope(x, cosf, sinf):
    lane = lax.broadcasted_iota(jnp.int32, x.shape, 1)
    sw = jnp.where(lane < KR_LANE + QK_ROPE // 2, lane_roll(x, LANE - QK_ROPE // 2), lane_roll(x, QK_ROPE // 2))
    return x * cosf + sw * sinf


def f_prep(step, p, c, x):
    q, kn, kr, cosf, sinf = x
    kr_rot = _rope(kr, cosf, sinf)
    qr = [_rope(q[:, LANE * h:LANE * (h + 1)], cosf, sinf) * ATT_C for h in range(N_HEADS)]
    kk = [kn[:, LANE * h:LANE * (h + 1)] + kr_rot for h in range(N_HEADS)]
    return [jnp.concatenate(qr, axis=1), jnp.concatenate(kk, axis=1)], []


def _conv(tail, x, w, b):
    xf = jnp.concatenate([tail, x], axis=0)
    acc = b + w[CONV_W - 1:CONV_W, :] * xf
    for k in range(CONV_W - 1):
        acc = acc + w[k:k + 1, :] * shift_down(xf, CONV_W - 1 - k)
    return acc[HALO:, :]


def f_pool(step, p, c, x):
    wp, sc = p
    (tail,) = c
    (u,) = x
    tm = u.shape[0]
    xf = jnp.concatenate([tail, u], axis=0)
    sums, s, w = [], xf, 1
    while w < POOL_WINDOWS[-1]:
        s = s + shift_down(s, w)
        w *= 2
        sums.append(s)
    t = step * tm + lax.broadcasted_iota(jnp.int32, (tm, 1), 0)
    ys = []
    for g, (w, s) in enumerate(zip(POOL_WINDOWS, sums)):
        sl = slice(LANE * g, LANE * (g + 1))
        cnt = jnp.minimum(t + 1, w).astype(f32)
        d = s[POOL_HALO:, sl] / cnt - u[:, sl]
        ys.append(jnp.dot(d.astype(bf16), wp[LANE * g:LANE * (g + 1), :].astype(bf16), preferred_element_type=f32))
    return [jnp.concatenate(ys, axis=1) * sc], [u[tm - POOL_HALO:, :]]


def _expand_heads(v):
    hrow = lax.broadcasted_iota(jnp.int32, (LANE, MIX), 0)
    col = lax.broadcasted_iota(jnp.int32, (LANE, MIX), 1)
    e = jnp.logical_and(col >= hrow * 64, col < hrow * 64 + 64).astype(f32)
    return jnp.dot(v, e, precision=lax.Precision.HIGHEST, preferred_element_type=f32)


def f_ssd(step, p, c, x):
    conv_w, conv_b, dtb, alog, dsk, ng = p
    tail, s_in = c[0], c[1:]
    z, xbc, dt = x
    ln = z.shape[0]
    xc = jax.nn.silu(_conv(tail, xbc, conv_w, conv_b))
    xs, bb, cc = xc[:, :MIX], xc[:, MIX:MIX + LANE], xc[:, MIX + LANE:]
    dtv = jax.nn.softplus(dt + dtb[0:1, :])
    a_head = -jnp.exp(alog)
    a = dtv * a_head[0:1, :]
    ri = lax.broadcasted_iota(jnp.int32, (ln, ln), 0)
    ci = lax.broadcasted_iota(jnp.int32, (ln, ln), 1)
    tril = (ri >= ci).astype(f32)
    triu = (ri <= ci).astype(f32)
    hi = lax.Precision.HIGHEST
    a_cs = jnp.dot(tril, a, precision=hi, preferred_element_type=f32)
    a_cs_t = lax.dot_general(a, triu, TN, precision=hi, preferred_element_type=f32)
    dt_e = _expand_heads(dtv)
    a_e = dt_e * _expand_heads(a_head)[0:1, :]
    a_cs_e = jnp.dot(tril, a_e, precision=hi, preferred_element_type=f32)
    tot_e = jnp.sum(a_e, axis=0, keepdims=True)
    xdt = xs * dt_e
    lane = lax.broadcasted_iota(jnp.int32, (1, LANE), 1)
    half = [(lane < 64).astype(f32), (lane >= 64).astype(f32)]
    hrow = lax.broadcasted_iota(jnp.int32, (LANE, 1), 0)
    cg = [(cc * half[g]).astype(bf16) for g in range(2)]
    bg = [(bb * half[g]).astype(bf16) for g in range(2)]
    cb = [lax.dot_general(cg[g], bg[g], NT, preferred_element_type=f32) for g in range(2)]
    ys, s_out = [], []
    for j in range(4):
        g = j // 2
        blk = slice(LANE * j, LANE * (j + 1))
        xj = xdt[:, blk]
        yj = jnp.zeros((ln, LANE), f32)
        for hh in range(2):
            h = 2 * j + hh
            col = jnp.sum(a_cs * (lane == h).astype(f32), axis=1, keepdims=True)
            rowv = jnp.sum(a_cs_t * (hrow == h).astype(f32), axis=0, keepdims=True)
            lmat = jnp.exp(jnp.where(ri >= ci, col - rowv, -jnp.inf))
            yj = yj + jnp.dot((cb[g] * lmat).astype(bf16), (xj * half[hh]).astype(bf16), preferred_element_type=f32)
        acs = a_cs_e[:, blk]
        tot = tot_e[:, blk]
        yj = yj + jnp.exp(acs) * jnp.dot(cg[g], s_in[j].astype(bf16), preferred_element_type=f32)
        s_new = jnp.exp(tot) * s_in[j] + lax.dot_general(bg[g], (xj * jnp.exp(tot - acs)).astype(bf16), TN,
                                                         preferred_element_type=f32)
        ys.append(yj)
        s_out.append(s_new)
    y = jnp.concatenate(ys, axis=1) + xs * _expand_heads(dsk)[0:1, :]
    y = y * jax.nn.silu(z)
    return [_rms(y, ng)], [xbc[ln - HALO:, :]] + s_out


def _neg_expm1(y):
    series = -y * (1.0 + y * (0.5 + y * (1.0 / 6 + y * (1.0 / 24 + y * (1.0 / 120)))))
    return jnp.where(y > -0.05, series, 1.0 - jnp.exp(y))


def f_lru_pre(step, p, c, x):
    cw, cb_, wa, ba, wi, bi, lam = p
    (tail,) = c
    (lx,) = x
    tm = lx.shape[0]
    xc = _conv(tail, lx, cw, cb_)
    xb = xc.astype(bf16)
    r = jax.nn.sigmoid(jnp.dot(xb, wa.astype(bf16), preferred_element_type=f32) + ba)
    it = jax.nn.sigmoid(jnp.dot(xb, wi.astype(bf16), preferred_element_type=f32) + bi)
    log_a = -LRU_C * r * jax.nn.softplus(-lam)
    mult = jnp.sqrt(_neg_expm1(2.0 * log_a))
    return [jnp.exp(log_a), xc * it * mult], [lx[tm - HALO:, :]]


def f_lru_post(step, p, c, x):
    h, g = x
    return [h * jax.nn.gelu(g)], []


def loss_head(x, tgt, g, tm):
    rows, d = x.shape
    n = rows // tm

    def body(x_ref, t_ref, g_ref, loss_ref, dx_ref, dg_ref):
        @pl.when(pl.program_id(0) == 0)
        def _():
            loss_ref[...] = jnp.zeros_like(loss_ref)
            dg_ref[...] = jnp.zeros_like(dg_ref)

        def fn(gv, xv):
            err = _rms(xv, gv) - t_ref[...]
            return 0.5 * jnp.sum(jnp.mean(err * err, axis=-1, keepdims=True))

        val, (dg, dx) = jax.value_and_grad(fn, argnums=(0, 1))(g_ref[...], x_ref[...])
        loss_ref[...] += val
        dg_ref[...] += dg
        dx_ref[...] = dx

    spec = pl.BlockSpec((tm, d), lambda i: (i, 0))
    return pl.pallas_call(
        body, name="loss_head", grid=(n,), in_specs=[spec, spec, _const_spec((1, d))],
        out_specs=[_const_spec((8, LANE)), spec, _const_spec((1, d))],
        out_shape=[jax.ShapeDtypeStruct((8, LANE), f32), jax.ShapeDtypeStruct((rows, d), f32),
                   jax.ShapeDtypeStruct((1, d), f32)],
        compiler_params=_cp(("arbitrary",)),
    )(x, tgt, g)


def ew(name, fn, ins, outs, tm):
    rows = ins[0][0].shape[0]
    ni = len(ins)

    def body(*refs):
        res = fn(*[r[...].astype(f32) for r in refs[:ni]])
        for r, v in zip(refs[ni:], res):
            r[...] = v.astype(r.dtype)

    return pl.pallas_call(
        body, name=name, grid=(rows // tm,), in_specs=[_tile_spec(tm, w, cb) for (_, w, cb) in ins],
        out_specs=[_tile_spec(tm, w, 0) for (w, _) in outs],
        out_shape=[jax.ShapeDtypeStruct((rows, w), dt) for (w, dt) in outs],
        compiler_params=_cp(("parallel",)),
    )(*[t[0] for t in ins])


def _peers():
    x, y, c = lax.axis_index("x"), lax.axis_index("y"), lax.axis_index("c")
    me = 4 * x + 2 * y + c
    out = []
    for k in range(1, N_DEV):
        px = 1 - x if k & 4 else x
        py = 1 - y if k & 2 else y
        pc = 1 - c if k & 1 else c
        out.append(((px, py, pc), 4 * px + 2 * py + pc))
    return me, out


_HBM = pl.BlockSpec(memory_space=pltpu.HBM)
_SEM = pl.BlockSpec(memory_space=pltpu.SEMAPHORE)
_EFFECT = pltpu.SideEffectType.DATAFLOW_SIDE_EFFECTING


def _remote(src_ref, land_ref, gather, me, pid, dev, send_sems, recv_sems, k, recv_side):
    return pltpu.make_async_remote_copy(
        src_ref=src_ref if gather else src_ref.at[pid], dst_ref=land_ref.at[pid if recv_side else me],
        send_sem=send_sems.at[k], recv_sem=recv_sems.at[k], device_id=dev, device_id_type=pl.DeviceIdType.MESH)


def exchange_start(name, srcs, gather, deps=()):
    n, nd = len(srcs), len(deps)
    shapes = [(s.shape if gather else s.shape[1:]) for s in srcs]
    lands = [lax.empty((N_DEV,) + tuple(sh), s.dtype) for s, sh in zip(srcs, shapes)]

    def body(*refs):
        src_refs, land_refs = refs[:n], refs[n:2 * n]
        send_sems, recv_sems = refs[2 * n + nd], refs[2 * n + nd + 1]
        token = refs[-1]
        me, peers = _peers()
        for k, (dev, pid) in enumerate(peers):
            for s_ref, l_ref in zip(src_refs, land_refs):
                _remote(s_ref, l_ref, gather, me, pid, dev, send_sems, recv_sems, k, False).start()
        token[...] = jnp.zeros_like(token)

    hbm = lambda a: pltpu.with_memory_space_constraint(a, pltpu.HBM)
    res = pl.pallas_call(
        body, name=name,
        out_shape=(pltpu.SemaphoreType.DMA((N_DEV - 1,)), pltpu.SemaphoreType.DMA((N_DEV - 1,)),
                   *[pltpu.HBM(a.shape, a.dtype) for a in list(srcs) + lands], jax.ShapeDtypeStruct((8, LANE), f32)),
        in_specs=[_HBM] * (2 * n) + [pl.BlockSpec(memory_space=pl.ANY)] * nd,
        out_specs=(_SEM, _SEM, *([_HBM] * (2 * n)), pl.BlockSpec(memory_space=pltpu.VMEM)),
        input_output_aliases={i: 2 + i for i in range(2 * n)},
        compiler_params=pltpu.CompilerParams(has_side_effects=_EFFECT),
    )(*[hbm(a) for a in list(srcs) + lands], *deps)
    return dict(sems=res[:2], srcs=list(res[2:2 + n]), lands=list(res[2 + n:2 + 2 * n]), token=res[-1], gather=gather)


def exchange_wait(name, h, afters):
    n, gather = len(h["srcs"]), h["gather"]

    def body(*refs):
        src_refs, land_refs = refs[:n], refs[n:2 * n]
        send_sems, recv_sems = refs[2 * n], refs[2 * n + 1]
        me, peers = _peers()
        for k, (dev, pid) in enumerate(peers):
            for s_ref, l_ref in zip(src_refs, land_refs):
                _remote(s_ref, l_ref, gather, me, pid, dev, send_sems, recv_sems, k, True).wait_recv()
        for k, (dev, pid) in enumerate(peers):
            for s_ref, l_ref in zip(src_refs, land_refs):
                _remote(s_ref, l_ref, gather, me, pid, dev, send_sems, recv_sems, k, False).wait_send()

    arrs = h["srcs"] + h["lands"]
    res = pl.pallas_call(
        body, name=name, out_shape=tuple(pltpu.HBM(a.shape, a.dtype) for a in arrs),
        in_specs=[_HBM] * (2 * n) + [_SEM, _SEM] + [pl.BlockSpec(memory_space=pl.ANY)] * len(afters),
        out_specs=tuple([_HBM] * (2 * n)), input_output_aliases={i: i for i in range(2 * n)},
        compiler_params=pltpu.CompilerParams(has_side_effects=_EFFECT),
    )(*arrs, *h["sems"], *afters)
    me = 4 * lax.axis_index("x") + 2 * lax.axis_index("y") + lax.axis_index("c")
    out = []
    for src, land in zip(res[:n], res[n:]):
        mine = src if gather else lax.dynamic_index_in_dim(src, me, 0, keepdims=False)
        out.append(lax.dynamic_update_index_in_dim(land, mine, me, 0))
    return out


def adamw(name, parts, w, m, v):
    nl = len(parts)
    shape = w.shape[1:]
    c = shape[-1]
    r = 1
    for s in shape[:-1]:
        r *= s
    tr = _pick(r, 256) if r % 8 == 0 else r
    nb = r // tr
    parts2 = [p.reshape(N_DEV, r, c) for p in parts]
    w2, m2, v2 = (a.reshape(nl, r, c) for a in (w, m, v))

    def body(*refs):
        p_refs = refs[:nl]
        w_ref, m_ref, v_ref, g_ref, d_ref, nm_ref, nv_ref = refs[nl:]
        layer = pl.program_id(0)
        for ll, p_ref in enumerate(p_refs):
            @pl.when(layer == ll)
            def _(p_ref=p_ref):
                g = p_ref[0].astype(f32)
                for i in range(1, N_DEV):
                    g = g + p_ref[i].astype(f32)
                mn = ADAM_B1 * m_ref[0] + (1.0 - ADAM_B1) * g
                vn = ADAM_B2 * v_ref[0] + (1.0 - ADAM_B2) * jnp.square(g)
                m_hat = mn / (1.0 - ADAM_B1 ** ADAM_STEP)
                v_hat = vn / (1.0 - ADAM_B2 ** ADAM_STEP)
                g_ref[0] = g
                d_ref[0] = -ADAM_LR * (m_hat / (jnp.sqrt(v_hat) + ADAM_EPS) + ADAM_WD * w_ref[0])
                nm_ref[0] = mn
                nv_ref[0] = vn

    def p_spec(ll):
        return pl.BlockSpec((N_DEV, tr, c), lambda l, i: (0, jnp.where(l == ll, i, jnp.where(l > ll, nb - 1, 0)), 0))

    spec = pl.BlockSpec((1, tr, c), lambda l, i: (l, i, 0))
    res = pl.pallas_call(
        body, name=name, grid=(nl, nb), in_specs=[p_spec(ll) for ll in range(nl)] + [spec, spec, spec],
        out_specs=[spec] * 4, out_shape=[jax.ShapeDtypeStruct((nl, r, c), f32)] * 4,
        compiler_params=_cp(("arbitrary", "arbitrary")),
    )(*parts2, w2, m2, v2)
    return [a.reshape(w.shape) for a in res]


_IN_SPLITS = dict(cq=(0, 384), ckv=(384, 640), kr=(640, 672), pool=(672, 1184), z=(1184, 1696), xbc=(1696, 2464),
                  dt=(2464, 2472), lg=(2472, 2984), lx=(2984, 3496), gates=(3496, 7592))


W_IN_SHARD = IN_COLS // N_DEV

_PAD_ORDER = ("gates", "pool", "z", "lg", "lx", "xbc", "cq", KR_LANE, "kr", LANE - KR_LANE - QK_ROPE, "ckv", "dt",
              LANE - 8, U_COLS - U_DT[0] - LANE)
_SEGMENTS = ((0, U_CQ[0], 384), (384, U_CKV[0], 256), (640, U_KR[0] + KR_LANE, QK_ROPE), (672, U_POOL[0], 512),
             (1184, U_Z[0], 512), (1696, U_XBC[0], 768), (2464, U_DT[0], 8), (2472, U_LG[0], 512), (2984, U_LX[0], 512),
             (3496, 0, 4096))


def _pad_w_in(shards):
    rows = shards.shape[1]
    pieces = []
    for item in _PAD_ORDER:
        if isinstance(item, int):
            pieces.append(jnp.zeros((rows, item), shards.dtype))
            continue
        a, b = _IN_SPLITS[item]
        for d in range(a // W_IN_SHARD, (b - 1) // W_IN_SHARD + 1):
            lo, hi = max(a, d * W_IN_SHARD), min(b, (d + 1) * W_IN_SHARD)
            pieces.append(shards[d, :, lo - d * W_IN_SHARD:hi - d * W_IN_SHARD])
    return jnp.concatenate(pieces, axis=1)


def _w_in_blocks(g):
    blocks = []
    for d in range(N_DEV):
        a, b = d * W_IN_SHARD, (d + 1) * W_IN_SHARD
        pieces = []
        for ref, pad, width in _SEGMENTS:
            lo, hi = max(a, ref), min(b, ref + width)
            if lo < hi:
                pieces.append(g[:, pad + lo - ref:pad + hi - ref])
        blocks.append(jnp.concatenate(pieces, axis=1))
    return jnp.stack(blocks).astype(bf16)


def _head_pad_cols(w, per, lo, hi):
    k = w.shape[0]
    w = w.reshape(k, N_HEADS, per)[:, :, lo:hi]
    return jnp.pad(w, ((0, 0), (0, 0), (0, LANE - (hi - lo)))).reshape(k, N_HEADS * LANE)


def _head_unpad_cols(g, n):
    k = g.shape[0]
    return g.reshape(k, N_HEADS, LANE)[:, :, :n]


def _block_diag(w):
    out = jnp.zeros((MIX, MIX), w.dtype)
    for i in range(8):
        out = lax.dynamic_update_slice(out, w[i], (64 * i, 64 * i))
    return out


def _block_diag_inv(g):
    return jnp.stack([g[64 * i:64 * (i + 1), 64 * i:64 * (i + 1)] for i in range(8)])


def _head8(v):
    return jnp.zeros((8, LANE), f32).at[0, :8].set(v)


GROUPS = dict(A=("w_in",), B=("w_uq", "w_ukv", "ssd_conv_w", "lru_conv_w", "w_branch", "w_out"),
              C=("w_ff1", "w_ff2", "w_ple_gate", "w_ple"))


def _kernel_weights(grp, fw):
    if grp == "A":
        return dict(w_in=_pad_w_in(fw["w_in"]))
    if grp == "C":
        return dict(w_ff1=fw["w_ff1"], w_ff2=fw["w_ff2"], w_pg=fw["w_ple_gate"], w_ple=fw["w_ple"])
    wb = fw["w_branch"]
    wb0 = jnp.pad(wb[0].reshape(N_HEADS, V_HEAD, D_MODEL), ((0, 0), (0, LANE - V_HEAD), (0, 0))).reshape(N_HEADS * LANE, D_MODEL)
    return dict(
        w_uq=_head_pad_cols(fw["w_uq"], QK_NOPE + QK_ROPE, 0, QK_NOPE + QK_ROPE),
        w_uk=_head_pad_cols(fw["w_ukv"], QK_NOPE + V_HEAD, 0, QK_NOPE),
        w_uv=_head_pad_cols(fw["w_ukv"], QK_NOPE + V_HEAD, QK_NOPE, QK_NOPE + V_HEAD),
        wb=[wb0, wb[1], wb[2], wb[3]], w_out=fw["w_out"], ssd_conv_w=fw["ssd_conv_w"], lru_conv_w=fw["lru_conv_w"])


def _layer_params(sp, l):
    row = lambda n: sp[n][l][None, :]
    return dict(
        g_mix=row("g_mix"), q_norm=row("q_norm"), kv_norm=row("kv_norm"),
        pool=[sp["w_pool"][l].reshape(4 * LANE, LANE), row("pool_scale")],
        ssd=[None, row("ssd_conv_b"), _head8(sp["ssd_dt_bias"][l]), _head8(sp["ssd_a_log"][l]),
             _head8(sp["ssd_d"][l]), row("ssd_norm")],
        lru=[None, row("lru_conv_b"), _block_diag(sp["lru_w_a"][l]), row("lru_b_a"),
             _block_diag(sp["lru_w_i"][l]), row("lru_b_i"), row("lru_lambda")],
        g_mlp=row("g_mlp"), g_ple=row("g_ple"),
    )


_sig = jax.nn.sigmoid
_SSD_CARRY = [(HALO, SSD_XBC)] + [(LANE, LANE)] * 4


def _tiles(rows):
    return dict(tm=_pick(rows, 512), ta=_pick(rows, 512), tp=_pick(rows, 256), tl=_pick(rows, 256), ts=_pick(rows, 256))


def _mixer_tiles(u):
    return dict(
        cq=(u, 384, U_CQ[0] // 384), ckv=(u, 256, U_CKV[0] // 256), kr=(u, LANE, U_KR[0] // LANE),
        pool=(u, MIX, U_POOL[0] // MIX), z=(u, MIX, U_Z[0] // MIX), xbc=(u, SSD_XBC, U_XBC[0] // SSD_XBC),
        dt=(u, LANE, U_DT[0] // LANE), lg=(u, MIX, U_LG[0] // MIX), lx=(u, MIX, U_LX[0] // MIX))


def _layer_fwd(x, p_bf, ctx, l, pr, cosf, sinf):
    rows = x.shape[0]
    ts = _tiles(rows)
    tm = ts["tm"]
    nm = lambda s: f"{s}_l{l}"
    r = dict(x=x)
    (h,), _ = seq_fwd(nm("rms_in"), f_rms, [pr["g_mix"]], [(x, D_MODEL, 0)], [], [(D_MODEL, bf16)], tm)
    w = dict(_kernel_weights("A", ctx.weights(l, "A", h)))
    u = matmul(nm("w_in"), h, w["w_in"])
    mt = _mixer_tiles(u)
    (cqn,), _ = seq_fwd(nm("rms_q"), f_rms, [pr["q_norm"]], [mt["cq"]], [], [(Q_LORA, bf16)], tm)
    (ckvn,), _ = seq_fwd(nm("rms_kv"), f_rms, [pr["kv_norm"]], [mt["ckv"]], [], [(KV_LORA, bf16)], tm)
    (yb,), pool_saved = seq_fwd(nm("pool"), f_pool, pr["pool"], [mt["pool"]], [(POOL_HALO, MIX)], [(MIX, bf16)], ts["tp"])
    w.update(_kernel_weights("B", ctx.weights(l, "B", yb)))
    pr = dict(pr, ssd=[w["ssd_conv_w"]] + pr["ssd"][1:], lru=[w["lru_conv_w"]] + pr["lru"][1:])
    q = matmul(nm("w_uq"), cqn, w["w_uq"])
    kn = matmul(nm("w_uk"), ckvn, w["w_uk"])
    vb = matmul(nm("w_uv"), ckvn, w["w_uv"], outs=(bf16,))
    hw = N_HEADS * LANE
    (qr, kr), _ = seq_fwd(nm("mla_prep"), f_prep, [], [(q, hw, 0), (kn, hw, 0), mt["kr"], (cosf, LANE, 0), (sinf, LANE, 0)],
                          [], [(hw, bf16), (hw, bf16)], tm)
    o, lse = attn_fwd(qr, kr, vb, ts["ta"])
    (yc,), ssd_saved = seq_fwd(nm("ssd"), f_ssd, pr["ssd"], [mt["z"], mt["xbc"], mt["dt"]], _SSD_CARRY, [(MIX, bf16)], SSD_CHUNK)
    (la, lu), lru_saved = seq_fwd(nm("lru_pre"), f_lru_pre, pr["lru"], [mt["lx"]], [(HALO, MIX)], [(MIX, f32), (MIX, f32)], ts["tl"])
    hh = scan_fwd(la, lu, ts["ts"])
    (yd,), _ = seq_fwd(nm("lru_post"), f_lru_post, [], [(hh, MIX, 0), mt["lg"]], [], [(MIX, bf16)], tm)
    ys = [o, yb, yc, yd]
    pres = []
    m = None
    for n in range(4):
        last = n == 3
        if n == 0:
            epi = lambda acc, g: (_sig(g) * acc, acc)
            extras = [(u, 0)]
        else:
            epi = lambda acc, g, prev: (prev + _sig(g) * acc, acc)
            extras = [(u, D_MODEL * n), (m, 0)]
        m, pre = matmul(nm(f"branch{n}"), ys[n], w["wb"][n], outs=(bf16 if last else f32, f32), epi=epi, extras=extras)
        pres.append(pre)
    x1 = matmul(nm("w_out"), m, w["w_out"], epi=lambda acc, xr: (acc + xr,), extras=[(x, 0)])
    (h2,), _ = seq_fwd(nm("rms_mlp"), f_rms, [pr["g_mlp"]], [(x1, D_MODEL, 0)], [], [(D_MODEL, bf16)], tm)
    w.update(_kernel_weights("C", ctx.weights(l, "C", h2)))
    a1, act = matmul(nm("ff1"), h2, w["w_ff1"], outs=(f32, bf16), epi=lambda acc: (acc, jnp.square(jnp.maximum(acc, 0.0))))
    x2 = matmul(nm("ff2"), act, w["w_ff2"], epi=lambda acc, xr: (acc + xr,), extras=[(x1, 0)])
    (h3,), _ = seq_fwd(nm("rms_ple"), f_rms, [pr["g_ple"]], [(x2, D_MODEL, 0)], [], [(D_MODEL, bf16)], tm)
    gl = matmul(nm("ple_gate"), h3, w["w_pg"])
    x3, pe = matmul(nm("ple"), p_bf, w["w_ple"], outs=(f32, f32), epi=lambda acc, g, xr: (xr + acc * _sig(g), acc),
                    extras=[(gl, 0), (x2, 0)])
    r.update(h=h, u=u, cqn=cqn, ckvn=ckvn, q=q, kn=kn, vb=vb, qr=qr, kr=kr, o=o, lse=lse, ys=ys, pres=pres, m=m, x1=x1,
             h2=h2, a1=a1, act=act, x2=x2, h3=h3, gl=gl, pe=pe, p_bf=p_bf, pool_saved=pool_saved, ssd_saved=ssd_saved,
             lru_saved=lru_saved, la=la, hh=hh, w=w, pr=pr)
    return x3, r


def _gate_bwd(d, g, pre):
    s = _sig(g)
    return d * s, d * pre * s * (1.0 - s)


def _layer_bwd(dx3, r, ctx, l, cosf, sinf, tok, extra_small):
    rows = dx3.shape[0]
    ts = _tiles(rows)
    tm = ts["tm"]
    nm = lambda s: f"{s}_l{l}"
    u, w, pr = r["u"], r["w"], r["pr"]
    mt = _mixer_tiles(u)
    g = {}
    full = lambda a: (a, a.shape[1], 0)
    dpe, dgl = ew(nm("ple_bwd"), _gate_bwd, [full(dx3), full(r["gl"]), full(r["pe"])], [(D_MODEL, bf16)] * 2, tm)
    g["w_ple"] = matmul(nm("d_w_ple"), r["p_bf"], dpe, ta=True, deps=[tok] if tok is not None else [])
    g["w_pg"] = matmul(nm("d_w_pg"), r["h3"], dgl, ta=True)
    dh3 = matmul(nm("d_h3"), dgl, w["w_pg"], tb=True)
    (g["g_ple"],), (dx2,) = seq_bwd(nm("rms_ple_bwd"), f_rms, [pr["g_ple"]], [full(r["x2"])], [True], [], [dh3], [f32], tm,
                                    add_to=(0, dx3))
    da1 = matmul(nm("d_act"), dx2, w["w_ff2"], tb=True, outs=(bf16,),
                 epi=lambda acc, a: (acc * 2.0 * jnp.maximum(a, 0.0),), extras=[(r["a1"], 0)])
    g["w_ff2"] = matmul(nm("d_w_ff2"), r["act"], dx2, ta=True)
    g["w_ff1"] = matmul(nm("d_w_ff1"), r["h2"], da1, ta=True)
    tok = ctx.grads(l, "C", dict(w_ff1=g["w_ff1"], w_ff2=g["w_ff2"], w_ple_gate=g["w_pg"], w_ple=g["w_ple"]))
    dh2 = matmul(nm("d_h2"), da1, w["w_ff1"], tb=True, deps=[tok])
    (g["g_mlp"],), (dx1,) = seq_bwd(nm("rms_mlp_bwd"), f_rms, [pr["g_mlp"]], [full(r["x1"])], [True], [], [dh2], [f32], tm,
                                    add_to=(0, dx2))
    dm = matmul(nm("d_merged"), dx1, w["w_out"], tb=True)
    g["w_out"] = matmul(nm("d_w_out"), r["m"], dx1, ta=True)
    dgates, dys, g["wb"] = [], [], []
    for n in range(4):
        dpre, dgate = ew(nm(f"gate_bwd{n}"), _gate_bwd, [full(dm), (u, D_MODEL, n), full(r["pres"][n])],
                         [(D_MODEL, bf16)] * 2, tm)
        dgates.append(dgate)
        g["wb"].append(matmul(nm(f"d_w_branch{n}"), r["ys"][n], dpre, ta=True))
        dys.append(matmul(nm(f"d_y{n}"), dpre, w["wb"][n], tb=True, outs=(bf16 if n == 0 else f32,)))
    dqr, dkr_, dv = attn_bwd(r["qr"], r["kr"], r["vb"], dys[0], r["o"], r["lse"], ts["ta"])
    _, (dq, dkn, dkrope) = seq_bwd(nm("mla_prep_bwd"), f_prep, [],
                                   [full(r["q"]), full(r["kn"]), mt["kr"], full(cosf), full(sinf)],
                                   [True, True, True, False, False], [], [dqr, dkr_], [bf16] * 3, tm)
    g["w_uq"] = matmul(nm("d_w_uq"), r["cqn"], dq, ta=True)
    g["w_uk"] = matmul(nm("d_w_uk"), r["ckvn"], dkn, ta=True)
    g["w_uv"] = matmul(nm("d_w_uv"), r["ckvn"], dv, ta=True)
    dcqn = matmul(nm("d_cqn"), dq, w["w_uq"], tb=True)
    dckvn = matmul(nm("d_ckvn_k"), dkn, w["w_uk"], tb=True)
    dckvn = matmul(nm("d_ckvn_v"), dv, w["w_uv"], tb=True, epi=lambda acc, prev: (acc + prev,), extras=[(dckvn, 0)])
    (g["q_norm"],), (dcq,) = seq_bwd(nm("rms_q_bwd"), f_rms, [pr["q_norm"]], [mt["cq"]], [True], [], [dcqn], [bf16], tm)
    (g["kv_norm"],), (dckv,) = seq_bwd(nm("rms_kv_bwd"), f_rms, [pr["kv_norm"]], [mt["ckv"]], [True], [], [dckvn], [bf16], tm)
    g["pool"], (dpool,) = seq_bwd(nm("pool_bwd"), f_pool, pr["pool"], [mt["pool"]], [True], r["pool_saved"], [dys[1]],
                                  [bf16], ts["tp"])
    g["ssd"], (dz, dxbc, ddt) = seq_bwd(nm("ssd_bwd"), f_ssd, pr["ssd"], [mt["z"], mt["xbc"], mt["dt"]], [True] * 3,
                                        r["ssd_saved"], [dys[2]], [bf16] * 3, SSD_CHUNK)
    _, (dhh, dlg) = seq_bwd(nm("lru_post_bwd"), f_lru_post, [], [full(r["hh"]), mt["lg"]], [True, True], [], [dys[3]],
                            [f32, bf16], tm)
    da, du = scan_bwd(r["la"], r["hh"], dhh, ts["ts"])
    g["lru"], (dlx,) = seq_bwd(nm("lru_pre_bwd"), f_lru_pre, pr["lru"], [mt["lx"]], [True], r["lru_saved"], [da, du],
                               [bf16], ts["tl"])
    dk = _head_unpad_cols(g["w_uk"], QK_NOPE)
    dv_ = _head_unpad_cols(g["w_uv"], V_HEAD)
    wb0 = g["wb"][0].reshape(N_HEADS, LANE, D_MODEL)[:, :V_HEAD].reshape(MIX, D_MODEL)
    ssd, lru, pool = g["ssd"], g["lru"], g["pool"]
    tok = ctx.grads(l, "B", dict(
        w_uq=_head_unpad_cols(g["w_uq"], QK_NOPE + QK_ROPE).reshape(Q_LORA, -1),
        w_ukv=jnp.concatenate([dk, dv_], axis=2).reshape(KV_LORA, -1), ssd_conv_w=ssd[0], lru_conv_w=lru[0],
        w_branch=jnp.stack([wb0, g["wb"][1], g["wb"][2], g["wb"][3]]), w_out=g["w_out"]))
    du_p = jnp.concatenate(dgates + [dpool, dz, dlg, dlx, dxbc, dcq, dkrope, dckv, ddt,
                                     jnp.zeros((rows, U_COLS - U_DT[0] - LANE), bf16)], axis=1)
    g_w_in = matmul(nm("d_w_in"), r["h"], du_p, ta=True, deps=[tok])
    tok = ctx.grads(l, "A", dict(w_in=_w_in_blocks(g_w_in)))
    dh = matmul(nm("d_h"), du_p, w["w_in"], tb=True, deps=[tok])
    (g_mix,), (dx,) = seq_bwd(nm("rms_in_bwd"), f_rms, [pr["g_mix"]], [full(r["x"])], [True], [], [dh], [f32], tm,
                              add_to=(0, dx1))
    small = dict(
        g_mix=g_mix[0], q_norm=g["q_norm"][0], kv_norm=g["kv_norm"][0],
        w_pool=pool[0].reshape(4, LANE, LANE), pool_scale=pool[1][0],
        ssd_conv_b=ssd[1][0], ssd_dt_bias=ssd[2][0, :8], ssd_a_log=ssd[3][0, :8], ssd_d=ssd[4][0, :8], ssd_norm=ssd[5][0],
        lru_conv_b=lru[1][0], lru_w_a=_block_diag_inv(lru[2]), lru_b_a=lru[3][0], lru_w_i=_block_diag_inv(lru[4]),
        lru_b_i=lru[5][0], lru_lambda=lru[6][0], g_mlp=g["g_mlp"][0], g_ple=g["g_ple"][0], **extra_small)
    tok = ctx.small(l, small)
    return dx, tok


def _rope_tables(positions):
    inv = 1.0 / (ROPE_THETA ** (jnp.arange(0, QK_ROPE, 2, dtype=f32) / QK_ROPE))
    ang = positions.astype(f32)[:, None] * inv
    cos, sin = jnp.cos(ang), jnp.sin(ang)
    rows = positions.shape[0]
    pad = jnp.zeros((rows, LANE - KR_LANE - QK_ROPE), f32)
    cosf = jnp.concatenate([jnp.ones((rows, KR_LANE), f32), cos, cos, pad], axis=1)
    sinf = jnp.concatenate([jnp.zeros((rows, KR_LANE), f32), -sin, sin, pad], axis=1)
    return cosf, sinf


WEIGHTS = ['g_mix', 'w_in', 'q_norm', 'w_uq', 'kv_norm', 'w_ukv', 'w_pool', 'pool_scale', 'ssd_conv_w', 'ssd_conv_b',
           'ssd_dt_bias', 'ssd_a_log', 'ssd_d', 'ssd_norm', 'lru_conv_w', 'lru_conv_b', 'lru_w_a', 'lru_b_a', 'lru_w_i',
           'lru_b_i', 'lru_lambda', 'w_branch', 'w_out', 'g_mlp', 'w_ff1', 'w_ff2', 'g_ple', 'w_ple_gate', 'w_ple', 'g_final']
SHARDED = dict(w_in=2, w_uq=2, w_ukv=2, ssd_conv_w=2, lru_conv_w=2, w_branch=3, w_out=1, w_ff1=2, w_ff2=1,
               w_ple_gate=1, w_ple=2)
F32_PAYLOAD = ("ssd_conv_w", "lru_conv_w")
DEPTH = 2


SMALL = [n for n in WEIGHTS if n not in SHARDED and n != "g_final"]


def local_step(x, p, positions, tgt, sp, ctx):
    cosf, sinf = _rope_tables(positions)
    res = []
    for l in range(DEPTH):
        x, r = _layer_fwd(x, p[l].astype(bf16), ctx, l, _layer_params(sp, l), cosf, sinf)
        res.append(r)
    loss8, dx, dgf = loss_head(x, tgt, sp["g_final"][None, :], _pick(x.shape[0], 512))
    tok = None
    for l in reversed(range(DEPTH)):
        dx, tok = _layer_bwd(dx, res[l], ctx, l, cosf, sinf, tok, dict(g_final=dgf[0]) if l == DEPTH - 1 else {})
    return loss8[0, 0], dx


def _payload(name, w):
    return w if name in F32_PAYLOAD else w.astype(bf16)


def _blocks(name, g):
    ax = SHARDED[name] - 1
    shape = list(g.shape)
    shape[ax:ax + 1] = [N_DEV, shape[ax] // N_DEV]
    return _payload(name, jnp.moveaxis(g.reshape(shape), ax, 0))


def _assemble(name, shards):
    ax = SHARDED[name] - 1
    shape = list(shards.shape[1:])
    shape[ax] *= N_DEV
    return jnp.moveaxis(shards, 0, ax).reshape(shape)


class _Exchanges:
    def __init__(self, wts):
        self.wts = wts
        self.ag, self.rs, self.sm = {}, {}, {}
        tok = None
        for l in range(DEPTH):
            for grp, names in GROUPS.items():
                h = exchange_start(f"ag_start_{grp}{l}", [_payload(n, wts[n][l]) for n in names], True,
                                   deps=[] if tok is None else [tok])
                tok = h["token"]
                self.ag[(l, grp)] = h
        self.all_started = tok

    def weights(self, l, grp, after):
        first = (l, grp) == (0, "A")
        got = exchange_wait(f"ag_wait_{grp}{l}", self.ag[(l, grp)], [after, self.all_started] if first else [after])
        out = {}
        for n, a in zip(GROUPS[grp], got):
            out[n] = a if n == "w_in" else _assemble(n, a)
        return out

    def grads(self, l, grp, g):
        h = exchange_start(f"rs_start_{grp}{l}", [g[n] if n == "w_in" else _blocks(n, g[n]) for n in GROUPS[grp]], False)
        self.rs[(l, grp)] = h
        return h["token"]

    def small(self, l, g):
        names = [n for n in SMALL + ["g_final"] if n in g]
        flat = jnp.concatenate([g[n].reshape(-1) for n in names])
        flat = jnp.pad(flat, (0, (-flat.shape[0]) % (8 * LANE))).reshape(-1, LANE)
        h = exchange_start(f"small_start_{l}", [flat], True)
        self.sm[l] = (h, [(n, g[n].shape) for n in names])
        return h["token"]

    def collect(self, groups, after):
        parts = {}
        for grp in groups:
            for l in reversed(range(DEPTH)):
                got = exchange_wait(f"rs_wait_{grp}{l}", self.rs[(l, grp)], [after])
                for n, a in zip(GROUPS[grp], got):
                    parts.setdefault(n, [None] * DEPTH)[l] = a
        return parts

    def collect_small(self, after):
        parts = {}
        for l in reversed(range(DEPTH)):
            h, layout = self.sm[l]
            (got,) = exchange_wait(f"small_wait_{l}", h, [after])
            got = got.reshape(N_DEV, -1)
            off = 0
            for n, shape in layout:
                size = 1
                for d in shape:
                    size *= d
                part = got[:, off:off + size].reshape((N_DEV,) + tuple(shape))
                off += size
                if n == "g_final":
                    parts[n] = [part]
                else:
                    parts.setdefault(n, [None] * DEPTH)[l] = part
        return parts


def kernel(x, p, positions, g_mix, w_in, q_norm, w_uq, kv_norm, w_ukv, w_pool, pool_scale, ssd_conv_w, ssd_conv_b,
           ssd_dt_bias, ssd_a_log, ssd_d, ssd_norm, lru_conv_w, lru_conv_b, lru_w_a, lru_b_a, lru_w_i, lru_b_i,
           lru_lambda, w_branch, w_out, g_mlp, w_ff1, w_ff2, g_ple, w_ple_gate, w_ple, g_final, loss_target, m_g_mix,
           m_w_in, m_q_norm, m_w_uq, m_kv_norm, m_w_ukv, m_w_pool, m_pool_scale, m_ssd_conv_w, m_ssd_conv_b,
           m_ssd_dt_bias, m_ssd_a_log, m_ssd_d, m_ssd_norm, m_lru_conv_w, m_lru_conv_b, m_lru_w_a, m_lru_b_a,
           m_lru_w_i, m_lru_b_i, m_lru_lambda, m_w_branch, m_w_out, m_g_mlp, m_w_ff1, m_w_ff2, m_g_ple, m_w_ple_gate,
           m_w_ple, m_g_final, v_g_mix, v_w_in, v_q_norm, v_w_uq, v_kv_norm, v_w_ukv, v_w_pool, v_pool_scale,
           v_ssd_conv_w, v_ssd_conv_b, v_ssd_dt_bias, v_ssd_a_log, v_ssd_d, v_ssd_norm, v_lru_conv_w, v_lru_conv_b,
           v_lru_w_a, v_lru_b_a, v_lru_w_i, v_lru_b_i, v_lru_lambda, v_w_branch, v_w_out, v_g_mlp, v_w_ff1, v_w_ff2,
           v_g_ple, v_w_ple_gate, v_w_ple, v_g_final):
    given = dict(locals())
    wts = {n: given[n] for n in WEIGHTS}
    ctx = _Exchanges(wts)
    loss, grad_x = local_step(x[0], p[:, 0], positions[0], loss_target[0], wts, ctx)

    def update(parts):
        out = {}
        for n, eight in parts.items():
            w, m, v = wts[n], given["m_" + n], given["v_" + n]
            if n == "g_final":
                out[n] = [a[0] for a in adamw(f"adamw_{n}", eight, w[None], m[None], v[None])]
            else:
                out[n] = adamw(f"adamw_{n}", eight, w, m, v)
        return out

    outs = update(ctx.collect(("C", "B"), grad_x))
    late = outs["w_ff1"][1]
    outs.update(update(ctx.collect(("A",), late)))
    outs.update(update(ctx.collect_small(late)))
    loss = lax.psum(loss, AXES)
    return (loss, grad_x[None], *[outs[n][0] for n in WEIGHTS], *[outs[n][1] for n in WEIGHTS],
            *[outs[n][2] for n in WEIGHTS], *[outs[n][3] for n in WEIGHTS])
```

```python
import functools

import jax
import jax.numpy as jnp
from jax import lax
from jax.experimental import pallas as pl
from jax.experimental.pallas import tpu as pltpu

f32 = jnp.float32
bf16 = jnp.bfloat16

D_MODEL = 1024
MIX = 512
N_HEADS = 8
QK_NOPE, QK_ROPE, V_HEAD = 64, 32, 64
Q_LORA, KV_LORA = 384, 256
ROPE_THETA = 10000.0
POOL_WINDOWS = (2, 4, 8, 16)
SSD_CHUNK = 128
SSD_XBC = 768
CONV_W = 4
LRU_C = 8.0
D_FF = 4096
EPS = 1e-6
IN_COLS = 7592
ADAM_LR, ADAM_B1, ADAM_B2, ADAM_EPS, ADAM_WD, ADAM_STEP = 0.001, 0.9, 0.999, 1e-08, 0.01, 10

LANE = 128
HALO = 8
POOL_HALO = 16
VMEM_LIMIT = 56 * 1024 * 1024
MATMUL_VMEM_BUDGET = 32 * 1024 * 1024
N_DEV = 8
AXES = ("x", "y", "c")

U_COLS = 8192
U_GATES, U_POOL, U_Z, U_LG, U_LX, U_XBC, U_CQ, U_KR, U_CKV, U_DT = (
    (0, 4096), (4096, 512), (4608, 512), (5120, 512), (5632, 512), (6144, 768),
    (6912, 384), (7296, 128), (7424, 256), (7680, 128))
KR_LANE = 64


def _cp(sem):
    return pltpu.CompilerParams(dimension_semantics=sem, vmem_limit_bytes=VMEM_LIMIT)


def _pick(dim, pref):
    if dim <= pref:
        return dim
    t = pref
    while t >= LANE:
        if dim % t == 0:
            return t
        t -= LANE
    t = pref
    while dim % t:
        t -= 8
    return t


@functools.partial(jax.custom_vjp, nondiff_argnums=(1,))
def shift_down(x, k):
    row = lax.broadcasted_iota(jnp.int32, x.shape, 0)
    return jnp.where(row >= k, pltpu.roll(x, k, 0), 0.0)


def _shift_down_fwd(x, k):
    return shift_down(x, k), None


def _shift_down_bwd(k, _, g):
    r = g.shape[0]
    row = lax.broadcasted_iota(jnp.int32, g.shape, 0)
    return (jnp.where(row < r - k, pltpu.roll(g, r - k, 0), 0.0),)


shift_down.defvjp(_shift_down_fwd, _shift_down_bwd)


@functools.partial(jax.custom_vjp, nondiff_argnums=(1,))
def lane_roll(x, s):
    return pltpu.roll(x, s, 1)


def _lane_roll_fwd(x, s):
    return lane_roll(x, s), None


def _lane_roll_bwd(s, _, g):
    return (pltpu.roll(g, (g.shape[1] - s) % g.shape[1], 1),)


lane_roll.defvjp(_lane_roll_fwd, _lane_roll_bwd)


def _tile_spec(tm, width, cb, n=None):
    if n is None:
        return pl.BlockSpec((tm, width), lambda i: (i, cb))
    return pl.BlockSpec((tm, width), lambda i: (n - 1 - i, cb))


def _const_spec(shape):
    nd = len(shape)
    return pl.BlockSpec(shape, lambda i: (0,) * nd)


def seq_fwd(name, f, params, tiles, carries, outs, tm):
    rows = tiles[0][0].shape[0]
    n = rows // tm
    np_, nt, no, nc = len(params), len(tiles), len(outs), len(carries)

    def body(*refs):
        p_refs = refs[:np_]
        t_refs = refs[np_:np_ + nt]
        o_refs = refs[np_ + nt:np_ + nt + no]
        s_refs = refs[np_ + nt + no:np_ + nt + no + nc]
        c_refs = refs[np_ + nt + no + nc:]
        i = pl.program_id(0)

        @pl.when(i == 0)
        def _():
            for c in c_refs:
                c[...] = jnp.zeros_like(c)

        cvals = [c[...] for c in c_refs]
        for s, c in zip(s_refs, cvals):
            s[0] = c
        o, newc = f(i, [r[...] for r in p_refs], cvals, [r[...].astype(f32) for r in t_refs])
        for r, v in zip(o_refs, o):
            r[...] = v.astype(r.dtype)
        for r, v in zip(c_refs, newc):
            r[...] = v

    in_specs = [_const_spec(p.shape) for p in params] + [_tile_spec(tm, w, cb) for (_, w, cb) in tiles]
    out_specs = [_tile_spec(tm, w, 0) for (w, _) in outs]
    out_specs += [pl.BlockSpec((1,) + tuple(c), lambda i, nd=len(c): (i,) + (0,) * nd) for c in carries]
    out_shape = [jax.ShapeDtypeStruct((rows, w), dt) for (w, dt) in outs]
    out_shape += [jax.ShapeDtypeStruct((n,) + tuple(c), f32) for c in carries]
    res = pl.pallas_call(
        body, name=name, grid=(n,), in_specs=in_specs, out_specs=out_specs, out_shape=out_shape,
        scratch_shapes=[pltpu.VMEM(tuple(c), f32) for c in carries],
        compiler_params=_cp(("arbitrary",)),
    )(*params, *[t[0] for t in tiles])
    return list(res[:no]), list(res[no:])


def seq_bwd(name, f, params, tiles, diff, saved, douts, gdtypes, tm, add_to=None):
    rows = tiles[0][0].shape[0]
    n = rows // tm
    np_, nt, nc, nd = len(params), len(tiles), len(saved), len(douts)
    didx = [k for k, d in enumerate(diff) if d]
    ng = len(didx)
    has_add = add_to is not None

    def body(*refs):
        p_refs = refs[:np_]
        t_refs = refs[np_:np_ + nt]
        s_refs = refs[np_ + nt:np_ + nt + nc]
        d_refs = refs[np_ + nt + nc:np_ + nt + nc + nd]
        pos = np_ + nt + nc + nd
        a_ref = refs[pos] if has_add else None
        pos += 1 if has_add else 0
        dp_refs = refs[pos:pos + np_]
        dt_refs = refs[pos + np_:pos + np_ + ng]
        dc_refs = refs[pos + np_ + ng:]
        i = pl.program_id(0)
        step = n - 1 - i

        @pl.when(i == 0)
        def _():
            for r in dp_refs:
                r[...] = jnp.zeros_like(r)
            for r in dc_refs:
                r[...] = jnp.zeros_like(r)

        pvals = [r[...] for r in p_refs]
        cvals = [r[0] for r in s_refs]
        xvals = [r[...].astype(f32) for r in t_refs]

        def fn(p, c, xd):
            x = list(xvals)
            for k, v in zip(didx, xd):
                x[k] = v
            return f(step, p, c, x)

        _, vjp = jax.vjp(fn, pvals, cvals, [xvals[k] for k in didx])
        dp, dc, dx = vjp(([r[...].astype(f32) for r in d_refs], [r[...] for r in dc_refs]))
        for r, v in zip(dp_refs, dp):
            r[...] += v
        for r, v in zip(dc_refs, dc):
            r[...] = v
        for k, (r, v) in enumerate(zip(dt_refs, dx)):
            if has_add and k == add_to[0]:
                v = v + a_ref[...].astype(f32)
            r[...] = v.astype(r.dtype)

    in_specs = [_const_spec(p.shape) for p in params] + [_tile_spec(tm, w, cb, n) for (_, w, cb) in tiles]
    in_specs += [pl.BlockSpec((1,) + tuple(s.shape[1:]), lambda i, nd_=s.ndim - 1: (n - 1 - i,) + (0,) * nd_) for s in saved]
    in_specs += [_tile_spec(tm, d.shape[1], 0, n) for d in douts]
    args = list(params) + [t[0] for t in tiles] + list(saved) + list(douts)
    if has_add:
        in_specs.append(_tile_spec(tm, add_to[1].shape[1], 0, n))
        args.append(add_to[1])
    out_specs = [_const_spec(p.shape) for p in params] + [_tile_spec(tm, tiles[k][1], 0, n) for k in didx]
    out_shape = [jax.ShapeDtypeStruct(p.shape, f32) for p in params]
    out_shape += [jax.ShapeDtypeStruct((rows, tiles[k][1]), dt) for k, dt in zip(didx, gdtypes)]
    res = pl.pallas_call(
        body, name=name, grid=(n,), in_specs=in_specs, out_specs=out_specs, out_shape=out_shape,
        scratch_shapes=[pltpu.VMEM(tuple(s.shape[1:]), f32) for s in saved],
        compiler_params=_cp(("arbitrary",)),
    )(*args)
    return list(res[:np_]), list(res[np_:])


def matmul(name, a, b, *, ta=False, tb=False, outs=(f32,), epi=None, extras=(), deps=(), tm=512, tn=1024, tk=1024):
    m, k = (a.shape[1], a.shape[0]) if ta else a.shape
    n = b.shape[0] if tb else b.shape[1]
    tm, tn, tk = _pick(m, tm), _pick(n, tn), _pick(k, tk)

    def vmem_bytes(tm_):
        per_out = sum(jnp.dtype(dt).itemsize for dt in outs) + sum(e[0].dtype.itemsize for e in extras)
        acc = 4 if k // tk > 1 else 0
        return 2 * (tm_ * tk * a.dtype.itemsize + tk * tn * b.dtype.itemsize + tm_ * tn * per_out) + tm_ * tn * acc

    while tm < m and m % (2 * tm) == 0 and vmem_bytes(2 * tm) <= MATMUL_VMEM_BUDGET:
        tm *= 2
    nk = k // tk
    ne = len(extras)
    dims = (((0 if ta else 1,), (1 if tb else 0,)), ((), ()))

    def body(*refs):
        a_ref, b_ref = refs[0], refs[1]
        e_refs = refs[2:2 + ne]
        o_refs = refs[2 + ne + len(deps):2 + ne + len(deps) + len(outs)]
        kk = pl.program_id(2)
        part = lax.dot_general(a_ref[...].astype(bf16), b_ref[...].astype(bf16), dims, preferred_element_type=f32)

        def finish(total):
            res = (total,) if epi is None else epi(total, *[e[...] for e in e_refs])
            for r, v in zip(o_refs, res):
                r[...] = v.astype(r.dtype)

        if nk == 1:
            finish(part)
            return
        acc = refs[-1]

        @pl.when(kk == 0)
        def _():
            acc[...] = part

        @pl.when(jnp.logical_and(kk > 0, kk < nk - 1))
        def _():
            acc[...] += part

        @pl.when(kk == nk - 1)
        def _():
            finish(acc[...] + part)

    a_spec = pl.BlockSpec((tk, tm), lambda i, j, q: (q, i)) if ta else pl.BlockSpec((tm, tk), lambda i, j, q: (i, q))
    b_spec = pl.BlockSpec((tn, tk), lambda i, j, q: (j, q)) if tb else pl.BlockSpec((tk, tn), lambda i, j, q: (q, j))
    assert all(off % tn == 0 for (_, off) in extras)
    e_specs = [pl.BlockSpec((tm, tn), lambda i, j, q, off=off // tn: (i, off + j)) for (_, off) in extras]
    res = pl.pallas_call(
        body, name=name, grid=(m // tm, n // tn, nk),
        in_specs=[a_spec, b_spec] + e_specs + [pl.BlockSpec(memory_space=pl.ANY) for _ in deps],
        out_specs=[pl.BlockSpec((tm, tn), lambda i, j, q: (i, j)) for _ in outs],
        out_shape=[jax.ShapeDtypeStruct((m, n), dt) for dt in outs],
        scratch_shapes=[pltpu.VMEM((tm, tn), f32)] if nk > 1 else [],
        compiler_params=_cp(("parallel", "parallel", "arbitrary")),
    )(a, b, *[e[0] for e in extras], *deps)
    return res[0] if len(outs) == 1 else tuple(res)


ATT_SCALE = (QK_NOPE + QK_ROPE) ** -0.5
LN2 = 0.6931471805599453
ATT_C = ATT_SCALE / LN2
NT = (((1,), (1,)), ((), ()))
TN = (((0,), (0,)), ((), ()))


def _causal(tq, tk):
    return lax.broadcasted_iota(jnp.int32, (tq, tk), 0) >= lax.broadcasted_iota(jnp.int32, (tq, tk), 1)


def _tri_pairs(n, by_column):
    if by_column:
        pairs = [(i, j) for j in range(n) for i in range(j, n)]
    else:
        pairs = [(i, j) for i in range(n) for j in range(i + 1)]
    return (jnp.asarray([a for a, _ in pairs], jnp.int32), jnp.asarray([b for _, b in pairs], jnp.int32))


HEADS_PER_STEP = 4
HEAD_PAIR = HEADS_PER_STEP * LANE


def attn_fwd(q, k, v, t):
    rows = q.shape[0]
    n = rows // t
    it, jt = _tri_pairs(n, False)

    def body(it_ref, jt_ref, q_ref, k_ref, v_ref, o_ref, lse_ref, m_s, l_s, acc_s):
        s_id = pl.program_id(1)
        i, j = it_ref[s_id], jt_ref[s_id]

        @pl.when(j == 0)
        def _():
            m_s[...] = jnp.full_like(m_s, -jnp.inf)
            l_s[...] = jnp.zeros_like(l_s)
            acc_s[...] = jnp.zeros_like(acc_s)

        def step(diag):
            for hh in range(HEADS_PER_STEP):
                sl = slice(LANE * hh, LANE * (hh + 1))
                s = lax.dot_general(q_ref[:, sl], k_ref[:, sl], NT, preferred_element_type=f32)
                if diag:
                    s = jnp.where(_causal(t, t), s, -jnp.inf)
                m_prev = m_s[:, sl]
                m_new = jnp.maximum(m_prev, jnp.max(s, axis=1, keepdims=True))
                alpha = jnp.exp2(m_prev - m_new)
                p = jnp.exp2(s - m_new[:, :1])
                l_s[:, sl] = alpha * l_s[:, sl] + jnp.sum(p, axis=1, keepdims=True)
                acc_s[:, sl] = alpha * acc_s[:, sl] + jnp.dot(p.astype(bf16), v_ref[:, sl], preferred_element_type=f32)
                m_s[:, sl] = m_new

        pl.when(j < i)(lambda: step(False))

        @pl.when(j == i)
        def _():
            step(True)
            o_ref[...] = (acc_s[...] / l_s[...]).astype(o_ref.dtype)
            lse_ref[...] = m_s[...] + jnp.log2(l_s[...])

    qs = pl.BlockSpec((t, HEAD_PAIR), lambda h, s, it_, jt_: (it_[s], h))
    ks = pl.BlockSpec((t, HEAD_PAIR), lambda h, s, it_, jt_: (jt_[s], h))
    hw = N_HEADS * LANE
    return pl.pallas_call(
        body, name="attn_fwd",
        grid_spec=pltpu.PrefetchScalarGridSpec(
            num_scalar_prefetch=2, grid=(hw // HEAD_PAIR, it.shape[0]), in_specs=[qs, ks, ks], out_specs=[qs, qs],
            scratch_shapes=[pltpu.VMEM((t, HEAD_PAIR), f32)] * 3),
        out_shape=[jax.ShapeDtypeStruct((rows, hw), bf16), jax.ShapeDtypeStruct((rows, hw), f32)],
        compiler_params=_cp(("parallel", "arbitrary")),
    )(it, jt, q, k, v)


def attn_bwd(q, k, v, do, o, lse, t):
    rows = q.shape[0]
    n = rows // t
    it, jt = _tri_pairs(n, True)

    def body(it_ref, jt_ref, q_ref, k_ref, v_ref, do_ref, o_ref, lse_ref, dq_ref, dk_ref, dv_ref, dk_s, dv_s):
        s_id = pl.program_id(1)
        i, j = it_ref[s_id], jt_ref[s_id]

        @pl.when(s_id == 0)
        def _():
            dq_ref[...] = jnp.zeros_like(dq_ref)

        @pl.when(i == j)
        def _():
            dk_s[...] = jnp.zeros_like(dk_s)
            dv_s[...] = jnp.zeros_like(dv_s)

        q_rows = pl.ds(pl.multiple_of(i * t, t), t)

        def step(diag):
            for hh in range(HEADS_PER_STEP):
                sl = slice(LANE * hh, LANE * (hh + 1))
                qh, kh, vh, doh = q_ref[:, sl], k_ref[:, sl], v_ref[:, sl], do_ref[:, sl]
                s = lax.dot_general(qh, kh, NT, preferred_element_type=f32)
                p = jnp.exp2(s - lse_ref[:, sl][:, :1])
                if diag:
                    p = jnp.where(_causal(t, t), p, 0.0)
                dp = lax.dot_general(doh, vh, NT, preferred_element_type=f32)
                delta = jnp.sum(doh.astype(f32) * o_ref[:, sl].astype(f32), axis=1, keepdims=True)
                ds = (p * (dp - delta) * LN2).astype(bf16)
                dv_s[:, sl] += lax.dot_general(p.astype(bf16), doh, TN, preferred_element_type=f32)
                dk_s[:, sl] += lax.dot_general(ds, qh, TN, preferred_element_type=f32)
                dq_ref[q_rows, sl] += jnp.dot(ds, kh, preferred_element_type=f32)

        pl.when(i > j)(lambda: step(False))
        pl.when(i == j)(lambda: step(True))

        @pl.when(i == n - 1)
        def _():
            dk_ref[...] = dk_s[...]
            dv_ref[...] = dv_s[...]

    qs = pl.BlockSpec((t, HEAD_PAIR), lambda h, s, it_, jt_: (it_[s], h))
    ks = pl.BlockSpec((t, HEAD_PAIR), lambda h, s, it_, jt_: (jt_[s], h))
    dqs = pl.BlockSpec((rows, HEAD_PAIR), lambda h, s, it_, jt_: (0, h))
    hw = N_HEADS * LANE
    return pl.pallas_call(
        body, name="attn_bwd",
        grid_spec=pltpu.PrefetchScalarGridSpec(
            num_scalar_prefetch=2, grid=(hw // HEAD_PAIR, it.shape[0]), in_specs=[qs, ks, ks, qs, qs, qs],
            out_specs=[dqs, ks, ks], scratch_shapes=[pltpu.VMEM((t, HEAD_PAIR), f32)] * 2),
        out_shape=[jax.ShapeDtypeStruct((rows, hw), f32)] * 3,
        compiler_params=_cp(("parallel", "arbitrary")),
    )(it, jt, q, k, v, do, o, lse)


def _steps(tm):
    k, out = 1, []
    while k < tm:
        out.append(k)
        k *= 2
    return out


def scan_fwd(a, u, tm):
    rows, ch = a.shape
    n = rows // tm

    def body(a_ref, u_ref, h_ref, h_s):
        @pl.when(pl.program_id(0) == 0)
        def _():
            h_s[...] = jnp.zeros_like(h_s)

        av, bv = a_ref[...], u_ref[...]
        row = lax.broadcasted_iota(jnp.int32, av.shape, 0)
        for k in _steps(tm):
            a_sh = jnp.where(row >= k, pltpu.roll(av, k, 0), 1.0)
            b_sh = jnp.where(row >= k, pltpu.roll(bv, k, 0), 0.0)
            bv = av * b_sh + bv
            av = av * a_sh
        h = bv + av * h_s[HALO - 1:HALO, :]
        h_ref[...] = h
        h_s[...] = h[tm - HALO:, :]

    spec = pl.BlockSpec((tm, ch), lambda i: (i, 0))
    return pl.pallas_call(
        body, name="lru_scan_fwd", grid=(n,), in_specs=[spec, spec], out_specs=spec,
        out_shape=jax.ShapeDtypeStruct((rows, ch), f32), scratch_shapes=[pltpu.VMEM((HALO, ch), f32)],
        compiler_params=_cp(("arbitrary",)),
    )(a, u)


def scan_bwd(a, h, dh, tm):
    rows, ch = a.shape
    n = rows // tm
    per = tm // HALO

    def body(a_ref, h_ref, hp_ref, dh_ref, da_ref, du_ref, g_s, a_s):
        i = pl.program_id(0)
        step = n - 1 - i

        @pl.when(i == 0)
        def _():
            g_s[...] = jnp.zeros_like(g_s)
            a_s[...] = jnp.zeros_like(a_s)

        a0 = a_ref[...]
        row = lax.broadcasted_iota(jnp.int32, a0.shape, 0)
        av = jnp.where(row < tm - 1, pltpu.roll(a0, tm - 1, 0), a_s[0:1, :])
        bv = dh_ref[...]
        for k in _steps(tm):
            a_sh = jnp.where(row < tm - k, pltpu.roll(av, tm - k, 0), 1.0)
            b_sh = jnp.where(row < tm - k, pltpu.roll(bv, tm - k, 0), 0.0)
            bv = bv + av * b_sh
            av = av * a_sh
        g = bv + av * g_s[0:1, :]
        h_last = jnp.where(step > 0, hp_ref[HALO - 1:HALO, :], 0.0)
        h_prev = jnp.where(row >= 1, pltpu.roll(h_ref[...], 1, 0), h_last)
        du_ref[...] = g
        da_ref[...] = g * h_prev
        g_s[...] = g[0:HALO, :]
        a_s[...] = a0[0:HALO, :]

    spec = pl.BlockSpec((tm, ch), lambda i: (n - 1 - i, 0))
    hp_spec = pl.BlockSpec((HALO, ch), lambda i: (jnp.maximum((n - 1 - i) * per - 1, 0), 0))
    return pl.pallas_call(
        body, name="lru_scan_bwd", grid=(n,), in_specs=[spec, spec, hp_spec, spec], out_specs=[spec, spec],
        out_shape=[jax.ShapeDtypeStruct((rows, ch), f32)] * 2,
        scratch_shapes=[pltpu.VMEM((HALO, ch), f32)] * 2,
        compiler_params=_cp(("arbitrary",)),
    )(a, h, h, dh)


def _rms(x, g):
    return x * lax.rsqrt(jnp.mean(x * x, axis=-1, keepdims=True) + EPS) * g


def f_rms(step, p, c, x):
    return [_rms(x[0], p[0])], []


def _rope(x, cosf, sinf):
    lane = lax.broadcasted_iota(jnp.int32, x.shape, 1)
    sw = jnp.where(lane < KR_LANE + QK_ROPE // 2, lane_roll(x, LANE - QK_ROPE // 2), lane_roll(x, QK_ROPE // 2))
    return x * cosf + sw * sinf


def f_prep(step, p, c, x):
    q, kn, kr, cosf, sinf = x
    kr_rot = _rope(kr, cosf, sinf)
    qr = [_rope(q[:, LANE * h:LANE * (h + 1)], cosf, sinf) * ATT_C for h in range(N_HEADS)]
    kk = [kn[:, LANE * h:LANE * (h + 1)] + kr_rot for h in range(N_HEADS)]
    return [jnp.concatenate(qr, axis=1), jnp.concatenate(kk, axis=1)], []


def _conv(tail, x, w, b):
    xf = jnp.concatenate([tail, x], axis=0)
    acc = b + w[CONV_W - 1:CONV_W, :] * xf
    for k in range(CONV_W - 1):
        acc = acc + w[k:k + 1, :] * shift_down(xf, CONV_W - 1 - k)
    return acc[HALO:, :]


def f_pool(step, p, c, x):
    wp, sc = p
    (tail,) = c
    (u,) = x
    tm = u.shape[0]
    xf = jnp.concatenate([tail, u], axis=0)
    sums, s, w = [], xf, 1
    while w < POOL_WINDOWS[-1]:
        s = s + shift_down(s, w)
        w *= 2
        sums.append(s)
    t = step * tm + lax.broadcasted_iota(jnp.int32, (tm, 1), 0)
    ys = []
    for g, (w, s) in enumerate(zip(POOL_WINDOWS, sums)):
        sl = slice(LANE * g, LANE * (g + 1))
        cnt = jnp.minimum(t + 1, w).astype(f32)
        d = s[POOL_HALO:, sl] / cnt - u[:, sl]
        ys.append(jnp.dot(d.astype(bf16), wp[LANE * g:LANE * (g + 1), :].astype(bf16), preferred_element_type=f32))
    return [jnp.concatenate(ys, axis=1) * sc], [u[tm - POOL_HALO:, :]]


def f_ssd(step, p, c, x):
    conv_w, conv_b, dtb, alog, dsk, ng = p
    tail, s_in = c[0], c[1:]
    z, xbc, dt = x
    ln = z.shape[0]
    xc = jax.nn.silu(_conv(tail, xbc, conv_w, conv_b))
    xs, bb, cc = xc[:, :MIX], xc[:, MIX:MIX + LANE], xc[:, MIX + LANE:]
    dtv = jax.nn.softplus(dt + dtb[0:1, :])
    a = dtv * -jnp.exp(alog[0:1, :])
    ri = lax.broadcasted_iota(jnp.int32, (ln, ln), 0)
    ci = lax.broadcasted_iota(jnp.int32, (ln, ln), 1)
    tril = (ri >= ci).astype(f32)
    triu = (ri <= ci).astype(f32)
    hi = lax.Precision.HIGHEST
    a_cs = jnp.dot(tril, a, precision=hi, preferred_element_type=f32)
    a_cs_t = lax.dot_general(a, triu, TN, precision=hi, preferred_element_type=f32)
    a_tot = jnp.sum(a, axis=0, keepdims=True)
    lane = lax.broadcasted_iota(jnp.int32, (1, LANE), 1)
    half = [(lane < 64).astype(f32), (lane >= 64).astype(f32)]
    hrow = lax.broadcasted_iota(jnp.int32, (LANE, 1), 0)

    def head(v, h):
        return jnp.sum(v * (lane == h).astype(f32), axis=1, keepdims=True)

    def pair(v, j):
        return head(v, 2 * j) * half[0] + head(v, 2 * j + 1) * half[1]

    cg = [(cc * half[g]).astype(bf16) for g in range(2)]
    bg = [(bb * half[g]).astype(bf16) for g in range(2)]
    cb = [lax.dot_general(cg[g], bg[g], NT, preferred_element_type=f32) for g in range(2)]
    ys, s_out = [], []
    for j in range(4):
        g = j // 2
        xs_j = xs[:, LANE * j:LANE * (j + 1)]
        xj = xs_j * pair(dtv, j)
        yj = xs_j * pair(dsk[0:1, :], j)
        for hh in range(2):
            h = 2 * j + hh
            rowv = jnp.sum(a_cs_t * (hrow == h).astype(f32), axis=0, keepdims=True)
            lmat = jnp.exp(jnp.where(ri >= ci, head(a_cs, h) - rowv, -jnp.inf))
            yj = yj + jnp.dot((cb[g] * lmat).astype(bf16), (xj * half[hh]).astype(bf16), preferred_element_type=f32)
        acs = pair(a_cs, j)
        tot = pair(a_tot, j)
        yj = yj + jnp.exp(acs) * jnp.dot(cg[g], s_in[j].astype(bf16), preferred_element_type=f32)
        s_new = jnp.exp(tot) * s_in[j] + lax.dot_general(bg[g], (xj * jnp.exp(tot - acs)).astype(bf16), TN,
                                                         preferred_element_type=f32)
        ys.append(yj)
        s_out.append(s_new)
    y = jnp.concatenate(ys, axis=1) * jax.nn.silu(z)
    return [_rms(y, ng)], [xbc[ln - HALO:, :]] + s_out


def _neg_expm1(y):
    series = -y * (1.0 + y * (0.5 + y * (1.0 / 6 + y * (1.0 / 24 + y * (1.0 / 120)))))
    return jnp.where(y > -0.05, series, 1.0 - jnp.exp(y))


def f_lru_pre(step, p, c, x):
    cw, cb_, wa, ba, wi, bi, lam = p
    (tail,) = c
    (lx,) = x
    tm = lx.shape[0]
    xc = _conv(tail, lx, cw, cb_)
    xb = xc.astype(bf16)
    r = jax.nn.sigmoid(jnp.dot(xb, wa.astype(bf16), preferred_element_type=f32) + ba)
    it = jax.nn.sigmoid(jnp.dot(xb, wi.astype(bf16), preferred_element_type=f32) + bi)
    log_a = -LRU_C * r * jax.nn.softplus(-lam)
    mult = jnp.sqrt(_neg_expm1(2.0 * log_a))
    return [jnp.exp(log_a), xc * it * mult], [lx[tm - HALO:, :]]


def f_lru_post(step, p, c, x):
    h, g = x
    return [h * jax.nn.gelu(g)], []


def loss_head(x, tgt, g, tm):
    rows, d = x.shape
    n = rows // tm

    def body(x_ref, t_ref, g_ref, loss_ref, dx_ref, dg_ref):
        @pl.when(pl.program_id(0) == 0)
        def _():
            loss_ref[...] = jnp.zeros_like(loss_ref)
            dg_ref[...] = jnp.zeros_like(dg_ref)

        def fn(gv, xv):
            err = _rms(xv, gv) - t_ref[...]
            return 0.5 * jnp.sum(jnp.mean(err * err, axis=-1, keepdims=True))

        val, (dg, dx) = jax.value_and_grad(fn, argnums=(0, 1))(g_ref[...], x_ref[...])
        loss_ref[...] += val
        dg_ref[...] += dg
        dx_ref[...] = dx

    spec = pl.BlockSpec((tm, d), lambda i: (i, 0))
    return pl.pallas_call(
        body, name="loss_head", grid=(n,), in_specs=[spec, spec, _const_spec((1, d))],
        out_specs=[_const_spec((8, LANE)), spec, _const_spec((1, d))],
        out_shape=[jax.ShapeDtypeStruct((8, LANE), f32), jax.ShapeDtypeStruct((rows, d), f32),
                   jax.ShapeDtypeStruct((1, d), f32)],
        compiler_params=_cp(("arbitrary",)),
    )(x, tgt, g)


def ew(name, fn, ins, outs, tm):
    rows = ins[0][0].shape[0]
    ni = len(ins)

    def body(*refs):
        res = fn(*[r[...].astype(f32) for r in refs[:ni]])
        for r, v in zip(refs[ni:], res):
            r[...] = v.astype(r.dtype)

    return pl.pallas_call(
        body, name=name, grid=(rows // tm,), in_specs=[_tile_spec(tm, w, cb) for (_, w, cb) in ins],
        out_specs=[_tile_spec(tm, w, 0) for (w, _) in outs],
        out_shape=[jax.ShapeDtypeStruct((rows, w), dt) for (w, dt) in outs],
        compiler_params=_cp(("parallel",)),
    )(*[t[0] for t in ins])


def _peers():
    x, y, c = lax.axis_index("x"), lax.axis_index("y"), lax.axis_index("c")
    me = 4 * x + 2 * y + c
    out = []
    for k in range(1, N_DEV):
        px = 1 - x if k & 4 else x
        py = 1 - y if k & 2 else y
        pc = 1 - c if k & 1 else c
        out.append(((px, py, pc), 4 * px + 2 * py + pc))
    return me, out


_HBM = pl.BlockSpec(memory_space=pltpu.HBM)
_SEM = pl.BlockSpec(memory_space=pltpu.SEMAPHORE)
_EFFECT = pltpu.SideEffectType.DATAFLOW_SIDE_EFFECTING


def _remote(src_ref, land_ref, gather, me, pid, dev, send_sems, recv_sems, k, recv_side):
    return pltpu.make_async_remote_copy(
        src_ref=src_ref if gather else src_ref.at[pid], dst_ref=land_ref.at[pid if recv_side else me],
        send_sem=send_sems.at[k], recv_sem=recv_sems.at[k], device_id=dev, device_id_type=pl.DeviceIdType.MESH)


def exchange_start(name, srcs, gather, deps=()):
    n, nd = len(srcs), len(deps)
    shapes = [(s.shape if gather else s.shape[1:]) for s in srcs]
    lands = [lax.empty((N_DEV,) + tuple(sh), s.dtype) for s, sh in zip(srcs, shapes)]

    def body(*refs):
        src_refs, land_refs = refs[:n], refs[n:2 * n]
        send_sems, recv_sems = refs[2 * n + nd], refs[2 * n + nd + 1]
        token = refs[-1]
        me, peers = _peers()
        for k, (dev, pid) in enumerate(peers):
            for s_ref, l_ref in zip(src_refs, land_refs):
                _remote(s_ref, l_ref, gather, me, pid, dev, send_sems, recv_sems, k, False).start()
        token[...] = jnp.zeros_like(token)

    hbm = lambda a: pltpu.with_memory_space_constraint(a, pltpu.HBM)
    res = pl.pallas_call(
        body, name=name,
        out_shape=(pltpu.SemaphoreType.DMA((N_DEV - 1,)), pltpu.SemaphoreType.DMA((N_DEV - 1,)),
                   *[pltpu.HBM(a.shape, a.dtype) for a in list(srcs) + lands], jax.ShapeDtypeStruct((8, LANE), f32)),
        in_specs=[_HBM] * (2 * n) + [pl.BlockSpec(memory_space=pl.ANY)] * nd,
        out_specs=(_SEM, _SEM, *([_HBM] * (2 * n)), pl.BlockSpec(memory_space=pltpu.VMEM)),
        input_output_aliases={i: 2 + i for i in range(2 * n)},
        compiler_params=pltpu.CompilerParams(has_side_effects=_EFFECT),
    )(*[hbm(a) for a in list(srcs) + lands], *deps)
    return dict(sems=res[:2], srcs=list(res[2:2 + n]), lands=list(res[2 + n:2 + 2 * n]), token=res[-1], gather=gather)


def exchange_wait(name, h, afters):
    n, gather = len(h["srcs"]), h["gather"]

    def body(*refs):
        src_refs, land_refs = refs[:n], refs[n:2 * n]
        send_sems, recv_sems = refs[2 * n], refs[2 * n + 1]
        me, peers = _peers()
        for k, (dev, pid) in enumerate(peers):
            for s_ref, l_ref in zip(src_refs, land_refs):
                _remote(s_ref, l_ref, gather, me, pid, dev, send_sems, recv_sems, k, True).wait_recv()
        for k, (dev, pid) in enumerate(peers):
            for s_ref, l_ref in zip(src_refs, land_refs):
                _remote(s_ref, l_ref, gather, me, pid, dev, send_sems, recv_sems, k, False).wait_send()

    arrs = h["srcs"] + h["lands"]
    res = pl.pallas_call(
        body, name=name, out_shape=tuple(pltpu.HBM(a.shape, a.dtype) for a in arrs),
        in_specs=[_HBM] * (2 * n) + [_SEM, _SEM] + [pl.BlockSpec(memory_space=pl.ANY)] * len(afters),
        out_specs=tuple([_HBM] * (2 * n)), input_output_aliases={i: i for i in range(2 * n)},
        compiler_params=pltpu.CompilerParams(has_side_effects=_EFFECT),
    )(*arrs, *h["sems"], *afters)
    me = 4 * lax.axis_index("x") + 2 * lax.axis_index("y") + lax.axis_index("c")
    out = []
    for src, land in zip(res[:n], res[n:]):
        mine = src if gather else lax.dynamic_index_in_dim(src, me, 0, keepdims=False)
        out.append(lax.dynamic_update_index_in_dim(land, mine, me, 0))
    return out


def adamw(name, parts, w, m, v):
    nl = len(parts)
    shape = w.shape[1:]
    c = shape[-1]
    r = 1
    for s in shape[:-1]:
        r *= s
    tr = _pick(r, 256) if r % 8 == 0 else r
    nb = r // tr
    parts2 = [p.reshape(N_DEV, r, c) for p in parts]
    w2, m2, v2 = (a.reshape(nl, r, c) for a in (w, m, v))

    def body(*refs):
        p_refs = refs[:nl]
        w_ref, m_ref, v_ref, g_ref, d_ref, nm_ref, nv_ref = refs[nl:]
        layer = pl.program_id(0)
        for ll, p_ref in enumerate(p_refs):
            @pl.when(layer == ll)
            def _(p_ref=p_ref):
                g = p_ref[0].astype(f32)
                for i in range(1, N_DEV):
                    g = g + p_ref[i].astype(f32)
                mn = ADAM_B1 * m_ref[0] + (1.0 - ADAM_B1) * g
                vn = ADAM_B2 * v_ref[0] + (1.0 - ADAM_B2) * jnp.square(g)
                m_hat = mn / (1.0 - ADAM_B1 ** ADAM_STEP)
                v_hat = vn / (1.0 - ADAM_B2 ** ADAM_STEP)
                g_ref[0] = g
                d_ref[0] = -ADAM_LR * (m_hat / (jnp.sqrt(v_hat) + ADAM_EPS) + ADAM_WD * w_ref[0])
                nm_ref[0] = mn
                nv_ref[0] = vn

    def p_spec(ll):
        return pl.BlockSpec((N_DEV, tr, c), lambda l, i: (0, jnp.where(l == ll, i, jnp.where(l > ll, nb - 1, 0)), 0))

    spec = pl.BlockSpec((1, tr, c), lambda l, i: (l, i, 0))
    res = pl.pallas_call(
        body, name=name, grid=(nl, nb), in_specs=[p_spec(ll) for ll in range(nl)] + [spec, spec, spec],
        out_specs=[spec] * 4, out_shape=[jax.ShapeDtypeStruct((nl, r, c), f32)] * 4,
        compiler_params=_cp(("arbitrary", "arbitrary")),
    )(*parts2, w2, m2, v2)
    return [a.reshape(w.shape) for a in res]


_IN_SPLITS = dict(cq=(0, 384), ckv=(384, 640), kr=(640, 672), pool=(672, 1184), z=(1184, 1696), xbc=(1696, 2464),
                  dt=(2464, 2472), lg=(2472, 2984), lx=(2984, 3496), gates=(3496, 7592))


W_IN_SHARD = IN_COLS // N_DEV

_PAD_ORDER = ("gates", "pool", "z", "lg", "lx", "xbc", "cq", KR_LANE, "kr", LANE - KR_LANE - QK_ROPE, "ckv", "dt",
              LANE - 8, U_COLS - U_DT[0] - LANE)
_SEGMENTS = ((0, U_CQ[0], 384), (384, U_CKV[0], 256), (640, U_KR[0] + KR_LANE, QK_ROPE), (672, U_POOL[0], 512),
             (1184, U_Z[0], 512), (1696, U_XBC[0], 768), (2464, U_DT[0], 8), (2472, U_LG[0], 512), (2984, U_LX[0], 512),
             (3496, 0, 4096))


def _pad_w_in(shards):
    rows = shards.shape[1]
    pieces = []
    for item in _PAD_ORDER:
        if isinstance(item, int):
            pieces.append(jnp.zeros((rows, item), shards.dtype))
            continue
        a, b = _IN_SPLITS[item]
        for d in range(a // W_IN_SHARD, (b - 1) // W_IN_SHARD + 1):
            lo, hi = max(a, d * W_IN_SHARD), min(b, (d + 1) * W_IN_SHARD)
            pieces.append(shards[d, :, lo - d * W_IN_SHARD:hi - d * W_IN_SHARD])
    return jnp.concatenate(pieces, axis=1)


def _w_in_blocks(g):
    blocks = []
    for d in range(N_DEV):
        a, b = d * W_IN_SHARD, (d + 1) * W_IN_SHARD
        pieces = []
        for ref, pad, width in _SEGMENTS:
            lo, hi = max(a, ref), min(b, ref + width)
            if lo < hi:
                pieces.append(g[:, pad + lo - ref:pad + hi - ref])
        blocks.append(jnp.concatenate(pieces, axis=1))
    return jnp.stack(blocks).astype(bf16)


def _head_pad_cols(w, per, lo, hi):
    k = w.shape[0]
    w = w.reshape(k, N_HEADS, per)[:, :, lo:hi]
    return jnp.pad(w, ((0, 0), (0, 0), (0, LANE - (hi - lo)))).reshape(k, N_HEADS * LANE)


def _head_unpad_cols(g, n):
    k = g.shape[0]
    return g.reshape(k, N_HEADS, LANE)[:, :, :n]


def _block_diag(w):
    out = jnp.zeros((MIX, MIX), w.dtype)
    for i in range(8):
        out = lax.dynamic_update_slice(out, w[i], (64 * i, 64 * i))
    return out


def _block_diag_inv(g):
    return jnp.stack([g[64 * i:64 * (i + 1), 64 * i:64 * (i + 1)] for i in range(8)])


def _head8(v):
    return jnp.zeros((8, LANE), f32).at[0, :8].set(v)


GROUPS = dict(A=("w_in",), B=("w_uq", "w_ukv", "ssd_conv_w", "lru_conv_w", "w_branch", "w_out"),
              C=("w_ff1", "w_ff2", "w_ple_gate", "w_ple"))


def _kernel_weights(grp, fw):
    if grp == "A":
        return dict(w_in=_pad_w_in(fw["w_in"]))
    if grp == "C":
        return dict(w_ff1=fw["w_ff1"], w_ff2=fw["w_ff2"], w_pg=fw["w_ple_gate"], w_ple=fw["w_ple"])
    wb = fw["w_branch"]
    wb0 = jnp.pad(wb[0].reshape(N_HEADS, V_HEAD, D_MODEL), ((0, 0), (0, LANE - V_HEAD), (0, 0))).reshape(N_HEADS * LANE, D_MODEL)
    return dict(
        w_uq=_head_pad_cols(fw["w_uq"], QK_NOPE + QK_ROPE, 0, QK_NOPE + QK_ROPE),
        w_uk=_head_pad_cols(fw["w_ukv"], QK_NOPE + V_HEAD, 0, QK_NOPE),
        w_uv=_head_pad_cols(fw["w_ukv"], QK_NOPE + V_HEAD, QK_NOPE, QK_NOPE + V_HEAD),
        wb=[wb0, wb[1], wb[2], wb[3]], w_out=fw["w_out"], ssd_conv_w=fw["ssd_conv_w"], lru_conv_w=fw["lru_conv_w"])


def _layer_params(sp, l):
    row = lambda n: sp[n][l][None, :]
    return dict(
        g_mix=row("g_mix"), q_norm=row("q_norm"), kv_norm=row("kv_norm"),
        pool=[sp["w_pool"][l].reshape(4 * LANE, LANE), row("pool_scale")],
        ssd=[None, row("ssd_conv_b"), _head8(sp["ssd_dt_bias"][l]), _head8(sp["ssd_a_log"][l]),
             _head8(sp["ssd_d"][l]), row("ssd_norm")],
        lru=[None, row("lru_conv_b"), _block_diag(sp["lru_w_a"][l]), row("lru_b_a"),
             _block_diag(sp["lru_w_i"][l]), row("lru_b_i"), row("lru_lambda")],
        g_mlp=row("g_mlp"), g_ple=row("g_ple"),
    )


_sig = jax.nn.sigmoid
_SSD_CARRY = [(HALO, SSD_XBC)] + [(LANE, LANE)] * 4


def _tiles(rows):
    return dict(tm=_pick(rows, 512), ta=_pick(rows, 512), tp=_pick(rows, 256), tl=_pick(rows, 256), ts=_pick(rows, 256))


def _mixer_tiles(u):
    return dict(
        cq=(u, 384, U_CQ[0] // 384), ckv=(u, 256, U_CKV[0] // 256), kr=(u, LANE, U_KR[0] // LANE),
        pool=(u, MIX, U_POOL[0] // MIX), z=(u, MIX, U_Z[0] // MIX), xbc=(u, SSD_XBC, U_XBC[0] // SSD_XBC),
        dt=(u, LANE, U_DT[0] // LANE), lg=(u, MIX, U_LG[0] // MIX), lx=(u, MIX, U_LX[0] // MIX))


def _layer_fwd(x, p_bf, ctx, l, pr, cosf, sinf):
    rows = x.shape[0]
    ts = _tiles(rows)
    tm = ts["tm"]
    nm = lambda s: f"{s}_l{l}"
    r = dict(x=x)
    (h,), _ = seq_fwd(nm("rms_in"), f_rms, [pr["g_mix"]], [(x, D_MODEL, 0)], [], [(D_MODEL, bf16)], tm)
    w = dict(_kernel_weights("A", ctx.weights(l, "A", h)))
    u = matmul(nm("w_in"), h, w["w_in"])
    mt = _mixer_tiles(u)
    (cqn,), _ = seq_fwd(nm("rms_q"), f_rms, [pr["q_norm"]], [mt["cq"]], [], [(Q_LORA, bf16)], tm)
    (ckvn,), _ = seq_fwd(nm("rms_kv"), f_rms, [pr["kv_norm"]], [mt["ckv"]], [], [(KV_LORA, bf16)], tm)
    (yb,), pool_saved = seq_fwd(nm("pool"), f_pool, pr["pool"], [mt["pool"]], [(POOL_HALO, MIX)], [(MIX, bf16)], ts["tp"])
    w.update(_kernel_weights("B", ctx.weights(l, "B", yb)))
    pr = dict(pr, ssd=[w["ssd_conv_w"]] + pr["ssd"][1:], lru=[w["lru_conv_w"]] + pr["lru"][1:])
    q = matmul(nm("w_uq"), cqn, w["w_uq"])
    kn = matmul(nm("w_uk"), ckvn, w["w_uk"])
    vb = matmul(nm("w_uv"), ckvn, w["w_uv"], outs=(bf16,))
    hw = N_HEADS * LANE
    (qr, kr), _ = seq_fwd(nm("mla_prep"), f_prep, [], [(q, hw, 0), (kn, hw, 0), mt["kr"], (cosf, LANE, 0), (sinf, LANE, 0)],
                          [], [(hw, bf16), (hw, bf16)], tm)
    o, lse = attn_fwd(qr, kr, vb, ts["ta"])
    (yc,), ssd_saved = seq_fwd(nm("ssd"), f_ssd, pr["ssd"], [mt["z"], mt["xbc"], mt["dt"]], _SSD_CARRY, [(MIX, bf16)], SSD_CHUNK)
    (la, lu), lru_saved = seq_fwd(nm("lru_pre"), f_lru_pre, pr["lru"], [mt["lx"]], [(HALO, MIX)], [(MIX, f32), (MIX, f32)], ts["tl"])
    hh = scan_fwd(la, lu, ts["ts"])
    (yd,), _ = seq_fwd(nm("lru_post"), f_lru_post, [], [(hh, MIX, 0), mt["lg"]], [], [(MIX, bf16)], tm)
    ys = [o, yb, yc, yd]
    pres = []
    m = None
    for n in range(4):
        last = n == 3
        if n == 0:
            epi = lambda acc, g: (_sig(g) * acc, acc)
            extras = [(u, 0)]
        else:
            epi = lambda acc, g, prev: (prev + _sig(g) * acc, acc)
            extras = [(u, D_MODEL * n), (m, 0)]
        m, pre = matmul(nm(f"branch{n}"), ys[n], w["wb"][n], outs=(bf16 if last else f32, bf16), epi=epi, extras=extras)
        pres.append(pre)
    x1 = matmul(nm("w_out"), m, w["w_out"], epi=lambda acc, xr: (acc + xr,), extras=[(x, 0)])
    (h2,), _ = seq_fwd(nm("rms_mlp"), f_rms, [pr["g_mlp"]], [(x1, D_MODEL, 0)], [], [(D_MODEL, bf16)], tm)
    w.update(_kernel_weights("C", ctx.weights(l, "C", h2)))
    a1, act = matmul(nm("ff1"), h2, w["w_ff1"], outs=(bf16, bf16), epi=lambda acc: (acc, jnp.square(jnp.maximum(acc, 0.0))))
    x2 = matmul(nm("ff2"), act, w["w_ff2"], epi=lambda acc, xr: (acc + xr,), extras=[(x1, 0)])
    (h3,), _ = seq_fwd(nm("rms_ple"), f_rms, [pr["g_ple"]], [(x2, D_MODEL, 0)], [], [(D_MODEL, bf16)], tm)
    gl = matmul(nm("ple_gate"), h3, w["w_pg"])
    x3, pe = matmul(nm("ple"), p_bf, w["w_ple"], outs=(f32, f32), epi=lambda acc, g, xr: (xr + acc * _sig(g), acc),
                    extras=[(gl, 0), (x2, 0)])
    r.update(h=h, u=u, cqn=cqn, ckvn=ckvn, q=q, kn=kn, vb=vb, qr=qr, kr=kr, o=o, lse=lse, ys=ys, pres=pres, m=m, x1=x1,
             h2=h2, a1=a1, act=act, x2=x2, h3=h3, gl=gl, pe=pe, p_bf=p_bf, pool_saved=pool_saved, ssd_saved=ssd_saved,
             lru_saved=lru_saved, la=la, hh=hh, w=w, pr=pr)
    return x3, r


def _gate_bwd(d, g, pre):
    s = _sig(g)
    return d * s, d * pre * s * (1.0 - s)


def _layer_bwd(dx3, r, ctx, l, cosf, sinf, tok, extra_small):
    rows = dx3.shape[0]
    ts = _tiles(rows)
    tm = ts["tm"]
    nm = lambda s: f"{s}_l{l}"
    u, w, pr = r["u"], r["w"], r["pr"]
    mt = _mixer_tiles(u)
    g = {}
    full = lambda a: (a, a.shape[1], 0)
    dpe, dgl = ew(nm("ple_bwd"), _gate_bwd, [full(dx3), full(r["gl"]), full(r["pe"])], [(D_MODEL, bf16)] * 2, tm)
    g["w_ple"] = matmul(nm("d_w_ple"), r["p_bf"], dpe, ta=True, deps=[tok] if tok is not None else [])
    g["w_pg"] = matmul(nm("d_w_pg"), r["h3"], dgl, ta=True)
    dh3 = matmul(nm("d_h3"), dgl, w["w_pg"], tb=True)
    (g["g_ple"],), (dx2,) = seq_bwd(nm("rms_ple_bwd"), f_rms, [pr["g_ple"]], [full(r["x2"])], [True], [], [dh3], [f32], tm,
                                    add_to=(0, dx3))
    da1 = matmul(nm("d_act"), dx2, w["w_ff2"], tb=True, outs=(bf16,),
                 epi=lambda acc, a: (acc * 2.0 * jnp.maximum(a, 0.0),), extras=[(r["a1"], 0)])
    g["w_ff2"] = matmul(nm("d_w_ff2"), r["act"], dx2, ta=True)
    g["w_ff1"] = matmul(nm("d_w_ff1"), r["h2"], da1, ta=True)
    tok = ctx.grads(l, "C", dict(w_ff1=g["w_ff1"], w_ff2=g["w_ff2"], w_ple_gate=g["w_pg"], w_ple=g["w_ple"]))
    dh2 = matmul(nm("d_h2"), da1, w["w_ff1"], tb=True, deps=[tok])
    (g["g_mlp"],), (dx1,) = seq_bwd(nm("rms_mlp_bwd"), f_rms, [pr["g_mlp"]], [full(r["x1"])], [True], [], [dh2], [f32], tm,
                                    add_to=(0, dx2))
    dm = matmul(nm("d_merged"), dx1, w["w_out"], tb=True)
    g["w_out"] = matmul(nm("d_w_out"), r["m"], dx1, ta=True)
    dgates, dys, g["wb"] = [], [], []
    for n in range(4):
        dpre, dgate = ew(nm(f"gate_bwd{n}"), _gate_bwd, [full(dm), (u, D_MODEL, n), full(r["pres"][n])],
                         [(D_MODEL, bf16)] * 2, tm)
        dgates.append(dgate)
        g["wb"].append(matmul(nm(f"d_w_branch{n}"), r["ys"][n], dpre, ta=True))
        dys.append(matmul(nm(f"d_y{n}"), dpre, w["wb"][n], tb=True, outs=(bf16 if n == 0 else f32,)))
    dqr, dkr_, dv = attn_bwd(r["qr"], r["kr"], r["vb"], dys[0], r["o"], r["lse"], ts["ta"])
    _, (dq, dkn, dkrope) = seq_bwd(nm("mla_prep_bwd"), f_prep, [],
                                   [full(r["q"]), full(r["kn"]), mt["kr"], full(cosf), full(sinf)],
                                   [True, True, True, False, False], [], [dqr, dkr_], [bf16] * 3, tm)
    g["w_uq"] = matmul(nm("d_w_uq"), r["cqn"], dq, ta=True)
    g["w_uk"] = matmul(nm("d_w_uk"), r["ckvn"], dkn, ta=True)
    g["w_uv"] = matmul(nm("d_w_uv"), r["ckvn"], dv, ta=True)
    dcqn = matmul(nm("d_cqn"), dq, w["w_uq"], tb=True)
    dckvn = matmul(nm("d_ckvn_k"), dkn, w["w_uk"], tb=True)
    dckvn = matmul(nm("d_ckvn_v"), dv, w["w_uv"], tb=True, epi=lambda acc, prev: (acc + prev,), extras=[(dckvn, 0)])
    (g["q_norm"],), (dcq,) = seq_bwd(nm("rms_q_bwd"), f_rms, [pr["q_norm"]], [mt["cq"]], [True], [], [dcqn], [bf16], tm)
    (g["kv_norm"],), (dckv,) = seq_bwd(nm("rms_kv_bwd"), f_rms, [pr["kv_norm"]], [mt["ckv"]], [True], [], [dckvn], [bf16], tm)
    g["pool"], (dpool,) = seq_bwd(nm("pool_bwd"), f_pool, pr["pool"], [mt["pool"]], [True], r["pool_saved"], [dys[1]],
                                  [bf16], ts["tp"])
    g["ssd"], (dz, dxbc, ddt) = seq_bwd(nm("ssd_bwd"), f_ssd, pr["ssd"], [mt["z"], mt["xbc"], mt["dt"]], [True] * 3,
                                        r["ssd_saved"], [dys[2]], [bf16] * 3, SSD_CHUNK)
    _, (dhh, dlg) = seq_bwd(nm("lru_post_bwd"), f_lru_post, [], [full(r["hh"]), mt["lg"]], [True, True], [], [dys[3]],
                            [f32, bf16], tm)
    da, du = scan_bwd(r["la"], r["hh"], dhh, ts["ts"])
    g["lru"], (dlx,) = seq_bwd(nm("lru_pre_bwd"), f_lru_pre, pr["lru"], [mt["lx"]], [True], r["lru_saved"], [da, du],
                               [bf16], ts["tl"])
    dk = _head_unpad_cols(g["w_uk"], QK_NOPE)
    dv_ = _head_unpad_cols(g["w_uv"], V_HEAD)
    wb0 = g["wb"][0].reshape(N_HEADS, LANE, D_MODEL)[:, :V_HEAD].reshape(MIX, D_MODEL)
    ssd, lru, pool = g["ssd"], g["lru"], g["pool"]
    tok = ctx.grads(l, "B", dict(
        w_uq=_head_unpad_cols(g["w_uq"], QK_NOPE + QK_ROPE).reshape(Q_LORA, -1),
        w_ukv=jnp.concatenate([dk, dv_], axis=2).reshape(KV_LORA, -1), ssd_conv_w=ssd[0], lru_conv_w=lru[0],
        w_branch=jnp.stack([wb0, g["wb"][1], g["wb"][2], g["wb"][3]]), w_out=g["w_out"]))
    du_p = jnp.concatenate(dgates + [dpool, dz, dlg, dlx, dxbc, dcq, dkrope, dckv, ddt,
                                     jnp.zeros((rows, U_COLS - U_DT[0] - LANE), bf16)], axis=1)
    g_w_in = matmul(nm("d_w_in"), r["h"], du_p, ta=True, deps=[tok])
    tok = ctx.grads(l, "A", dict(w_in=_w_in_blocks(g_w_in)))
    dh = matmul(nm("d_h"), du_p, w["w_in"], tb=True, deps=[tok])
    (g_mix,), (dx,) = seq_bwd(nm("rms_in_bwd"), f_rms, [pr["g_mix"]], [full(r["x"])], [True], [], [dh], [f32], tm,
                              add_to=(0, dx1))
    small = dict(
        g_mix=g_mix[0], q_norm=g["q_norm"][0], kv_norm=g["kv_norm"][0],
        w_pool=pool[0].reshape(4, LANE, LANE), pool_scale=pool[1][0],
        ssd_conv_b=ssd[1][0], ssd_dt_bias=ssd[2][0, :8], ssd_a_log=ssd[3][0, :8], ssd_d=ssd[4][0, :8], ssd_norm=ssd[5][0],
        lru_conv_b=lru[1][0], lru_w_a=_block_diag_inv(lru[2]), lru_b_a=lru[3][0], lru_w_i=_block_diag_inv(lru[4]),
        lru_b_i=lru[5][0], lru_lambda=lru[6][0], g_mlp=g["g_mlp"][0], g_ple=g["g_ple"][0], **extra_small)
    tok = ctx.small(l, small)
    return dx, tok


def _rope_tables(positions):
    inv = 1.0 / (ROPE_THETA ** (jnp.arange(0, QK_ROPE, 2, dtype=f32) / QK_ROPE))
    ang = positions.astype(f32)[:, None] * inv
    cos, sin = jnp.cos(ang), jnp.sin(ang)
    rows = positions.shape[0]
    pad = jnp.zeros((rows, LANE - KR_LANE - QK_ROPE), f32)
    cosf = jnp.concatenate([jnp.ones((rows, KR_LANE), f32), cos, cos, pad], axis=1)
    sinf = jnp.concatenate([jnp.zeros((rows, KR_LANE), f32), -sin, sin, pad], axis=1)
    return cosf, sinf


WEIGHTS = ['g_mix', 'w_in', 'q_norm', 'w_uq', 'kv_norm', 'w_ukv', 'w_pool', 'pool_scale', 'ssd_conv_w', 'ssd_conv_b',
           'ssd_dt_bias', 'ssd_a_log', 'ssd_d', 'ssd_norm', 'lru_conv_w', 'lru_conv_b', 'lru_w_a', 'lru_b_a', 'lru_w_i',
           'lru_b_i', 'lru_lambda', 'w_branch', 'w_out', 'g_mlp', 'w_ff1', 'w_ff2', 'g_ple', 'w_ple_gate', 'w_ple', 'g_final']
SHARDED = dict(w_in=2, w_uq=2, w_ukv=2, ssd_conv_w=2, lru_conv_w=2, w_branch=3, w_out=1, w_ff1=2, w_ff2=1,
               w_ple_gate=1, w_ple=2)
F32_PAYLOAD = ("ssd_conv_w", "lru_conv_w")
DEPTH = 2


SMALL = [n for n in WEIGHTS if n not in SHARDED and n != "g_final"]


def local_step(x, p, positions, tgt, sp, ctx):
    cosf, sinf = _rope_tables(positions)
    res = []
    for l in range(DEPTH):
        x, r = _layer_fwd(x, p[l].astype(bf16), ctx, l, _layer_params(sp, l), cosf, sinf)
        res.append(r)
    loss8, dx, dgf = loss_head(x, tgt, sp["g_final"][None, :], _pick(x.shape[0], 512))
    tok = None
    for l in reversed(range(DEPTH)):
        dx, tok = _layer_bwd(dx, res[l], ctx, l, cosf, sinf, tok, dict(g_final=dgf[0]) if l == DEPTH - 1 else {})
    return loss8[0, 0], dx


def _payload(name, w):
    return w if name in F32_PAYLOAD else w.astype(bf16)


def _blocks(name, g):
    ax = SHARDED[name] - 1
    shape = list(g.shape)
    shape[ax:ax + 1] = [N_DEV, shape[ax] // N_DEV]
    return _payload(name, jnp.moveaxis(g.reshape(shape), ax, 0))


def _assemble(name, shards):
    ax = SHARDED[name] - 1
    shape = list(shards.shape[1:])
    shape[ax] *= N_DEV
    return jnp.moveaxis(shards, 0, ax).reshape(shape)


class _Exchanges:
    def __init__(self, wts):
        self.wts = wts
        self.ag, self.rs, self.sm = {}, {}, {}
        tok = None
        for l in range(DEPTH):
            for grp, names in GROUPS.items():
                h = exchange_start(f"ag_start_{grp}{l}", [_payload(n, wts[n][l]) for n in names], True,
                                   deps=[] if tok is None else [tok])
                tok = h["token"]
                self.ag[(l, grp)] = h
        self.all_started = tok

    def weights(self, l, grp, after):
        first = (l, grp) == (0, "A")
        got = exchange_wait(f"ag_wait_{grp}{l}", self.ag[(l, grp)], [after, self.all_started] if first else [after])
        out = {}
        for n, a in zip(GROUPS[grp], got):
            out[n] = a if n == "w_in" else _assemble(n, a)
        return out

    def grads(self, l, grp, g):
        h = exchange_start(f"rs_start_{grp}{l}", [g[n] if n == "w_in" else _blocks(n, g[n]) for n in GROUPS[grp]], False)
        self.rs[(l, grp)] = h
        return h["token"]

    def small(self, l, g):
        names = [n for n in SMALL + ["g_final"] if n in g]
        flat = jnp.concatenate([g[n].reshape(-1) for n in names])
        flat = jnp.pad(flat, (0, (-flat.shape[0]) % (8 * LANE))).reshape(-1, LANE)
        h = exchange_start(f"small_start_{l}", [flat], True)
        self.sm[l] = (h, [(n, g[n].shape) for n in names])
        return h["token"]

    def collect(self, groups, after):
        parts = {}
        for grp in groups:
            for l in reversed(range(DEPTH)):
                got = exchange_wait(f"rs_wait_{grp}{l}", self.rs[(l, grp)], [after])
                for n, a in zip(GROUPS[grp], got):
                    parts.setdefault(n, [None] * DEPTH)[l] = a
        return parts

    def collect_small(self, after):
        parts = {}
        for l in reversed(range(DEPTH)):
            h, layout = self.sm[l]
            (got,) = exchange_wait(f"small_wait_{l}", h, [after])
            got = got.reshape(N_DEV, -1)
            off = 0
            for n, shape in layout:
                size = 1
                for d in shape:
                    size *= d
                part = got[:, off:off + size].reshape((N_DEV,) + tuple(shape))
                off += size
                if n == "g_final":
                    parts[n] = [part]
                else:
                    parts.setdefault(n, [None] * DEPTH)[l] = part
        return parts


def kernel(x, p, positions, g_mix, w_in, q_norm, w_uq, kv_norm, w_ukv, w_pool, pool_scale, ssd_conv_w, ssd_conv_b,
           ssd_dt_bias, ssd_a_log, ssd_d, ssd_norm, lru_conv_w, lru_conv_b, lru_w_a, lru_b_a, lru_w_i, lru_b_i,
           lru_lambda, w_branch, w_out, g_mlp, w_ff1, w_ff2, g_ple, w_ple_gate, w_ple, g_final, loss_target, m_g_mix,
           m_w_in, m_q_norm, m_w_uq, m_kv_norm, m_w_ukv, m_w_pool, m_pool_scale, m_ssd_conv_w, m_ssd_conv_b,
           m_ssd_dt_bias, m_ssd_a_log, m_ssd_d, m_ssd_norm, m_lru_conv_w, m_lru_conv_b, m_lru_w_a, m_lru_b_a,
           m_lru_w_i, m_lru_b_i, m_lru_lambda, m_w_branch, m_w_out, m_g_mlp, m_w_ff1, m_w_ff2, m_g_ple, m_w_ple_gate,
           m_w_ple, m_g_final, v_g_mix, v_w_in, v_q_norm, v_w_uq, v_kv_norm, v_w_ukv, v_w_pool, v_pool_scale,
           v_ssd_conv_w, v_ssd_conv_b, v_ssd_dt_bias, v_ssd_a_log, v_ssd_d, v_ssd_norm, v_lru_conv_w, v_lru_conv_b,
           v_lru_w_a, v_lru_b_a, v_lru_w_i, v_lru_b_i, v_lru_lambda, v_w_branch, v_w_out, v_g_mlp, v_w_ff1, v_w_ff2,
           v_g_ple, v_w_ple_gate, v_w_ple, v_g_final):
    given = dict(locals())
    wts = {n: given[n] for n in WEIGHTS}
    ctx = _Exchanges(wts)
    loss, grad_x = local_step(x[0], p[:, 0], positions[0], loss_target[0], wts, ctx)

    def update(parts):
        out = {}
        for n, eight in parts.items():
            w, m, v = wts[n], given["m_" + n], given["v_" + n]
            if n == "g_final":
                out[n] = [a[0] for a in adamw(f"adamw_{n}", eight, w[None], m[None], v[None])]
            else:
                out[n] = adamw(f"adamw_{n}", eight, w, m, v)
        return out

    outs = update(ctx.collect(("C", "B"), grad_x))
    late = outs["w_ff1"][1]
    outs.update(update(ctx.collect(("A",), late)))
    outs.update(update(ctx.collect_small(late)))
    loss = lax.psum(loss, AXES)
    return (loss, grad_x[None], *[outs[n][0] for n in WEIGHTS], *[outs[n][1] for n in WEIGHTS],
            *[outs[n][2] for n in WEIGHTS], *[outs[n][3] for n in WEIGHTS])
```

```python
import functools

import jax
import jax.numpy as jnp
from jax import lax
from jax.experimental import pallas as pl
from jax.experimental.pallas import tpu as pltpu

f32 = jnp.float32
bf16 = jnp.bfloat16

D_MODEL = 1024
MIX = 512
N_HEADS = 8
QK_NOPE, QK_ROPE, V_HEAD = 64, 32, 64
Q_LORA, KV_LORA = 384, 256
ROPE_THETA = 10000.0
POOL_WINDOWS = (2, 4, 8, 16)
SSD_CHUNK = 128
SSD_XBC = 768
CONV_W = 4
LRU_C = 8.0
D_FF = 4096
EPS = 1e-6
IN_COLS = 7592
ADAM_LR, ADAM_B1, ADAM_B2, ADAM_EPS, ADAM_WD, ADAM_STEP = 0.001, 0.9, 0.999, 1e-08, 0.01, 10

LANE = 128
HALO = 8
POOL_HALO = 16
VMEM_LIMIT = 56 * 1024 * 1024
MATMUL_MAX_K_TILE = 4096
MATMUL_VMEM_BUDGET = 32 * 1024 * 1024
N_DEV = 8
AXES = ("x", "y", "c")

U_COLS = 8192
U_GATES, U_POOL, U_Z, U_LG, U_LX, U_XBC, U_CQ, U_KR, U_CKV, U_DT = (
    (0, 4096), (4096, 512), (4608, 512), (5120, 512), (5632, 512), (6144, 768),
    (6912, 384), (7296, 128), (7424, 256), (7680, 128))
KR_LANE = 64


def _cp(sem):
    return pltpu.CompilerParams(dimension_semantics=sem, vmem_limit_bytes=VMEM_LIMIT)


def _pick(dim, pref):
    if dim <= pref:
        return dim
    t = pref
    while t >= LANE:
        if dim % t == 0:
            return t
        t -= LANE
    t = pref
    while dim % t:
        t -= 8
    return t


@functools.partial(jax.custom_vjp, nondiff_argnums=(1,))
def shift_down(x, k):
    row = lax.broadcasted_iota(jnp.int32, x.shape, 0)
    return jnp.where(row >= k, pltpu.roll(x, k, 0), 0.0)


def _shift_down_fwd(x, k):
    return shift_down(x, k), None


def _shift_down_bwd(k, _, g):
    r = g.shape[0]
    row = lax.broadcasted_iota(jnp.int32, g.shape, 0)
    return (jnp.where(row < r - k, pltpu.roll(g, r - k, 0), 0.0),)


shift_down.defvjp(_shift_down_fwd, _shift_down_bwd)


@functools.partial(jax.custom_vjp, nondiff_argnums=(1,))
def lane_roll(x, s):
    return pltpu.roll(x, s, 1)


def _lane_roll_fwd(x, s):
    return lane_roll(x, s), None


def _lane_roll_bwd(s, _, g):
    return (pltpu.roll(g, (g.shape[1] - s) % g.shape[1], 1),)


lane_roll.defvjp(_lane_roll_fwd, _lane_roll_bwd)


def _tile_spec(tm, width, cb, n=None):
    if n is None:
        return pl.BlockSpec((tm, width), lambda i: (i, cb))
    return pl.BlockSpec((tm, width), lambda i: (n - 1 - i, cb))


def _const_spec(shape):
    nd = len(shape)
    return pl.BlockSpec(shape, lambda i: (0,) * nd)


def seq_fwd(name, f, params, tiles, carries, outs, tm):
    rows = tiles[0][0].shape[0]
    n = rows // tm
    np_, nt, no, nc = len(params), len(tiles), len(outs), len(carries)

    def body(*refs):
        p_refs = refs[:np_]
        t_refs = refs[np_:np_ + nt]
        o_refs = refs[np_ + nt:np_ + nt + no]
        s_refs = refs[np_ + nt + no:np_ + nt + no + nc]
        c_refs = refs[np_ + nt + no + nc:]
        i = pl.program_id(0)

        @pl.when(i == 0)
        def _():
            for c in c_refs:
                c[...] = jnp.zeros_like(c)

        cvals = [c[...] for c in c_refs]
        for s, c in zip(s_refs, cvals):
            s[0] = c
        o, newc = f(i, [r[...] for r in p_refs], cvals, [r[...].astype(f32) for r in t_refs])
        for r, v in zip(o_refs, o):
            r[...] = v.astype(r.dtype)
        for r, v in zip(c_refs, newc):
            r[...] = v

    in_specs = [_const_spec(p.shape) for p in params] + [_tile_spec(tm, w, cb) for (_, w, cb) in tiles]
    out_specs = [_tile_spec(tm, w, 0) for (w, _) in outs]
    out_specs += [pl.BlockSpec((1,) + tuple(c), lambda i, nd=len(c): (i,) + (0,) * nd) for c in carries]
    out_shape = [jax.ShapeDtypeStruct((rows, w), dt) for (w, dt) in outs]
    out_shape += [jax.ShapeDtypeStruct((n,) + tuple(c), f32) for c in carries]
    res = pl.pallas_call(
        body, name=name, grid=(n,), in_specs=in_specs, out_specs=out_specs, out_shape=out_shape,
        scratch_shapes=[pltpu.VMEM(tuple(c), f32) for c in carries],
        compiler_params=_cp(("arbitrary",)),
    )(*params, *[t[0] for t in tiles])
    return list(res[:no]), list(res[no:])


def seq_bwd(name, f, params, tiles, diff, saved, douts, gdtypes, tm, add_to=None):
    rows = tiles[0][0].shape[0]
    n = rows // tm
    np_, nt, nc, nd = len(params), len(tiles), len(saved), len(douts)
    didx = [k for k, d in enumerate(diff) if d]
    ng = len(didx)
    has_add = add_to is not None

    def body(*refs):
        p_refs = refs[:np_]
        t_refs = refs[np_:np_ + nt]
        s_refs = refs[np_ + nt:np_ + nt + nc]
        d_refs = refs[np_ + nt + nc:np_ + nt + nc + nd]
        pos = np_ + nt + nc + nd
        a_ref = refs[pos] if has_add else None
        pos += 1 if has_add else 0
        dp_refs = refs[pos:pos + np_]
        dt_refs = refs[pos + np_:pos + np_ + ng]
        dc_refs = refs[pos + np_ + ng:]
        i = pl.program_id(0)
        step = n - 1 - i

        @pl.when(i == 0)
        def _():
            for r in dp_refs:
                r[...] = jnp.zeros_like(r)
            for r in dc_refs:
                r[...] = jnp.zeros_like(r)

        pvals = [r[...] for r in p_refs]
        cvals = [r[0] for r in s_refs]
        xvals = [r[...].astype(f32) for r in t_refs]

        def fn(p, c, xd):
            x = list(xvals)
            for k, v in zip(didx, xd):
                x[k] = v
            return f(step, p, c, x)

        _, vjp = jax.vjp(fn, pvals, cvals, [xvals[k] for k in didx])
        dp, dc, dx = vjp(([r[...].astype(f32) for r in d_refs], [r[...] for r in dc_refs]))
        for r, v in zip(dp_refs, dp):
            r[...] += v
        for r, v in zip(dc_refs, dc):
            r[...] = v
        for k, (r, v) in enumerate(zip(dt_refs, dx)):
            if has_add and k == add_to[0]:
                v = v + a_ref[...].astype(f32)
            r[...] = v.astype(r.dtype)

    in_specs = [_const_spec(p.shape) for p in params] + [_tile_spec(tm, w, cb, n) for (_, w, cb) in tiles]
    in_specs += [pl.BlockSpec((1,) + tuple(s.shape[1:]), lambda i, nd_=s.ndim - 1: (n - 1 - i,) + (0,) * nd_) for s in saved]
    in_specs += [_tile_spec(tm, d.shape[1], 0, n) for d in douts]
    args = list(params) + [t[0] for t in tiles] + list(saved) + list(douts)
    if has_add:
        in_specs.append(_tile_spec(tm, add_to[1].shape[1], 0, n))
        args.append(add_to[1])
    out_specs = [_const_spec(p.shape) for p in params] + [_tile_spec(tm, tiles[k][1], 0, n) for k in didx]
    out_shape = [jax.ShapeDtypeStruct(p.shape, f32) for p in params]
    out_shape += [jax.ShapeDtypeStruct((rows, tiles[k][1]), dt) for k, dt in zip(didx, gdtypes)]
    res = pl.pallas_call(
        body, name=name, grid=(n,), in_specs=in_specs, out_specs=out_specs, out_shape=out_shape,
        scratch_shapes=[pltpu.VMEM(tuple(s.shape[1:]), f32) for s in saved],
        compiler_params=_cp(("arbitrary",)),
    )(*args)
    return list(res[:np_]), list(res[np_:])


def matmul(name, a, b, *, ta=False, tb=False, outs=(f32,), epi=None, extras=(), deps=()):
    m, k = (a.shape[1], a.shape[0]) if ta else a.shape
    n = b.shape[0] if tb else b.shape[1]
    per_out = sum(jnp.dtype(dt).itemsize for dt in outs) + sum(e[0].dtype.itemsize for e in extras)

    def vmem_bytes(tm_, tn_, tk_):
        acc = 4 if k // tk_ > 1 else 0
        return 2 * (tm_ * tk_ * a.dtype.itemsize + tk_ * tn_ * b.dtype.itemsize + tm_ * tn_ * per_out) + tm_ * tn_ * acc

    tm, tn, tk = _pick(m, 512), _pick(n, 512), _pick(k, MATMUL_MAX_K_TILE)
    while vmem_bytes(tm, tn, tk) > MATMUL_VMEM_BUDGET and tk % 256 == 0 and tk > 512:
        tk //= 2
    while tm < m and m % (2 * tm) == 0 and vmem_bytes(2 * tm, tn, tk) <= MATMUL_VMEM_BUDGET:
        tm *= 2
    while tn < min(n, 1024) and n % (2 * tn) == 0 and vmem_bytes(tm, 2 * tn, tk) <= MATMUL_VMEM_BUDGET:
        tn *= 2
    nk = k // tk
    ne = len(extras)
    dims = (((0 if ta else 1,), (1 if tb else 0,)), ((), ()))

    def body(*refs):
        a_ref, b_ref = refs[0], refs[1]
        e_refs = refs[2:2 + ne]
        o_refs = refs[2 + ne + len(deps):2 + ne + len(deps) + len(outs)]
        kk = pl.program_id(2)
        part = lax.dot_general(a_ref[...].astype(bf16), b_ref[...].astype(bf16), dims, preferred_element_type=f32)

        def finish(total):
            res = (total,) if epi is None else epi(total, *[e[...] for e in e_refs])
            for r, v in zip(o_refs, res):
                r[...] = v.astype(r.dtype)

        if nk == 1:
            finish(part)
            return
        acc = refs[-1]

        @pl.when(kk == 0)
        def _():
            acc[...] = part

        @pl.when(jnp.logical_and(kk > 0, kk < nk - 1))
        def _():
            acc[...] += part

        @pl.when(kk == nk - 1)
        def _():
            finish(acc[...] + part)

    a_spec = pl.BlockSpec((tk, tm), lambda i, j, q: (q, i)) if ta else pl.BlockSpec((tm, tk), lambda i, j, q: (i, q))
    b_spec = pl.BlockSpec((tn, tk), lambda i, j, q: (j, q)) if tb else pl.BlockSpec((tk, tn), lambda i, j, q: (q, j))
    assert all(off % tn == 0 for (_, off) in extras)
    e_specs = [pl.BlockSpec((tm, tn), lambda i, j, q, off=off // tn: (i, off + j)) for (_, off) in extras]
    res = pl.pallas_call(
        body, name=name, grid=(m // tm, n // tn, nk),
        in_specs=[a_spec, b_spec] + e_specs + [pl.BlockSpec(memory_space=pl.ANY) for _ in deps],
        out_specs=[pl.BlockSpec((tm, tn), lambda i, j, q: (i, j)) for _ in outs],
        out_shape=[jax.ShapeDtypeStruct((m, n), dt) for dt in outs],
        scratch_shapes=[pltpu.VMEM((tm, tn), f32)] if nk > 1 else [],
        compiler_params=_cp(("parallel", "parallel", "arbitrary")),
    )(a, b, *[e[0] for e in extras], *deps)
    return res[0] if len(outs) == 1 else tuple(res)


def merge_fwd(name, ys, wbs, u):
    rows, n_out = ys[0].shape[0], wbs[0].shape[1]
    nb = len(ys)
    tm, tn = _pick(rows, 512), _pick(n_out, 512)

    def body(*refs):
        y_refs, w_refs, g_refs = refs[:nb], refs[nb:2 * nb], refs[2 * nb:3 * nb]
        m_ref, p_refs = refs[3 * nb], refs[3 * nb + 1:]
        total = None
        for y_ref, w_ref, g_ref, p_ref in zip(y_refs, w_refs, g_refs, p_refs):
            pre = jnp.dot(y_ref[...], w_ref[...], preferred_element_type=f32)
            p_ref[...] = pre.astype(p_ref.dtype)
            term = jax.nn.sigmoid(g_ref[...]) * pre
            total = term if total is None else total + term
        m_ref[...] = total.astype(m_ref.dtype)

    in_specs = [pl.BlockSpec((tm, y.shape[1]), lambda i, j: (i, 0)) for y in ys]
    in_specs += [pl.BlockSpec((w.shape[0], tn), lambda i, j: (0, j)) for w in wbs]
    in_specs += [pl.BlockSpec((tm, tn), lambda i, j, off=n * (n_out // tn): (i, off + j)) for n in range(nb)]
    out_spec = pl.BlockSpec((tm, tn), lambda i, j: (i, j))
    res = pl.pallas_call(
        body, name=name, grid=(rows // tm, n_out // tn), in_specs=in_specs, out_specs=[out_spec] * (nb + 1),
        out_shape=[jax.ShapeDtypeStruct((rows, n_out), bf16)] * (nb + 1),
        compiler_params=_cp(("parallel", "parallel")),
    )(*ys, *wbs, *([u] * nb))
    return res[0], list(res[1:])


ATT_SCALE = (QK_NOPE + QK_ROPE) ** -0.5
LN2 = 0.6931471805599453
ATT_C = ATT_SCALE / LN2
NT = (((1,), (1,)), ((), ()))
TN = (((0,), (0,)), ((), ()))


def _causal(tq, tk):
    return lax.broadcasted_iota(jnp.int32, (tq, tk), 0) >= lax.broadcasted_iota(jnp.int32, (tq, tk), 1)


def _tri_pairs(n, by_column):
    if by_column:
        pairs = [(i, j) for j in range(n) for i in range(j, n)]
    else:
        pairs = [(i, j) for i in range(n) for j in range(i + 1)]
    return (jnp.asarray([a for a, _ in pairs], jnp.int32), jnp.asarray([b for _, b in pairs], jnp.int32))


HEADS_PER_STEP = 4
HEAD_PAIR = HEADS_PER_STEP * LANE


def attn_fwd(q, k, v, t):
    rows = q.shape[0]
    n = rows // t
    it, jt = _tri_pairs(n, False)

    def body(it_ref, jt_ref, q_ref, k_ref, v_ref, o_ref, lse_ref, m_s, l_s, acc_s):
        s_id = pl.program_id(1)
        i, j = it_ref[s_id], jt_ref[s_id]

        @pl.when(j == 0)
        def _():
            m_s[...] = jnp.full_like(m_s, -jnp.inf)
            l_s[...] = jnp.zeros_like(l_s)
            acc_s[...] = jnp.zeros_like(acc_s)

        def step(diag):
            for hh in range(HEADS_PER_STEP):
                sl = slice(LANE * hh, LANE * (hh + 1))
                s = lax.dot_general(q_ref[:, sl], k_ref[:, sl], NT, preferred_element_type=f32)
                if diag:
                    s = jnp.where(_causal(t, t), s, -jnp.inf)
                m_prev = m_s[:, sl]
                m_new = jnp.maximum(m_prev, jnp.max(s, axis=1, keepdims=True))
                alpha = jnp.exp2(m_prev - m_new)
                p = jnp.exp2(s - m_new[:, :1])
                l_s[:, sl] = alpha * l_s[:, sl] + jnp.sum(p, axis=1, keepdims=True)
                acc_s[:, sl] = alpha * acc_s[:, sl] + jnp.dot(p.astype(bf16), v_ref[:, sl], preferred_element_type=f32)
                m_s[:, sl] = m_new

        pl.when(j < i)(lambda: step(False))

        @pl.when(j == i)
        def _():
            step(True)
            o_ref[...] = (acc_s[...] / l_s[...]).astype(o_ref.dtype)
            lse_ref[...] = m_s[...] + jnp.log2(l_s[...])

    qs = pl.BlockSpec((t, HEAD_PAIR), lambda h, s, it_, jt_: (it_[s], h))
    ks = pl.BlockSpec((t, HEAD_PAIR), lambda h, s, it_, jt_: (jt_[s], h))
    hw = N_HEADS * LANE
    return pl.pallas_call(
        body, name="attn_fwd",
        grid_spec=pltpu.PrefetchScalarGridSpec(
            num_scalar_prefetch=2, grid=(hw // HEAD_PAIR, it.shape[0]), in_specs=[qs, ks, ks], out_specs=[qs, qs],
            scratch_shapes=[pltpu.VMEM((t, HEAD_PAIR), f32)] * 3),
        out_shape=[jax.ShapeDtypeStruct((rows, hw), bf16), jax.ShapeDtypeStruct((rows, hw), f32)],
        compiler_params=_cp(("parallel", "arbitrary")),
    )(it, jt, q, k, v)


def attn_bwd(q, k, v, do, o, lse, t):
    rows = q.shape[0]
    n = rows // t
    it, jt = _tri_pairs(n, True)

    def body(it_ref, jt_ref, q_ref, k_ref, v_ref, do_ref, o_ref, lse_ref, dq_ref, dk_ref, dv_ref, dk_s, dv_s):
        s_id = pl.program_id(1)
        i, j = it_ref[s_id], jt_ref[s_id]

        @pl.when(s_id == 0)
        def _():
            dq_ref[...] = jnp.zeros_like(dq_ref)

        @pl.when(i == j)
        def _():
            dk_s[...] = jnp.zeros_like(dk_s)
            dv_s[...] = jnp.zeros_like(dv_s)

        q_rows = pl.ds(pl.multiple_of(i * t, t), t)

        def step(diag):
            for hh in range(HEADS_PER_STEP):
                sl = slice(LANE * hh, LANE * (hh + 1))
                qh, kh, vh, doh = q_ref[:, sl], k_ref[:, sl], v_ref[:, sl], do_ref[:, sl]
                s = lax.dot_general(qh, kh, NT, preferred_element_type=f32)
                p = jnp.exp2(s - lse_ref[:, sl][:, :1])
                if diag:
                    p = jnp.where(_causal(t, t), p, 0.0)
                dp = lax.dot_general(doh, vh, NT, preferred_element_type=f32)
                delta = jnp.sum(doh.astype(f32) * o_ref[:, sl].astype(f32), axis=1, keepdims=True)
                ds = (p * (dp - delta) * LN2).astype(bf16)
                dv_s[:, sl] += lax.dot_general(p.astype(bf16), doh, TN, preferred_element_type=f32)
                dk_s[:, sl] += lax.dot_general(ds, qh, TN, preferred_element_type=f32)
                dq_ref[q_rows, sl] += jnp.dot(ds, kh, preferred_element_type=f32)

        pl.when(i > j)(lambda: step(False))
        pl.when(i == j)(lambda: step(True))

        @pl.when(i == n - 1)
        def _():
            dk_ref[...] = dk_s[...]
            dv_ref[...] = dv_s[...]

    qs = pl.BlockSpec((t, HEAD_PAIR), lambda h, s, it_, jt_: (it_[s], h))
    ks = pl.BlockSpec((t, HEAD_PAIR), lambda h, s, it_, jt_: (jt_[s], h))
    dqs = pl.BlockSpec((rows, HEAD_PAIR), lambda h, s, it_, jt_: (0, h))
    hw = N_HEADS * LANE
    return pl.pallas_call(
        body, name="attn_bwd",
        grid_spec=pltpu.PrefetchScalarGridSpec(
            num_scalar_prefetch=2, grid=(hw // HEAD_PAIR, it.shape[0]), in_specs=[qs, ks, ks, qs, qs, qs],
            out_specs=[dqs, ks, ks], scratch_shapes=[pltpu.VMEM((t, HEAD_PAIR), f32)] * 2),
        out_shape=[jax.ShapeDtypeStruct((rows, hw), f32)] * 3,
        compiler_params=_cp(("parallel", "arbitrary")),
    )(it, jt, q, k, v, do, o, lse)


def _steps(tm):
    k, out = 1, []
    while k < tm:
        out.append(k)
        k *= 2
    return out


def scan_fwd(a, u, tm):
    rows, ch = a.shape
    n = rows // tm

    def body(a_ref, u_ref, h_ref, h_s):
        @pl.when(pl.program_id(0) == 0)
        def _():
            h_s[...] = jnp.zeros_like(h_s)

        av, bv = a_ref[...], u_ref[...]
        row = lax.broadcasted_iota(jnp.int32, av.shape, 0)
        for k in _steps(tm):
            a_sh = jnp.where(row >= k, pltpu.roll(av, k, 0), 1.0)
            b_sh = jnp.where(row >= k, pltpu.roll(bv, k, 0), 0.0)
            bv = av * b_sh + bv
            av = av * a_sh
        h = bv + av * h_s[HALO - 1:HALO, :]
        h_ref[...] = h
        h_s[...] = h[tm - HALO:, :]

    spec = pl.BlockSpec((tm, ch), lambda i: (i, 0))
    return pl.pallas_call(
        body, name="lru_scan_fwd", grid=(n,), in_specs=[spec, spec], out_specs=spec,
        out_shape=jax.ShapeDtypeStruct((rows, ch), f32), scratch_shapes=[pltpu.VMEM((HALO, ch), f32)],
        compiler_params=_cp(("arbitrary",)),
    )(a, u)


def scan_bwd(a, h, dh, tm):
    rows, ch = a.shape
    n = rows // tm
    per = tm // HALO

    def body(a_ref, h_ref, hp_ref, dh_ref, da_ref, du_ref, g_s, a_s):
        i = pl.program_id(0)
        step = n - 1 - i

        @pl.when(i == 0)
        def _():
            g_s[...] = jnp.zeros_like(g_s)
            a_s[...] = jnp.zeros_like(a_s)

        a0 = a_ref[...]
        row = lax.broadcasted_iota(jnp.int32, a0.shape, 0)
        av = jnp.where(row < tm - 1, pltpu.roll(a0, tm - 1, 0), a_s[0:1, :])
        bv = dh_ref[...]
        for k in _steps(tm):
            a_sh = jnp.where(row < tm - k, pltpu.roll(av, tm - k, 0), 1.0)
            b_sh = jnp.where(row < tm - k, pltpu.roll(bv, tm - k, 0), 0.0)
            bv = bv + av * b_sh
            av = av * a_sh
        g = bv + av * g_s[0:1, :]
        h_last = jnp.where(step > 0, hp_ref[HALO - 1:HALO, :], 0.0)
        h_prev = jnp.where(row >= 1, pltpu.roll(h_ref[...], 1, 0), h_last)
        du_ref[...] = g
        da_ref[...] = g * h_prev
        g_s[...] = g[0:HALO, :]
        a_s[...] = a0[0:HALO, :]

    spec = pl.BlockSpec((tm, ch), lambda i: (n - 1 - i, 0))
    hp_spec = pl.BlockSpec((HALO, ch), lambda i: (jnp.maximum((n - 1 - i) * per - 1, 0), 0))
    return pl.pallas_call(
        body, name="lru_scan_bwd", grid=(n,), in_specs=[spec, spec, hp_spec, spec], out_specs=[spec, spec],
        out_shape=[jax.ShapeDtypeStruct((rows, ch), f32)] * 2,
        scratch_shapes=[pltpu.VMEM((HALO, ch), f32)] * 2,
        compiler_params=_cp(("arbitrary",)),
    )(a, h, h, dh)


def _rms(x, g):
    return x * lax.rsqrt(jnp.mean(x * x, axis=-1, keepdims=True) + EPS) * g


def f_rms(step, p, c, x):
    return [_rms(x[0], p[0])], []


def _rope(x, cosf, sinf):
    lane = lax.broadcasted_iota(jnp.int32, x.shape, 1)
    sw = jnp.where(lane < KR_LANE + QK_ROPE // 2, lane_roll(x, LANE - QK_ROPE // 2), lane_roll(x, QK_ROPE // 2))
    return x * cosf + sw * sinf


def f_prep(step, p, c, x):
    q, kn, kr, cosf, sinf = x
    kr_rot = _rope(kr, cosf, sinf)
    qr = [_rope(q[:, LANE * h:LANE * (h + 1)], cosf, sinf) * ATT_C for h in range(N_HEADS)]
    kk = [kn[:, LANE * h:LANE * (h + 1)] + kr_rot for h in range(N_HEADS)]
    return [jnp.concatenate(qr, axis=1), jnp.concatenate(kk, axis=1)], []


def _conv(tail, x, w, b):
    xf = jnp.concatenate([tail, x], axis=0)
    acc = b + w[CONV_W - 1:CONV_W, :] * xf
    for k in range(CONV_W - 1):
        acc = acc + w[k:k + 1, :] * shift_down(xf, CONV_W - 1 - k)
    return acc[HALO:, :]


def f_pool(step, p, c, x):
    wp, sc = p
    (tail,) = c
    (u,) = x
    tm = u.shape[0]
    xf = jnp.concatenate([tail, u], axis=0)
    sums, s, w = [], xf, 1
    while w < POOL_WINDOWS[-1]:
        s = s + shift_down(s, w)
        w *= 2
        sums.append(s)
    t = step * tm + lax.broadcasted_iota(jnp.int32, (tm, 1), 0)
    ys = []
    for g, (w, s) in enumerate(zip(POOL_WINDOWS, sums)):
        sl = slice(LANE * g, LANE * (g + 1))
        cnt = jnp.minimum(t + 1, w).astype(f32)
        d = s[POOL_HALO:, sl] / cnt - u[:, sl]
        ys.append(jnp.dot(d.astype(bf16), wp[LANE * g:LANE * (g + 1), :].astype(bf16), preferred_element_type=f32))
    return [jnp.concatenate(ys, axis=1) * sc], [u[tm - POOL_HALO:, :]]


def f_ssd(step, p, c, x):
    conv_w, conv_b, dtb, alog, dsk, ng = p
    tail, s_in = c[0], c[1:]
    z, xbc, dt = x
    ln = z.shape[0]
    xc = jax.nn.silu(_conv(tail, xbc, conv_w, conv_b))
    xs, bb, cc = xc[:, :MIX], xc[:, MIX:MIX + LANE], xc[:, MIX + LANE:]
    dtv = jax.nn.softplus(dt + dtb[0:1, :])
    a = dtv * -jnp.exp(alog[0:1, :])
    ri = lax.broadcasted_iota(jnp.int32, (ln, ln), 0)
    ci = lax.broadcasted_iota(jnp.int32, (ln, ln), 1)
    tril = (ri >= ci).astype(f32)
    triu = (ri <= ci).astype(f32)
    hi = lax.Precision.HIGHEST
    a_cs = jnp.dot(tril, a, precision=hi, preferred_element_type=f32)
    a_cs_t = lax.dot_general(a, triu, TN, precision=hi, preferred_element_type=f32)
    a_tot = jnp.sum(a, axis=0, keepdims=True)
    lane = lax.broadcasted_iota(jnp.int32, (1, LANE), 1)
    half = [(lane < 64).astype(f32), (lane >= 64).astype(f32)]
    hrow = lax.broadcasted_iota(jnp.int32, (LANE, 1), 0)

    def head(v, h):
        return jnp.sum(v * (lane == h).astype(f32), axis=1, keepdims=True)

    def pair(v, j):
        return head(v, 2 * j) * half[0] + head(v, 2 * j + 1) * half[1]

    cg = [(cc * half[g]).astype(bf16) for g in range(2)]
    bg = [(bb * half[g]).astype(bf16) for g in range(2)]
    cb = [lax.dot_general(cg[g], bg[g], NT, preferred_element_type=f32) for g in range(2)]
    ys, s_out = [], []
    for j in range(4):
        g = j // 2
        xs_j = xs[:, LANE * j:LANE * (j + 1)]
        xj = xs_j * pair(dtv, j)
        yj = xs_j * pair(dsk[0:1, :], j)
        for hh in range(2):
            h = 2 * j + hh
            rowv = jnp.sum(a_cs_t * (hrow == h).astype(f32), axis=0, keepdims=True)
            lmat = jnp.exp(jnp.where(ri >= ci, head(a_cs, h) - rowv, -jnp.inf))
            yj = yj + jnp.dot((cb[g] * lmat).astype(bf16), (xj * half[hh]).astype(bf16), preferred_element_type=f32)
        acs = pair(a_cs, j)
        tot = pair(a_tot, j)
        yj = yj + jnp.exp(acs) * jnp.dot(cg[g], s_in[j].astype(bf16), preferred_element_type=f32)
        s_new = jnp.exp(tot) * s_in[j] + lax.dot_general(bg[g], (xj * jnp.exp(tot - acs)).astype(bf16), TN,
                                                         preferred_element_type=f32)
        ys.append(yj)
        s_out.append(s_new)
    y = jnp.concatenate(ys, axis=1) * jax.nn.silu(z)
    return [_rms(y, ng)], [xbc[ln - HALO:, :]] + s_out


def _neg_expm1(y):
    series = -y * (1.0 + y * (0.5 + y * (1.0 / 6 + y * (1.0 / 24 + y * (1.0 / 120)))))
    return jnp.where(y > -0.05, series, 1.0 - jnp.exp(y))


def f_lru_pre(step, p, c, x):
    cw, cb_, wa, ba, wi, bi, lam = p
    (tail,) = c
    (lx,) = x
    tm = lx.shape[0]
    xc = _conv(tail, lx, cw, cb_)
    xb = xc.astype(bf16)
    r = jax.nn.sigmoid(jnp.dot(xb, wa.astype(bf16), preferred_element_type=f32) + ba)
    it = jax.nn.sigmoid(jnp.dot(xb, wi.astype(bf16), preferred_element_type=f32) + bi)
    log_a = -LRU_C * r * jax.nn.softplus(-lam)
    mult = jnp.sqrt(_neg_expm1(2.0 * log_a))
    return [jnp.exp(log_a), xc * it * mult], [lx[tm - HALO:, :]]


def f_lru_post(step, p, c, x):
    h, g = x
    return [h * jax.nn.gelu(g)], []


def loss_head(x, tgt, g, tm):
    rows, d = x.shape
    n = rows // tm

    def body(x_ref, t_ref, g_ref, loss_ref, dx_ref, dg_ref):
        @pl.when(pl.program_id(0) == 0)
        def _():
            loss_ref[...] = jnp.zeros_like(loss_ref)
            dg_ref[...] = jnp.zeros_like(dg_ref)

        def fn(gv, xv):
            err = _rms(xv, gv) - t_ref[...]
            return 0.5 * jnp.sum(jnp.mean(err * err, axis=-1, keepdims=True))

        val, (dg, dx) = jax.value_and_grad(fn, argnums=(0, 1))(g_ref[...], x_ref[...])
        loss_ref[...] += val
        dg_ref[...] += dg
        dx_ref[...] = dx

    spec = pl.BlockSpec((tm, d), lambda i: (i, 0))
    return pl.pallas_call(
        body, name="loss_head", grid=(n,), in_specs=[spec, spec, _const_spec((1, d))],
        out_specs=[_const_spec((8, LANE)), spec, _const_spec((1, d))],
        out_shape=[jax.ShapeDtypeStruct((8, LANE), f32), jax.ShapeDtypeStruct((rows, d), f32),
                   jax.ShapeDtypeStruct((1, d), f32)],
        compiler_params=_cp(("arbitrary",)),
    )(x, tgt, g)


def ew(name, fn, ins, outs, tm):
    rows = ins[0][0].shape[0]
    ni = len(ins)

    def body(*refs):
        res = fn(*[r[...].astype(f32) for r in refs[:ni]])
        for r, v in zip(refs[ni:], res):
            r[...] = v.astype(r.dtype)

    return pl.pallas_call(
        body, name=name, grid=(rows // tm,), in_specs=[_tile_spec(tm, w, cb) for (_, w, cb) in ins],
        out_specs=[_tile_spec(tm, w, 0) for (w, _) in outs],
        out_shape=[jax.ShapeDtypeStruct((rows, w), dt) for (w, dt) in outs],
        compiler_params=_cp(("parallel",)),
    )(*[t[0] for t in ins])


def _peers():
    x, y, c = lax.axis_index("x"), lax.axis_index("y"), lax.axis_index("c")
    me = 4 * x + 2 * y + c
    out = []
    for k in range(1, N_DEV):
        px = 1 - x if k & 4 else x
        py = 1 - y if k & 2 else y
        pc = 1 - c if k & 1 else c
        out.append(((px, py, pc), 4 * px + 2 * py + pc))
    return me, out


_HBM = pl.BlockSpec(memory_space=pltpu.HBM)
_SEM = pl.BlockSpec(memory_space=pltpu.SEMAPHORE)
_EFFECT = pltpu.SideEffectType.DATAFLOW_SIDE_EFFECTING


def _remote(src_ref, land_ref, gather, me, pid, dev, send_sems, recv_sems, k, recv_side):
    return pltpu.make_async_remote_copy(
        src_ref=src_ref if gather else src_ref.at[pid], dst_ref=land_ref.at[pid if recv_side else me],
        send_sem=send_sems.at[k], recv_sem=recv_sems.at[k], device_id=dev, device_id_type=pl.DeviceIdType.MESH)


def exchange_start(name, srcs, gather, deps=()):
    n, nd = len(srcs), len(deps)
    shapes = [(s.shape if gather else s.shape[1:]) for s in srcs]
    lands = [lax.empty((N_DEV,) + tuple(sh), s.dtype) for s, sh in zip(srcs, shapes)]

    def body(*refs):
        src_refs, land_refs = refs[:n], refs[n:2 * n]
        send_sems, recv_sems = refs[2 * n + nd], refs[2 * n + nd + 1]
        token = refs[-1]
        me, peers = _peers()
        for k, (dev, pid) in enumerate(peers):
            for s_ref, l_ref in zip(src_refs, land_refs):
                _remote(s_ref, l_ref, gather, me, pid, dev, send_sems, recv_sems, k, False).start()
        token[...] = jnp.zeros_like(token)

    hbm = lambda a: pltpu.with_memory_space_constraint(a, pltpu.HBM)
    res = pl.pallas_call(
        body, name=name,
        out_shape=(pltpu.SemaphoreType.DMA((N_DEV - 1,)), pltpu.SemaphoreType.DMA((N_DEV - 1,)),
                   *[pltpu.HBM(a.shape, a.dtype) for a in list(srcs) + lands], jax.ShapeDtypeStruct((8, LANE), f32)),
        in_specs=[_HBM] * (2 * n) + [pl.BlockSpec(memory_space=pl.ANY)] * nd,
        out_specs=(_SEM, _SEM, *([_HBM] * (2 * n)), pl.BlockSpec(memory_space=pltpu.VMEM)),
        input_output_aliases={i: 2 + i for i in range(2 * n)},
        compiler_params=pltpu.CompilerParams(has_side_effects=_EFFECT),
    )(*[hbm(a) for a in list(srcs) + lands], *deps)
    return dict(sems=res[:2], srcs=list(res[2:2 + n]), lands=list(res[2 + n:2 + 2 * n]), token=res[-1], gather=gather)


def exchange_wait(name, h, afters):
    n, gather = len(h["srcs"]), h["gather"]

    def body(*refs):
        src_refs, land_refs = refs[:n], refs[n:2 * n]
        send_sems, recv_sems = refs[2 * n], refs[2 * n + 1]
        me, peers = _peers()
        for k, (dev, pid) in enumerate(peers):
            for s_ref, l_ref in zip(src_refs, land_refs):
                _remote(s_ref, l_ref, gather, me, pid, dev, send_sems, recv_sems, k, True).wait_recv()
        for k, (dev, pid) in enumerate(peers):
            for s_ref, l_ref in zip(src_refs, land_refs):
                _remote(s_ref, l_ref, gather, me, pid, dev, send_sems, recv_sems, k, False).wait_send()

    arrs = h["srcs"] + h["lands"]
    res = pl.pallas_call(
        body, name=name, out_shape=tuple(pltpu.HBM(a.shape, a.dtype) for a in arrs),
        in_specs=[_HBM] * (2 * n) + [_SEM, _SEM] + [pl.BlockSpec(memory_space=pl.ANY)] * len(afters),
        out_specs=tuple([_HBM] * (2 * n)), input_output_aliases={i: i for i in range(2 * n)},
        compiler_params=pltpu.CompilerParams(has_side_effects=_EFFECT),
    )(*arrs, *h["sems"], *afters)
    me = 4 * lax.axis_index("x") + 2 * lax.axis_index("y") + lax.axis_index("c")
    out = []
    for src, land in zip(res[:n], res[n:]):
        mine = src if gather else lax.dynamic_index_in_dim(src, me, 0, keepdims=False)
        out.append(lax.dynamic_update_index_in_dim(land, mine, me, 0))
    return out


def adamw(name, parts, w, m, v):
    nl = len(parts)
    shape = w.shape[1:]
    c = shape[-1]
    r = 1
    for s in shape[:-1]:
        r *= s
    tr = _pick(r, 256) if r % 8 == 0 else r
    nb = r // tr
    parts2 = [p.reshape(N_DEV, r, c) for p in parts]
    w2, m2, v2 = (a.reshape(nl, r, c) for a in (w, m, v))

    def body(*refs):
        p_refs = refs[:nl]
        w_ref, m_ref, v_ref, g_ref, d_ref, nm_ref, nv_ref = refs[nl:]
        layer = pl.program_id(0)
        for ll, p_ref in enumerate(p_refs):
            @pl.when(layer == ll)
            def _(p_ref=p_ref):
                g = p_ref[0].astype(f32)
                for i in range(1, N_DEV):
                    g = g + p_ref[i].astype(f32)
                mn = ADAM_B1 * m_ref[0] + (1.0 - ADAM_B1) * g
                vn = ADAM_B2 * v_ref[0] + (1.0 - ADAM_B2) * jnp.square(g)
                m_hat = mn / (1.0 - ADAM_B1 ** ADAM_STEP)
                v_hat = vn / (1.0 - ADAM_B2 ** ADAM_STEP)
                g_ref[0] = g
                d_ref[0] = -ADAM_LR * (m_hat / (jnp.sqrt(v_hat) + ADAM_EPS) + ADAM_WD * w_ref[0])
                nm_ref[0] = mn
                nv_ref[0] = vn

    def p_spec(ll):
        return pl.BlockSpec((N_DEV, tr, c), lambda l, i: (0, jnp.where(l == ll, i, jnp.where(l > ll, nb - 1, 0)), 0))

    spec = pl.BlockSpec((1, tr, c), lambda l, i: (l, i, 0))
    res = pl.pallas_call(
        body, name=name, grid=(nl, nb), in_specs=[p_spec(ll) for ll in range(nl)] + [spec, spec, spec],
        out_specs=[spec] * 4, out_shape=[jax.ShapeDtypeStruct((nl, r, c), f32)] * 4,
        compiler_params=_cp(("arbitrary", "arbitrary")),
    )(*parts2, w2, m2, v2)
    return [a.reshape(w.shape) for a in res]


_IN_SPLITS = dict(cq=(0, 384), ckv=(384, 640), kr=(640, 672), pool=(672, 1184), z=(1184, 1696), xbc=(1696, 2464),
                  dt=(2464, 2472), lg=(2472, 2984), lx=(2984, 3496), gates=(3496, 7592))


W_IN_SHARD = IN_COLS // N_DEV

_PAD_ORDER = ("gates", "pool", "z", "lg", "lx", "xbc", "cq", KR_LANE, "kr", LANE - KR_LANE - QK_ROPE, "ckv", "dt",
              LANE - 8, U_COLS - U_DT[0] - LANE)
_SEGMENTS = ((0, U_CQ[0], 384), (384, U_CKV[0], 256), (640, U_KR[0] + KR_LANE, QK_ROPE), (672, U_POOL[0], 512),
             (1184, U_Z[0], 512), (1696, U_XBC[0], 768), (2464, U_DT[0], 8), (2472, U_LG[0], 512), (2984, U_LX[0], 512),
             (3496, 0, 4096))


def _pad_w_in(shards):
    rows = shards.shape[1]
    pieces = []
    for item in _PAD_ORDER:
        if isinstance(item, int):
            pieces.append(jnp.zeros((rows, item), shards.dtype))
            continue
        a, b = _IN_SPLITS[item]
        for d in range(a // W_IN_SHARD, (b - 1) // W_IN_SHARD + 1):
            lo, hi = max(a, d * W_IN_SHARD), min(b, (d + 1) * W_IN_SHARD)
            pieces.append(shards[d, :, lo - d * W_IN_SHARD:hi - d * W_IN_SHARD])
    return jnp.concatenate(pieces, axis=1)


def _w_in_blocks(g):
    blocks = []
    for d in range(N_DEV):
        a, b = d * W_IN_SHARD, (d + 1) * W_IN_SHARD
        pieces = []
        for ref, pad, width in _SEGMENTS:
            lo, hi = max(a, ref), min(b, ref + width)
            if lo < hi:
                pieces.append(g[:, pad + lo - ref:pad + hi - ref])
        blocks.append(jnp.concatenate(pieces, axis=1))
    return jnp.stack(blocks).astype(bf16)


def _head_pad_cols(w, per, lo, hi):
    k = w.shape[0]
    w = w.reshape(k, N_HEADS, per)[:, :, lo:hi]
    return jnp.pad(w, ((0, 0), (0, 0), (0, LANE - (hi - lo)))).reshape(k, N_HEADS * LANE)


def _head_unpad_cols(g, n):
    k = g.shape[0]
    return g.reshape(k, N_HEADS, LANE)[:, :, :n]


def _block_diag(w):
    out = jnp.zeros((MIX, MIX), w.dtype)
    for i in range(8):
        out = lax.dynamic_update_slice(out, w[i], (64 * i, 64 * i))
    return out


def _block_diag_inv(g):
    return jnp.stack([g[64 * i:64 * (i + 1), 64 * i:64 * (i + 1)] for i in range(8)])


def _head8(v):
    return jnp.zeros((8, LANE), f32).at[0, :8].set(v)


GROUPS = dict(A=("w_in",), B=("w_uq", "w_ukv", "ssd_conv_w", "lru_conv_w", "w_branch", "w_out"),
              C=("w_ff1", "w_ff2", "w_ple_gate", "w_ple"))


def _kernel_weights(grp, fw):
    if grp == "A":
        return dict(w_in=_pad_w_in(fw["w_in"]))
    if grp == "C":
        return dict(w_ff1=fw["w_ff1"], w_ff2=fw["w_ff2"], w_pg=fw["w_ple_gate"], w_ple=fw["w_ple"])
    wb = fw["w_branch"]
    wb0 = jnp.pad(wb[0].reshape(N_HEADS, V_HEAD, D_MODEL), ((0, 0), (0, LANE - V_HEAD), (0, 0))).reshape(N_HEADS * LANE, D_MODEL)
    return dict(
        w_uq=_head_pad_cols(fw["w_uq"], QK_NOPE + QK_ROPE, 0, QK_NOPE + QK_ROPE),
        w_uk=_head_pad_cols(fw["w_ukv"], QK_NOPE + V_HEAD, 0, QK_NOPE),
        w_uv=_head_pad_cols(fw["w_ukv"], QK_NOPE + V_HEAD, QK_NOPE, QK_NOPE + V_HEAD),
        wb=[wb0, wb[1], wb[2], wb[3]], w_out=fw["w_out"], ssd_conv_w=fw["ssd_conv_w"], lru_conv_w=fw["lru_conv_w"])


def _layer_params(sp, l):
    row = lambda n: sp[n][l][None, :]
    return dict(
        g_mix=row("g_mix"), q_norm=row("q_norm"), kv_norm=row("kv_norm"),
        pool=[sp["w_pool"][l].reshape(4 * LANE, LANE), row("pool_scale")],
        ssd=[None, row("ssd_conv_b"), _head8(sp["ssd_dt_bias"][l]), _head8(sp["ssd_a_log"][l]),
             _head8(sp["ssd_d"][l]), row("ssd_norm")],
        lru=[None, row("lru_conv_b"), _block_diag(sp["lru_w_a"][l]), row("lru_b_a"),
             _block_diag(sp["lru_w_i"][l]), row("lru_b_i"), row("lru_lambda")],
        g_mlp=row("g_mlp"), g_ple=row("g_ple"),
    )


_sig = jax.nn.sigmoid
_SSD_CARRY = [(HALO, SSD_XBC)] + [(LANE, LANE)] * 4


def _tiles(rows):
    return dict(tm=_pick(rows, 512), ta=_pick(rows, 512), tp=_pick(rows, 256), tl=_pick(rows, 256), ts=_pick(rows, 256))


def _mixer_tiles(u):
    return dict(
        cq=(u, 384, U_CQ[0] // 384), ckv=(u, 256, U_CKV[0] // 256), kr=(u, LANE, U_KR[0] // LANE),
        pool=(u, MIX, U_POOL[0] // MIX), z=(u, MIX, U_Z[0] // MIX), xbc=(u, SSD_XBC, U_XBC[0] // SSD_XBC),
        dt=(u, LANE, U_DT[0] // LANE), lg=(u, MIX, U_LG[0] // MIX), lx=(u, MIX, U_LX[0] // MIX))


def _layer_fwd(x, p_bf, ctx, l, pr, cosf, sinf):
    rows = x.shape[0]
    ts = _tiles(rows)
    tm = ts["tm"]
    nm = lambda s: f"{s}_l{l}"
    r = dict(x=x)
    (h,), _ = seq_fwd(nm("rms_in"), f_rms, [pr["g_mix"]], [(x, D_MODEL, 0)], [], [(D_MODEL, bf16)], tm)
    w = dict(_kernel_weights("A", ctx.weights(l, "A", h)))
    u = matmul(nm("w_in"), h, w["w_in"])
    mt = _mixer_tiles(u)
    (cqn,), _ = seq_fwd(nm("rms_q"), f_rms, [pr["q_norm"]], [mt["cq"]], [], [(Q_LORA, bf16)], tm)
    (ckvn,), _ = seq_fwd(nm("rms_kv"), f_rms, [pr["kv_norm"]], [mt["ckv"]], [], [(KV_LORA, bf16)], tm)
    (yb,), pool_saved = seq_fwd(nm("pool"), f_pool, pr["pool"], [mt["pool"]], [(POOL_HALO, MIX)], [(MIX, bf16)], ts["tp"])
    w.update(_kernel_weights("B", ctx.weights(l, "B", yb)))
    pr = dict(pr, ssd=[w["ssd_conv_w"]] + pr["ssd"][1:], lru=[w["lru_conv_w"]] + pr["lru"][1:])
    q = matmul(nm("w_uq"), cqn, w["w_uq"])
    kn = matmul(nm("w_uk"), ckvn, w["w_uk"])
    vb = matmul(nm("w_uv"), ckvn, w["w_uv"], outs=(bf16,))
    hw = N_HEADS * LANE
    (qr, kr), _ = seq_fwd(nm("mla_prep"), f_prep, [], [(q, hw, 0), (kn, hw, 0), mt["kr"], (cosf, LANE, 0), (sinf, LANE, 0)],
                          [], [(hw, bf16), (hw, bf16)], tm)
    o, lse = attn_fwd(qr, kr, vb, ts["ta"])
    (yc,), ssd_saved = seq_fwd(nm("ssd"), f_ssd, pr["ssd"], [mt["z"], mt["xbc"], mt["dt"]], _SSD_CARRY, [(MIX, bf16)], SSD_CHUNK)
    (la, lu), lru_saved = seq_fwd(nm("lru_pre"), f_lru_pre, pr["lru"], [mt["lx"]], [(HALO, MIX)], [(MIX, f32), (MIX, f32)], ts["tl"])
    hh = scan_fwd(la, lu, ts["ts"])
    (yd,), _ = seq_fwd(nm("lru_post"), f_lru_post, [], [(hh, MIX, 0), mt["lg"]], [], [(MIX, bf16)], tm)
    ys = [o, yb, yc, yd]
    m, pres = merge_fwd(nm("merge"), ys, w["wb"], u)
    x1 = matmul(nm("w_out"), m, w["w_out"], epi=lambda acc, xr: (acc + xr,), extras=[(x, 0)])
    (h2,), _ = seq_fwd(nm("rms_mlp"), f_rms, [pr["g_mlp"]], [(x1, D_MODEL, 0)], [], [(D_MODEL, bf16)], tm)
    w.update(_kernel_weights("C", ctx.weights(l, "C", h2)))
    a1, act = matmul(nm("ff1"), h2, w["w_ff1"], outs=(bf16, bf16), epi=lambda acc: (acc, jnp.square(jnp.maximum(acc, 0.0))))
    x2 = matmul(nm("ff2"), act, w["w_ff2"], epi=lambda acc, xr: (acc + xr,), extras=[(x1, 0)])
    (h3,), _ = seq_fwd(nm("rms_ple"), f_rms, [pr["g_ple"]], [(x2, D_MODEL, 0)], [], [(D_MODEL, bf16)], tm)
    gl = matmul(nm("ple_gate"), h3, w["w_pg"])
    x3, pe = matmul(nm("ple"), p_bf, w["w_ple"], outs=(f32, f32), epi=lambda acc, g, xr: (xr + acc * _sig(g), acc),
                    extras=[(gl, 0), (x2, 0)])
    r.update(h=h, u=u, cqn=cqn, ckvn=ckvn, q=q, kn=kn, vb=vb, qr=qr, kr=kr, o=o, lse=lse, ys=ys, pres=pres, m=m, x1=x1,
             h2=h2, a1=a1, act=act, x2=x2, h3=h3, gl=gl, pe=pe, p_bf=p_bf, pool_saved=pool_saved, ssd_saved=ssd_saved,
             lru_saved=lru_saved, la=la, hh=hh, w=w, pr=pr)
    return x3, r


def _gate_bwd(d, g, pre):
    s = _sig(g)
    return d * s, d * pre * s * (1.0 - s)


def _layer_bwd(dx3, r, ctx, l, cosf, sinf, tok, extra_small):
    rows = dx3.shape[0]
    ts = _tiles(rows)
    tm = ts["tm"]
    nm = lambda s: f"{s}_l{l}"
    u, w, pr = r["u"], r["w"], r["pr"]
    mt = _mixer_tiles(u)
    g = {}
    full = lambda a: (a, a.shape[1], 0)
    dpe, dgl = ew(nm("ple_bwd"), _gate_bwd, [full(dx3), full(r["gl"]), full(r["pe"])], [(D_MODEL, bf16)] * 2, tm)
    g["w_ple"] = matmul(nm("d_w_ple"), r["p_bf"], dpe, ta=True, deps=[tok] if tok is not None else [])
    g["w_pg"] = matmul(nm("d_w_pg"), r["h3"], dgl, ta=True)
    dh3 = matmul(nm("d_h3"), dgl, w["w_pg"], tb=True)
    (g["g_ple"],), (dx2,) = seq_bwd(nm("rms_ple_bwd"), f_rms, [pr["g_ple"]], [full(r["x2"])], [True], [], [dh3], [f32], tm,
                                    add_to=(0, dx3))
    da1 = matmul(nm("d_act"), dx2, w["w_ff2"], tb=True, outs=(bf16,),
                 epi=lambda acc, a: (acc * 2.0 * jnp.maximum(a, 0.0),), extras=[(r["a1"], 0)])
    g["w_ff2"] = matmul(nm("d_w_ff2"), r["act"], dx2, ta=True)
    g["w_ff1"] = matmul(nm("d_w_ff1"), r["h2"], da1, ta=True)
    tok = ctx.grads(l, "C", dict(w_ff1=g["w_ff1"], w_ff2=g["w_ff2"], w_ple_gate=g["w_pg"], w_ple=g["w_ple"]))
    dh2 = matmul(nm("d_h2"), da1, w["w_ff1"], tb=True, deps=[tok])
    (g["g_mlp"],), (dx1,) = seq_bwd(nm("rms_mlp_bwd"), f_rms, [pr["g_mlp"]], [full(r["x1"])], [True], [], [dh2], [f32], tm,
                                    add_to=(0, dx2))
    def merge_bwd(dm, *gates_and_pres):
        both = [_gate_bwd(dm, gates_and_pres[n], gates_and_pres[4 + n]) for n in range(4)]
        return tuple(b[0] for b in both) + tuple(b[1] for b in both)

    res = matmul(nm("d_merged"), dx1, w["w_out"], tb=True, outs=(bf16,) * 8, epi=merge_bwd,
                 extras=[(u, D_MODEL * n) for n in range(4)] + [(pre, 0) for pre in r["pres"]])
    dpres, dgates = list(res[:4]), list(res[4:])
    g["w_out"] = matmul(nm("d_w_out"), r["m"], dx1, ta=True)
    dys, g["wb"] = [], []
    for n in range(4):
        g["wb"].append(matmul(nm(f"d_w_branch{n}"), r["ys"][n], dpres[n], ta=True))
        dys.append(matmul(nm(f"d_y{n}"), dpres[n], w["wb"][n], tb=True, outs=(bf16 if n == 0 else f32,)))
    dqr, dkr_, dv = attn_bwd(r["qr"], r["kr"], r["vb"], dys[0], r["o"], r["lse"], ts["ta"])
    _, (dq, dkn, dkrope) = seq_bwd(nm("mla_prep_bwd"), f_prep, [],
                                   [full(r["q"]), full(r["kn"]), mt["kr"], full(cosf), full(sinf)],
                                   [True, True, True, False, False], [], [dqr, dkr_], [bf16] * 3, tm)
    g["w_uq"] = matmul(nm("d_w_uq"), r["cqn"], dq, ta=True)
    g["w_uk"] = matmul(nm("d_w_uk"), r["ckvn"], dkn, ta=True)
    g["w_uv"] = matmul(nm("d_w_uv"), r["ckvn"], dv, ta=True)
    dcqn = matmul(nm("d_cqn"), dq, w["w_uq"], tb=True)
    dckvn = matmul(nm("d_ckvn_k"), dkn, w["w_uk"], tb=True)
    dckvn = matmul(nm("d_ckvn_v"), dv, w["w_uv"], tb=True, epi=lambda acc, prev: (acc + prev,), extras=[(dckvn, 0)])
    (g["q_norm"],), (dcq,) = seq_bwd(nm("rms_q_bwd"), f_rms, [pr["q_norm"]], [mt["cq"]], [True], [], [dcqn], [bf16], tm)
    (g["kv_norm"],), (dckv,) = seq_bwd(nm("rms_kv_bwd"), f_rms, [pr["kv_norm"]], [mt["ckv"]], [True], [], [dckvn], [bf16], tm)
    g["pool"], (dpool,) = seq_bwd(nm("pool_bwd"), f_pool, pr["pool"], [mt["pool"]], [True], r["pool_saved"], [dys[1]],
                                  [bf16], ts["tp"])
    g["ssd"], (dz, dxbc, ddt) = seq_bwd(nm("ssd_bwd"), f_ssd, pr["ssd"], [mt["z"], mt["xbc"], mt["dt"]], [True] * 3,
                                        r["ssd_saved"], [dys[2]], [bf16] * 3, SSD_CHUNK)
    _, (dhh, dlg) = seq_bwd(nm("lru_post_bwd"), f_lru_post, [], [full(r["hh"]), mt["lg"]], [True, True], [], [dys[3]],
                            [f32, bf16], tm)
    da, du = scan_bwd(r["la"], r["hh"], dhh, ts["ts"])
    g["lru"], (dlx,) = seq_bwd(nm("lru_pre_bwd"), f_lru_pre, pr["lru"], [mt["lx"]], [True], r["lru_saved"], [da, du],
                               [bf16], ts["tl"])
    dk = _head_unpad_cols(g["w_uk"], QK_NOPE)
    dv_ = _head_unpad_cols(g["w_uv"], V_HEAD)
    wb0 = g["wb"][0].reshape(N_HEADS, LANE, D_MODEL)[:, :V_HEAD].reshape(MIX, D_MODEL)
    ssd, lru, pool = g["ssd"], g["lru"], g["pool"]
    tok = ctx.grads(l, "B", dict(
        w_uq=_head_unpad_cols(g["w_uq"], QK_NOPE + QK_ROPE).reshape(Q_LORA, -1),
        w_ukv=jnp.concatenate([dk, dv_], axis=2).reshape(KV_LORA, -1), ssd_conv_w=ssd[0], lru_conv_w=lru[0],
        w_branch=jnp.stack([wb0, g["wb"][1], g["wb"][2], g["wb"][3]]), w_out=g["w_out"]))
    du_p = jnp.concatenate(dgates + [dpool, dz, dlg, dlx, dxbc, dcq, dkrope, dckv, ddt,
                                     jnp.zeros((rows, U_COLS - U_DT[0] - LANE), bf16)], axis=1)
    small = dict(
        q_norm=g["q_norm"][0], kv_norm=g["kv_norm"][0],
        w_pool=pool[0].reshape(4, LANE, LANE), pool_scale=pool[1][0],
        ssd_conv_b=ssd[1][0], ssd_dt_bias=ssd[2][0, :8], ssd_a_log=ssd[3][0, :8], ssd_d=ssd[4][0, :8], ssd_norm=ssd[5][0],
        lru_conv_b=lru[1][0], lru_w_a=_block_diag_inv(lru[2]), lru_b_a=lru[3][0], lru_w_i=_block_diag_inv(lru[4]),
        lru_b_i=lru[5][0], lru_lambda=lru[6][0], g_mlp=g["g_mlp"][0], g_ple=g["g_ple"][0])
    tok_small = ctx.small(f"l{l}", [(n, l, small[n]) for n in SMALL if n in small] + extra_small)
    g_w_in = matmul(nm("d_w_in"), r["h"], du_p, ta=True, deps=[tok, tok_small])
    tok = ctx.grads(l, "A", dict(w_in=_w_in_blocks(g_w_in)))
    dh = matmul(nm("d_h"), du_p, w["w_in"], tb=True, deps=[tok])
    (g_mix,), (dx,) = seq_bwd(nm("rms_in_bwd"), f_rms, [pr["g_mix"]], [full(r["x"])], [True], [], [dh], [f32], tm,
                              add_to=(0, dx1))
    return dx, tok, ("g_mix", l, g_mix[0])


def _rope_tables(positions):
    inv = 1.0 / (ROPE_THETA ** (jnp.arange(0, QK_ROPE, 2, dtype=f32) / QK_ROPE))
    ang = positions.astype(f32)[:, None] * inv
    cos, sin = jnp.cos(ang), jnp.sin(ang)
    rows = positions.shape[0]
    pad = jnp.zeros((rows, LANE - KR_LANE - QK_ROPE), f32)
    cosf = jnp.concatenate([jnp.ones((rows, KR_LANE), f32), cos, cos, pad], axis=1)
    sinf = jnp.concatenate([jnp.zeros((rows, KR_LANE), f32), -sin, sin, pad], axis=1)
    return cosf, sinf


WEIGHTS = ['g_mix', 'w_in', 'q_norm', 'w_uq', 'kv_norm', 'w_ukv', 'w_pool', 'pool_scale', 'ssd_conv_w', 'ssd_conv_b',
           'ssd_dt_bias', 'ssd_a_log', 'ssd_d', 'ssd_norm', 'lru_conv_w', 'lru_conv_b', 'lru_w_a', 'lru_b_a', 'lru_w_i',
           'lru_b_i', 'lru_lambda', 'w_branch', 'w_out', 'g_mlp', 'w_ff1', 'w_ff2', 'g_ple', 'w_ple_gate', 'w_ple', 'g_final']
SHARDED = dict(w_in=2, w_uq=2, w_ukv=2, ssd_conv_w=2, lru_conv_w=2, w_branch=3, w_out=1, w_ff1=2, w_ff2=1,
               w_ple_gate=1, w_ple=2)
F32_PAYLOAD = ("ssd_conv_w", "lru_conv_w")
DEPTH = 2


SMALL = [n for n in WEIGHTS if n not in SHARDED and n != "g_final"]


def local_step(x, p, positions, tgt, sp, ctx):
    cosf, sinf = _rope_tables(positions)
    res = []
    for l in range(DEPTH):
        x, r = _layer_fwd(x, p[l].astype(bf16), ctx, l, _layer_params(sp, l), cosf, sinf)
        res.append(r)
    loss8, dx, dgf = loss_head(x, tgt, sp["g_final"][None, :], _pick(x.shape[0], 512))
    tok = None
    pending = ("g_final", None, dgf[0])
    for l in reversed(range(DEPTH)):
        dx, tok, pending = _layer_bwd(dx, res[l], ctx, l, cosf, sinf, tok, [pending])
    ctx.small("last", [pending])
    return loss8[0, 0], dx


def _payload(name, w):
    return w if name in F32_PAYLOAD else w.astype(bf16)


def _blocks(name, g):
    ax = SHARDED[name] - 1
    shape = list(g.shape)
    shape[ax:ax + 1] = [N_DEV, shape[ax] // N_DEV]
    return _payload(name, jnp.moveaxis(g.reshape(shape), ax, 0))


def _assemble(name, shards):
    ax = SHARDED[name] - 1
    shape = list(shards.shape[1:])
    shape[ax] *= N_DEV
    return jnp.moveaxis(shards, 0, ax).reshape(shape)


class _Exchanges:
    def __init__(self, wts):
        self.wts = wts
        self.ag, self.rs, self.sm = {}, {}, {}
        tok = None
        for l in range(DEPTH):
            for grp, names in GROUPS.items():
                h = exchange_start(f"ag_start_{grp}{l}", [_payload(n, wts[n][l]) for n in names], True,
                                   deps=[] if tok is None else [tok])
                tok = h["token"]
                self.ag[(l, grp)] = h
        self.all_started = tok

    def weights(self, l, grp, after):
        first = (l, grp) == (0, "A")
        got = exchange_wait(f"ag_wait_{grp}{l}", self.ag[(l, grp)], [after, self.all_started] if first else [after])
        out = {}
        for n, a in zip(GROUPS[grp], got):
            out[n] = a if n == "w_in" else _assemble(n, a)
        return out

    def grads(self, l, grp, g):
        h = exchange_start(f"rs_start_{grp}{l}", [g[n] if n == "w_in" else _blocks(n, g[n]) for n in GROUPS[grp]], False)
        self.rs[(l, grp)] = h
        return h["token"]

    def small(self, tag, entries):
        flat = jnp.concatenate([a.reshape(-1) for _, _, a in entries])
        flat = jnp.pad(flat, (0, (-flat.shape[0]) % (8 * LANE))).reshape(-1, LANE)
        h = exchange_start(f"small_start_{tag}", [flat], True)
        self.sm[tag] = (h, [(n, l, a.shape) for n, l, a in entries])
        return h["token"]

    def collect(self, groups, after):
        parts = {}
        for grp in groups:
            for l in reversed(range(DEPTH)):
                got = exchange_wait(f"rs_wait_{grp}{l}", self.rs[(l, grp)], [after])
                for n, a in zip(GROUPS[grp], got):
                    parts.setdefault(n, [None] * DEPTH)[l] = a
        return parts

    def collect_small(self, after):
        parts = {}
        for tag, (h, layout) in self.sm.items():
            (got,) = exchange_wait(f"small_wait_{tag}", h, [after])
            got = got.reshape(N_DEV, -1)
            off = 0
            for n, l, shape in layout:
                size = 1
                for d in shape:
                    size *= d
                part = got[:, off:off + size].reshape((N_DEV,) + tuple(shape))
                off += size
                if l is None:
                    parts[n] = [part]
                else:
                    parts.setdefault(n, [None] * DEPTH)[l] = part
        return parts


def kernel(x, p, positions, g_mix, w_in, q_norm, w_uq, kv_norm, w_ukv, w_pool, pool_scale, ssd_conv_w, ssd_conv_b,
           ssd_dt_bias, ssd_a_log, ssd_d, ssd_norm, lru_conv_w, lru_conv_b, lru_w_a, lru_b_a, lru_w_i, lru_b_i,
           lru_lambda, w_branch, w_out, g_mlp, w_ff1, w_ff2, g_ple, w_ple_gate, w_ple, g_final, loss_target, m_g_mix,
           m_w_in, m_q_norm, m_w_uq, m_kv_norm, m_w_ukv, m_w_pool, m_pool_scale, m_ssd_conv_w, m_ssd_conv_b,
           m_ssd_dt_bias, m_ssd_a_log, m_ssd_d, m_ssd_norm, m_lru_conv_w, m_lru_conv_b, m_lru_w_a, m_lru_b_a,
           m_lru_w_i, m_lru_b_i, m_lru_lambda, m_w_branch, m_w_out, m_g_mlp, m_w_ff1, m_w_ff2, m_g_ple, m_w_ple_gate,
           m_w_ple, m_g_final, v_g_mix, v_w_in, v_q_norm, v_w_uq, v_kv_norm, v_w_ukv, v_w_pool, v_pool_scale,
           v_ssd_conv_w, v_ssd_conv_b, v_ssd_dt_bias, v_ssd_a_log, v_ssd_d, v_ssd_norm, v_lru_conv_w, v_lru_conv_b,
           v_lru_w_a, v_lru_b_a, v_lru_w_i, v_lru_b_i, v_lru_lambda, v_w_branch, v_w_out, v_g_mlp, v_w_ff1, v_w_ff2,
           v_g_ple, v_w_ple_gate, v_w_ple, v_g_final):
    given = dict(locals())
    wts = {n: given[n] for n in WEIGHTS}
    ctx = _Exchanges(wts)
    loss, grad_x = local_step(x[0], p[:, 0], positions[0], loss_target[0], wts, ctx)

    def update(parts):
        out = {}
        for n, eight in parts.items():
            w, m, v = wts[n], given["m_" + n], given["v_" + n]
            if n == "g_final":
                out[n] = [a[0] for a in adamw(f"adamw_{n}", eight, w[None], m[None], v[None])]
            else:
                out[n] = adamw(f"adamw_{n}", eight, w, m, v)
        return out

    outs = update(ctx.collect(("C", "B"), grad_x))
    late = outs["w_ff1"][1]
    outs.update(update(ctx.collect(("A",), late)))
    outs.update(update(ctx.collect_small(late)))
    loss = lax.psum(loss, AXES)
    return (loss, grad_x[None], *[outs[n][0] for n in WEIGHTS], *[outs[n][1] for n in WEIGHTS],
            *[outs[n][2] for n in WEIGHTS], *[outs[n][3] for n in WEIGHTS])
```

```python
import functools

import jax
import jax.numpy as jnp
from jax import lax
from jax.experimental import pallas as pl
from jax.experimental.pallas import tpu as pltpu

f32 = jnp.float32
bf16 = jnp.bfloat16

D_MODEL = 1024
MIX = 512
N_HEADS = 8
QK_NOPE, QK_ROPE, V_HEAD = 64, 32, 64
Q_LORA, KV_LORA = 384, 256
ROPE_THETA = 10000.0
POOL_WINDOWS = (2, 4, 8, 16)
SSD_CHUNK = 128
SSD_XBC = 768
CONV_W = 4
LRU_C = 8.0
D_FF = 4096
EPS = 1e-6
IN_COLS = 7592
ADAM_LR, ADAM_B1, ADAM_B2, ADAM_EPS, ADAM_WD, ADAM_STEP = 0.001, 0.9, 0.999, 1e-08, 0.01, 10

LANE = 128
HALO = 8
POOL_HALO = 16
VMEM_LIMIT = 56 * 1024 * 1024
MATMUL_MAX_K_TILE = 4096
MATMUL_ACC_PASS_WEIGHT = 0.3
MATMUL_VMEM_BUDGET = 32 * 1024 * 1024
N_DEV = 8
AXES = ("x", "y", "c")

U_COLS = 8192
U_GATES, U_POOL, U_Z, U_LG, U_LX, U_XBC, U_CQ, U_KR, U_CKV, U_DT = (
    (0, 4096), (4096, 512), (4608, 512), (5120, 512), (5632, 512), (6144, 768),
    (6912, 384), (7296, 128), (7424, 256), (7680, 128))
KR_LANE = 64


def _cp(sem):
    return pltpu.CompilerParams(dimension_semantics=sem, vmem_limit_bytes=VMEM_LIMIT)


def _pick(dim, pref):
    if dim <= pref:
        return dim
    t = pref
    while t >= LANE:
        if dim % t == 0:
            return t
        t -= LANE
    t = pref
    while dim % t:
        t -= 8
    return t


@functools.partial(jax.custom_vjp, nondiff_argnums=(1,))
def shift_down(x, k):
    row = lax.broadcasted_iota(jnp.int32, x.shape, 0)
    return jnp.where(row >= k, pltpu.roll(x, k, 0), 0.0)


def _shift_down_fwd(x, k):
    return shift_down(x, k), None


def _shift_down_bwd(k, _, g):
    r = g.shape[0]
    row = lax.broadcasted_iota(jnp.int32, g.shape, 0)
    return (jnp.where(row < r - k, pltpu.roll(g, r - k, 0), 0.0),)


shift_down.defvjp(_shift_down_fwd, _shift_down_bwd)


@functools.partial(jax.custom_vjp, nondiff_argnums=(1,))
def lane_roll(x, s):
    return pltpu.roll(x, s, 1)


def _lane_roll_fwd(x, s):
    return lane_roll(x, s), None


def _lane_roll_bwd(s, _, g):
    return (pltpu.roll(g, (g.shape[1] - s) % g.shape[1], 1),)


lane_roll.defvjp(_lane_roll_fwd, _lane_roll_bwd)


def _tile_spec(tm, width, cb, n=None):
    if n is None:
        return pl.BlockSpec((tm, width), lambda i: (i, cb))
    return pl.BlockSpec((tm, width), lambda i: (n - 1 - i, cb))


def _const_spec(shape):
    nd = len(shape)
    return pl.BlockSpec(shape, lambda i: (0,) * nd)


def seq_fwd(name, f, params, tiles, carries, outs, tm):
    rows = tiles[0][0].shape[0]
    n = rows // tm
    np_, nt, no, nc = len(params), len(tiles), len(outs), len(carries)

    def body(*refs):
        p_refs = refs[:np_]
        t_refs = refs[np_:np_ + nt]
        o_refs = refs[np_ + nt:np_ + nt + no]
        s_refs = refs[np_ + nt + no:np_ + nt + no + nc]
        c_refs = refs[np_ + nt + no + nc:]
        i = pl.program_id(0)

        @pl.when(i == 0)
        def _():
            for c in c_refs:
                c[...] = jnp.zeros_like(c)

        cvals = [c[...] for c in c_refs]
        for s, c in zip(s_refs, cvals):
            s[0] = c
        o, newc = f(i, [r[...] for r in p_refs], cvals, [r[...].astype(f32) for r in t_refs])
        for r, v in zip(o_refs, o):
            r[...] = v.astype(r.dtype)
        for r, v in zip(c_refs, newc):
            r[...] = v

    in_specs = [_const_spec(p.shape) for p in params] + [_tile_spec(tm, w, cb) for (_, w, cb) in tiles]
    out_specs = [_tile_spec(tm, w, 0) for (w, _) in outs]
    out_specs += [pl.BlockSpec((1,) + tuple(c), lambda i, nd=len(c): (i,) + (0,) * nd) for c in carries]
    out_shape = [jax.ShapeDtypeStruct((rows, w), dt) for (w, dt) in outs]
    out_shape += [jax.ShapeDtypeStruct((n,) + tuple(c), f32) for c in carries]
    res = pl.pallas_call(
        body, name=name, grid=(n,), in_specs=in_specs, out_specs=out_specs, out_shape=out_shape,
        scratch_shapes=[pltpu.VMEM(tuple(c), f32) for c in carries],
        compiler_params=_cp(("arbitrary",)),
    )(*params, *[t[0] for t in tiles])
    return list(res[:no]), list(res[no:])


def seq_bwd(name, f, params, tiles, diff, saved, douts, gdtypes, tm, add_to=None):
    rows = tiles[0][0].shape[0]
    n = rows // tm
    np_, nt, nc, nd = len(params), len(tiles), len(saved), len(douts)
    didx = [k for k, d in enumerate(diff) if d]
    ng = len(didx)
    has_add = add_to is not None

    def body(*refs):
        p_refs = refs[:np_]
        t_refs = refs[np_:np_ + nt]
        s_refs = refs[np_ + nt:np_ + nt + nc]
        d_refs = refs[np_ + nt + nc:np_ + nt + nc + nd]
        pos = np_ + nt + nc + nd
        a_ref = refs[pos] if has_add else None
        pos += 1 if has_add else 0
        dp_refs = refs[pos:pos + np_]
        dt_refs = refs[pos + np_:pos + np_ + ng]
        dc_refs = refs[pos + np_ + ng:]
        i = pl.program_id(0)
        step = n - 1 - i

        @pl.when(i == 0)
        def _():
            for r in dp_refs:
                r[...] = jnp.zeros_like(r)
            for r in dc_refs:
                r[...] = jnp.zeros_like(r)

        pvals = [r[...] for r in p_refs]
        cvals = [r[0] for r in s_refs]
        xvals = [r[...].astype(f32) for r in t_refs]

        def fn(p, c, xd):
            x = list(xvals)
            for k, v in zip(didx, xd):
                x[k] = v
            return f(step, p, c, x)

        _, vjp = jax.vjp(fn, pvals, cvals, [xvals[k] for k in didx])
        dp, dc, dx = vjp(([r[...].astype(f32) for r in d_refs], [r[...] for r in dc_refs]))
        for r, v in zip(dp_refs, dp):
            r[...] += v
        for r, v in zip(dc_refs, dc):
            r[...] = v
        for k, (r, v) in enumerate(zip(dt_refs, dx)):
            if has_add and k == add_to[0]:
                v = v + a_ref[...].astype(f32)
            r[...] = v.astype(r.dtype)

    in_specs = [_const_spec(p.shape) for p in params] + [_tile_spec(tm, w, cb, n) for (_, w, cb) in tiles]
    in_specs += [pl.BlockSpec((1,) + tuple(s.shape[1:]), lambda i, nd_=s.ndim - 1: (n - 1 - i,) + (0,) * nd_) for s in saved]
    in_specs += [_tile_spec(tm, d.shape[1], 0, n) for d in douts]
    args = list(params) + [t[0] for t in tiles] + list(saved) + list(douts)
    if has_add:
        in_specs.append(_tile_spec(tm, add_to[1].shape[1], 0, n))
        args.append(add_to[1])
    out_specs = [_const_spec(p.shape) for p in params] + [_tile_spec(tm, tiles[k][1], 0, n) for k in didx]
    out_shape = [jax.ShapeDtypeStruct(p.shape, f32) for p in params]
    out_shape += [jax.ShapeDtypeStruct((rows, tiles[k][1]), dt) for k, dt in zip(didx, gdtypes)]
    res = pl.pallas_call(
        body, name=name, grid=(n,), in_specs=in_specs, out_specs=out_specs, out_shape=out_shape,
        scratch_shapes=[pltpu.VMEM(tuple(s.shape[1:]), f32) for s in saved],
        compiler_params=_cp(("arbitrary",)),
    )(*args)
    return list(res[:np_]), list(res[np_:])


def _halvings(dim, lo, hi):
    t, out = _pick(dim, hi), []
    while t >= min(lo, dim) and dim % t == 0:
        out.append(t)
        if t % 2 or (t // 2) % 8:
            break
        t //= 2
    return out


def _matmul_tiles(m, n, k, a_item, b_item, per_out, max_tn=1024):
    def vmem_bytes(tm, tn, tk):
        acc = 4 if k // tk > 1 else 0
        return 2 * (tm * tk * a_item + tk * tn * b_item + tm * tn * per_out) + tm * tn * acc

    def traffic(tm, tn, tk):
        nk = k // tk
        return (m * k * a_item * (1 if nk == 1 else n // tn) + k * n * b_item * (m // tm)
                + (nk - 1) * m * n * 8 * MATMUL_ACC_PASS_WEIGHT)

    cands = [(traffic(tm, tn, tk), -tm * tn, tm, tn, tk)
             for tk in _halvings(k, 512, MATMUL_MAX_K_TILE) for tm in _halvings(m, 256, 4096)
             for tn in _halvings(n, 512, min(1024, max_tn))
             if vmem_bytes(tm, tn, tk) <= MATMUL_VMEM_BUDGET]
    return min(cands)[2:]


def matmul(name, a, b, *, ta=False, tb=False, outs=(f32,), epi=None, extras=(), deps=(), out_blocks=0):
    m, k = (a.shape[1], a.shape[0]) if ta else a.shape
    n = b.shape[0] if tb else b.shape[1]
    per_out = sum(jnp.dtype(dt).itemsize for dt in outs) + sum(e[0].dtype.itemsize for e in extras)
    tm, tn, tk = _matmul_tiles(m, n, k, a.dtype.itemsize, b.dtype.itemsize, per_out, n // out_blocks if out_blocks else n)
    nk = k // tk
    ne = len(extras)
    dims = (((0 if ta else 1,), (1 if tb else 0,)), ((), ()))

    def body(*refs):
        a_ref, b_ref = refs[0], refs[1]
        e_refs = refs[2:2 + ne]
        o_refs = refs[2 + ne + len(deps):2 + ne + len(deps) + len(outs)]
        kk = pl.program_id(2)
        part = lax.dot_general(a_ref[...].astype(bf16), b_ref[...].astype(bf16), dims, preferred_element_type=f32)

        def finish(total):
            res = (total,) if epi is None else epi(total, *[e[...] for e in e_refs])
            for r, v in zip(o_refs, res):
                r[...] = v.astype(r.dtype)

        if nk == 1:
            finish(part)
            return
        acc = refs[-1]

        @pl.when(kk == 0)
        def _():
            acc[...] = part

        @pl.when(jnp.logical_and(kk > 0, kk < nk - 1))
        def _():
            acc[...] += part

        @pl.when(kk == nk - 1)
        def _():
            finish(acc[...] + part)

    a_spec = pl.BlockSpec((tk, tm), lambda i, j, q: (q, i)) if ta else pl.BlockSpec((tm, tk), lambda i, j, q: (i, q))
    b_spec = pl.BlockSpec((tn, tk), lambda i, j, q: (j, q)) if tb else pl.BlockSpec((tk, tn), lambda i, j, q: (q, j))
    assert all(off % tn == 0 for (_, off) in extras)
    e_specs = [pl.BlockSpec((tm, tn), lambda i, j, q, off=off // tn: (i, off + j)) for (_, off) in extras]
    if out_blocks:
        per = n // out_blocks // tn
        out_spec = pl.BlockSpec((None, tm, tn), lambda i, j, q: (j // per, i, j % per))
        out_dims = (out_blocks, m, n // out_blocks)
    else:
        out_spec = pl.BlockSpec((tm, tn), lambda i, j, q: (i, j))
        out_dims = (m, n)
    res = pl.pallas_call(
        body, name=name, grid=(m // tm, n // tn, nk),
        in_specs=[a_spec, b_spec] + e_specs + [pl.BlockSpec(memory_space=pl.ANY) for _ in deps],
        out_specs=[out_spec for _ in outs],
        out_shape=[jax.ShapeDtypeStruct(out_dims, dt) for dt in outs],
        scratch_shapes=[pltpu.VMEM((tm, tn), f32)] if nk > 1 else [],
        compiler_params=_cp(("parallel", "parallel", "arbitrary")),
    )(a, b, *[e[0] for e in extras], *deps)
    return res[0] if len(outs) == 1 else tuple(res)


def merge_fwd(name, ys, wbs, u):
    rows, n_out = ys[0].shape[0], wbs[0].shape[1]
    nb = len(ys)
    tm, tn = _pick(rows, 512), _pick(n_out, 512)

    def body(*refs):
        y_refs, w_refs, g_refs = refs[:nb], refs[nb:2 * nb], refs[2 * nb:3 * nb]
        m_ref, p_refs = refs[3 * nb], refs[3 * nb + 1:]
        total = None
        for y_ref, w_ref, g_ref, p_ref in zip(y_refs, w_refs, g_refs, p_refs):
            pre = jnp.dot(y_ref[...], w_ref[...], preferred_element_type=f32)
            p_ref[...] = pre.astype(p_ref.dtype)
            term = jax.nn.sigmoid(g_ref[...]) * pre
            total = term if total is None else total + term
        m_ref[...] = total.astype(m_ref.dtype)

    in_specs = [pl.BlockSpec((tm, y.shape[1]), lambda i, j: (i, 0)) for y in ys]
    in_specs += [pl.BlockSpec((w.shape[0], tn), lambda i, j: (0, j)) for w in wbs]
    in_specs += [pl.BlockSpec((tm, tn), lambda i, j, off=n * (n_out // tn): (i, off + j)) for n in range(nb)]
    out_spec = pl.BlockSpec((tm, tn), lambda i, j: (i, j))
    res = pl.pallas_call(
        body, name=name, grid=(rows // tm, n_out // tn), in_specs=in_specs, out_specs=[out_spec] * (nb + 1),
        out_shape=[jax.ShapeDtypeStruct((rows, n_out), bf16)] * (nb + 1),
        compiler_params=_cp(("parallel", "parallel")),
    )(*ys, *wbs, *([u] * nb))
    return res[0], list(res[1:])


ATT_SCALE = (QK_NOPE + QK_ROPE) ** -0.5
LN2 = 0.6931471805599453
ATT_C = ATT_SCALE / LN2
NT = (((1,), (1,)), ((), ()))
TN = (((0,), (0,)), ((), ()))


def _causal(tq, tk):
    return lax.broadcasted_iota(jnp.int32, (tq, tk), 0) >= lax.broadcasted_iota(jnp.int32, (tq, tk), 1)


def _tri_pairs(n, by_column):
    if by_column:
        pairs = [(i, j) for j in range(n) for i in range(j, n)]
    else:
        pairs = [(i, j) for i in range(n) for j in range(i + 1)]
    return (jnp.asarray([a for a, _ in pairs], jnp.int32), jnp.asarray([b for _, b in pairs], jnp.int32))


HEADS_PER_STEP = 4
HEAD_PAIR = HEADS_PER_STEP * LANE


def attn_fwd(q, k, v, t):
    rows = q.shape[0]
    n = rows // t
    it, jt = _tri_pairs(n, False)

    def body(it_ref, jt_ref, q_ref, k_ref, v_ref, o_ref, lse_ref, m_s, l_s, acc_s):
        s_id = pl.program_id(1)
        i, j = it_ref[s_id], jt_ref[s_id]

        @pl.when(j == 0)
        def _():
            m_s[...] = jnp.full_like(m_s, -jnp.inf)
            l_s[...] = jnp.zeros_like(l_s)
            acc_s[...] = jnp.zeros_like(acc_s)

        def step(diag):
            for hh in range(HEADS_PER_STEP):
                sl = slice(LANE * hh, LANE * (hh + 1))
                s = lax.dot_general(q_ref[:, sl], k_ref[:, sl], NT, preferred_element_type=f32)
                if diag:
                    s = jnp.where(_causal(t, t), s, -jnp.inf)
                m_prev = m_s[:, sl]
                m_new = jnp.maximum(m_prev, jnp.max(s, axis=1, keepdims=True))
                alpha = jnp.exp2(m_prev - m_new)
                p = jnp.exp2(s - m_new[:, :1])
                l_s[:, sl] = alpha * l_s[:, sl] + jnp.sum(p, axis=1, keepdims=True)
                acc_s[:, sl] = alpha * acc_s[:, sl] + jnp.dot(p.astype(bf16), v_ref[:, sl], preferred_element_type=f32)
                m_s[:, sl] = m_new

        pl.when(j < i)(lambda: step(False))

        @pl.when(j == i)
        def _():
            step(True)
            o_ref[...] = (acc_s[...] / l_s[...]).astype(o_ref.dtype)
            lse_ref[...] = m_s[...] + jnp.log2(l_s[...])

    qs = pl.BlockSpec((t, HEAD_PAIR), lambda h, s, it_, jt_: (it_[s], h))
    ks = pl.BlockSpec((t, HEAD_PAIR), lambda h, s, it_, jt_: (jt_[s], h))
    hw = N_HEADS * LANE
    return pl.pallas_call(
        body, name="attn_fwd",
        grid_spec=pltpu.PrefetchScalarGridSpec(
            num_scalar_prefetch=2, grid=(hw // HEAD_PAIR, it.shape[0]), in_specs=[qs, ks, ks], out_specs=[qs, qs],
            scratch_shapes=[pltpu.VMEM((t, HEAD_PAIR), f32)] * 3),
        out_shape=[jax.ShapeDtypeStruct((rows, hw), bf16), jax.ShapeDtypeStruct((rows, hw), f32)],
        compiler_params=_cp(("parallel", "arbitrary")),
    )(it, jt, q, k, v)


def attn_bwd(q, k, v, do, o, lse, t):
    rows = q.shape[0]
    n = rows // t
    it, jt = _tri_pairs(n, True)

    def body(it_ref, jt_ref, q_ref, k_ref, v_ref, do_ref, o_ref, lse_ref, dq_ref, dk_ref, dv_ref, dk_s, dv_s):
        s_id = pl.program_id(1)
        i, j = it_ref[s_id], jt_ref[s_id]

        @pl.when(s_id == 0)
        def _():
            dq_ref[...] = jnp.zeros_like(dq_ref)

        @pl.when(i == j)
        def _():
            dk_s[...] = jnp.zeros_like(dk_s)
            dv_s[...] = jnp.zeros_like(dv_s)

        q_rows = pl.ds(pl.multiple_of(i * t, t), t)

        def step(diag):
            for hh in range(HEADS_PER_STEP):
                sl = slice(LANE * hh, LANE * (hh + 1))
                qh, kh, vh, doh = q_ref[:, sl], k_ref[:, sl], v_ref[:, sl], do_ref[:, sl]
                s = lax.dot_general(qh, kh, NT, preferred_element_type=f32)
                p = jnp.exp2(s - lse_ref[:, sl][:, :1])
                if diag:
                    p = jnp.where(_causal(t, t), p, 0.0)
                dp = lax.dot_general(doh, vh, NT, preferred_element_type=f32)
                delta = jnp.sum(doh.astype(f32) * o_ref[:, sl].astype(f32), axis=1, keepdims=True)
                ds = (p * (dp - delta) * LN2).astype(bf16)
                dv_s[:, sl] += lax.dot_general(p.astype(bf16), doh, TN, preferred_element_type=f32)
                dk_s[:, sl] += lax.dot_general(ds, qh, TN, preferred_element_type=f32)
                dq_ref[q_rows, sl] += jnp.dot(ds, kh, preferred_element_type=f32)

        pl.when(i > j)(lambda: step(False))
        pl.when(i == j)(lambda: step(True))

        @pl.when(i == n - 1)
        def _():
            dk_ref[...] = dk_s[...]
            dv_ref[...] = dv_s[...]

    qs = pl.BlockSpec((t, HEAD_PAIR), lambda h, s, it_, jt_: (it_[s], h))
    ks = pl.BlockSpec((t, HEAD_PAIR), lambda h, s, it_, jt_: (jt_[s], h))
    dqs = pl.BlockSpec((rows, HEAD_PAIR), lambda h, s, it_, jt_: (0, h))
    hw = N_HEADS * LANE
    return pl.pallas_call(
        body, name="attn_bwd",
        grid_spec=pltpu.PrefetchScalarGridSpec(
            num_scalar_prefetch=2, grid=(hw // HEAD_PAIR, it.shape[0]), in_specs=[qs, ks, ks, qs, qs, qs],
            out_specs=[dqs, ks, ks], scratch_shapes=[pltpu.VMEM((t, HEAD_PAIR), f32)] * 2),
        out_shape=[jax.ShapeDtypeStruct((rows, hw), f32)] * 3,
        compiler_params=_cp(("parallel", "arbitrary")),
    )(it, jt, q, k, v, do, o, lse)


def _steps(tm):
    k, out = 1, []
    while k < tm:
        out.append(k)
        k *= 2
    return out


def scan_fwd(a, u, tm):
    rows, ch = a.shape
    n = rows // tm

    def body(a_ref, u_ref, h_ref, h_s):
        @pl.when(pl.program_id(0) == 0)
        def _():
            h_s[...] = jnp.zeros_like(h_s)

        av, bv = a_ref[...], u_ref[...]
        row = lax.broadcasted_iota(jnp.int32, av.shape, 0)
        for k in _steps(tm):
            a_sh = jnp.where(row >= k, pltpu.roll(av, k, 0), 1.0)
            b_sh = jnp.where(row >= k, pltpu.roll(bv, k, 0), 0.0)
            bv = av * b_sh + bv
            av = av * a_sh
        h = bv + av * h_s[HALO - 1:HALO, :]
        h_ref[...] = h
        h_s[...] = h[tm - HALO:, :]

    spec = pl.BlockSpec((tm, ch), lambda i: (i, 0))
    return pl.pallas_call(
        body, name="lru_scan_fwd", grid=(n,), in_specs=[spec, spec], out_specs=spec,
        out_shape=jax.ShapeDtypeStruct((rows, ch), f32), scratch_shapes=[pltpu.VMEM((HALO, ch), f32)],
        compiler_params=_cp(("arbitrary",)),
    )(a, u)


def scan_bwd(a, h, dh, tm):
    rows, ch = a.shape
    n = rows // tm
    per = tm // HALO

    def body(a_ref, h_ref, hp_ref, dh_ref, da_ref, du_ref, g_s, a_s):
        i = pl.program_id(0)
        step = n - 1 - i

        @pl.when(i == 0)
        def _():
            g_s[...] = jnp.zeros_like(g_s)
            a_s[...] = jnp.zeros_like(a_s)

        a0 = a_ref[...]
        row = lax.broadcasted_iota(jnp.int32, a0.shape, 0)
        av = jnp.where(row < tm - 1, pltpu.roll(a0, tm - 1, 0), a_s[0:1, :])
        bv = dh_ref[...]
        for k in _steps(tm):
            a_sh = jnp.where(row < tm - k, pltpu.roll(av, tm - k, 0), 1.0)
            b_sh = jnp.where(row < tm - k, pltpu.roll(bv, tm - k, 0), 0.0)
            bv = bv + av * b_sh
            av = av * a_sh
        g = bv + av * g_s[0:1, :]
        h_last = jnp.where(step > 0, hp_ref[HALO - 1:HALO, :], 0.0)
        h_prev = jnp.where(row >= 1, pltpu.roll(h_ref[...], 1, 0), h_last)
        du_ref[...] = g
        da_ref[...] = g * h_prev
        g_s[...] = g[0:HALO, :]
        a_s[...] = a0[0:HALO, :]

    spec = pl.BlockSpec((tm, ch), lambda i: (n - 1 - i, 0))
    hp_spec = pl.BlockSpec((HALO, ch), lambda i: (jnp.maximum((n - 1 - i) * per - 1, 0), 0))
    return pl.pallas_call(
        body, name="lru_scan_bwd", grid=(n,), in_specs=[spec, spec, hp_spec, spec], out_specs=[spec, spec],
        out_shape=[jax.ShapeDtypeStruct((rows, ch), f32)] * 2,
        scratch_shapes=[pltpu.VMEM((HALO, ch), f32)] * 2,
        compiler_params=_cp(("arbitrary",)),
    )(a, h, h, dh)


def _rms(x, g):
    return x * lax.rsqrt(jnp.mean(x * x, axis=-1, keepdims=True) + EPS) * g


def f_rms(step, p, c, x):
    return [_rms(x[0], p[0])], []


def _rope(x, cosf, sinf):
    lane = lax.broadcasted_iota(jnp.int32, x.shape, 1)
    sw = jnp.where(lane < KR_LANE + QK_ROPE // 2, lane_roll(x, LANE - QK_ROPE // 2), lane_roll(x, QK_ROPE // 2))
    return x * cosf + sw * sinf


def f_prep(step, p, c, x):
    q, kn, kr, cosf, sinf = x
    kr_rot = _rope(kr, cosf, sinf)
    qr = [_rope(q[:, LANE * h:LANE * (h + 1)], cosf, sinf) * ATT_C for h in range(N_HEADS)]
    kk = [kn[:, LANE * h:LANE * (h + 1)] + kr_rot for h in range(N_HEADS)]
    return [jnp.concatenate(qr, axis=1), jnp.concatenate(kk, axis=1)], []


def _conv(tail, x, w, b):
    xf = jnp.concatenate([tail, x], axis=0)
    acc = b + w[CONV_W - 1:CONV_W, :] * xf
    for k in range(CONV_W - 1):
        acc = acc + w[k:k + 1, :] * shift_down(xf, CONV_W - 1 - k)
    return acc[HALO:, :]


def f_pool(step, p, c, x):
    wp, sc = p
    (tail,) = c
    (u,) = x
    tm = u.shape[0]
    xf = jnp.concatenate([tail, u], axis=0)
    sums, s, w = [], xf, 1
    while w < POOL_WINDOWS[-1]:
        s = s + shift_down(s, w)
        w *= 2
        sums.append(s)
    t = step * tm + lax.broadcasted_iota(jnp.int32, (tm, 1), 0)
    ys = []
    for g, (w, s) in enumerate(zip(POOL_WINDOWS, sums)):
        sl = slice(LANE * g, LANE * (g + 1))
        cnt = jnp.minimum(t + 1, w).astype(f32)
        d = s[POOL_HALO:, sl] / cnt - u[:, sl]
        ys.append(jnp.dot(d.astype(bf16), wp[LANE * g:LANE * (g + 1), :].astype(bf16), preferred_element_type=f32))
    return [jnp.concatenate(ys, axis=1) * sc], [u[tm - POOL_HALO:, :]]


def f_ssd(step, p, c, x):
    conv_w, conv_b, dtb, alog, dsk, ng = p
    tail, s_in = c[0], c[1:]
    z, xbc, dt = x
    ln = z.shape[0]
    xc = jax.nn.silu(_conv(tail, xbc, conv_w, conv_b))
    xs, bb, cc = xc[:, :MIX], xc[:, MIX:MIX + LANE], xc[:, MIX + LANE:]
    dtv = jax.nn.softplus(dt + dtb[0:1, :])
    a = dtv * -jnp.exp(alog[0:1, :])
    ri = lax.broadcasted_iota(jnp.int32, (ln, ln), 0)
    ci = lax.broadcasted_iota(jnp.int32, (ln, ln), 1)
    tril = (ri >= ci).astype(f32)
    triu = (ri <= ci).astype(f32)
    hi = lax.Precision.HIGHEST
    a_cs = jnp.dot(tril, a, precision=hi, preferred_element_type=f32)
    a_cs_t = lax.dot_general(a, triu, TN, precision=hi, preferred_element_type=f32)
    a_tot = jnp.sum(a, axis=0, keepdims=True)
    lane = lax.broadcasted_iota(jnp.int32, (1, LANE), 1)
    half = [(lane < 64).astype(f32), (lane >= 64).astype(f32)]
    hrow = lax.broadcasted_iota(jnp.int32, (LANE, 1), 0)

    def head(v, h):
        return jnp.sum(v * (lane == h).astype(f32), axis=1, keepdims=True)

    def pair(v, j):
        return head(v, 2 * j) * half[0] + head(v, 2 * j + 1) * half[1]

    cg = [(cc * half[g]).astype(bf16) for g in range(2)]
    bg = [(bb * half[g]).astype(bf16) for g in range(2)]
    cb = [lax.dot_general(cg[g], bg[g], NT, preferred_element_type=f32) for g in range(2)]
    ys, s_out = [], []
    for j in range(4):
        g = j // 2
        xs_j = xs[:, LANE * j:LANE * (j + 1)]
        xj = xs_j * pair(dtv, j)
        yj = xs_j * pair(dsk[0:1, :], j)
        for hh in range(2):
            h = 2 * j + hh
            rowv = jnp.sum(a_cs_t * (hrow == h).astype(f32), axis=0, keepdims=True)
            lmat = jnp.exp(jnp.where(ri >= ci, head(a_cs, h) - rowv, -jnp.inf))
            yj = yj + jnp.dot((cb[g] * lmat).astype(bf16), (xj * half[hh]).astype(bf16), preferred_element_type=f32)
        acs = pair(a_cs, j)
        tot = pair(a_tot, j)
        yj = yj + jnp.exp(acs) * jnp.dot(cg[g], s_in[j].astype(bf16), preferred_element_type=f32)
        s_new = jnp.exp(tot) * s_in[j] + lax.dot_general(bg[g], (xj * jnp.exp(tot - acs)).astype(bf16), TN,
                                                         preferred_element_type=f32)
        ys.append(yj)
        s_out.append(s_new)
    y = jnp.concatenate(ys, axis=1) * jax.nn.silu(z)
    return [_rms(y, ng)], [xbc[ln - HALO:, :]] + s_out


def _neg_expm1(y):
    series = -y * (1.0 + y * (0.5 + y * (1.0 / 6 + y * (1.0 / 24 + y * (1.0 / 120)))))
    return jnp.where(y > -0.05, series, 1.0 - jnp.exp(y))


def f_lru_pre(step, p, c, x):
    cw, cb_, wa, ba, wi, bi, lam = p
    (tail,) = c
    (lx,) = x
    tm = lx.shape[0]
    xc = _conv(tail, lx, cw, cb_)
    xb = xc.astype(bf16)
    r = jax.nn.sigmoid(jnp.dot(xb, wa.astype(bf16), preferred_element_type=f32) + ba)
    it = jax.nn.sigmoid(jnp.dot(xb, wi.astype(bf16), preferred_element_type=f32) + bi)
    log_a = -LRU_C * r * jax.nn.softplus(-lam)
    mult = jnp.sqrt(_neg_expm1(2.0 * log_a))
    return [jnp.exp(log_a), xc * it * mult], [lx[tm - HALO:, :]]


def f_lru_post(step, p, c, x):
    h, g = x
    return [h * jax.nn.gelu(g)], []


def loss_head(x, tgt, g, tm):
    rows, d = x.shape
    n = rows // tm

    def body(x_ref, t_ref, g_ref, loss_ref, dx_ref, dg_ref):
        @pl.when(pl.program_id(0) == 0)
        def _():
            loss_ref[...] = jnp.zeros_like(loss_ref)
            dg_ref[...] = jnp.zeros_like(dg_ref)

        def fn(gv, xv):
            err = _rms(xv, gv) - t_ref[...]
            return 0.5 * jnp.sum(jnp.mean(err * err, axis=-1, keepdims=True))

        val, (dg, dx) = jax.value_and_grad(fn, argnums=(0, 1))(g_ref[...], x_ref[...])
        loss_ref[...] += val
        dg_ref[...] += dg
        dx_ref[...] = dx

    spec = pl.BlockSpec((tm, d), lambda i: (i, 0))
    return pl.pallas_call(
        body, name="loss_head", grid=(n,), in_specs=[spec, spec, _const_spec((1, d))],
        out_specs=[_const_spec((8, LANE)), spec, _const_spec((1, d))],
        out_shape=[jax.ShapeDtypeStruct((8, LANE), f32), jax.ShapeDtypeStruct((rows, d), f32),
                   jax.ShapeDtypeStruct((1, d), f32)],
        compiler_params=_cp(("arbitrary",)),
    )(x, tgt, g)


def ew(name, fn, ins, outs, tm):
    rows = ins[0][0].shape[0]
    ni = len(ins)

    def body(*refs):
        res = fn(*[r[...].astype(f32) for r in refs[:ni]])
        for r, v in zip(refs[ni:], res):
            r[...] = v.astype(r.dtype)

    return pl.pallas_call(
        body, name=name, grid=(rows // tm,), in_specs=[_tile_spec(tm, w, cb) for (_, w, cb) in ins],
        out_specs=[_tile_spec(tm, w, 0) for (w, _) in outs],
        out_shape=[jax.ShapeDtypeStruct((rows, w), dt) for (w, dt) in outs],
        compiler_params=_cp(("parallel",)),
    )(*[t[0] for t in ins])


def _peers():
    x, y, c = lax.axis_index("x"), lax.axis_index("y"), lax.axis_index("c")
    me = 4 * x + 2 * y + c
    out = []
    for k in range(1, N_DEV):
        px = 1 - x if k & 4 else x
        py = 1 - y if k & 2 else y
        pc = 1 - c if k & 1 else c
        out.append(((px, py, pc), 4 * px + 2 * py + pc))
    return me, out


_HBM = pl.BlockSpec(memory_space=pltpu.HBM)
_SEM = pl.BlockSpec(memory_space=pltpu.SEMAPHORE)
_EFFECT = pltpu.SideEffectType.DATAFLOW_SIDE_EFFECTING


def _remote(src_ref, land_ref, gather, me, pid, dev, send_sems, recv_sems, k, recv_side):
    return pltpu.make_async_remote_copy(
        src_ref=src_ref if gather else src_ref.at[pid], dst_ref=land_ref.at[pid if recv_side else me],
        send_sem=send_sems.at[k], recv_sem=recv_sems.at[k], device_id=dev, device_id_type=pl.DeviceIdType.MESH)


def _own(src_ref, land_ref, gather, me, sem):
    return pltpu.make_async_copy(src_ref if gather else src_ref.at[me], land_ref.at[me], sem)


def exchange_start(name, srcs, gather, deps=()):
    n, nd = len(srcs), len(deps)
    shapes = [(s.shape if gather else s.shape[1:]) for s in srcs]
    lands = [lax.empty((N_DEV,) + tuple(sh), s.dtype) for s, sh in zip(srcs, shapes)]

    def body(*refs):
        src_refs, land_refs = refs[:n], refs[n:2 * n]
        send_sems, recv_sems, own_sem = refs[2 * n + nd:2 * n + nd + 3]
        token = refs[-1]
        me, peers = _peers()
        for k, (dev, pid) in enumerate(peers):
            for s_ref, l_ref in zip(src_refs, land_refs):
                _remote(s_ref, l_ref, gather, me, pid, dev, send_sems, recv_sems, k, False).start()
        for s_ref, l_ref in zip(src_refs, land_refs):
            _own(s_ref, l_ref, gather, me, own_sem).start()
        token[...] = jnp.zeros_like(token)

    hbm = lambda a: pltpu.with_memory_space_constraint(a, pltpu.HBM)
    res = pl.pallas_call(
        body, name=name,
        out_shape=(pltpu.SemaphoreType.DMA((N_DEV - 1,)), pltpu.SemaphoreType.DMA((N_DEV - 1,)), pltpu.SemaphoreType.DMA(()),
                   *[pltpu.HBM(a.shape, a.dtype) for a in list(srcs) + lands], jax.ShapeDtypeStruct((8, LANE), f32)),
        in_specs=[_HBM] * (2 * n) + [pl.BlockSpec(memory_space=pl.ANY)] * nd,
        out_specs=(_SEM, _SEM, _SEM, *([_HBM] * (2 * n)), pl.BlockSpec(memory_space=pltpu.VMEM)),
        input_output_aliases={i: 3 + i for i in range(2 * n)},
        compiler_params=pltpu.CompilerParams(has_side_effects=_EFFECT),
    )(*[hbm(a) for a in list(srcs) + lands], *deps)
    return dict(sems=res[:3], srcs=list(res[3:3 + n]), lands=list(res[3 + n:3 + 2 * n]), token=res[-1], gather=gather)


def exchange_wait(name, h, afters):
    n, gather = len(h["srcs"]), h["gather"]

    def body(*refs):
        src_refs, land_refs = refs[:n], refs[n:2 * n]
        send_sems, recv_sems, own_sem = refs[2 * n:2 * n + 3]
        me, peers = _peers()
        for k, (dev, pid) in enumerate(peers):
            for s_ref, l_ref in zip(src_refs, land_refs):
                _remote(s_ref, l_ref, gather, me, pid, dev, send_sems, recv_sems, k, True).wait_recv()
        for k, (dev, pid) in enumerate(peers):
            for s_ref, l_ref in zip(src_refs, land_refs):
                _remote(s_ref, l_ref, gather, me, pid, dev, send_sems, recv_sems, k, False).wait_send()
        for s_ref, l_ref in zip(src_refs, land_refs):
            _own(s_ref, l_ref, gather, me, own_sem).wait()

    arrs = h["srcs"] + h["lands"]
    res = pl.pallas_call(
        body, name=name, out_shape=tuple(pltpu.HBM(a.shape, a.dtype) for a in arrs),
        in_specs=[_HBM] * (2 * n) + [_SEM, _SEM, _SEM] + [pl.BlockSpec(memory_space=pl.ANY)] * len(afters),
        out_specs=tuple([_HBM] * (2 * n)), input_output_aliases={i: i for i in range(2 * n)},
        compiler_params=pltpu.CompilerParams(has_side_effects=_EFFECT),
    )(*arrs, *h["sems"], *afters)
    return list(res[n:])


def adamw(name, parts, w, m, v):
    nl = len(parts)
    shape = w.shape[1:]
    c = shape[-1]
    r = 1
    for s in shape[:-1]:
        r *= s
    tr = _pick(r, 256) if r % 8 == 0 else r
    nb = r // tr
    parts2 = [p.reshape(N_DEV, r, c) for p in parts]
    w2, m2, v2 = (a.reshape(nl, r, c) for a in (w, m, v))

    def body(*refs):
        p_refs = refs[:nl]
        w_ref, m_ref, v_ref, g_ref, d_ref, nm_ref, nv_ref = refs[nl:]
        layer = pl.program_id(0)
        for ll, p_ref in enumerate(p_refs):
            @pl.when(layer == ll)
            def _(p_ref=p_ref):
                g = p_ref[0].astype(f32)
                for i in range(1, N_DEV):
                    g = g + p_ref[i].astype(f32)
                mn = ADAM_B1 * m_ref[0] + (1.0 - ADAM_B1) * g
                vn = ADAM_B2 * v_ref[0] + (1.0 - ADAM_B2) * jnp.square(g)
                m_hat = mn / (1.0 - ADAM_B1 ** ADAM_STEP)
                v_hat = vn / (1.0 - ADAM_B2 ** ADAM_STEP)
                g_ref[0] = g
                d_ref[0] = -ADAM_LR * (m_hat / (jnp.sqrt(v_hat) + ADAM_EPS) + ADAM_WD * w_ref[0])
                nm_ref[0] = mn
                nv_ref[0] = vn

    def p_spec(ll):
        return pl.BlockSpec((N_DEV, tr, c), lambda l, i: (0, jnp.where(l == ll, i, jnp.where(l > ll, nb - 1, 0)), 0))

    spec = pl.BlockSpec((1, tr, c), lambda l, i: (l, i, 0))
    res = pl.pallas_call(
        body, name=name, grid=(nl, nb), in_specs=[p_spec(ll) for ll in range(nl)] + [spec, spec, spec],
        out_specs=[spec] * 4, out_shape=[jax.ShapeDtypeStruct((nl, r, c), f32)] * 4,
        compiler_params=_cp(("arbitrary", "arbitrary")),
    )(*parts2, w2, m2, v2)
    return [a.reshape(w.shape) for a in res]


_IN_SPLITS = dict(cq=(0, 384), ckv=(384, 640), kr=(640, 672), pool=(672, 1184), z=(1184, 1696), xbc=(1696, 2464),
                  dt=(2464, 2472), lg=(2472, 2984), lx=(2984, 3496), gates=(3496, 7592))


W_IN_SHARD = IN_COLS // N_DEV

_PAD_ORDER = ("gates", "pool", "z", "lg", "lx", "xbc", "cq", KR_LANE, "kr", LANE - KR_LANE - QK_ROPE, "ckv", "dt",
              LANE - 8, U_COLS - U_DT[0] - LANE)
_SEGMENTS = ((0, U_CQ[0], 384), (384, U_CKV[0], 256), (640, U_KR[0] + KR_LANE, QK_ROPE), (672, U_POOL[0], 512),
             (1184, U_Z[0], 512), (1696, U_XBC[0], 768), (2464, U_DT[0], 8), (2472, U_LG[0], 512), (2984, U_LX[0], 512),
             (3496, 0, 4096))


def _pad_w_in(shards):
    rows = shards.shape[1]
    pieces = []
    for item in _PAD_ORDER:
        if isinstance(item, int):
            pieces.append(jnp.zeros((rows, item), shards.dtype))
            continue
        a, b = _IN_SPLITS[item]
        for d in range(a // W_IN_SHARD, (b - 1) // W_IN_SHARD + 1):
            lo, hi = max(a, d * W_IN_SHARD), min(b, (d + 1) * W_IN_SHARD)
            pieces.append(shards[d, :, lo - d * W_IN_SHARD:hi - d * W_IN_SHARD])
    return jnp.concatenate(pieces, axis=1)


def _w_in_blocks(g):
    blocks = []
    for d in range(N_DEV):
        a, b = d * W_IN_SHARD, (d + 1) * W_IN_SHARD
        pieces = []
        for ref, pad, width in _SEGMENTS:
            lo, hi = max(a, ref), min(b, ref + width)
            if lo < hi:
                pieces.append(g[:, pad + lo - ref:pad + hi - ref])
        blocks.append(jnp.concatenate(pieces, axis=1))
    return jnp.stack(blocks).astype(bf16)


def _head_pad_cols(w, per, lo, hi):
    k = w.shape[0]
    w = w.reshape(k, N_HEADS, per)[:, :, lo:hi]
    return jnp.pad(w, ((0, 0), (0, 0), (0, LANE - (hi - lo)))).reshape(k, N_HEADS * LANE)


def _head_unpad_cols(g, n):
    k = g.shape[0]
    return g.reshape(k, N_HEADS, LANE)[:, :, :n]


def _block_diag(w):
    out = jnp.zeros((MIX, MIX), w.dtype)
    for i in range(8):
        out = lax.dynamic_update_slice(out, w[i], (64 * i, 64 * i))
    return out


def _block_diag_inv(g):
    return jnp.stack([g[64 * i:64 * (i + 1), 64 * i:64 * (i + 1)] for i in range(8)])


def _head8(v):
    return jnp.zeros((8, LANE), f32).at[0, :8].set(v)


GROUPS = dict(A=("w_in",), B=("w_uq", "w_ukv", "ssd_conv_w", "lru_conv_w", "w_branch", "w_out"),
              C=("w_ff1", "w_ff2", "w_ple_gate", "w_ple"))


def _kernel_weights(grp, fw):
    if grp == "A":
        return dict(w_in=_pad_w_in(fw["w_in"]))
    if grp == "C":
        return dict(w_ff1=fw["w_ff1"], w_ff2=fw["w_ff2"], w_pg=fw["w_ple_gate"], w_ple=fw["w_ple"])
    wb = fw["w_branch"]
    wb0 = jnp.pad(wb[0].reshape(N_HEADS, V_HEAD, D_MODEL), ((0, 0), (0, LANE - V_HEAD), (0, 0))).reshape(N_HEADS * LANE, D_MODEL)
    return dict(
        w_uq=_head_pad_cols(fw["w_uq"], QK_NOPE + QK_ROPE, 0, QK_NOPE + QK_ROPE),
        w_uk=_head_pad_cols(fw["w_ukv"], QK_NOPE + V_HEAD, 0, QK_NOPE),
        w_uv=_head_pad_cols(fw["w_ukv"], QK_NOPE + V_HEAD, QK_NOPE, QK_NOPE + V_HEAD),
        wb=[wb0, wb[1], wb[2], wb[3]], w_out=fw["w_out"], ssd_conv_w=fw["ssd_conv_w"], lru_conv_w=fw["lru_conv_w"])


def _layer_params(sp, l):
    row = lambda n: sp[n][l][None, :]
    return dict(
        g_mix=row("g_mix"), q_norm=row("q_norm"), kv_norm=row("kv_norm"),
        pool=[sp["w_pool"][l].reshape(4 * LANE, LANE), row("pool_scale")],
        ssd=[None, row("ssd_conv_b"), _head8(sp["ssd_dt_bias"][l]), _head8(sp["ssd_a_log"][l]),
             _head8(sp["ssd_d"][l]), row("ssd_norm")],
        lru=[None, row("lru_conv_b"), _block_diag(sp["lru_w_a"][l]), row("lru_b_a"),
             _block_diag(sp["lru_w_i"][l]), row("lru_b_i"), row("lru_lambda")],
        g_mlp=row("g_mlp"), g_ple=row("g_ple"),
    )


_sig = jax.nn.sigmoid
_SSD_CARRY = [(HALO, SSD_XBC)] + [(LANE, LANE)] * 4


def _tiles(rows):
    return dict(tm=_pick(rows, 512), ta=_pick(rows, 512), tp=_pick(rows, 256), tl=_pick(rows, 256), ts=_pick(rows, 256))


def _mixer_tiles(u):
    return dict(
        cq=(u, 384, U_CQ[0] // 384), ckv=(u, 256, U_CKV[0] // 256), kr=(u, LANE, U_KR[0] // LANE),
        pool=(u, MIX, U_POOL[0] // MIX), z=(u, MIX, U_Z[0] // MIX), xbc=(u, SSD_XBC, U_XBC[0] // SSD_XBC),
        dt=(u, LANE, U_DT[0] // LANE), lg=(u, MIX, U_LG[0] // MIX), lx=(u, MIX, U_LX[0] // MIX))


def _layer_fwd(x, p_bf, ctx, l, pr, cosf, sinf):
    rows = x.shape[0]
    ts = _tiles(rows)
    tm = ts["tm"]
    nm = lambda s: f"{s}_l{l}"
    r = dict(x=x)
    (h,), _ = seq_fwd(nm("rms_in"), f_rms, [pr["g_mix"]], [(x, D_MODEL, 0)], [], [(D_MODEL, bf16)], tm)
    w = dict(_kernel_weights("A", ctx.weights(l, "A", h)))
    u = matmul(nm("w_in"), h, w["w_in"])
    mt = _mixer_tiles(u)
    (cqn,), _ = seq_fwd(nm("rms_q"), f_rms, [pr["q_norm"]], [mt["cq"]], [], [(Q_LORA, bf16)], tm)
    (ckvn,), _ = seq_fwd(nm("rms_kv"), f_rms, [pr["kv_norm"]], [mt["ckv"]], [], [(KV_LORA, bf16)], tm)
    (yb,), pool_saved = seq_fwd(nm("pool"), f_pool, pr["pool"], [mt["pool"]], [(POOL_HALO, MIX)], [(MIX, bf16)], ts["tp"])
    w.update(_kernel_weights("B", ctx.weights(l, "B", yb)))
    pr = dict(pr, ssd=[w["ssd_conv_w"]] + pr["ssd"][1:], lru=[w["lru_conv_w"]] + pr["lru"][1:])
    q = matmul(nm("w_uq"), cqn, w["w_uq"])
    kn = matmul(nm("w_uk"), ckvn, w["w_uk"])
    vb = matmul(nm("w_uv"), ckvn, w["w_uv"], outs=(bf16,))
    hw = N_HEADS * LANE
    (qr, kr), _ = seq_fwd(nm("mla_prep"), f_prep, [], [(q, hw, 0), (kn, hw, 0), mt["kr"], (cosf, LANE, 0), (sinf, LANE, 0)],
                          [], [(hw, bf16), (hw, bf16)], tm)
    o, lse = attn_fwd(qr, kr, vb, ts["ta"])
    (yc,), ssd_saved = seq_fwd(nm("ssd"), f_ssd, pr["ssd"], [mt["z"], mt["xbc"], mt["dt"]], _SSD_CARRY, [(MIX, bf16)], SSD_CHUNK)
    (la, lu), lru_saved = seq_fwd(nm("lru_pre"), f_lru_pre, pr["lru"], [mt["lx"]], [(HALO, MIX)], [(MIX, f32), (MIX, f32)], ts["tl"])
    hh = scan_fwd(la, lu, ts["ts"])
    (yd,), _ = seq_fwd(nm("lru_post"), f_lru_post, [], [(hh, MIX, 0), mt["lg"]], [], [(MIX, bf16)], tm)
    ys = [o, yb, yc, yd]
    m, pres = merge_fwd(nm("merge"), ys, w["wb"], u)
    x1 = matmul(nm("w_out"), m, w["w_out"], epi=lambda acc, xr: (acc + xr,), extras=[(x, 0)])
    (h2,), _ = seq_fwd(nm("rms_mlp"), f_rms, [pr["g_mlp"]], [(x1, D_MODEL, 0)], [], [(D_MODEL, bf16)], tm)
    w.update(_kernel_weights("C", ctx.weights(l, "C", h2)))
    a1, act = matmul(nm("ff1"), h2, w["w_ff1"], outs=(bf16, bf16), epi=lambda acc: (acc, jnp.square(jnp.maximum(acc, 0.0))))
    x2 = matmul(nm("ff2"), act, w["w_ff2"], epi=lambda acc, xr: (acc + xr,), extras=[(x1, 0)])
    (h3,), _ = seq_fwd(nm("rms_ple"), f_rms, [pr["g_ple"]], [(x2, D_MODEL, 0)], [], [(D_MODEL, bf16)], tm)
    gl = matmul(nm("ple_gate"), h3, w["w_pg"])
    x3, pe = matmul(nm("ple"), p_bf, w["w_ple"], outs=(f32, f32), epi=lambda acc, g, xr: (xr + acc * _sig(g), acc),
                    extras=[(gl, 0), (x2, 0)])
    r.update(h=h, u=u, cqn=cqn, ckvn=ckvn, q=q, kn=kn, vb=vb, qr=qr, kr=kr, o=o, lse=lse, ys=ys, pres=pres, m=m, x1=x1,
             h2=h2, a1=a1, act=act, x2=x2, h3=h3, gl=gl, pe=pe, p_bf=p_bf, pool_saved=pool_saved, ssd_saved=ssd_saved,
             lru_saved=lru_saved, la=la, hh=hh, w=w, pr=pr)
    return x3, r


def _gate_bwd(d, g, pre):
    s = _sig(g)
    return d * s, d * pre * s * (1.0 - s)


def _layer_bwd(dx3, r, ctx, l, cosf, sinf, tok, extra_small):
    rows = dx3.shape[0]
    ts = _tiles(rows)
    tm = ts["tm"]
    nm = lambda s: f"{s}_l{l}"
    u, w, pr = r["u"], r["w"], r["pr"]
    mt = _mixer_tiles(u)
    g = {}
    full = lambda a: (a, a.shape[1], 0)
    dpe, dgl = ew(nm("ple_bwd"), _gate_bwd, [full(dx3), full(r["gl"]), full(r["pe"])], [(D_MODEL, bf16)] * 2, tm)
    g["w_ple"] = matmul(nm("d_w_ple"), r["p_bf"], dpe, ta=True, outs=(bf16,), deps=[tok] if tok is not None else [])
    g["w_pg"] = matmul(nm("d_w_pg"), r["h3"], dgl, ta=True, outs=(bf16,))
    dh3 = matmul(nm("d_h3"), dgl, w["w_pg"], tb=True)
    (g["g_ple"],), (dx2,) = seq_bwd(nm("rms_ple_bwd"), f_rms, [pr["g_ple"]], [full(r["x2"])], [True], [], [dh3], [f32], tm,
                                    add_to=(0, dx3))
    da1 = matmul(nm("d_act"), dx2, w["w_ff2"], tb=True, outs=(bf16,),
                 epi=lambda acc, a: (acc * 2.0 * jnp.maximum(a, 0.0),), extras=[(r["a1"], 0)])
    g["w_ff2"] = matmul(nm("d_w_ff2"), r["act"], dx2, ta=True, outs=(bf16,))
    g["w_ff1"] = matmul(nm("d_w_ff1"), r["h2"], da1, ta=True, outs=(bf16,), out_blocks=N_DEV)
    tok = ctx.grads(l, "C", dict(w_ff1=g["w_ff1"], w_ff2=g["w_ff2"], w_ple_gate=g["w_pg"], w_ple=g["w_ple"]))
    dh2 = matmul(nm("d_h2"), da1, w["w_ff1"], tb=True, deps=[tok])
    (g["g_mlp"],), (dx1,) = seq_bwd(nm("rms_mlp_bwd"), f_rms, [pr["g_mlp"]], [full(r["x1"])], [True], [], [dh2], [f32], tm,
                                    add_to=(0, dx2))
    def merge_bwd(dm, *gates_and_pres):
        both = [_gate_bwd(dm, gates_and_pres[n], gates_and_pres[4 + n]) for n in range(4)]
        return tuple(b[0] for b in both) + tuple(b[1] for b in both)

    res = matmul(nm("d_merged"), dx1, w["w_out"], tb=True, outs=(bf16,) * 8, epi=merge_bwd,
                 extras=[(u, D_MODEL * n) for n in range(4)] + [(pre, 0) for pre in r["pres"]])
    dpres, dgates = list(res[:4]), list(res[4:])
    g["w_out"] = matmul(nm("d_w_out"), r["m"], dx1, ta=True, outs=(bf16,))
    dys, g["wb"] = [], []
    for n in range(4):
        g["wb"].append(matmul(nm(f"d_w_branch{n}"), r["ys"][n], dpres[n], ta=True, outs=(bf16,)))
        dys.append(matmul(nm(f"d_y{n}"), dpres[n], w["wb"][n], tb=True, outs=(bf16 if n == 0 else f32,)))
    dqr, dkr_, dv = attn_bwd(r["qr"], r["kr"], r["vb"], dys[0], r["o"], r["lse"], ts["ta"])
    _, (dq, dkn, dkrope) = seq_bwd(nm("mla_prep_bwd"), f_prep, [],
                                   [full(r["q"]), full(r["kn"]), mt["kr"], full(cosf), full(sinf)],
                                   [True, True, True, False, False], [], [dqr, dkr_], [bf16] * 3, tm)
    g["w_uq"] = matmul(nm("d_w_uq"), r["cqn"], dq, ta=True, outs=(bf16,))
    g["w_uk"] = matmul(nm("d_w_uk"), r["ckvn"], dkn, ta=True, outs=(bf16,))
    g["w_uv"] = matmul(nm("d_w_uv"), r["ckvn"], dv, ta=True, outs=(bf16,))
    dcqn = matmul(nm("d_cqn"), dq, w["w_uq"], tb=True)
    dckvn = matmul(nm("d_ckvn_k"), dkn, w["w_uk"], tb=True)
    dckvn = matmul(nm("d_ckvn_v"), dv, w["w_uv"], tb=True, epi=lambda acc, prev: (acc + prev,), extras=[(dckvn, 0)])
    (g["q_norm"],), (dcq,) = seq_bwd(nm("rms_q_bwd"), f_rms, [pr["q_norm"]], [mt["cq"]], [True], [], [dcqn], [bf16], tm)
    (g["kv_norm"],), (dckv,) = seq_bwd(nm("rms_kv_bwd"), f_rms, [pr["kv_norm"]], [mt["ckv"]], [True], [], [dckvn], [bf16], tm)
    g["pool"], (dpool,) = seq_bwd(nm("pool_bwd"), f_pool, pr["pool"], [mt["pool"]], [True], r["pool_saved"], [dys[1]],
                                  [bf16], ts["tp"])
    g["ssd"], (dz, dxbc, ddt) = seq_bwd(nm("ssd_bwd"), f_ssd, pr["ssd"], [mt["z"], mt["xbc"], mt["dt"]], [True] * 3,
                                        r["ssd_saved"], [dys[2]], [bf16] * 3, SSD_CHUNK)
    _, (dhh, dlg) = seq_bwd(nm("lru_post_bwd"), f_lru_post, [], [full(r["hh"]), mt["lg"]], [True, True], [], [dys[3]],
                            [f32, bf16], tm)
    da, du = scan_bwd(r["la"], r["hh"], dhh, ts["ts"])
    g["lru"], (dlx,) = seq_bwd(nm("lru_pre_bwd"), f_lru_pre, pr["lru"], [mt["lx"]], [True], r["lru_saved"], [da, du],
                               [bf16], ts["tl"])
    dk = _head_unpad_cols(g["w_uk"], QK_NOPE)
    dv_ = _head_unpad_cols(g["w_uv"], V_HEAD)
    wb0 = g["wb"][0].reshape(N_HEADS, LANE, D_MODEL)[:, :V_HEAD].reshape(MIX, D_MODEL)
    ssd, lru, pool = g["ssd"], g["lru"], g["pool"]
    tok = ctx.grads(l, "B", dict(
        w_uq=_head_unpad_cols(g["w_uq"], QK_NOPE + QK_ROPE).reshape(Q_LORA, -1),
        w_ukv=jnp.concatenate([dk, dv_], axis=2).reshape(KV_LORA, -1), ssd_conv_w=ssd[0], lru_conv_w=lru[0],
        w_branch=jnp.stack([wb0, g["wb"][1], g["wb"][2], g["wb"][3]]), w_out=g["w_out"]))
    du_p = jnp.concatenate(dgates + [dpool, dz, dlg, dlx, dxbc, dcq, dkrope, dckv, ddt,
                                     jnp.zeros((rows, U_COLS - U_DT[0] - LANE), bf16)], axis=1)
    small = dict(
        q_norm=g["q_norm"][0], kv_norm=g["kv_norm"][0],
        w_pool=pool[0].reshape(4, LANE, LANE), pool_scale=pool[1][0],
        ssd_conv_b=ssd[1][0], ssd_dt_bias=ssd[2][0, :8], ssd_a_log=ssd[3][0, :8], ssd_d=ssd[4][0, :8], ssd_norm=ssd[5][0],
        lru_conv_b=lru[1][0], lru_w_a=_block_diag_inv(lru[2]), lru_b_a=lru[3][0], lru_w_i=_block_diag_inv(lru[4]),
        lru_b_i=lru[5][0], lru_lambda=lru[6][0], g_mlp=g["g_mlp"][0], g_ple=g["g_ple"][0])
    tok_small = ctx.small(f"l{l}", [(n, l, small[n]) for n in SMALL if n in small] + extra_small)
    g_w_in = matmul(nm("d_w_in"), r["h"], du_p, ta=True, outs=(bf16,), deps=[tok, tok_small])
    tok = ctx.grads(l, "A", dict(w_in=_w_in_blocks(g_w_in)))
    dh = matmul(nm("d_h"), du_p, w["w_in"], tb=True, deps=[tok])
    (g_mix,), (dx,) = seq_bwd(nm("rms_in_bwd"), f_rms, [pr["g_mix"]], [full(r["x"])], [True], [], [dh], [f32], tm,
                              add_to=(0, dx1))
    return dx, tok, ("g_mix", l, g_mix[0])


def _rope_tables(positions):
    inv = 1.0 / (ROPE_THETA ** (jnp.arange(0, QK_ROPE, 2, dtype=f32) / QK_ROPE))
    ang = positions.astype(f32)[:, None] * inv
    cos, sin = jnp.cos(ang), jnp.sin(ang)
    rows = positions.shape[0]
    pad = jnp.zeros((rows, LANE - KR_LANE - QK_ROPE), f32)
    cosf = jnp.concatenate([jnp.ones((rows, KR_LANE), f32), cos, cos, pad], axis=1)
    sinf = jnp.concatenate([jnp.zeros((rows, KR_LANE), f32), -sin, sin, pad], axis=1)
    return cosf, sinf


WEIGHTS = ['g_mix', 'w_in', 'q_norm', 'w_uq', 'kv_norm', 'w_ukv', 'w_pool', 'pool_scale', 'ssd_conv_w', 'ssd_conv_b',
           'ssd_dt_bias', 'ssd_a_log', 'ssd_d', 'ssd_norm', 'lru_conv_w', 'lru_conv_b', 'lru_w_a', 'lru_b_a', 'lru_w_i',
           'lru_b_i', 'lru_lambda', 'w_branch', 'w_out', 'g_mlp', 'w_ff1', 'w_ff2', 'g_ple', 'w_ple_gate', 'w_ple', 'g_final']
SHARDED = dict(w_in=2, w_uq=2, w_ukv=2, ssd_conv_w=2, lru_conv_w=2, w_branch=3, w_out=1, w_ff1=2, w_ff2=1,
               w_ple_gate=1, w_ple=2)
F32_PAYLOAD = ("ssd_conv_w", "lru_conv_w")
DEPTH = 2


SMALL = [n for n in WEIGHTS if n not in SHARDED and n != "g_final"]


def local_step(x, p, positions, tgt, sp, ctx):
    cosf, sinf = _rope_tables(positions)
    res = []
    for l in range(DEPTH):
        x, r = _layer_fwd(x, p[l].astype(bf16), ctx, l, _layer_params(sp, l), cosf, sinf)
        res.append(r)
    loss8, dx, dgf = loss_head(x, tgt, sp["g_final"][None, :], _pick(x.shape[0], 512))
    tok = None
    pending = ("g_final", None, dgf[0])
    for l in reversed(range(DEPTH)):
        dx, tok, pending = _layer_bwd(dx, res[l], ctx, l, cosf, sinf, tok, [pending])
    ctx.small("last", [pending])
    return loss8[0, 0], dx


def _payload(name, w):
    return w if name in F32_PAYLOAD else w.astype(bf16)


def _blocks(name, g):
    ax = SHARDED[name] - 1
    shape = list(g.shape)
    shape[ax:ax + 1] = [N_DEV, shape[ax] // N_DEV]
    return _payload(name, jnp.moveaxis(g.reshape(shape), ax, 0))


def _assemble(name, shards):
    ax = SHARDED[name] - 1
    shape = list(shards.shape[1:])
    shape[ax] *= N_DEV
    return jnp.moveaxis(shards, 0, ax).reshape(shape)


class _Exchanges:
    def __init__(self, wts):
        self.wts = wts
        self.ag, self.rs, self.sm = {}, {}, {}
        tok = None
        for l in range(DEPTH):
            for grp, names in GROUPS.items():
                h = exchange_start(f"ag_start_{grp}{l}", [_payload(n, wts[n][l]) for n in names], True,
                                   deps=[] if tok is None else [tok])
                tok = h["token"]
                self.ag[(l, grp)] = h
        self.all_started = tok

    def weights(self, l, grp, after):
        first = (l, grp) == (0, "A")
        got = exchange_wait(f"ag_wait_{grp}{l}", self.ag[(l, grp)], [after, self.all_started] if first else [after])
        out = {}
        for n, a in zip(GROUPS[grp], got):
            out[n] = a if n == "w_in" else _assemble(n, a)
        return out

    def grads(self, l, grp, g):
        cut = lambda n: g[n].ndim == self.wts[n].ndim
        h = exchange_start(f"rs_start_{grp}{l}", [g[n] if cut(n) else _blocks(n, g[n]) for n in GROUPS[grp]], False)
        self.rs[(l, grp)] = h
        return h["token"]

    def small(self, tag, entries):
        flat = jnp.concatenate([a.reshape(-1) for _, _, a in entries])
        flat = jnp.pad(flat, (0, (-flat.shape[0]) % (8 * LANE))).reshape(-1, LANE)
        h = exchange_start(f"small_start_{tag}", [flat], True)
        self.sm[tag] = (h, [(n, l, a.shape) for n, l, a in entries])
        return h["token"]

    def collect(self, groups, after):
        parts = {}
        for grp in groups:
            for l in reversed(range(DEPTH)):
                got = exchange_wait(f"rs_wait_{grp}{l}", self.rs[(l, grp)], [after])
                for n, a in zip(GROUPS[grp], got):
                    parts.setdefault(n, [None] * DEPTH)[l] = a
        return parts

    def collect_small(self, after):
        parts = {}
        for tag, (h, layout) in self.sm.items():
            (got,) = exchange_wait(f"small_wait_{tag}", h, [after])
            got = got.reshape(N_DEV, -1)
            off = 0
            for n, l, shape in layout:
                size = 1
                for d in shape:
                    size *= d
                part = got[:, off:off + size].reshape((N_DEV,) + tuple(shape))
                off += size
                if l is None:
                    parts[n] = [part]
                else:
                    parts.setdefault(n, [None] * DEPTH)[l] = part
        return parts


def kernel(x, p, positions, g_mix, w_in, q_norm, w_uq, kv_norm, w_ukv, w_pool, pool_scale, ssd_conv_w, ssd_conv_b,
           ssd_dt_bias, ssd_a_log, ssd_d, ssd_norm, lru_conv_w, lru_conv_b, lru_w_a, lru_b_a, lru_w_i, lru_b_i,
           lru_lambda, w_branch, w_out, g_mlp, w_ff1, w_ff2, g_ple, w_ple_gate, w_ple, g_final, loss_target, m_g_mix,
           m_w_in, m_q_norm, m_w_uq, m_kv_norm, m_w_ukv, m_w_pool, m_pool_scale, m_ssd_conv_w, m_ssd_conv_b,
           m_ssd_dt_bias, m_ssd_a_log, m_ssd_d, m_ssd_norm, m_lru_conv_w, m_lru_conv_b, m_lru_w_a, m_lru_b_a,
           m_lru_w_i, m_lru_b_i, m_lru_lambda, m_w_branch, m_w_out, m_g_mlp, m_w_ff1, m_w_ff2, m_g_ple, m_w_ple_gate,
           m_w_ple, m_g_final, v_g_mix, v_w_in, v_q_norm, v_w_uq, v_kv_norm, v_w_ukv, v_w_pool, v_pool_scale,
           v_ssd_conv_w, v_ssd_conv_b, v_ssd_dt_bias, v_ssd_a_log, v_ssd_d, v_ssd_norm, v_lru_conv_w, v_lru_conv_b,
           v_lru_w_a, v_lru_b_a, v_lru_w_i, v_lru_b_i, v_lru_lambda, v_w_branch, v_w_out, v_g_mlp, v_w_ff1, v_w_ff2,
           v_g_ple, v_w_ple_gate, v_w_ple, v_g_final):
    given = dict(locals())
    wts = {n: given[n] for n in WEIGHTS}
    ctx = _Exchanges(wts)
    loss, grad_x = local_step(x[0], p[:, 0], positions[0], loss_target[0], wts, ctx)

    def update(parts):
        out = {}
        for n, eight in parts.items():
            w, m, v = wts[n], given["m_" + n], given["v_" + n]
            if n == "g_final":
                out[n] = [a[0] for a in adamw(f"adamw_{n}", eight, w[None], m[None], v[None])]
            else:
                out[n] = adamw(f"adamw_{n}", eight, w, m, v)
        return out

    outs = update(ctx.collect(("C", "B"), grad_x))
    late = outs["w_ff1"][1]
    outs.update(update(ctx.collect(("A",), late)))
    outs.update(update(ctx.collect_small(late)))
    loss = lax.psum(loss, AXES)
    return (loss, grad_x[None], *[outs[n][0] for n in WEIGHTS], *[outs[n][1] for n in WEIGHTS],
            *[outs[n][2] for n in WEIGHTS], *[outs[n][3] for n in WEIGHTS])
```

```python
import functools

import jax
import jax.numpy as jnp
from jax import lax
from jax.experimental import pallas as pl
from jax.experimental.pallas import tpu as pltpu

f32 = jnp.float32
bf16 = jnp.bfloat16

D_MODEL = 1024
MIX = 512
N_HEADS = 8
QK_NOPE, QK_ROPE, V_HEAD = 64, 32, 64
Q_LORA, KV_LORA = 384, 256
ROPE_THETA = 10000.0
POOL_WINDOWS = (2, 4, 8, 16)
SSD_CHUNK = 128
SSD_XBC = 768
CONV_W = 4
LRU_C = 8.0
D_FF = 4096
EPS = 1e-6
IN_COLS = 7592
ADAM_LR, ADAM_B1, ADAM_B2, ADAM_EPS, ADAM_WD, ADAM_STEP = 0.001, 0.9, 0.999, 1e-08, 0.01, 10

LANE = 128
HALO = 8
POOL_HALO = 16
VMEM_LIMIT = 56 * 1024 * 1024
MATMUL_MAX_K_TILE = 4096
MATMUL_ACC_PASS_WEIGHT = 0.3
MATMUL_VMEM_BUDGET = 32 * 1024 * 1024
N_DEV = 8
AXES = ("x", "y", "c")

U_COLS = 8192
U_GATES, U_POOL, U_Z, U_LG, U_LX, U_XBC, U_CQ, U_KR, U_CKV, U_DT = (
    (0, 4096), (4096, 512), (4608, 512), (5120, 512), (5632, 512), (6144, 768),
    (6912, 384), (7296, 128), (7424, 256), (7680, 128))
KR_LANE = 64
U_DTYPE = bf16


def _cp(sem):
    return pltpu.CompilerParams(dimension_semantics=sem, vmem_limit_bytes=VMEM_LIMIT)


def _pick(dim, pref):
    if dim <= pref:
        return dim
    t = pref
    while t >= LANE:
        if dim % t == 0:
            return t
        t -= LANE
    t = pref
    while dim % t:
        t -= 8
    return t


@functools.partial(jax.custom_vjp, nondiff_argnums=(1,))
def shift_down(x, k):
    row = lax.broadcasted_iota(jnp.int32, x.shape, 0)
    return jnp.where(row >= k, pltpu.roll(x, k, 0), 0.0)


def _shift_down_fwd(x, k):
    return shift_down(x, k), None


def _shift_down_bwd(k, _, g):
    r = g.shape[0]
    row = lax.broadcasted_iota(jnp.int32, g.shape, 0)
    return (jnp.where(row < r - k, pltpu.roll(g, r - k, 0), 0.0),)


shift_down.defvjp(_shift_down_fwd, _shift_down_bwd)


@functools.partial(jax.custom_vjp, nondiff_argnums=(1,))
def lane_roll(x, s):
    return pltpu.roll(x, s, 1)


def _lane_roll_fwd(x, s):
    return lane_roll(x, s), None


def _lane_roll_bwd(s, _, g):
    return (pltpu.roll(g, (g.shape[1] - s) % g.shape[1], 1),)


lane_roll.defvjp(_lane_roll_fwd, _lane_roll_bwd)


def _tile_spec(tm, width, cb, n=None):
    if n is None:
        return pl.BlockSpec((tm, width), lambda i: (i, cb))
    return pl.BlockSpec((tm, width), lambda i: (n - 1 - i, cb))


def _const_spec(shape):
    nd = len(shape)
    return pl.BlockSpec(shape, lambda i: (0,) * nd)


def seq_fwd(name, f, params, tiles, carries, outs, tm):
    rows = tiles[0][0].shape[0]
    n = rows // tm
    np_, nt, no, nc = len(params), len(tiles), len(outs), len(carries)

    def body(*refs):
        p_refs = refs[:np_]
        t_refs = refs[np_:np_ + nt]
        o_refs = refs[np_ + nt:np_ + nt + no]
        s_refs = refs[np_ + nt + no:np_ + nt + no + nc]
        c_refs = refs[np_ + nt + no + nc:]
        i = pl.program_id(0)

        @pl.when(i == 0)
        def _():
            for c in c_refs:
                c[...] = jnp.zeros_like(c)

        cvals = [c[...] for c in c_refs]
        for s, c in zip(s_refs, cvals):
            s[0] = c
        o, newc = f(i, [r[...] for r in p_refs], cvals, [r[...].astype(f32) for r in t_refs])
        for r, v in zip(o_refs, o):
            r[...] = v.astype(r.dtype)
        for r, v in zip(c_refs, newc):
            r[...] = v

    in_specs = [_const_spec(p.shape) for p in params] + [_tile_spec(tm, w, cb) for (_, w, cb) in tiles]
    out_specs = [_tile_spec(tm, w, 0) for (w, _) in outs]
    out_specs += [pl.BlockSpec((1,) + tuple(c), lambda i, nd=len(c): (i,) + (0,) * nd) for c in carries]
    out_shape = [jax.ShapeDtypeStruct((rows, w), dt) for (w, dt) in outs]
    out_shape += [jax.ShapeDtypeStruct((n,) + tuple(c), f32) for c in carries]
    res = pl.pallas_call(
        body, name=name, grid=(n,), in_specs=in_specs, out_specs=out_specs, out_shape=out_shape,
        scratch_shapes=[pltpu.VMEM(tuple(c), f32) for c in carries],
        compiler_params=_cp(("arbitrary",)),
    )(*params, *[t[0] for t in tiles])
    return list(res[:no]), list(res[no:])


def seq_bwd(name, f, params, tiles, diff, saved, douts, gdtypes, tm, add_to=None):
    rows = tiles[0][0].shape[0]
    n = rows // tm
    np_, nt, nc, nd = len(params), len(tiles), len(saved), len(douts)
    didx = [k for k, d in enumerate(diff) if d]
    ng = len(didx)
    has_add = add_to is not None

    def body(*refs):
        p_refs = refs[:np_]
        t_refs = refs[np_:np_ + nt]
        s_refs = refs[np_ + nt:np_ + nt + nc]
        d_refs = refs[np_ + nt + nc:np_ + nt + nc + nd]
        pos = np_ + nt + nc + nd
        a_ref = refs[pos] if has_add else None
        pos += 1 if has_add else 0
        dp_refs = refs[pos:pos + np_]
        dt_refs = refs[pos + np_:pos + np_ + ng]
        dc_refs = refs[pos + np_ + ng:]
        i = pl.program_id(0)
        step = n - 1 - i

        @pl.when(i == 0)
        def _():
            for r in dp_refs:
                r[...] = jnp.zeros_like(r)
            for r in dc_refs:
                r[...] = jnp.zeros_like(r)

        pvals = [r[...] for r in p_refs]
        cvals = [r[0] for r in s_refs]
        xvals = [r[...].astype(f32) for r in t_refs]

        def fn(p, c, xd):
            x = list(xvals)
            for k, v in zip(didx, xd):
                x[k] = v
            return f(step, p, c, x)

        _, vjp = jax.vjp(fn, pvals, cvals, [xvals[k] for k in didx])
        dp, dc, dx = vjp(([r[...].astype(f32) for r in d_refs], [r[...] for r in dc_refs]))
        for r, v in zip(dp_refs, dp):
            r[...] += v
        for r, v in zip(dc_refs, dc):
            r[...] = v
        for k, (r, v) in enumerate(zip(dt_refs, dx)):
            if has_add and k == add_to[0]:
                v = v + a_ref[...].astype(f32)
            r[...] = v.astype(r.dtype)

    in_specs = [_const_spec(p.shape) for p in params] + [_tile_spec(tm, w, cb, n) for (_, w, cb) in tiles]
    in_specs += [pl.BlockSpec((1,) + tuple(s.shape[1:]), lambda i, nd_=s.ndim - 1: (n - 1 - i,) + (0,) * nd_) for s in saved]
    in_specs += [_tile_spec(tm, d.shape[1], 0, n) for d in douts]
    args = list(params) + [t[0] for t in tiles] + list(saved) + list(douts)
    if has_add:
        in_specs.append(_tile_spec(tm, add_to[1].shape[1], 0, n))
        args.append(add_to[1])
    out_specs = [_const_spec(p.shape) for p in params] + [_tile_spec(tm, tiles[k][1], 0, n) for k in didx]
    out_shape = [jax.ShapeDtypeStruct(p.shape, f32) for p in params]
    out_shape += [jax.ShapeDtypeStruct((rows, tiles[k][1]), dt) for k, dt in zip(didx, gdtypes)]
    res = pl.pallas_call(
        body, name=name, grid=(n,), in_specs=in_specs, out_specs=out_specs, out_shape=out_shape,
        scratch_shapes=[pltpu.VMEM(tuple(s.shape[1:]), f32) for s in saved],
        compiler_params=_cp(("arbitrary",)),
    )(*args)
    return list(res[:np_]), list(res[np_:])


def _halvings(dim, lo, hi):
    t, out = _pick(dim, hi), []
    while t >= min(lo, dim) and dim % t == 0:
        out.append(t)
        if t % 2 or (t // 2) % 8:
            break
        t //= 2
    return out


def _matmul_tiles(m, n, k, a_item, b_item, per_out, max_tn=1024):
    def vmem_bytes(tm, tn, tk):
        acc = 4 if k // tk > 1 else 0
        return 2 * (tm * tk * a_item + tk * tn * b_item + tm * tn * per_out) + tm * tn * acc

    def traffic(tm, tn, tk):
        nk = k // tk
        return (m * k * a_item * (1 if nk == 1 else n // tn) + k * n * b_item * (m // tm)
                + (nk - 1) * m * n * 8 * MATMUL_ACC_PASS_WEIGHT)

    cands = [(traffic(tm, tn, tk), -tm * tn, tm, tn, tk)
             for tk in _halvings(k, 512, MATMUL_MAX_K_TILE) for tm in _halvings(m, 256, 4096)
             for tn in _halvings(n, 512, min(1024, max_tn))
             if vmem_bytes(tm, tn, tk) <= MATMUL_VMEM_BUDGET]
    return min(cands)[2:]


def matmul(name, a, b, *, ta=False, tb=False, outs=(f32,), epi=None, extras=(), deps=(), out_blocks=0):
    m, k = (a.shape[1], a.shape[0]) if ta else a.shape
    n = b.shape[0] if tb else b.shape[1]
    per_out = sum(jnp.dtype(dt).itemsize for dt in outs) + sum(e[0].dtype.itemsize for e in extras)
    tm, tn, tk = _matmul_tiles(m, n, k, a.dtype.itemsize, b.dtype.itemsize, per_out, n // out_blocks if out_blocks else n)
    nk = k // tk
    ne = len(extras)
    dims = (((0 if ta else 1,), (1 if tb else 0,)), ((), ()))

    def body(*refs):
        a_ref, b_ref = refs[0], refs[1]
        e_refs = refs[2:2 + ne]
        o_refs = refs[2 + ne + len(deps):2 + ne + len(deps) + len(outs)]
        kk = pl.program_id(2)
        part = lax.dot_general(a_ref[...].astype(bf16), b_ref[...].astype(bf16), dims, preferred_element_type=f32)

        def finish(total):
            res = (total,) if epi is None else epi(total, *[e[...] for e in e_refs])
            for r, v in zip(o_refs, res):
                r[...] = v.astype(r.dtype)

        if nk == 1:
            finish(part)
            return
        acc = refs[-1]

        @pl.when(kk == 0)
        def _():
            acc[...] = part

        @pl.when(jnp.logical_and(kk > 0, kk < nk - 1))
        def _():
            acc[...] += part

        @pl.when(kk == nk - 1)
        def _():
            finish(acc[...] + part)

    a_spec = pl.BlockSpec((tk, tm), lambda i, j, q: (q, i)) if ta else pl.BlockSpec((tm, tk), lambda i, j, q: (i, q))
    b_spec = pl.BlockSpec((tn, tk), lambda i, j, q: (j, q)) if tb else pl.BlockSpec((tk, tn), lambda i, j, q: (q, j))
    assert all(off % tn == 0 for (_, off) in extras)
    e_specs = [pl.BlockSpec((tm, tn), lambda i, j, q, off=off // tn: (i, off + j)) for (_, off) in extras]
    if out_blocks:
        per = n // out_blocks // tn
        out_spec = pl.BlockSpec((None, tm, tn), lambda i, j, q: (j // per, i, j % per))
        out_dims = (out_blocks, m, n // out_blocks)
    else:
        out_spec = pl.BlockSpec((tm, tn), lambda i, j, q: (i, j))
        out_dims = (m, n)
    res = pl.pallas_call(
        body, name=name, grid=(m // tm, n // tn, nk),
        in_specs=[a_spec, b_spec] + e_specs + [pl.BlockSpec(memory_space=pl.ANY) for _ in deps],
        out_specs=[out_spec for _ in outs],
        out_shape=[jax.ShapeDtypeStruct(out_dims, dt) for dt in outs],
        scratch_shapes=[pltpu.VMEM((tm, tn), f32)] if nk > 1 else [],
        compiler_params=_cp(("parallel", "parallel", "arbitrary")),
    )(a, b, *[e[0] for e in extras], *deps)
    return res[0] if len(outs) == 1 else tuple(res)


def merge_fwd(name, ys, wbs, u):
    rows, n_out = ys[0].shape[0], wbs[0].shape[1]
    nb = len(ys)
    tm, tn = _pick(rows, 512), _pick(n_out, 512)

    def body(*refs):
        y_refs, w_refs, g_refs = refs[:nb], refs[nb:2 * nb], refs[2 * nb:3 * nb]
        m_ref, p_refs = refs[3 * nb], refs[3 * nb + 1:]
        total = None
        for y_ref, w_ref, g_ref, p_ref in zip(y_refs, w_refs, g_refs, p_refs):
            pre = jnp.dot(y_ref[...], w_ref[...], preferred_element_type=f32)
            p_ref[...] = pre.astype(p_ref.dtype)
            term = jax.nn.sigmoid(g_ref[...].astype(f32)) * pre
            total = term if total is None else total + term
        m_ref[...] = total.astype(m_ref.dtype)

    in_specs = [pl.BlockSpec((tm, y.shape[1]), lambda i, j: (i, 0)) for y in ys]
    in_specs += [pl.BlockSpec((w.shape[0], tn), lambda i, j: (0, j)) for w in wbs]
    in_specs += [pl.BlockSpec((tm, tn), lambda i, j, off=n * (n_out // tn): (i, off + j)) for n in range(nb)]
    out_spec = pl.BlockSpec((tm, tn), lambda i, j: (i, j))
    res = pl.pallas_call(
        body, name=name, grid=(rows // tm, n_out // tn), in_specs=in_specs, out_specs=[out_spec] * (nb + 1),
        out_shape=[jax.ShapeDtypeStruct((rows, n_out), bf16)] * (nb + 1),
        compiler_params=_cp(("parallel", "parallel")),
    )(*ys, *wbs, *([u] * nb))
    return res[0], list(res[1:])


ATT_SCALE = (QK_NOPE + QK_ROPE) ** -0.5
LN2 = 0.6931471805599453
ATT_C = ATT_SCALE / LN2
NT = (((1,), (1,)), ((), ()))
TN = (((0,), (0,)), ((), ()))


def _causal(tq, tk):
    return lax.broadcasted_iota(jnp.int32, (tq, tk), 0) >= lax.broadcasted_iota(jnp.int32, (tq, tk), 1)


def _tri_pairs(n, by_column):
    if by_column:
        pairs = [(i, j) for j in range(n) for i in range(j, n)]
    else:
        pairs = [(i, j) for i in range(n) for j in range(i + 1)]
    return (jnp.asarray([a for a, _ in pairs], jnp.int32), jnp.asarray([b for _, b in pairs], jnp.int32))


HEADS_PER_STEP = 4
HEAD_PAIR = HEADS_PER_STEP * LANE


def attn_fwd(q, k, v, t):
    rows = q.shape[0]
    n = rows // t
    it, jt = _tri_pairs(n, False)

    def body(it_ref, jt_ref, q_ref, k_ref, v_ref, o_ref, lse_ref, m_s, l_s, acc_s):
        s_id = pl.program_id(1)
        i, j = it_ref[s_id], jt_ref[s_id]

        @pl.when(j == 0)
        def _():
            m_s[...] = jnp.full_like(m_s, -jnp.inf)
            l_s[...] = jnp.zeros_like(l_s)
            acc_s[...] = jnp.zeros_like(acc_s)

        def step(diag):
            for hh in range(HEADS_PER_STEP):
                sl = slice(LANE * hh, LANE * (hh + 1))
                s = lax.dot_general(q_ref[:, sl], k_ref[:, sl], NT, preferred_element_type=f32)
                if diag:
                    s = jnp.where(_causal(t, t), s, -jnp.inf)
                m_prev = m_s[:, sl]
                m_new = jnp.maximum(m_prev, jnp.max(s, axis=1, keepdims=True))
                alpha = jnp.exp2(m_prev - m_new)
                p = jnp.exp2(s - m_new[:, :1])
                l_s[:, sl] = alpha * l_s[:, sl] + jnp.sum(p, axis=1, keepdims=True)
                acc_s[:, sl] = alpha * acc_s[:, sl] + jnp.dot(p.astype(bf16), v_ref[:, sl], preferred_element_type=f32)
                m_s[:, sl] = m_new

        pl.when(j < i)(lambda: step(False))

        @pl.when(j == i)
        def _():
            step(True)
            o_ref[...] = (acc_s[...] / l_s[...]).astype(o_ref.dtype)
            lse_ref[...] = m_s[...] + jnp.log2(l_s[...])

    qs = pl.BlockSpec((t, HEAD_PAIR), lambda h, s, it_, jt_: (it_[s], h))
    ks = pl.BlockSpec((t, HEAD_PAIR), lambda h, s, it_, jt_: (jt_[s], h))
    hw = N_HEADS * LANE
    return pl.pallas_call(
        body, name="attn_fwd",
        grid_spec=pltpu.PrefetchScalarGridSpec(
            num_scalar_prefetch=2, grid=(hw // HEAD_PAIR, it.shape[0]), in_specs=[qs, ks, ks], out_specs=[qs, qs],
            scratch_shapes=[pltpu.VMEM((t, HEAD_PAIR), f32)] * 3),
        out_shape=[jax.ShapeDtypeStruct((rows, hw), bf16), jax.ShapeDtypeStruct((rows, hw), f32)],
        compiler_params=_cp(("parallel", "arbitrary")),
    )(it, jt, q, k, v)


def attn_bwd(q, k, v, do, o, lse, t):
    rows = q.shape[0]
    n = rows // t
    it, jt = _tri_pairs(n, True)

    def body(it_ref, jt_ref, q_ref, k_ref, v_ref, do_ref, o_ref, lse_ref, dq_ref, dk_ref, dv_ref, dk_s, dv_s):
        s_id = pl.program_id(1)
        i, j = it_ref[s_id], jt_ref[s_id]

        @pl.when(s_id == 0)
        def _():
            dq_ref[...] = jnp.zeros_like(dq_ref)

        @pl.when(i == j)
        def _():
            dk_s[...] = jnp.zeros_like(dk_s)
            dv_s[...] = jnp.zeros_like(dv_s)

        q_rows = pl.ds(pl.multiple_of(i * t, t), t)

        def step(diag):
            for hh in range(HEADS_PER_STEP):
                sl = slice(LANE * hh, LANE * (hh + 1))
                qh, kh, vh, doh = q_ref[:, sl], k_ref[:, sl], v_ref[:, sl], do_ref[:, sl]
                s = lax.dot_general(qh, kh, NT, preferred_element_type=f32)
                p = jnp.exp2(s - lse_ref[:, sl][:, :1])
                if diag:
                    p = jnp.where(_causal(t, t), p, 0.0)
                dp = lax.dot_general(doh, vh, NT, preferred_element_type=f32)
                delta = jnp.sum(doh.astype(f32) * o_ref[:, sl].astype(f32), axis=1, keepdims=True)
                ds = (p * (dp - delta) * LN2).astype(bf16)
                dv_s[:, sl] += lax.dot_general(p.astype(bf16), doh, TN, preferred_element_type=f32)
                dk_s[:, sl] += lax.dot_general(ds, qh, TN, preferred_element_type=f32)
                dq_ref[q_rows, sl] += jnp.dot(ds, kh, preferred_element_type=f32)

        pl.when(i > j)(lambda: step(False))
        pl.when(i == j)(lambda: step(True))

        @pl.when(i == n - 1)
        def _():
            dk_ref[...] = dk_s[...]
            dv_ref[...] = dv_s[...]

    qs = pl.BlockSpec((t, HEAD_PAIR), lambda h, s, it_, jt_: (it_[s], h))
    ks = pl.BlockSpec((t, HEAD_PAIR), lambda h, s, it_, jt_: (jt_[s], h))
    dqs = pl.BlockSpec((rows, HEAD_PAIR), lambda h, s, it_, jt_: (0, h))
    hw = N_HEADS * LANE
    return pl.pallas_call(
        body, name="attn_bwd",
        grid_spec=pltpu.PrefetchScalarGridSpec(
            num_scalar_prefetch=2, grid=(hw // HEAD_PAIR, it.shape[0]), in_specs=[qs, ks, ks, qs, qs, qs],
            out_specs=[dqs, ks, ks], scratch_shapes=[pltpu.VMEM((t, HEAD_PAIR), f32)] * 2),
        out_shape=[jax.ShapeDtypeStruct((rows, hw), f32)] * 3,
        compiler_params=_cp(("parallel", "arbitrary")),
    )(it, jt, q, k, v, do, o, lse)


def _steps(tm):
    k, out = 1, []
    while k < tm:
        out.append(k)
        k *= 2
    return out


def scan_fwd(a, u, tm):
    rows, ch = a.shape
    n = rows // tm

    def body(a_ref, u_ref, h_ref, h_s):
        @pl.when(pl.program_id(0) == 0)
        def _():
            h_s[...] = jnp.zeros_like(h_s)

        av, bv = a_ref[...], u_ref[...]
        row = lax.broadcasted_iota(jnp.int32, av.shape, 0)
        for k in _steps(tm):
            a_sh = jnp.where(row >= k, pltpu.roll(av, k, 0), 1.0)
            b_sh = jnp.where(row >= k, pltpu.roll(bv, k, 0), 0.0)
            bv = av * b_sh + bv
            av = av * a_sh
        h = bv + av * h_s[HALO - 1:HALO, :]
        h_ref[...] = h
        h_s[...] = h[tm - HALO:, :]

    spec = pl.BlockSpec((tm, ch), lambda i: (i, 0))
    return pl.pallas_call(
        body, name="lru_scan_fwd", grid=(n,), in_specs=[spec, spec], out_specs=spec,
        out_shape=jax.ShapeDtypeStruct((rows, ch), f32), scratch_shapes=[pltpu.VMEM((HALO, ch), f32)],
        compiler_params=_cp(("arbitrary",)),
    )(a, u)


def scan_bwd(a, h, dh, tm):
    rows, ch = a.shape
    n = rows // tm
    per = tm // HALO

    def body(a_ref, h_ref, hp_ref, dh_ref, da_ref, du_ref, g_s, a_s):
        i = pl.program_id(0)
        step = n - 1 - i

        @pl.when(i == 0)
        def _():
            g_s[...] = jnp.zeros_like(g_s)
            a_s[...] = jnp.zeros_like(a_s)

        a0 = a_ref[...]
        row = lax.broadcasted_iota(jnp.int32, a0.shape, 0)
        av = jnp.where(row < tm - 1, pltpu.roll(a0, tm - 1, 0), a_s[0:1, :])
        bv = dh_ref[...]
        for k in _steps(tm):
            a_sh = jnp.where(row < tm - k, pltpu.roll(av, tm - k, 0), 1.0)
            b_sh = jnp.where(row < tm - k, pltpu.roll(bv, tm - k, 0), 0.0)
            bv = bv + av * b_sh
            av = av * a_sh
        g = bv + av * g_s[0:1, :]
        h_last = jnp.where(step > 0, hp_ref[HALO - 1:HALO, :], 0.0)
        h_prev = jnp.where(row >= 1, pltpu.roll(h_ref[...], 1, 0), h_last)
        du_ref[...] = g
        da_ref[...] = g * h_prev
        g_s[...] = g[0:HALO, :]
        a_s[...] = a0[0:HALO, :]

    spec = pl.BlockSpec((tm, ch), lambda i: (n - 1 - i, 0))
    hp_spec = pl.BlockSpec((HALO, ch), lambda i: (jnp.maximum((n - 1 - i) * per - 1, 0), 0))
    return pl.pallas_call(
        body, name="lru_scan_bwd", grid=(n,), in_specs=[spec, spec, hp_spec, spec], out_specs=[spec, spec],
        out_shape=[jax.ShapeDtypeStruct((rows, ch), f32)] * 2,
        scratch_shapes=[pltpu.VMEM((HALO, ch), f32)] * 2,
        compiler_params=_cp(("arbitrary",)),
    )(a, h, h, dh)


def _rms(x, g):
    return x * lax.rsqrt(jnp.mean(x * x, axis=-1, keepdims=True) + EPS) * g


def f_rms(step, p, c, x):
    return [_rms(x[0], p[0])], []


def _rope(x, cosf, sinf):
    lane = lax.broadcasted_iota(jnp.int32, x.shape, 1)
    sw = jnp.where(lane < KR_LANE + QK_ROPE // 2, lane_roll(x, LANE - QK_ROPE // 2), lane_roll(x, QK_ROPE // 2))
    return x * cosf + sw * sinf


def f_prep(step, p, c, x):
    q, kn, kr, cosf, sinf = x
    kr_rot = _rope(kr, cosf, sinf)
    qr = [_rope(q[:, LANE * h:LANE * (h + 1)], cosf, sinf) * ATT_C for h in range(N_HEADS)]
    kk = [kn[:, LANE * h:LANE * (h + 1)] + kr_rot for h in range(N_HEADS)]
    return [jnp.concatenate(qr, axis=1), jnp.concatenate(kk, axis=1)], []


def _conv(tail, x, w, b):
    xf = jnp.concatenate([tail, x], axis=0)
    acc = b + w[CONV_W - 1:CONV_W, :] * xf
    for k in range(CONV_W - 1):
        acc = acc + w[k:k + 1, :] * shift_down(xf, CONV_W - 1 - k)
    return acc[HALO:, :]


def f_pool(step, p, c, x):
    wp, sc = p
    (tail,) = c
    (u,) = x
    tm = u.shape[0]
    xf = jnp.concatenate([tail, u], axis=0)
    sums, s, w = [], xf, 1
    while w < POOL_WINDOWS[-1]:
        s = s + shift_down(s, w)
        w *= 2
        sums.append(s)
    t = step * tm + lax.broadcasted_iota(jnp.int32, (tm, 1), 0)
    ys = []
    for g, (w, s) in enumerate(zip(POOL_WINDOWS, sums)):
        sl = slice(LANE * g, LANE * (g + 1))
        cnt = jnp.minimum(t + 1, w).astype(f32)
        d = s[POOL_HALO:, sl] / cnt - u[:, sl]
        ys.append(jnp.dot(d.astype(bf16), wp[LANE * g:LANE * (g + 1), :].astype(bf16), preferred_element_type=f32))
    return [jnp.concatenate(ys, axis=1) * sc], [u[tm - POOL_HALO:, :]]


def f_ssd(step, p, c, x):
    conv_w, conv_b, dtb, alog, dsk, ng = p
    tail, s_in = c[0], c[1:]
    z, xbc, dt = x
    ln = z.shape[0]
    xc = jax.nn.silu(_conv(tail, xbc, conv_w, conv_b))
    xs, bb, cc = xc[:, :MIX], xc[:, MIX:MIX + LANE], xc[:, MIX + LANE:]
    dtv = jax.nn.softplus(dt + dtb[0:1, :])
    a = dtv * -jnp.exp(alog[0:1, :])
    ri = lax.broadcasted_iota(jnp.int32, (ln, ln), 0)
    ci = lax.broadcasted_iota(jnp.int32, (ln, ln), 1)
    tril = (ri >= ci).astype(f32)
    triu = (ri <= ci).astype(f32)
    hi = lax.Precision.HIGHEST
    a_cs = jnp.dot(tril, a, precision=hi, preferred_element_type=f32)
    a_cs_t = lax.dot_general(a, triu, TN, precision=hi, preferred_element_type=f32)
    a_tot = jnp.sum(a, axis=0, keepdims=True)
    lane = lax.broadcasted_iota(jnp.int32, (1, LANE), 1)
    half = [(lane < 64).astype(f32), (lane >= 64).astype(f32)]
    hrow = lax.broadcasted_iota(jnp.int32, (LANE, 1), 0)

    def head(v, h):
        return jnp.sum(v * (lane == h).astype(f32), axis=1, keepdims=True)

    def pair(v, j):
        return head(v, 2 * j) * half[0] + head(v, 2 * j + 1) * half[1]

    cg = [(cc * half[g]).astype(bf16) for g in range(2)]
    bg = [(bb * half[g]).astype(bf16) for g in range(2)]
    cb = [lax.dot_general(cg[g], bg[g], NT, preferred_element_type=f32) for g in range(2)]
    ys, s_out = [], []
    for j in range(4):
        g = j // 2
        xs_j = xs[:, LANE * j:LANE * (j + 1)]
        xj = xs_j * pair(dtv, j)
        yj = xs_j * pair(dsk[0:1, :], j)
        for hh in range(2):
            h = 2 * j + hh
            rowv = jnp.sum(a_cs_t * (hrow == h).astype(f32), axis=0, keepdims=True)
            lmat = jnp.exp(jnp.where(ri >= ci, head(a_cs, h) - rowv, -jnp.inf))
            yj = yj + jnp.dot((cb[g] * lmat).astype(bf16), (xj * half[hh]).astype(bf16), preferred_element_type=f32)
        acs = pair(a_cs, j)
        tot = pair(a_tot, j)
        yj = yj + jnp.exp(acs) * jnp.dot(cg[g], s_in[j].astype(bf16), preferred_element_type=f32)
        s_new = jnp.exp(tot) * s_in[j] + lax.dot_general(bg[g], (xj * jnp.exp(tot - acs)).astype(bf16), TN,
                                                         preferred_element_type=f32)
        ys.append(yj)
        s_out.append(s_new)
    y = jnp.concatenate(ys, axis=1) * jax.nn.silu(z)
    return [_rms(y, ng)], [xbc[ln - HALO:, :]] + s_out


def _neg_expm1(y):
    series = -y * (1.0 + y * (0.5 + y * (1.0 / 6 + y * (1.0 / 24 + y * (1.0 / 120)))))
    return jnp.where(y > -0.05, series, 1.0 - jnp.exp(y))


def f_lru_pre(step, p, c, x):
    cw, cb_, wa, ba, wi, bi, lam = p
    (tail,) = c
    (lx,) = x
    tm = lx.shape[0]
    xc = _conv(tail, lx, cw, cb_)
    xb = xc.astype(bf16)
    r = jax.nn.sigmoid(jnp.dot(xb, wa.astype(bf16), preferred_element_type=f32) + ba)
    it = jax.nn.sigmoid(jnp.dot(xb, wi.astype(bf16), preferred_element_type=f32) + bi)
    log_a = -LRU_C * r * jax.nn.softplus(-lam)
    mult = jnp.sqrt(_neg_expm1(2.0 * log_a))
    return [jnp.exp(log_a), xc * it * mult], [lx[tm - HALO:, :]]


def f_lru_post(step, p, c, x):
    h, g = x
    return [h * jax.nn.gelu(g)], []


def loss_head(x, tgt, g, tm):
    rows, d = x.shape
    n = rows // tm

    def body(x_ref, t_ref, g_ref, loss_ref, dx_ref, dg_ref):
        @pl.when(pl.program_id(0) == 0)
        def _():
            loss_ref[...] = jnp.zeros_like(loss_ref)
            dg_ref[...] = jnp.zeros_like(dg_ref)

        def fn(gv, xv):
            err = _rms(xv, gv) - t_ref[...]
            return 0.5 * jnp.sum(jnp.mean(err * err, axis=-1, keepdims=True))

        val, (dg, dx) = jax.value_and_grad(fn, argnums=(0, 1))(g_ref[...], x_ref[...])
        loss_ref[...] += val
        dg_ref[...] += dg
        dx_ref[...] = dx

    spec = pl.BlockSpec((tm, d), lambda i: (i, 0))
    return pl.pallas_call(
        body, name="loss_head", grid=(n,), in_specs=[spec, spec, _const_spec((1, d))],
        out_specs=[_const_spec((8, LANE)), spec, _const_spec((1, d))],
        out_shape=[jax.ShapeDtypeStruct((8, LANE), f32), jax.ShapeDtypeStruct((rows, d), f32),
                   jax.ShapeDtypeStruct((1, d), f32)],
        compiler_params=_cp(("arbitrary",)),
    )(x, tgt, g)


def ew(name, fn, ins, outs, tm):
    rows = ins[0][0].shape[0]
    ni = len(ins)

    def body(*refs):
        res = fn(*[r[...].astype(f32) for r in refs[:ni]])
        for r, v in zip(refs[ni:], res):
            r[...] = v.astype(r.dtype)

    return pl.pallas_call(
        body, name=name, grid=(rows // tm,), in_specs=[_tile_spec(tm, w, cb) for (_, w, cb) in ins],
        out_specs=[_tile_spec(tm, w, 0) for (w, _) in outs],
        out_shape=[jax.ShapeDtypeStruct((rows, w), dt) for (w, dt) in outs],
        compiler_params=_cp(("parallel",)),
    )(*[t[0] for t in ins])


def _peers():
    x, y, c = lax.axis_index("x"), lax.axis_index("y"), lax.axis_index("c")
    me = 4 * x + 2 * y + c
    out = []
    for k in range(1, N_DEV):
        px = 1 - x if k & 4 else x
        py = 1 - y if k & 2 else y
        pc = 1 - c if k & 1 else c
        out.append(((px, py, pc), 4 * px + 2 * py + pc))
    return me, out


_HBM = pl.BlockSpec(memory_space=pltpu.HBM)
_SEM = pl.BlockSpec(memory_space=pltpu.SEMAPHORE)
_EFFECT = pltpu.SideEffectType.DATAFLOW_SIDE_EFFECTING


def _remote(src_ref, land_ref, gather, me, pid, dev, send_sems, recv_sems, k, recv_side):
    return pltpu.make_async_remote_copy(
        src_ref=src_ref if gather else src_ref.at[pid], dst_ref=land_ref.at[pid if recv_side else me],
        send_sem=send_sems.at[k], recv_sem=recv_sems.at[k], device_id=dev, device_id_type=pl.DeviceIdType.MESH)


def _own(src_ref, land_ref, gather, me, sem):
    return pltpu.make_async_copy(src_ref if gather else src_ref.at[me], land_ref.at[me], sem)


def exchange_start(name, srcs, gather, deps=()):
    n, nd = len(srcs), len(deps)
    shapes = [(s.shape if gather else s.shape[1:]) for s in srcs]
    lands = [lax.empty((N_DEV,) + tuple(sh), s.dtype) for s, sh in zip(srcs, shapes)]

    def body(*refs):
        src_refs, land_refs = refs[:n], refs[n:2 * n]
        send_sems, recv_sems, own_sem = refs[2 * n + nd:2 * n + nd + 3]
        token = refs[-1]
        me, peers = _peers()
        for k, (dev, pid) in enumerate(peers):
            for s_ref, l_ref in zip(src_refs, land_refs):
                _remote(s_ref, l_ref, gather, me, pid, dev, send_sems, recv_sems, k, False).start()
        for s_ref, l_ref in zip(src_refs, land_refs):
            _own(s_ref, l_ref, gather, me, own_sem).start()
        token[...] = jnp.zeros_like(token)

    hbm = lambda a: pltpu.with_memory_space_constraint(a, pltpu.HBM)
    res = pl.pallas_call(
        body, name=name,
        out_shape=(pltpu.SemaphoreType.DMA((N_DEV - 1,)), pltpu.SemaphoreType.DMA((N_DEV - 1,)), pltpu.SemaphoreType.DMA(()),
                   *[pltpu.HBM(a.shape, a.dtype) for a in list(srcs) + lands], jax.ShapeDtypeStruct((8, LANE), f32)),
        in_specs=[_HBM] * (2 * n) + [pl.BlockSpec(memory_space=pl.ANY)] * nd,
        out_specs=(_SEM, _SEM, _SEM, *([_HBM] * (2 * n)), pl.BlockSpec(memory_space=pltpu.VMEM)),
        input_output_aliases={i: 3 + i for i in range(2 * n)},
        compiler_params=pltpu.CompilerParams(has_side_effects=_EFFECT),
    )(*[hbm(a) for a in list(srcs) + lands], *deps)
    return dict(sems=res[:3], srcs=list(res[3:3 + n]), lands=list(res[3 + n:3 + 2 * n]), token=res[-1], gather=gather)


def exchange_wait(name, h, afters):
    n, gather = len(h["srcs"]), h["gather"]

    def body(*refs):
        src_refs, land_refs = refs[:n], refs[n:2 * n]
        send_sems, recv_sems, own_sem = refs[2 * n:2 * n + 3]
        me, peers = _peers()
        for k, (dev, pid) in enumerate(peers):
            for s_ref, l_ref in zip(src_refs, land_refs):
                _remote(s_ref, l_ref, gather, me, pid, dev, send_sems, recv_sems, k, True).wait_recv()
        for k, (dev, pid) in enumerate(peers):
            for s_ref, l_ref in zip(src_refs, land_refs):
                _remote(s_ref, l_ref, gather, me, pid, dev, send_sems, recv_sems, k, False).wait_send()
        for s_ref, l_ref in zip(src_refs, land_refs):
            _own(s_ref, l_ref, gather, me, own_sem).wait()

    arrs = h["srcs"] + h["lands"]
    res = pl.pallas_call(
        body, name=name, out_shape=tuple(pltpu.HBM(a.shape, a.dtype) for a in arrs),
        in_specs=[_HBM] * (2 * n) + [_SEM, _SEM, _SEM] + [pl.BlockSpec(memory_space=pl.ANY)] * len(afters),
        out_specs=tuple([_HBM] * (2 * n)), input_output_aliases={i: i for i in range(2 * n)},
        compiler_params=pltpu.CompilerParams(has_side_effects=_EFFECT),
    )(*arrs, *h["sems"], *afters)
    return list(res[n:])


def adamw(name, parts, w, m, v):
    nl = len(parts)
    shape = w.shape[1:]
    c = shape[-1]
    r = 1
    for s in shape[:-1]:
        r *= s
    tr = _pick(r, 256) if r % 8 == 0 else r
    nb = r // tr
    parts2 = [p.reshape(N_DEV, r, c) for p in parts]
    w2, m2, v2 = (a.reshape(nl, r, c) for a in (w, m, v))

    def body(*refs):
        p_refs = refs[:nl]
        w_ref, m_ref, v_ref, g_ref, d_ref, nm_ref, nv_ref = refs[nl:]
        layer = pl.program_id(0)
        for ll, p_ref in enumerate(p_refs):
            @pl.when(layer == ll)
            def _(p_ref=p_ref):
                g = p_ref[0].astype(f32)
                for i in range(1, N_DEV):
                    g = g + p_ref[i].astype(f32)
                mn = ADAM_B1 * m_ref[0] + (1.0 - ADAM_B1) * g
                vn = ADAM_B2 * v_ref[0] + (1.0 - ADAM_B2) * jnp.square(g)
                m_hat = mn / (1.0 - ADAM_B1 ** ADAM_STEP)
                v_hat = vn / (1.0 - ADAM_B2 ** ADAM_STEP)
                g_ref[0] = g
                d_ref[0] = -ADAM_LR * (m_hat / (jnp.sqrt(v_hat) + ADAM_EPS) + ADAM_WD * w_ref[0])
                nm_ref[0] = mn
                nv_ref[0] = vn

    def p_spec(ll):
        return pl.BlockSpec((N_DEV, tr, c), lambda l, i: (0, jnp.where(l == ll, i, jnp.where(l > ll, nb - 1, 0)), 0))

    spec = pl.BlockSpec((1, tr, c), lambda l, i: (l, i, 0))
    res = pl.pallas_call(
        body, name=name, grid=(nl, nb), in_specs=[p_spec(ll) for ll in range(nl)] + [spec, spec, spec],
        out_specs=[spec] * 4, out_shape=[jax.ShapeDtypeStruct((nl, r, c), f32)] * 4,
        compiler_params=_cp(("arbitrary", "arbitrary")),
    )(*parts2, w2, m2, v2)
    return [a.reshape(w.shape) for a in res]


_IN_SPLITS = dict(cq=(0, 384), ckv=(384, 640), kr=(640, 672), pool=(672, 1184), z=(1184, 1696), xbc=(1696, 2464),
                  dt=(2464, 2472), lg=(2472, 2984), lx=(2984, 3496), gates=(3496, 7592))


W_IN_SHARD = IN_COLS // N_DEV

_PAD_ORDER = ("gates", "pool", "z", "lg", "lx", "xbc", "cq", KR_LANE, "kr", LANE - KR_LANE - QK_ROPE, "ckv", "dt",
              LANE - 8, U_COLS - U_DT[0] - LANE)
_SEGMENTS = ((0, U_CQ[0], 384), (384, U_CKV[0], 256), (640, U_KR[0] + KR_LANE, QK_ROPE), (672, U_POOL[0], 512),
             (1184, U_Z[0], 512), (1696, U_XBC[0], 768), (2464, U_DT[0], 8), (2472, U_LG[0], 512), (2984, U_LX[0], 512),
             (3496, 0, 4096))


def _pad_w_in(shards):
    rows = shards.shape[1]
    pieces = []
    for item in _PAD_ORDER:
        if isinstance(item, int):
            pieces.append(jnp.zeros((rows, item), shards.dtype))
            continue
        a, b = _IN_SPLITS[item]
        for d in range(a // W_IN_SHARD, (b - 1) // W_IN_SHARD + 1):
            lo, hi = max(a, d * W_IN_SHARD), min(b, (d + 1) * W_IN_SHARD)
            pieces.append(shards[d, :, lo - d * W_IN_SHARD:hi - d * W_IN_SHARD])
    return jnp.concatenate(pieces, axis=1)


def _w_in_blocks(g):
    blocks = []
    for d in range(N_DEV):
        a, b = d * W_IN_SHARD, (d + 1) * W_IN_SHARD
        pieces = []
        for ref, pad, width in _SEGMENTS:
            lo, hi = max(a, ref), min(b, ref + width)
            if lo < hi:
                pieces.append(g[:, pad + lo - ref:pad + hi - ref])
        blocks.append(jnp.concatenate(pieces, axis=1))
    return jnp.stack(blocks).astype(bf16)


def _head_pad_cols(w, per, lo, hi):
    k = w.shape[0]
    w = w.reshape(k, N_HEADS, per)[:, :, lo:hi]
    return jnp.pad(w, ((0, 0), (0, 0), (0, LANE - (hi - lo)))).reshape(k, N_HEADS * LANE)


def _head_unpad_cols(g, n):
    k = g.shape[0]
    return g.reshape(k, N_HEADS, LANE)[:, :, :n]


def _on_diagonal():
    i = lax.broadcasted_iota(jnp.int32, (8, 1, 8, 1), 0)
    j = lax.broadcasted_iota(jnp.int32, (8, 1, 8, 1), 2)
    return i == j


def _block_diag(w):
    w4 = jnp.broadcast_to(w[:, :, None, :], (8, 64, 8, 64))
    return jnp.where(_on_diagonal(), w4, 0.0).reshape(MIX, MIX)


def _block_diag_inv(g):
    return jnp.sum(jnp.where(_on_diagonal(), g.reshape(8, 64, 8, 64), 0.0), axis=2)


def _head8(v):
    return jnp.pad(v[None, :], ((0, 7), (0, LANE - v.shape[0])))


GROUPS = dict(A=("w_in",), B=("w_uq", "w_ukv", "ssd_conv_w", "lru_conv_w", "w_branch", "w_out"),
              C=("w_ff1", "w_ff2", "w_ple_gate", "w_ple"))


def _kernel_weights(grp, fw):
    if grp == "A":
        w_in = _pad_w_in(fw["w_in"])
        return dict(w_in=w_in, w_dt=w_in[:, U_DT[0]:U_DT[0] + LANE])
    if grp == "C":
        return dict(w_ff1=fw["w_ff1"], w_ff2=fw["w_ff2"], w_pg=fw["w_ple_gate"], w_ple=fw["w_ple"])
    wb = fw["w_branch"]
    wb0 = jnp.pad(wb[0].reshape(N_HEADS, V_HEAD, D_MODEL), ((0, 0), (0, LANE - V_HEAD), (0, 0))).reshape(N_HEADS * LANE, D_MODEL)
    return dict(
        w_uq=_head_pad_cols(fw["w_uq"], QK_NOPE + QK_ROPE, 0, QK_NOPE + QK_ROPE),
        w_uk=_head_pad_cols(fw["w_ukv"], QK_NOPE + V_HEAD, 0, QK_NOPE),
        w_uv=_head_pad_cols(fw["w_ukv"], QK_NOPE + V_HEAD, QK_NOPE, QK_NOPE + V_HEAD),
        wb=[wb0, wb[1], wb[2], wb[3]], w_out=fw["w_out"], ssd_conv_w=fw["ssd_conv_w"], lru_conv_w=fw["lru_conv_w"])


def _layer_params(sp, l):
    row = lambda n: sp[n][l][None, :]
    return dict(
        g_mix=row("g_mix"), q_norm=row("q_norm"), kv_norm=row("kv_norm"),
        pool=[sp["w_pool"][l].reshape(4 * LANE, LANE), row("pool_scale")],
        ssd=[None, row("ssd_conv_b"), _head8(sp["ssd_dt_bias"][l]), _head8(sp["ssd_a_log"][l]),
             _head8(sp["ssd_d"][l]), row("ssd_norm")],
        lru=[None, row("lru_conv_b"), _block_diag(sp["lru_w_a"][l]), row("lru_b_a"),
             _block_diag(sp["lru_w_i"][l]), row("lru_b_i"), row("lru_lambda")],
        g_mlp=row("g_mlp"), g_ple=row("g_ple"),
    )


_sig = jax.nn.sigmoid
_SSD_CARRY = [(HALO, SSD_XBC)] + [(LANE, LANE)] * 4


def _tiles(rows):
    return dict(tm=_pick(rows, 512), ta=_pick(rows, 512), tp=_pick(rows, 512), tl=_pick(rows, 512), ts=_pick(rows, 256))


def _mixer_tiles(u, dt32):
    return dict(
        cq=(u, 384, U_CQ[0] // 384), ckv=(u, 256, U_CKV[0] // 256), kr=(u, LANE, U_KR[0] // LANE),
        pool=(u, MIX, U_POOL[0] // MIX), z=(u, MIX, U_Z[0] // MIX), xbc=(u, SSD_XBC, U_XBC[0] // SSD_XBC),
        dt=(dt32, LANE, 0), lg=(u, MIX, U_LG[0] // MIX), lx=(u, MIX, U_LX[0] // MIX))


def _layer_fwd(x, p_bf, ctx, l, pr, cosf, sinf):
    rows = x.shape[0]
    ts = _tiles(rows)
    tm = ts["tm"]
    nm = lambda s: f"{s}_l{l}"
    r = dict(x=x)
    (h,), _ = seq_fwd(nm("rms_in"), f_rms, [pr["g_mix"]], [(x, D_MODEL, 0)], [], [(D_MODEL, bf16)], tm)
    w = dict(_kernel_weights("A", ctx.weights(l, "A", h)))
    u = matmul(nm("w_in"), h, w["w_in"], outs=(U_DTYPE,))
    dt32 = matmul(nm("w_dt"), h, w["w_dt"])
    mt = _mixer_tiles(u, dt32)
    (cqn,), _ = seq_fwd(nm("rms_q"), f_rms, [pr["q_norm"]], [mt["cq"]], [], [(Q_LORA, bf16)], tm)
    (ckvn,), _ = seq_fwd(nm("rms_kv"), f_rms, [pr["kv_norm"]], [mt["ckv"]], [], [(KV_LORA, bf16)], tm)
    (yb,), pool_saved = seq_fwd(nm("pool"), f_pool, pr["pool"], [mt["pool"]], [(POOL_HALO, MIX)], [(MIX, bf16)], ts["tp"])
    w.update(_kernel_weights("B", ctx.weights(l, "B", yb)))
    pr = dict(pr, ssd=[w["ssd_conv_w"]] + pr["ssd"][1:], lru=[w["lru_conv_w"]] + pr["lru"][1:])
    q = matmul(nm("w_uq"), cqn, w["w_uq"])
    kn = matmul(nm("w_uk"), ckvn, w["w_uk"])
    vb = matmul(nm("w_uv"), ckvn, w["w_uv"], outs=(bf16,))
    hw = N_HEADS * LANE
    (qr, kr), _ = seq_fwd(nm("mla_prep"), f_prep, [], [(q, hw, 0), (kn, hw, 0), mt["kr"], (cosf, LANE, 0), (sinf, LANE, 0)],
                          [], [(hw, bf16), (hw, bf16)], tm)
    o, lse = attn_fwd(qr, kr, vb, ts["ta"])
    (yc,), ssd_saved = seq_fwd(nm("ssd"), f_ssd, pr["ssd"], [mt["z"], mt["xbc"], mt["dt"]], _SSD_CARRY, [(MIX, bf16)], SSD_CHUNK)
    (la, lu), lru_saved = seq_fwd(nm("lru_pre"), f_lru_pre, pr["lru"], [mt["lx"]], [(HALO, MIX)], [(MIX, f32), (MIX, f32)], ts["tl"])
    hh = scan_fwd(la, lu, ts["ts"])
    (yd,), _ = seq_fwd(nm("lru_post"), f_lru_post, [], [(hh, MIX, 0), mt["lg"]], [], [(MIX, bf16)], tm)
    ys = [o, yb, yc, yd]
    m, pres = merge_fwd(nm("merge"), ys, w["wb"], u)
    x1 = matmul(nm("w_out"), m, w["w_out"], epi=lambda acc, xr: (acc + xr,), extras=[(x, 0)])
    (h2,), _ = seq_fwd(nm("rms_mlp"), f_rms, [pr["g_mlp"]], [(x1, D_MODEL, 0)], [], [(D_MODEL, bf16)], tm)
    w.update(_kernel_weights("C", ctx.weights(l, "C", h2)))
    a1, act = matmul(nm("ff1"), h2, w["w_ff1"], outs=(bf16, bf16), epi=lambda acc: (acc, jnp.square(jnp.maximum(acc, 0.0))))
    x2 = matmul(nm("ff2"), act, w["w_ff2"], epi=lambda acc, xr: (acc + xr,), extras=[(x1, 0)])
    (h3,), _ = seq_fwd(nm("rms_ple"), f_rms, [pr["g_ple"]], [(x2, D_MODEL, 0)], [], [(D_MODEL, bf16)], tm)
    gl = matmul(nm("ple_gate"), h3, w["w_pg"])
    x3, pe = matmul(nm("ple"), p_bf, w["w_ple"], outs=(f32, f32), epi=lambda acc, g, xr: (xr + acc * _sig(g), acc),
                    extras=[(gl, 0), (x2, 0)])
    r.update(h=h, u=u, cqn=cqn, ckvn=ckvn, q=q, kn=kn, vb=vb, qr=qr, kr=kr, o=o, lse=lse, ys=ys, pres=pres, m=m, x1=x1,
             h2=h2, a1=a1, act=act, x2=x2, h3=h3, gl=gl, pe=pe, p_bf=p_bf, pool_saved=pool_saved, ssd_saved=ssd_saved,
             lru_saved=lru_saved, la=la, hh=hh, w=w, pr=pr, dt32=dt32)
    return x3, r


def _gate_bwd(d, g, pre):
    s = _sig(g.astype(f32))
    return d * s, d * pre.astype(f32) * s * (1.0 - s)


def _layer_bwd(dx3, r, ctx, l, cosf, sinf, tok, extra_small):
    rows = dx3.shape[0]
    ts = _tiles(rows)
    tm = ts["tm"]
    nm = lambda s: f"{s}_l{l}"
    u, w, pr = r["u"], r["w"], r["pr"]
    mt = _mixer_tiles(u, r["dt32"])
    g = {}
    full = lambda a: (a, a.shape[1], 0)
    dpe, dgl = ew(nm("ple_bwd"), _gate_bwd, [full(dx3), full(r["gl"]), full(r["pe"])], [(D_MODEL, bf16)] * 2, tm)
    g["w_ple"] = matmul(nm("d_w_ple"), r["p_bf"], dpe, ta=True, outs=(bf16,), deps=[tok] if tok is not None else [])
    g["w_pg"] = matmul(nm("d_w_pg"), r["h3"], dgl, ta=True, outs=(bf16,))
    dh3 = matmul(nm("d_h3"), dgl, w["w_pg"], tb=True)
    (g["g_ple"],), (dx2,) = seq_bwd(nm("rms_ple_bwd"), f_rms, [pr["g_ple"]], [full(r["x2"])], [True], [], [dh3], [f32], tm,
                                    add_to=(0, dx3))
    da1 = matmul(nm("d_act"), dx2, w["w_ff2"], tb=True, outs=(bf16,),
                 epi=lambda acc, a: (acc * 2.0 * jnp.maximum(a, 0.0),), extras=[(r["a1"], 0)])
    g["w_ff2"] = matmul(nm("d_w_ff2"), r["act"], dx2, ta=True, outs=(bf16,))
    g["w_ff1"] = matmul(nm("d_w_ff1"), r["h2"], da1, ta=True, outs=(bf16,), out_blocks=N_DEV)
    tok = ctx.grads(l, "C", dict(w_ff1=g["w_ff1"], w_ff2=g["w_ff2"], w_ple_gate=g["w_pg"], w_ple=g["w_ple"]))
    dh2 = matmul(nm("d_h2"), da1, w["w_ff1"], tb=True, deps=[tok])
    (g["g_mlp"],), (dx1,) = seq_bwd(nm("rms_mlp_bwd"), f_rms, [pr["g_mlp"]], [full(r["x1"])], [True], [], [dh2], [f32], tm,
                                    add_to=(0, dx2))
    def merge_bwd(dm, *gates_and_pres):
        both = [_gate_bwd(dm, gates_and_pres[n], gates_and_pres[4 + n]) for n in range(4)]
        return tuple(b[0] for b in both) + tuple(b[1] for b in both)

    res = matmul(nm("d_merged"), dx1, w["w_out"], tb=True, outs=(bf16,) * 8, epi=merge_bwd,
                 extras=[(u, D_MODEL * n) for n in range(4)] + [(pre, 0) for pre in r["pres"]])
    dpres, dgates = list(res[:4]), list(res[4:])
    g["w_out"] = matmul(nm("d_w_out"), r["m"], dx1, ta=True, outs=(bf16,))
    dys, g["wb"] = [], []
    for n in range(4):
        g["wb"].append(matmul(nm(f"d_w_branch{n}"), r["ys"][n], dpres[n], ta=True, outs=(bf16,)))
        dys.append(matmul(nm(f"d_y{n}"), dpres[n], w["wb"][n], tb=True, outs=(bf16 if n == 0 else f32,)))
    dqr, dkr_, dv = attn_bwd(r["qr"], r["kr"], r["vb"], dys[0], r["o"], r["lse"], ts["ta"])
    _, (dq, dkn, dkrope) = seq_bwd(nm("mla_prep_bwd"), f_prep, [],
                                   [full(r["q"]), full(r["kn"]), mt["kr"], full(cosf), full(sinf)],
                                   [True, True, True, False, False], [], [dqr, dkr_], [bf16] * 3, tm)
    g["w_uq"] = matmul(nm("d_w_uq"), r["cqn"], dq, ta=True, outs=(bf16,))
    g["w_uk"] = matmul(nm("d_w_uk"), r["ckvn"], dkn, ta=True, outs=(bf16,))
    g["w_uv"] = matmul(nm("d_w_uv"), r["ckvn"], dv, ta=True, outs=(bf16,))
    dcqn = matmul(nm("d_cqn"), dq, w["w_uq"], tb=True)
    dckvn = matmul(nm("d_ckvn_k"), dkn, w["w_uk"], tb=True)
    dckvn = matmul(nm("d_ckvn_v"), dv, w["w_uv"], tb=True, epi=lambda acc, prev: (acc + prev,), extras=[(dckvn, 0)])
    (g["q_norm"],), (dcq,) = seq_bwd(nm("rms_q_bwd"), f_rms, [pr["q_norm"]], [mt["cq"]], [True], [], [dcqn], [bf16], tm)
    (g["kv_norm"],), (dckv,) = seq_bwd(nm("rms_kv_bwd"), f_rms, [pr["kv_norm"]], [mt["ckv"]], [True], [], [dckvn], [bf16], tm)
    g["pool"], (dpool,) = seq_bwd(nm("pool_bwd"), f_pool, pr["pool"], [mt["pool"]], [True], r["pool_saved"], [dys[1]],
                                  [bf16], ts["tp"])
    g["ssd"], (dz, dxbc, ddt) = seq_bwd(nm("ssd_bwd"), f_ssd, pr["ssd"], [mt["z"], mt["xbc"], mt["dt"]], [True] * 3,
                                        r["ssd_saved"], [dys[2]], [bf16] * 3, SSD_CHUNK)
    _, (dhh, dlg) = seq_bwd(nm("lru_post_bwd"), f_lru_post, [], [full(r["hh"]), mt["lg"]], [True, True], [], [dys[3]],
                            [f32, bf16], tm)
    da, du = scan_bwd(r["la"], r["hh"], dhh, ts["ts"])
    g["lru"], (dlx,) = seq_bwd(nm("lru_pre_bwd"), f_lru_pre, pr["lru"], [mt["lx"]], [True], r["lru_saved"], [da, du],
                               [bf16], ts["tl"])
    dk = _head_unpad_cols(g["w_uk"], QK_NOPE)
    dv_ = _head_unpad_cols(g["w_uv"], V_HEAD)
    wb0 = g["wb"][0].reshape(N_HEADS, LANE, D_MODEL)[:, :V_HEAD].reshape(MIX, D_MODEL)
    ssd, lru, pool = g["ssd"], g["lru"], g["pool"]
    tok = ctx.grads(l, "B", dict(
        w_uq=_head_unpad_cols(g["w_uq"], QK_NOPE + QK_ROPE).reshape(Q_LORA, -1),
        w_ukv=jnp.concatenate([dk, dv_], axis=2).reshape(KV_LORA, -1), ssd_conv_w=ssd[0], lru_conv_w=lru[0],
        w_branch=jnp.stack([wb0, g["wb"][1], g["wb"][2], g["wb"][3]]), w_out=g["w_out"]))
    du_p = jnp.concatenate(dgates + [dpool, dz, dlg, dlx, dxbc, dcq, dkrope, dckv, ddt,
                                     jnp.zeros((rows, U_COLS - U_DT[0] - LANE), bf16)], axis=1)
    small = dict(
        q_norm=g["q_norm"][0], kv_norm=g["kv_norm"][0],
        w_pool=pool[0].reshape(4, LANE, LANE), pool_scale=pool[1][0],
        ssd_conv_b=ssd[1][0], ssd_dt_bias=ssd[2][0, :8], ssd_a_log=ssd[3][0, :8], ssd_d=ssd[4][0, :8], ssd_norm=ssd[5][0],
        lru_conv_b=lru[1][0], lru_w_a=_block_diag_inv(lru[2]), lru_b_a=lru[3][0], lru_w_i=_block_diag_inv(lru[4]),
        lru_b_i=lru[5][0], lru_lambda=lru[6][0], g_mlp=g["g_mlp"][0], g_ple=g["g_ple"][0])
    tok_small = ctx.small(f"l{l}", [(n, l, small[n]) for n in SMALL if n in small] + extra_small)
    g_w_in = matmul(nm("d_w_in"), r["h"], du_p, ta=True, outs=(bf16,), deps=[tok, tok_small])
    tok = ctx.grads(l, "A", dict(w_in=_w_in_blocks(g_w_in)))
    dh = matmul(nm("d_h"), du_p, w["w_in"], tb=True, deps=[tok])
    (g_mix,), (dx,) = seq_bwd(nm("rms_in_bwd"), f_rms, [pr["g_mix"]], [full(r["x"])], [True], [], [dh], [f32], tm,
                              add_to=(0, dx1))
    return dx, tok, ("g_mix", l, g_mix[0])


def _rope_tables(positions):
    inv = 1.0 / (ROPE_THETA ** (jnp.arange(0, QK_ROPE, 2, dtype=f32) / QK_ROPE))
    ang = positions.astype(f32)[:, None] * inv
    cos, sin = jnp.cos(ang), jnp.sin(ang)
    rows = positions.shape[0]
    pad = jnp.zeros((rows, LANE - KR_LANE - QK_ROPE), f32)
    cosf = jnp.concatenate([jnp.ones((rows, KR_LANE), f32), cos, cos, pad], axis=1)
    sinf = jnp.concatenate([jnp.zeros((rows, KR_LANE), f32), -sin, sin, pad], axis=1)
    return cosf, sinf


WEIGHTS = ['g_mix', 'w_in', 'q_norm', 'w_uq', 'kv_norm', 'w_ukv', 'w_pool', 'pool_scale', 'ssd_conv_w', 'ssd_conv_b',
           'ssd_dt_bias', 'ssd_a_log', 'ssd_d', 'ssd_norm', 'lru_conv_w', 'lru_conv_b', 'lru_w_a', 'lru_b_a', 'lru_w_i',
           'lru_b_i', 'lru_lambda', 'w_branch', 'w_out', 'g_mlp', 'w_ff1', 'w_ff2', 'g_ple', 'w_ple_gate', 'w_ple', 'g_final']
SHARDED = dict(w_in=2, w_uq=2, w_ukv=2, ssd_conv_w=2, lru_conv_w=2, w_branch=3, w_out=1, w_ff1=2, w_ff2=1,
               w_ple_gate=1, w_ple=2)
F32_PAYLOAD = ("ssd_conv_w", "lru_conv_w")
DEPTH = 2


SMALL = [n for n in WEIGHTS if n not in SHARDED and n != "g_final"]


def local_step(x, p, positions, tgt, sp, ctx):
    cosf, sinf = _rope_tables(positions)
    res = []
    for l in range(DEPTH):
        x, r = _layer_fwd(x, p[l].astype(bf16), ctx, l, _layer_params(sp, l), cosf, sinf)
        res.append(r)
    loss8, dx, dgf = loss_head(x, tgt, sp["g_final"][None, :], _pick(x.shape[0], 512))
    tok = None
    pending = ("g_final", None, dgf[0])
    for l in reversed(range(DEPTH)):
        dx, tok, pending = _layer_bwd(dx, res[l], ctx, l, cosf, sinf, tok, [pending])
    ctx.small("last", [pending])
    return loss8[0, 0], dx


def _payload(name, w):
    return w if name in F32_PAYLOAD else w.astype(bf16)


def _blocks(name, g):
    ax = SHARDED[name] - 1
    shape = list(g.shape)
    shape[ax:ax + 1] = [N_DEV, shape[ax] // N_DEV]
    return _payload(name, jnp.moveaxis(g.reshape(shape), ax, 0))


def _assemble(name, shards):
    ax = SHARDED[name] - 1
    shape = list(shards.shape[1:])
    shape[ax] *= N_DEV
    return jnp.moveaxis(shards, 0, ax).reshape(shape)


class _Exchanges:
    def __init__(self, wts):
        self.wts = wts
        self.ag, self.rs, self.sm = {}, {}, {}
        tok = None
        for l in range(DEPTH):
            for grp, names in GROUPS.items():
                h = exchange_start(f"ag_start_{grp}{l}", [_payload(n, wts[n][l]) for n in names], True,
                                   deps=[] if tok is None else [tok])
                tok = h["token"]
                self.ag[(l, grp)] = h
        self.all_started = tok

    def weights(self, l, grp, after):
        first = (l, grp) == (0, "A")
        got = exchange_wait(f"ag_wait_{grp}{l}", self.ag[(l, grp)], [after, self.all_started] if first else [after])
        out = {}
        for n, a in zip(GROUPS[grp], got):
            out[n] = a if n == "w_in" else _assemble(n, a)
        return out

    def grads(self, l, grp, g):
        cut = lambda n: g[n].ndim == self.wts[n].ndim
        h = exchange_start(f"rs_start_{grp}{l}", [g[n] if cut(n) else _blocks(n, g[n]) for n in GROUPS[grp]], False)
        self.rs[(l, grp)] = h
        return h["token"]

    def small(self, tag, entries):
        flat = jnp.concatenate([a.reshape(-1) for _, _, a in entries])
        flat = jnp.pad(flat, (0, (-flat.shape[0]) % (8 * LANE))).reshape(-1, LANE)
        h = exchange_start(f"small_start_{tag}", [flat], True)
        self.sm[tag] = (h, [(n, l, a.shape) for n, l, a in entries])
        return h["token"]

    def collect(self, groups, after):
        parts = {}
        for grp in groups:
            for l in reversed(range(DEPTH)):
                got = exchange_wait(f"rs_wait_{grp}{l}", self.rs[(l, grp)], [after])
                for n, a in zip(GROUPS[grp], got):
                    parts.setdefault(n, [None] * DEPTH)[l] = a
        return parts

    def collect_small(self, after):
        parts = {}
        for tag, (h, layout) in self.sm.items():
            (got,) = exchange_wait(f"small_wait_{tag}", h, [after])
            got = got.reshape(N_DEV, -1)
            off = 0
            for n, l, shape in layout:
                size = 1
                for d in shape:
                    size *= d
                part = got[:, off:off + size].reshape((N_DEV,) + tuple(shape))
                off += size
                if l is None:
                    parts[n] = [part]
                else:
                    parts.setdefault(n, [None] * DEPTH)[l] = part
        return parts


def kernel(x, p, positions, g_mix, w_in, q_norm, w_uq, kv_norm, w_ukv, w_pool, pool_scale, ssd_conv_w, ssd_conv_b,
           ssd_dt_bias, ssd_a_log, ssd_d, ssd_norm, lru_conv_w, lru_conv_b, lru_w_a, lru_b_a, lru_w_i, lru_b_i,
           lru_lambda, w_branch, w_out, g_mlp, w_ff1, w_ff2, g_ple, w_ple_gate, w_ple, g_final, loss_target, m_g_mix,
           m_w_in, m_q_norm, m_w_uq, m_kv_norm, m_w_ukv, m_w_pool, m_pool_scale, m_ssd_conv_w, m_ssd_conv_b,
           m_ssd_dt_bias, m_ssd_a_log, m_ssd_d, m_ssd_norm, m_lru_conv_w, m_lru_conv_b, m_lru_w_a, m_lru_b_a,
           m_lru_w_i, m_lru_b_i, m_lru_lambda, m_w_branch, m_w_out, m_g_mlp, m_w_ff1, m_w_ff2, m_g_ple, m_w_ple_gate,
           m_w_ple, m_g_final, v_g_mix, v_w_in, v_q_norm, v_w_uq, v_kv_norm, v_w_ukv, v_w_pool, v_pool_scale,
           v_ssd_conv_w, v_ssd_conv_b, v_ssd_dt_bias, v_ssd_a_log, v_ssd_d, v_ssd_norm, v_lru_conv_w, v_lru_conv_b,
           v_lru_w_a, v_lru_b_a, v_lru_w_i, v_lru_b_i, v_lru_lambda, v_w_branch, v_w_out, v_g_mlp, v_w_ff1, v_w_ff2,
           v_g_ple, v_w_ple_gate, v_w_ple, v_g_final):
    given = dict(locals())
    wts = {n: given[n] for n in WEIGHTS}
    ctx = _Exchanges(wts)
    loss, grad_x = local_step(x[0], p[:, 0], positions[0], loss_target[0], wts, ctx)

    def update(parts):
        out = {}
        for n, eight in parts.items():
            w, m, v = wts[n], given["m_" + n], given["v_" + n]
            if n == "g_final":
                out[n] = [a[0] for a in adamw(f"adamw_{n}", eight, w[None], m[None], v[None])]
            else:
                out[n] = adamw(f"adamw_{n}", eight, w, m, v)
        return out

    outs = update(ctx.collect(("C", "B"), grad_x))
    late = outs["w_ff1"][1]
    outs.update(update(ctx.collect(("A",), late)))
    outs.update(update(ctx.collect_small(late)))
    loss = lax.psum(loss, AXES)
    return (loss, grad_x[None], *[outs[n][0] for n in WEIGHTS], *[outs[n][1] for n in WEIGHTS],
            *[outs[n][2] for n in WEIGHTS], *[outs[n][3] for n in WEIGHTS])
```

```python
import functools

import jax
import jax.numpy as jnp
from jax import lax
from jax.experimental import pallas as pl
from jax.experimental.pallas import tpu as pltpu

f32 = jnp.float32
bf16 = jnp.bfloat16

D_MODEL = 1024
MIX = 512
N_HEADS = 8
QK_NOPE, QK_ROPE, V_HEAD = 64, 32, 64
Q_LORA, KV_LORA = 384, 256
ROPE_THETA = 10000.0
POOL_WINDOWS = (2, 4, 8, 16)
SSD_CHUNK = 128
SSD_XBC = 768
CONV_W = 4
LRU_C = 8.0
D_FF = 4096
EPS = 1e-6
IN_COLS = 7592
ADAM_LR, ADAM_B1, ADAM_B2, ADAM_EPS, ADAM_WD, ADAM_STEP = 0.001, 0.9, 0.999, 1e-08, 0.01, 10

LANE = 128
HALO = 8
POOL_HALO = 16
VMEM_LIMIT = 56 * 1024 * 1024
MATMUL_MAX_K_TILE = 4096
MATMUL_ACC_PASS_WEIGHT = 0.3
MATMUL_VMEM_BUDGET = 32 * 1024 * 1024
N_DEV = 8
AXES = ("x", "y", "c")

U_COLS = 8192
U_GATES, U_POOL, U_Z, U_LG, U_LX, U_XBC, U_CQ, U_KR, U_CKV, U_DT = (
    (0, 4096), (4096, 512), (4608, 512), (5120, 512), (5632, 512), (6144, 768),
    (6912, 384), (7296, 128), (7424, 256), (7680, 128))
KR_LANE = 64
U_DTYPE = bf16


def _cp(sem):
    return pltpu.CompilerParams(dimension_semantics=sem, vmem_limit_bytes=VMEM_LIMIT)


def _pick(dim, pref):
    if dim <= pref:
        return dim
    t = pref
    while t >= LANE:
        if dim % t == 0:
            return t
        t -= LANE
    t = pref
    while dim % t:
        t -= 8
    return t


@functools.partial(jax.custom_vjp, nondiff_argnums=(1,))
def shift_down(x, k):
    row = lax.broadcasted_iota(jnp.int32, x.shape, 0)
    return jnp.where(row >= k, pltpu.roll(x, k, 0), 0.0)


def _shift_down_fwd(x, k):
    return shift_down(x, k), None


def _shift_down_bwd(k, _, g):
    r = g.shape[0]
    row = lax.broadcasted_iota(jnp.int32, g.shape, 0)
    return (jnp.where(row < r - k, pltpu.roll(g, r - k, 0), 0.0),)


shift_down.defvjp(_shift_down_fwd, _shift_down_bwd)


@functools.partial(jax.custom_vjp, nondiff_argnums=(1,))
def lane_roll(x, s):
    return pltpu.roll(x, s, 1)


def _lane_roll_fwd(x, s):
    return lane_roll(x, s), None


def _lane_roll_bwd(s, _, g):
    return (pltpu.roll(g, (g.shape[1] - s) % g.shape[1], 1),)


lane_roll.defvjp(_lane_roll_fwd, _lane_roll_bwd)


def _tile_spec(tm, width, cb, n=None):
    if n is None:
        return pl.BlockSpec((tm, width), lambda i: (i, cb))
    return pl.BlockSpec((tm, width), lambda i: (n - 1 - i, cb))


def _const_spec(shape):
    nd = len(shape)
    return pl.BlockSpec(shape, lambda i: (0,) * nd)


def seq_fwd(name, f, params, tiles, carries, outs, tm):
    rows = tiles[0][0].shape[0]
    n = rows // tm
    np_, nt, no, nc = len(params), len(tiles), len(outs), len(carries)

    def body(*refs):
        p_refs = refs[:np_]
        t_refs = refs[np_:np_ + nt]
        o_refs = refs[np_ + nt:np_ + nt + no]
        s_refs = refs[np_ + nt + no:np_ + nt + no + nc]
        c_refs = refs[np_ + nt + no + nc:]
        i = pl.program_id(0)

        @pl.when(i == 0)
        def _():
            for c in c_refs:
                c[...] = jnp.zeros_like(c)

        cvals = [c[...] for c in c_refs]
        for s, c in zip(s_refs, cvals):
            s[0] = c
        o, newc = f(i, [r[...] for r in p_refs], cvals, [r[...].astype(f32) for r in t_refs])
        for r, v in zip(o_refs, o):
            r[...] = v.astype(r.dtype)
        for r, v in zip(c_refs, newc):
            r[...] = v

    in_specs = [_const_spec(p.shape) for p in params] + [_tile_spec(tm, w, cb) for (_, w, cb) in tiles]
    out_specs = [_tile_spec(tm, w, 0) for (w, _) in outs]
    out_specs += [pl.BlockSpec((1,) + tuple(c), lambda i, nd=len(c): (i,) + (0,) * nd) for c in carries]
    out_shape = [jax.ShapeDtypeStruct((rows, w), dt) for (w, dt) in outs]
    out_shape += [jax.ShapeDtypeStruct((n,) + tuple(c), f32) for c in carries]
    res = pl.pallas_call(
        body, name=name, grid=(n,), in_specs=in_specs, out_specs=out_specs, out_shape=out_shape,
        scratch_shapes=[pltpu.VMEM(tuple(c), f32) for c in carries],
        compiler_params=_cp(("arbitrary",)),
    )(*params, *[t[0] for t in tiles])
    return list(res[:no]), list(res[no:])


def seq_bwd(name, f, params, tiles, diff, saved, douts, gdtypes, tm):
    rows = tiles[0][0].shape[0]
    n = rows // tm
    np_, nt, nc, nd = len(params), len(tiles), len(saved), len(douts)
    didx = [k for k, d in enumerate(diff) if d]
    ng = len(didx)

    def body(*refs):
        p_refs = refs[:np_]
        t_refs = refs[np_:np_ + nt]
        s_refs = refs[np_ + nt:np_ + nt + nc]
        d_refs = refs[np_ + nt + nc:np_ + nt + nc + nd]
        pos = np_ + nt + nc + nd
        dp_refs = refs[pos:pos + np_]
        dt_refs = refs[pos + np_:pos + np_ + ng]
        dc_refs = refs[pos + np_ + ng:]
        i = pl.program_id(0)
        step = n - 1 - i

        @pl.when(i == 0)
        def _():
            for r in dp_refs:
                r[...] = jnp.zeros_like(r)
            for r in dc_refs:
                r[...] = jnp.zeros_like(r)

        pvals = [r[...] for r in p_refs]
        cvals = [r[0] for r in s_refs]
        xvals = [r[...].astype(f32) for r in t_refs]

        def fn(p, c, xd):
            x = list(xvals)
            for k, v in zip(didx, xd):
                x[k] = v
            return f(step, p, c, x)

        _, vjp = jax.vjp(fn, pvals, cvals, [xvals[k] for k in didx])
        dp, dc, dx = vjp(([r[...].astype(f32) for r in d_refs], [r[...] for r in dc_refs]))
        for r, v in zip(dp_refs, dp):
            r[...] += v
        for r, v in zip(dc_refs, dc):
            r[...] = v
        for r, v in zip(dt_refs, dx):
            r[...] = v.astype(r.dtype)

    in_specs = [_const_spec(p.shape) for p in params] + [_tile_spec(tm, w, cb, n) for (_, w, cb) in tiles]
    in_specs += [pl.BlockSpec((1,) + tuple(s.shape[1:]), lambda i, nd_=s.ndim - 1: (n - 1 - i,) + (0,) * nd_) for s in saved]
    in_specs += [_tile_spec(tm, d.shape[1], 0, n) for d in douts]
    args = list(params) + [t[0] for t in tiles] + list(saved) + list(douts)
    out_specs = [_const_spec(p.shape) for p in params] + [_tile_spec(tm, tiles[k][1], 0, n) for k in didx]
    out_shape = [jax.ShapeDtypeStruct(p.shape, f32) for p in params]
    out_shape += [jax.ShapeDtypeStruct((rows, tiles[k][1]), dt) for k, dt in zip(didx, gdtypes)]
    res = pl.pallas_call(
        body, name=name, grid=(n,), in_specs=in_specs, out_specs=out_specs, out_shape=out_shape,
        scratch_shapes=[pltpu.VMEM(tuple(s.shape[1:]), f32) for s in saved],
        compiler_params=_cp(("arbitrary",)),
    )(*args)
    return list(res[:np_]), list(res[np_:])


def _halvings(dim, lo, hi):
    t, out = _pick(dim, hi), []
    while t >= min(lo, dim) and dim % t == 0:
        out.append(t)
        if t % 2 or (t // 2) % 8:
            break
        t //= 2
    return out


def _matmul_tiles(m, n, k, a_item, b_item, per_out, max_tn=1024, whole_rows=False):
    def vmem_bytes(tm, tn, tk):
        acc = 4 if k // tk > 1 else 0
        return 2 * (tm * tk * a_item + tk * tn * b_item + tm * tn * per_out) + tm * tn * acc

    def traffic(tm, tn, tk):
        nk = k // tk
        return (m * k * a_item * (1 if nk == 1 else n // tn) + k * n * b_item * (m // tm)
                + (nk - 1) * m * n * 8 * MATMUL_ACC_PASS_WEIGHT)

    cands = [(traffic(tm, tn, tk), -tm * tn, tm, tn, tk)
             for tk in _halvings(k, 512, MATMUL_MAX_K_TILE) for tm in _halvings(m, 256, 4096)
             for tn in ([n] if whole_rows else _halvings(n, 512, min(1024, max_tn)))
             if vmem_bytes(tm, tn, tk) <= MATMUL_VMEM_BUDGET]
    return min(cands)[2:]


def matmul(name, a, b, *, ta=False, tb=False, outs=(f32,), epi=None, extras=(), rows=(), row_sums=0, deps=(),
           out_blocks=0):
    m, k = (a.shape[1], a.shape[0]) if ta else a.shape
    n = b.shape[0] if tb else b.shape[1]
    per_out = sum(jnp.dtype(dt).itemsize for dt in outs) + sum(e[0].dtype.itemsize for e in extras)
    whole_rows = bool(rows) or row_sums > 0
    tm, tn, tk = _matmul_tiles(m, n, k, a.dtype.itemsize, b.dtype.itemsize, per_out,
                               n // out_blocks if out_blocks else n, whole_rows)
    nk = k // tk
    ne, nr, nd, no = len(extras), len(rows), len(deps), len(outs)
    dims = (((0 if ta else 1,), (1 if tb else 0,)), ((), ()))

    def body(*refs):
        a_ref, b_ref = refs[0], refs[1]
        e_refs = refs[2:2 + ne]
        r_refs = refs[2 + ne:2 + ne + nr]
        o_refs = refs[2 + ne + nr + nd:2 + ne + nr + nd + no]
        s_refs = refs[2 + ne + nr + nd + no:2 + ne + nr + nd + no + row_sums]
        i, kk = pl.program_id(0), pl.program_id(2)
        part = lax.dot_general(a_ref[...].astype(bf16), b_ref[...].astype(bf16), dims, preferred_element_type=f32)

        def finish(total):
            res = (total,) if epi is None else epi(total, *[e[...] for e in e_refs], *[r[...] for r in r_refs])
            for r, v in zip(o_refs, res[:no]):
                r[...] = v.astype(r.dtype)
            for r, v in zip(s_refs, res[no:]):
                v8 = jnp.broadcast_to(v, r.shape)

                @pl.when(i == 0)
                def _(r=r, v8=v8):
                    r[...] = v8

                @pl.when(i > 0)
                def _(r=r, v8=v8):
                    r[...] += v8

        if nk == 1:
            finish(part)
            return
        acc = refs[-1]

        @pl.when(kk == 0)
        def _():
            acc[...] = part

        @pl.when(jnp.logical_and(kk > 0, kk < nk - 1))
        def _():
            acc[...] += part

        @pl.when(kk == nk - 1)
        def _():
            finish(acc[...] + part)

    a_spec = pl.BlockSpec((tk, tm), lambda i, j, q: (q, i)) if ta else pl.BlockSpec((tm, tk), lambda i, j, q: (i, q))
    b_spec = pl.BlockSpec((tn, tk), lambda i, j, q: (j, q)) if tb else pl.BlockSpec((tk, tn), lambda i, j, q: (q, j))
    assert all(off % tn == 0 for (_, off) in extras)
    e_specs = [pl.BlockSpec((tm, tn), lambda i, j, q, off=off // tn: (i, off + j)) for (_, off) in extras]
    r_specs = [pl.BlockSpec((1, tn), lambda i, j, q: (0, j)) for _ in rows]
    if out_blocks:
        per = n // out_blocks // tn
        out_spec = pl.BlockSpec((None, tm, tn), lambda i, j, q: (j // per, i, j % per))
        out_dims = (out_blocks, m, n // out_blocks)
    else:
        out_spec = pl.BlockSpec((tm, tn), lambda i, j, q: (i, j))
        out_dims = (m, n)
    res = pl.pallas_call(
        body, name=name, grid=(m // tm, n // tn, nk),
        in_specs=[a_spec, b_spec] + e_specs + r_specs + [pl.BlockSpec(memory_space=pl.ANY) for _ in deps],
        out_specs=[out_spec for _ in outs] + [pl.BlockSpec((8, tn), lambda i, j, q: (0, j))] * row_sums,
        out_shape=[jax.ShapeDtypeStruct(out_dims, dt) for dt in outs] + [jax.ShapeDtypeStruct((8, n), f32)] * row_sums,
        scratch_shapes=[pltpu.VMEM((tm, tn), f32)] if nk > 1 else [],
        compiler_params=_cp(("arbitrary" if row_sums else "parallel", "parallel", "arbitrary")),
    )(a, b, *[e[0] for e in extras], *rows, *deps)
    return res[0] if len(res) == 1 else tuple(res)


def merge_fwd(name, ys, wbs, u):
    rows, n_out = ys[0].shape[0], wbs[0].shape[1]
    nb = len(ys)
    tm, tn = _pick(rows, 512), _pick(n_out, 512)

    def body(*refs):
        y_refs, w_refs, g_refs = refs[:nb], refs[nb:2 * nb], refs[2 * nb:3 * nb]
        m_ref, p_refs = refs[3 * nb], refs[3 * nb + 1:]
        total = None
        for y_ref, w_ref, g_ref, p_ref in zip(y_refs, w_refs, g_refs, p_refs):
            pre = jnp.dot(y_ref[...], w_ref[...], preferred_element_type=f32)
            p_ref[...] = pre.astype(p_ref.dtype)
            term = jax.nn.sigmoid(g_ref[...].astype(f32)) * pre
            total = term if total is None else total + term
        m_ref[...] = total.astype(m_ref.dtype)

    in_specs = [pl.BlockSpec((tm, y.shape[1]), lambda i, j: (i, 0)) for y in ys]
    in_specs += [pl.BlockSpec((w.shape[0], tn), lambda i, j: (0, j)) for w in wbs]
    in_specs += [pl.BlockSpec((tm, tn), lambda i, j, off=n * (n_out // tn): (i, off + j)) for n in range(nb)]
    out_spec = pl.BlockSpec((tm, tn), lambda i, j: (i, j))
    res = pl.pallas_call(
        body, name=name, grid=(rows // tm, n_out // tn), in_specs=in_specs, out_specs=[out_spec] * (nb + 1),
        out_shape=[jax.ShapeDtypeStruct((rows, n_out), bf16)] * (nb + 1),
        compiler_params=_cp(("parallel", "parallel")),
    )(*ys, *wbs, *([u] * nb))
    return res[0], list(res[1:])


ATT_SCALE = (QK_NOPE + QK_ROPE) ** -0.5
LN2 = 0.6931471805599453
ATT_C = ATT_SCALE / LN2
NT = (((1,), (1,)), ((), ()))
TN = (((0,), (0,)), ((), ()))


def _causal(tq, tk):
    return lax.broadcasted_iota(jnp.int32, (tq, tk), 0) >= lax.broadcasted_iota(jnp.int32, (tq, tk), 1)


def _tri_pairs(n, by_column):
    if by_column:
        pairs = [(i, j) for j in range(n) for i in range(j, n)]
    else:
        pairs = [(i, j) for i in range(n) for j in range(i + 1)]
    return (jnp.asarray([a for a, _ in pairs], jnp.int32), jnp.asarray([b for _, b in pairs], jnp.int32))


HEADS_PER_STEP = 4
HEAD_PAIR = HEADS_PER_STEP * LANE


def attn_fwd(q, k, v, t):
    rows = q.shape[0]
    n = rows // t
    it, jt = _tri_pairs(n, False)

    def body(it_ref, jt_ref, q_ref, k_ref, v_ref, o_ref, lse_ref, m_s, l_s, acc_s):
        s_id = pl.program_id(1)
        i, j = it_ref[s_id], jt_ref[s_id]

        @pl.when(j == 0)
        def _():
            m_s[...] = jnp.full_like(m_s, -jnp.inf)
            l_s[...] = jnp.zeros_like(l_s)
            acc_s[...] = jnp.zeros_like(acc_s)

        def step(diag):
            for hh in range(HEADS_PER_STEP):
                sl = slice(LANE * hh, LANE * (hh + 1))
                s = lax.dot_general(q_ref[:, sl], k_ref[:, sl], NT, preferred_element_type=f32)
                if diag:
                    s = jnp.where(_causal(t, t), s, -jnp.inf)
                m_prev = m_s[:, sl]
                m_new = jnp.maximum(m_prev, jnp.max(s, axis=1, keepdims=True))
                alpha = jnp.exp2(m_prev - m_new)
                p = jnp.exp2(s - m_new[:, :1])
                l_s[:, sl] = alpha * l_s[:, sl] + jnp.sum(p, axis=1, keepdims=True)
                acc_s[:, sl] = alpha * acc_s[:, sl] + jnp.dot(p.astype(bf16), v_ref[:, sl], preferred_element_type=f32)
                m_s[:, sl] = m_new

        pl.when(j < i)(lambda: step(False))

        @pl.when(j == i)
        def _():
            step(True)
            o_ref[...] = (acc_s[...] / l_s[...]).astype(o_ref.dtype)
            lse_ref[...] = m_s[...] + jnp.log2(l_s[...])

    qs = pl.BlockSpec((t, HEAD_PAIR), lambda h, s, it_, jt_: (it_[s], h))
    ks = pl.BlockSpec((t, HEAD_PAIR), lambda h, s, it_, jt_: (jt_[s], h))
    hw = N_HEADS * LANE
    return pl.pallas_call(
        body, name="attn_fwd",
        grid_spec=pltpu.PrefetchScalarGridSpec(
            num_scalar_prefetch=2, grid=(hw // HEAD_PAIR, it.shape[0]), in_specs=[qs, ks, ks], out_specs=[qs, qs],
            scratch_shapes=[pltpu.VMEM((t, HEAD_PAIR), f32)] * 3),
        out_shape=[jax.ShapeDtypeStruct((rows, hw), bf16), jax.ShapeDtypeStruct((rows, hw), f32)],
        compiler_params=_cp(("parallel", "arbitrary")),
    )(it, jt, q, k, v)


def attn_bwd(q, k, v, do, o, lse, t):
    rows = q.shape[0]
    n = rows // t
    it, jt = _tri_pairs(n, True)

    def body(it_ref, jt_ref, q_ref, k_ref, v_ref, do_ref, o_ref, lse_ref, dq_ref, dk_ref, dv_ref, dk_s, dv_s):
        s_id = pl.program_id(1)
        i, j = it_ref[s_id], jt_ref[s_id]

        @pl.when(s_id == 0)
        def _():
            dq_ref[...] = jnp.zeros_like(dq_ref)

        @pl.when(i == j)
        def _():
            dk_s[...] = jnp.zeros_like(dk_s)
            dv_s[...] = jnp.zeros_like(dv_s)

        q_rows = pl.ds(pl.multiple_of(i * t, t), t)

        def step(diag):
            for hh in range(HEADS_PER_STEP):
                sl = slice(LANE * hh, LANE * (hh + 1))
                qh, kh, vh, doh = q_ref[:, sl], k_ref[:, sl], v_ref[:, sl], do_ref[:, sl]
                s = lax.dot_general(qh, kh, NT, preferred_element_type=f32)
                p = jnp.exp2(s - lse_ref[:, sl][:, :1])
                if diag:
                    p = jnp.where(_causal(t, t), p, 0.0)
                dp = lax.dot_general(doh, vh, NT, preferred_element_type=f32)
                delta = jnp.sum(doh.astype(f32) * o_ref[:, sl].astype(f32), axis=1, keepdims=True)
                ds = (p * (dp - delta) * LN2).astype(bf16)
                dv_s[:, sl] += lax.dot_general(p.astype(bf16), doh, TN, preferred_element_type=f32)
                dk_s[:, sl] += lax.dot_general(ds, qh, TN, preferred_element_type=f32)
                dq_ref[q_rows, sl] += jnp.dot(ds, kh, preferred_element_type=f32)

        pl.when(i > j)(lambda: step(False))
        pl.when(i == j)(lambda: step(True))

        @pl.when(i == n - 1)
        def _():
            dk_ref[...] = dk_s[...]
            dv_ref[...] = dv_s[...]

    qs = pl.BlockSpec((t, HEAD_PAIR), lambda h, s, it_, jt_: (it_[s], h))
    ks = pl.BlockSpec((t, HEAD_PAIR), lambda h, s, it_, jt_: (jt_[s], h))
    dqs = pl.BlockSpec((rows, HEAD_PAIR), lambda h, s, it_, jt_: (0, h))
    hw = N_HEADS * LANE
    return pl.pallas_call(
        body, name="attn_bwd",
        grid_spec=pltpu.PrefetchScalarGridSpec(
            num_scalar_prefetch=2, grid=(hw // HEAD_PAIR, it.shape[0]), in_specs=[qs, ks, ks, qs, qs, qs],
            out_specs=[dqs, ks, ks], scratch_shapes=[pltpu.VMEM((t, HEAD_PAIR), f32)] * 2),
        out_shape=[jax.ShapeDtypeStruct((rows, hw), f32)] * 3,
        compiler_params=_cp(("parallel", "arbitrary")),
    )(it, jt, q, k, v, do, o, lse)


def _steps(tm):
    k, out = 1, []
    while k < tm:
        out.append(k)
        k *= 2
    return out


def scan_fwd(a, u, tm):
    rows, ch = a.shape
    n = rows // tm

    def body(a_ref, u_ref, h_ref, h_s):
        @pl.when(pl.program_id(0) == 0)
        def _():
            h_s[...] = jnp.zeros_like(h_s)

        av, bv = a_ref[...], u_ref[...]
        row = lax.broadcasted_iota(jnp.int32, av.shape, 0)
        for k in _steps(tm):
            a_sh = jnp.where(row >= k, pltpu.roll(av, k, 0), 1.0)
            b_sh = jnp.where(row >= k, pltpu.roll(bv, k, 0), 0.0)
            bv = av * b_sh + bv
            av = av * a_sh
        h = bv + av * h_s[HALO - 1:HALO, :]
        h_ref[...] = h
        h_s[...] = h[tm - HALO:, :]

    spec = pl.BlockSpec((tm, ch), lambda i: (i, 0))
    return pl.pallas_call(
        body, name="lru_scan_fwd", grid=(n,), in_specs=[spec, spec], out_specs=spec,
        out_shape=jax.ShapeDtypeStruct((rows, ch), f32), scratch_shapes=[pltpu.VMEM((HALO, ch), f32)],
        compiler_params=_cp(("arbitrary",)),
    )(a, u)


def scan_bwd(a, h, dh, tm):
    rows, ch = a.shape
    n = rows // tm
    per = tm // HALO

    def body(a_ref, h_ref, hp_ref, dh_ref, da_ref, du_ref, g_s, a_s):
        i = pl.program_id(0)
        step = n - 1 - i

        @pl.when(i == 0)
        def _():
            g_s[...] = jnp.zeros_like(g_s)
            a_s[...] = jnp.zeros_like(a_s)

        a0 = a_ref[...]
        row = lax.broadcasted_iota(jnp.int32, a0.shape, 0)
        av = jnp.where(row < tm - 1, pltpu.roll(a0, tm - 1, 0), a_s[0:1, :])
        bv = dh_ref[...]
        for k in _steps(tm):
            a_sh = jnp.where(row < tm - k, pltpu.roll(av, tm - k, 0), 1.0)
            b_sh = jnp.where(row < tm - k, pltpu.roll(bv, tm - k, 0), 0.0)
            bv = bv + av * b_sh
            av = av * a_sh
        g = bv + av * g_s[0:1, :]
        h_last = jnp.where(step > 0, hp_ref[HALO - 1:HALO, :], 0.0)
        h_prev = jnp.where(row >= 1, pltpu.roll(h_ref[...], 1, 0), h_last)
        du_ref[...] = g
        da_ref[...] = g * h_prev
        g_s[...] = g[0:HALO, :]
        a_s[...] = a0[0:HALO, :]

    spec = pl.BlockSpec((tm, ch), lambda i: (n - 1 - i, 0))
    hp_spec = pl.BlockSpec((HALO, ch), lambda i: (jnp.maximum((n - 1 - i) * per - 1, 0), 0))
    return pl.pallas_call(
        body, name="lru_scan_bwd", grid=(n,), in_specs=[spec, spec, hp_spec, spec], out_specs=[spec, spec],
        out_shape=[jax.ShapeDtypeStruct((rows, ch), f32)] * 2,
        scratch_shapes=[pltpu.VMEM((HALO, ch), f32)] * 2,
        compiler_params=_cp(("arbitrary",)),
    )(a, h, h, dh)


def _rms(x, g):
    return x * lax.rsqrt(jnp.mean(x * x, axis=-1, keepdims=True) + EPS) * g


def f_rms(step, p, c, x):
    return [_rms(x[0], p[0])], []


def _rope(x, cosf, sinf):
    lane = lax.broadcasted_iota(jnp.int32, x.shape, 1)
    sw = jnp.where(lane < KR_LANE + QK_ROPE // 2, lane_roll(x, LANE - QK_ROPE // 2), lane_roll(x, QK_ROPE // 2))
    return x * cosf + sw * sinf


def f_prep(step, p, c, x):
    q, kn, kr, cosf, sinf = x
    kr_rot = _rope(kr, cosf, sinf)
    qr = [_rope(q[:, LANE * h:LANE * (h + 1)], cosf, sinf) * ATT_C for h in range(N_HEADS)]
    kk = [kn[:, LANE * h:LANE * (h + 1)] + kr_rot for h in range(N_HEADS)]
    return [jnp.concatenate(qr, axis=1), jnp.concatenate(kk, axis=1)], []


def _conv(tail, x, w, b):
    xf = jnp.concatenate([tail, x], axis=0)
    acc = b + w[CONV_W - 1:CONV_W, :] * xf
    for k in range(CONV_W - 1):
        acc = acc + w[k:k + 1, :] * shift_down(xf, CONV_W - 1 - k)
    return acc[HALO:, :]


def f_pool(step, p, c, x):
    wp, sc = p
    (tail,) = c
    (u,) = x
    tm = u.shape[0]
    xf = jnp.concatenate([tail, u], axis=0)
    sums, s, w = [], xf, 1
    while w < POOL_WINDOWS[-1]:
        s = s + shift_down(s, w)
        w *= 2
        sums.append(s)
    t = step * tm + lax.broadcasted_iota(jnp.int32, (tm, 1), 0)
    ys = []
    for g, (w, s) in enumerate(zip(POOL_WINDOWS, sums)):
        sl = slice(LANE * g, LANE * (g + 1))
        cnt = jnp.minimum(t + 1, w).astype(f32)
        d = s[POOL_HALO:, sl] / cnt - u[:, sl]
        ys.append(jnp.dot(d.astype(bf16), wp[LANE * g:LANE * (g + 1), :].astype(bf16), preferred_element_type=f32))
    return [jnp.concatenate(ys, axis=1) * sc], [u[tm - POOL_HALO:, :]]


def f_ssd(step, p, c, x):
    conv_w, conv_b, dtb, alog, dsk, ng = p
    tail, s_in = c[0], c[1:]
    z, xbc, dt = x
    ln = z.shape[0]
    xc = jax.nn.silu(_conv(tail, xbc, conv_w, conv_b))
    xs, bb, cc = xc[:, :MIX], xc[:, MIX:MIX + LANE], xc[:, MIX + LANE:]
    dtv = jax.nn.softplus(dt + dtb[0:1, :])
    a = dtv * -jnp.exp(alog[0:1, :])
    ri = lax.broadcasted_iota(jnp.int32, (ln, ln), 0)
    ci = lax.broadcasted_iota(jnp.int32, (ln, ln), 1)
    tril = (ri >= ci).astype(f32)
    triu = (ri <= ci).astype(f32)
    hi = lax.Precision.HIGHEST
    a_cs = jnp.dot(tril, a, precision=hi, preferred_element_type=f32)
    a_cs_t = lax.dot_general(a, triu, TN, precision=hi, preferred_element_type=f32)
    a_tot = jnp.sum(a, axis=0, keepdims=True)
    lane = lax.broadcasted_iota(jnp.int32, (1, LANE), 1)
    half = [(lane < 64).astype(f32), (lane >= 64).astype(f32)]
    hrow = lax.broadcasted_iota(jnp.int32, (LANE, 1), 0)

    def head(v, h):
        return jnp.sum(v * (lane == h).astype(f32), axis=1, keepdims=True)

    def pair(v, j):
        return head(v, 2 * j) * half[0] + head(v, 2 * j + 1) * half[1]

    cg = [(cc * half[g]).astype(bf16) for g in range(2)]
    bg = [(bb * half[g]).astype(bf16) for g in range(2)]
    cb = [lax.dot_general(cg[g], bg[g], NT, preferred_element_type=f32) for g in range(2)]
    ys, s_out = [], []
    for j in range(4):
        g = j // 2
        xs_j = xs[:, LANE * j:LANE * (j + 1)]
        xj = xs_j * pair(dtv, j)
        yj = xs_j * pair(dsk[0:1, :], j)
        for hh in range(2):
            h = 2 * j + hh
            rowv = jnp.sum(a_cs_t * (hrow == h).astype(f32), axis=0, keepdims=True)
            lmat = jnp.exp(jnp.where(ri >= ci, head(a_cs, h) - rowv, -jnp.inf))
            yj = yj + jnp.dot((cb[g] * lmat).astype(bf16), (xj * half[hh]).astype(bf16), preferred_element_type=f32)
        acs = pair(a_cs, j)
        tot = pair(a_tot, j)
        yj = yj + jnp.exp(acs) * jnp.dot(cg[g], s_in[j].astype(bf16), preferred_element_type=f32)
        s_new = jnp.exp(tot) * s_in[j] + lax.dot_general(bg[g], (xj * jnp.exp(tot - acs)).astype(bf16), TN,
                                                         preferred_element_type=f32)
        ys.append(yj)
        s_out.append(s_new)
    y = jnp.concatenate(ys, axis=1) * jax.nn.silu(z)
    return [_rms(y, ng)], [xbc[ln - HALO:, :]] + s_out


def _neg_expm1(y):
    series = -y * (1.0 + y * (0.5 + y * (1.0 / 6 + y * (1.0 / 24 + y * (1.0 / 120)))))
    return jnp.where(y > -0.05, series, 1.0 - jnp.exp(y))


def f_lru_pre(step, p, c, x):
    cw, cb_, wa, ba, wi, bi, lam = p
    (tail,) = c
    (lx,) = x
    tm = lx.shape[0]
    xc = _conv(tail, lx, cw, cb_)
    xb = xc.astype(bf16)
    r = jax.nn.sigmoid(jnp.dot(xb, wa.astype(bf16), preferred_element_type=f32) + ba)
    it = jax.nn.sigmoid(jnp.dot(xb, wi.astype(bf16), preferred_element_type=f32) + bi)
    log_a = -LRU_C * r * jax.nn.softplus(-lam)
    mult = jnp.sqrt(_neg_expm1(2.0 * log_a))
    return [jnp.exp(log_a), xc * it * mult], [lx[tm - HALO:, :]]


def f_lru_post(step, p, c, x):
    h, g = x
    return [h * jax.nn.gelu(g)], []


def loss_head(x, tgt, g, tm):
    rows, d = x.shape
    n = rows // tm

    def body(x_ref, t_ref, g_ref, loss_ref, dx_ref, dg_ref):
        @pl.when(pl.program_id(0) == 0)
        def _():
            loss_ref[...] = jnp.zeros_like(loss_ref)
            dg_ref[...] = jnp.zeros_like(dg_ref)

        def fn(gv, xv):
            err = _rms(xv, gv) - t_ref[...]
            return 0.5 * jnp.sum(jnp.mean(err * err, axis=-1, keepdims=True))

        val, (dg, dx) = jax.value_and_grad(fn, argnums=(0, 1))(g_ref[...], x_ref[...])
        loss_ref[...] += val
        dg_ref[...] += dg
        dx_ref[...] = dx

    spec = pl.BlockSpec((tm, d), lambda i: (i, 0))
    return pl.pallas_call(
        body, name="loss_head", grid=(n,), in_specs=[spec, spec, _const_spec((1, d))],
        out_specs=[_const_spec((8, LANE)), spec, _const_spec((1, d))],
        out_shape=[jax.ShapeDtypeStruct((8, LANE), f32), jax.ShapeDtypeStruct((rows, d), f32),
                   jax.ShapeDtypeStruct((1, d), f32)],
        compiler_params=_cp(("arbitrary",)),
    )(x, tgt, g)


def ew(name, fn, ins, outs, tm):
    rows = ins[0][0].shape[0]
    ni = len(ins)

    def body(*refs):
        res = fn(*[r[...].astype(f32) for r in refs[:ni]])
        for r, v in zip(refs[ni:], res):
            r[...] = v.astype(r.dtype)

    return pl.pallas_call(
        body, name=name, grid=(rows // tm,), in_specs=[_tile_spec(tm, w, cb) for (_, w, cb) in ins],
        out_specs=[_tile_spec(tm, w, 0) for (w, _) in outs],
        out_shape=[jax.ShapeDtypeStruct((rows, w), dt) for (w, dt) in outs],
        compiler_params=_cp(("parallel",)),
    )(*[t[0] for t in ins])


def _peers():
    x, y, c = lax.axis_index("x"), lax.axis_index("y"), lax.axis_index("c")
    me = 4 * x + 2 * y + c
    out = []
    for k in range(1, N_DEV):
        px = 1 - x if k & 4 else x
        py = 1 - y if k & 2 else y
        pc = 1 - c if k & 1 else c
        out.append(((px, py, pc), 4 * px + 2 * py + pc))
    return me, out


_HBM = pl.BlockSpec(memory_space=pltpu.HBM)
_SEM = pl.BlockSpec(memory_space=pltpu.SEMAPHORE)
_EFFECT = pltpu.SideEffectType.DATAFLOW_SIDE_EFFECTING


def _remote(src_ref, land_ref, gather, me, pid, dev, send_sems, recv_sems, k, recv_side):
    return pltpu.make_async_remote_copy(
        src_ref=src_ref if gather else src_ref.at[pid], dst_ref=land_ref.at[pid if recv_side else me],
        send_sem=send_sems.at[k], recv_sem=recv_sems.at[k], device_id=dev, device_id_type=pl.DeviceIdType.MESH)


def _own(src_ref, land_ref, gather, me, sem):
    return pltpu.make_async_copy(src_ref if gather else src_ref.at[me], land_ref.at[me], sem)


def exchange_start(name, srcs, gather, deps=()):
    n, nd = len(srcs), len(deps)
    shapes = [(s.shape if gather else s.shape[1:]) for s in srcs]
    lands = [lax.empty((N_DEV,) + tuple(sh), s.dtype) for s, sh in zip(srcs, shapes)]

    def body(*refs):
        src_refs, land_refs = refs[:n], refs[n:2 * n]
        send_sems, recv_sems, own_sem = refs[2 * n + nd:2 * n + nd + 3]
        token = refs[-1]
        me, peers = _peers()
        for k, (dev, pid) in enumerate(peers):
            for s_ref, l_ref in zip(src_refs, land_refs):
                _remote(s_ref, l_ref, gather, me, pid, dev, send_sems, recv_sems, k, False).start()
        for s_ref, l_ref in zip(src_refs, land_refs):
            _own(s_ref, l_ref, gather, me, own_sem).start()
        token[...] = jnp.zeros_like(token)

    hbm = lambda a: pltpu.with_memory_space_constraint(a, pltpu.HBM)
    res = pl.pallas_call(
        body, name=name,
        out_shape=(pltpu.SemaphoreType.DMA((N_DEV - 1,)), pltpu.SemaphoreType.DMA((N_DEV - 1,)), pltpu.SemaphoreType.DMA(()),
                   *[pltpu.HBM(a.shape, a.dtype) for a in list(srcs) + lands], jax.ShapeDtypeStruct((8, LANE), f32)),
        in_specs=[_HBM] * (2 * n) + [pl.BlockSpec(memory_space=pl.ANY)] * nd,
        out_specs=(_SEM, _SEM, _SEM, *([_HBM] * (2 * n)), pl.BlockSpec(memory_space=pltpu.VMEM)),
        input_output_aliases={i: 3 + i for i in range(2 * n)},
        compiler_params=pltpu.CompilerParams(has_side_effects=_EFFECT),
    )(*[hbm(a) for a in list(srcs) + lands], *deps)
    return dict(sems=res[:3], srcs=list(res[3:3 + n]), lands=list(res[3 + n:3 + 2 * n]), token=res[-1], gather=gather)


def exchange_wait(name, h, afters):
    n, gather = len(h["srcs"]), h["gather"]

    def body(*refs):
        src_refs, land_refs = refs[:n], refs[n:2 * n]
        send_sems, recv_sems, own_sem = refs[2 * n:2 * n + 3]
        me, peers = _peers()
        for k, (dev, pid) in enumerate(peers):
            for s_ref, l_ref in zip(src_refs, land_refs):
                _remote(s_ref, l_ref, gather, me, pid, dev, send_sems, recv_sems, k, True).wait_recv()
        for k, (dev, pid) in enumerate(peers):
            for s_ref, l_ref in zip(src_refs, land_refs):
                _remote(s_ref, l_ref, gather, me, pid, dev, send_sems, recv_sems, k, False).wait_send()
        for s_ref, l_ref in zip(src_refs, land_refs):
            _own(s_ref, l_ref, gather, me, own_sem).wait()

    arrs = h["srcs"] + h["lands"]
    res = pl.pallas_call(
        body, name=name, out_shape=tuple(pltpu.HBM(a.shape, a.dtype) for a in arrs),
        in_specs=[_HBM] * (2 * n) + [_SEM, _SEM, _SEM] + [pl.BlockSpec(memory_space=pl.ANY)] * len(afters),
        out_specs=tuple([_HBM] * (2 * n)), input_output_aliases={i: i for i in range(2 * n)},
        compiler_params=pltpu.CompilerParams(has_side_effects=_EFFECT),
    )(*arrs, *h["sems"], *afters)
    return list(res[n:])


def adamw(name, parts, w, m, v):
    nl = len(parts)
    shape = w.shape[1:]
    c = shape[-1]
    r = 1
    for s in shape[:-1]:
        r *= s
    tr = _pick(r, 256) if r % 8 == 0 else r
    nb = r // tr
    parts2 = [p.reshape(N_DEV, r, c) for p in parts]
    w2, m2, v2 = (a.reshape(nl, r, c) for a in (w, m, v))

    def body(*refs):
        p_refs = refs[:nl]
        w_ref, m_ref, v_ref, g_ref, d_ref, nm_ref, nv_ref = refs[nl:]
        layer = pl.program_id(0)
        for ll, p_ref in enumerate(p_refs):
            @pl.when(layer == ll)
            def _(p_ref=p_ref):
                g = p_ref[0].astype(f32)
                for i in range(1, N_DEV):
                    g = g + p_ref[i].astype(f32)
                mn = ADAM_B1 * m_ref[0] + (1.0 - ADAM_B1) * g
                vn = ADAM_B2 * v_ref[0] + (1.0 - ADAM_B2) * jnp.square(g)
                m_hat = mn / (1.0 - ADAM_B1 ** ADAM_STEP)
                v_hat = vn / (1.0 - ADAM_B2 ** ADAM_STEP)
                g_ref[0] = g
                d_ref[0] = -ADAM_LR * (m_hat / (jnp.sqrt(v_hat) + ADAM_EPS) + ADAM_WD * w_ref[0])
                nm_ref[0] = mn
                nv_ref[0] = vn

    def p_spec(ll):
        return pl.BlockSpec((N_DEV, tr, c), lambda l, i: (0, jnp.where(l == ll, i, jnp.where(l > ll, nb - 1, 0)), 0))

    spec = pl.BlockSpec((1, tr, c), lambda l, i: (l, i, 0))
    res = pl.pallas_call(
        body, name=name, grid=(nl, nb), in_specs=[p_spec(ll) for ll in range(nl)] + [spec, spec, spec],
        out_specs=[spec] * 4, out_shape=[jax.ShapeDtypeStruct((nl, r, c), f32)] * 4,
        compiler_params=_cp(("arbitrary", "arbitrary")),
    )(*parts2, w2, m2, v2)
    return [a.reshape(w.shape) for a in res]


_IN_SPLITS = dict(cq=(0, 384), ckv=(384, 640), kr=(640, 672), pool=(672, 1184), z=(1184, 1696), xbc=(1696, 2464),
                  dt=(2464, 2472), lg=(2472, 2984), lx=(2984, 3496), gates=(3496, 7592))


W_IN_SHARD = IN_COLS // N_DEV

_PAD_ORDER = ("gates", "pool", "z", "lg", "lx", "xbc", "cq", KR_LANE, "kr", LANE - KR_LANE - QK_ROPE, "ckv", "dt",
              LANE - 8, U_COLS - U_DT[0] - LANE)
_SEGMENTS = ((0, U_CQ[0], 384), (384, U_CKV[0], 256), (640, U_KR[0] + KR_LANE, QK_ROPE), (672, U_POOL[0], 512),
             (1184, U_Z[0], 512), (1696, U_XBC[0], 768), (2464, U_DT[0], 8), (2472, U_LG[0], 512), (2984, U_LX[0], 512),
             (3496, 0, 4096))


def _pad_w_in(shards):
    rows = shards.shape[1]
    pieces = []
    for item in _PAD_ORDER:
        if isinstance(item, int):
            pieces.append(jnp.zeros((rows, item), shards.dtype))
            continue
        a, b = _IN_SPLITS[item]
        for d in range(a // W_IN_SHARD, (b - 1) // W_IN_SHARD + 1):
            lo, hi = max(a, d * W_IN_SHARD), min(b, (d + 1) * W_IN_SHARD)
            pieces.append(shards[d, :, lo - d * W_IN_SHARD:hi - d * W_IN_SHARD])
    return jnp.concatenate(pieces, axis=1)


def _w_in_blocks(g):
    blocks = []
    for d in range(N_DEV):
        a, b = d * W_IN_SHARD, (d + 1) * W_IN_SHARD
        pieces = []
        for ref, pad, width in _SEGMENTS:
            lo, hi = max(a, ref), min(b, ref + width)
            if lo < hi:
                pieces.append(g[:, pad + lo - ref:pad + hi - ref])
        blocks.append(jnp.concatenate(pieces, axis=1))
    return jnp.stack(blocks).astype(bf16)


def _head_pad_cols(w, per, lo, hi):
    k = w.shape[0]
    w = w.reshape(k, N_HEADS, per)[:, :, lo:hi]
    return jnp.pad(w, ((0, 0), (0, 0), (0, LANE - (hi - lo)))).reshape(k, N_HEADS * LANE)


def _head_unpad_cols(g, n):
    k = g.shape[0]
    return g.reshape(k, N_HEADS, LANE)[:, :, :n]


def _on_diagonal():
    i = lax.broadcasted_iota(jnp.int32, (8, 1, 8, 1), 0)
    j = lax.broadcasted_iota(jnp.int32, (8, 1, 8, 1), 2)
    return i == j


def _block_diag(w):
    w4 = jnp.broadcast_to(w[:, :, None, :], (8, 64, 8, 64))
    return jnp.where(_on_diagonal(), w4, 0.0).reshape(MIX, MIX)


def _block_diag_inv(g):
    return jnp.sum(jnp.where(_on_diagonal(), g.reshape(8, 64, 8, 64), 0.0), axis=2)


def _head8(v):
    return jnp.pad(v[None, :], ((0, 7), (0, LANE - v.shape[0])))


GROUPS = dict(A=("w_in",), B=("w_uq", "w_ukv", "ssd_conv_w", "lru_conv_w", "w_branch", "w_out"),
              C=("w_ff1", "w_ff2", "w_ple_gate", "w_ple"))


def _kernel_weights(grp, fw):
    if grp == "A":
        w_in = _pad_w_in(fw["w_in"])
        return dict(w_in=w_in, w_dt=w_in[:, U_DT[0]:U_DT[0] + LANE])
    if grp == "C":
        return dict(w_ff1=fw["w_ff1"], w_ff2=fw["w_ff2"], w_pg=fw["w_ple_gate"], w_ple=fw["w_ple"])
    wb = fw["w_branch"]
    wb0 = jnp.pad(wb[0].reshape(N_HEADS, V_HEAD, D_MODEL), ((0, 0), (0, LANE - V_HEAD), (0, 0))).reshape(N_HEADS * LANE, D_MODEL)
    return dict(
        w_uq=_head_pad_cols(fw["w_uq"], QK_NOPE + QK_ROPE, 0, QK_NOPE + QK_ROPE),
        w_uk=_head_pad_cols(fw["w_ukv"], QK_NOPE + V_HEAD, 0, QK_NOPE),
        w_uv=_head_pad_cols(fw["w_ukv"], QK_NOPE + V_HEAD, QK_NOPE, QK_NOPE + V_HEAD),
        wb=[wb0, wb[1], wb[2], wb[3]], w_out=fw["w_out"], ssd_conv_w=fw["ssd_conv_w"], lru_conv_w=fw["lru_conv_w"])


def _layer_params(sp, l):
    row = lambda n: sp[n][l][None, :]
    return dict(
        g_mix=row("g_mix"), q_norm=row("q_norm"), kv_norm=row("kv_norm"),
        pool=[sp["w_pool"][l].reshape(4 * LANE, LANE), row("pool_scale")],
        ssd=[None, row("ssd_conv_b"), _head8(sp["ssd_dt_bias"][l]), _head8(sp["ssd_a_log"][l]),
             _head8(sp["ssd_d"][l]), row("ssd_norm")],
        lru=[None, row("lru_conv_b"), _block_diag(sp["lru_w_a"][l]), row("lru_b_a"),
             _block_diag(sp["lru_w_i"][l]), row("lru_b_i"), row("lru_lambda")],
        g_mlp=row("g_mlp"), g_ple=row("g_ple"),
    )


_sig = jax.nn.sigmoid
_SSD_CARRY = [(HALO, SSD_XBC)] + [(LANE, LANE)] * 4


def _tiles(rows):
    return dict(tm=_pick(rows, 512), ta=_pick(rows, 512), tp=_pick(rows, 512), tl=_pick(rows, 512), ts=_pick(rows, 256))


def _mixer_tiles(u, dt32):
    return dict(
        cq=(u, 384, U_CQ[0] // 384), ckv=(u, 256, U_CKV[0] // 256), kr=(u, LANE, U_KR[0] // LANE),
        pool=(u, MIX, U_POOL[0] // MIX), z=(u, MIX, U_Z[0] // MIX), xbc=(u, SSD_XBC, U_XBC[0] // SSD_XBC),
        dt=(dt32, LANE, 0), lg=(u, MIX, U_LG[0] // MIX), lx=(u, MIX, U_LX[0] // MIX))


def _add_norm(acc, resid, g):
    x = acc + resid
    return x, _rms(x, g)


def _layer_fwd(x, h, p_bf, ctx, l, pr, g_next, cosf, sinf):
    rows = x.shape[0]
    ts = _tiles(rows)
    tm = ts["tm"]
    nm = lambda s: f"{s}_l{l}"
    r = dict(x=x)
    if h is None:
        (h,), _ = seq_fwd(nm("rms_in"), f_rms, [pr["g_mix"]], [(x, D_MODEL, 0)], [], [(D_MODEL, bf16)], tm)
    w = dict(_kernel_weights("A", ctx.weights(l, "A", h)))
    u = matmul(nm("w_in"), h, w["w_in"], outs=(U_DTYPE,))
    dt32 = matmul(nm("w_dt"), h, w["w_dt"])
    mt = _mixer_tiles(u, dt32)
    (cqn,), _ = seq_fwd(nm("rms_q"), f_rms, [pr["q_norm"]], [mt["cq"]], [], [(Q_LORA, bf16)], tm)
    (ckvn,), _ = seq_fwd(nm("rms_kv"), f_rms, [pr["kv_norm"]], [mt["ckv"]], [], [(KV_LORA, bf16)], tm)
    (yb,), pool_saved = seq_fwd(nm("pool"), f_pool, pr["pool"], [mt["pool"]], [(POOL_HALO, MIX)], [(MIX, bf16)], ts["tp"])
    w.update(_kernel_weights("B", ctx.weights(l, "B", yb)))
    pr = dict(pr, ssd=[w["ssd_conv_w"]] + pr["ssd"][1:], lru=[w["lru_conv_w"]] + pr["lru"][1:])
    q = matmul(nm("w_uq"), cqn, w["w_uq"])
    kn = matmul(nm("w_uk"), ckvn, w["w_uk"])
    vb = matmul(nm("w_uv"), ckvn, w["w_uv"], outs=(bf16,))
    hw = N_HEADS * LANE
    (qr, kr), _ = seq_fwd(nm("mla_prep"), f_prep, [], [(q, hw, 0), (kn, hw, 0), mt["kr"], (cosf, LANE, 0), (sinf, LANE, 0)],
                          [], [(hw, bf16), (hw, bf16)], tm)
    o, lse = attn_fwd(qr, kr, vb, ts["ta"])
    (yc,), ssd_saved = seq_fwd(nm("ssd"), f_ssd, pr["ssd"], [mt["z"], mt["xbc"], mt["dt"]], _SSD_CARRY, [(MIX, bf16)], SSD_CHUNK)
    (la, lu), lru_saved = seq_fwd(nm("lru_pre"), f_lru_pre, pr["lru"], [mt["lx"]], [(HALO, MIX)], [(MIX, f32), (MIX, f32)], ts["tl"])
    hh = scan_fwd(la, lu, ts["ts"])
    (yd,), _ = seq_fwd(nm("lru_post"), f_lru_post, [], [(hh, MIX, 0), mt["lg"]], [], [(MIX, bf16)], tm)
    ys = [o, yb, yc, yd]
    m, pres = merge_fwd(nm("merge"), ys, w["wb"], u)
    x1, h2 = matmul(nm("w_out"), m, w["w_out"], outs=(f32, bf16), epi=_add_norm, extras=[(x, 0)], rows=[pr["g_mlp"]])
    w.update(_kernel_weights("C", ctx.weights(l, "C", h2)))
    a1, act = matmul(nm("ff1"), h2, w["w_ff1"], outs=(bf16, bf16), epi=lambda acc: (acc, jnp.square(jnp.maximum(acc, 0.0))))
    x2, h3 = matmul(nm("ff2"), act, w["w_ff2"], outs=(f32, bf16), epi=_add_norm, extras=[(x1, 0)], rows=[pr["g_ple"]])
    gl = matmul(nm("ple_gate"), h3, w["w_pg"])
    if g_next is None:
        x3, pe = matmul(nm("ple"), p_bf, w["w_ple"], outs=(f32, f32), epi=lambda acc, g, xr: (xr + acc * _sig(g), acc),
                        extras=[(gl, 0), (x2, 0)])
        h_next = None
    else:
        def ple_norm(acc, g, xr, gn):
            xo = xr + acc * _sig(g)
            return xo, acc, _rms(xo, gn)

        x3, pe, h_next = matmul(nm("ple"), p_bf, w["w_ple"], outs=(f32, f32, bf16), epi=ple_norm,
                                extras=[(gl, 0), (x2, 0)], rows=[g_next])
    r.update(h=h, u=u, cqn=cqn, ckvn=ckvn, q=q, kn=kn, vb=vb, qr=qr, kr=kr, o=o, lse=lse, ys=ys, pres=pres, m=m, x1=x1,
             h2=h2, a1=a1, act=act, x2=x2, h3=h3, gl=gl, pe=pe, p_bf=p_bf, pool_saved=pool_saved, ssd_saved=ssd_saved,
             lru_saved=lru_saved, la=la, hh=hh, w=w, pr=pr, dt32=dt32)
    return x3, h_next, r


def _norm_bwd(dh, x, resid, g):
    rs = lax.rsqrt(jnp.mean(x * x, axis=-1, keepdims=True) + EPS)
    xhat = x * rs
    dxn = dh * g
    dx = rs * (dxn - xhat * jnp.mean(dxn * xhat, axis=-1, keepdims=True)) + resid
    return dx, jnp.sum(dh * xhat, axis=0, keepdims=True)


def _gate_bwd(d, g, pre):
    s = _sig(g.astype(f32))
    return d * s, d * pre.astype(f32) * s * (1.0 - s)


def _layer_bwd(dx3, r, ctx, l, cosf, sinf, tok, extra_small):
    rows = dx3.shape[0]
    ts = _tiles(rows)
    tm = ts["tm"]
    nm = lambda s: f"{s}_l{l}"
    u, w, pr = r["u"], r["w"], r["pr"]
    mt = _mixer_tiles(u, r["dt32"])
    g = {}
    full = lambda a: (a, a.shape[1], 0)
    dpe, dgl = ew(nm("ple_bwd"), _gate_bwd, [full(dx3), full(r["gl"]), full(r["pe"])], [(D_MODEL, bf16)] * 2, tm)
    g["w_ple"] = matmul(nm("d_w_ple"), r["p_bf"], dpe, ta=True, outs=(bf16,), deps=[tok] if tok is not None else [])
    g["w_pg"] = matmul(nm("d_w_pg"), r["h3"], dgl, ta=True, outs=(bf16,))
    dx2, g["g_ple"] = matmul(nm("d_h3"), dgl, w["w_pg"], tb=True, epi=_norm_bwd, extras=[(r["x2"], 0), (dx3, 0)],
                             rows=[pr["g_ple"]], row_sums=1)
    da1 = matmul(nm("d_act"), dx2, w["w_ff2"], tb=True, outs=(bf16,),
                 epi=lambda acc, a: (acc * 2.0 * jnp.maximum(a, 0.0),), extras=[(r["a1"], 0)])
    g["w_ff2"] = matmul(nm("d_w_ff2"), r["act"], dx2, ta=True, outs=(bf16,))
    g["w_ff1"] = matmul(nm("d_w_ff1"), r["h2"], da1, ta=True, outs=(bf16,), out_blocks=N_DEV)
    tok = ctx.grads(l, "C", dict(w_ff1=g["w_ff1"], w_ff2=g["w_ff2"], w_ple_gate=g["w_pg"], w_ple=g["w_ple"]))
    dx1, g["g_mlp"] = matmul(nm("d_h2"), da1, w["w_ff1"], tb=True, epi=_norm_bwd, extras=[(r["x1"], 0), (dx2, 0)],
                             rows=[pr["g_mlp"]], row_sums=1, deps=[tok])
    def merge_bwd(dm, *gates_and_pres):
        both = [_gate_bwd(dm, gates_and_pres[n], gates_and_pres[4 + n]) for n in range(4)]
        return tuple(b[0] for b in both) + tuple(b[1] for b in both)

    res = matmul(nm("d_merged"), dx1, w["w_out"], tb=True, outs=(bf16,) * 8, epi=merge_bwd,
                 extras=[(u, D_MODEL * n) for n in range(4)] + [(pre, 0) for pre in r["pres"]])
    dpres, dgates = list(res[:4]), list(res[4:])
    g["w_out"] = matmul(nm("d_w_out"), r["m"], dx1, ta=True, outs=(bf16,))
    dys, g["wb"] = [], []
    for n in range(4):
        g["wb"].append(matmul(nm(f"d_w_branch{n}"), r["ys"][n], dpres[n], ta=True, outs=(bf16,)))
        dys.append(matmul(nm(f"d_y{n}"), dpres[n], w["wb"][n], tb=True, outs=(bf16 if n == 0 else f32,)))
    dqr, dkr_, dv = attn_bwd(r["qr"], r["kr"], r["vb"], dys[0], r["o"], r["lse"], ts["ta"])
    _, (dq, dkn, dkrope) = seq_bwd(nm("mla_prep_bwd"), f_prep, [],
                                   [full(r["q"]), full(r["kn"]), mt["kr"], full(cosf), full(sinf)],
                                   [True, True, True, False, False], [], [dqr, dkr_], [bf16] * 3, tm)
    g["w_uq"] = matmul(nm("d_w_uq"), r["cqn"], dq, ta=True, outs=(bf16,))
    g["w_uk"] = matmul(nm("d_w_uk"), r["ckvn"], dkn, ta=True, outs=(bf16,))
    g["w_uv"] = matmul(nm("d_w_uv"), r["ckvn"], dv, ta=True, outs=(bf16,))
    dcqn = matmul(nm("d_cqn"), dq, w["w_uq"], tb=True)
    dckvn = matmul(nm("d_ckvn_k"), dkn, w["w_uk"], tb=True)
    dckvn = matmul(nm("d_ckvn_v"), dv, w["w_uv"], tb=True, epi=lambda acc, prev: (acc + prev,), extras=[(dckvn, 0)])
    (g["q_norm"],), (dcq,) = seq_bwd(nm("rms_q_bwd"), f_rms, [pr["q_norm"]], [mt["cq"]], [True], [], [dcqn], [bf16], tm)
    (g["kv_norm"],), (dckv,) = seq_bwd(nm("rms_kv_bwd"), f_rms, [pr["kv_norm"]], [mt["ckv"]], [True], [], [dckvn], [bf16], tm)
    g["pool"], (dpool,) = seq_bwd(nm("pool_bwd"), f_pool, pr["pool"], [mt["pool"]], [True], r["pool_saved"], [dys[1]],
                                  [bf16], ts["tp"])
    g["ssd"], (dz, dxbc, ddt) = seq_bwd(nm("ssd_bwd"), f_ssd, pr["ssd"], [mt["z"], mt["xbc"], mt["dt"]], [True] * 3,
                                        r["ssd_saved"], [dys[2]], [bf16] * 3, SSD_CHUNK)
    _, (dhh, dlg) = seq_bwd(nm("lru_post_bwd"), f_lru_post, [], [full(r["hh"]), mt["lg"]], [True, True], [], [dys[3]],
                            [f32, bf16], tm)
    da, du = scan_bwd(r["la"], r["hh"], dhh, ts["ts"])
    g["lru"], (dlx,) = seq_bwd(nm("lru_pre_bwd"), f_lru_pre, pr["lru"], [mt["lx"]], [True], r["lru_saved"], [da, du],
                               [bf16], ts["tl"])
    dk = _head_unpad_cols(g["w_uk"], QK_NOPE)
    dv_ = _head_unpad_cols(g["w_uv"], V_HEAD)
    wb0 = g["wb"][0].reshape(N_HEADS, LANE, D_MODEL)[:, :V_HEAD].reshape(MIX, D_MODEL)
    ssd, lru, pool = g["ssd"], g["lru"], g["pool"]
    tok = ctx.grads(l, "B", dict(
        w_uq=_head_unpad_cols(g["w_uq"], QK_NOPE + QK_ROPE).reshape(Q_LORA, -1),
        w_ukv=jnp.concatenate([dk, dv_], axis=2).reshape(KV_LORA, -1), ssd_conv_w=ssd[0], lru_conv_w=lru[0],
        w_branch=jnp.stack([wb0, g["wb"][1], g["wb"][2], g["wb"][3]]), w_out=g["w_out"]))
    du_p = jnp.concatenate(dgates + [dpool, dz, dlg, dlx, dxbc, dcq, dkrope, dckv, ddt,
                                     jnp.zeros((rows, U_COLS - U_DT[0] - LANE), bf16)], axis=1)
    small = dict(
        q_norm=g["q_norm"][0], kv_norm=g["kv_norm"][0],
        w_pool=pool[0].reshape(4, LANE, LANE), pool_scale=pool[1][0],
        ssd_conv_b=ssd[1][0], ssd_dt_bias=ssd[2][0, :8], ssd_a_log=ssd[3][0, :8], ssd_d=ssd[4][0, :8], ssd_norm=ssd[5][0],
        lru_conv_b=lru[1][0], lru_w_a=_block_diag_inv(lru[2]), lru_b_a=lru[3][0], lru_w_i=_block_diag_inv(lru[4]),
        lru_b_i=lru[5][0], lru_lambda=lru[6][0], g_mlp=g["g_mlp"][0], g_ple=g["g_ple"][0])
    tok_small = ctx.small(f"l{l}", [(n, l, small[n]) for n in SMALL if n in small] + extra_small)
    g_w_in = matmul(nm("d_w_in"), r["h"], du_p, ta=True, outs=(bf16,), deps=[tok, tok_small])
    tok = ctx.grads(l, "A", dict(w_in=_w_in_blocks(g_w_in)))
    dx, g_mix = matmul(nm("d_h"), du_p, w["w_in"], tb=True, epi=_norm_bwd, extras=[(r["x"], 0), (dx1, 0)],
                       rows=[pr["g_mix"]], row_sums=1, deps=[tok])
    return dx, tok, ("g_mix", l, g_mix[0])


def _rope_tables(positions):
    inv = 1.0 / (ROPE_THETA ** (jnp.arange(0, QK_ROPE, 2, dtype=f32) / QK_ROPE))
    ang = positions.astype(f32)[:, None] * inv
    cos, sin = jnp.cos(ang), jnp.sin(ang)
    rows = positions.shape[0]
    pad = jnp.zeros((rows, LANE - KR_LANE - QK_ROPE), f32)
    cosf = jnp.concatenate([jnp.ones((rows, KR_LANE), f32), cos, cos, pad], axis=1)
    sinf = jnp.concatenate([jnp.zeros((rows, KR_LANE), f32), -sin, sin, pad], axis=1)
    return cosf, sinf


WEIGHTS = ['g_mix', 'w_in', 'q_norm', 'w_uq', 'kv_norm', 'w_ukv', 'w_pool', 'pool_scale', 'ssd_conv_w', 'ssd_conv_b',
           'ssd_dt_bias', 'ssd_a_log', 'ssd_d', 'ssd_norm', 'lru_conv_w', 'lru_conv_b', 'lru_w_a', 'lru_b_a', 'lru_w_i',
           'lru_b_i', 'lru_lambda', 'w_branch', 'w_out', 'g_mlp', 'w_ff1', 'w_ff2', 'g_ple', 'w_ple_gate', 'w_ple', 'g_final']
SHARDED = dict(w_in=2, w_uq=2, w_ukv=2, ssd_conv_w=2, lru_conv_w=2, w_branch=3, w_out=1, w_ff1=2, w_ff2=1,
               w_ple_gate=1, w_ple=2)
F32_PAYLOAD = ("ssd_conv_w", "lru_conv_w")
DEPTH = 2


SMALL = [n for n in WEIGHTS if n not in SHARDED and n != "g_final"]


def local_step(x, p, positions, tgt, sp, ctx):
    cosf, sinf = _rope_tables(positions)
    res, h = [], None
    for l in range(DEPTH):
        g_next = sp["g_mix"][l + 1][None, :] if l + 1 < DEPTH else None
        x, h, r = _layer_fwd(x, h, p[l].astype(bf16), ctx, l, _layer_params(sp, l), g_next, cosf, sinf)
        res.append(r)
    loss8, dx, dgf = loss_head(x, tgt, sp["g_final"][None, :], _pick(x.shape[0], 512))
    tok = None
    pending = ("g_final", None, dgf[0])
    for l in reversed(range(DEPTH)):
        dx, tok, pending = _layer_bwd(dx, res[l], ctx, l, cosf, sinf, tok, [pending])
    ctx.small("last", [pending])
    return loss8[0, 0], dx


def _payload(name, w):
    return w if name in F32_PAYLOAD else w.astype(bf16)


def _blocks(name, g):
    ax = SHARDED[name] - 1
    shape = list(g.shape)
    shape[ax:ax + 1] = [N_DEV, shape[ax] // N_DEV]
    return _payload(name, jnp.moveaxis(g.reshape(shape), ax, 0))


def _assemble(name, shards):
    ax = SHARDED[name] - 1
    shape = list(shards.shape[1:])
    shape[ax] *= N_DEV
    return jnp.moveaxis(shards, 0, ax).reshape(shape)


class _Exchanges:
    def __init__(self, wts):
        self.wts = wts
        self.ag, self.rs, self.sm = {}, {}, {}
        tok = None
        for l in range(DEPTH):
            for grp, names in GROUPS.items():
                h = exchange_start(f"ag_start_{grp}{l}", [_payload(n, wts[n][l]) for n in names], True,
                                   deps=[] if tok is None else [tok])
                tok = h["token"]
                self.ag[(l, grp)] = h
        self.all_started = tok

    def weights(self, l, grp, after):
        first = (l, grp) == (0, "A")
        got = exchange_wait(f"ag_wait_{grp}{l}", self.ag[(l, grp)], [after, self.all_started] if first else [after])
        out = {}
        for n, a in zip(GROUPS[grp], got):
            out[n] = a if n == "w_in" else _assemble(n, a)
        return out

    def grads(self, l, grp, g):
        cut = lambda n: g[n].ndim == self.wts[n].ndim
        h = exchange_start(f"rs_start_{grp}{l}", [g[n] if cut(n) else _blocks(n, g[n]) for n in GROUPS[grp]], False)
        self.rs[(l, grp)] = h
        return h["token"]

    def small(self, tag, entries):
        flat = jnp.concatenate([a.reshape(-1) for _, _, a in entries])
        flat = jnp.pad(flat, (0, (-flat.shape[0]) % (8 * LANE))).reshape(-1, LANE)
        h = exchange_start(f"small_start_{tag}", [flat], True)
        self.sm[tag] = (h, [(n, l, a.shape) for n, l, a in entries])
        return h["token"]

    def collect(self, groups, after):
        parts = {}
        for grp in groups:
            for l in reversed(range(DEPTH)):
                got = exchange_wait(f"rs_wait_{grp}{l}", self.rs[(l, grp)], [after])
                for n, a in zip(GROUPS[grp], got):
                    parts.setdefault(n, [None] * DEPTH)[l] = a
        return parts

    def collect_small(self, after):
        parts = {}
        for tag, (h, layout) in self.sm.items():
            (got,) = exchange_wait(f"small_wait_{tag}", h, [after])
            got = got.reshape(N_DEV, -1)
            off = 0
            for n, l, shape in layout:
                size = 1
                for d in shape:
                    size *= d
                part = got[:, off:off + size].reshape((N_DEV,) + tuple(shape))
                off += size
                if l is None:
                    parts[n] = [part]
                else:
                    parts.setdefault(n, [None] * DEPTH)[l] = part
        return parts


def kernel(x, p, positions, g_mix, w_in, q_norm, w_uq, kv_norm, w_ukv, w_pool, pool_scale, ssd_conv_w, ssd_conv_b,
           ssd_dt_bias, ssd_a_log, ssd_d, ssd_norm, lru_conv_w, lru_conv_b, lru_w_a, lru_b_a, lru_w_i, lru_b_i,
           lru_lambda, w_branch, w_out, g_mlp, w_ff1, w_ff2, g_ple, w_ple_gate, w_ple, g_final, loss_target, m_g_mix,
           m_w_in, m_q_norm, m_w_uq, m_kv_norm, m_w_ukv, m_w_pool, m_pool_scale, m_ssd_conv_w, m_ssd_conv_b,
           m_ssd_dt_bias, m_ssd_a_log, m_ssd_d, m_ssd_norm, m_lru_conv_w, m_lru_conv_b, m_lru_w_a, m_lru_b_a,
           m_lru_w_i, m_lru_b_i, m_lru_lambda, m_w_branch, m_w_out, m_g_mlp, m_w_ff1, m_w_ff2, m_g_ple, m_w_ple_gate,
           m_w_ple, m_g_final, v_g_mix, v_w_in, v_q_norm, v_w_uq, v_kv_norm, v_w_ukv, v_w_pool, v_pool_scale,
           v_ssd_conv_w, v_ssd_conv_b, v_ssd_dt_bias, v_ssd_a_log, v_ssd_d, v_ssd_norm, v_lru_conv_w, v_lru_conv_b,
           v_lru_w_a, v_lru_b_a, v_lru_w_i, v_lru_b_i, v_lru_lambda, v_w_branch, v_w_out, v_g_mlp, v_w_ff1, v_w_ff2,
           v_g_ple, v_w_ple_gate, v_w_ple, v_g_final):
    given = dict(locals())
    wts = {n: given[n] for n in WEIGHTS}
    ctx = _Exchanges(wts)
    loss, grad_x = local_step(x[0], p[:, 0], positions[0], loss_target[0], wts, ctx)

    def update(parts):
        out = {}
        for n, eight in parts.items():
            w, m, v = wts[n], given["m_" + n], given["v_" + n]
            if n == "g_final":
                out[n] = [a[0] for a in adamw(f"adamw_{n}", eight, w[None], m[None], v[None])]
            else:
                out[n] = adamw(f"adamw_{n}", eight, w, m, v)
        return out

    outs = update(ctx.collect(("C", "B"), grad_x))
    late = outs["w_ff1"][1]
    outs.update(update(ctx.collect(("A",), late)))
    outs.update(update(ctx.collect_small(late)))
    loss = lax.psum(loss, AXES)
    return (loss, grad_x[None], *[outs[n][0] for n in WEIGHTS], *[outs[n][1] for n in WEIGHTS],
            *[outs[n][2] for n in WEIGHTS], *[outs[n][3] for n in WEIGHTS])
```

```python
import functools

import jax
import jax.numpy as jnp
from jax import lax
from jax.experimental import pallas as pl
from jax.experimental.pallas import tpu as pltpu

f32 = jnp.float32
bf16 = jnp.bfloat16

D_MODEL = 1024
MIX = 512
N_HEADS = 8
QK_NOPE, QK_ROPE, V_HEAD = 64, 32, 64
Q_LORA, KV_LORA = 384, 256
ROPE_THETA = 10000.0
POOL_WINDOWS = (2, 4, 8, 16)
SSD_CHUNK = 128
SSD_XBC = 768
CONV_W = 4
LRU_C = 8.0
D_FF = 4096
EPS = 1e-6
IN_COLS = 7592
ADAM_LR, ADAM_B1, ADAM_B2, ADAM_EPS, ADAM_WD, ADAM_STEP = 0.001, 0.9, 0.999, 1e-08, 0.01, 10

LANE = 128
HALO = 8
POOL_HALO = 16
VMEM_LIMIT = 56 * 1024 * 1024
MATMUL_MAX_K_TILE = 4096
MATMUL_ACC_PASS_WEIGHT = 0.3
MATMUL_VMEM_BUDGET = 40 * 1024 * 1024
N_DEV = 8
AXES = ("x", "y", "c")

U_COLS = 8192
U_GATES, U_POOL, U_Z, U_LG, U_LX, U_XBC, U_CQ, U_KR, U_CKV, U_DT = (
    (0, 4096), (4096, 512), (4608, 512), (5120, 512), (5632, 512), (6144, 768),
    (6912, 384), (7296, 128), (7424, 256), (7680, 128))
KR_LANE = 64
U_DTYPE = bf16


def _cp(sem):
    return pltpu.CompilerParams(dimension_semantics=sem, vmem_limit_bytes=VMEM_LIMIT)


def _pick(dim, pref):
    if dim <= pref:
        return dim
    t = pref
    while t >= LANE:
        if dim % t == 0:
            return t
        t -= LANE
    t = pref
    while dim % t:
        t -= 8
    return t


@functools.partial(jax.custom_vjp, nondiff_argnums=(1,))
def shift_down(x, k):
    row = lax.broadcasted_iota(jnp.int32, x.shape, 0)
    return jnp.where(row >= k, pltpu.roll(x, k, 0), 0.0)


def _shift_down_fwd(x, k):
    return shift_down(x, k), None


def _shift_down_bwd(k, _, g):
    r = g.shape[0]
    row = lax.broadcasted_iota(jnp.int32, g.shape, 0)
    return (jnp.where(row < r - k, pltpu.roll(g, r - k, 0), 0.0),)


shift_down.defvjp(_shift_down_fwd, _shift_down_bwd)


@functools.partial(jax.custom_vjp, nondiff_argnums=(1,))
def lane_roll(x, s):
    return pltpu.roll(x, s, 1)


def _lane_roll_fwd(x, s):
    return lane_roll(x, s), None


def _lane_roll_bwd(s, _, g):
    return (pltpu.roll(g, (g.shape[1] - s) % g.shape[1], 1),)


lane_roll.defvjp(_lane_roll_fwd, _lane_roll_bwd)


def _tile_spec(tm, width, cb, n=None):
    if n is None:
        return pl.BlockSpec((tm, width), lambda i: (i, cb))
    return pl.BlockSpec((tm, width), lambda i: (n - 1 - i, cb))


def _const_spec(shape):
    nd = len(shape)
    return pl.BlockSpec(shape, lambda i: (0,) * nd)


def seq_fwd(name, f, params, tiles, carries, outs, tm):
    rows = tiles[0][0].shape[0]
    n = rows // tm
    np_, nt, no, nc = len(params), len(tiles), len(outs), len(carries)

    def body(*refs):
        p_refs = refs[:np_]
        t_refs = refs[np_:np_ + nt]
        o_refs = refs[np_ + nt:np_ + nt + no]
        s_refs = refs[np_ + nt + no:np_ + nt + no + nc]
        c_refs = refs[np_ + nt + no + nc:]
        i = pl.program_id(0)

        @pl.when(i == 0)
        def _():
            for c in c_refs:
                c[...] = jnp.zeros_like(c)

        cvals = [c[...] for c in c_refs]
        for s, c in zip(s_refs, cvals):
            s[0] = c
        o, newc = f(i, [r[...] for r in p_refs], cvals, [r[...].astype(f32) for r in t_refs])
        for r, v in zip(o_refs, o):
            r[...] = v.astype(r.dtype)
        for r, v in zip(c_refs, newc):
            r[...] = v

    in_specs = [_const_spec(p.shape) for p in params] + [_tile_spec(tm, w, cb) for (_, w, cb) in tiles]
    out_specs = [_tile_spec(tm, w, 0) for (w, _) in outs]
    out_specs += [pl.BlockSpec((1,) + tuple(c), lambda i, nd=len(c): (i,) + (0,) * nd) for c in carries]
    out_shape = [jax.ShapeDtypeStruct((rows, w), dt) for (w, dt) in outs]
    out_shape += [jax.ShapeDtypeStruct((n,) + tuple(c), f32) for c in carries]
    res = pl.pallas_call(
        body, name=name, grid=(n,), in_specs=in_specs, out_specs=out_specs, out_shape=out_shape,
        scratch_shapes=[pltpu.VMEM(tuple(c), f32) for c in carries],
        compiler_params=_cp(("arbitrary",)),
    )(*params, *[t[0] for t in tiles])
    return list(res[:no]), list(res[no:])


def seq_bwd(name, f, params, tiles, diff, saved, douts, gdtypes, tm):
    rows = tiles[0][0].shape[0]
    n = rows // tm
    np_, nt, nc, nd = len(params), len(tiles), len(saved), len(douts)
    didx = [k for k, d in enumerate(diff) if d]
    ng = len(didx)

    def body(*refs):
        p_refs = refs[:np_]
        t_refs = refs[np_:np_ + nt]
        s_refs = refs[np_ + nt:np_ + nt + nc]
        d_refs = refs[np_ + nt + nc:np_ + nt + nc + nd]
        pos = np_ + nt + nc + nd
        dp_refs = refs[pos:pos + np_]
        dt_refs = refs[pos + np_:pos + np_ + ng]
        dc_refs = refs[pos + np_ + ng:]
        i = pl.program_id(0)
        step = n - 1 - i

        @pl.when(i == 0)
        def _():
            for r in dp_refs:
                r[...] = jnp.zeros_like(r)
            for r in dc_refs:
                r[...] = jnp.zeros_like(r)

        pvals = [r[...] for r in p_refs]
        cvals = [r[0] for r in s_refs]
        xvals = [r[...].astype(f32) for r in t_refs]

        def fn(p, c, xd):
            x = list(xvals)
            for k, v in zip(didx, xd):
                x[k] = v
            return f(step, p, c, x)

        _, vjp = jax.vjp(fn, pvals, cvals, [xvals[k] for k in didx])
        dp, dc, dx = vjp(([r[...].astype(f32) for r in d_refs], [r[...] for r in dc_refs]))
        for r, v in zip(dp_refs, dp):
            r[...] += v
        for r, v in zip(dc_refs, dc):
            r[...] = v
        for r, v in zip(dt_refs, dx):
            r[...] = v.astype(r.dtype)

    in_specs = [_const_spec(p.shape) for p in params] + [_tile_spec(tm, w, cb, n) for (_, w, cb) in tiles]
    in_specs += [pl.BlockSpec((1,) + tuple(s.shape[1:]), lambda i, nd_=s.ndim - 1: (n - 1 - i,) + (0,) * nd_) for s in saved]
    in_specs += [_tile_spec(tm, d.shape[1], 0, n) for d in douts]
    args = list(params) + [t[0] for t in tiles] + list(saved) + list(douts)
    out_specs = [_const_spec(p.shape) for p in params] + [_tile_spec(tm, tiles[k][1], 0, n) for k in didx]
    out_shape = [jax.ShapeDtypeStruct(p.shape, f32) for p in params]
    out_shape += [jax.ShapeDtypeStruct((rows, tiles[k][1]), dt) for k, dt in zip(didx, gdtypes)]
    res = pl.pallas_call(
        body, name=name, grid=(n,), in_specs=in_specs, out_specs=out_specs, out_shape=out_shape,
        scratch_shapes=[pltpu.VMEM(tuple(s.shape[1:]), f32) for s in saved],
        compiler_params=_cp(("arbitrary",)),
    )(*args)
    return list(res[:np_]), list(res[np_:])


def _halvings(dim, lo, hi):
    t, out = _pick(dim, hi), []
    while t >= min(lo, dim) and dim % t == 0:
        out.append(t)
        if t % 2 or (t // 2) % 8:
            break
        t //= 2
    return out


def _matmul_tiles(m, n, k, a_item, b_item, per_out, max_tn=1024, whole_rows=False):
    def vmem_bytes(tm, tn, tk):
        acc = 4 if k // tk > 1 else 0
        return 2 * (tm * tk * a_item + tk * tn * b_item + tm * tn * per_out) + tm * tn * acc

    def traffic(tm, tn, tk):
        nk = k // tk
        return (m * k * a_item * (1 if nk == 1 else n // tn) + k * n * b_item * (m // tm)
                + (nk - 1) * m * n * 8 * MATMUL_ACC_PASS_WEIGHT)

    cands = [(traffic(tm, tn, tk), -tm * tn, tm, tn, tk)
             for tk in _halvings(k, 512, MATMUL_MAX_K_TILE) for tm in _halvings(m, 256, 4096)
             for tn in ([n] if whole_rows else _halvings(n, 512, min(1024, max_tn)))
             if vmem_bytes(tm, tn, tk) <= MATMUL_VMEM_BUDGET]
    return min(cands)[2:]


def matmul(name, a, b, *, ta=False, tb=False, outs=(f32,), epi=None, extras=(), rows=(), row_sums=0, deps=(),
           out_blocks=0):
    m, k = (a.shape[1], a.shape[0]) if ta else a.shape
    n = b.shape[0] if tb else b.shape[1]
    per_out = sum(jnp.dtype(dt).itemsize for dt in outs) + sum(e[0].dtype.itemsize for e in extras)
    whole_rows = bool(rows) or row_sums > 0
    tm, tn, tk = _matmul_tiles(m, n, k, a.dtype.itemsize, b.dtype.itemsize, per_out,
                               n // out_blocks if out_blocks else n, whole_rows)
    nk = k // tk
    ne, nr, nd, no = len(extras), len(rows), len(deps), len(outs)
    dims = (((0 if ta else 1,), (1 if tb else 0,)), ((), ()))

    def body(*refs):
        a_ref, b_ref = refs[0], refs[1]
        e_refs = refs[2:2 + ne]
        r_refs = refs[2 + ne:2 + ne + nr]
        o_refs = refs[2 + ne + nr + nd:2 + ne + nr + nd + no]
        s_refs = refs[2 + ne + nr + nd + no:2 + ne + nr + nd + no + row_sums]
        i, kk = pl.program_id(0), pl.program_id(2)
        part = lax.dot_general(a_ref[...].astype(bf16), b_ref[...].astype(bf16), dims, preferred_element_type=f32)

        def finish(total):
            res = (total,) if epi is None else epi(total, *[e[...] for e in e_refs], *[r[...] for r in r_refs])
            for r, v in zip(o_refs, res[:no]):
                r[...] = v.astype(r.dtype)
            for r, v in zip(s_refs, res[no:]):
                v8 = jnp.broadcast_to(v, r.shape)

                @pl.when(i == 0)
                def _(r=r, v8=v8):
                    r[...] = v8

                @pl.when(i > 0)
                def _(r=r, v8=v8):
                    r[...] += v8

        if nk == 1:
            finish(part)
            return
        acc = refs[-1]

        @pl.when(kk == 0)
        def _():
            acc[...] = part

        @pl.when(jnp.logical_and(kk > 0, kk < nk - 1))
        def _():
            acc[...] += part

        @pl.when(kk == nk - 1)
        def _():
            finish(acc[...] + part)

    a_spec = pl.BlockSpec((tk, tm), lambda i, j, q: (q, i)) if ta else pl.BlockSpec((tm, tk), lambda i, j, q: (i, q))
    b_spec = pl.BlockSpec((tn, tk), lambda i, j, q: (j, q)) if tb else pl.BlockSpec((tk, tn), lambda i, j, q: (q, j))
    assert all(off % tn == 0 for (_, off) in extras)
    e_specs = [pl.BlockSpec((tm, tn), lambda i, j, q, off=off // tn: (i, off + j)) for (_, off) in extras]
    r_specs = [pl.BlockSpec((1, tn), lambda i, j, q: (0, j)) for _ in rows]
    if out_blocks:
        per = n // out_blocks // tn
        out_spec = pl.BlockSpec((None, tm, tn), lambda i, j, q: (j // per, i, j % per))
        out_dims = (out_blocks, m, n // out_blocks)
    else:
        out_spec = pl.BlockSpec((tm, tn), lambda i, j, q: (i, j))
        out_dims = (m, n)
    res = pl.pallas_call(
        body, name=name, grid=(m // tm, n // tn, nk),
        in_specs=[a_spec, b_spec] + e_specs + r_specs + [pl.BlockSpec(memory_space=pl.ANY) for _ in deps],
        out_specs=[out_spec for _ in outs] + [pl.BlockSpec((8, tn), lambda i, j, q: (0, j))] * row_sums,
        out_shape=[jax.ShapeDtypeStruct(out_dims, dt) for dt in outs] + [jax.ShapeDtypeStruct((8, n), f32)] * row_sums,
        scratch_shapes=[pltpu.VMEM((tm, tn), f32)] if nk > 1 else [],
        compiler_params=_cp(("arbitrary" if row_sums else "parallel", "parallel", "arbitrary")),
    )(a, b, *[e[0] for e in extras], *rows, *deps)
    return res[0] if len(res) == 1 else tuple(res)


def merge_fwd(name, ys, wbs, u):
    rows, n_out = ys[0].shape[0], wbs[0].shape[1]
    nb = len(ys)
    tm, tn = _pick(rows, 512), _pick(n_out, 512)

    def body(*refs):
        y_refs, w_refs, g_refs = refs[:nb], refs[nb:2 * nb], refs[2 * nb:3 * nb]
        m_ref, p_refs = refs[3 * nb], refs[3 * nb + 1:]
        total = None
        for y_ref, w_ref, g_ref, p_ref in zip(y_refs, w_refs, g_refs, p_refs):
            pre = jnp.dot(y_ref[...], w_ref[...], preferred_element_type=f32)
            p_ref[...] = pre.astype(p_ref.dtype)
            term = jax.nn.sigmoid(g_ref[...].astype(f32)) * pre
            total = term if total is None else total + term
        m_ref[...] = total.astype(m_ref.dtype)

    in_specs = [pl.BlockSpec((tm, y.shape[1]), lambda i, j: (i, 0)) for y in ys]
    in_specs += [pl.BlockSpec((w.shape[0], tn), lambda i, j: (0, j)) for w in wbs]
    in_specs += [pl.BlockSpec((tm, tn), lambda i, j, off=n * (n_out // tn): (i, off + j)) for n in range(nb)]
    out_spec = pl.BlockSpec((tm, tn), lambda i, j: (i, j))
    res = pl.pallas_call(
        body, name=name, grid=(rows // tm, n_out // tn), in_specs=in_specs, out_specs=[out_spec] * (nb + 1),
        out_shape=[jax.ShapeDtypeStruct((rows, n_out), bf16)] * (nb + 1),
        compiler_params=_cp(("parallel", "parallel")),
    )(*ys, *wbs, *([u] * nb))
    return res[0], list(res[1:])


ATT_SCALE = (QK_NOPE + QK_ROPE) ** -0.5
LN2 = 0.6931471805599453
ATT_C = ATT_SCALE / LN2
NT = (((1,), (1,)), ((), ()))
TN = (((0,), (0,)), ((), ()))


def _causal(tq, tk):
    return lax.broadcasted_iota(jnp.int32, (tq, tk), 0) >= lax.broadcasted_iota(jnp.int32, (tq, tk), 1)


def _tri_pairs(n, by_column):
    if by_column:
        pairs = [(i, j) for j in range(n) for i in range(j, n)]
    else:
        pairs = [(i, j) for i in range(n) for j in range(i + 1)]
    return (jnp.asarray([a for a, _ in pairs], jnp.int32), jnp.asarray([b for _, b in pairs], jnp.int32))


FWD_HEADS_PER_STEP = 8
HEADS_PER_STEP = 4
HEAD_PAIR = HEADS_PER_STEP * LANE


def attn_fwd(q, k, v, t):
    rows = q.shape[0]
    n = rows // t
    it, jt = _tri_pairs(n, False)

    def body(it_ref, jt_ref, q_ref, k_ref, v_ref, o_ref, lse_ref, m_s, l_s, acc_s):
        s_id = pl.program_id(1)
        i, j = it_ref[s_id], jt_ref[s_id]

        @pl.when(j == 0)
        def _():
            m_s[...] = jnp.full_like(m_s, -jnp.inf)
            l_s[...] = jnp.zeros_like(l_s)
            acc_s[...] = jnp.zeros_like(acc_s)

        def step(diag):
            for hh in range(FWD_HEADS_PER_STEP):
                sl = slice(LANE * hh, LANE * (hh + 1))
                s = lax.dot_general(q_ref[:, sl], k_ref[:, sl], NT, preferred_element_type=f32)
                if diag:
                    s = jnp.where(_causal(t, t), s, -jnp.inf)
                m_prev = m_s[:, sl]
                m_new = jnp.maximum(m_prev, jnp.max(s, axis=1, keepdims=True))
                alpha = jnp.exp2(m_prev - m_new)
                p = jnp.exp2(s - m_new[:, :1])
                l_s[:, sl] = alpha * l_s[:, sl] + jnp.sum(p, axis=1, keepdims=True)
                acc_s[:, sl] = alpha * acc_s[:, sl] + jnp.dot(p.astype(bf16), v_ref[:, sl], preferred_element_type=f32)
                m_s[:, sl] = m_new

        pl.when(j < i)(lambda: step(False))

        @pl.when(j == i)
        def _():
            step(True)
            o_ref[...] = (acc_s[...] / l_s[...]).astype(o_ref.dtype)
            lse_ref[...] = m_s[...] + jnp.log2(l_s[...])

    width = FWD_HEADS_PER_STEP * LANE
    qs = pl.BlockSpec((t, width), lambda h, s, it_, jt_: (it_[s], h))
    ks = pl.BlockSpec((t, width), lambda h, s, it_, jt_: (jt_[s], h))
    hw = N_HEADS * LANE
    return pl.pallas_call(
        body, name="attn_fwd",
        grid_spec=pltpu.PrefetchScalarGridSpec(
            num_scalar_prefetch=2, grid=(hw // width, it.shape[0]), in_specs=[qs, ks, ks], out_specs=[qs, qs],
            scratch_shapes=[pltpu.VMEM((t, width), f32)] * 3),
        out_shape=[jax.ShapeDtypeStruct((rows, hw), bf16), jax.ShapeDtypeStruct((rows, hw), f32)],
        compiler_params=_cp(("parallel", "arbitrary")),
    )(it, jt, q, k, v)


def attn_bwd(q, k, v, do, o, lse, t):
    rows = q.shape[0]
    n = rows // t
    it, jt = _tri_pairs(n, True)

    def body(it_ref, jt_ref, q_ref, k_ref, v_ref, do_ref, o_ref, lse_ref, dq_ref, dk_ref, dv_ref, dk_s, dv_s):
        s_id = pl.program_id(1)
        i, j = it_ref[s_id], jt_ref[s_id]

        @pl.when(s_id == 0)
        def _():
            dq_ref[...] = jnp.zeros_like(dq_ref)

        @pl.when(i == j)
        def _():
            dk_s[...] = jnp.zeros_like(dk_s)
            dv_s[...] = jnp.zeros_like(dv_s)

        q_rows = pl.ds(pl.multiple_of(i * t, t), t)

        def step(diag):
            for hh in range(HEADS_PER_STEP):
                sl = slice(LANE * hh, LANE * (hh + 1))
                qh, kh, vh, doh = q_ref[:, sl], k_ref[:, sl], v_ref[:, sl], do_ref[:, sl]
                s = lax.dot_general(qh, kh, NT, preferred_element_type=f32)
                p = jnp.exp2(s - lse_ref[:, sl][:, :1])
                if diag:
                    p = jnp.where(_causal(t, t), p, 0.0)
                dp = lax.dot_general(doh, vh, NT, preferred_element_type=f32)
                delta = jnp.sum(doh.astype(f32) * o_ref[:, sl].astype(f32), axis=1, keepdims=True)
                ds = (p * (dp - delta) * LN2).astype(bf16)
                dv_s[:, sl] += lax.dot_general(p.astype(bf16), doh, TN, preferred_element_type=f32)
                dk_s[:, sl] += lax.dot_general(ds, qh, TN, preferred_element_type=f32)
                dq_ref[q_rows, sl] += jnp.dot(ds, kh, preferred_element_type=f32)

        pl.when(i > j)(lambda: step(False))
        pl.when(i == j)(lambda: step(True))

        @pl.when(i == n - 1)
        def _():
            dk_ref[...] = dk_s[...]
            dv_ref[...] = dv_s[...]

    qs = pl.BlockSpec((t, HEAD_PAIR), lambda h, s, it_, jt_: (it_[s], h))
    ks = pl.BlockSpec((t, HEAD_PAIR), lambda h, s, it_, jt_: (jt_[s], h))
    dqs = pl.BlockSpec((rows, HEAD_PAIR), lambda h, s, it_, jt_: (0, h))
    hw = N_HEADS * LANE
    return pl.pallas_call(
        body, name="attn_bwd",
        grid_spec=pltpu.PrefetchScalarGridSpec(
            num_scalar_prefetch=2, grid=(hw // HEAD_PAIR, it.shape[0]), in_specs=[qs, ks, ks, qs, qs, qs],
            out_specs=[dqs, ks, ks], scratch_shapes=[pltpu.VMEM((t, HEAD_PAIR), f32)] * 2),
        out_shape=[jax.ShapeDtypeStruct((rows, hw), f32)] * 3,
        compiler_params=_cp(("parallel", "arbitrary")),
    )(it, jt, q, k, v, do, o, lse)


def _steps(tm):
    k, out = 1, []
    while k < tm:
        out.append(k)
        k *= 2
    return out


def scan_fwd(a, u, tm):
    rows, ch = a.shape
    n = rows // tm

    def body(a_ref, u_ref, h_ref, h_s):
        @pl.when(pl.program_id(0) == 0)
        def _():
            h_s[...] = jnp.zeros_like(h_s)

        av, bv = a_ref[...], u_ref[...]
        row = lax.broadcasted_iota(jnp.int32, av.shape, 0)
        for k in _steps(tm):
            a_sh = jnp.where(row >= k, pltpu.roll(av, k, 0), 1.0)
            b_sh = jnp.where(row >= k, pltpu.roll(bv, k, 0), 0.0)
            bv = av * b_sh + bv
            av = av * a_sh
        h = bv + av * h_s[HALO - 1:HALO, :]
        h_ref[...] = h
        h_s[...] = h[tm - HALO:, :]

    spec = pl.BlockSpec((tm, ch), lambda i: (i, 0))
    return pl.pallas_call(
        body, name="lru_scan_fwd", grid=(n,), in_specs=[spec, spec], out_specs=spec,
        out_shape=jax.ShapeDtypeStruct((rows, ch), f32), scratch_shapes=[pltpu.VMEM((HALO, ch), f32)],
        compiler_params=_cp(("arbitrary",)),
    )(a, u)


def scan_bwd(a, h, dh, tm):
    rows, ch = a.shape
    n = rows // tm
    per = tm // HALO

    def body(a_ref, h_ref, hp_ref, dh_ref, da_ref, du_ref, g_s, a_s):
        i = pl.program_id(0)
        step = n - 1 - i

        @pl.when(i == 0)
        def _():
            g_s[...] = jnp.zeros_like(g_s)
            a_s[...] = jnp.zeros_like(a_s)

        a0 = a_ref[...]
        row = lax.broadcasted_iota(jnp.int32, a0.shape, 0)
        av = jnp.where(row < tm - 1, pltpu.roll(a0, tm - 1, 0), a_s[0:1, :])
        bv = dh_ref[...]
        for k in _steps(tm):
            a_sh = jnp.where(row < tm - k, pltpu.roll(av, tm - k, 0), 1.0)
            b_sh = jnp.where(row < tm - k, pltpu.roll(bv, tm - k, 0), 0.0)
            bv = bv + av * b_sh
            av = av * a_sh
        g = bv + av * g_s[0:1, :]
        h_last = jnp.where(step > 0, hp_ref[HALO - 1:HALO, :], 0.0)
        h_prev = jnp.where(row >= 1, pltpu.roll(h_ref[...], 1, 0), h_last)
        du_ref[...] = g
        da_ref[...] = g * h_prev
        g_s[...] = g[0:HALO, :]
        a_s[...] = a0[0:HALO, :]

    spec = pl.BlockSpec((tm, ch), lambda i: (n - 1 - i, 0))
    hp_spec = pl.BlockSpec((HALO, ch), lambda i: (jnp.maximum((n - 1 - i) * per - 1, 0), 0))
    return pl.pallas_call(
        body, name="lru_scan_bwd", grid=(n,), in_specs=[spec, spec, hp_spec, spec], out_specs=[spec, spec],
        out_shape=[jax.ShapeDtypeStruct((rows, ch), f32)] * 2,
        scratch_shapes=[pltpu.VMEM((HALO, ch), f32)] * 2,
        compiler_params=_cp(("arbitrary",)),
    )(a, h, h, dh)


def _rms(x, g):
    return x * lax.rsqrt(jnp.mean(x * x, axis=-1, keepdims=True) + EPS) * g


def f_rms(step, p, c, x):
    return [_rms(x[0], p[0])], []


def _rope(x, cosf, sinf):
    lane = lax.broadcasted_iota(jnp.int32, x.shape, 1)
    sw = jnp.where(lane < KR_LANE + QK_ROPE // 2, lane_roll(x, LANE - QK_ROPE // 2), lane_roll(x, QK_ROPE // 2))
    return x * cosf + sw * sinf


def f_prep(step, p, c, x):
    q, kn, kr, cosf, sinf = x
    kr_rot = _rope(kr, cosf, sinf)
    qr = [_rope(q[:, LANE * h:LANE * (h + 1)], cosf, sinf) * ATT_C for h in range(N_HEADS)]
    kk = [kn[:, LANE * h:LANE * (h + 1)] + kr_rot for h in range(N_HEADS)]
    return [jnp.concatenate(qr, axis=1), jnp.concatenate(kk, axis=1)], []


def _conv(tail, x, w, b):
    xf = jnp.concatenate([tail, x], axis=0)
    acc = b + w[CONV_W - 1:CONV_W, :] * xf
    for k in range(CONV_W - 1):
        acc = acc + w[k:k + 1, :] * shift_down(xf, CONV_W - 1 - k)
    return acc[HALO:, :]


def f_pool(step, p, c, x):
    wp, sc = p
    (tail,) = c
    (u,) = x
    tm = u.shape[0]
    xf = jnp.concatenate([tail, u], axis=0)
    sums, s, w = [], xf, 1
    while w < POOL_WINDOWS[-1]:
        s = s + shift_down(s, w)
        w *= 2
        sums.append(s)
    t = step * tm + lax.broadcasted_iota(jnp.int32, (tm, 1), 0)
    ys = []
    for g, (w, s) in enumerate(zip(POOL_WINDOWS, sums)):
        sl = slice(LANE * g, LANE * (g + 1))
        cnt = jnp.minimum(t + 1, w).astype(f32)
        d = s[POOL_HALO:, sl] / cnt - u[:, sl]
        ys.append(jnp.dot(d.astype(bf16), wp[LANE * g:LANE * (g + 1), :].astype(bf16), preferred_element_type=f32))
    return [jnp.concatenate(ys, axis=1) * sc], [u[tm - POOL_HALO:, :]]


def f_ssd(step, p, c, x):
    conv_w, conv_b, dtb, alog, dsk, ng = p
    tail, s_in = c[0], c[1:]
    z, xbc, dt = x
    ln = z.shape[0]
    xc = jax.nn.silu(_conv(tail, xbc, conv_w, conv_b))
    xs, bb, cc = xc[:, :MIX], xc[:, MIX:MIX + LANE], xc[:, MIX + LANE:]
    dtv = jax.nn.softplus(dt + dtb[0:1, :])
    a = dtv * -jnp.exp(alog[0:1, :])
    ri = lax.broadcasted_iota(jnp.int32, (ln, ln), 0)
    ci = lax.broadcasted_iota(jnp.int32, (ln, ln), 1)
    tril = (ri >= ci).astype(f32)
    triu = (ri <= ci).astype(f32)
    hi = lax.Precision.HIGHEST
    a_cs = jnp.dot(tril, a, precision=hi, preferred_element_type=f32)
    a_cs_t = lax.dot_general(a, triu, TN, precision=hi, preferred_element_type=f32)
    a_tot = jnp.sum(a, axis=0, keepdims=True)
    lane = lax.broadcasted_iota(jnp.int32, (1, LANE), 1)
    half = [(lane < 64).astype(f32), (lane >= 64).astype(f32)]
    hrow = lax.broadcasted_iota(jnp.int32, (LANE, 1), 0)

    def head(v, h):
        return jnp.sum(v * (lane == h).astype(f32), axis=1, keepdims=True)

    def pair(v, j):
        return head(v, 2 * j) * half[0] + head(v, 2 * j + 1) * half[1]

    cg = [(cc * half[g]).astype(bf16) for g in range(2)]
    bg = [(bb * half[g]).astype(bf16) for g in range(2)]
    cb = [lax.dot_general(cg[g], bg[g], NT, preferred_element_type=f32) for g in range(2)]
    ys, s_out = [], []
    for j in range(4):
        g = j // 2
        xs_j = xs[:, LANE * j:LANE * (j + 1)]
        xj = xs_j * pair(dtv, j)
        yj = xs_j * pair(dsk[0:1, :], j)
        for hh in range(2):
            h = 2 * j + hh
            rowv = jnp.sum(a_cs_t * (hrow == h).astype(f32), axis=0, keepdims=True)
            lmat = jnp.exp(jnp.where(ri >= ci, head(a_cs, h) - rowv, -jnp.inf))
            yj = yj + jnp.dot((cb[g] * lmat).astype(bf16), (xj * half[hh]).astype(bf16), preferred_element_type=f32)
        acs = pair(a_cs, j)
        tot = pair(a_tot, j)
        yj = yj + jnp.exp(acs) * jnp.dot(cg[g], s_in[j].astype(bf16), preferred_element_type=f32)
        s_new = jnp.exp(tot) * s_in[j] + lax.dot_general(bg[g], (xj * jnp.exp(tot - acs)).astype(bf16), TN,
                                                         preferred_element_type=f32)
        ys.append(yj)
        s_out.append(s_new)
    y = jnp.concatenate(ys, axis=1) * jax.nn.silu(z)
    return [_rms(y, ng)], [xbc[ln - HALO:, :]] + s_out


def _neg_expm1(y):
    series = -y * (1.0 + y * (0.5 + y * (1.0 / 6 + y * (1.0 / 24 + y * (1.0 / 120)))))
    return jnp.where(y > -0.05, series, 1.0 - jnp.exp(y))


def f_lru_pre(step, p, c, x):
    cw, cb_, wa, ba, wi, bi, lam = p
    (tail,) = c
    (lx,) = x
    tm = lx.shape[0]
    xc = _conv(tail, lx, cw, cb_)
    xb = xc.astype(bf16)
    r = jax.nn.sigmoid(jnp.dot(xb, wa.astype(bf16), preferred_element_type=f32) + ba)
    it = jax.nn.sigmoid(jnp.dot(xb, wi.astype(bf16), preferred_element_type=f32) + bi)
    log_a = -LRU_C * r * jax.nn.softplus(-lam)
    mult = jnp.sqrt(_neg_expm1(2.0 * log_a))
    return [jnp.exp(log_a), xc * it * mult], [lx[tm - HALO:, :]]


def f_lru_post(step, p, c, x):
    h, g = x
    return [h * jax.nn.gelu(g)], []


def loss_head(x, tgt, g, tm):
    rows, d = x.shape
    n = rows // tm

    def body(x_ref, t_ref, g_ref, loss_ref, dx_ref, dg_ref):
        @pl.when(pl.program_id(0) == 0)
        def _():
            loss_ref[...] = jnp.zeros_like(loss_ref)
            dg_ref[...] = jnp.zeros_like(dg_ref)

        def fn(gv, xv):
            err = _rms(xv, gv) - t_ref[...]
            return 0.5 * jnp.sum(jnp.mean(err * err, axis=-1, keepdims=True))

        val, (dg, dx) = jax.value_and_grad(fn, argnums=(0, 1))(g_ref[...], x_ref[...])
        loss_ref[...] += val
        dg_ref[...] += dg
        dx_ref[...] = dx

    spec = pl.BlockSpec((tm, d), lambda i: (i, 0))
    return pl.pallas_call(
        body, name="loss_head", grid=(n,), in_specs=[spec, spec, _const_spec((1, d))],
        out_specs=[_const_spec((8, LANE)), spec, _const_spec((1, d))],
        out_shape=[jax.ShapeDtypeStruct((8, LANE), f32), jax.ShapeDtypeStruct((rows, d), f32),
                   jax.ShapeDtypeStruct((1, d), f32)],
        compiler_params=_cp(("arbitrary",)),
    )(x, tgt, g)


def ew(name, fn, ins, outs, tm):
    rows = ins[0][0].shape[0]
    ni = len(ins)

    def body(*refs):
        res = fn(*[r[...].astype(f32) for r in refs[:ni]])
        for r, v in zip(refs[ni:], res):
            r[...] = v.astype(r.dtype)

    return pl.pallas_call(
        body, name=name, grid=(rows // tm,), in_specs=[_tile_spec(tm, w, cb) for (_, w, cb) in ins],
        out_specs=[_tile_spec(tm, w, 0) for (w, _) in outs],
        out_shape=[jax.ShapeDtypeStruct((rows, w), dt) for (w, dt) in outs],
        compiler_params=_cp(("parallel",)),
    )(*[t[0] for t in ins])


def _peers():
    x, y, c = lax.axis_index("x"), lax.axis_index("y"), lax.axis_index("c")
    me = 4 * x + 2 * y + c
    out = []
    for k in range(1, N_DEV):
        px = 1 - x if k & 4 else x
        py = 1 - y if k & 2 else y
        pc = 1 - c if k & 1 else c
        out.append(((px, py, pc), 4 * px + 2 * py + pc))
    return me, out


_HBM = pl.BlockSpec(memory_space=pltpu.HBM)
_SEM = pl.BlockSpec(memory_space=pltpu.SEMAPHORE)
_EFFECT = pltpu.SideEffectType.DATAFLOW_SIDE_EFFECTING


def _remote(src_ref, land_ref, gather, me, pid, dev, send_sems, recv_sems, k, recv_side):
    return pltpu.make_async_remote_copy(
        src_ref=src_ref if gather else src_ref.at[pid], dst_ref=land_ref.at[pid if recv_side else me],
        send_sem=send_sems.at[k], recv_sem=recv_sems.at[k], device_id=dev, device_id_type=pl.DeviceIdType.MESH)


def _own(src_ref, land_ref, gather, me, sem):
    return pltpu.make_async_copy(src_ref if gather else src_ref.at[me], land_ref.at[me], sem)


def exchange_start(name, srcs, gather, deps=()):
    n, nd = len(srcs), len(deps)
    shapes = [(s.shape if gather else s.shape[1:]) for s in srcs]
    lands = [lax.empty((N_DEV,) + tuple(sh), s.dtype) for s, sh in zip(srcs, shapes)]

    def body(*refs):
        src_refs, land_refs = refs[:n], refs[n:2 * n]
        send_sems, recv_sems, own_sem = refs[2 * n + nd:2 * n + nd + 3]
        token = refs[-1]
        me, peers = _peers()
        for k, (dev, pid) in enumerate(peers):
            for s_ref, l_ref in zip(src_refs, land_refs):
                _remote(s_ref, l_ref, gather, me, pid, dev, send_sems, recv_sems, k, False).start()
        for s_ref, l_ref in zip(src_refs, land_refs):
            _own(s_ref, l_ref, gather, me, own_sem).start()
        token[...] = jnp.zeros_like(token)

    hbm = lambda a: pltpu.with_memory_space_constraint(a, pltpu.HBM)
    res = pl.pallas_call(
        body, name=name,
        out_shape=(pltpu.SemaphoreType.DMA((N_DEV - 1,)), pltpu.SemaphoreType.DMA((N_DEV - 1,)), pltpu.SemaphoreType.DMA(()),
                   *[pltpu.HBM(a.shape, a.dtype) for a in list(srcs) + lands], jax.ShapeDtypeStruct((8, LANE), f32)),
        in_specs=[_HBM] * (2 * n) + [pl.BlockSpec(memory_space=pl.ANY)] * nd,
        out_specs=(_SEM, _SEM, _SEM, *([_HBM] * (2 * n)), pl.BlockSpec(memory_space=pltpu.VMEM)),
        input_output_aliases={i: 3 + i for i in range(2 * n)},
        compiler_params=pltpu.CompilerParams(has_side_effects=_EFFECT),
    )(*[hbm(a) for a in list(srcs) + lands], *deps)
    return dict(sems=res[:3], srcs=list(res[3:3 + n]), lands=list(res[3 + n:3 + 2 * n]), token=res[-1], gather=gather)


def exchange_wait(name, h, afters):
    n, gather = len(h["srcs"]), h["gather"]

    def body(*refs):
        src_refs, land_refs = refs[:n], refs[n:2 * n]
        send_sems, recv_sems, own_sem = refs[2 * n:2 * n + 3]
        me, peers = _peers()
        for k, (dev, pid) in enumerate(peers):
            for s_ref, l_ref in zip(src_refs, land_refs):
                _remote(s_ref, l_ref, gather, me, pid, dev, send_sems, recv_sems, k, True).wait_recv()
        for k, (dev, pid) in enumerate(peers):
            for s_ref, l_ref in zip(src_refs, land_refs):
                _remote(s_ref, l_ref, gather, me, pid, dev, send_sems, recv_sems, k, False).wait_send()
        for s_ref, l_ref in zip(src_refs, land_refs):
            _own(s_ref, l_ref, gather, me, own_sem).wait()

    arrs = h["srcs"] + h["lands"]
    res = pl.pallas_call(
        body, name=name, out_shape=tuple(pltpu.HBM(a.shape, a.dtype) for a in arrs),
        in_specs=[_HBM] * (2 * n) + [_SEM, _SEM, _SEM] + [pl.BlockSpec(memory_space=pl.ANY)] * len(afters),
        out_specs=tuple([_HBM] * (2 * n)), input_output_aliases={i: i for i in range(2 * n)},
        compiler_params=pltpu.CompilerParams(has_side_effects=_EFFECT),
    )(*arrs, *h["sems"], *afters)
    return list(res[n:])


def adamw(name, parts, w, m, v):
    nl = len(parts)
    shape = w.shape[1:]
    c = shape[-1]
    r = 1
    for s in shape[:-1]:
        r *= s
    tr = _pick(r, 256) if r % 8 == 0 else r
    nb = r // tr
    parts2 = [p.reshape(N_DEV, r, c) for p in parts]
    w2, m2, v2 = (a.reshape(nl, r, c) for a in (w, m, v))

    def body(*refs):
        p_refs = refs[:nl]
        w_ref, m_ref, v_ref, g_ref, d_ref, nm_ref, nv_ref = refs[nl:]
        layer = pl.program_id(0)
        for ll, p_ref in enumerate(p_refs):
            @pl.when(layer == ll)
            def _(p_ref=p_ref):
                g = p_ref[0].astype(f32)
                for i in range(1, N_DEV):
                    g = g + p_ref[i].astype(f32)
                mn = ADAM_B1 * m_ref[0] + (1.0 - ADAM_B1) * g
                vn = ADAM_B2 * v_ref[0] + (1.0 - ADAM_B2) * jnp.square(g)
                m_hat = mn / (1.0 - ADAM_B1 ** ADAM_STEP)
                v_hat = vn / (1.0 - ADAM_B2 ** ADAM_STEP)
                g_ref[0] = g
                d_ref[0] = -ADAM_LR * (m_hat / (jnp.sqrt(v_hat) + ADAM_EPS) + ADAM_WD * w_ref[0])
                nm_ref[0] = mn
                nv_ref[0] = vn

    def p_spec(ll):
        return pl.BlockSpec((N_DEV, tr, c), lambda l, i: (0, jnp.where(l == ll, i, jnp.where(l > ll, nb - 1, 0)), 0))

    spec = pl.BlockSpec((1, tr, c), lambda l, i: (l, i, 0))
    res = pl.pallas_call(
        body, name=name, grid=(nl, nb), in_specs=[p_spec(ll) for ll in range(nl)] + [spec, spec, spec],
        out_specs=[spec] * 4, out_shape=[jax.ShapeDtypeStruct((nl, r, c), f32)] * 4,
        compiler_params=_cp(("arbitrary", "arbitrary")),
    )(*parts2, w2, m2, v2)
    return [a.reshape(w.shape) for a in res]


_IN_SPLITS = dict(cq=(0, 384), ckv=(384, 640), kr=(640, 672), pool=(672, 1184), z=(1184, 1696), xbc=(1696, 2464),
                  dt=(2464, 2472), lg=(2472, 2984), lx=(2984, 3496), gates=(3496, 7592))


W_IN_SHARD = IN_COLS // N_DEV

_PAD_ORDER = ("gates", "pool", "z", "lg", "lx", "xbc", "cq", KR_LANE, "kr", LANE - KR_LANE - QK_ROPE, "ckv", "dt",
              LANE - 8, U_COLS - U_DT[0] - LANE)
_SEGMENTS = ((0, U_CQ[0], 384), (384, U_CKV[0], 256), (640, U_KR[0] + KR_LANE, QK_ROPE), (672, U_POOL[0], 512),
             (1184, U_Z[0], 512), (1696, U_XBC[0], 768), (2464, U_DT[0], 8), (2472, U_LG[0], 512), (2984, U_LX[0], 512),
             (3496, 0, 4096))


def _pad_w_in(shards):
    rows = shards.shape[1]
    pieces = []
    for item in _PAD_ORDER:
        if isinstance(item, int):
            pieces.append(jnp.zeros((rows, item), shards.dtype))
            continue
        a, b = _IN_SPLITS[item]
        for d in range(a // W_IN_SHARD, (b - 1) // W_IN_SHARD + 1):
            lo, hi = max(a, d * W_IN_SHARD), min(b, (d + 1) * W_IN_SHARD)
            pieces.append(shards[d, :, lo - d * W_IN_SHARD:hi - d * W_IN_SHARD])
    return jnp.concatenate(pieces, axis=1)


def _w_in_blocks(g):
    blocks = []
    for d in range(N_DEV):
        a, b = d * W_IN_SHARD, (d + 1) * W_IN_SHARD
        pieces = []
        for ref, pad, width in _SEGMENTS:
            lo, hi = max(a, ref), min(b, ref + width)
            if lo < hi:
                pieces.append(g[:, pad + lo - ref:pad + hi - ref])
        blocks.append(jnp.concatenate(pieces, axis=1))
    return jnp.stack(blocks).astype(bf16)


def _head_pad_cols(w, per, lo, hi):
    k = w.shape[0]
    w = w.reshape(k, N_HEADS, per)[:, :, lo:hi]
    return jnp.pad(w, ((0, 0), (0, 0), (0, LANE - (hi - lo)))).reshape(k, N_HEADS * LANE)


def _head_unpad_cols(g, n):
    k = g.shape[0]
    return g.reshape(k, N_HEADS, LANE)[:, :, :n]


def _on_diagonal():
    i = lax.broadcasted_iota(jnp.int32, (8, 1, 8, 1), 0)
    j = lax.broadcasted_iota(jnp.int32, (8, 1, 8, 1), 2)
    return i == j


def _block_diag(w):
    w4 = jnp.broadcast_to(w[:, :, None, :], (8, 64, 8, 64))
    return jnp.where(_on_diagonal(), w4, 0.0).reshape(MIX, MIX)


def _block_diag_inv(g):
    return jnp.sum(jnp.where(_on_diagonal(), g.reshape(8, 64, 8, 64), 0.0), axis=2)


def _head8(v):
    return jnp.pad(v[None, :], ((0, 7), (0, LANE - v.shape[0])))


GROUPS = dict(A=("w_in",), B=("w_uq", "w_ukv", "ssd_conv_w", "lru_conv_w", "w_branch", "w_out"),
              C=("w_ff1", "w_ff2", "w_ple_gate", "w_ple"))


def _kernel_weights(grp, fw):
    if grp == "A":
        w_in = _pad_w_in(fw["w_in"])
        return dict(w_in=w_in, w_dt=w_in[:, U_DT[0]:U_DT[0] + LANE])
    if grp == "C":
        return dict(w_ff1=fw["w_ff1"], w_ff2=fw["w_ff2"], w_pg=fw["w_ple_gate"], w_ple=fw["w_ple"])
    wb = fw["w_branch"]
    wb0 = jnp.pad(wb[0].reshape(N_HEADS, V_HEAD, D_MODEL), ((0, 0), (0, LANE - V_HEAD), (0, 0))).reshape(N_HEADS * LANE, D_MODEL)
    return dict(
        w_uq=_head_pad_cols(fw["w_uq"], QK_NOPE + QK_ROPE, 0, QK_NOPE + QK_ROPE),
        w_uk=_head_pad_cols(fw["w_ukv"], QK_NOPE + V_HEAD, 0, QK_NOPE),
        w_uv=_head_pad_cols(fw["w_ukv"], QK_NOPE + V_HEAD, QK_NOPE, QK_NOPE + V_HEAD),
        wb=[wb0, wb[1], wb[2], wb[3]], w_out=fw["w_out"], ssd_conv_w=fw["ssd_conv_w"], lru_conv_w=fw["lru_conv_w"])


def _layer_params(sp, l):
    row = lambda n: sp[n][l][None, :]
    return dict(
        g_mix=row("g_mix"), q_norm=row("q_norm"), kv_norm=row("kv_norm"),
        pool=[sp["w_pool"][l].reshape(4 * LANE, LANE), row("pool_scale")],
        ssd=[None, row("ssd_conv_b"), _head8(sp["ssd_dt_bias"][l]), _head8(sp["ssd_a_log"][l]),
             _head8(sp["ssd_d"][l]), row("ssd_norm")],
        lru=[None, row("lru_conv_b"), _block_diag(sp["lru_w_a"][l]), row("lru_b_a"),
             _block_diag(sp["lru_w_i"][l]), row("lru_b_i"), row("lru_lambda")],
        g_mlp=row("g_mlp"), g_ple=row("g_ple"),
    )


_sig = jax.nn.sigmoid
_SSD_CARRY = [(HALO, SSD_XBC)] + [(LANE, LANE)] * 4


def _tiles(rows):
    return dict(tm=_pick(rows, 512), ta=_pick(rows, 512), tp=_pick(rows, 512), tl=_pick(rows, 512), ts=_pick(rows, 256))


def _mixer_tiles(u, dt32):
    return dict(
        cq=(u, 384, U_CQ[0] // 384), ckv=(u, 256, U_CKV[0] // 256), kr=(u, LANE, U_KR[0] // LANE),
        pool=(u, MIX, U_POOL[0] // MIX), z=(u, MIX, U_Z[0] // MIX), xbc=(u, SSD_XBC, U_XBC[0] // SSD_XBC),
        dt=(dt32, LANE, 0), lg=(u, MIX, U_LG[0] // MIX), lx=(u, MIX, U_LX[0] // MIX))


def _add_norm(acc, resid, g):
    x = acc + resid
    return x, _rms(x, g)


def _layer_fwd(x, h, p_bf, ctx, l, pr, g_next, cosf, sinf):
    rows = x.shape[0]
    ts = _tiles(rows)
    tm = ts["tm"]
    nm = lambda s: f"{s}_l{l}"
    r = dict(x=x)
    if h is None:
        (h,), _ = seq_fwd(nm("rms_in"), f_rms, [pr["g_mix"]], [(x, D_MODEL, 0)], [], [(D_MODEL, bf16)], tm)
    early = [h] + ([cosf, sinf, p_bf] + [a for v in pr.values() for a in (v if isinstance(v, list) else [v]) if a is not None]
                   if l == 0 else [])
    w = dict(_kernel_weights("A", ctx.weights(l, "A", early)))
    u = matmul(nm("w_in"), h, w["w_in"], outs=(U_DTYPE,))
    dt32 = matmul(nm("w_dt"), h, w["w_dt"])
    mt = _mixer_tiles(u, dt32)
    (cqn,), _ = seq_fwd(nm("rms_q"), f_rms, [pr["q_norm"]], [mt["cq"]], [], [(Q_LORA, bf16)], tm)
    (ckvn,), _ = seq_fwd(nm("rms_kv"), f_rms, [pr["kv_norm"]], [mt["ckv"]], [], [(KV_LORA, bf16)], tm)
    (yb,), pool_saved = seq_fwd(nm("pool"), f_pool, pr["pool"], [mt["pool"]], [(POOL_HALO, MIX)], [(MIX, bf16)], ts["tp"])
    w.update(_kernel_weights("B", ctx.weights(l, "B", yb)))
    pr = dict(pr, ssd=[w["ssd_conv_w"]] + pr["ssd"][1:], lru=[w["lru_conv_w"]] + pr["lru"][1:])
    q = matmul(nm("w_uq"), cqn, w["w_uq"])
    kn = matmul(nm("w_uk"), ckvn, w["w_uk"])
    vb = matmul(nm("w_uv"), ckvn, w["w_uv"], outs=(bf16,))
    hw = N_HEADS * LANE
    (qr, kr), _ = seq_fwd(nm("mla_prep"), f_prep, [], [(q, hw, 0), (kn, hw, 0), mt["kr"], (cosf, LANE, 0), (sinf, LANE, 0)],
                          [], [(hw, bf16), (hw, bf16)], tm)
    o, lse = attn_fwd(qr, kr, vb, ts["ta"])
    (yc,), ssd_saved = seq_fwd(nm("ssd"), f_ssd, pr["ssd"], [mt["z"], mt["xbc"], mt["dt"]], _SSD_CARRY, [(MIX, bf16)], SSD_CHUNK)
    (la, lu), lru_saved = seq_fwd(nm("lru_pre"), f_lru_pre, pr["lru"], [mt["lx"]], [(HALO, MIX)], [(MIX, f32), (MIX, f32)], ts["tl"])
    hh = scan_fwd(la, lu, ts["ts"])
    (yd,), _ = seq_fwd(nm("lru_post"), f_lru_post, [], [(hh, MIX, 0), mt["lg"]], [], [(MIX, bf16)], tm)
    ys = [o, yb, yc, yd]
    m, pres = merge_fwd(nm("merge"), ys, w["wb"], u)
    x1, h2 = matmul(nm("w_out"), m, w["w_out"], outs=(f32, bf16), epi=_add_norm, extras=[(x, 0)], rows=[pr["g_mlp"]])
    w.update(_kernel_weights("C", ctx.weights(l, "C", h2)))
    a1, act = matmul(nm("ff1"), h2, w["w_ff1"], outs=(bf16, bf16), epi=lambda acc: (acc, jnp.square(jnp.maximum(acc, 0.0))))
    x2, h3 = matmul(nm("ff2"), act, w["w_ff2"], outs=(f32, bf16), epi=_add_norm, extras=[(x1, 0)], rows=[pr["g_ple"]])
    gl = matmul(nm("ple_gate"), h3, w["w_pg"])
    if g_next is None:
        x3, pe = matmul(nm("ple"), p_bf, w["w_ple"], outs=(f32, f32), epi=lambda acc, g, xr: (xr + acc * _sig(g), acc),
                        extras=[(gl, 0), (x2, 0)])
        h_next = None
    else:
        def ple_norm(acc, g, xr, gn):
            xo = xr + acc * _sig(g)
            return xo, acc, _rms(xo, gn)

        x3, pe, h_next = matmul(nm("ple"), p_bf, w["w_ple"], outs=(f32, f32, bf16), epi=ple_norm,
                                extras=[(gl, 0), (x2, 0)], rows=[g_next])
    r.update(h=h, u=u, cqn=cqn, ckvn=ckvn, q=q, kn=kn, vb=vb, qr=qr, kr=kr, o=o, lse=lse, ys=ys, pres=pres, m=m, x1=x1,
             h2=h2, a1=a1, act=act, x2=x2, h3=h3, gl=gl, pe=pe, p_bf=p_bf, pool_saved=pool_saved, ssd_saved=ssd_saved,
             lru_saved=lru_saved, la=la, hh=hh, w=w, pr=pr, dt32=dt32)
    return x3, h_next, r


def _norm_bwd(dh, x, resid, g):
    rs = lax.rsqrt(jnp.mean(x * x, axis=-1, keepdims=True) + EPS)
    xhat = x * rs
    dxn = dh * g
    dx = rs * (dxn - xhat * jnp.mean(dxn * xhat, axis=-1, keepdims=True)) + resid
    return dx, jnp.sum(dh * xhat, axis=0, keepdims=True)


def _gate_bwd(d, g, pre):
    s = _sig(g.astype(f32))
    return d * s, d * pre.astype(f32) * s * (1.0 - s)


def _layer_bwd(dx3, r, ctx, l, cosf, sinf, tok, extra_small):
    rows = dx3.shape[0]
    ts = _tiles(rows)
    tm = ts["tm"]
    nm = lambda s: f"{s}_l{l}"
    u, w, pr = r["u"], r["w"], r["pr"]
    mt = _mixer_tiles(u, r["dt32"])
    g = {}
    full = lambda a: (a, a.shape[1], 0)
    dpe, dgl = ew(nm("ple_bwd"), _gate_bwd, [full(dx3), full(r["gl"]), full(r["pe"])], [(D_MODEL, bf16)] * 2, tm)
    g["w_ple"] = matmul(nm("d_w_ple"), r["p_bf"], dpe, ta=True, outs=(bf16,), deps=[tok] if tok is not None else [])
    g["w_pg"] = matmul(nm("d_w_pg"), r["h3"], dgl, ta=True, outs=(bf16,))
    dx2, g["g_ple"] = matmul(nm("d_h3"), dgl, w["w_pg"], tb=True, epi=_norm_bwd, extras=[(r["x2"], 0), (dx3, 0)],
                             rows=[pr["g_ple"]], row_sums=1)
    da1 = matmul(nm("d_act"), dx2, w["w_ff2"], tb=True, outs=(bf16,),
                 epi=lambda acc, a: (acc * 2.0 * jnp.maximum(a, 0.0),), extras=[(r["a1"], 0)])
    g["w_ff2"] = matmul(nm("d_w_ff2"), r["act"], dx2, ta=True, outs=(bf16,))
    g["w_ff1"] = matmul(nm("d_w_ff1"), r["h2"], da1, ta=True, outs=(bf16,), out_blocks=N_DEV)
    tok = ctx.grads(l, "C", dict(w_ff1=g["w_ff1"], w_ff2=g["w_ff2"], w_ple_gate=g["w_pg"], w_ple=g["w_ple"]))
    dx1, g["g_mlp"] = matmul(nm("d_h2"), da1, w["w_ff1"], tb=True, epi=_norm_bwd, extras=[(r["x1"], 0), (dx2, 0)],
                             rows=[pr["g_mlp"]], row_sums=1, deps=[tok])
    def merge_bwd(dm, *gates_and_pres):
        both = [_gate_bwd(dm, gates_and_pres[n], gates_and_pres[4 + n]) for n in range(4)]
        return tuple(b[0] for b in both) + tuple(b[1] for b in both)

    res = matmul(nm("d_merged"), dx1, w["w_out"], tb=True, outs=(bf16,) * 8, epi=merge_bwd,
                 extras=[(u, D_MODEL * n) for n in range(4)] + [(pre, 0) for pre in r["pres"]])
    dpres, dgates = list(res[:4]), list(res[4:])
    g["w_out"] = matmul(nm("d_w_out"), r["m"], dx1, ta=True, outs=(bf16,))
    dys, g["wb"] = [], []
    for n in range(4):
        g["wb"].append(matmul(nm(f"d_w_branch{n}"), r["ys"][n], dpres[n], ta=True, outs=(bf16,)))
        dys.append(matmul(nm(f"d_y{n}"), dpres[n], w["wb"][n], tb=True, outs=(bf16 if n == 0 else f32,)))
    dqr, dkr_, dv = attn_bwd(r["qr"], r["kr"], r["vb"], dys[0], r["o"], r["lse"], ts["ta"])
    _, (dq, dkn, dkrope) = seq_bwd(nm("mla_prep_bwd"), f_prep, [],
                                   [full(r["q"]), full(r["kn"]), mt["kr"], full(cosf), full(sinf)],
                                   [True, True, True, False, False], [], [dqr, dkr_], [bf16] * 3, tm)
    g["w_uq"] = matmul(nm("d_w_uq"), r["cqn"], dq, ta=True, outs=(bf16,))
    g["w_uk"] = matmul(nm("d_w_uk"), r["ckvn"], dkn, ta=True, outs=(bf16,))
    g["w_uv"] = matmul(nm("d_w_uv"), r["ckvn"], dv, ta=True, outs=(bf16,))
    dcqn = matmul(nm("d_cqn"), dq, w["w_uq"], tb=True)
    dckvn = matmul(nm("d_ckvn_k"), dkn, w["w_uk"], tb=True)
    dckvn = matmul(nm("d_ckvn_v"), dv, w["w_uv"], tb=True, epi=lambda acc, prev: (acc + prev,), extras=[(dckvn, 0)])
    (g["q_norm"],), (dcq,) = seq_bwd(nm("rms_q_bwd"), f_rms, [pr["q_norm"]], [mt["cq"]], [True], [], [dcqn], [bf16], tm)
    (g["kv_norm"],), (dckv,) = seq_bwd(nm("rms_kv_bwd"), f_rms, [pr["kv_norm"]], [mt["ckv"]], [True], [], [dckvn], [bf16], tm)
    g["pool"], (dpool,) = seq_bwd(nm("pool_bwd"), f_pool, pr["pool"], [mt["pool"]], [True], r["pool_saved"], [dys[1]],
                                  [bf16], ts["tp"])
    g["ssd"], (dz, dxbc, ddt) = seq_bwd(nm("ssd_bwd"), f_ssd, pr["ssd"], [mt["z"], mt["xbc"], mt["dt"]], [True] * 3,
                                        r["ssd_saved"], [dys[2]], [bf16] * 3, SSD_CHUNK)
    _, (dhh, dlg) = seq_bwd(nm("lru_post_bwd"), f_lru_post, [], [full(r["hh"]), mt["lg"]], [True, True], [], [dys[3]],
                            [f32, bf16], tm)
    da, du = scan_bwd(r["la"], r["hh"], dhh, ts["ts"])
    g["lru"], (dlx,) = seq_bwd(nm("lru_pre_bwd"), f_lru_pre, pr["lru"], [mt["lx"]], [True], r["lru_saved"], [da, du],
                               [bf16], ts["tl"])
    dk = _head_unpad_cols(g["w_uk"], QK_NOPE)
    dv_ = _head_unpad_cols(g["w_uv"], V_HEAD)
    wb0 = g["wb"][0].reshape(N_HEADS, LANE, D_MODEL)[:, :V_HEAD].reshape(MIX, D_MODEL)
    ssd, lru, pool = g["ssd"], g["lru"], g["pool"]
    tok = ctx.grads(l, "B", dict(
        w_uq=_head_unpad_cols(g["w_uq"], QK_NOPE + QK_ROPE).reshape(Q_LORA, -1),
        w_ukv=jnp.concatenate([dk, dv_], axis=2).reshape(KV_LORA, -1), ssd_conv_w=ssd[0], lru_conv_w=lru[0],
        w_branch=jnp.stack([wb0, g["wb"][1], g["wb"][2], g["wb"][3]]), w_out=g["w_out"]))
    du_p = jnp.concatenate(dgates + [dpool, dz, dlg, dlx, dxbc, dcq, dkrope, dckv, ddt,
                                     jnp.zeros((rows, U_COLS - U_DT[0] - LANE), bf16)], axis=1)
    small = dict(
        q_norm=g["q_norm"][0], kv_norm=g["kv_norm"][0],
        w_pool=pool[0].reshape(4, LANE, LANE), pool_scale=pool[1][0],
        ssd_conv_b=ssd[1][0], ssd_dt_bias=ssd[2][0, :8], ssd_a_log=ssd[3][0, :8], ssd_d=ssd[4][0, :8], ssd_norm=ssd[5][0],
        lru_conv_b=lru[1][0], lru_w_a=_block_diag_inv(lru[2]), lru_b_a=lru[3][0], lru_w_i=_block_diag_inv(lru[4]),
        lru_b_i=lru[5][0], lru_lambda=lru[6][0], g_mlp=g["g_mlp"][0], g_ple=g["g_ple"][0])
    tok_small = ctx.small(f"l{l}", [(n, l, small[n]) for n in SMALL if n in small] + extra_small)
    g_w_in = matmul(nm("d_w_in"), r["h"], du_p, ta=True, outs=(bf16,), deps=[tok, tok_small])
    tok = ctx.grads(l, "A", dict(w_in=_w_in_blocks(g_w_in)))
    dx, g_mix = matmul(nm("d_h"), du_p, w["w_in"], tb=True, epi=_norm_bwd, extras=[(r["x"], 0), (dx1, 0)],
                       rows=[pr["g_mix"]], row_sums=1, deps=[tok])
    return dx, tok, ("g_mix", l, g_mix[0])


def _rope_tables(positions):
    inv = 1.0 / (ROPE_THETA ** (jnp.arange(0, QK_ROPE, 2, dtype=f32) / QK_ROPE))
    ang = positions.astype(f32)[:, None] * inv
    cos, sin = jnp.cos(ang), jnp.sin(ang)
    rows = positions.shape[0]
    pad = jnp.zeros((rows, LANE - KR_LANE - QK_ROPE), f32)
    cosf = jnp.concatenate([jnp.ones((rows, KR_LANE), f32), cos, cos, pad], axis=1)
    sinf = jnp.concatenate([jnp.zeros((rows, KR_LANE), f32), -sin, sin, pad], axis=1)
    return cosf, sinf


WEIGHTS = ['g_mix', 'w_in', 'q_norm', 'w_uq', 'kv_norm', 'w_ukv', 'w_pool', 'pool_scale', 'ssd_conv_w', 'ssd_conv_b',
           'ssd_dt_bias', 'ssd_a_log', 'ssd_d', 'ssd_norm', 'lru_conv_w', 'lru_conv_b', 'lru_w_a', 'lru_b_a', 'lru_w_i',
           'lru_b_i', 'lru_lambda', 'w_branch', 'w_out', 'g_mlp', 'w_ff1', 'w_ff2', 'g_ple', 'w_ple_gate', 'w_ple', 'g_final']
SHARDED = dict(w_in=2, w_uq=2, w_ukv=2, ssd_conv_w=2, lru_conv_w=2, w_branch=3, w_out=1, w_ff1=2, w_ff2=1,
               w_ple_gate=1, w_ple=2)
F32_PAYLOAD = ("ssd_conv_w", "lru_conv_w")
DEPTH = 2


SMALL = [n for n in WEIGHTS if n not in SHARDED and n != "g_final"]


def local_step(x, p, positions, tgt, sp, ctx):
    cosf, sinf = _rope_tables(positions)
    res, h = [], None
    for l in range(DEPTH):
        g_next = sp["g_mix"][l + 1][None, :] if l + 1 < DEPTH else None
        x, h, r = _layer_fwd(x, h, p[l].astype(bf16), ctx, l, _layer_params(sp, l), g_next, cosf, sinf)
        res.append(r)
    loss8, dx, dgf = loss_head(x, tgt, sp["g_final"][None, :], _pick(x.shape[0], 512))
    tok = None
    pending = ("g_final", None, dgf[0])
    for l in reversed(range(DEPTH)):
        dx, tok, pending = _layer_bwd(dx, res[l], ctx, l, cosf, sinf, tok, [pending])
    ctx.small("last", [pending])
    return loss8[0, 0], dx


def _payload(name, w):
    return w if name in F32_PAYLOAD else w.astype(bf16)


def _blocks(name, g):
    ax = SHARDED[name] - 1
    shape = list(g.shape)
    shape[ax:ax + 1] = [N_DEV, shape[ax] // N_DEV]
    return _payload(name, jnp.moveaxis(g.reshape(shape), ax, 0))


def _assemble(name, shards):
    ax = SHARDED[name] - 1
    shape = list(shards.shape[1:])
    shape[ax] *= N_DEV
    return jnp.moveaxis(shards, 0, ax).reshape(shape)


class _Exchanges:
    def __init__(self, wts):
        self.wts = wts
        self.ag, self.rs, self.sm = {}, {}, {}
        tok = None
        for l in range(DEPTH):
            for grp, names in GROUPS.items():
                h = exchange_start(f"ag_start_{grp}{l}", [_payload(n, wts[n][l]) for n in names], True,
                                   deps=[] if tok is None else [tok])
                tok = h["token"]
                self.ag[(l, grp)] = h
        self.all_started = tok

    def weights(self, l, grp, after):
        afters = list(after) if isinstance(after, (list, tuple)) else [after]
        if (l, grp) == (0, "A"):
            afters.append(self.all_started)
        got = exchange_wait(f"ag_wait_{grp}{l}", self.ag[(l, grp)], afters)
        out = {}
        for n, a in zip(GROUPS[grp], got):
            out[n] = a if n == "w_in" else _assemble(n, a)
        return out

    def grads(self, l, grp, g):
        cut = lambda n: g[n].ndim == self.wts[n].ndim
        h = exchange_start(f"rs_start_{grp}{l}", [g[n] if cut(n) else _blocks(n, g[n]) for n in GROUPS[grp]], False)
        self.rs[(l, grp)] = h
        return h["token"]

    def small(self, tag, entries):
        flat = jnp.concatenate([a.reshape(-1) for _, _, a in entries])
        flat = jnp.pad(flat, (0, (-flat.shape[0]) % (8 * LANE))).reshape(-1, LANE)
        h = exchange_start(f"small_start_{tag}", [flat], True)
        self.sm[tag] = (h, [(n, l, a.shape) for n, l, a in entries])
        return h["token"]

    def collect(self, groups, after):
        parts = {}
        for grp in groups:
            for l in reversed(range(DEPTH)):
                got = exchange_wait(f"rs_wait_{grp}{l}", self.rs[(l, grp)], [after])
                for n, a in zip(GROUPS[grp], got):
                    parts.setdefault(n, [None] * DEPTH)[l] = a
        return parts

    def collect_small(self, after):
        parts = {}
        for tag, (h, layout) in self.sm.items():
            (got,) = exchange_wait(f"small_wait_{tag}", h, [after])
            got = got.reshape(N_DEV, -1)
            off = 0
            for n, l, shape in layout:
                size = 1
                for d in shape:
                    size *= d
                part = got[:, off:off + size].reshape((N_DEV,) + tuple(shape))
                off += size
                if l is None:
                    parts[n] = [part]
                else:
                    parts.setdefault(n, [None] * DEPTH)[l] = part
        return parts


def kernel(x, p, positions, g_mix, w_in, q_norm, w_uq, kv_norm, w_ukv, w_pool, pool_scale, ssd_conv_w, ssd_conv_b,
           ssd_dt_bias, ssd_a_log, ssd_d, ssd_norm, lru_conv_w, lru_conv_b, lru_w_a, lru_b_a, lru_w_i, lru_b_i,
           lru_lambda, w_branch, w_out, g_mlp, w_ff1, w_ff2, g_ple, w_ple_gate, w_ple, g_final, loss_target, m_g_mix,
           m_w_in, m_q_norm, m_w_uq, m_kv_norm, m_w_ukv, m_w_pool, m_pool_scale, m_ssd_conv_w, m_ssd_conv_b,
           m_ssd_dt_bias, m_ssd_a_log, m_ssd_d, m_ssd_norm, m_lru_conv_w, m_lru_conv_b, m_lru_w_a, m_lru_b_a,
           m_lru_w_i, m_lru_b_i, m_lru_lambda, m_w_branch, m_w_out, m_g_mlp, m_w_ff1, m_w_ff2, m_g_ple, m_w_ple_gate,
           m_w_ple, m_g_final, v_g_mix, v_w_in, v_q_norm, v_w_uq, v_kv_norm, v_w_ukv, v_w_pool, v_pool_scale,
           v_ssd_conv_w, v_ssd_conv_b, v_ssd_dt_bias, v_ssd_a_log, v_ssd_d, v_ssd_norm, v_lru_conv_w, v_lru_conv_b,
           v_lru_w_a, v_lru_b_a, v_lru_w_i, v_lru_b_i, v_lru_lambda, v_w_branch, v_w_out, v_g_mlp, v_w_ff1, v_w_ff2,
           v_g_ple, v_w_ple_gate, v_w_ple, v_g_final):
    given = dict(locals())
    wts = {n: given[n] for n in WEIGHTS}
    ctx = _Exchanges(wts)
    loss, grad_x = local_step(x[0], p[:, 0], positions[0], loss_target[0], wts, ctx)

    def update(parts):
        out = {}
        for n, eight in parts.items():
            w, m, v = wts[n], given["m_" + n], given["v_" + n]
            if n == "g_final":
                out[n] = [a[0] for a in adamw(f"adamw_{n}", eight, w[None], m[None], v[None])]
            else:
                out[n] = adamw(f"adamw_{n}", eight, w, m, v)
        return out

    outs = update(ctx.collect(("C", "B"), grad_x))
    late = outs["w_ff1"][1]
    outs.update(update(ctx.collect(("A",), late)))
    outs.update(update(ctx.collect_small(late)))
    loss = lax.psum(loss, AXES)
    return (loss, grad_x[None], *[outs[n][0] for n in WEIGHTS], *[outs[n][1] for n in WEIGHTS],
            *[outs[n][2] for n in WEIGHTS], *[outs[n][3] for n in WEIGHTS])
```

```python
import functools

import jax
import jax.numpy as jnp
from jax import lax
from jax.experimental import pallas as pl
from jax.experimental.pallas import tpu as pltpu

f32 = jnp.float32
bf16 = jnp.bfloat16

D_MODEL = 1024
MIX = 512
N_HEADS = 8
QK_NOPE, QK_ROPE, V_HEAD = 64, 32, 64
Q_LORA, KV_LORA = 384, 256
ROPE_THETA = 10000.0
POOL_WINDOWS = (2, 4, 8, 16)
SSD_CHUNK = 128
SSD_XBC = 768
CONV_W = 4
LRU_C = 8.0
D_FF = 4096
EPS = 1e-6
IN_COLS = 7592
ADAM_LR, ADAM_B1, ADAM_B2, ADAM_EPS, ADAM_WD, ADAM_STEP = 0.001, 0.9, 0.999, 1e-08, 0.01, 10

LANE = 128
HALO = 8
POOL_HALO = 16
VMEM_LIMIT = 56 * 1024 * 1024
MATMUL_MAX_K_TILE = 4096
MATMUL_ACC_PASS_WEIGHT = 0.3
MATMUL_VMEM_BUDGET = 40 * 1024 * 1024
N_DEV = 8
AXES = ("x", "y", "c")

U_COLS = 8192
U_GATES, U_POOL, U_Z, U_LG, U_LX, U_XBC, U_CQ, U_KR, U_CKV, U_DT = (
    (0, 4096), (4096, 512), (4608, 512), (5120, 512), (5632, 512), (6144, 768),
    (6912, 384), (7296, 128), (7424, 256), (7680, 128))
KR_LANE = 64
U_DTYPE = bf16


def _cp(sem):
    return pltpu.CompilerParams(dimension_semantics=sem, vmem_limit_bytes=VMEM_LIMIT)


def _pick(dim, pref):
    if dim <= pref:
        return dim
    t = pref
    while t >= LANE:
        if dim % t == 0:
            return t
        t -= LANE
    t = pref
    while dim % t:
        t -= 8
    return t


@functools.partial(jax.custom_vjp, nondiff_argnums=(1,))
def shift_down(x, k):
    row = lax.broadcasted_iota(jnp.int32, x.shape, 0)
    return jnp.where(row >= k, pltpu.roll(x, k, 0), 0.0)


def _shift_down_fwd(x, k):
    return shift_down(x, k), None


def _shift_down_bwd(k, _, g):
    r = g.shape[0]
    row = lax.broadcasted_iota(jnp.int32, g.shape, 0)
    return (jnp.where(row < r - k, pltpu.roll(g, r - k, 0), 0.0),)


shift_down.defvjp(_shift_down_fwd, _shift_down_bwd)


@functools.partial(jax.custom_vjp, nondiff_argnums=(1,))
def lane_roll(x, s):
    return pltpu.roll(x, s, 1)


def _lane_roll_fwd(x, s):
    return lane_roll(x, s), None


def _lane_roll_bwd(s, _, g):
    return (pltpu.roll(g, (g.shape[1] - s) % g.shape[1], 1),)


lane_roll.defvjp(_lane_roll_fwd, _lane_roll_bwd)


def _tile_spec(tm, width, cb, n=None):
    if n is None:
        return pl.BlockSpec((tm, width), lambda i: (i, cb))
    return pl.BlockSpec((tm, width), lambda i: (n - 1 - i, cb))


def _const_spec(shape):
    nd = len(shape)
    return pl.BlockSpec(shape, lambda i: (0,) * nd)


def seq_fwd(name, f, params, tiles, carries, outs, tm):
    rows = tiles[0][0].shape[0]
    n = rows // tm
    np_, nt, no, nc = len(params), len(tiles), len(outs), len(carries)

    def body(*refs):
        p_refs = refs[:np_]
        t_refs = refs[np_:np_ + nt]
        o_refs = refs[np_ + nt:np_ + nt + no]
        s_refs = refs[np_ + nt + no:np_ + nt + no + nc]
        c_refs = refs[np_ + nt + no + nc:]
        i = pl.program_id(0)

        @pl.when(i == 0)
        def _():
            for c in c_refs:
                c[...] = jnp.zeros_like(c)

        cvals = [c[...] for c in c_refs]
        for s, c in zip(s_refs, cvals):
            s[0] = c
        o, newc = f(i, [r[...] for r in p_refs], cvals, [r[...].astype(f32) for r in t_refs])
        for r, v in zip(o_refs, o):
            r[...] = v.astype(r.dtype)
        for r, v in zip(c_refs, newc):
            r[...] = v

    in_specs = [_const_spec(p.shape) for p in params] + [_tile_spec(tm, w, cb) for (_, w, cb) in tiles]
    out_specs = [_tile_spec(tm, w, 0) for (w, _) in outs]
    out_specs += [pl.BlockSpec((1,) + tuple(c), lambda i, nd=len(c): (i,) + (0,) * nd) for c in carries]
    out_shape = [jax.ShapeDtypeStruct((rows, w), dt) for (w, dt) in outs]
    out_shape += [jax.ShapeDtypeStruct((n,) + tuple(c), f32) for c in carries]
    res = pl.pallas_call(
        body, name=name, grid=(n,), in_specs=in_specs, out_specs=out_specs, out_shape=out_shape,
        scratch_shapes=[pltpu.VMEM(tuple(c), f32) for c in carries],
        compiler_params=_cp(("arbitrary",)),
    )(*params, *[t[0] for t in tiles])
    return list(res[:no]), list(res[no:])


def seq_bwd(name, f, params, tiles, diff, saved, douts, gdtypes, tm):
    rows = tiles[0][0].shape[0]
    n = rows // tm
    np_, nt, nc, nd = len(params), len(tiles), len(saved), len(douts)
    didx = [k for k, d in enumerate(diff) if d]
    ng = len(didx)

    def body(*refs):
        p_refs = refs[:np_]
        t_refs = refs[np_:np_ + nt]
        s_refs = refs[np_ + nt:np_ + nt + nc]
        d_refs = refs[np_ + nt + nc:np_ + nt + nc + nd]
        pos = np_ + nt + nc + nd
        dp_refs = refs[pos:pos + np_]
        dt_refs = refs[pos + np_:pos + np_ + ng]
        dc_refs = refs[pos + np_ + ng:]
        i = pl.program_id(0)
        step = n - 1 - i

        @pl.when(i == 0)
        def _():
            for r in dp_refs:
                r[...] = jnp.zeros_like(r)
            for r in dc_refs:
                r[...] = jnp.zeros_like(r)

        pvals = [r[...] for r in p_refs]
        cvals = [r[0] for r in s_refs]
        xvals = [r[...].astype(f32) for r in t_refs]

        def fn(p, c, xd):
            x = list(xvals)
            for k, v in zip(didx, xd):
                x[k] = v
            return f(step, p, c, x)

        _, vjp = jax.vjp(fn, pvals, cvals, [xvals[k] for k in didx])
        dp, dc, dx = vjp(([r[...].astype(f32) for r in d_refs], [r[...] for r in dc_refs]))
        for r, v in zip(dp_refs, dp):
            r[...] += v
        for r, v in zip(dc_refs, dc):
            r[...] = v
        for r, v in zip(dt_refs, dx):
            r[...] = v.astype(r.dtype)

    in_specs = [_const_spec(p.shape) for p in params] + [_tile_spec(tm, w, cb, n) for (_, w, cb) in tiles]
    in_specs += [pl.BlockSpec((1,) + tuple(s.shape[1:]), lambda i, nd_=s.ndim - 1: (n - 1 - i,) + (0,) * nd_) for s in saved]
    in_specs += [_tile_spec(tm, d.shape[1], 0, n) for d in douts]
    args = list(params) + [t[0] for t in tiles] + list(saved) + list(douts)
    out_specs = [_const_spec(p.shape) for p in params] + [_tile_spec(tm, tiles[k][1], 0, n) for k in didx]
    out_shape = [jax.ShapeDtypeStruct(p.shape, f32) for p in params]
    out_shape += [jax.ShapeDtypeStruct((rows, tiles[k][1]), dt) for k, dt in zip(didx, gdtypes)]
    res = pl.pallas_call(
        body, name=name, grid=(n,), in_specs=in_specs, out_specs=out_specs, out_shape=out_shape,
        scratch_shapes=[pltpu.VMEM(tuple(s.shape[1:]), f32) for s in saved],
        compiler_params=_cp(("arbitrary",)),
    )(*args)
    return list(res[:np_]), list(res[np_:])


def _halvings(dim, lo, hi):
    t, out = _pick(dim, hi), []
    while t >= min(lo, dim) and dim % t == 0:
        out.append(t)
        if t % 2 or (t // 2) % 8:
            break
        t //= 2
    return out


def _matmul_tiles(m, n, k, a_item, b_item, per_out, max_tn=1024, whole_rows=False):
    def vmem_bytes(tm, tn, tk):
        acc = 4 if k // tk > 1 else 0
        return 2 * (tm * tk * a_item + tk * tn * b_item + tm * tn * per_out) + tm * tn * acc

    def traffic(tm, tn, tk):
        nk = k // tk
        return (m * k * a_item * (1 if nk == 1 else n // tn) + k * n * b_item * (m // tm)
                + (nk - 1) * m * n * 8 * MATMUL_ACC_PASS_WEIGHT)

    cands = [(traffic(tm, tn, tk), -tm * tn, tm, tn, tk)
             for tk in _halvings(k, 512, MATMUL_MAX_K_TILE) for tm in _halvings(m, 256, 4096)
             for tn in ([n] if whole_rows else _halvings(n, 512, min(1024, max_tn)))
             if vmem_bytes(tm, tn, tk) <= MATMUL_VMEM_BUDGET]
    return min(cands)[2:]


def matmul(name, a, b, *, ta=False, tb=False, outs=(f32,), epi=None, extras=(), rows=(), row_sums=0, deps=(),
           out_blocks=0):
    m, k = (a.shape[1], a.shape[0]) if ta else a.shape
    n = b.shape[0] if tb else b.shape[1]
    per_out = sum(jnp.dtype(dt).itemsize for dt in outs) + sum(e[0].dtype.itemsize for e in extras)
    whole_rows = bool(rows) or row_sums > 0
    tm, tn, tk = _matmul_tiles(m, n, k, a.dtype.itemsize, b.dtype.itemsize, per_out,
                               n // out_blocks if out_blocks else n, whole_rows)
    nk = k // tk
    ne, nr, nd, no = len(extras), len(rows), len(deps), len(outs)
    dims = (((0 if ta else 1,), (1 if tb else 0,)), ((), ()))

    def body(*refs):
        a_ref, b_ref = refs[0], refs[1]
        e_refs = refs[2:2 + ne]
        r_refs = refs[2 + ne:2 + ne + nr]
        o_refs = refs[2 + ne + nr + nd:2 + ne + nr + nd + no]
        s_refs = refs[2 + ne + nr + nd + no:2 + ne + nr + nd + no + row_sums]
        i, kk = pl.program_id(0), pl.program_id(2)
        part = lax.dot_general(a_ref[...].astype(bf16), b_ref[...].astype(bf16), dims, preferred_element_type=f32)

        def finish(total):
            res = (total,) if epi is None else epi(total, *[e[...] for e in e_refs], *[r[...] for r in r_refs])
            for r, v in zip(o_refs, res[:no]):
                r[...] = v.astype(r.dtype)
            for r, v in zip(s_refs, res[no:]):
                v8 = jnp.broadcast_to(v, r.shape)

                @pl.when(i == 0)
                def _(r=r, v8=v8):
                    r[...] = v8

                @pl.when(i > 0)
                def _(r=r, v8=v8):
                    r[...] += v8

        if nk == 1:
            finish(part)
            return
        acc = refs[-1]

        @pl.when(kk == 0)
        def _():
            acc[...] = part

        @pl.when(jnp.logical_and(kk > 0, kk < nk - 1))
        def _():
            acc[...] += part

        @pl.when(kk == nk - 1)
        def _():
            finish(acc[...] + part)

    a_spec = pl.BlockSpec((tk, tm), lambda i, j, q: (q, i)) if ta else pl.BlockSpec((tm, tk), lambda i, j, q: (i, q))
    b_spec = pl.BlockSpec((tn, tk), lambda i, j, q: (j, q)) if tb else pl.BlockSpec((tk, tn), lambda i, j, q: (q, j))
    assert all(off % tn == 0 for (_, off) in extras)
    e_specs = [pl.BlockSpec((tm, tn), lambda i, j, q, off=off // tn: (i, off + j)) for (_, off) in extras]
    r_specs = [pl.BlockSpec((1, tn), lambda i, j, q: (0, j)) for _ in rows]
    if out_blocks:
        per = n // out_blocks // tn
        out_spec = pl.BlockSpec((None, tm, tn), lambda i, j, q: (j // per, i, j % per))
        out_dims = (out_blocks, m, n // out_blocks)
    else:
        out_spec = pl.BlockSpec((tm, tn), lambda i, j, q: (i, j))
        out_dims = (m, n)
    res = pl.pallas_call(
        body, name=name, grid=(m // tm, n // tn, nk),
        in_specs=[a_spec, b_spec] + e_specs + r_specs + [pl.BlockSpec(memory_space=pl.ANY) for _ in deps],
        out_specs=[out_spec for _ in outs] + [pl.BlockSpec((8, tn), lambda i, j, q: (0, j))] * row_sums,
        out_shape=[jax.ShapeDtypeStruct(out_dims, dt) for dt in outs] + [jax.ShapeDtypeStruct((8, n), f32)] * row_sums,
        scratch_shapes=[pltpu.VMEM((tm, tn), f32)] if nk > 1 else [],
        compiler_params=_cp(("arbitrary" if row_sums else "parallel", "parallel", "arbitrary")),
    )(a, b, *[e[0] for e in extras], *rows, *deps)
    return res[0] if len(res) == 1 else tuple(res)


def merge_fwd(name, ys, wbs, u):
    rows, n_out = ys[0].shape[0], wbs[0].shape[1]
    nb = len(ys)
    tm, tn = _pick(rows, 512), _pick(n_out, 512)

    def body(*refs):
        y_refs, w_refs, g_refs = refs[:nb], refs[nb:2 * nb], refs[2 * nb:3 * nb]
        m_ref, p_refs = refs[3 * nb], refs[3 * nb + 1:]
        total = None
        for y_ref, w_ref, g_ref, p_ref in zip(y_refs, w_refs, g_refs, p_refs):
            pre = jnp.dot(y_ref[...], w_ref[...], preferred_element_type=f32)
            p_ref[...] = pre.astype(p_ref.dtype)
            term = jax.nn.sigmoid(g_ref[...].astype(f32)) * pre
            total = term if total is None else total + term
        m_ref[...] = total.astype(m_ref.dtype)

    in_specs = [pl.BlockSpec((tm, y.shape[1]), lambda i, j: (i, 0)) for y in ys]
    in_specs += [pl.BlockSpec((w.shape[0], tn), lambda i, j: (0, j)) for w in wbs]
    in_specs += [pl.BlockSpec((tm, tn), lambda i, j, off=n * (n_out // tn): (i, off + j)) for n in range(nb)]
    out_spec = pl.BlockSpec((tm, tn), lambda i, j: (i, j))
    res = pl.pallas_call(
        body, name=name, grid=(rows // tm, n_out // tn), in_specs=in_specs, out_specs=[out_spec] * (nb + 1),
        out_shape=[jax.ShapeDtypeStruct((rows, n_out), bf16)] * (nb + 1),
        compiler_params=_cp(("parallel", "parallel")),
    )(*ys, *wbs, *([u] * nb))
    return res[0], list(res[1:])


ATT_SCALE = (QK_NOPE + QK_ROPE) ** -0.5
LN2 = 0.6931471805599453
ATT_C = ATT_SCALE / LN2
NT = (((1,), (1,)), ((), ()))
TN = (((0,), (0,)), ((), ()))


def _causal(tq, tk):
    return lax.broadcasted_iota(jnp.int32, (tq, tk), 0) >= lax.broadcasted_iota(jnp.int32, (tq, tk), 1)


def _tri_pairs(n, by_column):
    if by_column:
        pairs = [(i, j) for j in range(n) for i in range(j, n)]
    else:
        pairs = [(i, j) for i in range(n) for j in range(i + 1)]
    return (jnp.asarray([a for a, _ in pairs], jnp.int32), jnp.asarray([b for _, b in pairs], jnp.int32))


FWD_HEADS_PER_STEP = 8
HEADS_PER_STEP = 4
HEAD_PAIR = HEADS_PER_STEP * LANE


def attn_fwd(q, k, v, t):
    rows = q.shape[0]
    n = rows // t
    it, jt = _tri_pairs(n, False)

    def body(it_ref, jt_ref, q_ref, k_ref, v_ref, o_ref, lse_ref, m_s, l_s, acc_s):
        s_id = pl.program_id(1)
        i, j = it_ref[s_id], jt_ref[s_id]

        @pl.when(j == 0)
        def _():
            m_s[...] = jnp.full_like(m_s, -jnp.inf)
            l_s[...] = jnp.zeros_like(l_s)
            acc_s[...] = jnp.zeros_like(acc_s)

        def step(diag):
            for hh in range(FWD_HEADS_PER_STEP):
                sl = slice(LANE * hh, LANE * (hh + 1))
                s = lax.dot_general(q_ref[:, sl], k_ref[:, sl], NT, preferred_element_type=f32)
                if diag:
                    s = jnp.where(_causal(t, t), s, -jnp.inf)
                m_prev = m_s[:, sl]
                m_new = jnp.maximum(m_prev, jnp.max(s, axis=1, keepdims=True))
                alpha = jnp.exp2(m_prev - m_new)
                p = jnp.exp2(s - m_new[:, :1])
                l_s[:, sl] = alpha * l_s[:, sl] + jnp.sum(p, axis=1, keepdims=True)
                acc_s[:, sl] = alpha * acc_s[:, sl] + jnp.dot(p.astype(bf16), v_ref[:, sl], preferred_element_type=f32)
                m_s[:, sl] = m_new

        pl.when(j < i)(lambda: step(False))

        @pl.when(j == i)
        def _():
            step(True)
            o_ref[...] = (acc_s[...] / l_s[...]).astype(o_ref.dtype)
            lse_ref[...] = m_s[...] + jnp.log2(l_s[...])

    width = FWD_HEADS_PER_STEP * LANE
    qs = pl.BlockSpec((t, width), lambda h, s, it_, jt_: (it_[s], h))
    ks = pl.BlockSpec((t, width), lambda h, s, it_, jt_: (jt_[s], h))
    hw = N_HEADS * LANE
    return pl.pallas_call(
        body, name="attn_fwd",
        grid_spec=pltpu.PrefetchScalarGridSpec(
            num_scalar_prefetch=2, grid=(hw // width, it.shape[0]), in_specs=[qs, ks, ks], out_specs=[qs, qs],
            scratch_shapes=[pltpu.VMEM((t, width), f32)] * 3),
        out_shape=[jax.ShapeDtypeStruct((rows, hw), bf16), jax.ShapeDtypeStruct((rows, hw), f32)],
        compiler_params=_cp(("parallel", "arbitrary")),
    )(it, jt, q, k, v)


def attn_bwd(q, k, v, do, o, lse, t):
    rows = q.shape[0]
    n = rows // t
    it, jt = _tri_pairs(n, True)

    def body(it_ref, jt_ref, q_ref, k_ref, v_ref, do_ref, o_ref, lse_ref, dq_ref, dk_ref, dv_ref, dk_s, dv_s):
        s_id = pl.program_id(1)
        i, j = it_ref[s_id], jt_ref[s_id]

        @pl.when(s_id == 0)
        def _():
            dq_ref[...] = jnp.zeros_like(dq_ref)

        @pl.when(i == j)
        def _():
            dk_s[...] = jnp.zeros_like(dk_s)
            dv_s[...] = jnp.zeros_like(dv_s)

        q_rows = pl.ds(pl.multiple_of(i * t, t), t)

        def step(diag):
            for hh in range(HEADS_PER_STEP):
                sl = slice(LANE * hh, LANE * (hh + 1))
                qh, kh, vh, doh = q_ref[:, sl], k_ref[:, sl], v_ref[:, sl], do_ref[:, sl]
                s = lax.dot_general(qh, kh, NT, preferred_element_type=f32)
                p = jnp.exp2(s - lse_ref[:, sl][:, :1])
                if diag:
                    p = jnp.where(_causal(t, t), p, 0.0)
                dp = lax.dot_general(doh, vh, NT, preferred_element_type=f32)
                delta = jnp.sum(doh.astype(f32) * o_ref[:, sl].astype(f32), axis=1, keepdims=True)
                ds = (p * (dp - delta) * LN2).astype(bf16)
                dv_s[:, sl] += lax.dot_general(p.astype(bf16), doh, TN, preferred_element_type=f32)
                dk_s[:, sl] += lax.dot_general(ds, qh, TN, preferred_element_type=f32)
                dq_ref[q_rows, sl] += jnp.dot(ds, kh, preferred_element_type=f32)

        pl.when(i > j)(lambda: step(False))
        pl.when(i == j)(lambda: step(True))

        @pl.when(i == n - 1)
        def _():
            dk_ref[...] = dk_s[...]
            dv_ref[...] = dv_s[...]

    qs = pl.BlockSpec((t, HEAD_PAIR), lambda h, s, it_, jt_: (it_[s], h))
    ks = pl.BlockSpec((t, HEAD_PAIR), lambda h, s, it_, jt_: (jt_[s], h))
    dqs = pl.BlockSpec((rows, HEAD_PAIR), lambda h, s, it_, jt_: (0, h))
    hw = N_HEADS * LANE
    return pl.pallas_call(
        body, name="attn_bwd",
        grid_spec=pltpu.PrefetchScalarGridSpec(
            num_scalar_prefetch=2, grid=(hw // HEAD_PAIR, it.shape[0]), in_specs=[qs, ks, ks, qs, qs, qs],
            out_specs=[dqs, ks, ks], scratch_shapes=[pltpu.VMEM((t, HEAD_PAIR), f32)] * 2),
        out_shape=[jax.ShapeDtypeStruct((rows, hw), f32)] * 3,
        compiler_params=_cp(("parallel", "arbitrary")),
    )(it, jt, q, k, v, do, o, lse)


def _steps(tm):
    k, out = 1, []
    while k < tm:
        out.append(k)
        k *= 2
    return out


def scan_fwd(a, u, tm):
    rows, ch = a.shape
    n = rows // tm

    def body(a_ref, u_ref, h_ref, h_s):
        @pl.when(pl.program_id(0) == 0)
        def _():
            h_s[...] = jnp.zeros_like(h_s)

        av, bv = a_ref[...], u_ref[...]
        row = lax.broadcasted_iota(jnp.int32, av.shape, 0)
        for k in _steps(tm):
            a_sh = jnp.where(row >= k, pltpu.roll(av, k, 0), 1.0)
            b_sh = jnp.where(row >= k, pltpu.roll(bv, k, 0), 0.0)
            bv = av * b_sh + bv
            av = av * a_sh
        h = bv + av * h_s[HALO - 1:HALO, :]
        h_ref[...] = h
        h_s[...] = h[tm - HALO:, :]

    spec = pl.BlockSpec((tm, ch), lambda i: (i, 0))
    return pl.pallas_call(
        body, name="lru_scan_fwd", grid=(n,), in_specs=[spec, spec], out_specs=spec,
        out_shape=jax.ShapeDtypeStruct((rows, ch), f32), scratch_shapes=[pltpu.VMEM((HALO, ch), f32)],
        compiler_params=_cp(("arbitrary",)),
    )(a, u)


def scan_bwd(a, h, dh, tm):
    rows, ch = a.shape
    n = rows // tm
    per = tm // HALO

    def body(a_ref, h_ref, hp_ref, dh_ref, da_ref, du_ref, g_s, a_s):
        i = pl.program_id(0)
        step = n - 1 - i

        @pl.when(i == 0)
        def _():
            g_s[...] = jnp.zeros_like(g_s)
            a_s[...] = jnp.zeros_like(a_s)

        a0 = a_ref[...]
        row = lax.broadcasted_iota(jnp.int32, a0.shape, 0)
        av = jnp.where(row < tm - 1, pltpu.roll(a0, tm - 1, 0), a_s[0:1, :])
        bv = dh_ref[...]
        for k in _steps(tm):
            a_sh = jnp.where(row < tm - k, pltpu.roll(av, tm - k, 0), 1.0)
            b_sh = jnp.where(row < tm - k, pltpu.roll(bv, tm - k, 0), 0.0)
            bv = bv + av * b_sh
            av = av * a_sh
        g = bv + av * g_s[0:1, :]
        h_last = jnp.where(step > 0, hp_ref[HALO - 1:HALO, :], 0.0)
        h_prev = jnp.where(row >= 1, pltpu.roll(h_ref[...], 1, 0), h_last)
        du_ref[...] = g
        da_ref[...] = g * h_prev
        g_s[...] = g[0:HALO, :]
        a_s[...] = a0[0:HALO, :]

    spec = pl.BlockSpec((tm, ch), lambda i: (n - 1 - i, 0))
    hp_spec = pl.BlockSpec((HALO, ch), lambda i: (jnp.maximum((n - 1 - i) * per - 1, 0), 0))
    return pl.pallas_call(
        body, name="lru_scan_bwd", grid=(n,), in_specs=[spec, spec, hp_spec, spec], out_specs=[spec, spec],
        out_shape=[jax.ShapeDtypeStruct((rows, ch), f32)] * 2,
        scratch_shapes=[pltpu.VMEM((HALO, ch), f32)] * 2,
        compiler_params=_cp(("arbitrary",)),
    )(a, h, h, dh)


def _rms(x, g):
    return x * lax.rsqrt(jnp.mean(x * x, axis=-1, keepdims=True) + EPS) * g


def f_rms(step, p, c, x):
    return [_rms(x[0], p[0])], []


def _rope(x, cosf, sinf):
    lane = lax.broadcasted_iota(jnp.int32, x.shape, 1)
    sw = jnp.where(lane < KR_LANE + QK_ROPE // 2, lane_roll(x, LANE - QK_ROPE // 2), lane_roll(x, QK_ROPE // 2))
    return x * cosf + sw * sinf


def f_prep(step, p, c, x):
    q, kn, kr, cosf, sinf = x
    kr_rot = _rope(kr, cosf, sinf)
    qr = [_rope(q[:, LANE * h:LANE * (h + 1)], cosf, sinf) * ATT_C for h in range(N_HEADS)]
    kk = [kn[:, LANE * h:LANE * (h + 1)] + kr_rot for h in range(N_HEADS)]
    return [jnp.concatenate(qr, axis=1), jnp.concatenate(kk, axis=1)], []


def _conv(tail, x, w, b):
    xf = jnp.concatenate([tail, x], axis=0)
    acc = b + w[CONV_W - 1:CONV_W, :] * xf
    for k in range(CONV_W - 1):
        acc = acc + w[k:k + 1, :] * shift_down(xf, CONV_W - 1 - k)
    return acc[HALO:, :]


def f_pool(step, p, c, x):
    wp, sc = p
    (tail,) = c
    (u,) = x
    tm = u.shape[0]
    xf = jnp.concatenate([tail, u], axis=0)
    sums, s, w = [], xf, 1
    while w < POOL_WINDOWS[-1]:
        s = s + shift_down(s, w)
        w *= 2
        sums.append(s)
    t = step * tm + lax.broadcasted_iota(jnp.int32, (tm, 1), 0)
    ys = []
    for g, (w, s) in enumerate(zip(POOL_WINDOWS, sums)):
        sl = slice(LANE * g, LANE * (g + 1))
        cnt = jnp.minimum(t + 1, w).astype(f32)
        d = s[POOL_HALO:, sl] / cnt - u[:, sl]
        ys.append(jnp.dot(d.astype(bf16), wp[LANE * g:LANE * (g + 1), :].astype(bf16), preferred_element_type=f32))
    return [jnp.concatenate(ys, axis=1) * sc], [u[tm - POOL_HALO:, :]]


def f_ssd(step, p, c, x):
    conv_w, conv_b, dtb, alog, dsk, ng = p
    tail, s_in = c[0], c[1:]
    z, xbc, dt = x
    ln = z.shape[0]
    xc = jax.nn.silu(_conv(tail, xbc, conv_w, conv_b))
    xs, bb, cc = xc[:, :MIX], xc[:, MIX:MIX + LANE], xc[:, MIX + LANE:]
    dtv = jax.nn.softplus(dt + dtb[0:1, :])
    a = dtv * -jnp.exp(alog[0:1, :])
    ri = lax.broadcasted_iota(jnp.int32, (ln, ln), 0)
    ci = lax.broadcasted_iota(jnp.int32, (ln, ln), 1)
    tril = (ri >= ci).astype(f32)
    triu = (ri <= ci).astype(f32)
    hi = lax.Precision.HIGHEST
    a_cs = jnp.dot(tril, a, precision=hi, preferred_element_type=f32)
    a_cs_t = lax.dot_general(a, triu, TN, precision=hi, preferred_element_type=f32)
    a_tot = jnp.sum(a, axis=0, keepdims=True)
    lane = lax.broadcasted_iota(jnp.int32, (1, LANE), 1)
    half = [(lane < 64).astype(f32), (lane >= 64).astype(f32)]
    hrow = lax.broadcasted_iota(jnp.int32, (LANE, 1), 0)

    def head(v, h):
        return jnp.sum(v * (lane == h).astype(f32), axis=1, keepdims=True)

    def pair(v, j):
        return head(v, 2 * j) * half[0] + head(v, 2 * j + 1) * half[1]

    cg = [(cc * half[g]).astype(bf16) for g in range(2)]
    bg = [(bb * half[g]).astype(bf16) for g in range(2)]
    cb = [lax.dot_general(cg[g], bg[g], NT, preferred_element_type=f32) for g in range(2)]
    ys, s_out = [], []
    for j in range(4):
        g = j // 2
        xs_j = xs[:, LANE * j:LANE * (j + 1)]
        xj = xs_j * pair(dtv, j)
        yj = xs_j * pair(dsk[0:1, :], j)
        for hh in range(2):
            h = 2 * j + hh
            rowv = jnp.sum(a_cs_t * (hrow == h).astype(f32), axis=0, keepdims=True)
            lmat = jnp.exp(jnp.where(ri >= ci, head(a_cs, h) - rowv, -jnp.inf))
            yj = yj + jnp.dot((cb[g] * lmat).astype(bf16), (xj * half[hh]).astype(bf16), preferred_element_type=f32)
        acs = pair(a_cs, j)
        tot = pair(a_tot, j)
        yj = yj + jnp.exp(acs) * jnp.dot(cg[g], s_in[j].astype(bf16), preferred_element_type=f32)
        s_new = jnp.exp(tot) * s_in[j] + lax.dot_general(bg[g], (xj * jnp.exp(tot - acs)).astype(bf16), TN,
                                                         preferred_element_type=f32)
        ys.append(yj)
        s_out.append(s_new)
    y = jnp.concatenate(ys, axis=1) * jax.nn.silu(z)
    return [_rms(y, ng)], [xbc[ln - HALO:, :]] + s_out


def _neg_expm1(y):
    series = -y * (1.0 + y * (0.5 + y * (1.0 / 6 + y * (1.0 / 24 + y * (1.0 / 120)))))
    return jnp.where(y > -0.05, series, 1.0 - jnp.exp(y))


def f_lru_pre(step, p, c, x):
    cw, cb_, wa, ba, wi, bi, lam = p
    (tail,) = c
    (lx,) = x
    tm = lx.shape[0]
    xc = _conv(tail, lx, cw, cb_)
    xb = xc.astype(bf16)
    r = jax.nn.sigmoid(jnp.dot(xb, wa.astype(bf16), preferred_element_type=f32) + ba)
    it = jax.nn.sigmoid(jnp.dot(xb, wi.astype(bf16), preferred_element_type=f32) + bi)
    log_a = -LRU_C * r * jax.nn.softplus(-lam)
    mult = jnp.sqrt(_neg_expm1(2.0 * log_a))
    return [jnp.exp(log_a), xc * it * mult], [lx[tm - HALO:, :]]


def f_lru_post(step, p, c, x):
    h, g = x
    return [h * jax.nn.gelu(g)], []


def loss_head(x, tgt, g, tm):
    rows, d = x.shape
    n = rows // tm

    def body(x_ref, t_ref, g_ref, loss_ref, dx_ref, dg_ref):
        @pl.when(pl.program_id(0) == 0)
        def _():
            loss_ref[...] = jnp.zeros_like(loss_ref)
            dg_ref[...] = jnp.zeros_like(dg_ref)

        def fn(gv, xv):
            err = _rms(xv, gv) - t_ref[...]
            return 0.5 * jnp.sum(jnp.mean(err * err, axis=-1, keepdims=True))

        val, (dg, dx) = jax.value_and_grad(fn, argnums=(0, 1))(g_ref[...], x_ref[...])
        loss_ref[...] += val
        dg_ref[...] += dg
        dx_ref[...] = dx

    spec = pl.BlockSpec((tm, d), lambda i: (i, 0))
    return pl.pallas_call(
        body, name="loss_head", grid=(n,), in_specs=[spec, spec, _const_spec((1, d))],
        out_specs=[_const_spec((8, LANE)), spec, _const_spec((1, d))],
        out_shape=[jax.ShapeDtypeStruct((8, LANE), f32), jax.ShapeDtypeStruct((rows, d), f32),
                   jax.ShapeDtypeStruct((1, d), f32)],
        compiler_params=_cp(("arbitrary",)),
    )(x, tgt, g)


def ew(name, fn, ins, outs, tm):
    rows = ins[0][0].shape[0]
    ni = len(ins)

    def body(*refs):
        res = fn(*[r[...].astype(f32) for r in refs[:ni]])
        for r, v in zip(refs[ni:], res):
            r[...] = v.astype(r.dtype)

    return pl.pallas_call(
        body, name=name, grid=(rows // tm,), in_specs=[_tile_spec(tm, w, cb) for (_, w, cb) in ins],
        out_specs=[_tile_spec(tm, w, 0) for (w, _) in outs],
        out_shape=[jax.ShapeDtypeStruct((rows, w), dt) for (w, dt) in outs],
        compiler_params=_cp(("parallel",)),
    )(*[t[0] for t in ins])


def _peers():
    x, y, c = lax.axis_index("x"), lax.axis_index("y"), lax.axis_index("c")
    me = 4 * x + 2 * y + c
    out = []
    for k in range(1, N_DEV):
        px = 1 - x if k & 4 else x
        py = 1 - y if k & 2 else y
        pc = 1 - c if k & 1 else c
        out.append(((px, py, pc), 4 * px + 2 * py + pc))
    return me, out


_HBM = pl.BlockSpec(memory_space=pltpu.HBM)
_SEM = pl.BlockSpec(memory_space=pltpu.SEMAPHORE)
_EFFECT = pltpu.SideEffectType.DATAFLOW_SIDE_EFFECTING


def _remote(src_ref, land_ref, gather, me, pid, dev, send_sems, recv_sems, k, recv_side):
    return pltpu.make_async_remote_copy(
        src_ref=src_ref if gather else src_ref.at[pid], dst_ref=land_ref.at[pid if recv_side else me],
        send_sem=send_sems.at[k], recv_sem=recv_sems.at[k], device_id=dev, device_id_type=pl.DeviceIdType.MESH)


def _own(src_ref, land_ref, gather, me, sem):
    return pltpu.make_async_copy(src_ref if gather else src_ref.at[me], land_ref.at[me], sem)


def exchange_start(name, srcs, gather, deps=()):
    n, nd = len(srcs), len(deps)
    shapes = [(s.shape if gather else s.shape[1:]) for s in srcs]
    lands = [lax.empty((N_DEV,) + tuple(sh), s.dtype) for s, sh in zip(srcs, shapes)]

    def body(*refs):
        src_refs, land_refs = refs[:n], refs[n:2 * n]
        send_sems, recv_sems, own_sem = refs[2 * n + nd:2 * n + nd + 3]
        token = refs[-1]
        me, peers = _peers()
        for k, (dev, pid) in enumerate(peers):
            for s_ref, l_ref in zip(src_refs, land_refs):
                _remote(s_ref, l_ref, gather, me, pid, dev, send_sems, recv_sems, k, False).start()
        for s_ref, l_ref in zip(src_refs, land_refs):
            _own(s_ref, l_ref, gather, me, own_sem).start()
        token[...] = jnp.zeros_like(token)

    hbm = lambda a: pltpu.with_memory_space_constraint(a, pltpu.HBM)
    res = pl.pallas_call(
        body, name=name,
        out_shape=(pltpu.SemaphoreType.DMA((N_DEV - 1,)), pltpu.SemaphoreType.DMA((N_DEV - 1,)), pltpu.SemaphoreType.DMA(()),
                   *[pltpu.HBM(a.shape, a.dtype) for a in list(srcs) + lands], jax.ShapeDtypeStruct((8, LANE), f32)),
        in_specs=[_HBM] * (2 * n) + [pl.BlockSpec(memory_space=pl.ANY)] * nd,
        out_specs=(_SEM, _SEM, _SEM, *([_HBM] * (2 * n)), pl.BlockSpec(memory_space=pltpu.VMEM)),
        input_output_aliases={i: 3 + i for i in range(2 * n)},
        compiler_params=pltpu.CompilerParams(has_side_effects=_EFFECT),
    )(*[hbm(a) for a in list(srcs) + lands], *deps)
    return dict(sems=res[:3], srcs=list(res[3:3 + n]), lands=list(res[3 + n:3 + 2 * n]), token=res[-1], gather=gather)


def exchange_wait(name, h, afters):
    n, gather = len(h["srcs"]), h["gather"]

    def body(*refs):
        src_refs, land_refs = refs[:n], refs[n:2 * n]
        send_sems, recv_sems, own_sem = refs[2 * n:2 * n + 3]
        me, peers = _peers()
        for k, (dev, pid) in enumerate(peers):
            for s_ref, l_ref in zip(src_refs, land_refs):
                _remote(s_ref, l_ref, gather, me, pid, dev, send_sems, recv_sems, k, True).wait_recv()
        for k, (dev, pid) in enumerate(peers):
            for s_ref, l_ref in zip(src_refs, land_refs):
                _remote(s_ref, l_ref, gather, me, pid, dev, send_sems, recv_sems, k, False).wait_send()
        for s_ref, l_ref in zip(src_refs, land_refs):
            _own(s_ref, l_ref, gather, me, own_sem).wait()

    arrs = h["srcs"] + h["lands"]
    res = pl.pallas_call(
        body, name=name, out_shape=tuple(pltpu.HBM(a.shape, a.dtype) for a in arrs),
        in_specs=[_HBM] * (2 * n) + [_SEM, _SEM, _SEM] + [pl.BlockSpec(memory_space=pl.ANY)] * len(afters),
        out_specs=tuple([_HBM] * (2 * n)), input_output_aliases={i: i for i in range(2 * n)},
        compiler_params=pltpu.CompilerParams(has_side_effects=_EFFECT),
    )(*arrs, *h["sems"], *afters)
    return list(res[n:])


def _adam_update(g, w, m, v):
    mn = ADAM_B1 * m + (1.0 - ADAM_B1) * g
    vn = ADAM_B2 * v + (1.0 - ADAM_B2) * jnp.square(g)
    m_hat = mn / (1.0 - ADAM_B1 ** ADAM_STEP)
    v_hat = vn / (1.0 - ADAM_B2 ** ADAM_STEP)
    return -ADAM_LR * (m_hat / (jnp.sqrt(v_hat) + ADAM_EPS) + ADAM_WD * w), mn, vn


def _adamw_vectors(name, parts, w, m, v):
    nl = len(parts)

    def body(*refs):
        p_refs = refs[:nl]
        w_ref, m_ref, v_ref, g_ref, d_ref, nm_ref, nv_ref = refs[nl:]
        for ll, p_ref in enumerate(p_refs):
            row = slice(ll, ll + 1)
            g = p_ref[0:1, :]
            for i in range(1, N_DEV):
                g = g + p_ref[i:i + 1, :]
            delta, mn, vn = _adam_update(g, w_ref[row, :], m_ref[row, :], v_ref[row, :])
            g_ref[row, :] = g
            d_ref[row, :] = delta
            nm_ref[row, :] = mn
            nv_ref[row, :] = vn

    return list(pl.pallas_call(body, name=name, out_shape=[jax.ShapeDtypeStruct(w.shape, f32)] * 4)(*parts, w, m, v))


def adamw_packed(name, gots, where, ws, ms, vs):
    ng, npar = len(gots), len(ws)

    def body(*refs):
        g_refs = refs[:ng]
        w_refs, m_refs, v_refs = (refs[ng + k * npar:ng + (k + 1) * npar] for k in range(3))
        o_refs = refs[ng + 3 * npar:]
        for p in range(npar):
            width = w_refs[p].shape[1]
            for l, (which, off) in enumerate(where[p]):
                row = slice(l, l + 1)
                cols = slice(off, off + width)
                g = g_refs[which][0:1, cols]
                for i in range(1, N_DEV):
                    g = g + g_refs[which][i:i + 1, cols]
                delta, mn, vn = _adam_update(g, w_refs[p][row, :], m_refs[p][row, :], v_refs[p][row, :])
                for k, val in enumerate((g, delta, mn, vn)):
                    o_refs[4 * p + k][row, :] = val

    out_shape = [jax.ShapeDtypeStruct(w.shape, f32) for w in ws for _ in range(4)]
    res = pl.pallas_call(body, name=name, out_shape=out_shape, compiler_params=_cp(()))(*gots, *ws, *ms, *vs)
    return [list(res[4 * p:4 * p + 4]) for p in range(npar)]


def adamw(name, parts, w, m, v):
    nl = len(parts)
    shape = w.shape[1:]
    c = shape[-1]
    r = 1
    for s in shape[:-1]:
        r *= s
    if r == 1:
        return _adamw_vectors(name, parts, w, m, v)
    tr = _pick(r, 256) if r % 8 == 0 else r
    nb = r // tr
    parts2 = [p.reshape(N_DEV, r, c) for p in parts]
    w2, m2, v2 = (a.reshape(nl, r, c) for a in (w, m, v))

    def body(*refs):
        p_refs = refs[:nl]
        w_ref, m_ref, v_ref, g_ref, d_ref, nm_ref, nv_ref = refs[nl:]
        layer = pl.program_id(0)
        for ll, p_ref in enumerate(p_refs):
            @pl.when(layer == ll)
            def _(p_ref=p_ref):
                g = p_ref[0].astype(f32)
                for i in range(1, N_DEV):
                    g = g + p_ref[i].astype(f32)
                delta, mn, vn = _adam_update(g, w_ref[0], m_ref[0], v_ref[0])
                g_ref[0] = g
                d_ref[0] = delta
                nm_ref[0] = mn
                nv_ref[0] = vn

    def p_spec(ll):
        return pl.BlockSpec((N_DEV, tr, c), lambda l, i: (0, jnp.where(l == ll, i, jnp.where(l > ll, nb - 1, 0)), 0))

    spec = pl.BlockSpec((1, tr, c), lambda l, i: (l, i, 0))
    res = pl.pallas_call(
        body, name=name, grid=(nl, nb), in_specs=[p_spec(ll) for ll in range(nl)] + [spec, spec, spec],
        out_specs=[spec] * 4, out_shape=[jax.ShapeDtypeStruct((nl, r, c), f32)] * 4,
        compiler_params=_cp(("arbitrary", "arbitrary")),
    )(*parts2, w2, m2, v2)
    return [a.reshape(w.shape) for a in res]


_IN_SPLITS = dict(cq=(0, 384), ckv=(384, 640), kr=(640, 672), pool=(672, 1184), z=(1184, 1696), xbc=(1696, 2464),
                  dt=(2464, 2472), lg=(2472, 2984), lx=(2984, 3496), gates=(3496, 7592))


W_IN_SHARD = IN_COLS // N_DEV

_PAD_ORDER = ("gates", "pool", "z", "lg", "lx", "xbc", "cq", KR_LANE, "kr", LANE - KR_LANE - QK_ROPE, "ckv", "dt",
              LANE - 8, U_COLS - U_DT[0] - LANE)
_SEGMENTS = ((0, U_CQ[0], 384), (384, U_CKV[0], 256), (640, U_KR[0] + KR_LANE, QK_ROPE), (672, U_POOL[0], 512),
             (1184, U_Z[0], 512), (1696, U_XBC[0], 768), (2464, U_DT[0], 8), (2472, U_LG[0], 512), (2984, U_LX[0], 512),
             (3496, 0, 4096))


def _pad_w_in(shards):
    rows = shards.shape[1]
    pieces = []
    for item in _PAD_ORDER:
        if isinstance(item, int):
            pieces.append(jnp.zeros((rows, item), shards.dtype))
            continue
        a, b = _IN_SPLITS[item]
        for d in range(a // W_IN_SHARD, (b - 1) // W_IN_SHARD + 1):
            lo, hi = max(a, d * W_IN_SHARD), min(b, (d + 1) * W_IN_SHARD)
            pieces.append(shards[d, :, lo - d * W_IN_SHARD:hi - d * W_IN_SHARD])
    return jnp.concatenate(pieces, axis=1)


def _w_in_blocks(g):
    blocks = []
    for d in range(N_DEV):
        a, b = d * W_IN_SHARD, (d + 1) * W_IN_SHARD
        pieces = []
        for ref, pad, width in _SEGMENTS:
            lo, hi = max(a, ref), min(b, ref + width)
            if lo < hi:
                pieces.append(g[:, pad + lo - ref:pad + hi - ref])
        blocks.append(jnp.concatenate(pieces, axis=1))
    return jnp.stack(blocks).astype(bf16)


def _head_pad_cols(w, per, lo, hi):
    k = w.shape[0]
    w = w.reshape(k, N_HEADS, per)[:, :, lo:hi]
    return jnp.pad(w, ((0, 0), (0, 0), (0, LANE - (hi - lo)))).reshape(k, N_HEADS * LANE)


def _head_unpad_cols(g, n):
    k = g.shape[0]
    return g.reshape(k, N_HEADS, LANE)[:, :, :n]


def _on_diagonal():
    i = lax.broadcasted_iota(jnp.int32, (8, 1, 8, 1), 0)
    j = lax.broadcasted_iota(jnp.int32, (8, 1, 8, 1), 2)
    return i == j


def _block_diag(w):
    w4 = jnp.broadcast_to(w[:, :, None, :], (8, 64, 8, 64))
    return jnp.where(_on_diagonal(), w4, 0.0).reshape(MIX, MIX)


def _block_diag_inv(g):
    return jnp.sum(jnp.where(_on_diagonal(), g.reshape(8, 64, 8, 64), 0.0), axis=2)


def _head8(v):
    return jnp.pad(v[None, :], ((0, 7), (0, LANE - v.shape[0])))


GROUPS = dict(A=("w_in",), B=("w_uq", "w_ukv", "ssd_conv_w", "lru_conv_w", "w_branch", "w_out"),
              C=("w_ff1", "w_ff2", "w_ple_gate", "w_ple"))


def _kernel_weights(grp, fw):
    if grp == "A":
        w_in = _pad_w_in(fw["w_in"])
        return dict(w_in=w_in, w_dt=w_in[:, U_DT[0]:U_DT[0] + LANE])
    if grp == "C":
        return dict(w_ff1=fw["w_ff1"], w_ff2=fw["w_ff2"], w_pg=fw["w_ple_gate"], w_ple=fw["w_ple"])
    wb = fw["w_branch"]
    wb0 = jnp.pad(wb[0].reshape(N_HEADS, V_HEAD, D_MODEL), ((0, 0), (0, LANE - V_HEAD), (0, 0))).reshape(N_HEADS * LANE, D_MODEL)
    return dict(
        w_uq=_head_pad_cols(fw["w_uq"], QK_NOPE + QK_ROPE, 0, QK_NOPE + QK_ROPE),
        w_uk=_head_pad_cols(fw["w_ukv"], QK_NOPE + V_HEAD, 0, QK_NOPE),
        w_uv=_head_pad_cols(fw["w_ukv"], QK_NOPE + V_HEAD, QK_NOPE, QK_NOPE + V_HEAD),
        wb=[wb0, wb[1], wb[2], wb[3]], w_out=fw["w_out"], ssd_conv_w=fw["ssd_conv_w"], lru_conv_w=fw["lru_conv_w"])


def _layer_params(sp, l):
    row = lambda n: sp[n][l][None, :]
    return dict(
        g_mix=row("g_mix"), q_norm=row("q_norm"), kv_norm=row("kv_norm"),
        pool=[sp["w_pool"][l].reshape(4 * LANE, LANE), row("pool_scale")],
        ssd=[None, row("ssd_conv_b"), _head8(sp["ssd_dt_bias"][l]), _head8(sp["ssd_a_log"][l]),
             _head8(sp["ssd_d"][l]), row("ssd_norm")],
        lru=[None, row("lru_conv_b"), _block_diag(sp["lru_w_a"][l]), row("lru_b_a"),
             _block_diag(sp["lru_w_i"][l]), row("lru_b_i"), row("lru_lambda")],
        g_mlp=row("g_mlp"), g_ple=row("g_ple"),
    )


_sig = jax.nn.sigmoid
_SSD_CARRY = [(HALO, SSD_XBC)] + [(LANE, LANE)] * 4


def _tiles(rows):
    return dict(tm=_pick(rows, 512), ta=_pick(rows, 512), tp=_pick(rows, 512), tl=_pick(rows, 512), ts=_pick(rows, 256))


def _mixer_tiles(u, dt32):
    return dict(
        cq=(u, 384, U_CQ[0] // 384), ckv=(u, 256, U_CKV[0] // 256), kr=(u, LANE, U_KR[0] // LANE),
        pool=(u, MIX, U_POOL[0] // MIX), z=(u, MIX, U_Z[0] // MIX), xbc=(u, SSD_XBC, U_XBC[0] // SSD_XBC),
        dt=(dt32, LANE, 0), lg=(u, MIX, U_LG[0] // MIX), lx=(u, MIX, U_LX[0] // MIX))


def _add_norm(acc, resid, g):
    x = acc + resid
    return x, _rms(x, g)


def _layer_fwd(x, h, p_bf, ctx, l, pr, g_next, cosf, sinf):
    rows = x.shape[0]
    ts = _tiles(rows)
    tm = ts["tm"]
    nm = lambda s: f"{s}_l{l}"
    r = dict(x=x)
    if h is None:
        (h,), _ = seq_fwd(nm("rms_in"), f_rms, [pr["g_mix"]], [(x, D_MODEL, 0)], [], [(D_MODEL, bf16)], tm)
    early = [h] + ([cosf, sinf, p_bf] + [a for v in pr.values() for a in (v if isinstance(v, list) else [v]) if a is not None]
                   if l == 0 else [])
    w = dict(_kernel_weights("A", ctx.weights(l, "A", early)))
    u = matmul(nm("w_in"), h, w["w_in"], outs=(U_DTYPE,))
    dt32 = matmul(nm("w_dt"), h, w["w_dt"])
    mt = _mixer_tiles(u, dt32)
    (cqn,), _ = seq_fwd(nm("rms_q"), f_rms, [pr["q_norm"]], [mt["cq"]], [], [(Q_LORA, bf16)], tm)
    (ckvn,), _ = seq_fwd(nm("rms_kv"), f_rms, [pr["kv_norm"]], [mt["ckv"]], [], [(KV_LORA, bf16)], tm)
    (yb,), pool_saved = seq_fwd(nm("pool"), f_pool, pr["pool"], [mt["pool"]], [(POOL_HALO, MIX)], [(MIX, bf16)], ts["tp"])
    w.update(_kernel_weights("B", ctx.weights(l, "B", yb)))
    pr = dict(pr, ssd=[w["ssd_conv_w"]] + pr["ssd"][1:], lru=[w["lru_conv_w"]] + pr["lru"][1:])
    q = matmul(nm("w_uq"), cqn, w["w_uq"])
    kn = matmul(nm("w_uk"), ckvn, w["w_uk"])
    vb = matmul(nm("w_uv"), ckvn, w["w_uv"], outs=(bf16,))
    hw = N_HEADS * LANE
    (qr, kr), _ = seq_fwd(nm("mla_prep"), f_prep, [], [(q, hw, 0), (kn, hw, 0), mt["kr"], (cosf, LANE, 0), (sinf, LANE, 0)],
                          [], [(hw, bf16), (hw, bf16)], tm)
    o, lse = attn_fwd(qr, kr, vb, ts["ta"])
    (yc,), ssd_saved = seq_fwd(nm("ssd"), f_ssd, pr["ssd"], [mt["z"], mt["xbc"], mt["dt"]], _SSD_CARRY, [(MIX, bf16)], SSD_CHUNK)
    (la, lu), lru_saved = seq_fwd(nm("lru_pre"), f_lru_pre, pr["lru"], [mt["lx"]], [(HALO, MIX)], [(MIX, f32), (MIX, f32)], ts["tl"])
    hh = scan_fwd(la, lu, ts["ts"])
    (yd,), _ = seq_fwd(nm("lru_post"), f_lru_post, [], [(hh, MIX, 0), mt["lg"]], [], [(MIX, bf16)], tm)
    ys = [o, yb, yc, yd]
    m, pres = merge_fwd(nm("merge"), ys, w["wb"], u)
    x1, h2 = matmul(nm("w_out"), m, w["w_out"], outs=(f32, bf16), epi=_add_norm, extras=[(x, 0)], rows=[pr["g_mlp"]])
    w.update(_kernel_weights("C", ctx.weights(l, "C", h2)))
    a1, act = matmul(nm("ff1"), h2, w["w_ff1"], outs=(bf16, bf16), epi=lambda acc: (acc, jnp.square(jnp.maximum(acc, 0.0))))
    x2, h3 = matmul(nm("ff2"), act, w["w_ff2"], outs=(f32, bf16), epi=_add_norm, extras=[(x1, 0)], rows=[pr["g_ple"]])
    gl = matmul(nm("ple_gate"), h3, w["w_pg"])
    if g_next is None:
        x3, pe = matmul(nm("ple"), p_bf, w["w_ple"], outs=(f32, f32), epi=lambda acc, g, xr: (xr + acc * _sig(g), acc),
                        extras=[(gl, 0), (x2, 0)])
        h_next = None
    else:
        def ple_norm(acc, g, xr, gn):
            xo = xr + acc * _sig(g)
            return xo, acc, _rms(xo, gn)

        x3, pe, h_next = matmul(nm("ple"), p_bf, w["w_ple"], outs=(f32, f32, bf16), epi=ple_norm,
                                extras=[(gl, 0), (x2, 0)], rows=[g_next])
    r.update(h=h, u=u, cqn=cqn, ckvn=ckvn, q=q, kn=kn, vb=vb, qr=qr, kr=kr, o=o, lse=lse, ys=ys, pres=pres, m=m, x1=x1,
             h2=h2, a1=a1, act=act, x2=x2, h3=h3, gl=gl, pe=pe, p_bf=p_bf, pool_saved=pool_saved, ssd_saved=ssd_saved,
             lru_saved=lru_saved, la=la, hh=hh, w=w, pr=pr, dt32=dt32)
    return x3, h_next, r


def _norm_bwd(dh, x, resid, g):
    rs = lax.rsqrt(jnp.mean(x * x, axis=-1, keepdims=True) + EPS)
    xhat = x * rs
    dxn = dh * g
    dx = rs * (dxn - xhat * jnp.mean(dxn * xhat, axis=-1, keepdims=True)) + resid
    return dx, jnp.sum(dh * xhat, axis=0, keepdims=True)


def _gate_bwd(d, g, pre):
    s = _sig(g.astype(f32))
    return d * s, d * pre.astype(f32) * s * (1.0 - s)


def _layer_bwd(dx3, r, ctx, l, cosf, sinf, tok, extra_small):
    rows = dx3.shape[0]
    ts = _tiles(rows)
    tm = ts["tm"]
    nm = lambda s: f"{s}_l{l}"
    u, w, pr = r["u"], r["w"], r["pr"]
    mt = _mixer_tiles(u, r["dt32"])
    g = {}
    full = lambda a: (a, a.shape[1], 0)
    dpe, dgl = ew(nm("ple_bwd"), _gate_bwd, [full(dx3), full(r["gl"]), full(r["pe"])], [(D_MODEL, bf16)] * 2, tm)
    g["w_ple"] = matmul(nm("d_w_ple"), r["p_bf"], dpe, ta=True, outs=(bf16,), deps=[tok] if tok is not None else [])
    g["w_pg"] = matmul(nm("d_w_pg"), r["h3"], dgl, ta=True, outs=(bf16,))
    dx2, g["g_ple"] = matmul(nm("d_h3"), dgl, w["w_pg"], tb=True, epi=_norm_bwd, extras=[(r["x2"], 0), (dx3, 0)],
                             rows=[pr["g_ple"]], row_sums=1)
    da1 = matmul(nm("d_act"), dx2, w["w_ff2"], tb=True, outs=(bf16,),
                 epi=lambda acc, a: (acc * 2.0 * jnp.maximum(a, 0.0),), extras=[(r["a1"], 0)])
    g["w_ff2"] = matmul(nm("d_w_ff2"), r["act"], dx2, ta=True, outs=(bf16,))
    g["w_ff1"] = matmul(nm("d_w_ff1"), r["h2"], da1, ta=True, outs=(bf16,), out_blocks=N_DEV)
    tok = ctx.grads(l, "C", dict(w_ff1=g["w_ff1"], w_ff2=g["w_ff2"], w_ple_gate=g["w_pg"], w_ple=g["w_ple"]))
    dx1, g["g_mlp"] = matmul(nm("d_h2"), da1, w["w_ff1"], tb=True, epi=_norm_bwd, extras=[(r["x1"], 0), (dx2, 0)],
                             rows=[pr["g_mlp"]], row_sums=1, deps=[tok])
    def merge_bwd(dm, *gates_and_pres):
        both = [_gate_bwd(dm, gates_and_pres[n], gates_and_pres[4 + n]) for n in range(4)]
        return tuple(b[0] for b in both) + tuple(b[1] for b in both)

    res = matmul(nm("d_merged"), dx1, w["w_out"], tb=True, outs=(bf16,) * 8, epi=merge_bwd,
                 extras=[(u, D_MODEL * n) for n in range(4)] + [(pre, 0) for pre in r["pres"]])
    dpres, dgates = list(res[:4]), list(res[4:])
    g["w_out"] = matmul(nm("d_w_out"), r["m"], dx1, ta=True, outs=(bf16,))
    dys, g["wb"] = [], []
    for n in range(4):
        g["wb"].append(matmul(nm(f"d_w_branch{n}"), r["ys"][n], dpres[n], ta=True, outs=(bf16,)))
        dys.append(matmul(nm(f"d_y{n}"), dpres[n], w["wb"][n], tb=True, outs=(bf16 if n == 0 else f32,)))
    dqr, dkr_, dv = attn_bwd(r["qr"], r["kr"], r["vb"], dys[0], r["o"], r["lse"], ts["ta"])
    _, (dq, dkn, dkrope) = seq_bwd(nm("mla_prep_bwd"), f_prep, [],
                                   [full(r["q"]), full(r["kn"]), mt["kr"], full(cosf), full(sinf)],
                                   [True, True, True, False, False], [], [dqr, dkr_], [bf16] * 3, tm)
    g["w_uq"] = matmul(nm("d_w_uq"), r["cqn"], dq, ta=True, outs=(bf16,))
    g["w_uk"] = matmul(nm("d_w_uk"), r["ckvn"], dkn, ta=True, outs=(bf16,))
    g["w_uv"] = matmul(nm("d_w_uv"), r["ckvn"], dv, ta=True, outs=(bf16,))
    dcqn = matmul(nm("d_cqn"), dq, w["w_uq"], tb=True)
    dckvn = matmul(nm("d_ckvn_k"), dkn, w["w_uk"], tb=True)
    dckvn = matmul(nm("d_ckvn_v"), dv, w["w_uv"], tb=True, epi=lambda acc, prev: (acc + prev,), extras=[(dckvn, 0)])
    (g["q_norm"],), (dcq,) = seq_bwd(nm("rms_q_bwd"), f_rms, [pr["q_norm"]], [mt["cq"]], [True], [], [dcqn], [bf16], tm)
    (g["kv_norm"],), (dckv,) = seq_bwd(nm("rms_kv_bwd"), f_rms, [pr["kv_norm"]], [mt["ckv"]], [True], [], [dckvn], [bf16], tm)
    g["pool"], (dpool,) = seq_bwd(nm("pool_bwd"), f_pool, pr["pool"], [mt["pool"]], [True], r["pool_saved"], [dys[1]],
                                  [bf16], ts["tp"])
    g["ssd"], (dz, dxbc, ddt) = seq_bwd(nm("ssd_bwd"), f_ssd, pr["ssd"], [mt["z"], mt["xbc"], mt["dt"]], [True] * 3,
                                        r["ssd_saved"], [dys[2]], [bf16] * 3, SSD_CHUNK)
    _, (dhh, dlg) = seq_bwd(nm("lru_post_bwd"), f_lru_post, [], [full(r["hh"]), mt["lg"]], [True, True], [], [dys[3]],
                            [f32, bf16], tm)
    da, du = scan_bwd(r["la"], r["hh"], dhh, ts["ts"])
    g["lru"], (dlx,) = seq_bwd(nm("lru_pre_bwd"), f_lru_pre, pr["lru"], [mt["lx"]], [True], r["lru_saved"], [da, du],
                               [bf16], ts["tl"])
    dk = _head_unpad_cols(g["w_uk"], QK_NOPE)
    dv_ = _head_unpad_cols(g["w_uv"], V_HEAD)
    wb0 = g["wb"][0].reshape(N_HEADS, LANE, D_MODEL)[:, :V_HEAD].reshape(MIX, D_MODEL)
    ssd, lru, pool = g["ssd"], g["lru"], g["pool"]
    tok = ctx.grads(l, "B", dict(
        w_uq=_head_unpad_cols(g["w_uq"], QK_NOPE + QK_ROPE).reshape(Q_LORA, -1),
        w_ukv=jnp.concatenate([dk, dv_], axis=2).reshape(KV_LORA, -1), ssd_conv_w=ssd[0], lru_conv_w=lru[0],
        w_branch=jnp.stack([wb0, g["wb"][1], g["wb"][2], g["wb"][3]]), w_out=g["w_out"]))
    du_p = jnp.concatenate(dgates + [dpool, dz, dlg, dlx, dxbc, dcq, dkrope, dckv, ddt,
                                     jnp.zeros((rows, U_COLS - U_DT[0] - LANE), bf16)], axis=1)
    small = dict(
        q_norm=g["q_norm"][0], kv_norm=g["kv_norm"][0],
        w_pool=pool[0].reshape(4, LANE, LANE), pool_scale=pool[1][0],
        ssd_conv_b=ssd[1][0], ssd_dt_bias=ssd[2][0, :8], ssd_a_log=ssd[3][0, :8], ssd_d=ssd[4][0, :8], ssd_norm=ssd[5][0],
        lru_conv_b=lru[1][0], lru_w_a=_block_diag_inv(lru[2]), lru_b_a=lru[3][0], lru_w_i=_block_diag_inv(lru[4]),
        lru_b_i=lru[5][0], lru_lambda=lru[6][0], g_mlp=g["g_mlp"][0], g_ple=g["g_ple"][0])
    tok_small = ctx.small(f"l{l}", [(n, l, small[n]) for n in SMALL if n in small] + extra_small)
    g_w_in = matmul(nm("d_w_in"), r["h"], du_p, ta=True, outs=(bf16,), deps=[tok, tok_small])
    tok = ctx.grads(l, "A", dict(w_in=_w_in_blocks(g_w_in)))
    dx, g_mix = matmul(nm("d_h"), du_p, w["w_in"], tb=True, epi=_norm_bwd, extras=[(r["x"], 0), (dx1, 0)],
                       rows=[pr["g_mix"]], row_sums=1, deps=[tok])
    return dx, tok, ("g_mix", l, g_mix[0])


def _rope_tables(positions):
    inv = 1.0 / (ROPE_THETA ** (jnp.arange(0, QK_ROPE, 2, dtype=f32) / QK_ROPE))
    ang = positions.astype(f32)[:, None] * inv
    cos, sin = jnp.cos(ang), jnp.sin(ang)
    rows = positions.shape[0]
    pad = jnp.zeros((rows, LANE - KR_LANE - QK_ROPE), f32)
    cosf = jnp.concatenate([jnp.ones((rows, KR_LANE), f32), cos, cos, pad], axis=1)
    sinf = jnp.concatenate([jnp.zeros((rows, KR_LANE), f32), -sin, sin, pad], axis=1)
    return cosf, sinf


WEIGHTS = ['g_mix', 'w_in', 'q_norm', 'w_uq', 'kv_norm', 'w_ukv', 'w_pool', 'pool_scale', 'ssd_conv_w', 'ssd_conv_b',
           'ssd_dt_bias', 'ssd_a_log', 'ssd_d', 'ssd_norm', 'lru_conv_w', 'lru_conv_b', 'lru_w_a', 'lru_b_a', 'lru_w_i',
           'lru_b_i', 'lru_lambda', 'w_branch', 'w_out', 'g_mlp', 'w_ff1', 'w_ff2', 'g_ple', 'w_ple_gate', 'w_ple', 'g_final']
SHARDED = dict(w_in=2, w_uq=2, w_ukv=2, ssd_conv_w=2, lru_conv_w=2, w_branch=3, w_out=1, w_ff1=2, w_ff2=1,
               w_ple_gate=1, w_ple=2)
F32_PAYLOAD = ("ssd_conv_w", "lru_conv_w")
DEPTH = 2


SMALL = [n for n in WEIGHTS if n not in SHARDED and n != "g_final"]


def local_step(x, p, positions, tgt, sp, ctx):
    cosf, sinf = _rope_tables(positions)
    res, h = [], None
    for l in range(DEPTH):
        g_next = sp["g_mix"][l + 1][None, :] if l + 1 < DEPTH else None
        x, h, r = _layer_fwd(x, h, p[l].astype(bf16), ctx, l, _layer_params(sp, l), g_next, cosf, sinf)
        res.append(r)
    loss8, dx, dgf = loss_head(x, tgt, sp["g_final"][None, :], _pick(x.shape[0], 512))
    tok = None
    pending = ("g_final", None, dgf[0])
    for l in reversed(range(DEPTH)):
        dx, tok, pending = _layer_bwd(dx, res[l], ctx, l, cosf, sinf, tok, [pending])
    ctx.small("last", [pending])
    return loss8[0, 0], dx


def _payload(name, w):
    return w if name in F32_PAYLOAD else w.astype(bf16)


def _blocks(name, g):
    ax = SHARDED[name] - 1
    shape = list(g.shape)
    shape[ax:ax + 1] = [N_DEV, shape[ax] // N_DEV]
    return _payload(name, jnp.moveaxis(g.reshape(shape), ax, 0))


def _assemble(name, shards):
    ax = SHARDED[name] - 1
    shape = list(shards.shape[1:])
    shape[ax] *= N_DEV
    return jnp.moveaxis(shards, 0, ax).reshape(shape)


class _Exchanges:
    def __init__(self, wts):
        self.wts = wts
        self.ag, self.rs, self.sm = {}, {}, {}
        tok = None
        for l in range(DEPTH):
            for grp, names in GROUPS.items():
                h = exchange_start(f"ag_start_{grp}{l}", [_payload(n, wts[n][l]) for n in names], True,
                                   deps=[] if tok is None else [tok])
                tok = h["token"]
                self.ag[(l, grp)] = h
        self.all_started = tok

    def weights(self, l, grp, after):
        afters = list(after) if isinstance(after, (list, tuple)) else [after]
        if (l, grp) == (0, "A"):
            afters.append(self.all_started)
        got = exchange_wait(f"ag_wait_{grp}{l}", self.ag[(l, grp)], afters)
        out = {}
        for n, a in zip(GROUPS[grp], got):
            out[n] = a if n == "w_in" else _assemble(n, a)
        return out

    def grads(self, l, grp, g):
        cut = lambda n: g[n].ndim == self.wts[n].ndim
        h = exchange_start(f"rs_start_{grp}{l}", [g[n] if cut(n) else _blocks(n, g[n]) for n in GROUPS[grp]], False)
        self.rs[(l, grp)] = h
        return h["token"]

    def small(self, tag, entries):
        entries = sorted(entries, key=lambda e: e[2].size % LANE != 0)
        flat = jnp.concatenate([a.reshape(-1) for _, _, a in entries])
        flat = jnp.pad(flat, (0, (-flat.shape[0]) % (8 * LANE))).reshape(-1, LANE)
        h = exchange_start(f"small_start_{tag}", [flat], True)
        self.sm[tag] = (h, [(n, l, a.shape) for n, l, a in entries])
        return h["token"]

    def collect(self, groups, after):
        parts = {}
        for grp in groups:
            for l in reversed(range(DEPTH)):
                got = exchange_wait(f"rs_wait_{grp}{l}", self.rs[(l, grp)], [after])
                for n, a in zip(GROUPS[grp], got):
                    parts.setdefault(n, [None] * DEPTH)[l] = a
        return parts

    def collect_small(self, after):
        gots, where, parts = [], {}, {}
        for tag, (h, layout) in self.sm.items():
            (got,) = exchange_wait(f"small_wait_{tag}", h, [after])
            got = got.reshape(N_DEV, -1)
            off = 0
            for n, l, shape in layout:
                size = 1
                for d in shape:
                    size *= d
                if len(shape) == 1 and size % LANE == 0 and off % LANE == 0:
                    where.setdefault(n, [None] * (1 if l is None else DEPTH))[l or 0] = (len(gots), off)
                else:
                    part = got[:, off:off + size].reshape((N_DEV,) + tuple(shape))
                    if l is None:
                        parts[n] = [part]
                    else:
                        parts.setdefault(n, [None] * DEPTH)[l] = part
                off += size
            gots.append(got)
        return gots, where, parts


def kernel(x, p, positions, g_mix, w_in, q_norm, w_uq, kv_norm, w_ukv, w_pool, pool_scale, ssd_conv_w, ssd_conv_b,
           ssd_dt_bias, ssd_a_log, ssd_d, ssd_norm, lru_conv_w, lru_conv_b, lru_w_a, lru_b_a, lru_w_i, lru_b_i,
           lru_lambda, w_branch, w_out, g_mlp, w_ff1, w_ff2, g_ple, w_ple_gate, w_ple, g_final, loss_target, m_g_mix,
           m_w_in, m_q_norm, m_w_uq, m_kv_norm, m_w_ukv, m_w_pool, m_pool_scale, m_ssd_conv_w, m_ssd_conv_b,
           m_ssd_dt_bias, m_ssd_a_log, m_ssd_d, m_ssd_norm, m_lru_conv_w, m_lru_conv_b, m_lru_w_a, m_lru_b_a,
           m_lru_w_i, m_lru_b_i, m_lru_lambda, m_w_branch, m_w_out, m_g_mlp, m_w_ff1, m_w_ff2, m_g_ple, m_w_ple_gate,
           m_w_ple, m_g_final, v_g_mix, v_w_in, v_q_norm, v_w_uq, v_kv_norm, v_w_ukv, v_w_pool, v_pool_scale,
           v_ssd_conv_w, v_ssd_conv_b, v_ssd_dt_bias, v_ssd_a_log, v_ssd_d, v_ssd_norm, v_lru_conv_w, v_lru_conv_b,
           v_lru_w_a, v_lru_b_a, v_lru_w_i, v_lru_b_i, v_lru_lambda, v_w_branch, v_w_out, v_g_mlp, v_w_ff1, v_w_ff2,
           v_g_ple, v_w_ple_gate, v_w_ple, v_g_final):
    given = dict(locals())
    wts = {n: given[n] for n in WEIGHTS}
    ctx = _Exchanges(wts)
    loss, grad_x = local_step(x[0], p[:, 0], positions[0], loss_target[0], wts, ctx)

    def update(parts):
        out = {}
        for n, eight in parts.items():
            out[n] = adamw(f"adamw_{n}", eight, wts[n], given["m_" + n], given["v_" + n])
        return out

    outs = update(ctx.collect(("C", "B"), grad_x))
    late = outs["w_ff1"][1]
    outs.update(update(ctx.collect(("A",), late)))
    gots, where, parts = ctx.collect_small(late)
    outs.update(update(parts))
    names = sorted(where)
    rows = lambda a: a[None] if a.ndim == 1 else a
    res = adamw_packed("adamw_vectors", gots, [where[n] for n in names], [rows(wts[n]) for n in names],
                       [rows(given["m_" + n]) for n in names], [rows(given["v_" + n]) for n in names])
    for n, four in zip(names, res):
        outs[n] = [a[0] for a in four] if wts[n].ndim == 1 else four
    loss = lax.psum(loss, AXES)
    return (loss, grad_x[None], *[outs[n][0] for n in WEIGHTS], *[outs[n][1] for n in WEIGHTS],
            *[outs[n][2] for n in WEIGHTS], *[outs[n][3] for n in WEIGHTS])
```

```python
import functools

import jax
import jax.numpy as jnp
from jax import lax
from jax.experimental import pallas as pl
from jax.experimental.pallas import tpu as pltpu

f32 = jnp.float32
bf16 = jnp.bfloat16

D_MODEL = 1024
MIX = 512
N_HEADS = 8
QK_NOPE, QK_ROPE, V_HEAD = 64, 32, 64
Q_LORA, KV_LORA = 384, 256
ROPE_THETA = 10000.0
POOL_WINDOWS = (2, 4, 8, 16)
SSD_CHUNK = 128
SSD_XBC = 768
CONV_W = 4
LRU_C = 8.0
D_FF = 4096
EPS = 1e-6
IN_COLS = 7592
ADAM_LR, ADAM_B1, ADAM_B2, ADAM_EPS, ADAM_WD, ADAM_STEP = 0.001, 0.9, 0.999, 1e-08, 0.01, 10

LANE = 128
HALO = 8
POOL_HALO = 16
VMEM_LIMIT = 56 * 1024 * 1024
MATMUL_MAX_K_TILE = 4096
MATMUL_ACC_PASS_WEIGHT = 0.3
MATMUL_VMEM_BUDGET = 40 * 1024 * 1024
N_DEV = 8
AXES = ("x", "y", "c")

U_COLS = 8192
U_GATES, U_POOL, U_Z, U_LG, U_LX, U_XBC, U_CQ, U_KR, U_CKV, U_DT = (
    (0, 4096), (4096, 512), (4608, 512), (5120, 512), (5632, 512), (6144, 768),
    (6912, 384), (7296, 128), (7424, 256), (7680, 128))
KR_LANE = 64
U_DTYPE = bf16


def _cp(sem):
    return pltpu.CompilerParams(dimension_semantics=sem, vmem_limit_bytes=VMEM_LIMIT)


def _pick(dim, pref):
    if dim <= pref:
        return dim
    t = pref
    while t >= LANE:
        if dim % t == 0:
            return t
        t -= LANE
    t = pref
    while dim % t:
        t -= 8
    return t


@functools.partial(jax.custom_vjp, nondiff_argnums=(1,))
def shift_down(x, k):
    row = lax.broadcasted_iota(jnp.int32, x.shape, 0)
    return jnp.where(row >= k, pltpu.roll(x, k, 0), 0.0)


def _shift_down_fwd(x, k):
    return shift_down(x, k), None


def _shift_down_bwd(k, _, g):
    r = g.shape[0]
    row = lax.broadcasted_iota(jnp.int32, g.shape, 0)
    return (jnp.where(row < r - k, pltpu.roll(g, r - k, 0), 0.0),)


shift_down.defvjp(_shift_down_fwd, _shift_down_bwd)


@functools.partial(jax.custom_vjp, nondiff_argnums=(1,))
def lane_roll(x, s):
    return pltpu.roll(x, s, 1)


def _lane_roll_fwd(x, s):
    return lane_roll(x, s), None


def _lane_roll_bwd(s, _, g):
    return (pltpu.roll(g, (g.shape[1] - s) % g.shape[1], 1),)


lane_roll.defvjp(_lane_roll_fwd, _lane_roll_bwd)


def _tile_spec(tm, width, cb, n=None):
    if n is None:
        return pl.BlockSpec((tm, width), lambda i: (i, cb))
    return pl.BlockSpec((tm, width), lambda i: (n - 1 - i, cb))


def _const_spec(shape):
    nd = len(shape)
    return pl.BlockSpec(shape, lambda i: (0,) * nd)


def seq_fwd(name, f, params, tiles, carries, outs, tm):
    rows = tiles[0][0].shape[0]
    n = rows // tm
    np_, nt, no, nc = len(params), len(tiles), len(outs), len(carries)

    def body(*refs):
        p_refs = refs[:np_]
        t_refs = refs[np_:np_ + nt]
        o_refs = refs[np_ + nt:np_ + nt + no]
        s_refs = refs[np_ + nt + no:np_ + nt + no + nc]
        c_refs = refs[np_ + nt + no + nc:]
        i = pl.program_id(0)

        @pl.when(i == 0)
        def _():
            for c in c_refs:
                c[...] = jnp.zeros_like(c)

        cvals = [c[...] for c in c_refs]
        for s, c in zip(s_refs, cvals):
            s[0] = c
        o, newc = f(i, [r[...] for r in p_refs], cvals, [r[...].astype(f32) for r in t_refs])
        for r, v in zip(o_refs, o):
            r[...] = v.astype(r.dtype)
        for r, v in zip(c_refs, newc):
            r[...] = v

    in_specs = [_const_spec(p.shape) for p in params] + [_tile_spec(tm, w, cb) for (_, w, cb) in tiles]
    out_specs = [_tile_spec(tm, w, 0) for (w, _) in outs]
    out_specs += [pl.BlockSpec((1,) + tuple(c), lambda i, nd=len(c): (i,) + (0,) * nd) for c in carries]
    out_shape = [jax.ShapeDtypeStruct((rows, w), dt) for (w, dt) in outs]
    out_shape += [jax.ShapeDtypeStruct((n,) + tuple(c), f32) for c in carries]
    res = pl.pallas_call(
        body, name=name, grid=(n,), in_specs=in_specs, out_specs=out_specs, out_shape=out_shape,
        scratch_shapes=[pltpu.VMEM(tuple(c), f32) for c in carries],
        compiler_params=_cp(("arbitrary",)),
    )(*params, *[t[0] for t in tiles])
    return list(res[:no]), list(res[no:])


def seq_bwd(name, f, params, tiles, diff, saved, douts, gdtypes, tm):
    rows = tiles[0][0].shape[0]
    n = rows // tm
    np_, nt, nc, nd = len(params), len(tiles), len(saved), len(douts)
    didx = [k for k, d in enumerate(diff) if d]
    ng = len(didx)

    def body(*refs):
        p_refs = refs[:np_]
        t_refs = refs[np_:np_ + nt]
        s_refs = refs[np_ + nt:np_ + nt + nc]
        d_refs = refs[np_ + nt + nc:np_ + nt + nc + nd]
        pos = np_ + nt + nc + nd
        dp_refs = refs[pos:pos + np_]
        dt_refs = refs[pos + np_:pos + np_ + ng]
        dc_refs = refs[pos + np_ + ng:]
        i = pl.program_id(0)
        step = n - 1 - i

        @pl.when(i == 0)
        def _():
            for r in dp_refs:
                r[...] = jnp.zeros_like(r)
            for r in dc_refs:
                r[...] = jnp.zeros_like(r)

        pvals = [r[...] for r in p_refs]
        cvals = [r[0] for r in s_refs]
        xvals = [r[...].astype(f32) for r in t_refs]

        def fn(p, c, xd):
            x = list(xvals)
            for k, v in zip(didx, xd):
                x[k] = v
            return f(step, p, c, x)

        _, vjp = jax.vjp(fn, pvals, cvals, [xvals[k] for k in didx])
        dp, dc, dx = vjp(([r[...].astype(f32) for r in d_refs], [r[...] for r in dc_refs]))
        for r, v in zip(dp_refs, dp):
            r[...] += v
        for r, v in zip(dc_refs, dc):
            r[...] = v
        for r, v in zip(dt_refs, dx):
            r[...] = v.astype(r.dtype)

    in_specs = [_const_spec(p.shape) for p in params] + [_tile_spec(tm, w, cb, n) for (_, w, cb) in tiles]
    in_specs += [pl.BlockSpec((1,) + tuple(s.shape[1:]), lambda i, nd_=s.ndim - 1: (n - 1 - i,) + (0,) * nd_) for s in saved]
    in_specs += [_tile_spec(tm, d.shape[1], 0, n) for d in douts]
    args = list(params) + [t[0] for t in tiles] + list(saved) + list(douts)
    out_specs = [_const_spec(p.shape) for p in params] + [_tile_spec(tm, tiles[k][1], 0, n) for k in didx]
    out_shape = [jax.ShapeDtypeStruct(p.shape, f32) for p in params]
    out_shape += [jax.ShapeDtypeStruct((rows, tiles[k][1]), dt) for k, dt in zip(didx, gdtypes)]
    res = pl.pallas_call(
        body, name=name, grid=(n,), in_specs=in_specs, out_specs=out_specs, out_shape=out_shape,
        scratch_shapes=[pltpu.VMEM(tuple(s.shape[1:]), f32) for s in saved],
        compiler_params=_cp(("arbitrary",)),
    )(*args)
    return list(res[:np_]), list(res[np_:])


def _halvings(dim, lo, hi):
    t, out = _pick(dim, hi), []
    while t >= min(lo, dim) and dim % t == 0:
        out.append(t)
        if t % 2 or (t // 2) % 8:
            break
        t //= 2
    return out


def _matmul_tiles(m, n, k, a_item, b_item, per_out, max_tn=1024, whole_rows=False):
    def vmem_bytes(tm, tn, tk):
        acc = 4 if k // tk > 1 else 0
        return 2 * (tm * tk * a_item + tk * tn * b_item + tm * tn * per_out) + tm * tn * acc

    def traffic(tm, tn, tk):
        nk = k // tk
        return (m * k * a_item * (1 if nk == 1 else n // tn) + k * n * b_item * (m // tm)
                + (nk - 1) * m * n * 8 * MATMUL_ACC_PASS_WEIGHT)

    cands = [(traffic(tm, tn, tk), -tm * tn, tm, tn, tk)
             for tk in _halvings(k, 512, MATMUL_MAX_K_TILE) for tm in _halvings(m, 256, 4096)
             for tn in ([n] if whole_rows else _halvings(n, 512, min(1024, max_tn)))
             if vmem_bytes(tm, tn, tk) <= MATMUL_VMEM_BUDGET]
    return min(cands)[2:]


def matmul(name, a, b, *, ta=False, tb=False, outs=(f32,), epi=None, extras=(), rows=(), row_sums=0, deps=(),
           out_blocks=0):
    m, k = (a.shape[1], a.shape[0]) if ta else a.shape
    n = b.shape[0] if tb else b.shape[1]
    per_out = sum(jnp.dtype(dt).itemsize for dt in outs) + sum(e[0].dtype.itemsize for e in extras)
    whole_rows = bool(rows) or row_sums > 0
    tm, tn, tk = _matmul_tiles(m, n, k, a.dtype.itemsize, b.dtype.itemsize, per_out,
                               n // out_blocks if out_blocks else n, whole_rows)
    nk = k // tk
    ne, nr, nd, no = len(extras), len(rows), len(deps), len(outs)
    dims = (((0 if ta else 1,), (1 if tb else 0,)), ((), ()))

    def body(*refs):
        a_ref, b_ref = refs[0], refs[1]
        e_refs = refs[2:2 + ne]
        r_refs = refs[2 + ne:2 + ne + nr]
        o_refs = refs[2 + ne + nr + nd:2 + ne + nr + nd + no]
        s_refs = refs[2 + ne + nr + nd + no:2 + ne + nr + nd + no + row_sums]
        i, kk = pl.program_id(0), pl.program_id(2)
        part = lax.dot_general(a_ref[...].astype(bf16), b_ref[...].astype(bf16), dims, preferred_element_type=f32)

        def finish(total):
            res = (total,) if epi is None else epi(total, *[e[...] for e in e_refs], *[r[...] for r in r_refs])
            for r, v in zip(o_refs, res[:no]):
                r[...] = v.astype(r.dtype)
            for r, v in zip(s_refs, res[no:]):
                v8 = jnp.broadcast_to(v, r.shape)

                @pl.when(i == 0)
                def _(r=r, v8=v8):
                    r[...] = v8

                @pl.when(i > 0)
                def _(r=r, v8=v8):
                    r[...] += v8

        if nk == 1:
            finish(part)
            return
        acc = refs[-1]

        @pl.when(kk == 0)
        def _():
            acc[...] = part

        @pl.when(jnp.logical_and(kk > 0, kk < nk - 1))
        def _():
            acc[...] += part

        @pl.when(kk == nk - 1)
        def _():
            finish(acc[...] + part)

    a_spec = pl.BlockSpec((tk, tm), lambda i, j, q: (q, i)) if ta else pl.BlockSpec((tm, tk), lambda i, j, q: (i, q))
    b_spec = pl.BlockSpec((tn, tk), lambda i, j, q: (j, q)) if tb else pl.BlockSpec((tk, tn), lambda i, j, q: (q, j))
    assert all(off % tn == 0 for (_, off) in extras)
    e_specs = [pl.BlockSpec((tm, tn), lambda i, j, q, off=off // tn: (i, off + j)) for (_, off) in extras]
    r_specs = [pl.BlockSpec((1, tn), lambda i, j, q: (0, j)) for _ in rows]
    if out_blocks:
        per = n // out_blocks // tn
        out_spec = pl.BlockSpec((None, tm, tn), lambda i, j, q: (j // per, i, j % per))
        out_dims = (out_blocks, m, n // out_blocks)
    else:
        out_spec = pl.BlockSpec((tm, tn), lambda i, j, q: (i, j))
        out_dims = (m, n)
    res = pl.pallas_call(
        body, name=name, grid=(m // tm, n // tn, nk),
        in_specs=[a_spec, b_spec] + e_specs + r_specs + [pl.BlockSpec(memory_space=pl.ANY) for _ in deps],
        out_specs=[out_spec for _ in outs] + [pl.BlockSpec((8, tn), lambda i, j, q: (0, j))] * row_sums,
        out_shape=[jax.ShapeDtypeStruct(out_dims, dt) for dt in outs] + [jax.ShapeDtypeStruct((8, n), f32)] * row_sums,
        scratch_shapes=[pltpu.VMEM((tm, tn), f32)] if nk > 1 else [],
        compiler_params=_cp(("arbitrary" if row_sums else "parallel", "parallel", "arbitrary")),
    )(a, b, *[e[0] for e in extras], *rows, *deps)
    return res[0] if len(res) == 1 else tuple(res)


def merge_fwd(name, ys, wbs, u):
    rows, n_out = ys[0].shape[0], wbs[0].shape[1]
    nb = len(ys)
    tm, tn = _pick(rows, 512), _pick(n_out, 512)

    def body(*refs):
        y_refs, w_refs, g_refs = refs[:nb], refs[nb:2 * nb], refs[2 * nb:3 * nb]
        m_ref, p_refs = refs[3 * nb], refs[3 * nb + 1:]
        total = None
        for y_ref, w_ref, g_ref, p_ref in zip(y_refs, w_refs, g_refs, p_refs):
            pre = jnp.dot(y_ref[...], w_ref[...], preferred_element_type=f32)
            p_ref[...] = pre.astype(p_ref.dtype)
            term = jax.nn.sigmoid(g_ref[...].astype(f32)) * pre
            total = term if total is None else total + term
        m_ref[...] = total.astype(m_ref.dtype)

    in_specs = [pl.BlockSpec((tm, y.shape[1]), lambda i, j: (i, 0)) for y in ys]
    in_specs += [pl.BlockSpec((w.shape[0], tn), lambda i, j: (0, j)) for w in wbs]
    in_specs += [pl.BlockSpec((tm, tn), lambda i, j, off=n * (n_out // tn): (i, off + j)) for n in range(nb)]
    out_spec = pl.BlockSpec((tm, tn), lambda i, j: (i, j))
    res = pl.pallas_call(
        body, name=name, grid=(rows // tm, n_out // tn), in_specs=in_specs, out_specs=[out_spec] * (nb + 1),
        out_shape=[jax.ShapeDtypeStruct((rows, n_out), bf16)] * (nb + 1),
        compiler_params=_cp(("parallel", "parallel")),
    )(*ys, *wbs, *([u] * nb))
    return res[0], list(res[1:])


ATT_SCALE = (QK_NOPE + QK_ROPE) ** -0.5
LN2 = 0.6931471805599453
ATT_C = ATT_SCALE / LN2
NT = (((1,), (1,)), ((), ()))
TN = (((0,), (0,)), ((), ()))


def _causal(tq, tk):
    return lax.broadcasted_iota(jnp.int32, (tq, tk), 0) >= lax.broadcasted_iota(jnp.int32, (tq, tk), 1)


def _tri_pairs(n, by_column):
    if by_column:
        pairs = [(i, j) for j in range(n) for i in range(j, n)]
    else:
        pairs = [(i, j) for i in range(n) for j in range(i + 1)]
    return (jnp.asarray([a for a, _ in pairs], jnp.int32), jnp.asarray([b for _, b in pairs], jnp.int32))


FWD_HEADS_PER_STEP = 8
HEADS_PER_STEP = 4
HEAD_PAIR = HEADS_PER_STEP * LANE


def attn_fwd(q, k, v, t):
    rows = q.shape[0]
    n = rows // t
    it, jt = _tri_pairs(n, False)

    def body(it_ref, jt_ref, q_ref, k_ref, v_ref, o_ref, lse_ref, m_s, l_s, acc_s):
        s_id = pl.program_id(1)
        i, j = it_ref[s_id], jt_ref[s_id]

        @pl.when(j == 0)
        def _():
            m_s[...] = jnp.full_like(m_s, -jnp.inf)
            l_s[...] = jnp.zeros_like(l_s)
            acc_s[...] = jnp.zeros_like(acc_s)

        def step(diag):
            for hh in range(FWD_HEADS_PER_STEP):
                sl = slice(LANE * hh, LANE * (hh + 1))
                s = lax.dot_general(q_ref[:, sl], k_ref[:, sl], NT, preferred_element_type=f32)
                if diag:
                    s = jnp.where(_causal(t, t), s, -jnp.inf)
                m_prev = m_s[:, sl]
                m_new = jnp.maximum(m_prev, jnp.max(s, axis=1, keepdims=True))
                alpha = jnp.exp2(m_prev - m_new)
                p = jnp.exp2(s - m_new[:, :1])
                l_s[:, sl] = alpha * l_s[:, sl] + jnp.sum(p, axis=1, keepdims=True)
                acc_s[:, sl] = alpha * acc_s[:, sl] + jnp.dot(p.astype(bf16), v_ref[:, sl], preferred_element_type=f32)
                m_s[:, sl] = m_new

        pl.when(j < i)(lambda: step(False))

        @pl.when(j == i)
        def _():
            step(True)
            o_ref[...] = (acc_s[...] / l_s[...]).astype(o_ref.dtype)
            lse_ref[...] = m_s[...] + jnp.log2(l_s[...])

    width = FWD_HEADS_PER_STEP * LANE
    qs = pl.BlockSpec((t, width), lambda h, s, it_, jt_: (it_[s], h))
    ks = pl.BlockSpec((t, width), lambda h, s, it_, jt_: (jt_[s], h))
    hw = N_HEADS * LANE
    return pl.pallas_call(
        body, name="attn_fwd",
        grid_spec=pltpu.PrefetchScalarGridSpec(
            num_scalar_prefetch=2, grid=(hw // width, it.shape[0]), in_specs=[qs, ks, ks], out_specs=[qs, qs],
            scratch_shapes=[pltpu.VMEM((t, width), f32)] * 3),
        out_shape=[jax.ShapeDtypeStruct((rows, hw), bf16), jax.ShapeDtypeStruct((rows, hw), f32)],
        compiler_params=_cp(("parallel", "arbitrary")),
    )(it, jt, q, k, v)


def attn_bwd(q, k, v, do, o, lse, t):
    rows = q.shape[0]
    n = rows // t
    it, jt = _tri_pairs(n, True)

    def body(it_ref, jt_ref, q_ref, k_ref, v_ref, do_ref, o_ref, lse_ref, dq_ref, dk_ref, dv_ref, dk_s, dv_s):
        s_id = pl.program_id(1)
        i, j = it_ref[s_id], jt_ref[s_id]

        @pl.when(s_id == 0)
        def _():
            dq_ref[...] = jnp.zeros_like(dq_ref)

        @pl.when(i == j)
        def _():
            dk_s[...] = jnp.zeros_like(dk_s)
            dv_s[...] = jnp.zeros_like(dv_s)

        q_rows = pl.ds(pl.multiple_of(i * t, t), t)

        def step(diag):
            for hh in range(HEADS_PER_STEP):
                sl = slice(LANE * hh, LANE * (hh + 1))
                qh, kh, vh, doh = q_ref[:, sl], k_ref[:, sl], v_ref[:, sl], do_ref[:, sl]
                s = lax.dot_general(qh, kh, NT, preferred_element_type=f32)
                p = jnp.exp2(s - lse_ref[:, sl][:, :1])
                if diag:
                    p = jnp.where(_causal(t, t), p, 0.0)
                dp = lax.dot_general(doh, vh, NT, preferred_element_type=f32)
                delta = jnp.sum(doh.astype(f32) * o_ref[:, sl].astype(f32), axis=1, keepdims=True)
                ds = (p * (dp - delta) * LN2).astype(bf16)
                dv_s[:, sl] += lax.dot_general(p.astype(bf16), doh, TN, preferred_element_type=f32)
                dk_s[:, sl] += lax.dot_general(ds, qh, TN, preferred_element_type=f32)
                dq_ref[q_rows, sl] += jnp.dot(ds, kh, preferred_element_type=f32)

        pl.when(i > j)(lambda: step(False))
        pl.when(i == j)(lambda: step(True))

        @pl.when(i == n - 1)
        def _():
            dk_ref[...] = dk_s[...]
            dv_ref[...] = dv_s[...]

    qs = pl.BlockSpec((t, HEAD_PAIR), lambda h, s, it_, jt_: (it_[s], h))
    ks = pl.BlockSpec((t, HEAD_PAIR), lambda h, s, it_, jt_: (jt_[s], h))
    dqs = pl.BlockSpec((rows, HEAD_PAIR), lambda h, s, it_, jt_: (0, h))
    hw = N_HEADS * LANE
    return pl.pallas_call(
        body, name="attn_bwd",
        grid_spec=pltpu.PrefetchScalarGridSpec(
            num_scalar_prefetch=2, grid=(hw // HEAD_PAIR, it.shape[0]), in_specs=[qs, ks, ks, qs, qs, qs],
            out_specs=[dqs, ks, ks], scratch_shapes=[pltpu.VMEM((t, HEAD_PAIR), f32)] * 2),
        out_shape=[jax.ShapeDtypeStruct((rows, hw), f32)] * 3,
        compiler_params=_cp(("parallel", "arbitrary")),
    )(it, jt, q, k, v, do, o, lse)


def _steps(tm):
    k, out = 1, []
    while k < tm:
        out.append(k)
        k *= 2
    return out


def scan_fwd(a, u, tm):
    rows, ch = a.shape
    n = rows // tm

    def body(a_ref, u_ref, h_ref, h_s):
        @pl.when(pl.program_id(0) == 0)
        def _():
            h_s[...] = jnp.zeros_like(h_s)

        av, bv = a_ref[...], u_ref[...]
        row = lax.broadcasted_iota(jnp.int32, av.shape, 0)
        for k in _steps(tm):
            a_sh = jnp.where(row >= k, pltpu.roll(av, k, 0), 1.0)
            b_sh = jnp.where(row >= k, pltpu.roll(bv, k, 0), 0.0)
            bv = av * b_sh + bv
            av = av * a_sh
        h = bv + av * h_s[HALO - 1:HALO, :]
        h_ref[...] = h
        h_s[...] = h[tm - HALO:, :]

    spec = pl.BlockSpec((tm, ch), lambda i: (i, 0))
    return pl.pallas_call(
        body, name="lru_scan_fwd", grid=(n,), in_specs=[spec, spec], out_specs=spec,
        out_shape=jax.ShapeDtypeStruct((rows, ch), f32), scratch_shapes=[pltpu.VMEM((HALO, ch), f32)],
        compiler_params=_cp(("arbitrary",)),
    )(a, u)


def scan_bwd(a, h, dh, tm):
    rows, ch = a.shape
    n = rows // tm
    per = tm // HALO

    def body(a_ref, h_ref, hp_ref, dh_ref, da_ref, du_ref, g_s, a_s):
        i = pl.program_id(0)
        step = n - 1 - i

        @pl.when(i == 0)
        def _():
            g_s[...] = jnp.zeros_like(g_s)
            a_s[...] = jnp.zeros_like(a_s)

        a0 = a_ref[...]
        row = lax.broadcasted_iota(jnp.int32, a0.shape, 0)
        av = jnp.where(row < tm - 1, pltpu.roll(a0, tm - 1, 0), a_s[0:1, :])
        bv = dh_ref[...]
        for k in _steps(tm):
            a_sh = jnp.where(row < tm - k, pltpu.roll(av, tm - k, 0), 1.0)
            b_sh = jnp.where(row < tm - k, pltpu.roll(bv, tm - k, 0), 0.0)
            bv = bv + av * b_sh
            av = av * a_sh
        g = bv + av * g_s[0:1, :]
        h_last = jnp.where(step > 0, hp_ref[HALO - 1:HALO, :], 0.0)
        h_prev = jnp.where(row >= 1, pltpu.roll(h_ref[...], 1, 0), h_last)
        du_ref[...] = g
        da_ref[...] = g * h_prev
        g_s[...] = g[0:HALO, :]
        a_s[...] = a0[0:HALO, :]

    spec = pl.BlockSpec((tm, ch), lambda i: (n - 1 - i, 0))
    hp_spec = pl.BlockSpec((HALO, ch), lambda i: (jnp.maximum((n - 1 - i) * per - 1, 0), 0))
    return pl.pallas_call(
        body, name="lru_scan_bwd", grid=(n,), in_specs=[spec, spec, hp_spec, spec], out_specs=[spec, spec],
        out_shape=[jax.ShapeDtypeStruct((rows, ch), f32)] * 2,
        scratch_shapes=[pltpu.VMEM((HALO, ch), f32)] * 2,
        compiler_params=_cp(("arbitrary",)),
    )(a, h, h, dh)


def _rms(x, g):
    return x * lax.rsqrt(jnp.mean(x * x, axis=-1, keepdims=True) + EPS) * g


def f_rms(step, p, c, x):
    return [_rms(x[0], p[0])], []


def _rope(x, cosf, sinf):
    lane = lax.broadcasted_iota(jnp.int32, x.shape, 1)
    sw = jnp.where(lane < KR_LANE + QK_ROPE // 2, lane_roll(x, LANE - QK_ROPE // 2), lane_roll(x, QK_ROPE // 2))
    return x * cosf + sw * sinf


def f_prep(step, p, c, x):
    q, kn, kr, cosf, sinf = x
    kr_rot = _rope(kr, cosf, sinf)
    qr = [_rope(q[:, LANE * h:LANE * (h + 1)], cosf, sinf) * ATT_C for h in range(N_HEADS)]
    kk = [kn[:, LANE * h:LANE * (h + 1)] + kr_rot for h in range(N_HEADS)]
    return [jnp.concatenate(qr, axis=1), jnp.concatenate(kk, axis=1)], []


def _conv(tail, x, w, b):
    xf = jnp.concatenate([tail, x], axis=0)
    acc = b + w[CONV_W - 1:CONV_W, :] * xf
    for k in range(CONV_W - 1):
        acc = acc + w[k:k + 1, :] * shift_down(xf, CONV_W - 1 - k)
    return acc[HALO:, :]


def f_pool(step, p, c, x):
    wp, sc = p
    (tail,) = c
    (u,) = x
    tm = u.shape[0]
    xf = jnp.concatenate([tail, u], axis=0)
    sums, s, w = [], xf, 1
    while w < POOL_WINDOWS[-1]:
        s = s + shift_down(s, w)
        w *= 2
        sums.append(s)
    t = step * tm + lax.broadcasted_iota(jnp.int32, (tm, 1), 0)
    ys = []
    for g, (w, s) in enumerate(zip(POOL_WINDOWS, sums)):
        sl = slice(LANE * g, LANE * (g + 1))
        cnt = jnp.minimum(t + 1, w).astype(f32)
        d = s[POOL_HALO:, sl] / cnt - u[:, sl]
        ys.append(jnp.dot(d.astype(bf16), wp[LANE * g:LANE * (g + 1), :].astype(bf16), preferred_element_type=f32))
    return [jnp.concatenate(ys, axis=1) * sc], [u[tm - POOL_HALO:, :]]


def f_ssd(step, p, c, x):
    conv_w, conv_b, dtb, alog, dsk, ng = p
    tail, s_in = c[0], c[1:]
    z, xbc, dt = x
    ln = z.shape[0]
    xc = jax.nn.silu(_conv(tail, xbc, conv_w, conv_b))
    xs, bb, cc = xc[:, :MIX], xc[:, MIX:MIX + LANE], xc[:, MIX + LANE:]
    dtv = jax.nn.softplus(dt + dtb[0:1, :])
    a = dtv * -jnp.exp(alog[0:1, :])
    ri = lax.broadcasted_iota(jnp.int32, (ln, ln), 0)
    ci = lax.broadcasted_iota(jnp.int32, (ln, ln), 1)
    tril = (ri >= ci).astype(f32)
    triu = (ri <= ci).astype(f32)
    hi = lax.Precision.HIGHEST
    a_cs = jnp.dot(tril, a, precision=hi, preferred_element_type=f32)
    a_cs_t = lax.dot_general(a, triu, TN, precision=hi, preferred_element_type=f32)
    a_tot = jnp.sum(a, axis=0, keepdims=True)
    lane = lax.broadcasted_iota(jnp.int32, (1, LANE), 1)
    half = [(lane < 64).astype(f32), (lane >= 64).astype(f32)]
    hrow = lax.broadcasted_iota(jnp.int32, (LANE, 1), 0)

    def head(v, h):
        return jnp.sum(v * (lane == h).astype(f32), axis=1, keepdims=True)

    def pair(v, j):
        return head(v, 2 * j) * half[0] + head(v, 2 * j + 1) * half[1]

    cg = [(cc * half[g]).astype(bf16) for g in range(2)]
    bg = [(bb * half[g]).astype(bf16) for g in range(2)]
    cb = [lax.dot_general(cg[g], bg[g], NT, preferred_element_type=f32) for g in range(2)]
    ys, s_out = [], []
    for j in range(4):
        g = j // 2
        xs_j = xs[:, LANE * j:LANE * (j + 1)]
        xj = xs_j * pair(dtv, j)
        yj = xs_j * pair(dsk[0:1, :], j)
        for hh in range(2):
            h = 2 * j + hh
            rowv = jnp.sum(a_cs_t * (hrow == h).astype(f32), axis=0, keepdims=True)
            lmat = jnp.exp(jnp.where(ri >= ci, head(a_cs, h) - rowv, -jnp.inf))
            yj = yj + jnp.dot((cb[g] * lmat).astype(bf16), (xj * half[hh]).astype(bf16), preferred_element_type=f32)
        acs = pair(a_cs, j)
        tot = pair(a_tot, j)
        yj = yj + jnp.exp(acs) * jnp.dot(cg[g], s_in[j].astype(bf16), preferred_element_type=f32)
        s_new = jnp.exp(tot) * s_in[j] + lax.dot_general(bg[g], (xj * jnp.exp(tot - acs)).astype(bf16), TN,
                                                         preferred_element_type=f32)
        ys.append(yj)
        s_out.append(s_new)
    y = jnp.concatenate(ys, axis=1) * jax.nn.silu(z)
    return [_rms(y, ng)], [xbc[ln - HALO:, :]] + s_out


def _neg_expm1(y):
    series = -y * (1.0 + y * (0.5 + y * (1.0 / 6 + y * (1.0 / 24 + y * (1.0 / 120)))))
    return jnp.where(y > -0.05, series, 1.0 - jnp.exp(y))


def f_lru_pre(step, p, c, x):
    cw, cb_, wa, ba, wi, bi, lam = p
    (tail,) = c
    (lx,) = x
    tm = lx.shape[0]
    xc = _conv(tail, lx, cw, cb_)
    xb = xc.astype(bf16)
    r = jax.nn.sigmoid(jnp.dot(xb, wa.astype(bf16), preferred_element_type=f32) + ba)
    it = jax.nn.sigmoid(jnp.dot(xb, wi.astype(bf16), preferred_element_type=f32) + bi)
    log_a = -LRU_C * r * jax.nn.softplus(-lam)
    mult = jnp.sqrt(_neg_expm1(2.0 * log_a))
    return [jnp.exp(log_a), xc * it * mult], [lx[tm - HALO:, :]]


def f_lru_post(step, p, c, x):
    h, g = x
    return [h * jax.nn.gelu(g)], []


def loss_head(x, tgt, g, tm):
    rows, d = x.shape
    n = rows // tm

    def body(x_ref, t_ref, g_ref, loss_ref, dx_ref, dg_ref):
        @pl.when(pl.program_id(0) == 0)
        def _():
            loss_ref[...] = jnp.zeros_like(loss_ref)
            dg_ref[...] = jnp.zeros_like(dg_ref)

        def fn(gv, xv):
            err = _rms(xv, gv) - t_ref[...]
            return 0.5 * jnp.sum(jnp.mean(err * err, axis=-1, keepdims=True))

        val, (dg, dx) = jax.value_and_grad(fn, argnums=(0, 1))(g_ref[...], x_ref[...])
        loss_ref[...] += val
        dg_ref[...] += dg
        dx_ref[...] = dx

    spec = pl.BlockSpec((tm, d), lambda i: (i, 0))
    return pl.pallas_call(
        body, name="loss_head", grid=(n,), in_specs=[spec, spec, _const_spec((1, d))],
        out_specs=[_const_spec((8, LANE)), spec, _const_spec((1, d))],
        out_shape=[jax.ShapeDtypeStruct((8, LANE), f32), jax.ShapeDtypeStruct((rows, d), f32),
                   jax.ShapeDtypeStruct((1, d), f32)],
        compiler_params=_cp(("arbitrary",)),
    )(x, tgt, g)


def ew(name, fn, ins, outs, tm):
    rows = ins[0][0].shape[0]
    ni = len(ins)

    def body(*refs):
        res = fn(*[r[...].astype(f32) for r in refs[:ni]])
        for r, v in zip(refs[ni:], res):
            r[...] = v.astype(r.dtype)

    return pl.pallas_call(
        body, name=name, grid=(rows // tm,), in_specs=[_tile_spec(tm, w, cb) for (_, w, cb) in ins],
        out_specs=[_tile_spec(tm, w, 0) for (w, _) in outs],
        out_shape=[jax.ShapeDtypeStruct((rows, w), dt) for (w, dt) in outs],
        compiler_params=_cp(("parallel",)),
    )(*[t[0] for t in ins])


def _peers():
    x, y, c = lax.axis_index("x"), lax.axis_index("y"), lax.axis_index("c")
    me = 4 * x + 2 * y + c
    out = []
    for k in range(1, N_DEV):
        px = 1 - x if k & 4 else x
        py = 1 - y if k & 2 else y
        pc = 1 - c if k & 1 else c
        out.append(((px, py, pc), 4 * px + 2 * py + pc))
    return me, out


_HBM = pl.BlockSpec(memory_space=pltpu.HBM)
_SEM = pl.BlockSpec(memory_space=pltpu.SEMAPHORE)
_EFFECT = pltpu.SideEffectType.DATAFLOW_SIDE_EFFECTING


def _remote(src_ref, land_ref, gather, me, pid, dev, send_sems, recv_sems, k, recv_side):
    return pltpu.make_async_remote_copy(
        src_ref=src_ref if gather else src_ref.at[pid], dst_ref=land_ref.at[pid if recv_side else me],
        send_sem=send_sems.at[k], recv_sem=recv_sems.at[k], device_id=dev, device_id_type=pl.DeviceIdType.MESH)


def _own(src_ref, land_ref, gather, me, sem):
    return pltpu.make_async_copy(src_ref if gather else src_ref.at[me], land_ref.at[me], sem)


def exchange_start(name, srcs, gather, deps=()):
    n, nd = len(srcs), len(deps)
    shapes = [(s.shape if gather else s.shape[1:]) for s in srcs]
    lands = [lax.empty((N_DEV,) + tuple(sh), s.dtype) for s, sh in zip(srcs, shapes)]

    def body(*refs):
        src_refs, land_refs = refs[:n], refs[n:2 * n]
        send_sems, recv_sems, own_sem = refs[2 * n + nd:2 * n + nd + 3]
        token = refs[-1]
        me, peers = _peers()
        for k, (dev, pid) in enumerate(peers):
            for s_ref, l_ref in zip(src_refs, land_refs):
                _remote(s_ref, l_ref, gather, me, pid, dev, send_sems, recv_sems, k, False).start()
        for s_ref, l_ref in zip(src_refs, land_refs):
            _own(s_ref, l_ref, gather, me, own_sem).start()
        token[...] = jnp.zeros_like(token)

    hbm = lambda a: pltpu.with_memory_space_constraint(a, pltpu.HBM)
    res = pl.pallas_call(
        body, name=name,
        out_shape=(pltpu.SemaphoreType.DMA((N_DEV - 1,)), pltpu.SemaphoreType.DMA((N_DEV - 1,)), pltpu.SemaphoreType.DMA(()),
                   *[pltpu.HBM(a.shape, a.dtype) for a in list(srcs) + lands], jax.ShapeDtypeStruct((8, LANE), f32)),
        in_specs=[_HBM] * (2 * n) + [pl.BlockSpec(memory_space=pl.ANY)] * nd,
        out_specs=(_SEM, _SEM, _SEM, *([_HBM] * (2 * n)), pl.BlockSpec(memory_space=pltpu.VMEM)),
        input_output_aliases={i: 3 + i for i in range(2 * n)},
        compiler_params=pltpu.CompilerParams(has_side_effects=_EFFECT),
    )(*[hbm(a) for a in list(srcs) + lands], *deps)
    return dict(sems=res[:3], srcs=list(res[3:3 + n]), lands=list(res[3 + n:3 + 2 * n]), token=res[-1], gather=gather)


def exchange_wait(name, h, afters):
    n, gather = len(h["srcs"]), h["gather"]

    def body(*refs):
        src_refs, land_refs = refs[:n], refs[n:2 * n]
        send_sems, recv_sems, own_sem = refs[2 * n:2 * n + 3]
        me, peers = _peers()
        for k, (dev, pid) in enumerate(peers):
            for s_ref, l_ref in zip(src_refs, land_refs):
                _remote(s_ref, l_ref, gather, me, pid, dev, send_sems, recv_sems, k, True).wait_recv()
        for k, (dev, pid) in enumerate(peers):
            for s_ref, l_ref in zip(src_refs, land_refs):
                _remote(s_ref, l_ref, gather, me, pid, dev, send_sems, recv_sems, k, False).wait_send()
        for s_ref, l_ref in zip(src_refs, land_refs):
            _own(s_ref, l_ref, gather, me, own_sem).wait()

    arrs = h["srcs"] + h["lands"]
    res = pl.pallas_call(
        body, name=name, out_shape=tuple(pltpu.HBM(a.shape, a.dtype) for a in arrs),
        in_specs=[_HBM] * (2 * n) + [_SEM, _SEM, _SEM] + [pl.BlockSpec(memory_space=pl.ANY)] * len(afters),
        out_specs=tuple([_HBM] * (2 * n)), input_output_aliases={i: i for i in range(2 * n)},
        compiler_params=pltpu.CompilerParams(has_side_effects=_EFFECT),
    )(*arrs, *h["sems"], *afters)
    return list(res[n:])


def _adam_update(g, w, m, v):
    mn = ADAM_B1 * m + (1.0 - ADAM_B1) * g
    vn = ADAM_B2 * v + (1.0 - ADAM_B2) * jnp.square(g)
    m_hat = mn / (1.0 - ADAM_B1 ** ADAM_STEP)
    v_hat = vn / (1.0 - ADAM_B2 ** ADAM_STEP)
    return -ADAM_LR * (m_hat / (jnp.sqrt(v_hat) + ADAM_EPS) + ADAM_WD * w), mn, vn


def _adamw_vectors(name, parts, w, m, v):
    nl = len(parts)

    def body(*refs):
        p_refs = refs[:nl]
        w_ref, m_ref, v_ref, g_ref, d_ref, nm_ref, nv_ref = refs[nl:]
        for ll, p_ref in enumerate(p_refs):
            row = slice(ll, ll + 1)
            g = p_ref[0:1, :]
            for i in range(1, N_DEV):
                g = g + p_ref[i:i + 1, :]
            delta, mn, vn = _adam_update(g, w_ref[row, :], m_ref[row, :], v_ref[row, :])
            g_ref[row, :] = g
            d_ref[row, :] = delta
            nm_ref[row, :] = mn
            nv_ref[row, :] = vn

    return list(pl.pallas_call(body, name=name, out_shape=[jax.ShapeDtypeStruct(w.shape, f32)] * 4)(*parts, w, m, v))


def adamw_packed(name, gots, where, ws, ms, vs):
    ng, npar = len(gots), len(ws)

    def body(*refs):
        g_refs = refs[:ng]
        w_refs, m_refs, v_refs = (refs[ng + k * npar:ng + (k + 1) * npar] for k in range(3))
        o_refs = refs[ng + 3 * npar:]
        for p in range(npar):
            width = w_refs[p].shape[1]
            for l, (which, off) in enumerate(where[p]):
                row = slice(l, l + 1)
                cols = slice(off, off + width)
                g = g_refs[which][0:1, cols]
                for i in range(1, N_DEV):
                    g = g + g_refs[which][i:i + 1, cols]
                delta, mn, vn = _adam_update(g, w_refs[p][row, :], m_refs[p][row, :], v_refs[p][row, :])
                for k, val in enumerate((g, delta, mn, vn)):
                    o_refs[4 * p + k][row, :] = val

    out_shape = [jax.ShapeDtypeStruct(w.shape, f32) for w in ws for _ in range(4)]
    res = pl.pallas_call(body, name=name, out_shape=out_shape, compiler_params=_cp(()))(*gots, *ws, *ms, *vs)
    return [list(res[4 * p:4 * p + 4]) for p in range(npar)]


def adamw_columns(name, parts, w, m, v):
    nl, kk, cc = w.shape
    view = lambda a: jnp.transpose(a, (2, 0, 1)).reshape(cc * nl, kk)
    tc = min(LANE, cc)
    nq = kk // LANE

    def body(*refs):
        p_refs = refs[:nl]
        w_ref, m_ref, v_ref, g_ref, d_ref, nm_ref, nv_ref, g_s = refs[nl:]
        for l, p_ref in enumerate(p_refs):
            g = p_ref[0].astype(f32)
            for i in range(1, N_DEV):
                g = g + p_ref[i].astype(f32)
            gt = g.T
            for q in range(nq):
                g_s.at[q][pl.ds(l, tc, stride=nl), :] = gt[:, LANE * q:LANE * (q + 1)]
        g = jnp.concatenate([g_s[q] for q in range(nq)], axis=1)
        delta, mn, vn = _adam_update(g, w_ref[...], m_ref[...], v_ref[...])
        g_ref[...] = g
        d_ref[...] = delta
        nm_ref[...] = mn
        nv_ref[...] = vn

    p_spec = pl.BlockSpec((N_DEV, kk, tc), lambda j: (0, 0, j))
    w_spec = pl.BlockSpec((nl * tc, kk), lambda j: (j, 0))
    res = pl.pallas_call(
        body, name=name, grid=(pl.cdiv(cc, tc),), in_specs=[p_spec] * nl + [w_spec] * 3, out_specs=[w_spec] * 4,
        out_shape=[jax.ShapeDtypeStruct((cc * nl, kk), f32)] * 4, scratch_shapes=[pltpu.VMEM((nq, nl * tc, LANE), f32)],
        compiler_params=_cp(("parallel",)),
    )(*parts, view(w), view(m), view(v))
    return [jnp.transpose(a.reshape(cc, nl, kk), (1, 2, 0)) for a in res]


def adamw(name, parts, w, m, v):
    nl = len(parts)
    shape = w.shape[1:]
    c = shape[-1]
    r = 1
    for s in shape[:-1]:
        r *= s
    if r == 1:
        return _adamw_vectors(name, parts, w, m, v)
    tr = _pick(r, 256) if r % 8 == 0 else r
    nb = r // tr
    parts2 = [p.reshape(N_DEV, r, c) for p in parts]
    w2, m2, v2 = (a.reshape(nl, r, c) for a in (w, m, v))

    def body(*refs):
        p_refs = refs[:nl]
        w_ref, m_ref, v_ref, g_ref, d_ref, nm_ref, nv_ref = refs[nl:]
        layer = pl.program_id(0)
        for ll, p_ref in enumerate(p_refs):
            @pl.when(layer == ll)
            def _(p_ref=p_ref):
                g = p_ref[0].astype(f32)
                for i in range(1, N_DEV):
                    g = g + p_ref[i].astype(f32)
                delta, mn, vn = _adam_update(g, w_ref[0], m_ref[0], v_ref[0])
                g_ref[0] = g
                d_ref[0] = delta
                nm_ref[0] = mn
                nv_ref[0] = vn

    def p_spec(ll):
        return pl.BlockSpec((N_DEV, tr, c), lambda l, i: (0, jnp.where(l == ll, i, jnp.where(l > ll, nb - 1, 0)), 0))

    spec = pl.BlockSpec((1, tr, c), lambda l, i: (l, i, 0))
    res = pl.pallas_call(
        body, name=name, grid=(nl, nb), in_specs=[p_spec(ll) for ll in range(nl)] + [spec, spec, spec],
        out_specs=[spec] * 4, out_shape=[jax.ShapeDtypeStruct((nl, r, c), f32)] * 4,
        compiler_params=_cp(("arbitrary", "arbitrary")),
    )(*parts2, w2, m2, v2)
    return [a.reshape(w.shape) for a in res]


_IN_SPLITS = dict(cq=(0, 384), ckv=(384, 640), kr=(640, 672), pool=(672, 1184), z=(1184, 1696), xbc=(1696, 2464),
                  dt=(2464, 2472), lg=(2472, 2984), lx=(2984, 3496), gates=(3496, 7592))


W_IN_SHARD = IN_COLS // N_DEV

_PAD_ORDER = ("gates", "pool", "z", "lg", "lx", "xbc", "cq", KR_LANE, "kr", LANE - KR_LANE - QK_ROPE, "ckv", "dt",
              LANE - 8, U_COLS - U_DT[0] - LANE)
_SEGMENTS = ((0, U_CQ[0], 384), (384, U_CKV[0], 256), (640, U_KR[0] + KR_LANE, QK_ROPE), (672, U_POOL[0], 512),
             (1184, U_Z[0], 512), (1696, U_XBC[0], 768), (2464, U_DT[0], 8), (2472, U_LG[0], 512), (2984, U_LX[0], 512),
             (3496, 0, 4096))


def _pad_w_in(shards):
    rows = shards.shape[1]
    pieces = []
    for item in _PAD_ORDER:
        if isinstance(item, int):
            pieces.append(jnp.zeros((rows, item), shards.dtype))
            continue
        a, b = _IN_SPLITS[item]
        for d in range(a // W_IN_SHARD, (b - 1) // W_IN_SHARD + 1):
            lo, hi = max(a, d * W_IN_SHARD), min(b, (d + 1) * W_IN_SHARD)
            pieces.append(shards[d, :, lo - d * W_IN_SHARD:hi - d * W_IN_SHARD])
    return jnp.concatenate(pieces, axis=1)


def _w_in_blocks(g):
    blocks = []
    for d in range(N_DEV):
        a, b = d * W_IN_SHARD, (d + 1) * W_IN_SHARD
        pieces = []
        for ref, pad, width in _SEGMENTS:
            lo, hi = max(a, ref), min(b, ref + width)
            if lo < hi:
                pieces.append(g[:, pad + lo - ref:pad + hi - ref])
        blocks.append(jnp.concatenate(pieces, axis=1))
    return jnp.stack(blocks).astype(bf16)


def _head_pad_cols(w, per, lo, hi):
    k = w.shape[0]
    w = w.reshape(k, N_HEADS, per)[:, :, lo:hi]
    return jnp.pad(w, ((0, 0), (0, 0), (0, LANE - (hi - lo)))).reshape(k, N_HEADS * LANE)


def _head_unpad_cols(g, n):
    k = g.shape[0]
    return g.reshape(k, N_HEADS, LANE)[:, :, :n]


def _on_diagonal():
    i = lax.broadcasted_iota(jnp.int32, (8, 1, 8, 1), 0)
    j = lax.broadcasted_iota(jnp.int32, (8, 1, 8, 1), 2)
    return i == j


def _block_diag(w):
    w4 = jnp.broadcast_to(w[:, :, None, :], (8, 64, 8, 64))
    return jnp.where(_on_diagonal(), w4, 0.0).reshape(MIX, MIX)


def _block_diag_inv(g):
    return jnp.sum(jnp.where(_on_diagonal(), g.reshape(8, 64, 8, 64), 0.0), axis=2)


def _head8(v):
    return jnp.pad(v[None, :], ((0, 7), (0, LANE - v.shape[0])))


GROUPS = dict(A=("w_in",), B=("w_uq", "w_ukv", "ssd_conv_w", "lru_conv_w", "w_branch", "w_out"),
              C=("w_ff1", "w_ff2", "w_ple_gate", "w_ple"))


def _kernel_weights(grp, fw):
    if grp == "A":
        w_in = _pad_w_in(fw["w_in"])
        return dict(w_in=w_in, w_dt=w_in[:, U_DT[0]:U_DT[0] + LANE])
    if grp == "C":
        return dict(w_ff1=fw["w_ff1"], w_ff2=fw["w_ff2"], w_pg=fw["w_ple_gate"], w_ple=fw["w_ple"])
    wb = fw["w_branch"]
    wb0 = jnp.pad(wb[0].reshape(N_HEADS, V_HEAD, D_MODEL), ((0, 0), (0, LANE - V_HEAD), (0, 0))).reshape(N_HEADS * LANE, D_MODEL)
    return dict(
        w_uq=_head_pad_cols(fw["w_uq"], QK_NOPE + QK_ROPE, 0, QK_NOPE + QK_ROPE),
        w_uk=_head_pad_cols(fw["w_ukv"], QK_NOPE + V_HEAD, 0, QK_NOPE),
        w_uv=_head_pad_cols(fw["w_ukv"], QK_NOPE + V_HEAD, QK_NOPE, QK_NOPE + V_HEAD),
        wb=[wb0, wb[1], wb[2], wb[3]], w_out=fw["w_out"], ssd_conv_w=fw["ssd_conv_w"], lru_conv_w=fw["lru_conv_w"])


def _layer_params(sp, l):
    row = lambda n: sp[n][l][None, :]
    return dict(
        g_mix=row("g_mix"), q_norm=row("q_norm"), kv_norm=row("kv_norm"),
        pool=[sp["w_pool"][l].reshape(4 * LANE, LANE), row("pool_scale")],
        ssd=[None, row("ssd_conv_b"), _head8(sp["ssd_dt_bias"][l]), _head8(sp["ssd_a_log"][l]),
             _head8(sp["ssd_d"][l]), row("ssd_norm")],
        lru=[None, row("lru_conv_b"), _block_diag(sp["lru_w_a"][l]), row("lru_b_a"),
             _block_diag(sp["lru_w_i"][l]), row("lru_b_i"), row("lru_lambda")],
        g_mlp=row("g_mlp"), g_ple=row("g_ple"),
    )


_sig = jax.nn.sigmoid
_SSD_CARRY = [(HALO, SSD_XBC)] + [(LANE, LANE)] * 4


def _tiles(rows):
    return dict(tm=_pick(rows, 512), ta=_pick(rows, 512), tp=_pick(rows, 512), tl=_pick(rows, 512), ts=_pick(rows, 256))


def _mixer_tiles(u, dt32):
    return dict(
        cq=(u, 384, U_CQ[0] // 384), ckv=(u, 256, U_CKV[0] // 256), kr=(u, LANE, U_KR[0] // LANE),
        pool=(u, MIX, U_POOL[0] // MIX), z=(u, MIX, U_Z[0] // MIX), xbc=(u, SSD_XBC, U_XBC[0] // SSD_XBC),
        dt=(dt32, LANE, 0), lg=(u, MIX, U_LG[0] // MIX), lx=(u, MIX, U_LX[0] // MIX))


def _add_norm(acc, resid, g):
    x = acc + resid
    return x, _rms(x, g)


def _layer_fwd(x, h, p_bf, ctx, l, pr, g_next, cosf, sinf):
    rows = x.shape[0]
    ts = _tiles(rows)
    tm = ts["tm"]
    nm = lambda s: f"{s}_l{l}"
    r = dict(x=x)
    if h is None:
        (h,), _ = seq_fwd(nm("rms_in"), f_rms, [pr["g_mix"]], [(x, D_MODEL, 0)], [], [(D_MODEL, bf16)], tm)
    early = [h] + ([cosf, sinf, p_bf] + [a for v in pr.values() for a in (v if isinstance(v, list) else [v]) if a is not None]
                   if l == 0 else [])
    w = dict(_kernel_weights("A", ctx.weights(l, "A", early)))
    u = matmul(nm("w_in"), h, w["w_in"], outs=(U_DTYPE,))
    dt32 = matmul(nm("w_dt"), h, w["w_dt"])
    mt = _mixer_tiles(u, dt32)
    (cqn,), _ = seq_fwd(nm("rms_q"), f_rms, [pr["q_norm"]], [mt["cq"]], [], [(Q_LORA, bf16)], tm)
    (ckvn,), _ = seq_fwd(nm("rms_kv"), f_rms, [pr["kv_norm"]], [mt["ckv"]], [], [(KV_LORA, bf16)], tm)
    (yb,), pool_saved = seq_fwd(nm("pool"), f_pool, pr["pool"], [mt["pool"]], [(POOL_HALO, MIX)], [(MIX, bf16)], ts["tp"])
    w.update(_kernel_weights("B", ctx.weights(l, "B", yb)))
    pr = dict(pr, ssd=[w["ssd_conv_w"]] + pr["ssd"][1:], lru=[w["lru_conv_w"]] + pr["lru"][1:])
    q = matmul(nm("w_uq"), cqn, w["w_uq"])
    kn = matmul(nm("w_uk"), ckvn, w["w_uk"])
    vb = matmul(nm("w_uv"), ckvn, w["w_uv"], outs=(bf16,))
    hw = N_HEADS * LANE
    (qr, kr), _ = seq_fwd(nm("mla_prep"), f_prep, [], [(q, hw, 0), (kn, hw, 0), mt["kr"], (cosf, LANE, 0), (sinf, LANE, 0)],
                          [], [(hw, bf16), (hw, bf16)], tm)
    o, lse = attn_fwd(qr, kr, vb, ts["ta"])
    (yc,), ssd_saved = seq_fwd(nm("ssd"), f_ssd, pr["ssd"], [mt["z"], mt["xbc"], mt["dt"]], _SSD_CARRY, [(MIX, bf16)], SSD_CHUNK)
    (la, lu), lru_saved = seq_fwd(nm("lru_pre"), f_lru_pre, pr["lru"], [mt["lx"]], [(HALO, MIX)], [(MIX, f32), (MIX, f32)], ts["tl"])
    hh = scan_fwd(la, lu, ts["ts"])
    (yd,), _ = seq_fwd(nm("lru_post"), f_lru_post, [], [(hh, MIX, 0), mt["lg"]], [], [(MIX, bf16)], tm)
    ys = [o, yb, yc, yd]
    m, pres = merge_fwd(nm("merge"), ys, w["wb"], u)
    x1, h2 = matmul(nm("w_out"), m, w["w_out"], outs=(f32, bf16), epi=_add_norm, extras=[(x, 0)], rows=[pr["g_mlp"]])
    w.update(_kernel_weights("C", ctx.weights(l, "C", h2)))
    a1, act = matmul(nm("ff1"), h2, w["w_ff1"], outs=(bf16, bf16), epi=lambda acc: (acc, jnp.square(jnp.maximum(acc, 0.0))))
    x2, h3 = matmul(nm("ff2"), act, w["w_ff2"], outs=(f32, bf16), epi=_add_norm, extras=[(x1, 0)], rows=[pr["g_ple"]])
    gl = matmul(nm("ple_gate"), h3, w["w_pg"])
    if g_next is None:
        x3, pe = matmul(nm("ple"), p_bf, w["w_ple"], outs=(f32, f32), epi=lambda acc, g, xr: (xr + acc * _sig(g), acc),
                        extras=[(gl, 0), (x2, 0)])
        h_next = None
    else:
        def ple_norm(acc, g, xr, gn):
            xo = xr + acc * _sig(g)
            return xo, acc, _rms(xo, gn)

        x3, pe, h_next = matmul(nm("ple"), p_bf, w["w_ple"], outs=(f32, f32, bf16), epi=ple_norm,
                                extras=[(gl, 0), (x2, 0)], rows=[g_next])
    r.update(h=h, u=u, cqn=cqn, ckvn=ckvn, q=q, kn=kn, vb=vb, qr=qr, kr=kr, o=o, lse=lse, ys=ys, pres=pres, m=m, x1=x1,
             h2=h2, a1=a1, act=act, x2=x2, h3=h3, gl=gl, pe=pe, p_bf=p_bf, pool_saved=pool_saved, ssd_saved=ssd_saved,
             lru_saved=lru_saved, la=la, hh=hh, w=w, pr=pr, dt32=dt32)
    return x3, h_next, r


def _norm_bwd(dh, x, resid, g):
    rs = lax.rsqrt(jnp.mean(x * x, axis=-1, keepdims=True) + EPS)
    xhat = x * rs
    dxn = dh * g
    dx = rs * (dxn - xhat * jnp.mean(dxn * xhat, axis=-1, keepdims=True)) + resid
    return dx, jnp.sum(dh * xhat, axis=0, keepdims=True)


def _gate_bwd(d, g, pre):
    s = _sig(g.astype(f32))
    return d * s, d * pre.astype(f32) * s * (1.0 - s)


def _layer_bwd(dx3, r, ctx, l, cosf, sinf, tok, extra_small):
    rows = dx3.shape[0]
    ts = _tiles(rows)
    tm = ts["tm"]
    nm = lambda s: f"{s}_l{l}"
    u, w, pr = r["u"], r["w"], r["pr"]
    mt = _mixer_tiles(u, r["dt32"])
    g = {}
    full = lambda a: (a, a.shape[1], 0)
    dpe, dgl = ew(nm("ple_bwd"), _gate_bwd, [full(dx3), full(r["gl"]), full(r["pe"])], [(D_MODEL, bf16)] * 2, tm)
    g["w_ple"] = matmul(nm("d_w_ple"), r["p_bf"], dpe, ta=True, outs=(bf16,), deps=[tok] if tok is not None else [])
    g["w_pg"] = matmul(nm("d_w_pg"), r["h3"], dgl, ta=True, outs=(bf16,))
    dx2, g["g_ple"] = matmul(nm("d_h3"), dgl, w["w_pg"], tb=True, epi=_norm_bwd, extras=[(r["x2"], 0), (dx3, 0)],
                             rows=[pr["g_ple"]], row_sums=1)
    da1 = matmul(nm("d_act"), dx2, w["w_ff2"], tb=True, outs=(bf16,),
                 epi=lambda acc, a: (acc * 2.0 * jnp.maximum(a, 0.0),), extras=[(r["a1"], 0)])
    g["w_ff2"] = matmul(nm("d_w_ff2"), r["act"], dx2, ta=True, outs=(bf16,))
    g["w_ff1"] = matmul(nm("d_w_ff1"), r["h2"], da1, ta=True, outs=(bf16,), out_blocks=N_DEV)
    tok = ctx.grads(l, "C", dict(w_ff1=g["w_ff1"], w_ff2=g["w_ff2"], w_ple_gate=g["w_pg"], w_ple=g["w_ple"]))
    dx1, g["g_mlp"] = matmul(nm("d_h2"), da1, w["w_ff1"], tb=True, epi=_norm_bwd, extras=[(r["x1"], 0), (dx2, 0)],
                             rows=[pr["g_mlp"]], row_sums=1, deps=[tok])
    def merge_bwd(dm, *gates_and_pres):
        both = [_gate_bwd(dm, gates_and_pres[n], gates_and_pres[4 + n]) for n in range(4)]
        return tuple(b[0] for b in both) + tuple(b[1] for b in both)

    res = matmul(nm("d_merged"), dx1, w["w_out"], tb=True, outs=(bf16,) * 8, epi=merge_bwd,
                 extras=[(u, D_MODEL * n) for n in range(4)] + [(pre, 0) for pre in r["pres"]])
    dpres, dgates = list(res[:4]), list(res[4:])
    g["w_out"] = matmul(nm("d_w_out"), r["m"], dx1, ta=True, outs=(bf16,))
    dys, g["wb"] = [], []
    for n in range(4):
        g["wb"].append(matmul(nm(f"d_w_branch{n}"), r["ys"][n], dpres[n], ta=True, outs=(bf16,)))
        dys.append(matmul(nm(f"d_y{n}"), dpres[n], w["wb"][n], tb=True, outs=(bf16 if n == 0 else f32,)))
    dqr, dkr_, dv = attn_bwd(r["qr"], r["kr"], r["vb"], dys[0], r["o"], r["lse"], ts["ta"])
    _, (dq, dkn, dkrope) = seq_bwd(nm("mla_prep_bwd"), f_prep, [],
                                   [full(r["q"]), full(r["kn"]), mt["kr"], full(cosf), full(sinf)],
                                   [True, True, True, False, False], [], [dqr, dkr_], [bf16] * 3, tm)
    g["w_uq"] = matmul(nm("d_w_uq"), r["cqn"], dq, ta=True, outs=(bf16,))
    g["w_uk"] = matmul(nm("d_w_uk"), r["ckvn"], dkn, ta=True, outs=(bf16,))
    g["w_uv"] = matmul(nm("d_w_uv"), r["ckvn"], dv, ta=True, outs=(bf16,))
    dcqn = matmul(nm("d_cqn"), dq, w["w_uq"], tb=True)
    dckvn = matmul(nm("d_ckvn_k"), dkn, w["w_uk"], tb=True)
    dckvn = matmul(nm("d_ckvn_v"), dv, w["w_uv"], tb=True, epi=lambda acc, prev: (acc + prev,), extras=[(dckvn, 0)])
    (g["q_norm"],), (dcq,) = seq_bwd(nm("rms_q_bwd"), f_rms, [pr["q_norm"]], [mt["cq"]], [True], [], [dcqn], [bf16], tm)
    (g["kv_norm"],), (dckv,) = seq_bwd(nm("rms_kv_bwd"), f_rms, [pr["kv_norm"]], [mt["ckv"]], [True], [], [dckvn], [bf16], tm)
    g["pool"], (dpool,) = seq_bwd(nm("pool_bwd"), f_pool, pr["pool"], [mt["pool"]], [True], r["pool_saved"], [dys[1]],
                                  [bf16], ts["tp"])
    g["ssd"], (dz, dxbc, ddt) = seq_bwd(nm("ssd_bwd"), f_ssd, pr["ssd"], [mt["z"], mt["xbc"], mt["dt"]], [True] * 3,
                                        r["ssd_saved"], [dys[2]], [bf16] * 3, SSD_CHUNK)
    _, (dhh, dlg) = seq_bwd(nm("lru_post_bwd"), f_lru_post, [], [full(r["hh"]), mt["lg"]], [True, True], [], [dys[3]],
                            [f32, bf16], tm)
    da, du = scan_bwd(r["la"], r["hh"], dhh, ts["ts"])
    g["lru"], (dlx,) = seq_bwd(nm("lru_pre_bwd"), f_lru_pre, pr["lru"], [mt["lx"]], [True], r["lru_saved"], [da, du],
                               [bf16], ts["tl"])
    dk = _head_unpad_cols(g["w_uk"], QK_NOPE)
    dv_ = _head_unpad_cols(g["w_uv"], V_HEAD)
    wb0 = g["wb"][0].reshape(N_HEADS, LANE, D_MODEL)[:, :V_HEAD].reshape(MIX, D_MODEL)
    ssd, lru, pool = g["ssd"], g["lru"], g["pool"]
    tok = ctx.grads(l, "B", dict(
        w_uq=_head_unpad_cols(g["w_uq"], QK_NOPE + QK_ROPE).reshape(Q_LORA, -1),
        w_ukv=jnp.concatenate([dk, dv_], axis=2).reshape(KV_LORA, -1), ssd_conv_w=ssd[0], lru_conv_w=lru[0],
        w_branch=jnp.stack([wb0, g["wb"][1], g["wb"][2], g["wb"][3]]), w_out=g["w_out"]))
    du_p = jnp.concatenate(dgates + [dpool, dz, dlg, dlx, dxbc, dcq, dkrope, dckv, ddt,
                                     jnp.zeros((rows, U_COLS - U_DT[0] - LANE), bf16)], axis=1)
    small = dict(
        q_norm=g["q_norm"][0], kv_norm=g["kv_norm"][0],
        w_pool=pool[0].reshape(4, LANE, LANE), pool_scale=pool[1][0],
        ssd_conv_b=ssd[1][0], ssd_dt_bias=ssd[2][0, :8], ssd_a_log=ssd[3][0, :8], ssd_d=ssd[4][0, :8], ssd_norm=ssd[5][0],
        lru_conv_b=lru[1][0], lru_w_a=_block_diag_inv(lru[2]), lru_b_a=lru[3][0], lru_w_i=_block_diag_inv(lru[4]),
        lru_b_i=lru[5][0], lru_lambda=lru[6][0], g_mlp=g["g_mlp"][0], g_ple=g["g_ple"][0])
    tok_small = ctx.small(f"l{l}", [(n, l, small[n]) for n in SMALL if n in small] + extra_small)
    g_w_in = matmul(nm("d_w_in"), r["h"], du_p, ta=True, outs=(bf16,), deps=[tok, tok_small])
    tok = ctx.grads(l, "A", dict(w_in=_w_in_blocks(g_w_in)))
    dx, g_mix = matmul(nm("d_h"), du_p, w["w_in"], tb=True, epi=_norm_bwd, extras=[(r["x"], 0), (dx1, 0)],
                       rows=[pr["g_mix"]], row_sums=1, deps=[tok])
    return dx, tok, ("g_mix", l, g_mix[0])


def _rope_tables(positions):
    inv = 1.0 / (ROPE_THETA ** (jnp.arange(0, QK_ROPE, 2, dtype=f32) / QK_ROPE))
    ang = positions.astype(f32)[:, None] * inv
    cos, sin = jnp.cos(ang), jnp.sin(ang)
    rows = positions.shape[0]
    pad = jnp.zeros((rows, LANE - KR_LANE - QK_ROPE), f32)
    cosf = jnp.concatenate([jnp.ones((rows, KR_LANE), f32), cos, cos, pad], axis=1)
    sinf = jnp.concatenate([jnp.zeros((rows, KR_LANE), f32), -sin, sin, pad], axis=1)
    return cosf, sinf


WEIGHTS = ['g_mix', 'w_in', 'q_norm', 'w_uq', 'kv_norm', 'w_ukv', 'w_pool', 'pool_scale', 'ssd_conv_w', 'ssd_conv_b',
           'ssd_dt_bias', 'ssd_a_log', 'ssd_d', 'ssd_norm', 'lru_conv_w', 'lru_conv_b', 'lru_w_a', 'lru_b_a', 'lru_w_i',
           'lru_b_i', 'lru_lambda', 'w_branch', 'w_out', 'g_mlp', 'w_ff1', 'w_ff2', 'g_ple', 'w_ple_gate', 'w_ple', 'g_final']
SHARDED = dict(w_in=2, w_uq=2, w_ukv=2, ssd_conv_w=2, lru_conv_w=2, w_branch=3, w_out=1, w_ff1=2, w_ff2=1,
               w_ple_gate=1, w_ple=2)
F32_PAYLOAD = ("ssd_conv_w", "lru_conv_w")
DEPTH = 2


SMALL = [n for n in WEIGHTS if n not in SHARDED and n != "g_final"]


def local_step(x, p, positions, tgt, sp, ctx):
    cosf, sinf = _rope_tables(positions)
    res, h = [], None
    for l in range(DEPTH):
        g_next = sp["g_mix"][l + 1][None, :] if l + 1 < DEPTH else None
        x, h, r = _layer_fwd(x, h, p[l].astype(bf16), ctx, l, _layer_params(sp, l), g_next, cosf, sinf)
        res.append(r)
    loss8, dx, dgf = loss_head(x, tgt, sp["g_final"][None, :], _pick(x.shape[0], 512))
    tok = None
    pending = ("g_final", None, dgf[0])
    for l in reversed(range(DEPTH)):
        dx, tok, pending = _layer_bwd(dx, res[l], ctx, l, cosf, sinf, tok, [pending])
    ctx.small("last", [pending])
    return loss8[0, 0], dx


def _payload(name, w):
    return w if name in F32_PAYLOAD else w.astype(bf16)


def _blocks(name, g):
    ax = SHARDED[name] - 1
    shape = list(g.shape)
    shape[ax:ax + 1] = [N_DEV, shape[ax] // N_DEV]
    return _payload(name, jnp.moveaxis(g.reshape(shape), ax, 0))


def _assemble(name, shards):
    ax = SHARDED[name] - 1
    shape = list(shards.shape[1:])
    shape[ax] *= N_DEV
    return jnp.moveaxis(shards, 0, ax).reshape(shape)


class _Exchanges:
    def __init__(self, wts):
        self.wts = wts
        self.ag, self.rs, self.sm = {}, {}, {}
        tok = None
        for l in range(DEPTH):
            for grp, names in GROUPS.items():
                h = exchange_start(f"ag_start_{grp}{l}", [_payload(n, wts[n][l]) for n in names], True,
                                   deps=[] if tok is None else [tok])
                tok = h["token"]
                self.ag[(l, grp)] = h
        self.all_started = tok

    def weights(self, l, grp, after):
        afters = list(after) if isinstance(after, (list, tuple)) else [after]
        if (l, grp) == (0, "A"):
            afters.append(self.all_started)
        got = exchange_wait(f"ag_wait_{grp}{l}", self.ag[(l, grp)], afters)
        out = {}
        for n, a in zip(GROUPS[grp], got):
            out[n] = a if n == "w_in" else _assemble(n, a)
        return out

    def grads(self, l, grp, g):
        cut = lambda n: g[n].ndim == self.wts[n].ndim
        h = exchange_start(f"rs_start_{grp}{l}", [g[n] if cut(n) else _blocks(n, g[n]) for n in GROUPS[grp]], False)
        self.rs[(l, grp)] = h
        return h["token"]

    def small(self, tag, entries):
        entries = sorted(entries, key=lambda e: e[2].size % LANE != 0)
        flat = jnp.concatenate([a.reshape(-1) for _, _, a in entries])
        flat = jnp.pad(flat, (0, (-flat.shape[0]) % (8 * LANE))).reshape(-1, LANE)
        h = exchange_start(f"small_start_{tag}", [flat], True)
        self.sm[tag] = (h, [(n, l, a.shape) for n, l, a in entries])
        return h["token"]

    def collect(self, groups, after):
        parts = {}
        for grp in groups:
            for l in reversed(range(DEPTH)):
                got = exchange_wait(f"rs_wait_{grp}{l}", self.rs[(l, grp)], [after])
                for n, a in zip(GROUPS[grp], got):
                    parts.setdefault(n, [None] * DEPTH)[l] = a
        return parts

    def collect_small(self, after):
        gots, where, parts = [], {}, {}
        for tag, (h, layout) in self.sm.items():
            (got,) = exchange_wait(f"small_wait_{tag}", h, [after])
            got = got.reshape(N_DEV, -1)
            off = 0
            for n, l, shape in layout:
                size = 1
                for d in shape:
                    size *= d
                if len(shape) == 1 and size % LANE == 0 and off % LANE == 0:
                    where.setdefault(n, [None] * (1 if l is None else DEPTH))[l or 0] = (len(gots), off)
                else:
                    part = got[:, off:off + size].reshape((N_DEV,) + tuple(shape))
                    if l is None:
                        parts[n] = [part]
                    else:
                        parts.setdefault(n, [None] * DEPTH)[l] = part
                off += size
            gots.append(got)
        return gots, where, parts


def kernel(x, p, positions, g_mix, w_in, q_norm, w_uq, kv_norm, w_ukv, w_pool, pool_scale, ssd_conv_w, ssd_conv_b,
           ssd_dt_bias, ssd_a_log, ssd_d, ssd_norm, lru_conv_w, lru_conv_b, lru_w_a, lru_b_a, lru_w_i, lru_b_i,
           lru_lambda, w_branch, w_out, g_mlp, w_ff1, w_ff2, g_ple, w_ple_gate, w_ple, g_final, loss_target, m_g_mix,
           m_w_in, m_q_norm, m_w_uq, m_kv_norm, m_w_ukv, m_w_pool, m_pool_scale, m_ssd_conv_w, m_ssd_conv_b,
           m_ssd_dt_bias, m_ssd_a_log, m_ssd_d, m_ssd_norm, m_lru_conv_w, m_lru_conv_b, m_lru_w_a, m_lru_b_a,
           m_lru_w_i, m_lru_b_i, m_lru_lambda, m_w_branch, m_w_out, m_g_mlp, m_w_ff1, m_w_ff2, m_g_ple, m_w_ple_gate,
           m_w_ple, m_g_final, v_g_mix, v_w_in, v_q_norm, v_w_uq, v_kv_norm, v_w_ukv, v_w_pool, v_pool_scale,
           v_ssd_conv_w, v_ssd_conv_b, v_ssd_dt_bias, v_ssd_a_log, v_ssd_d, v_ssd_norm, v_lru_conv_w, v_lru_conv_b,
           v_lru_w_a, v_lru_b_a, v_lru_w_i, v_lru_b_i, v_lru_lambda, v_w_branch, v_w_out, v_g_mlp, v_w_ff1, v_w_ff2,
           v_g_ple, v_w_ple_gate, v_w_ple, v_g_final):
    given = dict(locals())
    wts = {n: given[n] for n in WEIGHTS}
    ctx = _Exchanges(wts)
    loss, grad_x = local_step(x[0], p[:, 0], positions[0], loss_target[0], wts, ctx)

    def update(parts):
        out = {}
        for n, eight in parts.items():
            step = adamw_columns if n == "w_in" else adamw
            out[n] = step(f"adamw_{n}", eight, wts[n], given["m_" + n], given["v_" + n])
        return out

    outs = update(ctx.collect(("C", "B"), grad_x))
    late = outs["w_ff1"][1]
    outs.update(update(ctx.collect(("A",), late)))
    gots, where, parts = ctx.collect_small(late)
    outs.update(update(parts))
    names = sorted(where)
    rows = lambda a: a[None] if a.ndim == 1 else a
    res = adamw_packed("adamw_vectors", gots, [where[n] for n in names], [rows(wts[n]) for n in names],
                       [rows(given["m_" + n]) for n in names], [rows(given["v_" + n]) for n in names])
    for n, four in zip(names, res):
        outs[n] = [a[0] for a in four] if wts[n].ndim == 1 else four
    loss = lax.psum(loss, AXES)
    return (loss, grad_x[None], *[outs[n][0] for n in WEIGHTS], *[outs[n][1] for n in WEIGHTS],
            *[outs[n][2] for n in WEIGHTS], *[outs[n][3] for n in WEIGHTS])
```

```python
import functools

import jax
import jax.numpy as jnp
from jax import lax
from jax.experimental import pallas as pl
from jax.experimental.pallas import tpu as pltpu

f32 = jnp.float32
bf16 = jnp.bfloat16

D_MODEL = 1024
MIX = 512
N_HEADS = 8
QK_NOPE, QK_ROPE, V_HEAD = 64, 32, 64
Q_LORA, KV_LORA = 384, 256
ROPE_THETA = 10000.0
POOL_WINDOWS = (2, 4, 8, 16)
SSD_CHUNK = 128
SSD_XBC = 768
CONV_W = 4
LRU_C = 8.0
D_FF = 4096
EPS = 1e-6
IN_COLS = 7592
ADAM_LR, ADAM_B1, ADAM_B2, ADAM_EPS, ADAM_WD, ADAM_STEP = 0.001, 0.9, 0.999, 1e-08, 0.01, 10

LANE = 128
HALO = 8
POOL_HALO = 16
VMEM_LIMIT = 56 * 1024 * 1024
MATMUL_MAX_K_TILE = 4096
MATMUL_ACC_PASS_WEIGHT = 0.3
MATMUL_VMEM_BUDGET = 40 * 1024 * 1024
N_DEV = 8
AXES = ("x", "y", "c")

U_COLS = 8192
U_GATES, U_POOL, U_Z, U_LG, U_LX, U_XBC, U_CQ, U_KR, U_CKV, U_DT = (
    (0, 4096), (4096, 512), (4608, 512), (5120, 512), (5632, 512), (6144, 768),
    (6912, 384), (7296, 128), (7424, 256), (7680, 128))
KR_LANE = 64
U_DTYPE = bf16


def _cp(sem):
    return pltpu.CompilerParams(dimension_semantics=sem, vmem_limit_bytes=VMEM_LIMIT)


def _pick(dim, pref):
    if dim <= pref:
        return dim
    t = pref
    while t >= LANE:
        if dim % t == 0:
            return t
        t -= LANE
    t = pref
    while dim % t:
        t -= 8
    return t


@functools.partial(jax.custom_vjp, nondiff_argnums=(1,))
def shift_down(x, k):
    row = lax.broadcasted_iota(jnp.int32, x.shape, 0)
    return jnp.where(row >= k, pltpu.roll(x, k, 0), 0.0)


def _shift_down_fwd(x, k):
    return shift_down(x, k), None


def _shift_down_bwd(k, _, g):
    r = g.shape[0]
    row = lax.broadcasted_iota(jnp.int32, g.shape, 0)
    return (jnp.where(row < r - k, pltpu.roll(g, r - k, 0), 0.0),)


shift_down.defvjp(_shift_down_fwd, _shift_down_bwd)


@functools.partial(jax.custom_vjp, nondiff_argnums=(1,))
def lane_roll(x, s):
    return pltpu.roll(x, s, 1)


def _lane_roll_fwd(x, s):
    return lane_roll(x, s), None


def _lane_roll_bwd(s, _, g):
    return (pltpu.roll(g, (g.shape[1] - s) % g.shape[1], 1),)


lane_roll.defvjp(_lane_roll_fwd, _lane_roll_bwd)


def _tile_spec(tm, width, cb, n=None):
    if n is None:
        return pl.BlockSpec((tm, width), lambda i: (i, cb))
    return pl.BlockSpec((tm, width), lambda i: (n - 1 - i, cb))


def _const_spec(shape):
    nd = len(shape)
    return pl.BlockSpec(shape, lambda i: (0,) * nd)


def seq_fwd(name, f, params, tiles, carries, outs, tm):
    rows = tiles[0][0].shape[0]
    n = rows // tm
    np_, nt, no, nc = len(params), len(tiles), len(outs), len(carries)

    def body(*refs):
        p_refs = refs[:np_]
        t_refs = refs[np_:np_ + nt]
        o_refs = refs[np_ + nt:np_ + nt + no]
        s_refs = refs[np_ + nt + no:np_ + nt + no + nc]
        c_refs = refs[np_ + nt + no + nc:]
        i = pl.program_id(0)

        @pl.when(i == 0)
        def _():
            for c in c_refs:
                c[...] = jnp.zeros_like(c)

        cvals = [c[...] for c in c_refs]
        for s, c in zip(s_refs, cvals):
            s[0] = c
        o, newc = f(i, [r[...] for r in p_refs], cvals, [r[...].astype(f32) for r in t_refs])
        for r, v in zip(o_refs, o):
            r[...] = v.astype(r.dtype)
        for r, v in zip(c_refs, newc):
            r[...] = v

    in_specs = [_const_spec(p.shape) for p in params] + [_tile_spec(tm, w, cb) for (_, w, cb) in tiles]
    out_specs = [_tile_spec(tm, w, 0) for (w, _) in outs]
    out_specs += [pl.BlockSpec((1,) + tuple(c), lambda i, nd=len(c): (i,) + (0,) * nd) for c in carries]
    out_shape = [jax.ShapeDtypeStruct((rows, w), dt) for (w, dt) in outs]
    out_shape += [jax.ShapeDtypeStruct((n,) + tuple(c), f32) for c in carries]
    res = pl.pallas_call(
        body, name=name, grid=(n,), in_specs=in_specs, out_specs=out_specs, out_shape=out_shape,
        scratch_shapes=[pltpu.VMEM(tuple(c), f32) for c in carries],
        compiler_params=_cp(("arbitrary",)),
    )(*params, *[t[0] for t in tiles])
    return list(res[:no]), list(res[no:])


def seq_bwd(name, f, params, tiles, diff, saved, douts, gdtypes, tm):
    rows = tiles[0][0].shape[0]
    n = rows // tm
    np_, nt, nc, nd = len(params), len(tiles), len(saved), len(douts)
    didx = [k for k, d in enumerate(diff) if d]
    ng = len(didx)

    def body(*refs):
        p_refs = refs[:np_]
        t_refs = refs[np_:np_ + nt]
        s_refs = refs[np_ + nt:np_ + nt + nc]
        d_refs = refs[np_ + nt + nc:np_ + nt + nc + nd]
        pos = np_ + nt + nc + nd
        dp_refs = refs[pos:pos + np_]
        dt_refs = refs[pos + np_:pos + np_ + ng]
        dc_refs = refs[pos + np_ + ng:]
        i = pl.program_id(0)
        step = n - 1 - i

        @pl.when(i == 0)
        def _():
            for r in dp_refs:
                r[...] = jnp.zeros_like(r)
            for r in dc_refs:
                r[...] = jnp.zeros_like(r)

        pvals = [r[...] for r in p_refs]
        cvals = [r[0] for r in s_refs]
        xvals = [r[...].astype(f32) for r in t_refs]

        def fn(p, c, xd):
            x = list(xvals)
            for k, v in zip(didx, xd):
                x[k] = v
            return f(step, p, c, x)

        _, vjp = jax.vjp(fn, pvals, cvals, [xvals[k] for k in didx])
        dp, dc, dx = vjp(([r[...].astype(f32) for r in d_refs], [r[...] for r in dc_refs]))
        for r, v in zip(dp_refs, dp):
            r[...] += v
        for r, v in zip(dc_refs, dc):
            r[...] = v
        for r, v in zip(dt_refs, dx):
            r[...] = v.astype(r.dtype)

    in_specs = [_const_spec(p.shape) for p in params] + [_tile_spec(tm, w, cb, n) for (_, w, cb) in tiles]
    in_specs += [pl.BlockSpec((1,) + tuple(s.shape[1:]), lambda i, nd_=s.ndim - 1: (n - 1 - i,) + (0,) * nd_) for s in saved]
    in_specs += [_tile_spec(tm, d.shape[1], 0, n) for d in douts]
    args = list(params) + [t[0] for t in tiles] + list(saved) + list(douts)
    out_specs = [_const_spec(p.shape) for p in params] + [_tile_spec(tm, tiles[k][1], 0, n) for k in didx]
    out_shape = [jax.ShapeDtypeStruct(p.shape, f32) for p in params]
    out_shape += [jax.ShapeDtypeStruct((rows, tiles[k][1]), dt) for k, dt in zip(didx, gdtypes)]
    res = pl.pallas_call(
        body, name=name, grid=(n,), in_specs=in_specs, out_specs=out_specs, out_shape=out_shape,
        scratch_shapes=[pltpu.VMEM(tuple(s.shape[1:]), f32) for s in saved],
        compiler_params=_cp(("arbitrary",)),
    )(*args)
    return list(res[:np_]), list(res[np_:])


def _halvings(dim, lo, hi):
    t, out = _pick(dim, hi), []
    while t >= min(lo, dim) and dim % t == 0:
        out.append(t)
        if t % 2 or (t // 2) % 8:
            break
        t //= 2
    return out


def _matmul_tiles(m, n, k, a_item, b_item, per_out, max_tn=1024, whole_rows=False):
    def vmem_bytes(tm, tn, tk):
        acc = 4 if k // tk > 1 else 0
        return 2 * (tm * tk * a_item + tk * tn * b_item + tm * tn * per_out) + tm * tn * acc

    def traffic(tm, tn, tk):
        nk = k // tk
        return (m * k * a_item * (1 if nk == 1 else n // tn) + k * n * b_item * (m // tm)
                + (nk - 1) * m * n * 8 * MATMUL_ACC_PASS_WEIGHT)

    cands = [(traffic(tm, tn, tk), -tm * tn, tm, tn, tk)
             for tk in _halvings(k, 512, MATMUL_MAX_K_TILE) for tm in _halvings(m, 256, 4096)
             for tn in ([n] if whole_rows else _halvings(n, 512, min(1024, max_tn)))
             if vmem_bytes(tm, tn, tk) <= MATMUL_VMEM_BUDGET]
    return min(cands)[2:]


def matmul(name, a, b, *, ta=False, tb=False, outs=(f32,), epi=None, extras=(), rows=(), row_sums=0, deps=(),
           out_blocks=0):
    m, k = (a.shape[1], a.shape[0]) if ta else a.shape
    n = b.shape[0] if tb else b.shape[1]
    per_out = sum(jnp.dtype(dt).itemsize for dt in outs) + sum(e[0].dtype.itemsize for e in extras)
    whole_rows = bool(rows) or row_sums > 0
    tm, tn, tk = _matmul_tiles(m, n, k, a.dtype.itemsize, b.dtype.itemsize, per_out,
                               n // out_blocks if out_blocks else n, whole_rows)
    nk = k // tk
    ne, nr, nd, no = len(extras), len(rows), len(deps), len(outs)
    dims = (((0 if ta else 1,), (1 if tb else 0,)), ((), ()))

    def body(*refs):
        a_ref, b_ref = refs[0], refs[1]
        e_refs = refs[2:2 + ne]
        r_refs = refs[2 + ne:2 + ne + nr]
        o_refs = refs[2 + ne + nr + nd:2 + ne + nr + nd + no]
        s_refs = refs[2 + ne + nr + nd + no:2 + ne + nr + nd + no + row_sums]
        i, kk = pl.program_id(0), pl.program_id(2)
        part = lax.dot_general(a_ref[...].astype(bf16), b_ref[...].astype(bf16), dims, preferred_element_type=f32)

        def finish(total):
            res = (total,) if epi is None else epi(total, *[e[...] for e in e_refs], *[r[...] for r in r_refs])
            for r, v in zip(o_refs, res[:no]):
                r[...] = v.astype(r.dtype)
            for r, v in zip(s_refs, res[no:]):
                v8 = jnp.broadcast_to(v, r.shape)

                @pl.when(i == 0)
                def _(r=r, v8=v8):
                    r[...] = v8

                @pl.when(i > 0)
                def _(r=r, v8=v8):
                    r[...] += v8

        if nk == 1:
            finish(part)
            return
        acc = refs[-1]

        @pl.when(kk == 0)
        def _():
            acc[...] = part

        @pl.when(jnp.logical_and(kk > 0, kk < nk - 1))
        def _():
            acc[...] += part

        @pl.when(kk == nk - 1)
        def _():
            finish(acc[...] + part)

    a_spec = pl.BlockSpec((tk, tm), lambda i, j, q: (q, i)) if ta else pl.BlockSpec((tm, tk), lambda i, j, q: (i, q))
    b_spec = pl.BlockSpec((tn, tk), lambda i, j, q: (j, q)) if tb else pl.BlockSpec((tk, tn), lambda i, j, q: (q, j))
    assert all(off % tn == 0 for (_, off) in extras)
    e_specs = [pl.BlockSpec((tm, tn), lambda i, j, q, off=off // tn: (i, off + j)) for (_, off) in extras]
    r_specs = [pl.BlockSpec((1, tn), lambda i, j, q: (0, j)) for _ in rows]
    if out_blocks:
        per = n // out_blocks // tn
        out_spec = pl.BlockSpec((None, tm, tn), lambda i, j, q: (j // per, i, j % per))
        out_dims = (out_blocks, m, n // out_blocks)
    else:
        out_spec = pl.BlockSpec((tm, tn), lambda i, j, q: (i, j))
        out_dims = (m, n)
    res = pl.pallas_call(
        body, name=name, grid=(m // tm, n // tn, nk),
        in_specs=[a_spec, b_spec] + e_specs + r_specs + [pl.BlockSpec(memory_space=pl.ANY) for _ in deps],
        out_specs=[out_spec for _ in outs] + [pl.BlockSpec((8, tn), lambda i, j, q: (0, j))] * row_sums,
        out_shape=[jax.ShapeDtypeStruct(out_dims, dt) for dt in outs] + [jax.ShapeDtypeStruct((8, n), f32)] * row_sums,
        scratch_shapes=[pltpu.VMEM((tm, tn), f32)] if nk > 1 else [],
        compiler_params=_cp(("arbitrary" if row_sums else "parallel", "parallel", "arbitrary")),
    )(a, b, *[e[0] for e in extras], *rows, *deps)
    return res[0] if len(res) == 1 else tuple(res)


def merge_fwd(name, ys, wbs, u):
    rows, n_out = ys[0].shape[0], wbs[0].shape[1]
    nb = len(ys)
    tm, tn = _pick(rows, 512), _pick(n_out, 512)

    def body(*refs):
        y_refs, w_refs, g_refs = refs[:nb], refs[nb:2 * nb], refs[2 * nb:3 * nb]
        m_ref, p_refs = refs[3 * nb], refs[3 * nb + 1:]
        total = None
        for y_ref, w_ref, g_ref, p_ref in zip(y_refs, w_refs, g_refs, p_refs):
            pre = jnp.dot(y_ref[...], w_ref[...], preferred_element_type=f32)
            p_ref[...] = pre.astype(p_ref.dtype)
            term = jax.nn.sigmoid(g_ref[...].astype(f32)) * pre
            total = term if total is None else total + term
        m_ref[...] = total.astype(m_ref.dtype)

    in_specs = [pl.BlockSpec((tm, y.shape[1]), lambda i, j: (i, 0)) for y in ys]
    in_specs += [pl.BlockSpec((w.shape[0], tn), lambda i, j: (0, j)) for w in wbs]
    in_specs += [pl.BlockSpec((tm, tn), lambda i, j, off=n * (n_out // tn): (i, off + j)) for n in range(nb)]
    out_spec = pl.BlockSpec((tm, tn), lambda i, j: (i, j))
    res = pl.pallas_call(
        body, name=name, grid=(rows // tm, n_out // tn), in_specs=in_specs, out_specs=[out_spec] * (nb + 1),
        out_shape=[jax.ShapeDtypeStruct((rows, n_out), bf16)] * (nb + 1),
        compiler_params=_cp(("parallel", "parallel")),
    )(*ys, *wbs, *([u] * nb))
    return res[0], list(res[1:])


ATT_SCALE = (QK_NOPE + QK_ROPE) ** -0.5
LN2 = 0.6931471805599453
ATT_C = ATT_SCALE / LN2
NT = (((1,), (1,)), ((), ()))
TN = (((0,), (0,)), ((), ()))


def _causal(tq, tk):
    return lax.broadcasted_iota(jnp.int32, (tq, tk), 0) >= lax.broadcasted_iota(jnp.int32, (tq, tk), 1)


def _tri_pairs(n, by_column):
    if by_column:
        pairs = [(i, j) for j in range(n) for i in range(j, n)]
    else:
        pairs = [(i, j) for i in range(n) for j in range(i + 1)]
    return (jnp.asarray([a for a, _ in pairs], jnp.int32), jnp.asarray([b for _, b in pairs], jnp.int32))


FWD_HEADS_PER_STEP = 8
HEADS_PER_STEP = 4
HEAD_PAIR = HEADS_PER_STEP * LANE


def attn_fwd(q, k, v, t):
    rows = q.shape[0]
    n = rows // t
    it, jt = _tri_pairs(n, False)

    def body(it_ref, jt_ref, q_ref, k_ref, v_ref, o_ref, lse_ref, m_s, l_s, acc_s):
        s_id = pl.program_id(1)
        i, j = it_ref[s_id], jt_ref[s_id]

        @pl.when(j == 0)
        def _():
            m_s[...] = jnp.full_like(m_s, -jnp.inf)
            l_s[...] = jnp.zeros_like(l_s)
            acc_s[...] = jnp.zeros_like(acc_s)

        def step(diag):
            for hh in range(FWD_HEADS_PER_STEP):
                sl = slice(LANE * hh, LANE * (hh + 1))
                s = lax.dot_general(q_ref[:, sl], k_ref[:, sl], NT, preferred_element_type=f32)
                if diag:
                    s = jnp.where(_causal(t, t), s, -jnp.inf)
                m_prev = m_s[:, sl]
                m_new = jnp.maximum(m_prev, jnp.max(s, axis=1, keepdims=True))
                alpha = jnp.exp2(m_prev - m_new)
                p = jnp.exp2(s - m_new[:, :1])
                l_s[:, sl] = alpha * l_s[:, sl] + jnp.sum(p, axis=1, keepdims=True)
                acc_s[:, sl] = alpha * acc_s[:, sl] + jnp.dot(p.astype(bf16), v_ref[:, sl], preferred_element_type=f32)
                m_s[:, sl] = m_new

        pl.when(j < i)(lambda: step(False))

        @pl.when(j == i)
        def _():
            step(True)
            o_ref[...] = (acc_s[...] / l_s[...]).astype(o_ref.dtype)
            lse_ref[...] = m_s[...] + jnp.log2(l_s[...])

    width = FWD_HEADS_PER_STEP * LANE
    qs = pl.BlockSpec((t, width), lambda h, s, it_, jt_: (it_[s], h))
    ks = pl.BlockSpec((t, width), lambda h, s, it_, jt_: (jt_[s], h))
    hw = N_HEADS * LANE
    return pl.pallas_call(
        body, name="attn_fwd",
        grid_spec=pltpu.PrefetchScalarGridSpec(
            num_scalar_prefetch=2, grid=(hw // width, it.shape[0]), in_specs=[qs, ks, ks], out_specs=[qs, qs],
            scratch_shapes=[pltpu.VMEM((t, width), f32)] * 3),
        out_shape=[jax.ShapeDtypeStruct((rows, hw), bf16), jax.ShapeDtypeStruct((rows, hw), f32)],
        compiler_params=_cp(("parallel", "arbitrary")),
    )(it, jt, q, k, v)


def attn_bwd(q, k, v, do, o, lse, t):
    rows = q.shape[0]
    n = rows // t
    it, jt = _tri_pairs(n, True)

    def body(it_ref, jt_ref, q_ref, k_ref, v_ref, do_ref, o_ref, lse_ref, dq_ref, dk_ref, dv_ref, dk_s, dv_s):
        s_id = pl.program_id(1)
        i, j = it_ref[s_id], jt_ref[s_id]

        @pl.when(s_id == 0)
        def _():
            dq_ref[...] = jnp.zeros_like(dq_ref)

        @pl.when(i == j)
        def _():
            dk_s[...] = jnp.zeros_like(dk_s)
            dv_s[...] = jnp.zeros_like(dv_s)

        q_rows = pl.ds(pl.multiple_of(i * t, t), t)

        def step(diag):
            for hh in range(HEADS_PER_STEP):
                sl = slice(LANE * hh, LANE * (hh + 1))
                qh, kh, vh, doh = q_ref[:, sl], k_ref[:, sl], v_ref[:, sl], do_ref[:, sl]
                s = lax.dot_general(qh, kh, NT, preferred_element_type=f32)
                p = jnp.exp2(s - lse_ref[:, sl][:, :1])
                if diag:
                    p = jnp.where(_causal(t, t), p, 0.0)
                dp = lax.dot_general(doh, vh, NT, preferred_element_type=f32)
                delta = jnp.sum(doh.astype(f32) * o_ref[:, sl].astype(f32), axis=1, keepdims=True)
                ds = (p * (dp - delta) * LN2).astype(bf16)
                dv_s[:, sl] += lax.dot_general(p.astype(bf16), doh, TN, preferred_element_type=f32)
                dk_s[:, sl] += lax.dot_general(ds, qh, TN, preferred_element_type=f32)
                dq_ref[q_rows, sl] += jnp.dot(ds, kh, preferred_element_type=f32)

        pl.when(i > j)(lambda: step(False))
        pl.when(i == j)(lambda: step(True))

        @pl.when(i == n - 1)
        def _():
            dk_ref[...] = dk_s[...]
            dv_ref[...] = dv_s[...]

    qs = pl.BlockSpec((t, HEAD_PAIR), lambda h, s, it_, jt_: (it_[s], h))
    ks = pl.BlockSpec((t, HEAD_PAIR), lambda h, s, it_, jt_: (jt_[s], h))
    dqs = pl.BlockSpec((rows, HEAD_PAIR), lambda h, s, it_, jt_: (0, h))
    hw = N_HEADS * LANE
    return pl.pallas_call(
        body, name="attn_bwd",
        grid_spec=pltpu.PrefetchScalarGridSpec(
            num_scalar_prefetch=2, grid=(hw // HEAD_PAIR, it.shape[0]), in_specs=[qs, ks, ks, qs, qs, qs],
            out_specs=[dqs, ks, ks], scratch_shapes=[pltpu.VMEM((t, HEAD_PAIR), f32)] * 2),
        out_shape=[jax.ShapeDtypeStruct((rows, hw), f32)] * 3,
        compiler_params=_cp(("parallel", "arbitrary")),
    )(it, jt, q, k, v, do, o, lse)


def _steps(tm):
    k, out = 1, []
    while k < tm:
        out.append(k)
        k *= 2
    return out


def scan_fwd(a, u, tm):
    rows, ch = a.shape
    n = rows // tm

    def body(a_ref, u_ref, h_ref, h_s):
        @pl.when(pl.program_id(0) == 0)
        def _():
            h_s[...] = jnp.zeros_like(h_s)

        av, bv = a_ref[...], u_ref[...]
        row = lax.broadcasted_iota(jnp.int32, av.shape, 0)
        for k in _steps(tm):
            a_sh = jnp.where(row >= k, pltpu.roll(av, k, 0), 1.0)
            b_sh = jnp.where(row >= k, pltpu.roll(bv, k, 0), 0.0)
            bv = av * b_sh + bv
            av = av * a_sh
        h = bv + av * h_s[HALO - 1:HALO, :]
        h_ref[...] = h
        h_s[...] = h[tm - HALO:, :]

    spec = pl.BlockSpec((tm, ch), lambda i: (i, 0))
    return pl.pallas_call(
        body, name="lru_scan_fwd", grid=(n,), in_specs=[spec, spec], out_specs=spec,
        out_shape=jax.ShapeDtypeStruct((rows, ch), f32), scratch_shapes=[pltpu.VMEM((HALO, ch), f32)],
        compiler_params=_cp(("arbitrary",)),
    )(a, u)


def scan_bwd(a, h, dh, tm):
    rows, ch = a.shape
    n = rows // tm
    per = tm // HALO

    def body(a_ref, h_ref, hp_ref, dh_ref, da_ref, du_ref, g_s, a_s):
        i = pl.program_id(0)
        step = n - 1 - i

        @pl.when(i == 0)
        def _():
            g_s[...] = jnp.zeros_like(g_s)
            a_s[...] = jnp.zeros_like(a_s)

        a0 = a_ref[...]
        row = lax.broadcasted_iota(jnp.int32, a0.shape, 0)
        av = jnp.where(row < tm - 1, pltpu.roll(a0, tm - 1, 0), a_s[0:1, :])
        bv = dh_ref[...]
        for k in _steps(tm):
            a_sh = jnp.where(row < tm - k, pltpu.roll(av, tm - k, 0), 1.0)
            b_sh = jnp.where(row < tm - k, pltpu.roll(bv, tm - k, 0), 0.0)
            bv = bv + av * b_sh
            av = av * a_sh
        g = bv + av * g_s[0:1, :]
        h_last = jnp.where(step > 0, hp_ref[HALO - 1:HALO, :], 0.0)
        h_prev = jnp.where(row >= 1, pltpu.roll(h_ref[...], 1, 0), h_last)
        du_ref[...] = g
        da_ref[...] = g * h_prev
        g_s[...] = g[0:HALO, :]
        a_s[...] = a0[0:HALO, :]

    spec = pl.BlockSpec((tm, ch), lambda i: (n - 1 - i, 0))
    hp_spec = pl.BlockSpec((HALO, ch), lambda i: (jnp.maximum((n - 1 - i) * per - 1, 0), 0))
    return pl.pallas_call(
        body, name="lru_scan_bwd", grid=(n,), in_specs=[spec, spec, hp_spec, spec], out_specs=[spec, spec],
        out_shape=[jax.ShapeDtypeStruct((rows, ch), f32)] * 2,
        scratch_shapes=[pltpu.VMEM((HALO, ch), f32)] * 2,
        compiler_params=_cp(("arbitrary",)),
    )(a, h, h, dh)


def _rms(x, g):
    return x * lax.rsqrt(jnp.mean(x * x, axis=-1, keepdims=True) + EPS) * g


def f_rms(step, p, c, x):
    return [_rms(x[0], p[0])], []


def _rope(x, cosf, sinf):
    lane = lax.broadcasted_iota(jnp.int32, x.shape, 1)
    sw = jnp.where(lane < KR_LANE + QK_ROPE // 2, lane_roll(x, LANE - QK_ROPE // 2), lane_roll(x, QK_ROPE // 2))
    return x * cosf + sw * sinf


def f_prep(step, p, c, x):
    q, kn, kr, cosf, sinf = x
    kr_rot = _rope(kr, cosf, sinf)
    qr = [_rope(q[:, LANE * h:LANE * (h + 1)], cosf, sinf) * ATT_C for h in range(N_HEADS)]
    kk = [kn[:, LANE * h:LANE * (h + 1)] + kr_rot for h in range(N_HEADS)]
    return [jnp.concatenate(qr, axis=1), jnp.concatenate(kk, axis=1)], []


def _conv(tail, x, w, b):
    xf = jnp.concatenate([tail, x], axis=0)
    acc = b + w[CONV_W - 1:CONV_W, :] * xf
    for k in range(CONV_W - 1):
        acc = acc + w[k:k + 1, :] * shift_down(xf, CONV_W - 1 - k)
    return acc[HALO:, :]


def f_pool(step, p, c, x):
    wp, sc = p
    (tail,) = c
    (u,) = x
    tm = u.shape[0]
    xf = jnp.concatenate([tail, u], axis=0)
    sums, s, w = [], xf, 1
    while w < POOL_WINDOWS[-1]:
        s = s + shift_down(s, w)
        w *= 2
        sums.append(s)
    t = step * tm + lax.broadcasted_iota(jnp.int32, (tm, 1), 0)
    ys = []
    for g, (w, s) in enumerate(zip(POOL_WINDOWS, sums)):
        sl = slice(LANE * g, LANE * (g + 1))
        cnt = jnp.minimum(t + 1, w).astype(f32)
        d = s[POOL_HALO:, sl] / cnt - u[:, sl]
        ys.append(jnp.dot(d.astype(bf16), wp[LANE * g:LANE * (g + 1), :].astype(bf16), preferred_element_type=f32))
    return [jnp.concatenate(ys, axis=1) * sc], [u[tm - POOL_HALO:, :]]


def f_ssd(step, p, c, x):
    conv_w, conv_b, dtb, alog, dsk, ng = p
    tail, s_in = c[0], c[1:]
    z, xbc, dt = x
    ln = z.shape[0]
    xc = jax.nn.silu(_conv(tail, xbc, conv_w, conv_b))
    xs, bb, cc = xc[:, :MIX], xc[:, MIX:MIX + LANE], xc[:, MIX + LANE:]
    dtv = jax.nn.softplus(dt + dtb[0:1, :])
    a = dtv * -jnp.exp(alog[0:1, :])
    ri = lax.broadcasted_iota(jnp.int32, (ln, ln), 0)
    ci = lax.broadcasted_iota(jnp.int32, (ln, ln), 1)
    tril = (ri >= ci).astype(f32)
    triu = (ri <= ci).astype(f32)
    hi = lax.Precision.HIGHEST
    a_cs = jnp.dot(tril, a, precision=hi, preferred_element_type=f32)
    a_cs_t = lax.dot_general(a, triu, TN, precision=hi, preferred_element_type=f32)
    a_tot = jnp.sum(a, axis=0, keepdims=True)
    lane = lax.broadcasted_iota(jnp.int32, (1, LANE), 1)
    half = [(lane < 64).astype(f32), (lane >= 64).astype(f32)]
    hrow = lax.broadcasted_iota(jnp.int32, (LANE, 1), 0)

    def head(v, h):
        return jnp.sum(v * (lane == h).astype(f32), axis=1, keepdims=True)

    def pair(v, j):
        return head(v, 2 * j) * half[0] + head(v, 2 * j + 1) * half[1]

    cg = [(cc * half[g]).astype(bf16) for g in range(2)]
    bg = [(bb * half[g]).astype(bf16) for g in range(2)]
    cb = [lax.dot_general(cg[g], bg[g], NT, preferred_element_type=f32) for g in range(2)]
    ys, s_out = [], []
    for j in range(4):
        g = j // 2
        xs_j = xs[:, LANE * j:LANE * (j + 1)]
        xj = xs_j * pair(dtv, j)
        yj = xs_j * pair(dsk[0:1, :], j)
        for hh in range(2):
            h = 2 * j + hh
            rowv = jnp.sum(a_cs_t * (hrow == h).astype(f32), axis=0, keepdims=True)
            lmat = jnp.exp(jnp.where(ri >= ci, head(a_cs, h) - rowv, -jnp.inf))
            yj = yj + jnp.dot((cb[g] * lmat).astype(bf16), (xj * half[hh]).astype(bf16), preferred_element_type=f32)
        acs = pair(a_cs, j)
        tot = pair(a_tot, j)
        yj = yj + jnp.exp(acs) * jnp.dot(cg[g], s_in[j].astype(bf16), preferred_element_type=f32)
        s_new = jnp.exp(tot) * s_in[j] + lax.dot_general(bg[g], (xj * jnp.exp(tot - acs)).astype(bf16), TN,
                                                         preferred_element_type=f32)
        ys.append(yj)
        s_out.append(s_new)
    y = jnp.concatenate(ys, axis=1) * jax.nn.silu(z)
    return [_rms(y, ng)], [xbc[ln - HALO:, :]] + s_out


def _neg_expm1(y):
    series = -y * (1.0 + y * (0.5 + y * (1.0 / 6 + y * (1.0 / 24 + y * (1.0 / 120)))))
    return jnp.where(y > -0.05, series, 1.0 - jnp.exp(y))


def f_lru_pre(step, p, c, x):
    cw, cb_, wa, ba, wi, bi, lam = p
    (tail,) = c
    (lx,) = x
    tm = lx.shape[0]
    xc = _conv(tail, lx, cw, cb_)
    xb = xc.astype(bf16)
    r = jax.nn.sigmoid(jnp.dot(xb, wa.astype(bf16), preferred_element_type=f32) + ba)
    it = jax.nn.sigmoid(jnp.dot(xb, wi.astype(bf16), preferred_element_type=f32) + bi)
    log_a = -LRU_C * r * jax.nn.softplus(-lam)
    mult = jnp.sqrt(_neg_expm1(2.0 * log_a))
    return [jnp.exp(log_a), xc * it * mult], [lx[tm - HALO:, :]]


def f_lru_post(step, p, c, x):
    h, g = x
    return [h * jax.nn.gelu(g)], []


def loss_head(x, tgt, g, tm):
    rows, d = x.shape
    n = rows // tm

    def body(x_ref, t_ref, g_ref, loss_ref, dx_ref, dg_ref):
        @pl.when(pl.program_id(0) == 0)
        def _():
            loss_ref[...] = jnp.zeros_like(loss_ref)
            dg_ref[...] = jnp.zeros_like(dg_ref)

        def fn(gv, xv):
            err = _rms(xv, gv) - t_ref[...]
            return 0.5 * jnp.sum(jnp.mean(err * err, axis=-1, keepdims=True))

        val, (dg, dx) = jax.value_and_grad(fn, argnums=(0, 1))(g_ref[...], x_ref[...])
        loss_ref[...] += val
        dg_ref[...] += dg
        dx_ref[...] = dx

    spec = pl.BlockSpec((tm, d), lambda i: (i, 0))
    return pl.pallas_call(
        body, name="loss_head", grid=(n,), in_specs=[spec, spec, _const_spec((1, d))],
        out_specs=[_const_spec((8, LANE)), spec, _const_spec((1, d))],
        out_shape=[jax.ShapeDtypeStruct((8, LANE), f32), jax.ShapeDtypeStruct((rows, d), f32),
                   jax.ShapeDtypeStruct((1, d), f32)],
        compiler_params=_cp(("arbitrary",)),
    )(x, tgt, g)


def ew(name, fn, ins, outs, tm):
    rows = ins[0][0].shape[0]
    ni = len(ins)

    def body(*refs):
        res = fn(*[r[...].astype(f32) for r in refs[:ni]])
        for r, v in zip(refs[ni:], res):
            r[...] = v.astype(r.dtype)

    return pl.pallas_call(
        body, name=name, grid=(rows // tm,), in_specs=[_tile_spec(tm, w, cb) for (_, w, cb) in ins],
        out_specs=[_tile_spec(tm, w, 0) for (w, _) in outs],
        out_shape=[jax.ShapeDtypeStruct((rows, w), dt) for (w, dt) in outs],
        compiler_params=_cp(("parallel",)),
    )(*[t[0] for t in ins])


def _peers():
    x, y, c = lax.axis_index("x"), lax.axis_index("y"), lax.axis_index("c")
    me = 4 * x + 2 * y + c
    out = []
    for k in range(1, N_DEV):
        px = 1 - x if k & 4 else x
        py = 1 - y if k & 2 else y
        pc = 1 - c if k & 1 else c
        out.append(((px, py, pc), 4 * px + 2 * py + pc))
    return me, out


_HBM = pl.BlockSpec(memory_space=pltpu.HBM)
_SEM = pl.BlockSpec(memory_space=pltpu.SEMAPHORE)
_EFFECT = pltpu.SideEffectType.DATAFLOW_SIDE_EFFECTING


def _remote(src_ref, land_ref, gather, me, pid, dev, send_sems, recv_sems, k, recv_side):
    return pltpu.make_async_remote_copy(
        src_ref=src_ref if gather else src_ref.at[pid], dst_ref=land_ref.at[pid if recv_side else me],
        send_sem=send_sems.at[k], recv_sem=recv_sems.at[k], device_id=dev, device_id_type=pl.DeviceIdType.MESH)


def _own(src_ref, land_ref, gather, me, sem):
    return pltpu.make_async_copy(src_ref if gather else src_ref.at[me], land_ref.at[me], sem)


def exchange_start(name, srcs, gather, deps=()):
    n, nd = len(srcs), len(deps)
    shapes = [(s.shape if gather else s.shape[1:]) for s in srcs]
    lands = [lax.empty((N_DEV,) + tuple(sh), s.dtype) for s, sh in zip(srcs, shapes)]

    def body(*refs):
        src_refs, land_refs = refs[:n], refs[n:2 * n]
        send_sems, recv_sems, own_sem = refs[2 * n + nd:2 * n + nd + 3]
        token = refs[-1]
        me, peers = _peers()
        for k, (dev, pid) in enumerate(peers):
            for s_ref, l_ref in zip(src_refs, land_refs):
                _remote(s_ref, l_ref, gather, me, pid, dev, send_sems, recv_sems, k, False).start()
        for s_ref, l_ref in zip(src_refs, land_refs):
            _own(s_ref, l_ref, gather, me, own_sem).start()
        token[...] = jnp.zeros_like(token)

    hbm = lambda a: pltpu.with_memory_space_constraint(a, pltpu.HBM)
    res = pl.pallas_call(
        body, name=name,
        out_shape=(pltpu.SemaphoreType.DMA((N_DEV - 1,)), pltpu.SemaphoreType.DMA((N_DEV - 1,)), pltpu.SemaphoreType.DMA(()),
                   *[pltpu.HBM(a.shape, a.dtype) for a in list(srcs) + lands], jax.ShapeDtypeStruct((8, LANE), f32)),
        in_specs=[_HBM] * (2 * n) + [pl.BlockSpec(memory_space=pl.ANY)] * nd,
        out_specs=(_SEM, _SEM, _SEM, *([_HBM] * (2 * n)), pl.BlockSpec(memory_space=pltpu.VMEM)),
        input_output_aliases={i: 3 + i for i in range(2 * n)},
        compiler_params=pltpu.CompilerParams(has_side_effects=_EFFECT),
    )(*[hbm(a) for a in list(srcs) + lands], *deps)
    return dict(sems=res[:3], srcs=list(res[3:3 + n]), lands=list(res[3 + n:3 + 2 * n]), token=res[-1], gather=gather)


def exchange_wait(name, h, afters):
    n, gather = len(h["srcs"]), h["gather"]

    def body(*refs):
        src_refs, land_refs = refs[:n], refs[n:2 * n]
        send_sems, recv_sems, own_sem = refs[2 * n:2 * n + 3]
        me, peers = _peers()
        for k, (dev, pid) in enumerate(peers):
            for s_ref, l_ref in zip(src_refs, land_refs):
                _remote(s_ref, l_ref, gather, me, pid, dev, send_sems, recv_sems, k, True).wait_recv()
        for k, (dev, pid) in enumerate(peers):
            for s_ref, l_ref in zip(src_refs, land_refs):
                _remote(s_ref, l_ref, gather, me, pid, dev, send_sems, recv_sems, k, False).wait_send()
        for s_ref, l_ref in zip(src_refs, land_refs):
            _own(s_ref, l_ref, gather, me, own_sem).wait()

    arrs = h["srcs"] + h["lands"]
    res = pl.pallas_call(
        body, name=name, out_shape=tuple(pltpu.HBM(a.shape, a.dtype) for a in arrs),
        in_specs=[_HBM] * (2 * n) + [_SEM, _SEM, _SEM] + [pl.BlockSpec(memory_space=pl.ANY)] * len(afters),
        out_specs=tuple([_HBM] * (2 * n)), input_output_aliases={i: i for i in range(2 * n)},
        compiler_params=pltpu.CompilerParams(has_side_effects=_EFFECT),
    )(*arrs, *h["sems"], *afters)
    return list(res[n:])


def _adam_update(g, w, m, v):
    mn = ADAM_B1 * m + (1.0 - ADAM_B1) * g
    vn = ADAM_B2 * v + (1.0 - ADAM_B2) * jnp.square(g)
    m_hat = mn / (1.0 - ADAM_B1 ** ADAM_STEP)
    v_hat = vn / (1.0 - ADAM_B2 ** ADAM_STEP)
    return -ADAM_LR * (m_hat / (jnp.sqrt(v_hat) + ADAM_EPS) + ADAM_WD * w), mn, vn


def _adamw_vectors(name, parts, w, m, v):
    nl = len(parts)

    def body(*refs):
        p_refs = refs[:nl]
        w_ref, m_ref, v_ref, g_ref, d_ref, nm_ref, nv_ref = refs[nl:]
        for ll, p_ref in enumerate(p_refs):
            row = slice(ll, ll + 1)
            g = p_ref[0:1, :]
            for i in range(1, N_DEV):
                g = g + p_ref[i:i + 1, :]
            delta, mn, vn = _adam_update(g, w_ref[row, :], m_ref[row, :], v_ref[row, :])
            g_ref[row, :] = g
            d_ref[row, :] = delta
            nm_ref[row, :] = mn
            nv_ref[row, :] = vn

    return list(pl.pallas_call(body, name=name, out_shape=[jax.ShapeDtypeStruct(w.shape, f32)] * 4)(*parts, w, m, v))


def adamw_packed(name, gots, where, ws, ms, vs):
    ng, npar = len(gots), len(ws)

    def body(*refs):
        g_refs = refs[:ng]
        w_refs, m_refs, v_refs = (refs[ng + k * npar:ng + (k + 1) * npar] for k in range(3))
        o_refs = refs[ng + 3 * npar:]
        for p in range(npar):
            width = w_refs[p].shape[1]
            for l, (which, off) in enumerate(where[p]):
                row = slice(l, l + 1)
                cols = slice(off, off + width)
                g = g_refs[which][0:1, cols]
                for i in range(1, N_DEV):
                    g = g + g_refs[which][i:i + 1, cols]
                delta, mn, vn = _adam_update(g, w_refs[p][row, :], m_refs[p][row, :], v_refs[p][row, :])
                for k, val in enumerate((g, delta, mn, vn)):
                    o_refs[4 * p + k][row, :] = val

    out_shape = [jax.ShapeDtypeStruct(w.shape, f32) for w in ws for _ in range(4)]
    res = pl.pallas_call(body, name=name, out_shape=out_shape, compiler_params=_cp(()))(*gots, *ws, *ms, *vs)
    return [list(res[4 * p:4 * p + 4]) for p in range(npar)]


def adamw_columns(name, parts, w, m, v):
    nl, kk, cc = w.shape
    view = lambda a: jnp.transpose(a, (2, 0, 1))
    tc = min(LANE, cc)

    def body(*refs):
        p_refs = refs[:nl]
        w_ref, m_ref, v_ref, g_ref, d_ref, nm_ref, nv_ref = refs[nl:]
        for l, p_ref in enumerate(p_refs):
            g = p_ref[0].astype(f32)
            for i in range(1, N_DEV):
                g = g + p_ref[i].astype(f32)
            g = g.T
            delta, mn, vn = _adam_update(g, w_ref[:, l, :], m_ref[:, l, :], v_ref[:, l, :])
            g_ref[:, l, :] = g
            d_ref[:, l, :] = delta
            nm_ref[:, l, :] = mn
            nv_ref[:, l, :] = vn

    p_spec = pl.BlockSpec((N_DEV, kk, tc), lambda j: (0, 0, j))
    w_spec = pl.BlockSpec((tc, nl, kk), lambda j: (j, 0, 0))
    res = pl.pallas_call(
        body, name=name, grid=(pl.cdiv(cc, tc),), in_specs=[p_spec] * nl + [w_spec] * 3, out_specs=[w_spec] * 4,
        out_shape=[jax.ShapeDtypeStruct((cc, nl, kk), f32)] * 4, compiler_params=_cp(("parallel",)),
    )(*parts, view(w), view(m), view(v))
    return [jnp.transpose(a, (1, 2, 0)) for a in res]


def adamw(name, parts, w, m, v):
    nl = len(parts)
    shape = w.shape[1:]
    c = shape[-1]
    r = 1
    for s in shape[:-1]:
        r *= s
    if r == 1:
        return _adamw_vectors(name, parts, w, m, v)
    tr = _pick(r, 256) if r % 8 == 0 else r
    nb = r // tr
    parts2 = [p.reshape(N_DEV, r, c) for p in parts]
    w2, m2, v2 = (a.reshape(nl, r, c) for a in (w, m, v))

    def body(*refs):
        p_refs = refs[:nl]
        w_ref, m_ref, v_ref, g_ref, d_ref, nm_ref, nv_ref = refs[nl:]
        layer = pl.program_id(0)
        for ll, p_ref in enumerate(p_refs):
            @pl.when(layer == ll)
            def _(p_ref=p_ref):
                g = p_ref[0].astype(f32)
                for i in range(1, N_DEV):
                    g = g + p_ref[i].astype(f32)
                delta, mn, vn = _adam_update(g, w_ref[0], m_ref[0], v_ref[0])
                g_ref[0] = g
                d_ref[0] = delta
                nm_ref[0] = mn
                nv_ref[0] = vn

    def p_spec(ll):
        return pl.BlockSpec((N_DEV, tr, c), lambda l, i: (0, jnp.where(l == ll, i, jnp.where(l > ll, nb - 1, 0)), 0))

    spec = pl.BlockSpec((1, tr, c), lambda l, i: (l, i, 0))
    res = pl.pallas_call(
        body, name=name, grid=(nl, nb), in_specs=[p_spec(ll) for ll in range(nl)] + [spec, spec, spec],
        out_specs=[spec] * 4, out_shape=[jax.ShapeDtypeStruct((nl, r, c), f32)] * 4,
        compiler_params=_cp(("arbitrary", "arbitrary")),
    )(*parts2, w2, m2, v2)
    return [a.reshape(w.shape) for a in res]


_IN_SPLITS = dict(cq=(0, 384), ckv=(384, 640), kr=(640, 672), pool=(672, 1184), z=(1184, 1696), xbc=(1696, 2464),
                  dt=(2464, 2472), lg=(2472, 2984), lx=(2984, 3496), gates=(3496, 7592))


W_IN_SHARD = IN_COLS // N_DEV

_PAD_ORDER = ("gates", "pool", "z", "lg", "lx", "xbc", "cq", KR_LANE, "kr", LANE - KR_LANE - QK_ROPE, "ckv", "dt",
              LANE - 8, U_COLS - U_DT[0] - LANE)
_SEGMENTS = ((0, U_CQ[0], 384), (384, U_CKV[0], 256), (640, U_KR[0] + KR_LANE, QK_ROPE), (672, U_POOL[0], 512),
             (1184, U_Z[0], 512), (1696, U_XBC[0], 768), (2464, U_DT[0], 8), (2472, U_LG[0], 512), (2984, U_LX[0], 512),
             (3496, 0, 4096))


def _pad_w_in(shards):
    rows = shards.shape[1]
    pieces = []
    for item in _PAD_ORDER:
        if isinstance(item, int):
            pieces.append(jnp.zeros((rows, item), shards.dtype))
            continue
        a, b = _IN_SPLITS[item]
        for d in range(a // W_IN_SHARD, (b - 1) // W_IN_SHARD + 1):
            lo, hi = max(a, d * W_IN_SHARD), min(b, (d + 1) * W_IN_SHARD)
            pieces.append(shards[d, :, lo - d * W_IN_SHARD:hi - d * W_IN_SHARD])
    return jnp.concatenate(pieces, axis=1)


def _w_in_blocks(g):
    blocks = []
    for d in range(N_DEV):
        a, b = d * W_IN_SHARD, (d + 1) * W_IN_SHARD
        pieces = []
        for ref, pad, width in _SEGMENTS:
            lo, hi = max(a, ref), min(b, ref + width)
            if lo < hi:
                pieces.append(g[:, pad + lo - ref:pad + hi - ref])
        blocks.append(jnp.concatenate(pieces, axis=1))
    return jnp.stack(blocks).astype(bf16)


def _head_pad_cols(w, per, lo, hi):
    k = w.shape[0]
    w = w.reshape(k, N_HEADS, per)[:, :, lo:hi]
    return jnp.pad(w, ((0, 0), (0, 0), (0, LANE - (hi - lo)))).reshape(k, N_HEADS * LANE)


def _head_unpad_cols(g, n):
    k = g.shape[0]
    return g.reshape(k, N_HEADS, LANE)[:, :, :n]


def _on_diagonal():
    i = lax.broadcasted_iota(jnp.int32, (8, 1, 8, 1), 0)
    j = lax.broadcasted_iota(jnp.int32, (8, 1, 8, 1), 2)
    return i == j


def _block_diag(w):
    w4 = jnp.broadcast_to(w[:, :, None, :], (8, 64, 8, 64))
    return jnp.where(_on_diagonal(), w4, 0.0).reshape(MIX, MIX)


def _block_diag_inv(g):
    return jnp.sum(jnp.where(_on_diagonal(), g.reshape(8, 64, 8, 64), 0.0), axis=2)


def _head8(v):
    return jnp.pad(v[None, :], ((0, 7), (0, LANE - v.shape[0])))


GROUPS = dict(A=("w_in",), B=("w_uq", "w_ukv", "ssd_conv_w", "lru_conv_w", "w_branch", "w_out"),
              C=("w_ff1", "w_ff2", "w_ple_gate", "w_ple"))


def _kernel_weights(grp, fw):
    if grp == "A":
        w_in = _pad_w_in(fw["w_in"])
        return dict(w_in=w_in, w_dt=w_in[:, U_DT[0]:U_DT[0] + LANE])
    if grp == "C":
        return dict(w_ff1=fw["w_ff1"], w_ff2=fw["w_ff2"], w_pg=fw["w_ple_gate"], w_ple=fw["w_ple"])
    wb = fw["w_branch"]
    wb0 = jnp.pad(wb[0].reshape(N_HEADS, V_HEAD, D_MODEL), ((0, 0), (0, LANE - V_HEAD), (0, 0))).reshape(N_HEADS * LANE, D_MODEL)
    return dict(
        w_uq=_head_pad_cols(fw["w_uq"], QK_NOPE + QK_ROPE, 0, QK_NOPE + QK_ROPE),
        w_uk=_head_pad_cols(fw["w_ukv"], QK_NOPE + V_HEAD, 0, QK_NOPE),
        w_uv=_head_pad_cols(fw["w_ukv"], QK_NOPE + V_HEAD, QK_NOPE, QK_NOPE + V_HEAD),
        wb=[wb0, wb[1], wb[2], wb[3]], w_out=fw["w_out"], ssd_conv_w=fw["ssd_conv_w"], lru_conv_w=fw["lru_conv_w"])


def _layer_params(sp, l):
    row = lambda n: sp[n][l][None, :]
    return dict(
        g_mix=row("g_mix"), q_norm=row("q_norm"), kv_norm=row("kv_norm"),
        pool=[sp["w_pool"][l].reshape(4 * LANE, LANE), row("pool_scale")],
        ssd=[None, row("ssd_conv_b"), _head8(sp["ssd_dt_bias"][l]), _head8(sp["ssd_a_log"][l]),
             _head8(sp["ssd_d"][l]), row("ssd_norm")],
        lru=[None, row("lru_conv_b"), _block_diag(sp["lru_w_a"][l]), row("lru_b_a"),
             _block_diag(sp["lru_w_i"][l]), row("lru_b_i"), row("lru_lambda")],
        g_mlp=row("g_mlp"), g_ple=row("g_ple"),
    )


_sig = jax.nn.sigmoid
_SSD_CARRY = [(HALO, SSD_XBC)] + [(LANE, LANE)] * 4


def _tiles(rows):
    return dict(tm=_pick(rows, 512), ta=_pick(rows, 512), tp=_pick(rows, 512), tl=_pick(rows, 512), ts=_pick(rows, 256))


def _mixer_tiles(u, dt32):
    return dict(
        cq=(u, 384, U_CQ[0] // 384), ckv=(u, 256, U_CKV[0] // 256), kr=(u, LANE, U_KR[0] // LANE),
        pool=(u, MIX, U_POOL[0] // MIX), z=(u, MIX, U_Z[0] // MIX), xbc=(u, SSD_XBC, U_XBC[0] // SSD_XBC),
        dt=(dt32, LANE, 0), lg=(u, MIX, U_LG[0] // MIX), lx=(u, MIX, U_LX[0] // MIX))


def _add_norm(acc, resid, g):
    x = acc + resid
    return x, _rms(x, g)


def _layer_fwd(x, h, p_bf, ctx, l, pr, g_next, cosf, sinf):
    rows = x.shape[0]
    ts = _tiles(rows)
    tm = ts["tm"]
    nm = lambda s: f"{s}_l{l}"
    r = dict(x=x)
    if h is None:
        (h,), _ = seq_fwd(nm("rms_in"), f_rms, [pr["g_mix"]], [(x, D_MODEL, 0)], [], [(D_MODEL, bf16)], tm)
    early = [h] + ([cosf, sinf, p_bf] + [a for v in pr.values() for a in (v if isinstance(v, list) else [v]) if a is not None]
                   if l == 0 else [])
    w = dict(_kernel_weights("A", ctx.weights(l, "A", early)))
    u = matmul(nm("w_in"), h, w["w_in"], outs=(U_DTYPE,))
    dt32 = matmul(nm("w_dt"), h, w["w_dt"])
    mt = _mixer_tiles(u, dt32)
    (cqn,), _ = seq_fwd(nm("rms_q"), f_rms, [pr["q_norm"]], [mt["cq"]], [], [(Q_LORA, bf16)], tm)
    (ckvn,), _ = seq_fwd(nm("rms_kv"), f_rms, [pr["kv_norm"]], [mt["ckv"]], [], [(KV_LORA, bf16)], tm)
    (yb,), pool_saved = seq_fwd(nm("pool"), f_pool, pr["pool"], [mt["pool"]], [(POOL_HALO, MIX)], [(MIX, bf16)], ts["tp"])
    w.update(_kernel_weights("B", ctx.weights(l, "B", yb)))
    pr = dict(pr, ssd=[w["ssd_conv_w"]] + pr["ssd"][1:], lru=[w["lru_conv_w"]] + pr["lru"][1:])
    q = matmul(nm("w_uq"), cqn, w["w_uq"])
    kn = matmul(nm("w_uk"), ckvn, w["w_uk"])
    vb = matmul(nm("w_uv"), ckvn, w["w_uv"], outs=(bf16,))
    hw = N_HEADS * LANE
    (qr, kr), _ = seq_fwd(nm("mla_prep"), f_prep, [], [(q, hw, 0), (kn, hw, 0), mt["kr"], (cosf, LANE, 0), (sinf, LANE, 0)],
                          [], [(hw, bf16), (hw, bf16)], tm)
    o, lse = attn_fwd(qr, kr, vb, ts["ta"])
    (yc,), ssd_saved = seq_fwd(nm("ssd"), f_ssd, pr["ssd"], [mt["z"], mt["xbc"], mt["dt"]], _SSD_CARRY, [(MIX, bf16)], SSD_CHUNK)
    (la, lu), lru_saved = seq_fwd(nm("lru_pre"), f_lru_pre, pr["lru"], [mt["lx"]], [(HALO, MIX)], [(MIX, f32), (MIX, f32)], ts["tl"])
    hh = scan_fwd(la, lu, ts["ts"])
    (yd,), _ = seq_fwd(nm("lru_post"), f_lru_post, [], [(hh, MIX, 0), mt["lg"]], [], [(MIX, bf16)], tm)
    ys = [o, yb, yc, yd]
    m, pres = merge_fwd(nm("merge"), ys, w["wb"], u)
    x1, h2 = matmul(nm("w_out"), m, w["w_out"], outs=(f32, bf16), epi=_add_norm, extras=[(x, 0)], rows=[pr["g_mlp"]])
    w.update(_kernel_weights("C", ctx.weights(l, "C", h2)))
    a1, act = matmul(nm("ff1"), h2, w["w_ff1"], outs=(bf16, bf16), epi=lambda acc: (acc, jnp.square(jnp.maximum(acc, 0.0))))
    x2, h3 = matmul(nm("ff2"), act, w["w_ff2"], outs=(f32, bf16), epi=_add_norm, extras=[(x1, 0)], rows=[pr["g_ple"]])
    gl = matmul(nm("ple_gate"), h3, w["w_pg"])
    if g_next is None:
        x3, pe = matmul(nm("ple"), p_bf, w["w_ple"], outs=(f32, f32), epi=lambda acc, g, xr: (xr + acc * _sig(g), acc),
                        extras=[(gl, 0), (x2, 0)])
        h_next = None
    else:
        def ple_norm(acc, g, xr, gn):
            xo = xr + acc * _sig(g)
            return xo, acc, _rms(xo, gn)

        x3, pe, h_next = matmul(nm("ple"), p_bf, w["w_ple"], outs=(f32, f32, bf16), epi=ple_norm,
                                extras=[(gl, 0), (x2, 0)], rows=[g_next])
    r.update(h=h, u=u, cqn=cqn, ckvn=ckvn, q=q, kn=kn, vb=vb, qr=qr, kr=kr, o=o, lse=lse, ys=ys, pres=pres, m=m, x1=x1,
             h2=h2, a1=a1, act=act, x2=x2, h3=h3, gl=gl, pe=pe, p_bf=p_bf, pool_saved=pool_saved, ssd_saved=ssd_saved,
             lru_saved=lru_saved, la=la, hh=hh, w=w, pr=pr, dt32=dt32)
    return x3, h_next, r


def _norm_bwd(dh, x, resid, g):
    rs = lax.rsqrt(jnp.mean(x * x, axis=-1, keepdims=True) + EPS)
    xhat = x * rs
    dxn = dh * g
    dx = rs * (dxn - xhat * jnp.mean(dxn * xhat, axis=-1, keepdims=True)) + resid
    return dx, jnp.sum(dh * xhat, axis=0, keepdims=True)


def _gate_bwd(d, g, pre):
    s = _sig(g.astype(f32))
    return d * s, d * pre.astype(f32) * s * (1.0 - s)


def _layer_bwd(dx3, r, ctx, l, cosf, sinf, tok, extra_small):
    rows = dx3.shape[0]
    ts = _tiles(rows)
    tm = ts["tm"]
    nm = lambda s: f"{s}_l{l}"
    u, w, pr = r["u"], r["w"], r["pr"]
    mt = _mixer_tiles(u, r["dt32"])
    g = {}
    full = lambda a: (a, a.shape[1], 0)
    dpe, dgl = ew(nm("ple_bwd"), _gate_bwd, [full(dx3), full(r["gl"]), full(r["pe"])], [(D_MODEL, bf16)] * 2, tm)
    g["w_ple"] = matmul(nm("d_w_ple"), r["p_bf"], dpe, ta=True, outs=(bf16,), deps=[tok] if tok is not None else [])
    g["w_pg"] = matmul(nm("d_w_pg"), r["h3"], dgl, ta=True, outs=(bf16,))
    dx2, g["g_ple"] = matmul(nm("d_h3"), dgl, w["w_pg"], tb=True, epi=_norm_bwd, extras=[(r["x2"], 0), (dx3, 0)],
                             rows=[pr["g_ple"]], row_sums=1)
    da1 = matmul(nm("d_act"), dx2, w["w_ff2"], tb=True, outs=(bf16,),
                 epi=lambda acc, a: (acc * 2.0 * jnp.maximum(a, 0.0),), extras=[(r["a1"], 0)])
    g["w_ff2"] = matmul(nm("d_w_ff2"), r["act"], dx2, ta=True, outs=(bf16,))
    g["w_ff1"] = matmul(nm("d_w_ff1"), r["h2"], da1, ta=True, outs=(bf16,), out_blocks=N_DEV)
    tok = ctx.grads(l, "C", dict(w_ff1=g["w_ff1"], w_ff2=g["w_ff2"], w_ple_gate=g["w_pg"], w_ple=g["w_ple"]))
    dx1, g["g_mlp"] = matmul(nm("d_h2"), da1, w["w_ff1"], tb=True, epi=_norm_bwd, extras=[(r["x1"], 0), (dx2, 0)],
                             rows=[pr["g_mlp"]], row_sums=1, deps=[tok])
    def merge_bwd(dm, *gates_and_pres):
        both = [_gate_bwd(dm, gates_and_pres[n], gates_and_pres[4 + n]) for n in range(4)]
        return tuple(b[0] for b in both) + tuple(b[1] for b in both)

    res = matmul(nm("d_merged"), dx1, w["w_out"], tb=True, outs=(bf16,) * 8, epi=merge_bwd,
                 extras=[(u, D_MODEL * n) for n in range(4)] + [(pre, 0) for pre in r["pres"]])
    dpres, dgates = list(res[:4]), list(res[4:])
    g["w_out"] = matmul(nm("d_w_out"), r["m"], dx1, ta=True, outs=(bf16,))
    dys, g["wb"] = [], []
    for n in range(4):
        g["wb"].append(matmul(nm(f"d_w_branch{n}"), r["ys"][n], dpres[n], ta=True, outs=(bf16,)))
        dys.append(matmul(nm(f"d_y{n}"), dpres[n], w["wb"][n], tb=True, outs=(bf16 if n == 0 else f32,)))
    dqr, dkr_, dv = attn_bwd(r["qr"], r["kr"], r["vb"], dys[0], r["o"], r["lse"], ts["ta"])
    _, (dq, dkn, dkrope) = seq_bwd(nm("mla_prep_bwd"), f_prep, [],
                                   [full(r["q"]), full(r["kn"]), mt["kr"], full(cosf), full(sinf)],
                                   [True, True, True, False, False], [], [dqr, dkr_], [bf16] * 3, tm)
    g["w_uq"] = matmul(nm("d_w_uq"), r["cqn"], dq, ta=True, outs=(bf16,))
    g["w_uk"] = matmul(nm("d_w_uk"), r["ckvn"], dkn, ta=True, outs=(bf16,))
    g["w_uv"] = matmul(nm("d_w_uv"), r["ckvn"], dv, ta=True, outs=(bf16,))
    dcqn = matmul(nm("d_cqn"), dq, w["w_uq"], tb=True)
    dckvn = matmul(nm("d_ckvn_k"), dkn, w["w_uk"], tb=True)
    dckvn = matmul(nm("d_ckvn_v"), dv, w["w_uv"], tb=True, epi=lambda acc, prev: (acc + prev,), extras=[(dckvn, 0)])
    (g["q_norm"],), (dcq,) = seq_bwd(nm("rms_q_bwd"), f_rms, [pr["q_norm"]], [mt["cq"]], [True], [], [dcqn], [bf16], tm)
    (g["kv_norm"],), (dckv,) = seq_bwd(nm("rms_kv_bwd"), f_rms, [pr["kv_norm"]], [mt["ckv"]], [True], [], [dckvn], [bf16], tm)
    g["pool"], (dpool,) = seq_bwd(nm("pool_bwd"), f_pool, pr["pool"], [mt["pool"]], [True], r["pool_saved"], [dys[1]],
                                  [bf16], ts["tp"])
    g["ssd"], (dz, dxbc, ddt) = seq_bwd(nm("ssd_bwd"), f_ssd, pr["ssd"], [mt["z"], mt["xbc"], mt["dt"]], [True] * 3,
                                        r["ssd_saved"], [dys[2]], [bf16] * 3, SSD_CHUNK)
    _, (dhh, dlg) = seq_bwd(nm("lru_post_bwd"), f_lru_post, [], [full(r["hh"]), mt["lg"]], [True, True], [], [dys[3]],
                            [f32, bf16], tm)
    da, du = scan_bwd(r["la"], r["hh"], dhh, ts["ts"])
    g["lru"], (dlx,) = seq_bwd(nm("lru_pre_bwd"), f_lru_pre, pr["lru"], [mt["lx"]], [True], r["lru_saved"], [da, du],
                               [bf16], ts["tl"])
    dk = _head_unpad_cols(g["w_uk"], QK_NOPE)
    dv_ = _head_unpad_cols(g["w_uv"], V_HEAD)
    wb0 = g["wb"][0].reshape(N_HEADS, LANE, D_MODEL)[:, :V_HEAD].reshape(MIX, D_MODEL)
    ssd, lru, pool = g["ssd"], g["lru"], g["pool"]
    tok = ctx.grads(l, "B", dict(
        w_uq=_head_unpad_cols(g["w_uq"], QK_NOPE + QK_ROPE).reshape(Q_LORA, -1),
        w_ukv=jnp.concatenate([dk, dv_], axis=2).reshape(KV_LORA, -1), ssd_conv_w=ssd[0], lru_conv_w=lru[0],
        w_branch=jnp.stack([wb0, g["wb"][1], g["wb"][2], g["wb"][3]]), w_out=g["w_out"]))
    du_p = jnp.concatenate(dgates + [dpool, dz, dlg, dlx, dxbc, dcq, dkrope, dckv, ddt,
                                     jnp.zeros((rows, U_COLS - U_DT[0] - LANE), bf16)], axis=1)
    small = dict(
        q_norm=g["q_norm"][0], kv_norm=g["kv_norm"][0],
        w_pool=pool[0].reshape(4, LANE, LANE), pool_scale=pool[1][0],
        ssd_conv_b=ssd[1][0], ssd_dt_bias=ssd[2][0, :8], ssd_a_log=ssd[3][0, :8], ssd_d=ssd[4][0, :8], ssd_norm=ssd[5][0],
        lru_conv_b=lru[1][0], lru_w_a=_block_diag_inv(lru[2]), lru_b_a=lru[3][0], lru_w_i=_block_diag_inv(lru[4]),
        lru_b_i=lru[5][0], lru_lambda=lru[6][0], g_mlp=g["g_mlp"][0], g_ple=g["g_ple"][0])
    tok_small = ctx.small(f"l{l}", [(n, l, small[n]) for n in SMALL if n in small] + extra_small)
    g_w_in = matmul(nm("d_w_in"), r["h"], du_p, ta=True, outs=(bf16,), deps=[tok, tok_small])
    tok = ctx.grads(l, "A", dict(w_in=_w_in_blocks(g_w_in)))
    dx, g_mix = matmul(nm("d_h"), du_p, w["w_in"], tb=True, epi=_norm_bwd, extras=[(r["x"], 0), (dx1, 0)],
                       rows=[pr["g_mix"]], row_sums=1, deps=[tok])
    return dx, tok, ("g_mix", l, g_mix[0])


def _rope_tables(positions):
    inv = 1.0 / (ROPE_THETA ** (jnp.arange(0, QK_ROPE, 2, dtype=f32) / QK_ROPE))
    ang = positions.astype(f32)[:, None] * inv
    cos, sin = jnp.cos(ang), jnp.sin(ang)
    rows = positions.shape[0]
    pad = jnp.zeros((rows, LANE - KR_LANE - QK_ROPE), f32)
    cosf = jnp.concatenate([jnp.ones((rows, KR_LANE), f32), cos, cos, pad], axis=1)
    sinf = jnp.concatenate([jnp.zeros((rows, KR_LANE), f32), -sin, sin, pad], axis=1)
    return cosf, sinf


WEIGHTS = ['g_mix', 'w_in', 'q_norm', 'w_uq', 'kv_norm', 'w_ukv', 'w_pool', 'pool_scale', 'ssd_conv_w', 'ssd_conv_b',
           'ssd_dt_bias', 'ssd_a_log', 'ssd_d', 'ssd_norm', 'lru_conv_w', 'lru_conv_b', 'lru_w_a', 'lru_b_a', 'lru_w_i',
           'lru_b_i', 'lru_lambda', 'w_branch', 'w_out', 'g_mlp', 'w_ff1', 'w_ff2', 'g_ple', 'w_ple_gate', 'w_ple', 'g_final']
SHARDED = dict(w_in=2, w_uq=2, w_ukv=2, ssd_conv_w=2, lru_conv_w=2, w_branch=3, w_out=1, w_ff1=2, w_ff2=1,
               w_ple_gate=1, w_ple=2)
F32_PAYLOAD = ("ssd_conv_w", "lru_conv_w")
DEPTH = 2


SMALL = [n for n in WEIGHTS if n not in SHARDED and n != "g_final"]


def local_step(x, p, positions, tgt, sp, ctx):
    cosf, sinf = _rope_tables(positions)
    res, h = [], None
    for l in range(DEPTH):
        g_next = sp["g_mix"][l + 1][None, :] if l + 1 < DEPTH else None
        x, h, r = _layer_fwd(x, h, p[l].astype(bf16), ctx, l, _layer_params(sp, l), g_next, cosf, sinf)
        res.append(r)
    loss8, dx, dgf = loss_head(x, tgt, sp["g_final"][None, :], _pick(x.shape[0], 512))
    tok = None
    pending = ("g_final", None, dgf[0])
    for l in reversed(range(DEPTH)):
        dx, tok, pending = _layer_bwd(dx, res[l], ctx, l, cosf, sinf, tok, [pending])
    ctx.small("last", [pending])
    return loss8[0, 0], dx


def _payload(name, w):
    return w if name in F32_PAYLOAD else w.astype(bf16)


def _blocks(name, g):
    ax = SHARDED[name] - 1
    shape = list(g.shape)
    shape[ax:ax + 1] = [N_DEV, shape[ax] // N_DEV]
    return _payload(name, jnp.moveaxis(g.reshape(shape), ax, 0))


def _assemble(name, shards):
    ax = SHARDED[name] - 1
    shape = list(shards.shape[1:])
    shape[ax] *= N_DEV
    return jnp.moveaxis(shards, 0, ax).reshape(shape)


class _Exchanges:
    def __init__(self, wts):
        self.wts = wts
        self.ag, self.rs, self.sm = {}, {}, {}
        tok = None
        for l in range(DEPTH):
            for grp, names in GROUPS.items():
                h = exchange_start(f"ag_start_{grp}{l}", [_payload(n, wts[n][l]) for n in names], True,
                                   deps=[] if tok is None else [tok])
                tok = h["token"]
                self.ag[(l, grp)] = h
        self.all_started = tok

    def weights(self, l, grp, after):
        afters = list(after) if isinstance(after, (list, tuple)) else [after]
        if (l, grp) == (0, "A"):
            afters.append(self.all_started)
        got = exchange_wait(f"ag_wait_{grp}{l}", self.ag[(l, grp)], afters)
        out = {}
        for n, a in zip(GROUPS[grp], got):
            out[n] = a if n == "w_in" else _assemble(n, a)
        return out

    def grads(self, l, grp, g):
        cut = lambda n: g[n].ndim == self.wts[n].ndim
        h = exchange_start(f"rs_start_{grp}{l}", [g[n] if cut(n) else _blocks(n, g[n]) for n in GROUPS[grp]], False)
        self.rs[(l, grp)] = h
        return h["token"]

    def small(self, tag, entries):
        entries = sorted(entries, key=lambda e: e[2].size % LANE != 0)
        flat = jnp.concatenate([a.reshape(-1) for _, _, a in entries])
        flat = jnp.pad(flat, (0, (-flat.shape[0]) % (8 * LANE))).reshape(-1, LANE)
        h = exchange_start(f"small_start_{tag}", [flat], True)
        self.sm[tag] = (h, [(n, l, a.shape) for n, l, a in entries])
        return h["token"]

    def collect(self, groups, after):
        parts = {}
        for grp in groups:
            for l in reversed(range(DEPTH)):
                got = exchange_wait(f"rs_wait_{grp}{l}", self.rs[(l, grp)], [after])
                for n, a in zip(GROUPS[grp], got):
                    parts.setdefault(n, [None] * DEPTH)[l] = a
        return parts

    def collect_small(self, after):
        gots, where, parts = [], {}, {}
        for tag, (h, layout) in self.sm.items():
            (got,) = exchange_wait(f"small_wait_{tag}", h, [after])
            got = got.reshape(N_DEV, -1)
            off = 0
            for n, l, shape in layout:
                size = 1
                for d in shape:
                    size *= d
                if len(shape) == 1 and size % LANE == 0 and off % LANE == 0:
                    where.setdefault(n, [None] * (1 if l is None else DEPTH))[l or 0] = (len(gots), off)
                else:
                    part = got[:, off:off + size].reshape((N_DEV,) + tuple(shape))
                    if l is None:
                        parts[n] = [part]
                    else:
                        parts.setdefault(n, [None] * DEPTH)[l] = part
                off += size
            gots.append(got)
        return gots, where, parts


def kernel(x, p, positions, g_mix, w_in, q_norm, w_uq, kv_norm, w_ukv, w_pool, pool_scale, ssd_conv_w, ssd_conv_b,
           ssd_dt_bias, ssd_a_log, ssd_d, ssd_norm, lru_conv_w, lru_conv_b, lru_w_a, lru_b_a, lru_w_i, lru_b_i,
           lru_lambda, w_branch, w_out, g_mlp, w_ff1, w_ff2, g_ple, w_ple_gate, w_ple, g_final, loss_target, m_g_mix,
           m_w_in, m_q_norm, m_w_uq, m_kv_norm, m_w_ukv, m_w_pool, m_pool_scale, m_ssd_conv_w, m_ssd_conv_b,
           m_ssd_dt_bias, m_ssd_a_log, m_ssd_d, m_ssd_norm, m_lru_conv_w, m_lru_conv_b, m_lru_w_a, m_lru_b_a,
           m_lru_w_i, m_lru_b_i, m_lru_lambda, m_w_branch, m_w_out, m_g_mlp, m_w_ff1, m_w_ff2, m_g_ple, m_w_ple_gate,
           m_w_ple, m_g_final, v_g_mix, v_w_in, v_q_norm, v_w_uq, v_kv_norm, v_w_ukv, v_w_pool, v_pool_scale,
           v_ssd_conv_w, v_ssd_conv_b, v_ssd_dt_bias, v_ssd_a_log, v_ssd_d, v_ssd_norm, v_lru_conv_w, v_lru_conv_b,
           v_lru_w_a, v_lru_b_a, v_lru_w_i, v_lru_b_i, v_lru_lambda, v_w_branch, v_w_out, v_g_mlp, v_w_ff1, v_w_ff2,
           v_g_ple, v_w_ple_gate, v_w_ple, v_g_final):
    given = dict(locals())
    wts = {n: given[n] for n in WEIGHTS}
    ctx = _Exchanges(wts)
    loss, grad_x = local_step(x[0], p[:, 0], positions[0], loss_target[0], wts, ctx)

    def update(parts):
        out = {}
        for n, eight in parts.items():
            step = adamw_columns if n == "w_in" else adamw
            out[n] = step(f"adamw_{n}", eight, wts[n], given["m_" + n], given["v_" + n])
        return out

    outs = update(ctx.collect(("C", "B"), grad_x))
    late = outs["w_ff1"][1]
    outs.update(update(ctx.collect(("A",), late)))
    gots, where, parts = ctx.collect_small(late)
    outs.update(update(parts))
    names = sorted(where)
    rows = lambda a: a[None] if a.ndim == 1 else a
    res = adamw_packed("adamw_vectors", gots, [where[n] for n in names], [rows(wts[n]) for n in names],
                       [rows(given["m_" + n]) for n in names], [rows(given["v_" + n]) for n in names])
    for n, four in zip(names, res):
        outs[n] = [a[0] for a in four] if wts[n].ndim == 1 else four
    loss = lax.psum(loss, AXES)
    return (loss, grad_x[None], *[outs[n][0] for n in WEIGHTS], *[outs[n][1] for n in WEIGHTS],
            *[outs[n][2] for n in WEIGHTS], *[outs[n][3] for n in WEIGHTS])
```

```python
import functools

import jax
import jax.numpy as jnp
from jax import lax
from jax.experimental import pallas as pl
from jax.experimental.pallas import tpu as pltpu

f32 = jnp.float32
bf16 = jnp.bfloat16

D_MODEL = 1024
MIX = 512
N_HEADS = 8
QK_NOPE, QK_ROPE, V_HEAD = 64, 32, 64
Q_LORA, KV_LORA = 384, 256
ROPE_THETA = 10000.0
POOL_WINDOWS = (2, 4, 8, 16)
SSD_CHUNK = 128
SSD_XBC = 768
CONV_W = 4
LRU_C = 8.0
D_FF = 4096
EPS = 1e-6
IN_COLS = 7592
ADAM_LR, ADAM_B1, ADAM_B2, ADAM_EPS, ADAM_WD, ADAM_STEP = 0.001, 0.9, 0.999, 1e-08, 0.01, 10

LANE = 128
HALO = 8
POOL_HALO = 16
VMEM_LIMIT = 56 * 1024 * 1024
MATMUL_MAX_K_TILE = 4096
MATMUL_ACC_PASS_WEIGHT = 0.3
MATMUL_VMEM_BUDGET = 40 * 1024 * 1024
N_DEV = 8
AXES = ("x", "y", "c")

U_COLS = 8192
U_GATES, U_POOL, U_Z, U_LG, U_LX, U_XBC, U_CQ, U_KR, U_CKV, U_DT = (
    (0, 4096), (4096, 512), (4608, 512), (5120, 512), (5632, 512), (6144, 768),
    (6912, 384), (7296, 128), (7424, 256), (7680, 128))
KR_LANE = 64
U_DTYPE = bf16


def _cp(sem):
    return pltpu.CompilerParams(dimension_semantics=sem, vmem_limit_bytes=VMEM_LIMIT)


def _pick(dim, pref):
    if dim <= pref:
        return dim
    t = pref
    while t >= LANE:
        if dim % t == 0:
            return t
        t -= LANE
    t = pref
    while dim % t:
        t -= 8
    return t


@functools.partial(jax.custom_vjp, nondiff_argnums=(1,))
def shift_down(x, k):
    row = lax.broadcasted_iota(jnp.int32, x.shape, 0)
    return jnp.where(row >= k, pltpu.roll(x, k, 0), 0.0)


def _shift_down_fwd(x, k):
    return shift_down(x, k), None


def _shift_down_bwd(k, _, g):
    r = g.shape[0]
    row = lax.broadcasted_iota(jnp.int32, g.shape, 0)
    return (jnp.where(row < r - k, pltpu.roll(g, r - k, 0), 0.0),)


shift_down.defvjp(_shift_down_fwd, _shift_down_bwd)


def _tile_spec(tm, width, cb, n=None):
    if n is None:
        return pl.BlockSpec((tm, width), lambda i: (i, cb))
    return pl.BlockSpec((tm, width), lambda i: (n - 1 - i, cb))


def _const_spec(shape):
    nd = len(shape)
    return pl.BlockSpec(shape, lambda i: (0,) * nd)


def seq_fwd(name, f, params, tiles, carries, outs, tm):
    rows = tiles[0][0].shape[0]
    n = rows // tm
    np_, nt, no, nc = len(params), len(tiles), len(outs), len(carries)

    def body(*refs):
        p_refs = refs[:np_]
        t_refs = refs[np_:np_ + nt]
        o_refs = refs[np_ + nt:np_ + nt + no]
        s_refs = refs[np_ + nt + no:np_ + nt + no + nc]
        c_refs = refs[np_ + nt + no + nc:]
        i = pl.program_id(0)

        @pl.when(i == 0)
        def _():
            for c in c_refs:
                c[...] = jnp.zeros_like(c)

        cvals = [c[...] for c in c_refs]
        for s, c in zip(s_refs, cvals):
            s[0] = c
        o, newc = f(i, [r[...] for r in p_refs], cvals, [r[...].astype(f32) for r in t_refs])
        for r, v in zip(o_refs, o):
            r[...] = v.astype(r.dtype)
        for r, v in zip(c_refs, newc):
            r[...] = v

    in_specs = [_const_spec(p.shape) for p in params] + [_tile_spec(tm, w, cb) for (_, w, cb) in tiles]
    out_specs = [_tile_spec(tm, w, 0) for (w, _) in outs]
    out_specs += [pl.BlockSpec((1,) + tuple(c), lambda i, nd=len(c): (i,) + (0,) * nd) for c in carries]
    out_shape = [jax.ShapeDtypeStruct((rows, w), dt) for (w, dt) in outs]
    out_shape += [jax.ShapeDtypeStruct((n,) + tuple(c), f32) for c in carries]
    res = pl.pallas_call(
        body, name=name, grid=(n,), in_specs=in_specs, out_specs=out_specs, out_shape=out_shape,
        scratch_shapes=[pltpu.VMEM(tuple(c), f32) for c in carries],
        compiler_params=_cp(("arbitrary",)),
    )(*params, *[t[0] for t in tiles])
    return list(res[:no]), list(res[no:])


def seq_bwd(name, f, params, tiles, diff, saved, douts, gdtypes, tm):
    rows = tiles[0][0].shape[0]
    n = rows // tm
    np_, nt, nc, nd = len(params), len(tiles), len(saved), len(douts)
    didx = [k for k, d in enumerate(diff) if d]
    ng = len(didx)

    def body(*refs):
        p_refs = refs[:np_]
        t_refs = refs[np_:np_ + nt]
        s_refs = refs[np_ + nt:np_ + nt + nc]
        d_refs = refs[np_ + nt + nc:np_ + nt + nc + nd]
        pos = np_ + nt + nc + nd
        dp_refs = refs[pos:pos + np_]
        dt_refs = refs[pos + np_:pos + np_ + ng]
        dc_refs = refs[pos + np_ + ng:]
        i = pl.program_id(0)
        step = n - 1 - i

        @pl.when(i == 0)
        def _():
            for r in dp_refs:
                r[...] = jnp.zeros_like(r)
            for r in dc_refs:
                r[...] = jnp.zeros_like(r)

        pvals = [r[...] for r in p_refs]
        cvals = [r[0] for r in s_refs]
        xvals = [r[...].astype(f32) for r in t_refs]

        def fn(p, c, xd):
            x = list(xvals)
            for k, v in zip(didx, xd):
                x[k] = v
            return f(step, p, c, x)

        _, vjp = jax.vjp(fn, pvals, cvals, [xvals[k] for k in didx])
        dp, dc, dx = vjp(([r[...].astype(f32) for r in d_refs], [r[...] for r in dc_refs]))
        for r, v in zip(dp_refs, dp):
            r[...] += v
        for r, v in zip(dc_refs, dc):
            r[...] = v
        for r, v in zip(dt_refs, dx):
            r[...] = v.astype(r.dtype)

    in_specs = [_const_spec(p.shape) for p in params] + [_tile_spec(tm, w, cb, n) for (_, w, cb) in tiles]
    in_specs += [pl.BlockSpec((1,) + tuple(s.shape[1:]), lambda i, nd_=s.ndim - 1: (n - 1 - i,) + (0,) * nd_) for s in saved]
    in_specs += [_tile_spec(tm, d.shape[1], 0, n) for d in douts]
    args = list(params) + [t[0] for t in tiles] + list(saved) + list(douts)
    out_specs = [_const_spec(p.shape) for p in params] + [_tile_spec(tm, tiles[k][1], 0, n) for k in didx]
    out_shape = [jax.ShapeDtypeStruct(p.shape, f32) for p in params]
    out_shape += [jax.ShapeDtypeStruct((rows, tiles[k][1]), dt) for k, dt in zip(didx, gdtypes)]
    res = pl.pallas_call(
        body, name=name, grid=(n,), in_specs=in_specs, out_specs=out_specs, out_shape=out_shape,
        scratch_shapes=[pltpu.VMEM(tuple(s.shape[1:]), f32) for s in saved],
        compiler_params=_cp(("arbitrary",)),
    )(*args)
    return list(res[:np_]), list(res[np_:])


def _halvings(dim, lo, hi):
    t, out = _pick(dim, hi), []
    while t >= min(lo, dim) and dim % t == 0:
        out.append(t)
        if t % 2 or (t // 2) % 8:
            break
        t //= 2
    return out


def _matmul_tiles(m, n, k, a_item, b_item, per_out, max_tn=1024, whole_rows=False):
    def vmem_bytes(tm, tn, tk):
        acc = 4 if k // tk > 1 else 0
        return 2 * (tm * tk * a_item + tk * tn * b_item + tm * tn * per_out) + tm * tn * acc

    def traffic(tm, tn, tk):
        nk = k // tk
        return (m * k * a_item * (1 if nk == 1 else n // tn) + k * n * b_item * (m // tm)
                + (nk - 1) * m * n * 8 * MATMUL_ACC_PASS_WEIGHT)

    cands = [(traffic(tm, tn, tk), -tm * tn, tm, tn, tk)
             for tk in _halvings(k, 512, MATMUL_MAX_K_TILE) for tm in _halvings(m, 256, 4096)
             for tn in ([n] if whole_rows else _halvings(n, 512, min(1024, max_tn)))
             if vmem_bytes(tm, tn, tk) <= MATMUL_VMEM_BUDGET]
    return min(cands)[2:]


def matmul(name, a, b, *, ta=False, tb=False, outs=(f32,), epi=None, extras=(), rows=(), row_sums=0, deps=(),
           out_blocks=0):
    m, k = (a.shape[1], a.shape[0]) if ta else a.shape
    n = b.shape[0] if tb else b.shape[1]
    per_out = sum(jnp.dtype(dt).itemsize for dt in outs) + sum(e[0].dtype.itemsize for e in extras)
    whole_rows = bool(rows) or row_sums > 0
    tm, tn, tk = _matmul_tiles(m, n, k, a.dtype.itemsize, b.dtype.itemsize, per_out,
                               n // out_blocks if out_blocks else n, whole_rows)
    nk = k // tk
    ne, nr, nd, no = len(extras), len(rows), len(deps), len(outs)
    dims = (((0 if ta else 1,), (1 if tb else 0,)), ((), ()))

    def body(*refs):
        a_ref, b_ref = refs[0], refs[1]
        e_refs = refs[2:2 + ne]
        r_refs = refs[2 + ne:2 + ne + nr]
        o_refs = refs[2 + ne + nr + nd:2 + ne + nr + nd + no]
        s_refs = refs[2 + ne + nr + nd + no:2 + ne + nr + nd + no + row_sums]
        i, kk = pl.program_id(0), pl.program_id(2)
        part = lax.dot_general(a_ref[...].astype(bf16), b_ref[...].astype(bf16), dims, preferred_element_type=f32)

        def finish(total):
            res = (total,) if epi is None else epi(total, *[e[...] for e in e_refs], *[r[...] for r in r_refs])
            for r, v in zip(o_refs, res[:no]):
                r[...] = v.astype(r.dtype)
            for r, v in zip(s_refs, res[no:]):
                v8 = jnp.broadcast_to(v, r.shape)

                @pl.when(i == 0)
                def _(r=r, v8=v8):
                    r[...] = v8

                @pl.when(i > 0)
                def _(r=r, v8=v8):
                    r[...] += v8

        if nk == 1:
            finish(part)
            return
        acc = refs[-1]

        @pl.when(kk == 0)
        def _():
            acc[...] = part

        @pl.when(jnp.logical_and(kk > 0, kk < nk - 1))
        def _():
            acc[...] += part

        @pl.when(kk == nk - 1)
        def _():
            finish(acc[...] + part)

    a_spec = pl.BlockSpec((tk, tm), lambda i, j, q: (q, i)) if ta else pl.BlockSpec((tm, tk), lambda i, j, q: (i, q))
    b_spec = pl.BlockSpec((tn, tk), lambda i, j, q: (j, q)) if tb else pl.BlockSpec((tk, tn), lambda i, j, q: (q, j))
    def e_spec(e):
        if len(e) == 3:
            return pl.BlockSpec((tm, e[2]), lambda i, j, q, cb=e[1] // e[2]: (i, cb))
        assert e[1] % tn == 0
        return pl.BlockSpec((tm, tn), lambda i, j, q, off=e[1] // tn: (i, off + j))

    e_specs = [e_spec(e) for e in extras]
    r_specs = [pl.BlockSpec((1, tn), lambda i, j, q: (0, j)) for _ in rows]
    if out_blocks:
        per = n // out_blocks // tn
        out_spec = pl.BlockSpec((None, tm, tn), lambda i, j, q: (j // per, i, j % per))
        out_dims = (out_blocks, m, n // out_blocks)
    else:
        out_spec = pl.BlockSpec((tm, tn), lambda i, j, q: (i, j))
        out_dims = (m, n)
    res = pl.pallas_call(
        body, name=name, grid=(m // tm, n // tn, nk),
        in_specs=[a_spec, b_spec] + e_specs + r_specs + [pl.BlockSpec(memory_space=pl.ANY) for _ in deps],
        out_specs=[out_spec for _ in outs] + [pl.BlockSpec((8, tn), lambda i, j, q: (0, j))] * row_sums,
        out_shape=[jax.ShapeDtypeStruct(out_dims, dt) for dt in outs] + [jax.ShapeDtypeStruct((8, n), f32)] * row_sums,
        scratch_shapes=[pltpu.VMEM((tm, tn), f32)] if nk > 1 else [],
        compiler_params=_cp(("arbitrary" if row_sums else "parallel", "parallel", "arbitrary")),
    )(a, b, *[e[0] for e in extras], *rows, *deps)
    return res[0] if len(res) == 1 else tuple(res)


def merge_fwd(name, ys, wbs, u):
    rows, n_out = ys[0].shape[0], wbs[0].shape[1]
    nb = len(ys)
    tm, tn = _pick(rows, 512), _pick(n_out, 512)

    def body(*refs):
        y_refs, w_refs, g_refs = refs[:nb], refs[nb:2 * nb], refs[2 * nb:3 * nb]
        m_ref, p_refs = refs[3 * nb], refs[3 * nb + 1:]
        total = None
        for y_ref, w_ref, g_ref, p_ref in zip(y_refs, w_refs, g_refs, p_refs):
            pre = jnp.dot(y_ref[...], w_ref[...], preferred_element_type=f32)
            p_ref[...] = pre.astype(p_ref.dtype)
            term = jax.nn.sigmoid(g_ref[...].astype(f32)) * pre
            total = term if total is None else total + term
        m_ref[...] = total.astype(m_ref.dtype)

    in_specs = [pl.BlockSpec((tm, y.shape[1]), lambda i, j: (i, 0)) for y in ys]
    in_specs += [pl.BlockSpec((w.shape[0], tn), lambda i, j: (0, j)) for w in wbs]
    in_specs += [pl.BlockSpec((tm, tn), lambda i, j, off=n * (n_out // tn): (i, off + j)) for n in range(nb)]
    out_spec = pl.BlockSpec((tm, tn), lambda i, j: (i, j))
    res = pl.pallas_call(
        body, name=name, grid=(rows // tm, n_out // tn), in_specs=in_specs, out_specs=[out_spec] * (nb + 1),
        out_shape=[jax.ShapeDtypeStruct((rows, n_out), bf16)] * (nb + 1),
        compiler_params=_cp(("parallel", "parallel")),
    )(*ys, *wbs, *([u] * nb))
    return res[0], list(res[1:])


ATT_SCALE = (QK_NOPE + QK_ROPE) ** -0.5
LN2 = 0.6931471805599453
ATT_C = ATT_SCALE / LN2
NT = (((1,), (1,)), ((), ()))
TN = (((0,), (0,)), ((), ()))


def _causal(tq, tk):
    return lax.broadcasted_iota(jnp.int32, (tq, tk), 0) >= lax.broadcasted_iota(jnp.int32, (tq, tk), 1)


def _tri_pairs(n, by_column):
    if by_column:
        pairs = [(i, j) for j in range(n) for i in range(j, n)]
    else:
        pairs = [(i, j) for i in range(n) for j in range(i + 1)]
    return (jnp.asarray([a for a, _ in pairs], jnp.int32), jnp.asarray([b for _, b in pairs], jnp.int32))


FWD_HEADS_PER_STEP = 8
HEADS_PER_STEP = 4
HEAD_PAIR = HEADS_PER_STEP * LANE


def attn_fwd(q, k, v, t):
    rows = q.shape[0]
    n = rows // t
    it, jt = _tri_pairs(n, False)

    def body(it_ref, jt_ref, q_ref, k_ref, v_ref, o_ref, lse_ref, m_s, l_s, acc_s):
        s_id = pl.program_id(1)
        i, j = it_ref[s_id], jt_ref[s_id]

        @pl.when(j == 0)
        def _():
            m_s[...] = jnp.full_like(m_s, -jnp.inf)
            l_s[...] = jnp.zeros_like(l_s)
            acc_s[...] = jnp.zeros_like(acc_s)

        def step(diag):
            for hh in range(FWD_HEADS_PER_STEP):
                sl = slice(LANE * hh, LANE * (hh + 1))
                s = lax.dot_general(q_ref[:, sl], k_ref[:, sl], NT, preferred_element_type=f32)
                if diag:
                    s = jnp.where(_causal(t, t), s, -jnp.inf)
                m_prev = m_s[:, sl]
                m_new = jnp.maximum(m_prev, jnp.max(s, axis=1, keepdims=True))
                alpha = jnp.exp2(m_prev - m_new)
                p = jnp.exp2(s - m_new[:, :1])
                l_s[:, sl] = alpha * l_s[:, sl] + jnp.sum(p, axis=1, keepdims=True)
                acc_s[:, sl] = alpha * acc_s[:, sl] + jnp.dot(p.astype(bf16), v_ref[:, sl], preferred_element_type=f32)
                m_s[:, sl] = m_new

        pl.when(j < i)(lambda: step(False))

        @pl.when(j == i)
        def _():
            step(True)
            o_ref[...] = (acc_s[...] / l_s[...]).astype(o_ref.dtype)
            lse_ref[...] = m_s[...] + jnp.log2(l_s[...])

    width = FWD_HEADS_PER_STEP * LANE
    qs = pl.BlockSpec((t, width), lambda h, s, it_, jt_: (it_[s], h))
    ks = pl.BlockSpec((t, width), lambda h, s, it_, jt_: (jt_[s], h))
    hw = N_HEADS * LANE
    return pl.pallas_call(
        body, name="attn_fwd",
        grid_spec=pltpu.PrefetchScalarGridSpec(
            num_scalar_prefetch=2, grid=(hw // width, it.shape[0]), in_specs=[qs, ks, ks], out_specs=[qs, qs],
            scratch_shapes=[pltpu.VMEM((t, width), f32)] * 3),
        out_shape=[jax.ShapeDtypeStruct((rows, hw), bf16), jax.ShapeDtypeStruct((rows, hw), f32)],
        compiler_params=_cp(("parallel", "arbitrary")),
    )(it, jt, q, k, v)


def attn_bwd(q, k, v, do, o, lse, t):
    rows = q.shape[0]
    n = rows // t
    it, jt = _tri_pairs(n, True)

    def body(it_ref, jt_ref, q_ref, k_ref, v_ref, do_ref, o_ref, lse_ref, dq_ref, dk_ref, dv_ref, dk_s, dv_s):
        s_id = pl.program_id(1)
        i, j = it_ref[s_id], jt_ref[s_id]

        @pl.when(s_id == 0)
        def _():
            dq_ref[...] = jnp.zeros_like(dq_ref)

        @pl.when(i == j)
        def _():
            dk_s[...] = jnp.zeros_like(dk_s)
            dv_s[...] = jnp.zeros_like(dv_s)

        q_rows = pl.ds(pl.multiple_of(i * t, t), t)

        def step(diag):
            for hh in range(HEADS_PER_STEP):
                sl = slice(LANE * hh, LANE * (hh + 1))
                qh, kh, vh, doh = q_ref[:, sl], k_ref[:, sl], v_ref[:, sl], do_ref[:, sl]
                s = lax.dot_general(qh, kh, NT, preferred_element_type=f32)
                p = jnp.exp2(s - lse_ref[:, sl][:, :1])
                if diag:
                    p = jnp.where(_causal(t, t), p, 0.0)
                dp = lax.dot_general(doh, vh, NT, preferred_element_type=f32)
                delta = jnp.sum(doh.astype(f32) * o_ref[:, sl].astype(f32), axis=1, keepdims=True)
                ds = (p * (dp - delta) * LN2).astype(bf16)
                dv_s[:, sl] += lax.dot_general(p.astype(bf16), doh, TN, preferred_element_type=f32)
                dk_s[:, sl] += lax.dot_general(ds, qh, TN, preferred_element_type=f32)
                dq_ref[q_rows, sl] += jnp.dot(ds, kh, preferred_element_type=f32)

        pl.when(i > j)(lambda: step(False))
        pl.when(i == j)(lambda: step(True))

        @pl.when(i == n - 1)
        def _():
            dk_ref[...] = dk_s[...]
            dv_ref[...] = dv_s[...]

    qs = pl.BlockSpec((t, HEAD_PAIR), lambda h, s, it_, jt_: (it_[s], h))
    ks = pl.BlockSpec((t, HEAD_PAIR), lambda h, s, it_, jt_: (jt_[s], h))
    dqs = pl.BlockSpec((rows, HEAD_PAIR), lambda h, s, it_, jt_: (0, h))
    hw = N_HEADS * LANE
    return pl.pallas_call(
        body, name="attn_bwd",
        grid_spec=pltpu.PrefetchScalarGridSpec(
            num_scalar_prefetch=2, grid=(hw // HEAD_PAIR, it.shape[0]), in_specs=[qs, ks, ks, qs, qs, qs],
            out_specs=[dqs, ks, ks], scratch_shapes=[pltpu.VMEM((t, HEAD_PAIR), f32)] * 2),
        out_shape=[jax.ShapeDtypeStruct((rows, hw), f32)] * 3,
        compiler_params=_cp(("parallel", "arbitrary")),
    )(it, jt, q, k, v, do, o, lse)


def _steps(tm):
    k, out = 1, []
    while k < tm:
        out.append(k)
        k *= 2
    return out


def scan_fwd(a, u, tm):
    rows, ch = a.shape
    n = rows // tm

    def body(a_ref, u_ref, h_ref, h_s):
        @pl.when(pl.program_id(0) == 0)
        def _():
            h_s[...] = jnp.zeros_like(h_s)

        av, bv = a_ref[...], u_ref[...]
        row = lax.broadcasted_iota(jnp.int32, av.shape, 0)
        for k in _steps(tm):
            a_sh = jnp.where(row >= k, pltpu.roll(av, k, 0), 1.0)
            b_sh = jnp.where(row >= k, pltpu.roll(bv, k, 0), 0.0)
            bv = av * b_sh + bv
            av = av * a_sh
        h = bv + av * h_s[HALO - 1:HALO, :]
        h_ref[...] = h
        h_s[...] = h[tm - HALO:, :]

    spec = pl.BlockSpec((tm, ch), lambda i: (i, 0))
    return pl.pallas_call(
        body, name="lru_scan_fwd", grid=(n,), in_specs=[spec, spec], out_specs=spec,
        out_shape=jax.ShapeDtypeStruct((rows, ch), f32), scratch_shapes=[pltpu.VMEM((HALO, ch), f32)],
        compiler_params=_cp(("arbitrary",)),
    )(a, u)


def scan_bwd(a, h, dh, tm):
    rows, ch = a.shape
    n = rows // tm
    per = tm // HALO

    def body(a_ref, h_ref, hp_ref, dh_ref, da_ref, du_ref, g_s, a_s):
        i = pl.program_id(0)
        step = n - 1 - i

        @pl.when(i == 0)
        def _():
            g_s[...] = jnp.zeros_like(g_s)
            a_s[...] = jnp.zeros_like(a_s)

        a0 = a_ref[...]
        row = lax.broadcasted_iota(jnp.int32, a0.shape, 0)
        av = jnp.where(row < tm - 1, pltpu.roll(a0, tm - 1, 0), a_s[0:1, :])
        bv = dh_ref[...]
        for k in _steps(tm):
            a_sh = jnp.where(row < tm - k, pltpu.roll(av, tm - k, 0), 1.0)
            b_sh = jnp.where(row < tm - k, pltpu.roll(bv, tm - k, 0), 0.0)
            bv = bv + av * b_sh
            av = av * a_sh
        g = bv + av * g_s[0:1, :]
        h_last = jnp.where(step > 0, hp_ref[HALO - 1:HALO, :], 0.0)
        h_prev = jnp.where(row >= 1, pltpu.roll(h_ref[...], 1, 0), h_last)
        du_ref[...] = g
        da_ref[...] = g * h_prev
        g_s[...] = g[0:HALO, :]
        a_s[...] = a0[0:HALO, :]

    spec = pl.BlockSpec((tm, ch), lambda i: (n - 1 - i, 0))
    hp_spec = pl.BlockSpec((HALO, ch), lambda i: (jnp.maximum((n - 1 - i) * per - 1, 0), 0))
    return pl.pallas_call(
        body, name="lru_scan_bwd", grid=(n,), in_specs=[spec, spec, hp_spec, spec], out_specs=[spec, spec],
        out_shape=[jax.ShapeDtypeStruct((rows, ch), f32)] * 2,
        scratch_shapes=[pltpu.VMEM((HALO, ch), f32)] * 2,
        compiler_params=_cp(("arbitrary",)),
    )(a, h, h, dh)


def _rms(x, g):
    return x * lax.rsqrt(jnp.mean(x * x, axis=-1, keepdims=True) + EPS) * g


def f_rms(step, p, c, x):
    return [_rms(x[0], p[0])], []


def _rope_swap(x):
    lane = lax.broadcasted_iota(jnp.int32, x.shape, 1)
    half = QK_ROPE // 2
    sw = jnp.where(lane < KR_LANE + half, pltpu.roll(x, LANE - half, 1), pltpu.roll(x, half, 1))
    return jnp.where(jnp.logical_and(lane >= KR_LANE, lane < KR_LANE + QK_ROPE), sw, 0.0)


def _rope(x, cosf, sinf):
    return x * cosf + _rope_swap(x) * sinf


def _heads(x):
    return [x[:, LANE * h:LANE * (h + 1)] for h in range(x.shape[1] // LANE)]


def q_rope_epi(q, cosf, sinf):
    return (jnp.concatenate([_rope(b, cosf, sinf) * ATT_C for b in _heads(q)], axis=1),)


def k_rope_epi(kn, kr, cosf, sinf):
    kr_rot = _rope(kr.astype(f32), cosf, sinf)
    return (jnp.concatenate([b + kr_rot for b in _heads(kn)], axis=1),)


def rope_bwd(dqr, dkr, cosf, sinf):
    back = lambda g: g * cosf + _rope_swap(g * sinf)
    dq = jnp.concatenate([back(b) * ATT_C for b in _heads(dqr)], axis=1)
    dkrope = back(sum(_heads(dkr)))
    return dq, dkr, dkrope


def _conv(tail, x, w, b):
    xf = jnp.concatenate([tail, x], axis=0)
    acc = b + w[CONV_W - 1:CONV_W, :] * xf
    for k in range(CONV_W - 1):
        acc = acc + w[k:k + 1, :] * shift_down(xf, CONV_W - 1 - k)
    return acc[HALO:, :]


def f_pool(step, p, c, x):
    wp, sc = p
    (tail,) = c
    (u,) = x
    tm = u.shape[0]
    xf = jnp.concatenate([tail, u], axis=0)
    sums, s, w = [], xf, 1
    while w < POOL_WINDOWS[-1]:
        s = s + shift_down(s, w)
        w *= 2
        sums.append(s)
    t = step * tm + lax.broadcasted_iota(jnp.int32, (tm, 1), 0)
    ys = []
    for g, (w, s) in enumerate(zip(POOL_WINDOWS, sums)):
        sl = slice(LANE * g, LANE * (g + 1))
        cnt = jnp.minimum(t + 1, w).astype(f32)
        d = s[POOL_HALO:, sl] / cnt - u[:, sl]
        ys.append(jnp.dot(d.astype(bf16), wp[LANE * g:LANE * (g + 1), :].astype(bf16), preferred_element_type=f32))
    return [jnp.concatenate(ys, axis=1) * sc], [u[tm - POOL_HALO:, :]]


def f_ssd(step, p, c, x):
    conv_w, conv_b, dtb, alog, dsk, ng = p
    tail, s_in = c[0], c[1:]
    z, xbc, dt = x
    ln = z.shape[0]
    xc = jax.nn.silu(_conv(tail, xbc, conv_w, conv_b))
    xs, bb, cc = xc[:, :MIX], xc[:, MIX:MIX + LANE], xc[:, MIX + LANE:]
    dtv = jax.nn.softplus(dt + dtb[0:1, :])
    a = dtv * -jnp.exp(alog[0:1, :])
    ri = lax.broadcasted_iota(jnp.int32, (ln, ln), 0)
    ci = lax.broadcasted_iota(jnp.int32, (ln, ln), 1)
    tril = (ri >= ci).astype(f32)
    triu = (ri <= ci).astype(f32)
    hi = lax.Precision.HIGHEST
    a_cs = jnp.dot(tril, a, precision=hi, preferred_element_type=f32)
    a_cs_t = lax.dot_general(a, triu, TN, precision=hi, preferred_element_type=f32)
    a_tot = jnp.sum(a, axis=0, keepdims=True)
    lane = lax.broadcasted_iota(jnp.int32, (1, LANE), 1)
    half = [(lane < 64).astype(f32), (lane >= 64).astype(f32)]
    hrow = lax.broadcasted_iota(jnp.int32, (LANE, 1), 0)

    def head(v, h):
        return jnp.sum(v * (lane == h).astype(f32), axis=1, keepdims=True)

    def pair(v, j):
        return head(v, 2 * j) * half[0] + head(v, 2 * j + 1) * half[1]

    cg = [(cc * half[g]).astype(bf16) for g in range(2)]
    bg = [(bb * half[g]).astype(bf16) for g in range(2)]
    cb = [lax.dot_general(cg[g], bg[g], NT, preferred_element_type=f32) for g in range(2)]
    ys, s_out = [], []
    for j in range(4):
        g = j // 2
        xs_j = xs[:, LANE * j:LANE * (j + 1)]
        xj = xs_j * pair(dtv, j)
        yj = xs_j * pair(dsk[0:1, :], j)
        for hh in range(2):
            h = 2 * j + hh
            rowv = jnp.sum(a_cs_t * (hrow == h).astype(f32), axis=0, keepdims=True)
            lmat = jnp.exp(jnp.where(ri >= ci, head(a_cs, h) - rowv, -jnp.inf))
            yj = yj + jnp.dot((cb[g] * lmat).astype(bf16), (xj * half[hh]).astype(bf16), preferred_element_type=f32)
        acs = pair(a_cs, j)
        tot = pair(a_tot, j)
        yj = yj + jnp.exp(acs) * jnp.dot(cg[g], s_in[j].astype(bf16), preferred_element_type=f32)
        s_new = jnp.exp(tot) * s_in[j] + lax.dot_general(bg[g], (xj * jnp.exp(tot - acs)).astype(bf16), TN,
                                                         preferred_element_type=f32)
        ys.append(yj)
        s_out.append(s_new)
    y = jnp.concatenate(ys, axis=1) * jax.nn.silu(z)
    return [_rms(y, ng)], [xbc[ln - HALO:, :]] + s_out


def _neg_expm1(y):
    series = -y * (1.0 + y * (0.5 + y * (1.0 / 6 + y * (1.0 / 24 + y * (1.0 / 120)))))
    return jnp.where(y > -0.05, series, 1.0 - jnp.exp(y))


def f_lru_pre(step, p, c, x):
    cw, cb_, wa, ba, wi, bi, lam = p
    (tail,) = c
    (lx,) = x
    tm = lx.shape[0]
    xc = _conv(tail, lx, cw, cb_)
    xb = xc.astype(bf16)
    r = jax.nn.sigmoid(jnp.dot(xb, wa.astype(bf16), preferred_element_type=f32) + ba)
    it = jax.nn.sigmoid(jnp.dot(xb, wi.astype(bf16), preferred_element_type=f32) + bi)
    log_a = -LRU_C * r * jax.nn.softplus(-lam)
    mult = jnp.sqrt(_neg_expm1(2.0 * log_a))
    return [jnp.exp(log_a), xc * it * mult], [lx[tm - HALO:, :]]


def f_lru_post(step, p, c, x):
    h, g = x
    return [h * jax.nn.gelu(g)], []


def loss_head(x, tgt, g, tm):
    rows, d = x.shape
    n = rows // tm

    def body(x_ref, t_ref, g_ref, loss_ref, dx_ref, dg_ref):
        @pl.when(pl.program_id(0) == 0)
        def _():
            loss_ref[...] = jnp.zeros_like(loss_ref)
            dg_ref[...] = jnp.zeros_like(dg_ref)

        def fn(gv, xv):
            err = _rms(xv, gv) - t_ref[...]
            return 0.5 * jnp.sum(jnp.mean(err * err, axis=-1, keepdims=True))

        val, (dg, dx) = jax.value_and_grad(fn, argnums=(0, 1))(g_ref[...], x_ref[...])
        loss_ref[...] += val
        dg_ref[...] += dg
        dx_ref[...] = dx

    spec = pl.BlockSpec((tm, d), lambda i: (i, 0))
    return pl.pallas_call(
        body, name="loss_head", grid=(n,), in_specs=[spec, spec, _const_spec((1, d))],
        out_specs=[_const_spec((8, LANE)), spec, _const_spec((1, d))],
        out_shape=[jax.ShapeDtypeStruct((8, LANE), f32), jax.ShapeDtypeStruct((rows, d), f32),
                   jax.ShapeDtypeStruct((1, d), f32)],
        compiler_params=_cp(("arbitrary",)),
    )(x, tgt, g)


def ew(name, fn, ins, outs, tm):
    rows = ins[0][0].shape[0]
    ni = len(ins)

    def body(*refs):
        res = fn(*[r[...].astype(f32) for r in refs[:ni]])
        for r, v in zip(refs[ni:], res):
            r[...] = v.astype(r.dtype)

    return pl.pallas_call(
        body, name=name, grid=(rows // tm,), in_specs=[_tile_spec(tm, w, cb) for (_, w, cb) in ins],
        out_specs=[_tile_spec(tm, w, 0) for (w, _) in outs],
        out_shape=[jax.ShapeDtypeStruct((rows, w), dt) for (w, dt) in outs],
        compiler_params=_cp(("parallel",)),
    )(*[t[0] for t in ins])


def _peers():
    x, y, c = lax.axis_index("x"), lax.axis_index("y"), lax.axis_index("c")
    me = 4 * x + 2 * y + c
    out = []
    for k in range(1, N_DEV):
        px = 1 - x if k & 4 else x
        py = 1 - y if k & 2 else y
        pc = 1 - c if k & 1 else c
        out.append(((px, py, pc), 4 * px + 2 * py + pc))
    return me, out


_HBM = pl.BlockSpec(memory_space=pltpu.HBM)
_SEM = pl.BlockSpec(memory_space=pltpu.SEMAPHORE)
_EFFECT = pltpu.SideEffectType.DATAFLOW_SIDE_EFFECTING


def _remote(src_ref, land_ref, gather, me, pid, dev, send_sems, recv_sems, k, recv_side):
    return pltpu.make_async_remote_copy(
        src_ref=src_ref if gather else src_ref.at[pid], dst_ref=land_ref.at[pid if recv_side else me],
        send_sem=send_sems.at[k], recv_sem=recv_sems.at[k], device_id=dev, device_id_type=pl.DeviceIdType.MESH)


def _own(src_ref, land_ref, gather, me, sem):
    return pltpu.make_async_copy(src_ref if gather else src_ref.at[me], land_ref.at[me], sem)


def exchange_start(name, srcs, gather, deps=()):
    n, nd = len(srcs), len(deps)
    shapes = [(s.shape if gather else s.shape[1:]) for s in srcs]
    lands = [lax.empty((N_DEV,) + tuple(sh), s.dtype) for s, sh in zip(srcs, shapes)]

    def body(*refs):
        src_refs, land_refs = refs[:n], refs[n:2 * n]
        send_sems, recv_sems, own_sem = refs[2 * n + nd:2 * n + nd + 3]
        token = refs[-1]
        me, peers = _peers()
        for k, (dev, pid) in enumerate(peers):
            for s_ref, l_ref in zip(src_refs, land_refs):
                _remote(s_ref, l_ref, gather, me, pid, dev, send_sems, recv_sems, k, False).start()
        for s_ref, l_ref in zip(src_refs, land_refs):
            _own(s_ref, l_ref, gather, me, own_sem).start()
        token[...] = jnp.zeros_like(token)

    hbm = lambda a: pltpu.with_memory_space_constraint(a, pltpu.HBM)
    res = pl.pallas_call(
        body, name=name,
        out_shape=(pltpu.SemaphoreType.DMA((N_DEV - 1,)), pltpu.SemaphoreType.DMA((N_DEV - 1,)), pltpu.SemaphoreType.DMA(()),
                   *[pltpu.HBM(a.shape, a.dtype) for a in list(srcs) + lands], jax.ShapeDtypeStruct((8, LANE), f32)),
        in_specs=[_HBM] * (2 * n) + [pl.BlockSpec(memory_space=pl.ANY)] * nd,
        out_specs=(_SEM, _SEM, _SEM, *([_HBM] * (2 * n)), pl.BlockSpec(memory_space=pltpu.VMEM)),
        input_output_aliases={i: 3 + i for i in range(2 * n)},
        compiler_params=pltpu.CompilerParams(has_side_effects=_EFFECT),
    )(*[hbm(a) for a in list(srcs) + lands], *deps)
    return dict(sems=res[:3], srcs=list(res[3:3 + n]), lands=list(res[3 + n:3 + 2 * n]), token=res[-1], gather=gather)


def exchange_wait(name, h, afters):
    n, gather = len(h["srcs"]), h["gather"]

    def body(*refs):
        src_refs, land_refs = refs[:n], refs[n:2 * n]
        send_sems, recv_sems, own_sem = refs[2 * n:2 * n + 3]
        me, peers = _peers()
        for k, (dev, pid) in enumerate(peers):
            for s_ref, l_ref in zip(src_refs, land_refs):
                _remote(s_ref, l_ref, gather, me, pid, dev, send_sems, recv_sems, k, True).wait_recv()
        for k, (dev, pid) in enumerate(peers):
            for s_ref, l_ref in zip(src_refs, land_refs):
                _remote(s_ref, l_ref, gather, me, pid, dev, send_sems, recv_sems, k, False).wait_send()
        for s_ref, l_ref in zip(src_refs, land_refs):
            _own(s_ref, l_ref, gather, me, own_sem).wait()

    arrs = h["srcs"] + h["lands"]
    res = pl.pallas_call(
        body, name=name, out_shape=tuple(pltpu.HBM(a.shape, a.dtype) for a in arrs),
        in_specs=[_HBM] * (2 * n) + [_SEM, _SEM, _SEM] + [pl.BlockSpec(memory_space=pl.ANY)] * len(afters),
        out_specs=tuple([_HBM] * (2 * n)), input_output_aliases={i: i for i in range(2 * n)},
        compiler_params=pltpu.CompilerParams(has_side_effects=_EFFECT),
    )(*arrs, *h["sems"], *afters)
    return list(res[n:])


def _adam_update(g, w, m, v):
    mn = ADAM_B1 * m + (1.0 - ADAM_B1) * g
    vn = ADAM_B2 * v + (1.0 - ADAM_B2) * jnp.square(g)
    m_hat = mn / (1.0 - ADAM_B1 ** ADAM_STEP)
    v_hat = vn / (1.0 - ADAM_B2 ** ADAM_STEP)
    return -ADAM_LR * (m_hat / (jnp.sqrt(v_hat) + ADAM_EPS) + ADAM_WD * w), mn, vn


def _adamw_vectors(name, parts, w, m, v):
    nl = len(parts)

    def body(*refs):
        p_refs = refs[:nl]
        w_ref, m_ref, v_ref, g_ref, d_ref, nm_ref, nv_ref = refs[nl:]
        for ll, p_ref in enumerate(p_refs):
            row = slice(ll, ll + 1)
            g = p_ref[0:1, :]
            for i in range(1, N_DEV):
                g = g + p_ref[i:i + 1, :]
            delta, mn, vn = _adam_update(g, w_ref[row, :], m_ref[row, :], v_ref[row, :])
            g_ref[row, :] = g
            d_ref[row, :] = delta
            nm_ref[row, :] = mn
            nv_ref[row, :] = vn

    return list(pl.pallas_call(body, name=name, out_shape=[jax.ShapeDtypeStruct(w.shape, f32)] * 4)(*parts, w, m, v))


def adamw_packed(name, gots, where, ws, ms, vs):
    ng, npar = len(gots), len(ws)

    def body(*refs):
        g_refs = refs[:ng]
        w_refs, m_refs, v_refs = (refs[ng + k * npar:ng + (k + 1) * npar] for k in range(3))
        o_refs = refs[ng + 3 * npar:]
        for p in range(npar):
            width = w_refs[p].shape[1]
            for l, (which, off) in enumerate(where[p]):
                row = slice(l, l + 1)
                cols = slice(off, off + width)
                g = g_refs[which][0:1, cols]
                for i in range(1, N_DEV):
                    g = g + g_refs[which][i:i + 1, cols]
                delta, mn, vn = _adam_update(g, w_refs[p][row, :], m_refs[p][row, :], v_refs[p][row, :])
                for k, val in enumerate((g, delta, mn, vn)):
                    o_refs[4 * p + k][row, :] = val

    out_shape = [jax.ShapeDtypeStruct(w.shape, f32) for w in ws for _ in range(4)]
    res = pl.pallas_call(body, name=name, out_shape=out_shape, compiler_params=_cp(()))(*gots, *ws, *ms, *vs)
    return [list(res[4 * p:4 * p + 4]) for p in range(npar)]


def adamw_columns(name, parts, w, m, v):
    nl, kk, cc = w.shape
    view = lambda a: jnp.transpose(a, (2, 0, 1))
    tc = min(LANE, cc)

    def body(*refs):
        p_refs = refs[:nl]
        w_ref, m_ref, v_ref, g_ref, d_ref, nm_ref, nv_ref = refs[nl:]
        for l, p_ref in enumerate(p_refs):
            g = p_ref[0].astype(f32)
            for i in range(1, N_DEV):
                g = g + p_ref[i].astype(f32)
            g = g.T
            delta, mn, vn = _adam_update(g, w_ref[:, l, :], m_ref[:, l, :], v_ref[:, l, :])
            g_ref[:, l, :] = g
            d_ref[:, l, :] = delta
            nm_ref[:, l, :] = mn
            nv_ref[:, l, :] = vn

    p_spec = pl.BlockSpec((N_DEV, kk, tc), lambda j: (0, 0, j))
    w_spec = pl.BlockSpec((tc, nl, kk), lambda j: (j, 0, 0))
    res = pl.pallas_call(
        body, name=name, grid=(pl.cdiv(cc, tc),), in_specs=[p_spec] * nl + [w_spec] * 3, out_specs=[w_spec] * 4,
        out_shape=[jax.ShapeDtypeStruct((cc, nl, kk), f32)] * 4, compiler_params=_cp(("parallel",)),
    )(*parts, view(w), view(m), view(v))
    return [jnp.transpose(a, (1, 2, 0)) for a in res]


def adamw(name, parts, w, m, v):
    nl = len(parts)
    shape = w.shape[1:]
    c = shape[-1]
    r = 1
    for s in shape[:-1]:
        r *= s
    if r == 1:
        return _adamw_vectors(name, parts, w, m, v)
    tr = _pick(r, 256) if r % 8 == 0 else r
    nb = r // tr
    parts2 = [p.reshape(N_DEV, r, c) for p in parts]
    w2, m2, v2 = (a.reshape(nl, r, c) for a in (w, m, v))

    def body(*refs):
        p_refs = refs[:nl]
        w_ref, m_ref, v_ref, g_ref, d_ref, nm_ref, nv_ref = refs[nl:]
        layer = pl.program_id(0)
        for ll, p_ref in enumerate(p_refs):
            @pl.when(layer == ll)
            def _(p_ref=p_ref):
                g = p_ref[0].astype(f32)
                for i in range(1, N_DEV):
                    g = g + p_ref[i].astype(f32)
                delta, mn, vn = _adam_update(g, w_ref[0], m_ref[0], v_ref[0])
                g_ref[0] = g
                d_ref[0] = delta
                nm_ref[0] = mn
                nv_ref[0] = vn

    def p_spec(ll):
        return pl.BlockSpec((N_DEV, tr, c), lambda l, i: (0, jnp.where(l == ll, i, jnp.where(l > ll, nb - 1, 0)), 0))

    spec = pl.BlockSpec((1, tr, c), lambda l, i: (l, i, 0))
    res = pl.pallas_call(
        body, name=name, grid=(nl, nb), in_specs=[p_spec(ll) for ll in range(nl)] + [spec, spec, spec],
        out_specs=[spec] * 4, out_shape=[jax.ShapeDtypeStruct((nl, r, c), f32)] * 4,
        compiler_params=_cp(("arbitrary", "arbitrary")),
    )(*parts2, w2, m2, v2)
    return [a.reshape(w.shape) for a in res]


_IN_SPLITS = dict(cq=(0, 384), ckv=(384, 640), kr=(640, 672), pool=(672, 1184), z=(1184, 1696), xbc=(1696, 2464),
                  dt=(2464, 2472), lg=(2472, 2984), lx=(2984, 3496), gates=(3496, 7592))


W_IN_SHARD = IN_COLS // N_DEV

_PAD_ORDER = ("gates", "pool", "z", "lg", "lx", "xbc", "cq", KR_LANE, "kr", LANE - KR_LANE - QK_ROPE, "ckv", "dt",
              LANE - 8, U_COLS - U_DT[0] - LANE)
_SEGMENTS = ((0, U_CQ[0], 384), (384, U_CKV[0], 256), (640, U_KR[0] + KR_LANE, QK_ROPE), (672, U_POOL[0], 512),
             (1184, U_Z[0], 512), (1696, U_XBC[0], 768), (2464, U_DT[0], 8), (2472, U_LG[0], 512), (2984, U_LX[0], 512),
             (3496, 0, 4096))


def _pad_w_in(shards):
    rows = shards.shape[1]
    pieces = []
    for item in _PAD_ORDER:
        if isinstance(item, int):
            pieces.append(jnp.zeros((rows, item), shards.dtype))
            continue
        a, b = _IN_SPLITS[item]
        for d in range(a // W_IN_SHARD, (b - 1) // W_IN_SHARD + 1):
            lo, hi = max(a, d * W_IN_SHARD), min(b, (d + 1) * W_IN_SHARD)
            pieces.append(shards[d, :, lo - d * W_IN_SHARD:hi - d * W_IN_SHARD])
    return jnp.concatenate(pieces, axis=1)


def _w_in_blocks(g):
    blocks = []
    for d in range(N_DEV):
        a, b = d * W_IN_SHARD, (d + 1) * W_IN_SHARD
        pieces = []
        for ref, pad, width in _SEGMENTS:
            lo, hi = max(a, ref), min(b, ref + width)
            if lo < hi:
                pieces.append(g[:, pad + lo - ref:pad + hi - ref])
        blocks.append(jnp.concatenate(pieces, axis=1))
    return jnp.stack(blocks).astype(bf16)


def _head_pad_cols(w, per, lo, hi):
    k = w.shape[0]
    w = w.reshape(k, N_HEADS, per)[:, :, lo:hi]
    return jnp.pad(w, ((0, 0), (0, 0), (0, LANE - (hi - lo)))).reshape(k, N_HEADS * LANE)


def _head_unpad_cols(g, n):
    k = g.shape[0]
    return g.reshape(k, N_HEADS, LANE)[:, :, :n]


def _on_diagonal():
    i = lax.broadcasted_iota(jnp.int32, (8, 1, 8, 1), 0)
    j = lax.broadcasted_iota(jnp.int32, (8, 1, 8, 1), 2)
    return i == j


def _block_diag(w):
    w4 = jnp.broadcast_to(w[:, :, None, :], (8, 64, 8, 64))
    return jnp.where(_on_diagonal(), w4, 0.0).reshape(MIX, MIX)


def _block_diag_inv(g):
    return jnp.sum(jnp.where(_on_diagonal(), g.reshape(8, 64, 8, 64), 0.0), axis=2)


def _head8(v):
    return jnp.pad(v[None, :], ((0, 7), (0, LANE - v.shape[0])))


GROUPS = dict(A=("w_in",), B=("w_uq", "w_ukv", "ssd_conv_w", "lru_conv_w", "w_branch", "w_out"),
              C=("w_ff1", "w_ff2", "w_ple_gate", "w_ple"))


def _kernel_weights(grp, fw):
    if grp == "A":
        w_in = _pad_w_in(fw["w_in"])
        return dict(w_in=w_in, w_dt=w_in[:, U_DT[0]:U_DT[0] + LANE])
    if grp == "C":
        return dict(w_ff1=fw["w_ff1"], w_ff2=fw["w_ff2"], w_pg=fw["w_ple_gate"], w_ple=fw["w_ple"])
    wb = fw["w_branch"]
    wb0 = jnp.pad(wb[0].reshape(N_HEADS, V_HEAD, D_MODEL), ((0, 0), (0, LANE - V_HEAD), (0, 0))).reshape(N_HEADS * LANE, D_MODEL)
    return dict(
        w_uq=_head_pad_cols(fw["w_uq"], QK_NOPE + QK_ROPE, 0, QK_NOPE + QK_ROPE),
        w_uk=_head_pad_cols(fw["w_ukv"], QK_NOPE + V_HEAD, 0, QK_NOPE),
        w_uv=_head_pad_cols(fw["w_ukv"], QK_NOPE + V_HEAD, QK_NOPE, QK_NOPE + V_HEAD),
        wb=[wb0, wb[1], wb[2], wb[3]], w_out=fw["w_out"], ssd_conv_w=fw["ssd_conv_w"], lru_conv_w=fw["lru_conv_w"])


def _layer_params(sp, l):
    row = lambda n: sp[n][l][None, :]
    return dict(
        g_mix=row("g_mix"), q_norm=row("q_norm"), kv_norm=row("kv_norm"),
        pool=[sp["w_pool"][l].reshape(4 * LANE, LANE), row("pool_scale")],
        ssd=[None, row("ssd_conv_b"), _head8(sp["ssd_dt_bias"][l]), _head8(sp["ssd_a_log"][l]),
             _head8(sp["ssd_d"][l]), row("ssd_norm")],
        lru=[None, row("lru_conv_b"), _block_diag(sp["lru_w_a"][l]), row("lru_b_a"),
             _block_diag(sp["lru_w_i"][l]), row("lru_b_i"), row("lru_lambda")],
        g_mlp=row("g_mlp"), g_ple=row("g_ple"),
    )


_sig = jax.nn.sigmoid
_SSD_CARRY = [(HALO, SSD_XBC)] + [(LANE, LANE)] * 4


def _tiles(rows):
    return dict(tm=_pick(rows, 512), ta=_pick(rows, 512), tp=_pick(rows, 512), tl=_pick(rows, 512), ts=_pick(rows, 256))


def _mixer_tiles(u, dt32):
    return dict(
        cq=(u, 384, U_CQ[0] // 384), ckv=(u, 256, U_CKV[0] // 256), kr=(u, LANE, U_KR[0] // LANE),
        pool=(u, MIX, U_POOL[0] // MIX), z=(u, MIX, U_Z[0] // MIX), xbc=(u, SSD_XBC, U_XBC[0] // SSD_XBC),
        dt=(dt32, LANE, 0), lg=(u, MIX, U_LG[0] // MIX), lx=(u, MIX, U_LX[0] // MIX))


def _add_norm(acc, resid, g):
    x = acc + resid
    return x, _rms(x, g)


def _layer_fwd(x, h, p_bf, ctx, l, pr, g_next, cosf, sinf):
    rows = x.shape[0]
    ts = _tiles(rows)
    tm = ts["tm"]
    nm = lambda s: f"{s}_l{l}"
    r = dict(x=x)
    if h is None:
        (h,), _ = seq_fwd(nm("rms_in"), f_rms, [pr["g_mix"]], [(x, D_MODEL, 0)], [], [(D_MODEL, bf16)], tm)
    early = [h] + ([cosf, sinf, p_bf] + [a for v in pr.values() for a in (v if isinstance(v, list) else [v]) if a is not None]
                   if l == 0 else [])
    w = dict(_kernel_weights("A", ctx.weights(l, "A", early)))
    u = matmul(nm("w_in"), h, w["w_in"], outs=(U_DTYPE,))
    dt32 = matmul(nm("w_dt"), h, w["w_dt"])
    mt = _mixer_tiles(u, dt32)
    (cqn,), _ = seq_fwd(nm("rms_q"), f_rms, [pr["q_norm"]], [mt["cq"]], [], [(Q_LORA, bf16)], tm)
    (ckvn,), _ = seq_fwd(nm("rms_kv"), f_rms, [pr["kv_norm"]], [mt["ckv"]], [], [(KV_LORA, bf16)], tm)
    (yb,), pool_saved = seq_fwd(nm("pool"), f_pool, pr["pool"], [mt["pool"]], [(POOL_HALO, MIX)], [(MIX, bf16)], ts["tp"])
    w.update(_kernel_weights("B", ctx.weights(l, "B", yb)))
    pr = dict(pr, ssd=[w["ssd_conv_w"]] + pr["ssd"][1:], lru=[w["lru_conv_w"]] + pr["lru"][1:])
    tables = [(cosf, 0, LANE), (sinf, 0, LANE)]
    qr = matmul(nm("w_uq"), cqn, w["w_uq"], outs=(bf16,), epi=q_rope_epi, extras=tables)
    kr = matmul(nm("w_uk"), ckvn, w["w_uk"], outs=(bf16,), epi=k_rope_epi, extras=[(u, U_KR[0], LANE)] + tables)
    vb = matmul(nm("w_uv"), ckvn, w["w_uv"], outs=(bf16,))
    o, lse = attn_fwd(qr, kr, vb, ts["ta"])
    (yc,), ssd_saved = seq_fwd(nm("ssd"), f_ssd, pr["ssd"], [mt["z"], mt["xbc"], mt["dt"]], _SSD_CARRY, [(MIX, bf16)], SSD_CHUNK)
    (la, lu), lru_saved = seq_fwd(nm("lru_pre"), f_lru_pre, pr["lru"], [mt["lx"]], [(HALO, MIX)], [(MIX, f32), (MIX, f32)], ts["tl"])
    hh = scan_fwd(la, lu, ts["ts"])
    (yd,), _ = seq_fwd(nm("lru_post"), f_lru_post, [], [(hh, MIX, 0), mt["lg"]], [], [(MIX, bf16)], tm)
    ys = [o, yb, yc, yd]
    m, pres = merge_fwd(nm("merge"), ys, w["wb"], u)
    x1, h2 = matmul(nm("w_out"), m, w["w_out"], outs=(f32, bf16), epi=_add_norm, extras=[(x, 0)], rows=[pr["g_mlp"]])
    w.update(_kernel_weights("C", ctx.weights(l, "C", h2)))
    a1, act = matmul(nm("ff1"), h2, w["w_ff1"], outs=(bf16, bf16), epi=lambda acc: (acc, jnp.square(jnp.maximum(acc, 0.0))))
    x2, h3 = matmul(nm("ff2"), act, w["w_ff2"], outs=(f32, bf16), epi=_add_norm, extras=[(x1, 0)], rows=[pr["g_ple"]])
    gl = matmul(nm("ple_gate"), h3, w["w_pg"])
    if g_next is None:
        x3, pe = matmul(nm("ple"), p_bf, w["w_ple"], outs=(f32, f32), epi=lambda acc, g, xr: (xr + acc * _sig(g), acc),
                        extras=[(gl, 0), (x2, 0)])
        h_next = None
    else:
        def ple_norm(acc, g, xr, gn):
            xo = xr + acc * _sig(g)
            return xo, acc, _rms(xo, gn)

        x3, pe, h_next = matmul(nm("ple"), p_bf, w["w_ple"], outs=(f32, f32, bf16), epi=ple_norm,
                                extras=[(gl, 0), (x2, 0)], rows=[g_next])
    r.update(h=h, u=u, cqn=cqn, ckvn=ckvn, vb=vb, qr=qr, kr=kr, o=o, lse=lse, ys=ys, pres=pres, m=m, x1=x1,
             h2=h2, a1=a1, act=act, x2=x2, h3=h3, gl=gl, pe=pe, p_bf=p_bf, pool_saved=pool_saved, ssd_saved=ssd_saved,
             lru_saved=lru_saved, la=la, hh=hh, w=w, pr=pr, dt32=dt32)
    return x3, h_next, r


def _norm_bwd(dh, x, resid, g):
    rs = lax.rsqrt(jnp.mean(x * x, axis=-1, keepdims=True) + EPS)
    xhat = x * rs
    dxn = dh * g
    dx = rs * (dxn - xhat * jnp.mean(dxn * xhat, axis=-1, keepdims=True)) + resid
    return dx, jnp.sum(dh * xhat, axis=0, keepdims=True)


def _gate_bwd(d, g, pre):
    s = _sig(g.astype(f32))
    return d * s, d * pre.astype(f32) * s * (1.0 - s)


def _layer_bwd(dx3, r, ctx, l, cosf, sinf, tok, extra_small):
    rows = dx3.shape[0]
    ts = _tiles(rows)
    tm = ts["tm"]
    nm = lambda s: f"{s}_l{l}"
    u, w, pr = r["u"], r["w"], r["pr"]
    mt = _mixer_tiles(u, r["dt32"])
    g = {}
    full = lambda a: (a, a.shape[1], 0)
    dpe, dgl = ew(nm("ple_bwd"), _gate_bwd, [full(dx3), full(r["gl"]), full(r["pe"])], [(D_MODEL, bf16)] * 2, tm)
    g["w_ple"] = matmul(nm("d_w_ple"), r["p_bf"], dpe, ta=True, outs=(bf16,), deps=[tok] if tok is not None else [])
    g["w_pg"] = matmul(nm("d_w_pg"), r["h3"], dgl, ta=True, outs=(bf16,))
    dx2, g["g_ple"] = matmul(nm("d_h3"), dgl, w["w_pg"], tb=True, epi=_norm_bwd, extras=[(r["x2"], 0), (dx3, 0)],
                             rows=[pr["g_ple"]], row_sums=1)
    da1 = matmul(nm("d_act"), dx2, w["w_ff2"], tb=True, outs=(bf16,),
                 epi=lambda acc, a: (acc * 2.0 * jnp.maximum(a, 0.0),), extras=[(r["a1"], 0)])
    g["w_ff2"] = matmul(nm("d_w_ff2"), r["act"], dx2, ta=True, outs=(bf16,))
    g["w_ff1"] = matmul(nm("d_w_ff1"), r["h2"], da1, ta=True, outs=(bf16,), out_blocks=N_DEV)
    tok = ctx.grads(l, "C", dict(w_ff1=g["w_ff1"], w_ff2=g["w_ff2"], w_ple_gate=g["w_pg"], w_ple=g["w_ple"]))
    dx1, g["g_mlp"] = matmul(nm("d_h2"), da1, w["w_ff1"], tb=True, epi=_norm_bwd, extras=[(r["x1"], 0), (dx2, 0)],
                             rows=[pr["g_mlp"]], row_sums=1, deps=[tok])
    def merge_bwd(dm, *gates_and_pres):
        both = [_gate_bwd(dm, gates_and_pres[n], gates_and_pres[4 + n]) for n in range(4)]
        return tuple(b[0] for b in both) + tuple(b[1] for b in both)

    res = matmul(nm("d_merged"), dx1, w["w_out"], tb=True, outs=(bf16,) * 8, epi=merge_bwd,
                 extras=[(u, D_MODEL * n) for n in range(4)] + [(pre, 0) for pre in r["pres"]])
    dpres, dgates = list(res[:4]), list(res[4:])
    g["w_out"] = matmul(nm("d_w_out"), r["m"], dx1, ta=True, outs=(bf16,))
    dys, g["wb"] = [], []
    for n in range(4):
        g["wb"].append(matmul(nm(f"d_w_branch{n}"), r["ys"][n], dpres[n], ta=True, outs=(bf16,)))
        dys.append(matmul(nm(f"d_y{n}"), dpres[n], w["wb"][n], tb=True, outs=(bf16 if n == 0 else f32,)))
    dqr, dkr_, dv = attn_bwd(r["qr"], r["kr"], r["vb"], dys[0], r["o"], r["lse"], ts["ta"])
    hw = N_HEADS * LANE
    dq, dkn, dkrope = ew(nm("rope_bwd"), rope_bwd, [full(dqr), full(dkr_), full(cosf), full(sinf)],
                         [(hw, bf16), (hw, bf16), (LANE, bf16)], tm)
    g["w_uq"] = matmul(nm("d_w_uq"), r["cqn"], dq, ta=True, outs=(bf16,))
    g["w_uk"] = matmul(nm("d_w_uk"), r["ckvn"], dkn, ta=True, outs=(bf16,))
    g["w_uv"] = matmul(nm("d_w_uv"), r["ckvn"], dv, ta=True, outs=(bf16,))
    dcqn = matmul(nm("d_cqn"), dq, w["w_uq"], tb=True)
    dckvn = matmul(nm("d_ckvn_k"), dkn, w["w_uk"], tb=True)
    dckvn = matmul(nm("d_ckvn_v"), dv, w["w_uv"], tb=True, epi=lambda acc, prev: (acc + prev,), extras=[(dckvn, 0)])
    (g["q_norm"],), (dcq,) = seq_bwd(nm("rms_q_bwd"), f_rms, [pr["q_norm"]], [mt["cq"]], [True], [], [dcqn], [bf16], tm)
    (g["kv_norm"],), (dckv,) = seq_bwd(nm("rms_kv_bwd"), f_rms, [pr["kv_norm"]], [mt["ckv"]], [True], [], [dckvn], [bf16], tm)
    g["pool"], (dpool,) = seq_bwd(nm("pool_bwd"), f_pool, pr["pool"], [mt["pool"]], [True], r["pool_saved"], [dys[1]],
                                  [bf16], ts["tp"])
    g["ssd"], (dz, dxbc, ddt) = seq_bwd(nm("ssd_bwd"), f_ssd, pr["ssd"], [mt["z"], mt["xbc"], mt["dt"]], [True] * 3,
                                        r["ssd_saved"], [dys[2]], [bf16] * 3, SSD_CHUNK)
    _, (dhh, dlg) = seq_bwd(nm("lru_post_bwd"), f_lru_post, [], [full(r["hh"]), mt["lg"]], [True, True], [], [dys[3]],
                            [f32, bf16], tm)
    da, du = scan_bwd(r["la"], r["hh"], dhh, ts["ts"])
    g["lru"], (dlx,) = seq_bwd(nm("lru_pre_bwd"), f_lru_pre, pr["lru"], [mt["lx"]], [True], r["lru_saved"], [da, du],
                               [bf16], ts["tl"])
    dk = _head_unpad_cols(g["w_uk"], QK_NOPE)
    dv_ = _head_unpad_cols(g["w_uv"], V_HEAD)
    wb0 = g["wb"][0].reshape(N_HEADS, LANE, D_MODEL)[:, :V_HEAD].reshape(MIX, D_MODEL)
    ssd, lru, pool = g["ssd"], g["lru"], g["pool"]
    tok = ctx.grads(l, "B", dict(
        w_uq=_head_unpad_cols(g["w_uq"], QK_NOPE + QK_ROPE).reshape(Q_LORA, -1),
        w_ukv=jnp.concatenate([dk, dv_], axis=2).reshape(KV_LORA, -1), ssd_conv_w=ssd[0], lru_conv_w=lru[0],
        w_branch=jnp.stack([wb0, g["wb"][1], g["wb"][2], g["wb"][3]]), w_out=g["w_out"]))
    du_p = jnp.concatenate(dgates + [dpool, dz, dlg, dlx, dxbc, dcq, dkrope, dckv, ddt,
                                     jnp.zeros((rows, U_COLS - U_DT[0] - LANE), bf16)], axis=1)
    small = dict(
        q_norm=g["q_norm"][0], kv_norm=g["kv_norm"][0],
        w_pool=pool[0].reshape(4, LANE, LANE), pool_scale=pool[1][0],
        ssd_conv_b=ssd[1][0], ssd_dt_bias=ssd[2][0, :8], ssd_a_log=ssd[3][0, :8], ssd_d=ssd[4][0, :8], ssd_norm=ssd[5][0],
        lru_conv_b=lru[1][0], lru_w_a=_block_diag_inv(lru[2]), lru_b_a=lru[3][0], lru_w_i=_block_diag_inv(lru[4]),
        lru_b_i=lru[5][0], lru_lambda=lru[6][0], g_mlp=g["g_mlp"][0], g_ple=g["g_ple"][0])
    tok_small = ctx.small(f"l{l}", [(n, l, small[n]) for n in SMALL if n in small] + extra_small)
    g_w_in = matmul(nm("d_w_in"), r["h"], du_p, ta=True, outs=(bf16,), deps=[tok, tok_small])
    tok = ctx.grads(l, "A", dict(w_in=_w_in_blocks(g_w_in)))
    dx, g_mix = matmul(nm("d_h"), du_p, w["w_in"], tb=True, epi=_norm_bwd, extras=[(r["x"], 0), (dx1, 0)],
                       rows=[pr["g_mix"]], row_sums=1, deps=[tok])
    return dx, tok, ("g_mix", l, g_mix[0])


def _rope_tables(positions):
    inv = 1.0 / (ROPE_THETA ** (jnp.arange(0, QK_ROPE, 2, dtype=f32) / QK_ROPE))
    ang = positions.astype(f32)[:, None] * inv
    cos, sin = jnp.cos(ang), jnp.sin(ang)
    rows = positions.shape[0]
    pad = jnp.zeros((rows, LANE - KR_LANE - QK_ROPE), f32)
    cosf = jnp.concatenate([jnp.ones((rows, KR_LANE), f32), cos, cos, pad], axis=1)
    sinf = jnp.concatenate([jnp.zeros((rows, KR_LANE), f32), -sin, sin, pad], axis=1)
    return cosf, sinf


WEIGHTS = ['g_mix', 'w_in', 'q_norm', 'w_uq', 'kv_norm', 'w_ukv', 'w_pool', 'pool_scale', 'ssd_conv_w', 'ssd_conv_b',
           'ssd_dt_bias', 'ssd_a_log', 'ssd_d', 'ssd_norm', 'lru_conv_w', 'lru_conv_b', 'lru_w_a', 'lru_b_a', 'lru_w_i',
           'lru_b_i', 'lru_lambda', 'w_branch', 'w_out', 'g_mlp', 'w_ff1', 'w_ff2', 'g_ple', 'w_ple_gate', 'w_ple', 'g_final']
SHARDED = dict(w_in=2, w_uq=2, w_ukv=2, ssd_conv_w=2, lru_conv_w=2, w_branch=3, w_out=1, w_ff1=2, w_ff2=1,
               w_ple_gate=1, w_ple=2)
F32_PAYLOAD = ("ssd_conv_w", "lru_conv_w")
DEPTH = 2


SMALL = [n for n in WEIGHTS if n not in SHARDED and n != "g_final"]


def local_step(x, p, positions, tgt, sp, ctx):
    cosf, sinf = _rope_tables(positions)
    res, h = [], None
    for l in range(DEPTH):
        g_next = sp["g_mix"][l + 1][None, :] if l + 1 < DEPTH else None
        x, h, r = _layer_fwd(x, h, p[l].astype(bf16), ctx, l, _layer_params(sp, l), g_next, cosf, sinf)
        res.append(r)
    loss8, dx, dgf = loss_head(x, tgt, sp["g_final"][None, :], _pick(x.shape[0], 512))
    tok = None
    pending = ("g_final", None, dgf[0])
    for l in reversed(range(DEPTH)):
        dx, tok, pending = _layer_bwd(dx, res[l], ctx, l, cosf, sinf, tok, [pending])
    ctx.small("last", [pending])
    return loss8[0, 0], dx


def _payload(name, w):
    return w if name in F32_PAYLOAD else w.astype(bf16)


def _blocks(name, g):
    ax = SHARDED[name] - 1
    shape = list(g.shape)
    shape[ax:ax + 1] = [N_DEV, shape[ax] // N_DEV]
    return _payload(name, jnp.moveaxis(g.reshape(shape), ax, 0))


def _assemble(name, shards):
    ax = SHARDED[name] - 1
    shape = list(shards.shape[1:])
    shape[ax] *= N_DEV
    return jnp.moveaxis(shards, 0, ax).reshape(shape)


class _Exchanges:
    def __init__(self, wts):
        self.wts = wts
        self.ag, self.rs, self.sm = {}, {}, {}
        tok = None
        for l in range(DEPTH):
            for grp, names in GROUPS.items():
                h = exchange_start(f"ag_start_{grp}{l}", [_payload(n, wts[n][l]) for n in names], True,
                                   deps=[] if tok is None else [tok])
                tok = h["token"]
                self.ag[(l, grp)] = h
        self.all_started = tok

    def weights(self, l, grp, after):
        afters = list(after) if isinstance(after, (list, tuple)) else [after]
        if (l, grp) == (0, "A"):
            afters.append(self.all_started)
        got = exchange_wait(f"ag_wait_{grp}{l}", self.ag[(l, grp)], afters)
        out = {}
        for n, a in zip(GROUPS[grp], got):
            out[n] = a if n == "w_in" else _assemble(n, a)
        return out

    def grads(self, l, grp, g):
        cut = lambda n: g[n].ndim == self.wts[n].ndim
        h = exchange_start(f"rs_start_{grp}{l}", [g[n] if cut(n) else _blocks(n, g[n]) for n in GROUPS[grp]], False)
        self.rs[(l, grp)] = h
        return h["token"]

    def small(self, tag, entries):
        entries = sorted(entries, key=lambda e: e[2].size % LANE != 0)
        flat = jnp.concatenate([a.reshape(-1) for _, _, a in entries])
        flat = jnp.pad(flat, (0, (-flat.shape[0]) % (8 * LANE))).reshape(-1, LANE)
        h = exchange_start(f"small_start_{tag}", [flat], True)
        self.sm[tag] = (h, [(n, l, a.shape) for n, l, a in entries])
        return h["token"]

    def collect(self, groups, after):
        parts = {}
        for grp in groups:
            for l in reversed(range(DEPTH)):
                got = exchange_wait(f"rs_wait_{grp}{l}", self.rs[(l, grp)], [after])
                for n, a in zip(GROUPS[grp], got):
                    parts.setdefault(n, [None] * DEPTH)[l] = a
        return parts

    def collect_small(self, after):
        gots, where, parts = [], {}, {}
        for tag, (h, layout) in self.sm.items():
            (got,) = exchange_wait(f"small_wait_{tag}", h, [after])
            got = got.reshape(N_DEV, -1)
            off = 0
            for n, l, shape in layout:
                size = 1
                for d in shape:
                    size *= d
                if len(shape) == 1 and size % LANE == 0 and off % LANE == 0:
                    where.setdefault(n, [None] * (1 if l is None else DEPTH))[l or 0] = (len(gots), off)
                else:
                    part = got[:, off:off + size].reshape((N_DEV,) + tuple(shape))
                    if l is None:
                        parts[n] = [part]
                    else:
                        parts.setdefault(n, [None] * DEPTH)[l] = part
                off += size
            gots.append(got)
        return gots, where, parts


def kernel(x, p, positions, g_mix, w_in, q_norm, w_uq, kv_norm, w_ukv, w_pool, pool_scale, ssd_conv_w, ssd_conv_b,
           ssd_dt_bias, ssd_a_log, ssd_d, ssd_norm, lru_conv_w, lru_conv_b, lru_w_a, lru_b_a, lru_w_i, lru_b_i,
           lru_lambda, w_branch, w_out, g_mlp, w_ff1, w_ff2, g_ple, w_ple_gate, w_ple, g_final, loss_target, m_g_mix,
           m_w_in, m_q_norm, m_w_uq, m_kv_norm, m_w_ukv, m_w_pool, m_pool_scale, m_ssd_conv_w, m_ssd_conv_b,
           m_ssd_dt_bias, m_ssd_a_log, m_ssd_d, m_ssd_norm, m_lru_conv_w, m_lru_conv_b, m_lru_w_a, m_lru_b_a,
           m_lru_w_i, m_lru_b_i, m_lru_lambda, m_w_branch, m_w_out, m_g_mlp, m_w_ff1, m_w_ff2, m_g_ple, m_w_ple_gate,
           m_w_ple, m_g_final, v_g_mix, v_w_in, v_q_norm, v_w_uq, v_kv_norm, v_w_ukv, v_w_pool, v_pool_scale,
           v_ssd_conv_w, v_ssd_conv_b, v_ssd_dt_bias, v_ssd_a_log, v_ssd_d, v_ssd_norm, v_lru_conv_w, v_lru_conv_b,
           v_lru_w_a, v_lru_b_a, v_lru_w_i, v_lru_b_i, v_lru_lambda, v_w_branch, v_w_out, v_g_mlp, v_w_ff1, v_w_ff2,
           v_g_ple, v_w_ple_gate, v_w_ple, v_g_final):
    given = dict(locals())
    wts = {n: given[n] for n in WEIGHTS}
    ctx = _Exchanges(wts)
    loss, grad_x = local_step(x[0], p[:, 0], positions[0], loss_target[0], wts, ctx)

    def update(parts):
        out = {}
        for n, eight in parts.items():
            step = adamw_columns if n == "w_in" else adamw
            out[n] = step(f"adamw_{n}", eight, wts[n], given["m_" + n], given["v_" + n])
        return out

    outs = update(ctx.collect(("C", "B"), grad_x))
    late = outs["w_ff1"][1]
    outs.update(update(ctx.collect(("A",), late)))
    gots, where, parts = ctx.collect_small(late)
    outs.update(update(parts))
    names = sorted(where)
    rows = lambda a: a[None] if a.ndim == 1 else a
    res = adamw_packed("adamw_vectors", gots, [where[n] for n in names], [rows(wts[n]) for n in names],
                       [rows(given["m_" + n]) for n in names], [rows(given["v_" + n]) for n in names])
    for n, four in zip(names, res):
        outs[n] = [a[0] for a in four] if wts[n].ndim == 1 else four
    loss = lax.psum(loss, AXES)
    return (loss, grad_x[None], *[outs[n][0] for n in WEIGHTS], *[outs[n][1] for n in WEIGHTS],
            *[outs[n][2] for n in WEIGHTS], *[outs[n][3] for n in WEIGHTS])
```

```python
import functools

import jax
import jax.numpy as jnp
from jax import lax
from jax.experimental import pallas as pl
from jax.experimental.pallas import tpu as pltpu

f32 = jnp.float32
bf16 = jnp.bfloat16

D_MODEL = 1024
MIX = 512
N_HEADS = 8
QK_NOPE, QK_ROPE, V_HEAD = 64, 32, 64
Q_LORA, KV_LORA = 384, 256
ROPE_THETA = 10000.0
POOL_WINDOWS = (2, 4, 8, 16)
SSD_CHUNK = 128
SSD_XBC = 768
CONV_W = 4
LRU_C = 8.0
EPS = 1e-6
IN_COLS = 7592
ADAM_LR, ADAM_B1, ADAM_B2, ADAM_EPS, ADAM_WD, ADAM_STEP = 0.001, 0.9, 0.999, 1e-08, 0.01, 10

LANE = 128
HALO = 8
POOL_HALO = 16
VMEM_LIMIT = 56 * 1024 * 1024
MATMUL_MAX_K_TILE = 4096
MATMUL_ACC_PASS_WEIGHT = 0.3
MATMUL_VMEM_BUDGET = 40 * 1024 * 1024
N_DEV = 8
AXES = ("x", "y", "c")

U_COLS = 8192
U_GATES, U_POOL, U_Z, U_LG, U_LX, U_XBC, U_CQ, U_KR, U_CKV, U_DT = (
    (0, 4096), (4096, 512), (4608, 512), (5120, 512), (5632, 512), (6144, 768),
    (6912, 384), (7296, 128), (7424, 256), (7680, 128))
KR_LANE = 64
U_DTYPE = bf16


def _cp(sem):
    return pltpu.CompilerParams(dimension_semantics=sem, vmem_limit_bytes=VMEM_LIMIT)


def _pick(dim, pref):
    if dim <= pref:
        return dim
    t = pref
    while t >= LANE:
        if dim % t == 0:
            return t
        t -= LANE
    t = pref
    while dim % t:
        t -= 8
    return t


@functools.partial(jax.custom_vjp, nondiff_argnums=(1,))
def shift_down(x, k):
    row = lax.broadcasted_iota(jnp.int32, x.shape, 0)
    return jnp.where(row >= k, pltpu.roll(x, k, 0), 0.0)


def _shift_down_fwd(x, k):
    return shift_down(x, k), None


def _shift_down_bwd(k, _, g):
    r = g.shape[0]
    row = lax.broadcasted_iota(jnp.int32, g.shape, 0)
    return (jnp.where(row < r - k, pltpu.roll(g, r - k, 0), 0.0),)


shift_down.defvjp(_shift_down_fwd, _shift_down_bwd)


def _tile_spec(tm, width, cb, n=None):
    if n is None:
        return pl.BlockSpec((tm, width), lambda i: (i, cb))
    return pl.BlockSpec((tm, width), lambda i: (n - 1 - i, cb))


def _const_spec(shape):
    nd = len(shape)
    return pl.BlockSpec(shape, lambda i: (0,) * nd)


def seq_fwd(name, f, params, tiles, carries, outs, tm):
    rows = tiles[0][0].shape[0]
    n = rows // tm
    np_, nt, no, nc = len(params), len(tiles), len(outs), len(carries)

    def body(*refs):
        p_refs = refs[:np_]
        t_refs = refs[np_:np_ + nt]
        o_refs = refs[np_ + nt:np_ + nt + no]
        s_refs = refs[np_ + nt + no:np_ + nt + no + nc]
        c_refs = refs[np_ + nt + no + nc:]
        i = pl.program_id(0)

        @pl.when(i == 0)
        def _():
            for c in c_refs:
                c[...] = jnp.zeros_like(c)

        cvals = [c[...] for c in c_refs]
        for s, c in zip(s_refs, cvals):
            s[0] = c
        o, newc = f(i, [r[...] for r in p_refs], cvals, [r[...].astype(f32) for r in t_refs])
        for r, v in zip(o_refs, o):
            r[...] = v.astype(r.dtype)
        for r, v in zip(c_refs, newc):
            r[...] = v

    in_specs = [_const_spec(p.shape) for p in params] + [_tile_spec(tm, w, cb) for (_, w, cb) in tiles]
    out_specs = [_tile_spec(tm, w, 0) for (w, _) in outs]
    out_specs += [pl.BlockSpec((1,) + tuple(c), lambda i, nd=len(c): (i,) + (0,) * nd) for c in carries]
    out_shape = [jax.ShapeDtypeStruct((rows, w), dt) for (w, dt) in outs]
    out_shape += [jax.ShapeDtypeStruct((n,) + tuple(c), f32) for c in carries]
    res = pl.pallas_call(
        body, name=name, grid=(n,), in_specs=in_specs, out_specs=out_specs, out_shape=out_shape,
        scratch_shapes=[pltpu.VMEM(tuple(c), f32) for c in carries],
        compiler_params=_cp(("arbitrary",)),
    )(*params, *[t[0] for t in tiles])
    return list(res[:no]), list(res[no:])


def seq_bwd(name, f, params, tiles, diff, saved, douts, gdtypes, tm):
    rows = tiles[0][0].shape[0]
    n = rows // tm
    np_, nt, nc, nd = len(params), len(tiles), len(saved), len(douts)
    didx = [k for k, d in enumerate(diff) if d]
    ng = len(didx)

    def body(*refs):
        p_refs = refs[:np_]
        t_refs = refs[np_:np_ + nt]
        s_refs = refs[np_ + nt:np_ + nt + nc]
        d_refs = refs[np_ + nt + nc:np_ + nt + nc + nd]
        pos = np_ + nt + nc + nd
        dp_refs = refs[pos:pos + np_]
        dt_refs = refs[pos + np_:pos + np_ + ng]
        dc_refs = refs[pos + np_ + ng:]
        i = pl.program_id(0)
        step = n - 1 - i

        @pl.when(i == 0)
        def _():
            for r in dp_refs:
                r[...] = jnp.zeros_like(r)
            for r in dc_refs:
                r[...] = jnp.zeros_like(r)

        pvals = [r[...] for r in p_refs]
        cvals = [r[0] for r in s_refs]
        xvals = [r[...].astype(f32) for r in t_refs]

        def fn(p, c, xd):
            x = list(xvals)
            for k, v in zip(didx, xd):
                x[k] = v
            return f(step, p, c, x)

        _, vjp = jax.vjp(fn, pvals, cvals, [xvals[k] for k in didx])
        dp, dc, dx = vjp(([r[...].astype(f32) for r in d_refs], [r[...] for r in dc_refs]))
        for r, v in zip(dp_refs, dp):
            r[...] += v
        for r, v in zip(dc_refs, dc):
            r[...] = v
        for r, v in zip(dt_refs, dx):
            r[...] = v.astype(r.dtype)

    in_specs = [_const_spec(p.shape) for p in params] + [_tile_spec(tm, w, cb, n) for (_, w, cb) in tiles]
    in_specs += [pl.BlockSpec((1,) + tuple(s.shape[1:]), lambda i, nd_=s.ndim - 1: (n - 1 - i,) + (0,) * nd_) for s in saved]
    in_specs += [_tile_spec(tm, d.shape[1], 0, n) for d in douts]
    args = list(params) + [t[0] for t in tiles] + list(saved) + list(douts)
    out_specs = [_const_spec(p.shape) for p in params] + [_tile_spec(tm, tiles[k][1], 0, n) for k in didx]
    out_shape = [jax.ShapeDtypeStruct(p.shape, f32) for p in params]
    out_shape += [jax.ShapeDtypeStruct((rows, tiles[k][1]), dt) for k, dt in zip(didx, gdtypes)]
    res = pl.pallas_call(
        body, name=name, grid=(n,), in_specs=in_specs, out_specs=out_specs, out_shape=out_shape,
        scratch_shapes=[pltpu.VMEM(tuple(s.shape[1:]), f32) for s in saved],
        compiler_params=_cp(("arbitrary",)),
    )(*args)
    return list(res[:np_]), list(res[np_:])


def _halvings(dim, lo, hi):
    t, out = _pick(dim, hi), []
    while t >= min(lo, dim) and dim % t == 0:
        out.append(t)
        if t % 2 or (t // 2) % 8:
            break
        t //= 2
    return out


def _matmul_tiles(m, n, k, a_item, b_item, per_out, max_tn=1024, whole_rows=False):
    def vmem_bytes(tm, tn, tk):
        acc = 4 if k // tk > 1 else 0
        return 2 * (tm * tk * a_item + tk * tn * b_item + tm * tn * per_out) + tm * tn * acc

    def traffic(tm, tn, tk):
        nk = k // tk
        return (m * k * a_item * (1 if nk == 1 else n // tn) + k * n * b_item * (m // tm)
                + (nk - 1) * m * n * 8 * MATMUL_ACC_PASS_WEIGHT)

    cands = [(traffic(tm, tn, tk), -tm * tn, tm, tn, tk)
             for tk in _halvings(k, 512, MATMUL_MAX_K_TILE) for tm in _halvings(m, 256, 4096)
             for tn in ([n] if whole_rows else _halvings(n, 512, min(1024, max_tn)))
             if vmem_bytes(tm, tn, tk) <= MATMUL_VMEM_BUDGET]
    return min(cands)[2:]


def matmul(name, a, b, *, ta=False, tb=False, outs=(f32,), epi=None, extras=(), rows=(), row_sums=0, deps=(),
           out_blocks=0):
    m, k = (a.shape[1], a.shape[0]) if ta else a.shape
    n = b.shape[0] if tb else b.shape[1]
    per_out = sum(jnp.dtype(dt).itemsize for dt in outs) + sum(e[0].dtype.itemsize for e in extras)
    whole_rows = bool(rows) or row_sums > 0
    tm, tn, tk = _matmul_tiles(m, n, k, a.dtype.itemsize, b.dtype.itemsize, per_out,
                               n // out_blocks if out_blocks else n, whole_rows)
    nk = k // tk
    ne, nr, nd, no = len(extras), len(rows), len(deps), len(outs)
    dims = (((0 if ta else 1,), (1 if tb else 0,)), ((), ()))

    def body(*refs):
        a_ref, b_ref = refs[0], refs[1]
        e_refs = refs[2:2 + ne]
        r_refs = refs[2 + ne:2 + ne + nr]
        o_refs = refs[2 + ne + nr + nd:2 + ne + nr + nd + no]
        s_refs = refs[2 + ne + nr + nd + no:2 + ne + nr + nd + no + row_sums]
        i, kk = pl.program_id(0), pl.program_id(2)
        part = lax.dot_general(a_ref[...].astype(bf16), b_ref[...].astype(bf16), dims, preferred_element_type=f32)

        def finish(total):
            res = (total,) if epi is None else epi(total, *[e[...] for e in e_refs], *[r[...] for r in r_refs])
            for r, v in zip(o_refs, res[:no]):
                r[...] = v.astype(r.dtype)
            for r, v in zip(s_refs, res[no:]):
                v8 = jnp.broadcast_to(v, r.shape)

                @pl.when(i == 0)
                def _(r=r, v8=v8):
                    r[...] = v8

                @pl.when(i > 0)
                def _(r=r, v8=v8):
                    r[...] += v8

        if nk == 1:
            finish(part)
            return
        acc = refs[-1]

        @pl.when(kk == 0)
        def _():
            acc[...] = part

        @pl.when(jnp.logical_and(kk > 0, kk < nk - 1))
        def _():
            acc[...] += part

        @pl.when(kk == nk - 1)
        def _():
            finish(acc[...] + part)

    a_spec = pl.BlockSpec((tk, tm), lambda i, j, q: (q, i)) if ta else pl.BlockSpec((tm, tk), lambda i, j, q: (i, q))
    b_spec = pl.BlockSpec((tn, tk), lambda i, j, q: (j, q)) if tb else pl.BlockSpec((tk, tn), lambda i, j, q: (q, j))
    def e_spec(e):
        if len(e) == 3:
            return pl.BlockSpec((tm, e[2]), lambda i, j, q, cb=e[1] // e[2]: (i, cb))
        assert e[1] % tn == 0
        return pl.BlockSpec((tm, tn), lambda i, j, q, off=e[1] // tn: (i, off + j))

    e_specs = [e_spec(e) for e in extras]
    r_specs = [pl.BlockSpec((1, tn), lambda i, j, q: (0, j)) for _ in rows]
    if out_blocks:
        per = n // out_blocks // tn
        out_spec = pl.BlockSpec((None, tm, tn), lambda i, j, q: (j // per, i, j % per))
        out_dims = (out_blocks, m, n // out_blocks)
    else:
        out_spec = pl.BlockSpec((tm, tn), lambda i, j, q: (i, j))
        out_dims = (m, n)
    res = pl.pallas_call(
        body, name=name, grid=(m // tm, n // tn, nk),
        in_specs=[a_spec, b_spec] + e_specs + r_specs + [pl.BlockSpec(memory_space=pl.ANY) for _ in deps],
        out_specs=[out_spec for _ in outs] + [pl.BlockSpec((8, tn), lambda i, j, q: (0, j))] * row_sums,
        out_shape=[jax.ShapeDtypeStruct(out_dims, dt) for dt in outs] + [jax.ShapeDtypeStruct((8, n), f32)] * row_sums,
        scratch_shapes=[pltpu.VMEM((tm, tn), f32)] if nk > 1 else [],
        compiler_params=_cp(("arbitrary" if row_sums else "parallel", "parallel", "arbitrary")),
    )(a, b, *[e[0] for e in extras], *rows, *deps)
    return res[0] if len(res) == 1 else tuple(res)


def merge_fwd(name, ys, wbs, u):
    rows, n_out = ys[0].shape[0], wbs[0].shape[1]
    nb = len(ys)
    tm, tn = _pick(rows, 512), _pick(n_out, 512)

    def body(*refs):
        y_refs, w_refs, g_refs = refs[:nb], refs[nb:2 * nb], refs[2 * nb:3 * nb]
        m_ref, p_refs = refs[3 * nb], refs[3 * nb + 1:]
        total = None
        for y_ref, w_ref, g_ref, p_ref in zip(y_refs, w_refs, g_refs, p_refs):
            pre = jnp.dot(y_ref[...], w_ref[...], preferred_element_type=f32)
            p_ref[...] = pre.astype(p_ref.dtype)
            term = jax.nn.sigmoid(g_ref[...].astype(f32)) * pre
            total = term if total is None else total + term
        m_ref[...] = total.astype(m_ref.dtype)

    in_specs = [pl.BlockSpec((tm, y.shape[1]), lambda i, j: (i, 0)) for y in ys]
    in_specs += [pl.BlockSpec((w.shape[0], tn), lambda i, j: (0, j)) for w in wbs]
    in_specs += [pl.BlockSpec((tm, tn), lambda i, j, off=n * (n_out // tn): (i, off + j)) for n in range(nb)]
    out_spec = pl.BlockSpec((tm, tn), lambda i, j: (i, j))
    res = pl.pallas_call(
        body, name=name, grid=(rows // tm, n_out // tn), in_specs=in_specs, out_specs=[out_spec] * (nb + 1),
        out_shape=[jax.ShapeDtypeStruct((rows, n_out), bf16)] * (nb + 1),
        compiler_params=_cp(("parallel", "parallel")),
    )(*ys, *wbs, *([u] * nb))
    return res[0], list(res[1:])


ATT_SCALE = (QK_NOPE + QK_ROPE) ** -0.5
LN2 = 0.6931471805599453
ATT_C = ATT_SCALE / LN2
NT = (((1,), (1,)), ((), ()))
TN = (((0,), (0,)), ((), ()))


def _causal(tq, tk):
    return lax.broadcasted_iota(jnp.int32, (tq, tk), 0) >= lax.broadcasted_iota(jnp.int32, (tq, tk), 1)


def _tri_pairs(n, by_column):
    if by_column:
        pairs = [(i, j) for j in range(n) for i in range(j, n)]
    else:
        pairs = [(i, j) for i in range(n) for j in range(i + 1)]
    return (jnp.asarray([a for a, _ in pairs], jnp.int32), jnp.asarray([b for _, b in pairs], jnp.int32))


FWD_HEADS_PER_STEP = 8
HEADS_PER_STEP = 4
HEAD_PAIR = HEADS_PER_STEP * LANE


def attn_fwd(q, k, v, t):
    rows = q.shape[0]
    n = rows // t
    it, jt = _tri_pairs(n, False)

    def body(it_ref, jt_ref, q_ref, k_ref, v_ref, o_ref, lse_ref, m_s, l_s, acc_s):
        s_id = pl.program_id(1)
        i, j = it_ref[s_id], jt_ref[s_id]

        @pl.when(j == 0)
        def _():
            m_s[...] = jnp.full_like(m_s, -jnp.inf)
            l_s[...] = jnp.zeros_like(l_s)
            acc_s[...] = jnp.zeros_like(acc_s)

        def step(diag):
            for hh in range(FWD_HEADS_PER_STEP):
                sl = slice(LANE * hh, LANE * (hh + 1))
                s = lax.dot_general(q_ref[:, sl], k_ref[:, sl], NT, preferred_element_type=f32)
                if diag:
                    s = jnp.where(_causal(t, t), s, -jnp.inf)
                m_prev = m_s[:, sl]
                m_new = jnp.maximum(m_prev, jnp.max(s, axis=1, keepdims=True))
                alpha = jnp.exp2(m_prev - m_new)
                p = jnp.exp2(s - m_new[:, :1])
                l_s[:, sl] = alpha * l_s[:, sl] + jnp.sum(p, axis=1, keepdims=True)
                acc_s[:, sl] = alpha * acc_s[:, sl] + jnp.dot(p.astype(bf16), v_ref[:, sl], preferred_element_type=f32)
                m_s[:, sl] = m_new

        pl.when(j < i)(lambda: step(False))

        @pl.when(j == i)
        def _():
            step(True)
            o_ref[...] = (acc_s[...] / l_s[...]).astype(o_ref.dtype)
            lse_ref[...] = m_s[...] + jnp.log2(l_s[...])

    width = FWD_HEADS_PER_STEP * LANE
    qs = pl.BlockSpec((t, width), lambda h, s, it_, jt_: (it_[s], h))
    ks = pl.BlockSpec((t, width), lambda h, s, it_, jt_: (jt_[s], h))
    hw = N_HEADS * LANE
    return pl.pallas_call(
        body, name="attn_fwd",
        grid_spec=pltpu.PrefetchScalarGridSpec(
            num_scalar_prefetch=2, grid=(hw // width, it.shape[0]), in_specs=[qs, ks, ks], out_specs=[qs, qs],
            scratch_shapes=[pltpu.VMEM((t, width), f32)] * 3),
        out_shape=[jax.ShapeDtypeStruct((rows, hw), bf16), jax.ShapeDtypeStruct((rows, hw), f32)],
        compiler_params=_cp(("parallel", "arbitrary")),
    )(it, jt, q, k, v)


def attn_bwd(q, k, v, do, o, lse, t):
    rows = q.shape[0]
    n = rows // t
    it, jt = _tri_pairs(n, True)

    def body(it_ref, jt_ref, q_ref, k_ref, v_ref, do_ref, o_ref, lse_ref, dq_ref, dk_ref, dv_ref, dk_s, dv_s):
        s_id = pl.program_id(1)
        i, j = it_ref[s_id], jt_ref[s_id]

        @pl.when(s_id == 0)
        def _():
            dq_ref[...] = jnp.zeros_like(dq_ref)

        @pl.when(i == j)
        def _():
            dk_s[...] = jnp.zeros_like(dk_s)
            dv_s[...] = jnp.zeros_like(dv_s)

        q_rows = pl.ds(pl.multiple_of(i * t, t), t)

        def step(diag):
            for hh in range(HEADS_PER_STEP):
                sl = slice(LANE * hh, LANE * (hh + 1))
                qh, kh, vh, doh = q_ref[:, sl], k_ref[:, sl], v_ref[:, sl], do_ref[:, sl]
                s = lax.dot_general(qh, kh, NT, preferred_element_type=f32)
                p = jnp.exp2(s - lse_ref[:, sl][:, :1])
                if diag:
                    p = jnp.where(_causal(t, t), p, 0.0)
                dp = lax.dot_general(doh, vh, NT, preferred_element_type=f32)
                delta = jnp.sum(doh.astype(f32) * o_ref[:, sl].astype(f32), axis=1, keepdims=True)
                ds = (p * (dp - delta) * LN2).astype(bf16)
                dv_s[:, sl] += lax.dot_general(p.astype(bf16), doh, TN, preferred_element_type=f32)
                dk_s[:, sl] += lax.dot_general(ds, qh, TN, preferred_element_type=f32)
                dq_ref[q_rows, sl] += jnp.dot(ds, kh, preferred_element_type=f32)

        pl.when(i > j)(lambda: step(False))
        pl.when(i == j)(lambda: step(True))

        @pl.when(i == n - 1)
        def _():
            dk_ref[...] = dk_s[...]
            dv_ref[...] = dv_s[...]

    qs = pl.BlockSpec((t, HEAD_PAIR), lambda h, s, it_, jt_: (it_[s], h))
    ks = pl.BlockSpec((t, HEAD_PAIR), lambda h, s, it_, jt_: (jt_[s], h))
    dqs = pl.BlockSpec((rows, HEAD_PAIR), lambda h, s, it_, jt_: (0, h))
    hw = N_HEADS * LANE
    return pl.pallas_call(
        body, name="attn_bwd",
        grid_spec=pltpu.PrefetchScalarGridSpec(
            num_scalar_prefetch=2, grid=(hw // HEAD_PAIR, it.shape[0]), in_specs=[qs, ks, ks, qs, qs, qs],
            out_specs=[dqs, ks, ks], scratch_shapes=[pltpu.VMEM((t, HEAD_PAIR), f32)] * 2),
        out_shape=[jax.ShapeDtypeStruct((rows, hw), f32)] * 3,
        compiler_params=_cp(("parallel", "arbitrary")),
    )(it, jt, q, k, v, do, o, lse)


def _steps(tm):
    k, out = 1, []
    while k < tm:
        out.append(k)
        k *= 2
    return out


def _gelu_gate(h, g):
    return h * jax.nn.gelu(g)


def scan_fwd(a, u, gate, tm):
    rows, ch = a.shape
    n = rows // tm

    def body(a_ref, u_ref, gt_ref, h_ref, y_ref, h_s):
        @pl.when(pl.program_id(0) == 0)
        def _():
            h_s[...] = jnp.zeros_like(h_s)

        av, bv = a_ref[...], u_ref[...]
        row = lax.broadcasted_iota(jnp.int32, av.shape, 0)
        for k in _steps(tm):
            a_sh = jnp.where(row >= k, pltpu.roll(av, k, 0), 1.0)
            b_sh = jnp.where(row >= k, pltpu.roll(bv, k, 0), 0.0)
            bv = av * b_sh + bv
            av = av * a_sh
        h = bv + av * h_s[HALO - 1:HALO, :]
        h_ref[...] = h
        y_ref[...] = _gelu_gate(h, gt_ref[...].astype(f32)).astype(y_ref.dtype)
        h_s[...] = h[tm - HALO:, :]

    spec = pl.BlockSpec((tm, ch), lambda i: (i, 0))
    gt_spec = pl.BlockSpec((tm, gate[1]), lambda i: (i, gate[2]))
    return pl.pallas_call(
        body, name="lru_scan_fwd", grid=(n,), in_specs=[spec, spec, gt_spec], out_specs=[spec, spec],
        out_shape=[jax.ShapeDtypeStruct((rows, ch), f32), jax.ShapeDtypeStruct((rows, ch), bf16)],
        scratch_shapes=[pltpu.VMEM((HALO, ch), f32)], compiler_params=_cp(("arbitrary",)),
    )(a, u, gate[0])


def scan_bwd(a, h, gate, dy, tm):
    rows, ch = a.shape
    n = rows // tm
    per = tm // HALO

    def body(a_ref, h_ref, hp_ref, gt_ref, dy_ref, da_ref, du_ref, dg_ref, g_s, a_s):
        i = pl.program_id(0)
        step = n - 1 - i

        @pl.when(i == 0)
        def _():
            g_s[...] = jnp.zeros_like(g_s)
            a_s[...] = jnp.zeros_like(a_s)

        _, vjp = jax.vjp(_gelu_gate, h_ref[...], gt_ref[...].astype(f32))
        dh, dgate = vjp(dy_ref[...].astype(f32))
        dg_ref[...] = dgate.astype(dg_ref.dtype)
        a0 = a_ref[...]
        row = lax.broadcasted_iota(jnp.int32, a0.shape, 0)
        av = jnp.where(row < tm - 1, pltpu.roll(a0, tm - 1, 0), a_s[0:1, :])
        bv = dh
        for k in _steps(tm):
            a_sh = jnp.where(row < tm - k, pltpu.roll(av, tm - k, 0), 1.0)
            b_sh = jnp.where(row < tm - k, pltpu.roll(bv, tm - k, 0), 0.0)
            bv = bv + av * b_sh
            av = av * a_sh
        g = bv + av * g_s[0:1, :]
        h_last = jnp.where(step > 0, hp_ref[HALO - 1:HALO, :], 0.0)
        h_prev = jnp.where(row >= 1, pltpu.roll(h_ref[...], 1, 0), h_last)
        du_ref[...] = g
        da_ref[...] = g * h_prev
        g_s[...] = g[0:HALO, :]
        a_s[...] = a0[0:HALO, :]

    spec = pl.BlockSpec((tm, ch), lambda i: (n - 1 - i, 0))
    hp_spec = pl.BlockSpec((HALO, ch), lambda i: (jnp.maximum((n - 1 - i) * per - 1, 0), 0))
    gt_spec = pl.BlockSpec((tm, gate[1]), lambda i: (n - 1 - i, gate[2]))
    return pl.pallas_call(
        body, name="lru_scan_bwd", grid=(n,), in_specs=[spec, spec, hp_spec, gt_spec, spec], out_specs=[spec, spec, spec],
        out_shape=[jax.ShapeDtypeStruct((rows, ch), f32)] * 2 + [jax.ShapeDtypeStruct((rows, ch), bf16)],
        scratch_shapes=[pltpu.VMEM((HALO, ch), f32)] * 2,
        compiler_params=_cp(("arbitrary",)),
    )(a, h, h, gate[0], dy)


def _rms(x, g):
    return x * lax.rsqrt(jnp.mean(x * x, axis=-1, keepdims=True) + EPS) * g


def f_rms(step, p, c, x):
    return [_rms(x[0], p[0])], []


def _rope_swap(x):
    lane = lax.broadcasted_iota(jnp.int32, x.shape, 1)
    half = QK_ROPE // 2
    sw = jnp.where(lane < KR_LANE + half, pltpu.roll(x, LANE - half, 1), pltpu.roll(x, half, 1))
    return jnp.where(jnp.logical_and(lane >= KR_LANE, lane < KR_LANE + QK_ROPE), sw, 0.0)


def _rope(x, cosf, sinf):
    return x * cosf + _rope_swap(x) * sinf


def _heads(x):
    return [x[:, LANE * h:LANE * (h + 1)] for h in range(x.shape[1] // LANE)]


def q_rope_epi(q, cosf, sinf):
    return (jnp.concatenate([_rope(b, cosf, sinf) * ATT_C for b in _heads(q)], axis=1),)


def k_rope_epi(kn, kr, cosf, sinf):
    kr_rot = _rope(kr.astype(f32), cosf, sinf)
    return (jnp.concatenate([b + kr_rot for b in _heads(kn)], axis=1),)


def rope_bwd(dqr, dkr, cosf, sinf):
    back = lambda g: g * cosf + _rope_swap(g * sinf)
    dq = jnp.concatenate([back(b) * ATT_C for b in _heads(dqr)], axis=1)
    dkrope = back(sum(_heads(dkr)))
    return dq, dkr, dkrope


def _conv(tail, x, w, b):
    xf = jnp.concatenate([tail, x], axis=0)
    acc = b + w[CONV_W - 1:CONV_W, :] * xf
    for k in range(CONV_W - 1):
        acc = acc + w[k:k + 1, :] * shift_down(xf, CONV_W - 1 - k)
    return acc[HALO:, :]


def f_pool(step, p, c, x):
    wp, sc = p
    (tail,) = c
    (u,) = x
    tm = u.shape[0]
    xf = jnp.concatenate([tail, u], axis=0)
    sums, s, w = [], xf, 1
    while w < POOL_WINDOWS[-1]:
        s = s + shift_down(s, w)
        w *= 2
        sums.append(s)
    t = step * tm + lax.broadcasted_iota(jnp.int32, (tm, 1), 0)
    ys = []
    for g, (w, s) in enumerate(zip(POOL_WINDOWS, sums)):
        sl = slice(LANE * g, LANE * (g + 1))
        cnt = jnp.minimum(t + 1, w).astype(f32)
        d = s[POOL_HALO:, sl] / cnt - u[:, sl]
        ys.append(jnp.dot(d.astype(bf16), wp[LANE * g:LANE * (g + 1), :].astype(bf16), preferred_element_type=f32))
    return [jnp.concatenate(ys, axis=1) * sc], [u[tm - POOL_HALO:, :]]


def f_ssd(step, p, c, x):
    conv_w, conv_b, dtb, alog, dsk, ng = p
    tail, s_in = c[0], c[1:]
    z, xbc, dt = x
    ln = z.shape[0]
    xc = jax.nn.silu(_conv(tail, xbc, conv_w, conv_b))
    xs, bb, cc = xc[:, :MIX], xc[:, MIX:MIX + LANE], xc[:, MIX + LANE:]
    dtv = jax.nn.softplus(dt + dtb[0:1, :])
    a = dtv * -jnp.exp(alog[0:1, :])
    ri = lax.broadcasted_iota(jnp.int32, (ln, ln), 0)
    ci = lax.broadcasted_iota(jnp.int32, (ln, ln), 1)
    tril = (ri >= ci).astype(f32)
    triu = (ri <= ci).astype(f32)
    hi = lax.Precision.HIGHEST
    a_cs = jnp.dot(tril, a, precision=hi, preferred_element_type=f32)
    a_cs_t = lax.dot_general(a, triu, TN, precision=hi, preferred_element_type=f32)
    a_tot = jnp.sum(a, axis=0, keepdims=True)
    lane = lax.broadcasted_iota(jnp.int32, (1, LANE), 1)
    half = [(lane < 64).astype(f32), (lane >= 64).astype(f32)]
    hrow = lax.broadcasted_iota(jnp.int32, (LANE, 1), 0)

    def head(v, h):
        return jnp.sum(v * (lane == h).astype(f32), axis=1, keepdims=True)

    def pair(v, j):
        return head(v, 2 * j) * half[0] + head(v, 2 * j + 1) * half[1]

    cg = [(cc * half[g]).astype(bf16) for g in range(2)]
    bg = [(bb * half[g]).astype(bf16) for g in range(2)]
    cb = [lax.dot_general(cg[g], bg[g], NT, preferred_element_type=f32) for g in range(2)]
    ys, s_out = [], []
    for j in range(4):
        g = j // 2
        xs_j = xs[:, LANE * j:LANE * (j + 1)]
        xj = xs_j * pair(dtv, j)
        yj = xs_j * pair(dsk[0:1, :], j)
        for hh in range(2):
            h = 2 * j + hh
            rowv = jnp.sum(a_cs_t * (hrow == h).astype(f32), axis=0, keepdims=True)
            lmat = jnp.exp(jnp.where(ri >= ci, head(a_cs, h) - rowv, -jnp.inf))
            yj = yj + jnp.dot((cb[g] * lmat).astype(bf16), (xj * half[hh]).astype(bf16), preferred_element_type=f32)
        acs = pair(a_cs, j)
        tot = pair(a_tot, j)
        yj = yj + jnp.exp(acs) * jnp.dot(cg[g], s_in[j].astype(bf16), preferred_element_type=f32)
        s_new = jnp.exp(tot) * s_in[j] + lax.dot_general(bg[g], (xj * jnp.exp(tot - acs)).astype(bf16), TN,
                                                         preferred_element_type=f32)
        ys.append(yj)
        s_out.append(s_new)
    y = jnp.concatenate(ys, axis=1) * jax.nn.silu(z)
    return [_rms(y, ng)], [xbc[ln - HALO:, :]] + s_out


def _neg_expm1(y):
    series = -y * (1.0 + y * (0.5 + y * (1.0 / 6 + y * (1.0 / 24 + y * (1.0 / 120)))))
    return jnp.where(y > -0.05, series, 1.0 - jnp.exp(y))


def f_lru_pre(step, p, c, x):
    cw, cb_, wa, ba, wi, bi, lam = p
    (tail,) = c
    (lx,) = x
    tm = lx.shape[0]
    xc = _conv(tail, lx, cw, cb_)
    xb = xc.astype(bf16)
    r = jax.nn.sigmoid(jnp.dot(xb, wa.astype(bf16), preferred_element_type=f32) + ba)
    it = jax.nn.sigmoid(jnp.dot(xb, wi.astype(bf16), preferred_element_type=f32) + bi)
    log_a = -LRU_C * r * jax.nn.softplus(-lam)
    mult = jnp.sqrt(_neg_expm1(2.0 * log_a))
    return [jnp.exp(log_a), xc * it * mult], [lx[tm - HALO:, :]]


def loss_head(x, tgt, g, tm):
    rows, d = x.shape
    n = rows // tm

    def body(x_ref, t_ref, g_ref, loss_ref, dx_ref, dg_ref):
        @pl.when(pl.program_id(0) == 0)
        def _():
            loss_ref[...] = jnp.zeros_like(loss_ref)
            dg_ref[...] = jnp.zeros_like(dg_ref)

        def fn(gv, xv):
            err = _rms(xv, gv) - t_ref[...]
            return 0.5 * jnp.sum(jnp.mean(err * err, axis=-1, keepdims=True))

        val, (dg, dx) = jax.value_and_grad(fn, argnums=(0, 1))(g_ref[...], x_ref[...])
        loss_ref[...] += val
        dg_ref[...] += dg
        dx_ref[...] = dx

    spec = pl.BlockSpec((tm, d), lambda i: (i, 0))
    return pl.pallas_call(
        body, name="loss_head", grid=(n,), in_specs=[spec, spec, _const_spec((1, d))],
        out_specs=[_const_spec((8, LANE)), spec, _const_spec((1, d))],
        out_shape=[jax.ShapeDtypeStruct((8, LANE), f32), jax.ShapeDtypeStruct((rows, d), f32),
                   jax.ShapeDtypeStruct((1, d), f32)],
        compiler_params=_cp(("arbitrary",)),
    )(x, tgt, g)


def ew(name, fn, ins, outs, tm):
    rows = ins[0][0].shape[0]
    ni = len(ins)

    def body(*refs):
        res = fn(*[r[...].astype(f32) for r in refs[:ni]])
        for r, v in zip(refs[ni:], res):
            r[...] = v.astype(r.dtype)

    return pl.pallas_call(
        body, name=name, grid=(rows // tm,), in_specs=[_tile_spec(tm, w, cb) for (_, w, cb) in ins],
        out_specs=[_tile_spec(tm, w, 0) for (w, _) in outs],
        out_shape=[jax.ShapeDtypeStruct((rows, w), dt) for (w, dt) in outs],
        compiler_params=_cp(("parallel",)),
    )(*[t[0] for t in ins])


def _peers():
    x, y, c = lax.axis_index("x"), lax.axis_index("y"), lax.axis_index("c")
    me = 4 * x + 2 * y + c
    out = []
    for k in range(1, N_DEV):
        px = 1 - x if k & 4 else x
        py = 1 - y if k & 2 else y
        pc = 1 - c if k & 1 else c
        out.append(((px, py, pc), 4 * px + 2 * py + pc))
    return me, out


_HBM = pl.BlockSpec(memory_space=pltpu.HBM)
_SEM = pl.BlockSpec(memory_space=pltpu.SEMAPHORE)
_EFFECT = pltpu.SideEffectType.DATAFLOW_SIDE_EFFECTING


def _remote(src_ref, land_ref, gather, me, pid, dev, send_sems, recv_sems, k, recv_side):
    return pltpu.make_async_remote_copy(
        src_ref=src_ref if gather else src_ref.at[pid], dst_ref=land_ref.at[pid if recv_side else me],
        send_sem=send_sems.at[k], recv_sem=recv_sems.at[k], device_id=dev, device_id_type=pl.DeviceIdType.MESH)


def _own(src_ref, land_ref, gather, me, sem):
    return pltpu.make_async_copy(src_ref if gather else src_ref.at[me], land_ref.at[me], sem)


def exchange_start(name, srcs, gather, deps=()):
    n, nd = len(srcs), len(deps)
    shapes = [(s.shape if gather else s.shape[1:]) for s in srcs]
    lands = [lax.empty((N_DEV,) + tuple(sh), s.dtype) for s, sh in zip(srcs, shapes)]

    def body(*refs):
        src_refs, land_refs = refs[:n], refs[n:2 * n]
        send_sems, recv_sems, own_sem = refs[2 * n + nd:2 * n + nd + 3]
        token = refs[-1]
        me, peers = _peers()
        for k, (dev, pid) in enumerate(peers):
            for s_ref, l_ref in zip(src_refs, land_refs):
                _remote(s_ref, l_ref, gather, me, pid, dev, send_sems, recv_sems, k, False).start()
        for s_ref, l_ref in zip(src_refs, land_refs):
            _own(s_ref, l_ref, gather, me, own_sem).start()
        token[...] = jnp.zeros_like(token)

    hbm = lambda a: pltpu.with_memory_space_constraint(a, pltpu.HBM)
    res = pl.pallas_call(
        body, name=name,
        out_shape=(pltpu.SemaphoreType.DMA((N_DEV - 1,)), pltpu.SemaphoreType.DMA((N_DEV - 1,)), pltpu.SemaphoreType.DMA(()),
                   *[pltpu.HBM(a.shape, a.dtype) for a in list(srcs) + lands], jax.ShapeDtypeStruct((8, LANE), f32)),
        in_specs=[_HBM] * (2 * n) + [pl.BlockSpec(memory_space=pl.ANY)] * nd,
        out_specs=(_SEM, _SEM, _SEM, *([_HBM] * (2 * n)), pl.BlockSpec(memory_space=pltpu.VMEM)),
        input_output_aliases={i: 3 + i for i in range(2 * n)},
        compiler_params=pltpu.CompilerParams(has_side_effects=_EFFECT),
    )(*[hbm(a) for a in list(srcs) + lands], *deps)
    return dict(sems=res[:3], srcs=list(res[3:3 + n]), lands=list(res[3 + n:3 + 2 * n]), token=res[-1], gather=gather)


def exchange_wait(name, h, afters):
    n, gather = len(h["srcs"]), h["gather"]

    def body(*refs):
        src_refs, land_refs = refs[:n], refs[n:2 * n]
        send_sems, recv_sems, own_sem = refs[2 * n:2 * n + 3]
        me, peers = _peers()
        for k, (dev, pid) in enumerate(peers):
            for s_ref, l_ref in zip(src_refs, land_refs):
                _remote(s_ref, l_ref, gather, me, pid, dev, send_sems, recv_sems, k, True).wait_recv()
        for k, (dev, pid) in enumerate(peers):
            for s_ref, l_ref in zip(src_refs, land_refs):
                _remote(s_ref, l_ref, gather, me, pid, dev, send_sems, recv_sems, k, False).wait_send()
        for s_ref, l_ref in zip(src_refs, land_refs):
            _own(s_ref, l_ref, gather, me, own_sem).wait()

    arrs = h["srcs"] + h["lands"]
    res = pl.pallas_call(
        body, name=name, out_shape=tuple(pltpu.HBM(a.shape, a.dtype) for a in arrs),
        in_specs=[_HBM] * (2 * n) + [_SEM, _SEM, _SEM] + [pl.BlockSpec(memory_space=pl.ANY)] * len(afters),
        out_specs=tuple([_HBM] * (2 * n)), input_output_aliases={i: i for i in range(2 * n)},
        compiler_params=pltpu.CompilerParams(has_side_effects=_EFFECT),
    )(*arrs, *h["sems"], *afters)
    return list(res[n:])


def _adam_update(g, w, m, v):
    mn = ADAM_B1 * m + (1.0 - ADAM_B1) * g
    vn = ADAM_B2 * v + (1.0 - ADAM_B2) * jnp.square(g)
    m_hat = mn / (1.0 - ADAM_B1 ** ADAM_STEP)
    v_hat = vn / (1.0 - ADAM_B2 ** ADAM_STEP)
    return -ADAM_LR * (m_hat / (jnp.sqrt(v_hat) + ADAM_EPS) + ADAM_WD * w), mn, vn


def _adamw_vectors(name, parts, w, m, v):
    nl = len(parts)

    def body(*refs):
        p_refs = refs[:nl]
        w_ref, m_ref, v_ref, g_ref, d_ref, nm_ref, nv_ref = refs[nl:]
        for ll, p_ref in enumerate(p_refs):
            row = slice(ll, ll + 1)
            g = p_ref[0:1, :]
            for i in range(1, N_DEV):
                g = g + p_ref[i:i + 1, :]
            delta, mn, vn = _adam_update(g, w_ref[row, :], m_ref[row, :], v_ref[row, :])
            g_ref[row, :] = g
            d_ref[row, :] = delta
            nm_ref[row, :] = mn
            nv_ref[row, :] = vn

    return list(pl.pallas_call(body, name=name, out_shape=[jax.ShapeDtypeStruct(w.shape, f32)] * 4)(*parts, w, m, v))


def adamw_packed(name, gots, where, ws, ms, vs):
    ng, npar = len(gots), len(ws)

    def body(*refs):
        g_refs = refs[:ng]
        w_refs, m_refs, v_refs = (refs[ng + k * npar:ng + (k + 1) * npar] for k in range(3))
        o_refs = refs[ng + 3 * npar:]
        for p in range(npar):
            width = w_refs[p].shape[1]
            for l, (which, off) in enumerate(where[p]):
                row = slice(l, l + 1)
                cols = slice(off, off + width)
                g = g_refs[which][0:1, cols]
                for i in range(1, N_DEV):
                    g = g + g_refs[which][i:i + 1, cols]
                delta, mn, vn = _adam_update(g, w_refs[p][row, :], m_refs[p][row, :], v_refs[p][row, :])
                for k, val in enumerate((g, delta, mn, vn)):
                    o_refs[4 * p + k][row, :] = val

    out_shape = [jax.ShapeDtypeStruct(w.shape, f32) for w in ws for _ in range(4)]
    res = pl.pallas_call(body, name=name, out_shape=out_shape, compiler_params=_cp(()))(*gots, *ws, *ms, *vs)
    return [list(res[4 * p:4 * p + 4]) for p in range(npar)]


def adamw_columns(name, parts, w, m, v):
    nl, kk, cc = w.shape
    view = lambda a: jnp.transpose(a, (2, 0, 1))
    tc = min(LANE, cc)

    def body(*refs):
        p_refs = refs[:nl]
        w_ref, m_ref, v_ref, g_ref, d_ref, nm_ref, nv_ref = refs[nl:]
        for l, p_ref in enumerate(p_refs):
            g = p_ref[0].astype(f32)
            for i in range(1, N_DEV):
                g = g + p_ref[i].astype(f32)
            g = g.T
            delta, mn, vn = _adam_update(g, w_ref[:, l, :], m_ref[:, l, :], v_ref[:, l, :])
            g_ref[:, l, :] = g
            d_ref[:, l, :] = delta
            nm_ref[:, l, :] = mn
            nv_ref[:, l, :] = vn

    p_spec = pl.BlockSpec((N_DEV, kk, tc), lambda j: (0, 0, j))
    w_spec = pl.BlockSpec((tc, nl, kk), lambda j: (j, 0, 0))
    res = pl.pallas_call(
        body, name=name, grid=(pl.cdiv(cc, tc),), in_specs=[p_spec] * nl + [w_spec] * 3, out_specs=[w_spec] * 4,
        out_shape=[jax.ShapeDtypeStruct((cc, nl, kk), f32)] * 4, compiler_params=_cp(("parallel",)),
    )(*parts, view(w), view(m), view(v))
    return [jnp.transpose(a, (1, 2, 0)) for a in res]


def adamw(name, parts, w, m, v):
    nl = len(parts)
    shape = w.shape[1:]
    c = shape[-1]
    r = 1
    for s in shape[:-1]:
        r *= s
    if r == 1:
        return _adamw_vectors(name, parts, w, m, v)
    tr = _pick(r, 256) if r % 8 == 0 else r
    nb = r // tr
    parts2 = [p.reshape(N_DEV, r, c) for p in parts]
    w2, m2, v2 = (a.reshape(nl, r, c) for a in (w, m, v))

    def body(*refs):
        p_refs = refs[:nl]
        w_ref, m_ref, v_ref, g_ref, d_ref, nm_ref, nv_ref = refs[nl:]
        layer = pl.program_id(0)
        for ll, p_ref in enumerate(p_refs):
            @pl.when(layer == ll)
            def _(p_ref=p_ref):
                g = p_ref[0].astype(f32)
                for i in range(1, N_DEV):
                    g = g + p_ref[i].astype(f32)
                delta, mn, vn = _adam_update(g, w_ref[0], m_ref[0], v_ref[0])
                g_ref[0] = g
                d_ref[0] = delta
                nm_ref[0] = mn
                nv_ref[0] = vn

    def p_spec(ll):
        return pl.BlockSpec((N_DEV, tr, c), lambda l, i: (0, jnp.where(l == ll, i, jnp.where(l > ll, nb - 1, 0)), 0))

    spec = pl.BlockSpec((1, tr, c), lambda l, i: (l, i, 0))
    res = pl.pallas_call(
        body, name=name, grid=(nl, nb), in_specs=[p_spec(ll) for ll in range(nl)] + [spec, spec, spec],
        out_specs=[spec] * 4, out_shape=[jax.ShapeDtypeStruct((nl, r, c), f32)] * 4,
        compiler_params=_cp(("arbitrary", "arbitrary")),
    )(*parts2, w2, m2, v2)
    return [a.reshape(w.shape) for a in res]


_IN_SPLITS = dict(cq=(0, 384), ckv=(384, 640), kr=(640, 672), pool=(672, 1184), z=(1184, 1696), xbc=(1696, 2464),
                  dt=(2464, 2472), lg=(2472, 2984), lx=(2984, 3496), gates=(3496, 7592))


W_IN_SHARD = IN_COLS // N_DEV

_PAD_ORDER = ("gates", "pool", "z", "lg", "lx", "xbc", "cq", KR_LANE, "kr", LANE - KR_LANE - QK_ROPE, "ckv", "dt",
              LANE - 8, U_COLS - U_DT[0] - LANE)
_SEGMENTS = ((0, U_CQ[0], 384), (384, U_CKV[0], 256), (640, U_KR[0] + KR_LANE, QK_ROPE), (672, U_POOL[0], 512),
             (1184, U_Z[0], 512), (1696, U_XBC[0], 768), (2464, U_DT[0], 8), (2472, U_LG[0], 512), (2984, U_LX[0], 512),
             (3496, 0, 4096))


def _pad_w_in(shards):
    rows = shards.shape[1]
    pieces = []
    for item in _PAD_ORDER:
        if isinstance(item, int):
            pieces.append(jnp.zeros((rows, item), shards.dtype))
            continue
        a, b = _IN_SPLITS[item]
        for d in range(a // W_IN_SHARD, (b - 1) // W_IN_SHARD + 1):
            lo, hi = max(a, d * W_IN_SHARD), min(b, (d + 1) * W_IN_SHARD)
            pieces.append(shards[d, :, lo - d * W_IN_SHARD:hi - d * W_IN_SHARD])
    return jnp.concatenate(pieces, axis=1)


def _w_in_blocks(g):
    blocks = []
    for d in range(N_DEV):
        a, b = d * W_IN_SHARD, (d + 1) * W_IN_SHARD
        pieces = []
        for ref, pad, width in _SEGMENTS:
            lo, hi = max(a, ref), min(b, ref + width)
            if lo < hi:
                pieces.append(g[:, pad + lo - ref:pad + hi - ref])
        blocks.append(jnp.concatenate(pieces, axis=1))
    return jnp.stack(blocks).astype(bf16)


def _head_pad_cols(w, per, lo, hi):
    k = w.shape[0]
    w = w.reshape(k, N_HEADS, per)[:, :, lo:hi]
    return jnp.pad(w, ((0, 0), (0, 0), (0, LANE - (hi - lo)))).reshape(k, N_HEADS * LANE)


def _head_unpad_cols(g, n):
    k = g.shape[0]
    return g.reshape(k, N_HEADS, LANE)[:, :, :n]


def _on_diagonal():
    i = lax.broadcasted_iota(jnp.int32, (8, 1, 8, 1), 0)
    j = lax.broadcasted_iota(jnp.int32, (8, 1, 8, 1), 2)
    return i == j


def _block_diag(w):
    w4 = jnp.broadcast_to(w[:, :, None, :], (8, 64, 8, 64))
    return jnp.where(_on_diagonal(), w4, 0.0).reshape(MIX, MIX)


def _block_diag_inv(g):
    return jnp.sum(jnp.where(_on_diagonal(), g.reshape(8, 64, 8, 64), 0.0), axis=2)


def _head8(v):
    return jnp.pad(v[None, :], ((0, 7), (0, LANE - v.shape[0])))


GROUPS = dict(A=("w_in",), B=("w_uq", "w_ukv", "ssd_conv_w", "lru_conv_w", "w_branch", "w_out"),
              C=("w_ff1", "w_ff2", "w_ple_gate", "w_ple"))


def _kernel_weights(grp, fw):
    if grp == "A":
        w_in = _pad_w_in(fw["w_in"])
        return dict(w_in=w_in, w_dt=w_in[:, U_DT[0]:U_DT[0] + LANE])
    if grp == "C":
        return dict(w_ff1=fw["w_ff1"], w_ff2=fw["w_ff2"], w_pg=fw["w_ple_gate"], w_ple=fw["w_ple"])
    wb = fw["w_branch"]
    wb0 = jnp.pad(wb[0].reshape(N_HEADS, V_HEAD, D_MODEL), ((0, 0), (0, LANE - V_HEAD), (0, 0))).reshape(N_HEADS * LANE, D_MODEL)
    return dict(
        w_uq=_head_pad_cols(fw["w_uq"], QK_NOPE + QK_ROPE, 0, QK_NOPE + QK_ROPE),
        w_uk=_head_pad_cols(fw["w_ukv"], QK_NOPE + V_HEAD, 0, QK_NOPE),
        w_uv=_head_pad_cols(fw["w_ukv"], QK_NOPE + V_HEAD, QK_NOPE, QK_NOPE + V_HEAD),
        wb=[wb0, wb[1], wb[2], wb[3]], w_out=fw["w_out"], ssd_conv_w=fw["ssd_conv_w"], lru_conv_w=fw["lru_conv_w"])


def _layer_params(sp, l):
    row = lambda n: sp[n][l][None, :]
    return dict(
        g_mix=row("g_mix"), q_norm=row("q_norm"), kv_norm=row("kv_norm"),
        pool=[sp["w_pool"][l].reshape(4 * LANE, LANE), row("pool_scale")],
        ssd=[None, row("ssd_conv_b"), _head8(sp["ssd_dt_bias"][l]), _head8(sp["ssd_a_log"][l]),
             _head8(sp["ssd_d"][l]), row("ssd_norm")],
        lru=[None, row("lru_conv_b"), _block_diag(sp["lru_w_a"][l]), row("lru_b_a"),
             _block_diag(sp["lru_w_i"][l]), row("lru_b_i"), row("lru_lambda")],
        g_mlp=row("g_mlp"), g_ple=row("g_ple"),
    )


_sig = jax.nn.sigmoid
_SSD_CARRY = [(HALO, SSD_XBC)] + [(LANE, LANE)] * 4


def _tiles(rows):
    return dict(tm=_pick(rows, 512), ta=_pick(rows, 512), tp=_pick(rows, 512), tl=_pick(rows, 512), ts=_pick(rows, 256))


def _mixer_tiles(u, dt32):
    return dict(
        cq=(u, 384, U_CQ[0] // 384), ckv=(u, 256, U_CKV[0] // 256), kr=(u, LANE, U_KR[0] // LANE),
        pool=(u, MIX, U_POOL[0] // MIX), z=(u, MIX, U_Z[0] // MIX), xbc=(u, SSD_XBC, U_XBC[0] // SSD_XBC),
        dt=(dt32, LANE, 0), lg=(u, MIX, U_LG[0] // MIX), lx=(u, MIX, U_LX[0] // MIX))


def _add_norm(acc, resid, g):
    x = acc + resid
    return x, _rms(x, g)


def _layer_fwd(x, h, p_bf, ctx, l, pr, g_next, cosf, sinf):
    rows = x.shape[0]
    ts = _tiles(rows)
    tm = ts["tm"]
    nm = lambda s: f"{s}_l{l}"
    r = dict(x=x)
    if h is None:
        (h,), _ = seq_fwd(nm("rms_in"), f_rms, [pr["g_mix"]], [(x, D_MODEL, 0)], [], [(D_MODEL, bf16)], tm)
    early = [h] + ([cosf, sinf, p_bf] + [a for v in pr.values() for a in (v if isinstance(v, list) else [v]) if a is not None]
                   if l == 0 else [])
    w = dict(_kernel_weights("A", ctx.weights(l, "A", early)))
    u = matmul(nm("w_in"), h, w["w_in"], outs=(U_DTYPE,))
    dt32 = matmul(nm("w_dt"), h, w["w_dt"])
    mt = _mixer_tiles(u, dt32)
    (cqn,), _ = seq_fwd(nm("rms_q"), f_rms, [pr["q_norm"]], [mt["cq"]], [], [(Q_LORA, bf16)], tm)
    (ckvn,), _ = seq_fwd(nm("rms_kv"), f_rms, [pr["kv_norm"]], [mt["ckv"]], [], [(KV_LORA, bf16)], tm)
    (yb,), pool_saved = seq_fwd(nm("pool"), f_pool, pr["pool"], [mt["pool"]], [(POOL_HALO, MIX)], [(MIX, bf16)], ts["tp"])
    w.update(_kernel_weights("B", ctx.weights(l, "B", yb)))
    pr = dict(pr, ssd=[w["ssd_conv_w"]] + pr["ssd"][1:], lru=[w["lru_conv_w"]] + pr["lru"][1:])
    tables = [(cosf, 0, LANE), (sinf, 0, LANE)]
    qr = matmul(nm("w_uq"), cqn, w["w_uq"], outs=(bf16,), epi=q_rope_epi, extras=tables)
    kr = matmul(nm("w_uk"), ckvn, w["w_uk"], outs=(bf16,), epi=k_rope_epi, extras=[(u, U_KR[0], LANE)] + tables)
    vb = matmul(nm("w_uv"), ckvn, w["w_uv"], outs=(bf16,))
    o, lse = attn_fwd(qr, kr, vb, ts["ta"])
    (yc,), ssd_saved = seq_fwd(nm("ssd"), f_ssd, pr["ssd"], [mt["z"], mt["xbc"], mt["dt"]], _SSD_CARRY, [(MIX, bf16)], SSD_CHUNK)
    (la, lu), lru_saved = seq_fwd(nm("lru_pre"), f_lru_pre, pr["lru"], [mt["lx"]], [(HALO, MIX)], [(MIX, f32), (MIX, f32)], ts["tl"])
    hh, yd = scan_fwd(la, lu, mt["lg"], ts["ts"])
    ys = [o, yb, yc, yd]
    m, pres = merge_fwd(nm("merge"), ys, w["wb"], u)
    x1, h2 = matmul(nm("w_out"), m, w["w_out"], outs=(f32, bf16), epi=_add_norm, extras=[(x, 0)], rows=[pr["g_mlp"]])
    w.update(_kernel_weights("C", ctx.weights(l, "C", h2)))
    a1, act = matmul(nm("ff1"), h2, w["w_ff1"], outs=(bf16, bf16), epi=lambda acc: (acc, jnp.square(jnp.maximum(acc, 0.0))))
    x2, h3 = matmul(nm("ff2"), act, w["w_ff2"], outs=(f32, bf16), epi=_add_norm, extras=[(x1, 0)], rows=[pr["g_ple"]])
    gl = matmul(nm("ple_gate"), h3, w["w_pg"])
    if g_next is None:
        x3, pe = matmul(nm("ple"), p_bf, w["w_ple"], outs=(f32, f32), epi=lambda acc, g, xr: (xr + acc * _sig(g), acc),
                        extras=[(gl, 0), (x2, 0)])
        h_next = None
    else:
        def ple_norm(acc, g, xr, gn):
            xo = xr + acc * _sig(g)
            return xo, acc, _rms(xo, gn)

        x3, pe, h_next = matmul(nm("ple"), p_bf, w["w_ple"], outs=(f32, f32, bf16), epi=ple_norm,
                                extras=[(gl, 0), (x2, 0)], rows=[g_next])
    r.update(h=h, u=u, cqn=cqn, ckvn=ckvn, vb=vb, qr=qr, kr=kr, o=o, lse=lse, ys=ys, pres=pres, m=m, x1=x1,
             h2=h2, a1=a1, act=act, x2=x2, h3=h3, gl=gl, pe=pe, p_bf=p_bf, pool_saved=pool_saved, ssd_saved=ssd_saved,
             lru_saved=lru_saved, la=la, hh=hh, w=w, pr=pr, dt32=dt32)
    return x3, h_next, r


def _norm_bwd(dh, x, resid, g):
    rs = lax.rsqrt(jnp.mean(x * x, axis=-1, keepdims=True) + EPS)
    xhat = x * rs
    dxn = dh * g
    dx = rs * (dxn - xhat * jnp.mean(dxn * xhat, axis=-1, keepdims=True)) + resid
    return dx, jnp.sum(dh * xhat, axis=0, keepdims=True)


def _gate_bwd(d, g, pre):
    s = _sig(g.astype(f32))
    return d * s, d * pre.astype(f32) * s * (1.0 - s)


def _layer_bwd(dx3, r, ctx, l, cosf, sinf, tok, extra_small):
    rows = dx3.shape[0]
    ts = _tiles(rows)
    tm = ts["tm"]
    nm = lambda s: f"{s}_l{l}"
    u, w, pr = r["u"], r["w"], r["pr"]
    mt = _mixer_tiles(u, r["dt32"])
    g = {}
    full = lambda a: (a, a.shape[1], 0)
    dpe, dgl = ew(nm("ple_bwd"), _gate_bwd, [full(dx3), full(r["gl"]), full(r["pe"])], [(D_MODEL, bf16)] * 2, tm)
    g["w_ple"] = matmul(nm("d_w_ple"), r["p_bf"], dpe, ta=True, outs=(bf16,), deps=[tok] if tok is not None else [])
    g["w_pg"] = matmul(nm("d_w_pg"), r["h3"], dgl, ta=True, outs=(bf16,))
    dx2, g["g_ple"] = matmul(nm("d_h3"), dgl, w["w_pg"], tb=True, epi=_norm_bwd, extras=[(r["x2"], 0), (dx3, 0)],
                             rows=[pr["g_ple"]], row_sums=1)
    da1 = matmul(nm("d_act"), dx2, w["w_ff2"], tb=True, outs=(bf16,),
                 epi=lambda acc, a: (acc * 2.0 * jnp.maximum(a, 0.0),), extras=[(r["a1"], 0)])
    g["w_ff2"] = matmul(nm("d_w_ff2"), r["act"], dx2, ta=True, outs=(bf16,))
    g["w_ff1"] = matmul(nm("d_w_ff1"), r["h2"], da1, ta=True, outs=(bf16,), out_blocks=N_DEV)
    tok = ctx.grads(l, "C", dict(w_ff1=g["w_ff1"], w_ff2=g["w_ff2"], w_ple_gate=g["w_pg"], w_ple=g["w_ple"]))
    dx1, g["g_mlp"] = matmul(nm("d_h2"), da1, w["w_ff1"], tb=True, epi=_norm_bwd, extras=[(r["x1"], 0), (dx2, 0)],
                             rows=[pr["g_mlp"]], row_sums=1, deps=[tok])
    def merge_bwd(dm, *gates_and_pres):
        both = [_gate_bwd(dm, gates_and_pres[n], gates_and_pres[4 + n]) for n in range(4)]
        return tuple(b[0] for b in both) + tuple(b[1] for b in both)

    res = matmul(nm("d_merged"), dx1, w["w_out"], tb=True, outs=(bf16,) * 8, epi=merge_bwd,
                 extras=[(u, D_MODEL * n) for n in range(4)] + [(pre, 0) for pre in r["pres"]])
    dpres, dgates = list(res[:4]), list(res[4:])
    g["w_out"] = matmul(nm("d_w_out"), r["m"], dx1, ta=True, outs=(bf16,))
    dys, g["wb"] = [], []
    for n in range(4):
        g["wb"].append(matmul(nm(f"d_w_branch{n}"), r["ys"][n], dpres[n], ta=True, outs=(bf16,)))
        dys.append(matmul(nm(f"d_y{n}"), dpres[n], w["wb"][n], tb=True, outs=(bf16 if n == 0 else f32,)))
    dqr, dkr_, dv = attn_bwd(r["qr"], r["kr"], r["vb"], dys[0], r["o"], r["lse"], ts["ta"])
    hw = N_HEADS * LANE
    dq, dkn, dkrope = ew(nm("rope_bwd"), rope_bwd, [full(dqr), full(dkr_), full(cosf), full(sinf)],
                         [(hw, bf16), (hw, bf16), (LANE, bf16)], tm)
    g["w_uq"] = matmul(nm("d_w_uq"), r["cqn"], dq, ta=True, outs=(bf16,))
    g["w_uk"] = matmul(nm("d_w_uk"), r["ckvn"], dkn, ta=True, outs=(bf16,))
    g["w_uv"] = matmul(nm("d_w_uv"), r["ckvn"], dv, ta=True, outs=(bf16,))
    dcqn = matmul(nm("d_cqn"), dq, w["w_uq"], tb=True)
    dckvn = matmul(nm("d_ckvn_k"), dkn, w["w_uk"], tb=True)
    dckvn = matmul(nm("d_ckvn_v"), dv, w["w_uv"], tb=True, epi=lambda acc, prev: (acc + prev,), extras=[(dckvn, 0)])
    (g["q_norm"],), (dcq,) = seq_bwd(nm("rms_q_bwd"), f_rms, [pr["q_norm"]], [mt["cq"]], [True], [], [dcqn], [bf16], tm)
    (g["kv_norm"],), (dckv,) = seq_bwd(nm("rms_kv_bwd"), f_rms, [pr["kv_norm"]], [mt["ckv"]], [True], [], [dckvn], [bf16], tm)
    g["pool"], (dpool,) = seq_bwd(nm("pool_bwd"), f_pool, pr["pool"], [mt["pool"]], [True], r["pool_saved"], [dys[1]],
                                  [bf16], ts["tp"])
    g["ssd"], (dz, dxbc, ddt) = seq_bwd(nm("ssd_bwd"), f_ssd, pr["ssd"], [mt["z"], mt["xbc"], mt["dt"]], [True] * 3,
                                        r["ssd_saved"], [dys[2]], [bf16] * 3, SSD_CHUNK)
    da, du, dlg = scan_bwd(r["la"], r["hh"], mt["lg"], dys[3], ts["ts"])
    g["lru"], (dlx,) = seq_bwd(nm("lru_pre_bwd"), f_lru_pre, pr["lru"], [mt["lx"]], [True], r["lru_saved"], [da, du],
                               [bf16], ts["tl"])
    dk = _head_unpad_cols(g["w_uk"], QK_NOPE)
    dv_ = _head_unpad_cols(g["w_uv"], V_HEAD)
    wb0 = g["wb"][0].reshape(N_HEADS, LANE, D_MODEL)[:, :V_HEAD].reshape(MIX, D_MODEL)
    ssd, lru, pool = g["ssd"], g["lru"], g["pool"]
    tok = ctx.grads(l, "B", dict(
        w_uq=_head_unpad_cols(g["w_uq"], QK_NOPE + QK_ROPE).reshape(Q_LORA, -1),
        w_ukv=jnp.concatenate([dk, dv_], axis=2).reshape(KV_LORA, -1), ssd_conv_w=ssd[0], lru_conv_w=lru[0],
        w_branch=jnp.stack([wb0, g["wb"][1], g["wb"][2], g["wb"][3]]), w_out=g["w_out"]))
    du_p = jnp.concatenate(dgates + [dpool, dz, dlg, dlx, dxbc, dcq, dkrope, dckv, ddt,
                                     jnp.zeros((rows, U_COLS - U_DT[0] - LANE), bf16)], axis=1)
    small = dict(
        q_norm=g["q_norm"][0], kv_norm=g["kv_norm"][0],
        w_pool=pool[0].reshape(4, LANE, LANE), pool_scale=pool[1][0],
        ssd_conv_b=ssd[1][0], ssd_dt_bias=ssd[2][0, :8], ssd_a_log=ssd[3][0, :8], ssd_d=ssd[4][0, :8], ssd_norm=ssd[5][0],
        lru_conv_b=lru[1][0], lru_w_a=_block_diag_inv(lru[2]), lru_b_a=lru[3][0], lru_w_i=_block_diag_inv(lru[4]),
        lru_b_i=lru[5][0], lru_lambda=lru[6][0], g_mlp=g["g_mlp"][0], g_ple=g["g_ple"][0])
    tok_small = ctx.small(f"l{l}", [(n, l, small[n]) for n in SMALL if n in small] + extra_small)
    g_w_in = matmul(nm("d_w_in"), r["h"], du_p, ta=True, outs=(bf16,), deps=[tok, tok_small])
    tok = ctx.grads(l, "A", dict(w_in=_w_in_blocks(g_w_in)))
    dx, g_mix = matmul(nm("d_h"), du_p, w["w_in"], tb=True, epi=_norm_bwd, extras=[(r["x"], 0), (dx1, 0)],
                       rows=[pr["g_mix"]], row_sums=1, deps=[tok])
    return dx, tok, ("g_mix", l, g_mix[0])


def _rope_tables(positions):
    inv = 1.0 / (ROPE_THETA ** (jnp.arange(0, QK_ROPE, 2, dtype=f32) / QK_ROPE))
    ang = positions.astype(f32)[:, None] * inv
    cos, sin = jnp.cos(ang), jnp.sin(ang)
    rows = positions.shape[0]
    pad = jnp.zeros((rows, LANE - KR_LANE - QK_ROPE), f32)
    cosf = jnp.concatenate([jnp.ones((rows, KR_LANE), f32), cos, cos, pad], axis=1)
    sinf = jnp.concatenate([jnp.zeros((rows, KR_LANE), f32), -sin, sin, pad], axis=1)
    return cosf, sinf


WEIGHTS = ['g_mix', 'w_in', 'q_norm', 'w_uq', 'kv_norm', 'w_ukv', 'w_pool', 'pool_scale', 'ssd_conv_w', 'ssd_conv_b',
           'ssd_dt_bias', 'ssd_a_log', 'ssd_d', 'ssd_norm', 'lru_conv_w', 'lru_conv_b', 'lru_w_a', 'lru_b_a', 'lru_w_i',
           'lru_b_i', 'lru_lambda', 'w_branch', 'w_out', 'g_mlp', 'w_ff1', 'w_ff2', 'g_ple', 'w_ple_gate', 'w_ple', 'g_final']
SHARDED = dict(w_in=2, w_uq=2, w_ukv=2, ssd_conv_w=2, lru_conv_w=2, w_branch=3, w_out=1, w_ff1=2, w_ff2=1,
               w_ple_gate=1, w_ple=2)
F32_PAYLOAD = ("ssd_conv_w", "lru_conv_w")
DEPTH = 2


SMALL = [n for n in WEIGHTS if n not in SHARDED and n != "g_final"]


def local_step(x, p, positions, tgt, sp, ctx):
    cosf, sinf = _rope_tables(positions)
    res, h = [], None
    for l in range(DEPTH):
        g_next = sp["g_mix"][l + 1][None, :] if l + 1 < DEPTH else None
        x, h, r = _layer_fwd(x, h, p[l].astype(bf16), ctx, l, _layer_params(sp, l), g_next, cosf, sinf)
        res.append(r)
    loss8, dx, dgf = loss_head(x, tgt, sp["g_final"][None, :], _pick(x.shape[0], 512))
    tok = None
    pending = ("g_final", None, dgf[0])
    for l in reversed(range(DEPTH)):
        dx, tok, pending = _layer_bwd(dx, res[l], ctx, l, cosf, sinf, tok, [pending])
    ctx.small("last", [pending])
    return loss8[0, 0], dx


def _payload(name, w):
    return w if name in F32_PAYLOAD else w.astype(bf16)


def _blocks(name, g):
    ax = SHARDED[name] - 1
    shape = list(g.shape)
    shape[ax:ax + 1] = [N_DEV, shape[ax] // N_DEV]
    return _payload(name, jnp.moveaxis(g.reshape(shape), ax, 0))


def _assemble(name, shards):
    ax = SHARDED[name] - 1
    shape = list(shards.shape[1:])
    shape[ax] *= N_DEV
    return jnp.moveaxis(shards, 0, ax).reshape(shape)


class _Exchanges:
    def __init__(self, wts):
        self.wts = wts
        self.ag, self.rs, self.sm = {}, {}, {}
        tok = None
        for l in range(DEPTH):
            for grp, names in GROUPS.items():
                h = exchange_start(f"ag_start_{grp}{l}", [_payload(n, wts[n][l]) for n in names], True,
                                   deps=[] if tok is None else [tok])
                tok = h["token"]
                self.ag[(l, grp)] = h
        self.all_started = tok

    def weights(self, l, grp, after):
        afters = list(after) if isinstance(after, (list, tuple)) else [after]
        if (l, grp) == (0, "A"):
            afters.append(self.all_started)
        got = exchange_wait(f"ag_wait_{grp}{l}", self.ag[(l, grp)], afters)
        out = {}
        for n, a in zip(GROUPS[grp], got):
            out[n] = a if n == "w_in" else _assemble(n, a)
        return out

    def grads(self, l, grp, g):
        cut = lambda n: g[n].ndim == self.wts[n].ndim
        h = exchange_start(f"rs_start_{grp}{l}", [g[n] if cut(n) else _blocks(n, g[n]) for n in GROUPS[grp]], False)
        self.rs[(l, grp)] = h
        return h["token"]

    def small(self, tag, entries):
        entries = sorted(entries, key=lambda e: e[2].size % LANE != 0)
        flat = jnp.concatenate([a.reshape(-1) for _, _, a in entries])
        flat = jnp.pad(flat, (0, (-flat.shape[0]) % (8 * LANE))).reshape(-1, LANE)
        h = exchange_start(f"small_start_{tag}", [flat], True)
        self.sm[tag] = (h, [(n, l, a.shape) for n, l, a in entries])
        return h["token"]

    def collect(self, groups, after):
        parts = {}
        for grp in groups:
            for l in reversed(range(DEPTH)):
                got = exchange_wait(f"rs_wait_{grp}{l}", self.rs[(l, grp)], [after])
                for n, a in zip(GROUPS[grp], got):
                    parts.setdefault(n, [None] * DEPTH)[l] = a
        return parts

    def collect_small(self, after):
        gots, where, parts = [], {}, {}
        for tag, (h, layout) in self.sm.items():
            (got,) = exchange_wait(f"small_wait_{tag}", h, [after])
            got = got.reshape(N_DEV, -1)
            off = 0
            for n, l, shape in layout:
                size = 1
                for d in shape:
                    size *= d
                if len(shape) == 1 and size % LANE == 0 and off % LANE == 0:
                    where.setdefault(n, [None] * (1 if l is None else DEPTH))[l or 0] = (len(gots), off)
                else:
                    part = got[:, off:off + size].reshape((N_DEV,) + tuple(shape))
                    if l is None:
                        parts[n] = [part]
                    else:
                        parts.setdefault(n, [None] * DEPTH)[l] = part
                off += size
            gots.append(got)
        return gots, where, parts


def kernel(x, p, positions, g_mix, w_in, q_norm, w_uq, kv_norm, w_ukv, w_pool, pool_scale, ssd_conv_w, ssd_conv_b,
           ssd_dt_bias, ssd_a_log, ssd_d, ssd_norm, lru_conv_w, lru_conv_b, lru_w_a, lru_b_a, lru_w_i, lru_b_i,
           lru_lambda, w_branch, w_out, g_mlp, w_ff1, w_ff2, g_ple, w_ple_gate, w_ple, g_final, loss_target, m_g_mix,
           m_w_in, m_q_norm, m_w_uq, m_kv_norm, m_w_ukv, m_w_pool, m_pool_scale, m_ssd_conv_w, m_ssd_conv_b,
           m_ssd_dt_bias, m_ssd_a_log, m_ssd_d, m_ssd_norm, m_lru_conv_w, m_lru_conv_b, m_lru_w_a, m_lru_b_a,
           m_lru_w_i, m_lru_b_i, m_lru_lambda, m_w_branch, m_w_out, m_g_mlp, m_w_ff1, m_w_ff2, m_g_ple, m_w_ple_gate,
           m_w_ple, m_g_final, v_g_mix, v_w_in, v_q_norm, v_w_uq, v_kv_norm, v_w_ukv, v_w_pool, v_pool_scale,
           v_ssd_conv_w, v_ssd_conv_b, v_ssd_dt_bias, v_ssd_a_log, v_ssd_d, v_ssd_norm, v_lru_conv_w, v_lru_conv_b,
           v_lru_w_a, v_lru_b_a, v_lru_w_i, v_lru_b_i, v_lru_lambda, v_w_branch, v_w_out, v_g_mlp, v_w_ff1, v_w_ff2,
           v_g_ple, v_w_ple_gate, v_w_ple, v_g_final):
    given = dict(locals())
    wts = {n: given[n] for n in WEIGHTS}
    ctx = _Exchanges(wts)
    loss, grad_x = local_step(x[0], p[:, 0], positions[0], loss_target[0], wts, ctx)

    def update(parts):
        out = {}
        for n, eight in parts.items():
            step = adamw_columns if n == "w_in" else adamw
            out[n] = step(f"adamw_{n}", eight, wts[n], given["m_" + n], given["v_" + n])
        return out

    outs = update(ctx.collect(("C", "B"), grad_x))
    late = outs["w_ff1"][1]
    outs.update(update(ctx.collect(("A",), late)))
    gots, where, parts = ctx.collect_small(late)
    outs.update(update(parts))
    names = sorted(where)
    rows = lambda a: a[None] if a.ndim == 1 else a
    res = adamw_packed("adamw_vectors", gots, [where[n] for n in names], [rows(wts[n]) for n in names],
                       [rows(given["m_" + n]) for n in names], [rows(given["v_" + n]) for n in names])
    for n, four in zip(names, res):
        outs[n] = [a[0] for a in four] if wts[n].ndim == 1 else four
    loss = lax.psum(loss, AXES)
    return (loss, grad_x[None], *[outs[n][0] for n in WEIGHTS], *[outs[n][1] for n in WEIGHTS],
            *[outs[n][2] for n in WEIGHTS], *[outs[n][3] for n in WEIGHTS])
```

```python
import functools

import jax
import jax.numpy as jnp
from jax import lax
from jax.experimental import pallas as pl
from jax.experimental.pallas import tpu as pltpu

f32 = jnp.float32
bf16 = jnp.bfloat16

D_MODEL = 1024
MIX = 512
N_HEADS = 8
QK_NOPE, QK_ROPE, V_HEAD = 64, 32, 64
Q_LORA, KV_LORA = 384, 256
ROPE_THETA = 10000.0
POOL_WINDOWS = (2, 4, 8, 16)
SSD_CHUNK = 128
SSD_CHUNKS_PER_TILE = 2
SSD_XBC = 768
CONV_W = 4
LRU_C = 8.0
EPS = 1e-6
IN_COLS = 7592
ADAM_LR, ADAM_B1, ADAM_B2, ADAM_EPS, ADAM_WD, ADAM_STEP = 0.001, 0.9, 0.999, 1e-08, 0.01, 10

LANE = 128
HALO = 8
POOL_HALO = 16
VMEM_LIMIT = 56 * 1024 * 1024
MATMUL_MAX_K_TILE = 4096
MATMUL_ACC_PASS_WEIGHT = 0.3
MATMUL_VMEM_BUDGET = 40 * 1024 * 1024
N_DEV = 8
AXES = ("x", "y", "c")

U_COLS = 8192
U_GATES, U_POOL, U_Z, U_LG, U_LX, U_XBC, U_CQ, U_KR, U_CKV, U_DT = (
    (0, 4096), (4096, 512), (4608, 512), (5120, 512), (5632, 512), (6144, 768),
    (6912, 384), (7296, 128), (7424, 256), (7680, 128))
KR_LANE = 64
U_DTYPE = bf16


def _cp(sem):
    return pltpu.CompilerParams(dimension_semantics=sem, vmem_limit_bytes=VMEM_LIMIT)


def _pick(dim, pref):
    if dim <= pref:
        return dim
    t = pref
    while t >= LANE:
        if dim % t == 0:
            return t
        t -= LANE
    t = pref
    while dim % t:
        t -= 8
    return t


@functools.partial(jax.custom_vjp, nondiff_argnums=(1,))
def shift_down(x, k):
    row = lax.broadcasted_iota(jnp.int32, x.shape, 0)
    return jnp.where(row >= k, pltpu.roll(x, k, 0), 0.0)


def _shift_down_fwd(x, k):
    return shift_down(x, k), None


def _shift_down_bwd(k, _, g):
    r = g.shape[0]
    row = lax.broadcasted_iota(jnp.int32, g.shape, 0)
    return (jnp.where(row < r - k, pltpu.roll(g, r - k, 0), 0.0),)


shift_down.defvjp(_shift_down_fwd, _shift_down_bwd)


def _tile_spec(tm, width, cb, n=None):
    if n is None:
        return pl.BlockSpec((tm, width), lambda i: (i, cb))
    return pl.BlockSpec((tm, width), lambda i: (n - 1 - i, cb))


def _const_spec(shape):
    nd = len(shape)
    return pl.BlockSpec(shape, lambda i: (0,) * nd)


def seq_fwd(name, f, params, tiles, carries, outs, tm):
    rows = tiles[0][0].shape[0]
    n = rows // tm
    np_, nt, no, nc = len(params), len(tiles), len(outs), len(carries)

    def body(*refs):
        p_refs = refs[:np_]
        t_refs = refs[np_:np_ + nt]
        o_refs = refs[np_ + nt:np_ + nt + no]
        s_refs = refs[np_ + nt + no:np_ + nt + no + nc]
        c_refs = refs[np_ + nt + no + nc:]
        i = pl.program_id(0)

        @pl.when(i == 0)
        def _():
            for c in c_refs:
                c[...] = jnp.zeros_like(c)

        cvals = [c[...] for c in c_refs]
        for s, c in zip(s_refs, cvals):
            s[0] = c
        o, newc = f(i, [r[...] for r in p_refs], cvals, [r[...].astype(f32) for r in t_refs])
        for r, v in zip(o_refs, o):
            r[...] = v.astype(r.dtype)
        for r, v in zip(c_refs, newc):
            r[...] = v

    in_specs = [_const_spec(p.shape) for p in params] + [_tile_spec(tm, w, cb) for (_, w, cb) in tiles]
    out_specs = [_tile_spec(tm, w, 0) for (w, _) in outs]
    out_specs += [pl.BlockSpec((1,) + tuple(c), lambda i, nd=len(c): (i,) + (0,) * nd) for c in carries]
    out_shape = [jax.ShapeDtypeStruct((rows, w), dt) for (w, dt) in outs]
    out_shape += [jax.ShapeDtypeStruct((n,) + tuple(c), f32) for c in carries]
    res = pl.pallas_call(
        body, name=name, grid=(n,), in_specs=in_specs, out_specs=out_specs, out_shape=out_shape,
        scratch_shapes=[pltpu.VMEM(tuple(c), f32) for c in carries],
        compiler_params=_cp(("arbitrary",)),
    )(*params, *[t[0] for t in tiles])
    return list(res[:no]), list(res[no:])


def seq_bwd(name, f, params, tiles, diff, saved, douts, gdtypes, tm):
    rows = tiles[0][0].shape[0]
    n = rows // tm
    np_, nt, nc, nd = len(params), len(tiles), len(saved), len(douts)
    didx = [k for k, d in enumerate(diff) if d]
    ng = len(didx)

    def body(*refs):
        p_refs = refs[:np_]
        t_refs = refs[np_:np_ + nt]
        s_refs = refs[np_ + nt:np_ + nt + nc]
        d_refs = refs[np_ + nt + nc:np_ + nt + nc + nd]
        pos = np_ + nt + nc + nd
        dp_refs = refs[pos:pos + np_]
        dt_refs = refs[pos + np_:pos + np_ + ng]
        dc_refs = refs[pos + np_ + ng:]
        i = pl.program_id(0)
        step = n - 1 - i

        @pl.when(i == 0)
        def _():
            for r in dp_refs:
                r[...] = jnp.zeros_like(r)
            for r in dc_refs:
                r[...] = jnp.zeros_like(r)

        pvals = [r[...] for r in p_refs]
        cvals = [r[0] for r in s_refs]
        xvals = [r[...].astype(f32) for r in t_refs]

        def fn(p, c, xd):
            x = list(xvals)
            for k, v in zip(didx, xd):
                x[k] = v
            return f(step, p, c, x)

        _, vjp = jax.vjp(fn, pvals, cvals, [xvals[k] for k in didx])
        dp, dc, dx = vjp(([r[...].astype(f32) for r in d_refs], [r[...] for r in dc_refs]))
        for r, v in zip(dp_refs, dp):
            r[...] += v
        for r, v in zip(dc_refs, dc):
            r[...] = v
        for r, v in zip(dt_refs, dx):
            r[...] = v.astype(r.dtype)

    in_specs = [_const_spec(p.shape) for p in params] + [_tile_spec(tm, w, cb, n) for (_, w, cb) in tiles]
    in_specs += [pl.BlockSpec((1,) + tuple(s.shape[1:]), lambda i, nd_=s.ndim - 1: (n - 1 - i,) + (0,) * nd_) for s in saved]
    in_specs += [_tile_spec(tm, d.shape[1], 0, n) for d in douts]
    args = list(params) + [t[0] for t in tiles] + list(saved) + list(douts)
    out_specs = [_const_spec(p.shape) for p in params] + [_tile_spec(tm, tiles[k][1], 0, n) for k in didx]
    out_shape = [jax.ShapeDtypeStruct(p.shape, f32) for p in params]
    out_shape += [jax.ShapeDtypeStruct((rows, tiles[k][1]), dt) for k, dt in zip(didx, gdtypes)]
    res = pl.pallas_call(
        body, name=name, grid=(n,), in_specs=in_specs, out_specs=out_specs, out_shape=out_shape,
        scratch_shapes=[pltpu.VMEM(tuple(s.shape[1:]), f32) for s in saved],
        compiler_params=_cp(("arbitrary",)),
    )(*args)
    return list(res[:np_]), list(res[np_:])


def _halvings(dim, lo, hi):
    t, out = _pick(dim, hi), []
    while t >= min(lo, dim) and dim % t == 0:
        out.append(t)
        if t % 2 or (t // 2) % 8:
            break
        t //= 2
    return out


def _matmul_tiles(m, n, k, a_item, b_item, per_out, max_tn=1024, whole_rows=False):
    def vmem_bytes(tm, tn, tk):
        acc = 4 if k // tk > 1 else 0
        return 2 * (tm * tk * a_item + tk * tn * b_item + tm * tn * per_out) + tm * tn * acc

    def traffic(tm, tn, tk):
        nk = k // tk
        return (m * k * a_item * (1 if nk == 1 else n // tn) + k * n * b_item * (m // tm)
                + (nk - 1) * m * n * 8 * MATMUL_ACC_PASS_WEIGHT)

    cands = [(traffic(tm, tn, tk), -tm * tn, tm, tn, tk)
             for tk in _halvings(k, 512, MATMUL_MAX_K_TILE) for tm in _halvings(m, 256, 4096)
             for tn in ([n] if whole_rows else _halvings(n, 512, min(1024, max_tn)))
             if vmem_bytes(tm, tn, tk) <= MATMUL_VMEM_BUDGET]
    return min(cands)[2:]


def matmul(name, a, b, *, ta=False, tb=False, outs=(f32,), epi=None, extras=(), rows=(), row_sums=0, deps=(),
           out_blocks=0):
    m, k = (a.shape[1], a.shape[0]) if ta else a.shape
    n = b.shape[0] if tb else b.shape[1]
    per_out = sum(jnp.dtype(dt).itemsize for dt in outs) + sum(e[0].dtype.itemsize for e in extras)
    whole_rows = bool(rows) or row_sums > 0
    tm, tn, tk = _matmul_tiles(m, n, k, a.dtype.itemsize, b.dtype.itemsize, per_out,
                               n // out_blocks if out_blocks else n, whole_rows)
    nk = k // tk
    ne, nr, nd, no = len(extras), len(rows), len(deps), len(outs)
    dims = (((0 if ta else 1,), (1 if tb else 0,)), ((), ()))

    def body(*refs):
        a_ref, b_ref = refs[0], refs[1]
        e_refs = refs[2:2 + ne]
        r_refs = refs[2 + ne:2 + ne + nr]
        o_refs = refs[2 + ne + nr + nd:2 + ne + nr + nd + no]
        s_refs = refs[2 + ne + nr + nd + no:2 + ne + nr + nd + no + row_sums]
        i, kk = pl.program_id(0), pl.program_id(2)
        part = lax.dot_general(a_ref[...].astype(bf16), b_ref[...].astype(bf16), dims, preferred_element_type=f32)

        def finish(total):
            res = (total,) if epi is None else epi(total, *[e[...] for e in e_refs], *[r[...] for r in r_refs])
            for r, v in zip(o_refs, res[:no]):
                r[...] = v.astype(r.dtype)
            for r, v in zip(s_refs, res[no:]):
                v8 = jnp.broadcast_to(v, r.shape)

                @pl.when(i == 0)
                def _(r=r, v8=v8):
                    r[...] = v8

                @pl.when(i > 0)
                def _(r=r, v8=v8):
                    r[...] += v8

        if nk == 1:
            finish(part)
            return
        acc = refs[-1]

        @pl.when(kk == 0)
        def _():
            acc[...] = part

        @pl.when(jnp.logical_and(kk > 0, kk < nk - 1))
        def _():
            acc[...] += part

        @pl.when(kk == nk - 1)
        def _():
            finish(acc[...] + part)

    a_spec = pl.BlockSpec((tk, tm), lambda i, j, q: (q, i)) if ta else pl.BlockSpec((tm, tk), lambda i, j, q: (i, q))
    b_spec = pl.BlockSpec((tn, tk), lambda i, j, q: (j, q)) if tb else pl.BlockSpec((tk, tn), lambda i, j, q: (q, j))
    def e_spec(e):
        if len(e) == 3:
            return pl.BlockSpec((tm, e[2]), lambda i, j, q, cb=e[1] // e[2]: (i, cb))
        assert e[1] % tn == 0
        return pl.BlockSpec((tm, tn), lambda i, j, q, off=e[1] // tn: (i, off + j))

    e_specs = [e_spec(e) for e in extras]
    r_specs = [pl.BlockSpec((1, tn), lambda i, j, q: (0, j)) for _ in rows]
    if out_blocks:
        per = n // out_blocks // tn
        out_spec = pl.BlockSpec((None, tm, tn), lambda i, j, q: (j // per, i, j % per))
        out_dims = (out_blocks, m, n // out_blocks)
    else:
        out_spec = pl.BlockSpec((tm, tn), lambda i, j, q: (i, j))
        out_dims = (m, n)
    res = pl.pallas_call(
        body, name=name, grid=(m // tm, n // tn, nk),
        in_specs=[a_spec, b_spec] + e_specs + r_specs + [pl.BlockSpec(memory_space=pl.ANY) for _ in deps],
        out_specs=[out_spec for _ in outs] + [pl.BlockSpec((8, tn), lambda i, j, q: (0, j))] * row_sums,
        out_shape=[jax.ShapeDtypeStruct(out_dims, dt) for dt in outs] + [jax.ShapeDtypeStruct((8, n), f32)] * row_sums,
        scratch_shapes=[pltpu.VMEM((tm, tn), f32)] if nk > 1 else [],
        compiler_params=_cp(("arbitrary" if row_sums else "parallel", "parallel", "arbitrary")),
    )(a, b, *[e[0] for e in extras], *rows, *deps)
    return res[0] if len(res) == 1 else tuple(res)


def merge_fwd(name, ys, wbs, u):
    rows, n_out = ys[0].shape[0], wbs[0].shape[1]
    nb = len(ys)
    tm, tn = _pick(rows, 512), _pick(n_out, 512)

    def body(*refs):
        y_refs, w_refs, g_refs = refs[:nb], refs[nb:2 * nb], refs[2 * nb:3 * nb]
        m_ref, p_refs = refs[3 * nb], refs[3 * nb + 1:]
        total = None
        for y_ref, w_ref, g_ref, p_ref in zip(y_refs, w_refs, g_refs, p_refs):
            pre = jnp.dot(y_ref[...], w_ref[...], preferred_element_type=f32)
            p_ref[...] = pre.astype(p_ref.dtype)
            term = jax.nn.sigmoid(g_ref[...].astype(f32)) * pre
            total = term if total is None else total + term
        m_ref[...] = total.astype(m_ref.dtype)

    in_specs = [pl.BlockSpec((tm, y.shape[1]), lambda i, j: (i, 0)) for y in ys]
    in_specs += [pl.BlockSpec((w.shape[0], tn), lambda i, j: (0, j)) for w in wbs]
    in_specs += [pl.BlockSpec((tm, tn), lambda i, j, off=n * (n_out // tn): (i, off + j)) for n in range(nb)]
    out_spec = pl.BlockSpec((tm, tn), lambda i, j: (i, j))
    res = pl.pallas_call(
        body, name=name, grid=(rows // tm, n_out // tn), in_specs=in_specs, out_specs=[out_spec] * (nb + 1),
        out_shape=[jax.ShapeDtypeStruct((rows, n_out), bf16)] * (nb + 1),
        compiler_params=_cp(("parallel", "parallel")),
    )(*ys, *wbs, *([u] * nb))
    return res[0], list(res[1:])


ATT_SCALE = (QK_NOPE + QK_ROPE) ** -0.5
LN2 = 0.6931471805599453
ATT_C = ATT_SCALE / LN2
NT = (((1,), (1,)), ((), ()))
TN = (((0,), (0,)), ((), ()))


def _causal(tq, tk):
    return lax.broadcasted_iota(jnp.int32, (tq, tk), 0) >= lax.broadcasted_iota(jnp.int32, (tq, tk), 1)


def _tri_pairs(n, by_column):
    if by_column:
        pairs = [(i, j) for j in range(n) for i in range(j, n)]
    else:
        pairs = [(i, j) for i in range(n) for j in range(i + 1)]
    return (jnp.asarray([a for a, _ in pairs], jnp.int32), jnp.asarray([b for _, b in pairs], jnp.int32))


FWD_HEADS_PER_STEP = 8
HEADS_PER_STEP = 4
HEAD_PAIR = HEADS_PER_STEP * LANE


def attn_fwd(q, k, v, t):
    rows = q.shape[0]
    n = rows // t
    it, jt = _tri_pairs(n, False)

    def body(it_ref, jt_ref, q_ref, k_ref, v_ref, o_ref, lse_ref, m_s, l_s, acc_s):
        s_id = pl.program_id(1)
        i, j = it_ref[s_id], jt_ref[s_id]

        @pl.when(j == 0)
        def _():
            m_s[...] = jnp.full_like(m_s, -jnp.inf)
            l_s[...] = jnp.zeros_like(l_s)
            acc_s[...] = jnp.zeros_like(acc_s)

        def step(diag):
            for hh in range(FWD_HEADS_PER_STEP):
                sl = slice(LANE * hh, LANE * (hh + 1))
                s = lax.dot_general(q_ref[:, sl], k_ref[:, sl], NT, preferred_element_type=f32)
                if diag:
                    s = jnp.where(_causal(t, t), s, -jnp.inf)
                m_prev = m_s[:, sl]
                m_new = jnp.maximum(m_prev, jnp.max(s, axis=1, keepdims=True))
                alpha = jnp.exp2(m_prev - m_new)
                p = jnp.exp2(s - m_new[:, :1])
                l_s[:, sl] = alpha * l_s[:, sl] + jnp.sum(p, axis=1, keepdims=True)
                acc_s[:, sl] = alpha * acc_s[:, sl] + jnp.dot(p.astype(bf16), v_ref[:, sl], preferred_element_type=f32)
                m_s[:, sl] = m_new

        pl.when(j < i)(lambda: step(False))

        @pl.when(j == i)
        def _():
            step(True)
            o_ref[...] = (acc_s[...] / l_s[...]).astype(o_ref.dtype)
            lse_ref[...] = m_s[...] + jnp.log2(l_s[...])

    width = FWD_HEADS_PER_STEP * LANE
    qs = pl.BlockSpec((t, width), lambda h, s, it_, jt_: (it_[s], h))
    ks = pl.BlockSpec((t, width), lambda h, s, it_, jt_: (jt_[s], h))
    hw = N_HEADS * LANE
    return pl.pallas_call(
        body, name="attn_fwd",
        grid_spec=pltpu.PrefetchScalarGridSpec(
            num_scalar_prefetch=2, grid=(hw // width, it.shape[0]), in_specs=[qs, ks, ks], out_specs=[qs, qs],
            scratch_shapes=[pltpu.VMEM((t, width), f32)] * 3),
        out_shape=[jax.ShapeDtypeStruct((rows, hw), bf16), jax.ShapeDtypeStruct((rows, hw), f32)],
        compiler_params=_cp(("parallel", "arbitrary")),
    )(it, jt, q, k, v)


def attn_bwd(q, k, v, do, o, lse, t):
    rows = q.shape[0]
    n = rows // t
    it, jt = _tri_pairs(n, True)

    def body(it_ref, jt_ref, q_ref, k_ref, v_ref, do_ref, o_ref, lse_ref, dq_ref, dk_ref, dv_ref, dk_s, dv_s):
        s_id = pl.program_id(1)
        i, j = it_ref[s_id], jt_ref[s_id]

        @pl.when(s_id == 0)
        def _():
            dq_ref[...] = jnp.zeros_like(dq_ref)

        @pl.when(i == j)
        def _():
            dk_s[...] = jnp.zeros_like(dk_s)
            dv_s[...] = jnp.zeros_like(dv_s)

        q_rows = pl.ds(pl.multiple_of(i * t, t), t)

        def step(diag):
            for hh in range(HEADS_PER_STEP):
                sl = slice(LANE * hh, LANE * (hh + 1))
                qh, kh, vh, doh = q_ref[:, sl], k_ref[:, sl], v_ref[:, sl], do_ref[:, sl]
                s = lax.dot_general(qh, kh, NT, preferred_element_type=f32)
                p = jnp.exp2(s - lse_ref[:, sl][:, :1])
                if diag:
                    p = jnp.where(_causal(t, t), p, 0.0)
                dp = lax.dot_general(doh, vh, NT, preferred_element_type=f32)
                delta = jnp.sum(doh.astype(f32) * o_ref[:, sl].astype(f32), axis=1, keepdims=True)
                ds = (p * (dp - delta) * LN2).astype(bf16)
                dv_s[:, sl] += lax.dot_general(p.astype(bf16), doh, TN, preferred_element_type=f32)
                dk_s[:, sl] += lax.dot_general(ds, qh, TN, preferred_element_type=f32)
                dq_ref[q_rows, sl] += jnp.dot(ds, kh, preferred_element_type=f32)

        pl.when(i > j)(lambda: step(False))
        pl.when(i == j)(lambda: step(True))

        @pl.when(i == n - 1)
        def _():
            dk_ref[...] = dk_s[...]
            dv_ref[...] = dv_s[...]

    qs = pl.BlockSpec((t, HEAD_PAIR), lambda h, s, it_, jt_: (it_[s], h))
    ks = pl.BlockSpec((t, HEAD_PAIR), lambda h, s, it_, jt_: (jt_[s], h))
    dqs = pl.BlockSpec((rows, HEAD_PAIR), lambda h, s, it_, jt_: (0, h))
    hw = N_HEADS * LANE
    return pl.pallas_call(
        body, name="attn_bwd",
        grid_spec=pltpu.PrefetchScalarGridSpec(
            num_scalar_prefetch=2, grid=(hw // HEAD_PAIR, it.shape[0]), in_specs=[qs, ks, ks, qs, qs, qs],
            out_specs=[dqs, ks, ks], scratch_shapes=[pltpu.VMEM((t, HEAD_PAIR), f32)] * 2),
        out_shape=[jax.ShapeDtypeStruct((rows, hw), f32)] * 3,
        compiler_params=_cp(("parallel", "arbitrary")),
    )(it, jt, q, k, v, do, o, lse)


def _steps(tm):
    k, out = 1, []
    while k < tm:
        out.append(k)
        k *= 2
    return out


def _gelu_gate(h, g):
    return h * jax.nn.gelu(g)


def scan_fwd(a, u, gate, tm):
    rows, ch = a.shape
    n = rows // tm

    def body(a_ref, u_ref, gt_ref, h_ref, y_ref, h_s):
        @pl.when(pl.program_id(0) == 0)
        def _():
            h_s[...] = jnp.zeros_like(h_s)

        av, bv = a_ref[...], u_ref[...]
        row = lax.broadcasted_iota(jnp.int32, av.shape, 0)
        for k in _steps(tm):
            a_sh = jnp.where(row >= k, pltpu.roll(av, k, 0), 1.0)
            b_sh = jnp.where(row >= k, pltpu.roll(bv, k, 0), 0.0)
            bv = av * b_sh + bv
            av = av * a_sh
        h = bv + av * h_s[HALO - 1:HALO, :]
        h_ref[...] = h
        y_ref[...] = _gelu_gate(h, gt_ref[...].astype(f32)).astype(y_ref.dtype)
        h_s[...] = h[tm - HALO:, :]

    spec = pl.BlockSpec((tm, ch), lambda i: (i, 0))
    gt_spec = pl.BlockSpec((tm, gate[1]), lambda i: (i, gate[2]))
    return pl.pallas_call(
        body, name="lru_scan_fwd", grid=(n,), in_specs=[spec, spec, gt_spec], out_specs=[spec, spec],
        out_shape=[jax.ShapeDtypeStruct((rows, ch), f32), jax.ShapeDtypeStruct((rows, ch), bf16)],
        scratch_shapes=[pltpu.VMEM((HALO, ch), f32)], compiler_params=_cp(("arbitrary",)),
    )(a, u, gate[0])


def scan_bwd(a, h, gate, dy, tm):
    rows, ch = a.shape
    n = rows // tm
    per = tm // HALO

    def body(a_ref, h_ref, hp_ref, gt_ref, dy_ref, da_ref, du_ref, dg_ref, g_s, a_s):
        i = pl.program_id(0)
        step = n - 1 - i

        @pl.when(i == 0)
        def _():
            g_s[...] = jnp.zeros_like(g_s)
            a_s[...] = jnp.zeros_like(a_s)

        _, vjp = jax.vjp(_gelu_gate, h_ref[...], gt_ref[...].astype(f32))
        dh, dgate = vjp(dy_ref[...].astype(f32))
        dg_ref[...] = dgate.astype(dg_ref.dtype)
        a0 = a_ref[...]
        row = lax.broadcasted_iota(jnp.int32, a0.shape, 0)
        av = jnp.where(row < tm - 1, pltpu.roll(a0, tm - 1, 0), a_s[0:1, :])
        bv = dh
        for k in _steps(tm):
            a_sh = jnp.where(row < tm - k, pltpu.roll(av, tm - k, 0), 1.0)
            b_sh = jnp.where(row < tm - k, pltpu.roll(bv, tm - k, 0), 0.0)
            bv = bv + av * b_sh
            av = av * a_sh
        g = bv + av * g_s[0:1, :]
        h_last = jnp.where(step > 0, hp_ref[HALO - 1:HALO, :], 0.0)
        h_prev = jnp.where(row >= 1, pltpu.roll(h_ref[...], 1, 0), h_last)
        du_ref[...] = g
        da_ref[...] = g * h_prev
        g_s[...] = g[0:HALO, :]
        a_s[...] = a0[0:HALO, :]

    spec = pl.BlockSpec((tm, ch), lambda i: (n - 1 - i, 0))
    hp_spec = pl.BlockSpec((HALO, ch), lambda i: (jnp.maximum((n - 1 - i) * per - 1, 0), 0))
    gt_spec = pl.BlockSpec((tm, gate[1]), lambda i: (n - 1 - i, gate[2]))
    return pl.pallas_call(
        body, name="lru_scan_bwd", grid=(n,), in_specs=[spec, spec, hp_spec, gt_spec, spec], out_specs=[spec, spec, spec],
        out_shape=[jax.ShapeDtypeStruct((rows, ch), f32)] * 2 + [jax.ShapeDtypeStruct((rows, ch), bf16)],
        scratch_shapes=[pltpu.VMEM((HALO, ch), f32)] * 2,
        compiler_params=_cp(("arbitrary",)),
    )(a, h, h, gate[0], dy)


def _rms(x, g):
    return x * lax.rsqrt(jnp.mean(x * x, axis=-1, keepdims=True) + EPS) * g


def f_rms(step, p, c, x):
    return [_rms(x[0], p[0])], []


def _rope_swap(x):
    lane = lax.broadcasted_iota(jnp.int32, x.shape, 1)
    half = QK_ROPE // 2
    sw = jnp.where(lane < KR_LANE + half, pltpu.roll(x, LANE - half, 1), pltpu.roll(x, half, 1))
    return jnp.where(jnp.logical_and(lane >= KR_LANE, lane < KR_LANE + QK_ROPE), sw, 0.0)


def _rope(x, cosf, sinf):
    return x * cosf + _rope_swap(x) * sinf


def _heads(x):
    return [x[:, LANE * h:LANE * (h + 1)] for h in range(x.shape[1] // LANE)]


def q_rope_epi(q, cosf, sinf):
    return (jnp.concatenate([_rope(b, cosf, sinf) * ATT_C for b in _heads(q)], axis=1),)


def k_rope_epi(kn, kr, cosf, sinf):
    kr_rot = _rope(kr.astype(f32), cosf, sinf)
    return (jnp.concatenate([b + kr_rot for b in _heads(kn)], axis=1),)


def rope_bwd(dqr, dkr, cosf, sinf):
    back = lambda g: g * cosf + _rope_swap(g * sinf)
    dq = jnp.concatenate([back(b) * ATT_C for b in _heads(dqr)], axis=1)
    dkrope = back(sum(_heads(dkr)))
    return dq, dkr, dkrope


def _conv(tail, x, w, b):
    xf = jnp.concatenate([tail, x], axis=0)
    acc = b + w[CONV_W - 1:CONV_W, :] * xf
    for k in range(CONV_W - 1):
        acc = acc + w[k:k + 1, :] * shift_down(xf, CONV_W - 1 - k)
    return acc[HALO:, :]


def f_pool(step, p, c, x):
    wp, sc = p
    (tail,) = c
    (u,) = x
    tm = u.shape[0]
    xf = jnp.concatenate([tail, u], axis=0)
    sums, s, w = [], xf, 1
    while w < POOL_WINDOWS[-1]:
        s = s + shift_down(s, w)
        w *= 2
        sums.append(s)
    t = step * tm + lax.broadcasted_iota(jnp.int32, (tm, 1), 0)
    ys = []
    for g, (w, s) in enumerate(zip(POOL_WINDOWS, sums)):
        sl = slice(LANE * g, LANE * (g + 1))
        cnt = jnp.minimum(t + 1, w).astype(f32)
        d = s[POOL_HALO:, sl] / cnt - u[:, sl]
        ys.append(jnp.dot(d.astype(bf16), wp[LANE * g:LANE * (g + 1), :].astype(bf16), preferred_element_type=f32))
    return [jnp.concatenate(ys, axis=1) * sc], [u[tm - POOL_HALO:, :]]


def f_ssd_tile(step, p, c, x):
    outs = []
    for k in range(x[0].shape[0] // SSD_CHUNK):
        o, c = f_ssd(step, p, c, [t[SSD_CHUNK * k:SSD_CHUNK * (k + 1), :] for t in x])
        outs.append(o[0])
    return [jnp.concatenate(outs, axis=0)], c


def f_ssd(step, p, c, x):
    conv_w, conv_b, dtb, alog, dsk, ng = p
    tail, s_in = c[0], c[1:]
    z, xbc, dt = x
    ln = z.shape[0]
    xc = jax.nn.silu(_conv(tail, xbc, conv_w, conv_b))
    xs, bb, cc = xc[:, :MIX], xc[:, MIX:MIX + LANE], xc[:, MIX + LANE:]
    dtv = jax.nn.softplus(dt + dtb[0:1, :])
    a = dtv * -jnp.exp(alog[0:1, :])
    ri = lax.broadcasted_iota(jnp.int32, (ln, ln), 0)
    ci = lax.broadcasted_iota(jnp.int32, (ln, ln), 1)
    tril = (ri >= ci).astype(f32)
    triu = (ri <= ci).astype(f32)
    hi = lax.Precision.HIGHEST
    a_cs = jnp.dot(tril, a, precision=hi, preferred_element_type=f32)
    a_cs_t = lax.dot_general(a, triu, TN, precision=hi, preferred_element_type=f32)
    a_tot = jnp.sum(a, axis=0, keepdims=True)
    lane = lax.broadcasted_iota(jnp.int32, (1, LANE), 1)
    half = [(lane < 64).astype(f32), (lane >= 64).astype(f32)]
    hrow = lax.broadcasted_iota(jnp.int32, (LANE, 1), 0)

    def head(v, h):
        return jnp.sum(v * (lane == h).astype(f32), axis=1, keepdims=True)

    def pair(v, j):
        return head(v, 2 * j) * half[0] + head(v, 2 * j + 1) * half[1]

    cg = [(cc * half[g]).astype(bf16) for g in range(2)]
    bg = [(bb * half[g]).astype(bf16) for g in range(2)]
    cb = [lax.dot_general(cg[g], bg[g], NT, preferred_element_type=f32) for g in range(2)]
    ys, s_out = [], []
    for j in range(4):
        g = j // 2
        xs_j = xs[:, LANE * j:LANE * (j + 1)]
        xj = xs_j * pair(dtv, j)
        yj = xs_j * pair(dsk[0:1, :], j)
        for hh in range(2):
            h = 2 * j + hh
            rowv = jnp.sum(a_cs_t * (hrow == h).astype(f32), axis=0, keepdims=True)
            lmat = jnp.exp(jnp.where(ri >= ci, head(a_cs, h) - rowv, -jnp.inf))
            yj = yj + jnp.dot((cb[g] * lmat).astype(bf16), (xj * half[hh]).astype(bf16), preferred_element_type=f32)
        acs = pair(a_cs, j)
        tot = pair(a_tot, j)
        yj = yj + jnp.exp(acs) * jnp.dot(cg[g], s_in[j].astype(bf16), preferred_element_type=f32)
        s_new = jnp.exp(tot) * s_in[j] + lax.dot_general(bg[g], (xj * jnp.exp(tot - acs)).astype(bf16), TN,
                                                         preferred_element_type=f32)
        ys.append(yj)
        s_out.append(s_new)
    y = jnp.concatenate(ys, axis=1) * jax.nn.silu(z)
    return [_rms(y, ng)], [xbc[ln - HALO:, :]] + s_out


def _neg_expm1(y):
    series = -y * (1.0 + y * (0.5 + y * (1.0 / 6 + y * (1.0 / 24 + y * (1.0 / 120)))))
    return jnp.where(y > -0.05, series, 1.0 - jnp.exp(y))


def f_lru_pre(step, p, c, x):
    cw, cb_, wa, ba, wi, bi, lam = p
    (tail,) = c
    (lx,) = x
    tm = lx.shape[0]
    xc = _conv(tail, lx, cw, cb_)
    xb = xc.astype(bf16)
    r = jax.nn.sigmoid(jnp.dot(xb, wa.astype(bf16), preferred_element_type=f32) + ba)
    it = jax.nn.sigmoid(jnp.dot(xb, wi.astype(bf16), preferred_element_type=f32) + bi)
    log_a = -LRU_C * r * jax.nn.softplus(-lam)
    mult = jnp.sqrt(_neg_expm1(2.0 * log_a))
    return [jnp.exp(log_a), xc * it * mult], [lx[tm - HALO:, :]]


def loss_head(x, tgt, g, tm):
    rows, d = x.shape
    n = rows // tm

    def body(x_ref, t_ref, g_ref, loss_ref, dx_ref, dg_ref):
        @pl.when(pl.program_id(0) == 0)
        def _():
            loss_ref[...] = jnp.zeros_like(loss_ref)
            dg_ref[...] = jnp.zeros_like(dg_ref)

        def fn(gv, xv):
            err = _rms(xv, gv) - t_ref[...]
            return 0.5 * jnp.sum(jnp.mean(err * err, axis=-1, keepdims=True))

        val, (dg, dx) = jax.value_and_grad(fn, argnums=(0, 1))(g_ref[...], x_ref[...])
        loss_ref[...] += val
        dg_ref[...] += dg
        dx_ref[...] = dx

    spec = pl.BlockSpec((tm, d), lambda i: (i, 0))
    return pl.pallas_call(
        body, name="loss_head", grid=(n,), in_specs=[spec, spec, _const_spec((1, d))],
        out_specs=[_const_spec((8, LANE)), spec, _const_spec((1, d))],
        out_shape=[jax.ShapeDtypeStruct((8, LANE), f32), jax.ShapeDtypeStruct((rows, d), f32),
                   jax.ShapeDtypeStruct((1, d), f32)],
        compiler_params=_cp(("arbitrary",)),
    )(x, tgt, g)


def ew(name, fn, ins, outs, tm):
    rows = ins[0][0].shape[0]
    ni = len(ins)

    def body(*refs):
        res = fn(*[r[...].astype(f32) for r in refs[:ni]])
        for r, v in zip(refs[ni:], res):
            r[...] = v.astype(r.dtype)

    return pl.pallas_call(
        body, name=name, grid=(rows // tm,), in_specs=[_tile_spec(tm, w, cb) for (_, w, cb) in ins],
        out_specs=[_tile_spec(tm, w, 0) for (w, _) in outs],
        out_shape=[jax.ShapeDtypeStruct((rows, w), dt) for (w, dt) in outs],
        compiler_params=_cp(("parallel",)),
    )(*[t[0] for t in ins])


def _peers():
    x, y, c = lax.axis_index("x"), lax.axis_index("y"), lax.axis_index("c")
    me = 4 * x + 2 * y + c
    out = []
    for k in range(1, N_DEV):
        px = 1 - x if k & 4 else x
        py = 1 - y if k & 2 else y
        pc = 1 - c if k & 1 else c
        out.append(((px, py, pc), 4 * px + 2 * py + pc))
    return me, out


_HBM = pl.BlockSpec(memory_space=pltpu.HBM)
_SEM = pl.BlockSpec(memory_space=pltpu.SEMAPHORE)
_EFFECT = pltpu.SideEffectType.DATAFLOW_SIDE_EFFECTING


def _remote(src_ref, land_ref, gather, me, pid, dev, send_sems, recv_sems, k, recv_side):
    return pltpu.make_async_remote_copy(
        src_ref=src_ref if gather else src_ref.at[pid], dst_ref=land_ref.at[pid if recv_side else me],
        send_sem=send_sems.at[k], recv_sem=recv_sems.at[k], device_id=dev, device_id_type=pl.DeviceIdType.MESH)


def _own(src_ref, land_ref, gather, me, sem):
    return pltpu.make_async_copy(src_ref if gather else src_ref.at[me], land_ref.at[me], sem)


def exchange_start(name, srcs, gather, deps=()):
    n, nd = len(srcs), len(deps)
    shapes = [(s.shape if gather else s.shape[1:]) for s in srcs]
    lands = [lax.empty((N_DEV,) + tuple(sh), s.dtype) for s, sh in zip(srcs, shapes)]

    def body(*refs):
        src_refs, land_refs = refs[:n], refs[n:2 * n]
        send_sems, recv_sems, own_sem = refs[2 * n + nd:2 * n + nd + 3]
        token = refs[-1]
        me, peers = _peers()
        for k, (dev, pid) in enumerate(peers):
            for s_ref, l_ref in zip(src_refs, land_refs):
                _remote(s_ref, l_ref, gather, me, pid, dev, send_sems, recv_sems, k, False).start()
        for s_ref, l_ref in zip(src_refs, land_refs):
            _own(s_ref, l_ref, gather, me, own_sem).start()
        token[...] = jnp.zeros_like(token)

    hbm = lambda a: pltpu.with_memory_space_constraint(a, pltpu.HBM)
    res = pl.pallas_call(
        body, name=name,
        out_shape=(pltpu.SemaphoreType.DMA((N_DEV - 1,)), pltpu.SemaphoreType.DMA((N_DEV - 1,)), pltpu.SemaphoreType.DMA(()),
                   *[pltpu.HBM(a.shape, a.dtype) for a in list(srcs) + lands], jax.ShapeDtypeStruct((8, LANE), f32)),
        in_specs=[_HBM] * (2 * n) + [pl.BlockSpec(memory_space=pl.ANY)] * nd,
        out_specs=(_SEM, _SEM, _SEM, *([_HBM] * (2 * n)), pl.BlockSpec(memory_space=pltpu.VMEM)),
        input_output_aliases={i: 3 + i for i in range(2 * n)},
        compiler_params=pltpu.CompilerParams(has_side_effects=_EFFECT),
    )(*[hbm(a) for a in list(srcs) + lands], *deps)
    return dict(sems=res[:3], srcs=list(res[3:3 + n]), lands=list(res[3 + n:3 + 2 * n]), token=res[-1], gather=gather)


def exchange_wait(name, h, afters):
    n, gather = len(h["srcs"]), h["gather"]

    def body(*refs):
        src_refs, land_refs = refs[:n], refs[n:2 * n]
        send_sems, recv_sems, own_sem = refs[2 * n:2 * n + 3]
        me, peers = _peers()
        for k, (dev, pid) in enumerate(peers):
            for s_ref, l_ref in zip(src_refs, land_refs):
                _remote(s_ref, l_ref, gather, me, pid, dev, send_sems, recv_sems, k, True).wait_recv()
        for k, (dev, pid) in enumerate(peers):
            for s_ref, l_ref in zip(src_refs, land_refs):
                _remote(s_ref, l_ref, gather, me, pid, dev, send_sems, recv_sems, k, False).wait_send()
        for s_ref, l_ref in zip(src_refs, land_refs):
            _own(s_ref, l_ref, gather, me, own_sem).wait()

    arrs = h["srcs"] + h["lands"]
    res = pl.pallas_call(
        body, name=name, out_shape=tuple(pltpu.HBM(a.shape, a.dtype) for a in arrs),
        in_specs=[_HBM] * (2 * n) + [_SEM, _SEM, _SEM] + [pl.BlockSpec(memory_space=pl.ANY)] * len(afters),
        out_specs=tuple([_HBM] * (2 * n)), input_output_aliases={i: i for i in range(2 * n)},
        compiler_params=pltpu.CompilerParams(has_side_effects=_EFFECT),
    )(*arrs, *h["sems"], *afters)
    return list(res[n:])


def _adam_update(g, w, m, v):
    mn = ADAM_B1 * m + (1.0 - ADAM_B1) * g
    vn = ADAM_B2 * v + (1.0 - ADAM_B2) * jnp.square(g)
    m_hat = mn / (1.0 - ADAM_B1 ** ADAM_STEP)
    v_hat = vn / (1.0 - ADAM_B2 ** ADAM_STEP)
    return -ADAM_LR * (m_hat / (jnp.sqrt(v_hat) + ADAM_EPS) + ADAM_WD * w), mn, vn


def _adamw_vectors(name, parts, w, m, v):
    nl = len(parts)

    def body(*refs):
        p_refs = refs[:nl]
        w_ref, m_ref, v_ref, g_ref, d_ref, nm_ref, nv_ref = refs[nl:]
        for ll, p_ref in enumerate(p_refs):
            row = slice(ll, ll + 1)
            g = p_ref[0:1, :]
            for i in range(1, N_DEV):
                g = g + p_ref[i:i + 1, :]
            delta, mn, vn = _adam_update(g, w_ref[row, :], m_ref[row, :], v_ref[row, :])
            g_ref[row, :] = g
            d_ref[row, :] = delta
            nm_ref[row, :] = mn
            nv_ref[row, :] = vn

    return list(pl.pallas_call(body, name=name, out_shape=[jax.ShapeDtypeStruct(w.shape, f32)] * 4)(*parts, w, m, v))


def adamw_packed(name, gots, where, ws, ms, vs):
    ng, npar = len(gots), len(ws)

    def body(*refs):
        g_refs = refs[:ng]
        w_refs, m_refs, v_refs = (refs[ng + k * npar:ng + (k + 1) * npar] for k in range(3))
        o_refs = refs[ng + 3 * npar:]
        for p in range(npar):
            width = w_refs[p].shape[1]
            for l, (which, off) in enumerate(where[p]):
                row = slice(l, l + 1)
                cols = slice(off, off + width)
                g = g_refs[which][0:1, cols]
                for i in range(1, N_DEV):
                    g = g + g_refs[which][i:i + 1, cols]
                delta, mn, vn = _adam_update(g, w_refs[p][row, :], m_refs[p][row, :], v_refs[p][row, :])
                for k, val in enumerate((g, delta, mn, vn)):
                    o_refs[4 * p + k][row, :] = val

    out_shape = [jax.ShapeDtypeStruct(w.shape, f32) for w in ws for _ in range(4)]
    res = pl.pallas_call(body, name=name, out_shape=out_shape, compiler_params=_cp(()))(*gots, *ws, *ms, *vs)
    return [list(res[4 * p:4 * p + 4]) for p in range(npar)]


def adamw_columns(name, parts, w, m, v):
    nl, kk, cc = w.shape
    view = lambda a: jnp.transpose(a, (2, 0, 1))
    tc = min(LANE, cc)

    def body(*refs):
        p_refs = refs[:nl]
        w_ref, m_ref, v_ref, g_ref, d_ref, nm_ref, nv_ref = refs[nl:]
        for l, p_ref in enumerate(p_refs):
            g = p_ref[0].astype(f32)
            for i in range(1, N_DEV):
                g = g + p_ref[i].astype(f32)
            g = g.T
            delta, mn, vn = _adam_update(g, w_ref[:, l, :], m_ref[:, l, :], v_ref[:, l, :])
            g_ref[:, l, :] = g
            d_ref[:, l, :] = delta
            nm_ref[:, l, :] = mn
            nv_ref[:, l, :] = vn

    p_spec = pl.BlockSpec((N_DEV, kk, tc), lambda j: (0, 0, j))
    w_spec = pl.BlockSpec((tc, nl, kk), lambda j: (j, 0, 0))
    res = pl.pallas_call(
        body, name=name, grid=(pl.cdiv(cc, tc),), in_specs=[p_spec] * nl + [w_spec] * 3, out_specs=[w_spec] * 4,
        out_shape=[jax.ShapeDtypeStruct((cc, nl, kk), f32)] * 4, compiler_params=_cp(("parallel",)),
    )(*parts, view(w), view(m), view(v))
    return [jnp.transpose(a, (1, 2, 0)) for a in res]


def adamw(name, parts, w, m, v):
    nl = len(parts)
    shape = w.shape[1:]
    c = shape[-1]
    r = 1
    for s in shape[:-1]:
        r *= s
    if r == 1:
        return _adamw_vectors(name, parts, w, m, v)
    tr = _pick(r, 256) if r % 8 == 0 else r
    nb = r // tr
    parts2 = [p.reshape(N_DEV, r, c) for p in parts]
    w2, m2, v2 = (a.reshape(nl, r, c) for a in (w, m, v))

    def body(*refs):
        p_refs = refs[:nl]
        w_ref, m_ref, v_ref, g_ref, d_ref, nm_ref, nv_ref = refs[nl:]
        layer = pl.program_id(0)
        for ll, p_ref in enumerate(p_refs):
            @pl.when(layer == ll)
            def _(p_ref=p_ref):
                g = p_ref[0].astype(f32)
                for i in range(1, N_DEV):
                    g = g + p_ref[i].astype(f32)
                delta, mn, vn = _adam_update(g, w_ref[0], m_ref[0], v_ref[0])
                g_ref[0] = g
                d_ref[0] = delta
                nm_ref[0] = mn
                nv_ref[0] = vn

    def p_spec(ll):
        return pl.BlockSpec((N_DEV, tr, c), lambda l, i: (0, jnp.where(l == ll, i, jnp.where(l > ll, nb - 1, 0)), 0))

    spec = pl.BlockSpec((1, tr, c), lambda l, i: (l, i, 0))
    res = pl.pallas_call(
        body, name=name, grid=(nl, nb), in_specs=[p_spec(ll) for ll in range(nl)] + [spec, spec, spec],
        out_specs=[spec] * 4, out_shape=[jax.ShapeDtypeStruct((nl, r, c), f32)] * 4,
        compiler_params=_cp(("arbitrary", "arbitrary")),
    )(*parts2, w2, m2, v2)
    return [a.reshape(w.shape) for a in res]


_IN_SPLITS = dict(cq=(0, 384), ckv=(384, 640), kr=(640, 672), pool=(672, 1184), z=(1184, 1696), xbc=(1696, 2464),
                  dt=(2464, 2472), lg=(2472, 2984), lx=(2984, 3496), gates=(3496, 7592))


W_IN_SHARD = IN_COLS // N_DEV

_PAD_ORDER = ("gates", "pool", "z", "lg", "lx", "xbc", "cq", KR_LANE, "kr", LANE - KR_LANE - QK_ROPE, "ckv", "dt",
              LANE - 8, U_COLS - U_DT[0] - LANE)
_SEGMENTS = ((0, U_CQ[0], 384), (384, U_CKV[0], 256), (640, U_KR[0] + KR_LANE, QK_ROPE), (672, U_POOL[0], 512),
             (1184, U_Z[0], 512), (1696, U_XBC[0], 768), (2464, U_DT[0], 8), (2472, U_LG[0], 512), (2984, U_LX[0], 512),
             (3496, 0, 4096))


def _pad_w_in(shards):
    rows = shards.shape[1]
    pieces = []
    for item in _PAD_ORDER:
        if isinstance(item, int):
            pieces.append(jnp.zeros((rows, item), shards.dtype))
            continue
        a, b = _IN_SPLITS[item]
        for d in range(a // W_IN_SHARD, (b - 1) // W_IN_SHARD + 1):
            lo, hi = max(a, d * W_IN_SHARD), min(b, (d + 1) * W_IN_SHARD)
            pieces.append(shards[d, :, lo - d * W_IN_SHARD:hi - d * W_IN_SHARD])
    return jnp.concatenate(pieces, axis=1)


def _w_in_blocks(g):
    blocks = []
    for d in range(N_DEV):
        a, b = d * W_IN_SHARD, (d + 1) * W_IN_SHARD
        pieces = []
        for ref, pad, width in _SEGMENTS:
            lo, hi = max(a, ref), min(b, ref + width)
            if lo < hi:
                pieces.append(g[:, pad + lo - ref:pad + hi - ref])
        blocks.append(jnp.concatenate(pieces, axis=1))
    return jnp.stack(blocks).astype(bf16)


def _head_pad_cols(w, per, lo, hi):
    k = w.shape[0]
    w = w.reshape(k, N_HEADS, per)[:, :, lo:hi]
    return jnp.pad(w, ((0, 0), (0, 0), (0, LANE - (hi - lo)))).reshape(k, N_HEADS * LANE)


def _head_unpad_cols(g, n):
    k = g.shape[0]
    return g.reshape(k, N_HEADS, LANE)[:, :, :n]


def _on_diagonal():
    i = lax.broadcasted_iota(jnp.int32, (8, 1, 8, 1), 0)
    j = lax.broadcasted_iota(jnp.int32, (8, 1, 8, 1), 2)
    return i == j


def _block_diag(w):
    w4 = jnp.broadcast_to(w[:, :, None, :], (8, 64, 8, 64))
    return jnp.where(_on_diagonal(), w4, 0.0).reshape(MIX, MIX)


def _block_diag_inv(g):
    return jnp.sum(jnp.where(_on_diagonal(), g.reshape(8, 64, 8, 64), 0.0), axis=2)


def _head8(v):
    return jnp.pad(v[None, :], ((0, 7), (0, LANE - v.shape[0])))


GROUPS = dict(A=("w_in",), B=("w_uq", "w_ukv", "ssd_conv_w", "lru_conv_w", "w_branch", "w_out"),
              C=("w_ff1", "w_ff2", "w_ple_gate", "w_ple"))


def _kernel_weights(grp, fw):
    if grp == "A":
        w_in = _pad_w_in(fw["w_in"])
        return dict(w_in=w_in, w_dt=w_in[:, U_DT[0]:U_DT[0] + LANE])
    if grp == "C":
        return dict(w_ff1=fw["w_ff1"], w_ff2=fw["w_ff2"], w_pg=fw["w_ple_gate"], w_ple=fw["w_ple"])
    wb = fw["w_branch"]
    wb0 = jnp.pad(wb[0].reshape(N_HEADS, V_HEAD, D_MODEL), ((0, 0), (0, LANE - V_HEAD), (0, 0))).reshape(N_HEADS * LANE, D_MODEL)
    return dict(
        w_uq=_head_pad_cols(fw["w_uq"], QK_NOPE + QK_ROPE, 0, QK_NOPE + QK_ROPE),
        w_uk=_head_pad_cols(fw["w_ukv"], QK_NOPE + V_HEAD, 0, QK_NOPE),
        w_uv=_head_pad_cols(fw["w_ukv"], QK_NOPE + V_HEAD, QK_NOPE, QK_NOPE + V_HEAD),
        wb=[wb0, wb[1], wb[2], wb[3]], w_out=fw["w_out"], ssd_conv_w=fw["ssd_conv_w"], lru_conv_w=fw["lru_conv_w"])


def _layer_params(sp, l):
    row = lambda n: sp[n][l][None, :]
    return dict(
        g_mix=row("g_mix"), q_norm=row("q_norm"), kv_norm=row("kv_norm"),
        pool=[sp["w_pool"][l].reshape(4 * LANE, LANE), row("pool_scale")],
        ssd=[None, row("ssd_conv_b"), _head8(sp["ssd_dt_bias"][l]), _head8(sp["ssd_a_log"][l]),
             _head8(sp["ssd_d"][l]), row("ssd_norm")],
        lru=[None, row("lru_conv_b"), _block_diag(sp["lru_w_a"][l]), row("lru_b_a"),
             _block_diag(sp["lru_w_i"][l]), row("lru_b_i"), row("lru_lambda")],
        g_mlp=row("g_mlp"), g_ple=row("g_ple"),
    )


_sig = jax.nn.sigmoid
_SSD_CARRY = [(HALO, SSD_XBC)] + [(LANE, LANE)] * 4


def _tiles(rows):
    return dict(tm=_pick(rows, 512), ta=_pick(rows, 512), tp=_pick(rows, 512), tl=_pick(rows, 512), ts=_pick(rows, 256),
                tssd=_pick(rows, SSD_CHUNK * SSD_CHUNKS_PER_TILE))


def _mixer_tiles(u, dt32):
    return dict(
        cq=(u, 384, U_CQ[0] // 384), ckv=(u, 256, U_CKV[0] // 256), kr=(u, LANE, U_KR[0] // LANE),
        pool=(u, MIX, U_POOL[0] // MIX), z=(u, MIX, U_Z[0] // MIX), xbc=(u, SSD_XBC, U_XBC[0] // SSD_XBC),
        dt=(dt32, LANE, 0), lg=(u, MIX, U_LG[0] // MIX), lx=(u, MIX, U_LX[0] // MIX))


def _add_norm(acc, resid, g):
    x = acc + resid
    return x, _rms(x, g)


def _layer_fwd(x, h, p_bf, ctx, l, pr, g_next, cosf, sinf):
    rows = x.shape[0]
    ts = _tiles(rows)
    tm = ts["tm"]
    nm = lambda s: f"{s}_l{l}"
    r = dict(x=x)
    if h is None:
        (h,), _ = seq_fwd(nm("rms_in"), f_rms, [pr["g_mix"]], [(x, D_MODEL, 0)], [], [(D_MODEL, bf16)], tm)
    early = [h] + ([cosf, sinf, p_bf] + [a for v in pr.values() for a in (v if isinstance(v, list) else [v]) if a is not None]
                   if l == 0 else [])
    w = dict(_kernel_weights("A", ctx.weights(l, "A", early)))
    u = matmul(nm("w_in"), h, w["w_in"], outs=(U_DTYPE,))
    dt32 = matmul(nm("w_dt"), h, w["w_dt"])
    mt = _mixer_tiles(u, dt32)
    (cqn,), _ = seq_fwd(nm("rms_q"), f_rms, [pr["q_norm"]], [mt["cq"]], [], [(Q_LORA, bf16)], tm)
    (ckvn,), _ = seq_fwd(nm("rms_kv"), f_rms, [pr["kv_norm"]], [mt["ckv"]], [], [(KV_LORA, bf16)], tm)
    (yb,), pool_saved = seq_fwd(nm("pool"), f_pool, pr["pool"], [mt["pool"]], [(POOL_HALO, MIX)], [(MIX, bf16)], ts["tp"])
    w.update(_kernel_weights("B", ctx.weights(l, "B", yb)))
    pr = dict(pr, ssd=[w["ssd_conv_w"]] + pr["ssd"][1:], lru=[w["lru_conv_w"]] + pr["lru"][1:])
    tables = [(cosf, 0, LANE), (sinf, 0, LANE)]
    qr = matmul(nm("w_uq"), cqn, w["w_uq"], outs=(bf16,), epi=q_rope_epi, extras=tables)
    kr = matmul(nm("w_uk"), ckvn, w["w_uk"], outs=(bf16,), epi=k_rope_epi, extras=[(u, U_KR[0], LANE)] + tables)
    vb = matmul(nm("w_uv"), ckvn, w["w_uv"], outs=(bf16,))
    o, lse = attn_fwd(qr, kr, vb, ts["ta"])
    (yc,), ssd_saved = seq_fwd(nm("ssd"), f_ssd_tile, pr["ssd"], [mt["z"], mt["xbc"], mt["dt"]], _SSD_CARRY, [(MIX, bf16)],
                               ts["tssd"])
    (la, lu), lru_saved = seq_fwd(nm("lru_pre"), f_lru_pre, pr["lru"], [mt["lx"]], [(HALO, MIX)], [(MIX, f32), (MIX, f32)], ts["tl"])
    hh, yd = scan_fwd(la, lu, mt["lg"], ts["ts"])
    ys = [o, yb, yc, yd]
    m, pres = merge_fwd(nm("merge"), ys, w["wb"], u)
    x1, h2 = matmul(nm("w_out"), m, w["w_out"], outs=(f32, bf16), epi=_add_norm, extras=[(x, 0)], rows=[pr["g_mlp"]])
    w.update(_kernel_weights("C", ctx.weights(l, "C", h2)))
    a1, act = matmul(nm("ff1"), h2, w["w_ff1"], outs=(bf16, bf16), epi=lambda acc: (acc, jnp.square(jnp.maximum(acc, 0.0))))
    x2, h3 = matmul(nm("ff2"), act, w["w_ff2"], outs=(f32, bf16), epi=_add_norm, extras=[(x1, 0)], rows=[pr["g_ple"]])
    gl = matmul(nm("ple_gate"), h3, w["w_pg"])
    if g_next is None:
        x3, pe = matmul(nm("ple"), p_bf, w["w_ple"], outs=(f32, f32), epi=lambda acc, g, xr: (xr + acc * _sig(g), acc),
                        extras=[(gl, 0), (x2, 0)])
        h_next = None
    else:
        def ple_norm(acc, g, xr, gn):
            xo = xr + acc * _sig(g)
            return xo, acc, _rms(xo, gn)

        x3, pe, h_next = matmul(nm("ple"), p_bf, w["w_ple"], outs=(f32, f32, bf16), epi=ple_norm,
                                extras=[(gl, 0), (x2, 0)], rows=[g_next])
    r.update(h=h, u=u, cqn=cqn, ckvn=ckvn, vb=vb, qr=qr, kr=kr, o=o, lse=lse, ys=ys, pres=pres, m=m, x1=x1,
             h2=h2, a1=a1, act=act, x2=x2, h3=h3, gl=gl, pe=pe, p_bf=p_bf, pool_saved=pool_saved, ssd_saved=ssd_saved,
             lru_saved=lru_saved, la=la, hh=hh, w=w, pr=pr, dt32=dt32)
    return x3, h_next, r


def _norm_bwd(dh, x, resid, g):
    rs = lax.rsqrt(jnp.mean(x * x, axis=-1, keepdims=True) + EPS)
    xhat = x * rs
    dxn = dh * g
    dx = rs * (dxn - xhat * jnp.mean(dxn * xhat, axis=-1, keepdims=True)) + resid
    return dx, jnp.sum(dh * xhat, axis=0, keepdims=True)


def _gate_bwd(d, g, pre):
    s = _sig(g.astype(f32))
    return d * s, d * pre.astype(f32) * s * (1.0 - s)


def _layer_bwd(dx3, r, ctx, l, cosf, sinf, tok, extra_small):
    rows = dx3.shape[0]
    ts = _tiles(rows)
    tm = ts["tm"]
    nm = lambda s: f"{s}_l{l}"
    u, w, pr = r["u"], r["w"], r["pr"]
    mt = _mixer_tiles(u, r["dt32"])
    g = {}
    full = lambda a: (a, a.shape[1], 0)
    dpe, dgl = ew(nm("ple_bwd"), _gate_bwd, [full(dx3), full(r["gl"]), full(r["pe"])], [(D_MODEL, bf16)] * 2, tm)
    g["w_ple"] = matmul(nm("d_w_ple"), r["p_bf"], dpe, ta=True, outs=(bf16,), deps=[tok] if tok is not None else [])
    g["w_pg"] = matmul(nm("d_w_pg"), r["h3"], dgl, ta=True, outs=(bf16,))
    dx2, g["g_ple"] = matmul(nm("d_h3"), dgl, w["w_pg"], tb=True, epi=_norm_bwd, extras=[(r["x2"], 0), (dx3, 0)],
                             rows=[pr["g_ple"]], row_sums=1)
    da1 = matmul(nm("d_act"), dx2, w["w_ff2"], tb=True, outs=(bf16,),
                 epi=lambda acc, a: (acc * 2.0 * jnp.maximum(a, 0.0),), extras=[(r["a1"], 0)])
    g["w_ff2"] = matmul(nm("d_w_ff2"), r["act"], dx2, ta=True, outs=(bf16,))
    g["w_ff1"] = matmul(nm("d_w_ff1"), r["h2"], da1, ta=True, outs=(bf16,), out_blocks=N_DEV)
    tok = ctx.grads(l, "C", dict(w_ff1=g["w_ff1"], w_ff2=g["w_ff2"], w_ple_gate=g["w_pg"], w_ple=g["w_ple"]))
    dx1, g["g_mlp"] = matmul(nm("d_h2"), da1, w["w_ff1"], tb=True, epi=_norm_bwd, extras=[(r["x1"], 0), (dx2, 0)],
                             rows=[pr["g_mlp"]], row_sums=1, deps=[tok])
    def merge_bwd(dm, *gates_and_pres):
        both = [_gate_bwd(dm, gates_and_pres[n], gates_and_pres[4 + n]) for n in range(4)]
        return tuple(b[0] for b in both) + tuple(b[1] for b in both)

    res = matmul(nm("d_merged"), dx1, w["w_out"], tb=True, outs=(bf16,) * 8, epi=merge_bwd,
                 extras=[(u, D_MODEL * n) for n in range(4)] + [(pre, 0) for pre in r["pres"]])
    dpres, dgates = list(res[:4]), list(res[4:])
    g["w_out"] = matmul(nm("d_w_out"), r["m"], dx1, ta=True, outs=(bf16,))
    dys, g["wb"] = [], []
    for n in range(4):
        g["wb"].append(matmul(nm(f"d_w_branch{n}"), r["ys"][n], dpres[n], ta=True, outs=(bf16,)))
        dys.append(matmul(nm(f"d_y{n}"), dpres[n], w["wb"][n], tb=True, outs=(bf16 if n == 0 else f32,)))
    dqr, dkr_, dv = attn_bwd(r["qr"], r["kr"], r["vb"], dys[0], r["o"], r["lse"], ts["ta"])
    hw = N_HEADS * LANE
    dq, dkn, dkrope = ew(nm("rope_bwd"), rope_bwd, [full(dqr), full(dkr_), full(cosf), full(sinf)],
                         [(hw, bf16), (hw, bf16), (LANE, bf16)], tm)
    g["w_uq"] = matmul(nm("d_w_uq"), r["cqn"], dq, ta=True, outs=(bf16,))
    g["w_uk"] = matmul(nm("d_w_uk"), r["ckvn"], dkn, ta=True, outs=(bf16,))
    g["w_uv"] = matmul(nm("d_w_uv"), r["ckvn"], dv, ta=True, outs=(bf16,))
    dcqn = matmul(nm("d_cqn"), dq, w["w_uq"], tb=True)
    dckvn = matmul(nm("d_ckvn_k"), dkn, w["w_uk"], tb=True)
    dckvn = matmul(nm("d_ckvn_v"), dv, w["w_uv"], tb=True, epi=lambda acc, prev: (acc + prev,), extras=[(dckvn, 0)])
    (g["q_norm"],), (dcq,) = seq_bwd(nm("rms_q_bwd"), f_rms, [pr["q_norm"]], [mt["cq"]], [True], [], [dcqn], [bf16], tm)
    (g["kv_norm"],), (dckv,) = seq_bwd(nm("rms_kv_bwd"), f_rms, [pr["kv_norm"]], [mt["ckv"]], [True], [], [dckvn], [bf16], tm)
    g["pool"], (dpool,) = seq_bwd(nm("pool_bwd"), f_pool, pr["pool"], [mt["pool"]], [True], r["pool_saved"], [dys[1]],
                                  [bf16], ts["tp"])
    g["ssd"], (dz, dxbc, ddt) = seq_bwd(nm("ssd_bwd"), f_ssd_tile, pr["ssd"], [mt["z"], mt["xbc"], mt["dt"]], [True] * 3,
                                        r["ssd_saved"], [dys[2]], [bf16] * 3, ts["tssd"])
    da, du, dlg = scan_bwd(r["la"], r["hh"], mt["lg"], dys[3], ts["ts"])
    g["lru"], (dlx,) = seq_bwd(nm("lru_pre_bwd"), f_lru_pre, pr["lru"], [mt["lx"]], [True], r["lru_saved"], [da, du],
                               [bf16], ts["tl"])
    dk = _head_unpad_cols(g["w_uk"], QK_NOPE)
    dv_ = _head_unpad_cols(g["w_uv"], V_HEAD)
    wb0 = g["wb"][0].reshape(N_HEADS, LANE, D_MODEL)[:, :V_HEAD].reshape(MIX, D_MODEL)
    ssd, lru, pool = g["ssd"], g["lru"], g["pool"]
    tok = ctx.grads(l, "B", dict(
        w_uq=_head_unpad_cols(g["w_uq"], QK_NOPE + QK_ROPE).reshape(Q_LORA, -1),
        w_ukv=jnp.concatenate([dk, dv_], axis=2).reshape(KV_LORA, -1), ssd_conv_w=ssd[0], lru_conv_w=lru[0],
        w_branch=jnp.stack([wb0, g["wb"][1], g["wb"][2], g["wb"][3]]), w_out=g["w_out"]))
    du_p = jnp.concatenate(dgates + [dpool, dz, dlg, dlx, dxbc, dcq, dkrope, dckv, ddt,
                                     jnp.zeros((rows, U_COLS - U_DT[0] - LANE), bf16)], axis=1)
    small = dict(
        q_norm=g["q_norm"][0], kv_norm=g["kv_norm"][0],
        w_pool=pool[0].reshape(4, LANE, LANE), pool_scale=pool[1][0],
        ssd_conv_b=ssd[1][0], ssd_dt_bias=ssd[2][0, :8], ssd_a_log=ssd[3][0, :8], ssd_d=ssd[4][0, :8], ssd_norm=ssd[5][0],
        lru_conv_b=lru[1][0], lru_w_a=_block_diag_inv(lru[2]), lru_b_a=lru[3][0], lru_w_i=_block_diag_inv(lru[4]),
        lru_b_i=lru[5][0], lru_lambda=lru[6][0], g_mlp=g["g_mlp"][0], g_ple=g["g_ple"][0])
    tok_small = ctx.small(f"l{l}", [(n, l, small[n]) for n in SMALL if n in small] + extra_small)
    g_w_in = matmul(nm("d_w_in"), r["h"], du_p, ta=True, outs=(bf16,), deps=[tok, tok_small])
    tok = ctx.grads(l, "A", dict(w_in=_w_in_blocks(g_w_in)))
    dx, g_mix = matmul(nm("d_h"), du_p, w["w_in"], tb=True, epi=_norm_bwd, extras=[(r["x"], 0), (dx1, 0)],
                       rows=[pr["g_mix"]], row_sums=1, deps=[tok])
    return dx, tok, ("g_mix", l, g_mix[0])


def _rope_tables(positions):
    inv = 1.0 / (ROPE_THETA ** (jnp.arange(0, QK_ROPE, 2, dtype=f32) / QK_ROPE))
    ang = positions.astype(f32)[:, None] * inv
    cos, sin = jnp.cos(ang), jnp.sin(ang)
    rows = positions.shape[0]
    pad = jnp.zeros((rows, LANE - KR_LANE - QK_ROPE), f32)
    cosf = jnp.concatenate([jnp.ones((rows, KR_LANE), f32), cos, cos, pad], axis=1)
    sinf = jnp.concatenate([jnp.zeros((rows, KR_LANE), f32), -sin, sin, pad], axis=1)
    return cosf, sinf


WEIGHTS = ['g_mix', 'w_in', 'q_norm', 'w_uq', 'kv_norm', 'w_ukv', 'w_pool', 'pool_scale', 'ssd_conv_w', 'ssd_conv_b',
           'ssd_dt_bias', 'ssd_a_log', 'ssd_d', 'ssd_norm', 'lru_conv_w', 'lru_conv_b', 'lru_w_a', 'lru_b_a', 'lru_w_i',
           'lru_b_i', 'lru_lambda', 'w_branch', 'w_out', 'g_mlp', 'w_ff1', 'w_ff2', 'g_ple', 'w_ple_gate', 'w_ple', 'g_final']
SHARDED = dict(w_in=2, w_uq=2, w_ukv=2, ssd_conv_w=2, lru_conv_w=2, w_branch=3, w_out=1, w_ff1=2, w_ff2=1,
               w_ple_gate=1, w_ple=2)
F32_PAYLOAD = ("ssd_conv_w", "lru_conv_w")
DEPTH = 2


SMALL = [n for n in WEIGHTS if n not in SHARDED and n != "g_final"]


def local_step(x, p, positions, tgt, sp, ctx):
    cosf, sinf = _rope_tables(positions)
    res, h = [], None
    for l in range(DEPTH):
        g_next = sp["g_mix"][l + 1][None, :] if l + 1 < DEPTH else None
        x, h, r = _layer_fwd(x, h, p[l].astype(bf16), ctx, l, _layer_params(sp, l), g_next, cosf, sinf)
        res.append(r)
    loss8, dx, dgf = loss_head(x, tgt, sp["g_final"][None, :], _pick(x.shape[0], 512))
    tok = None
    pending = ("g_final", None, dgf[0])
    for l in reversed(range(DEPTH)):
        dx, tok, pending = _layer_bwd(dx, res[l], ctx, l, cosf, sinf, tok, [pending])
    ctx.small("last", [pending])
    return loss8[0, 0], dx


def _payload(name, w):
    return w if name in F32_PAYLOAD else w.astype(bf16)


def _blocks(name, g):
    ax = SHARDED[name] - 1
    shape = list(g.shape)
    shape[ax:ax + 1] = [N_DEV, shape[ax] // N_DEV]
    return _payload(name, jnp.moveaxis(g.reshape(shape), ax, 0))


def _assemble(name, shards):
    ax = SHARDED[name] - 1
    shape = list(shards.shape[1:])
    shape[ax] *= N_DEV
    return jnp.moveaxis(shards, 0, ax).reshape(shape)


class _Exchanges:
    def __init__(self, wts):
        self.wts = wts
        self.ag, self.rs, self.sm = {}, {}, {}
        tok = None
        for l in range(DEPTH):
            for grp, names in GROUPS.items():
                h = exchange_start(f"ag_start_{grp}{l}", [_payload(n, wts[n][l]) for n in names], True,
                                   deps=[] if tok is None else [tok])
                tok = h["token"]
                self.ag[(l, grp)] = h
        self.all_started = tok

    def weights(self, l, grp, after):
        afters = list(after) if isinstance(after, (list, tuple)) else [after]
        if (l, grp) == (0, "A"):
            afters.append(self.all_started)
        got = exchange_wait(f"ag_wait_{grp}{l}", self.ag[(l, grp)], afters)
        out = {}
        for n, a in zip(GROUPS[grp], got):
            out[n] = a if n == "w_in" else _assemble(n, a)
        return out

    def grads(self, l, grp, g):
        cut = lambda n: g[n].ndim == self.wts[n].ndim
        h = exchange_start(f"rs_start_{grp}{l}", [g[n] if cut(n) else _blocks(n, g[n]) for n in GROUPS[grp]], False)
        self.rs[(l, grp)] = h
        return h["token"]

    def small(self, tag, entries):
        entries = sorted(entries, key=lambda e: e[2].size % LANE != 0)
        flat = jnp.concatenate([a.reshape(-1) for _, _, a in entries])
        flat = jnp.pad(flat, (0, (-flat.shape[0]) % (8 * LANE))).reshape(-1, LANE)
        h = exchange_start(f"small_start_{tag}", [flat], True)
        self.sm[tag] = (h, [(n, l, a.shape) for n, l, a in entries])
        return h["token"]

    def collect(self, groups, after):
        parts = {}
        for grp in groups:
            for l in reversed(range(DEPTH)):
                got = exchange_wait(f"rs_wait_{grp}{l}", self.rs[(l, grp)], [after])
                for n, a in zip(GROUPS[grp], got):
                    parts.setdefault(n, [None] * DEPTH)[l] = a
        return parts

    def collect_small(self, after):
        gots, where, parts = [], {}, {}
        for tag, (h, layout) in self.sm.items():
            (got,) = exchange_wait(f"small_wait_{tag}", h, [after])
            got = got.reshape(N_DEV, -1)
            off = 0
            for n, l, shape in layout:
                size = 1
                for d in shape:
                    size *= d
                if len(shape) == 1 and size % LANE == 0 and off % LANE == 0:
                    where.setdefault(n, [None] * (1 if l is None else DEPTH))[l or 0] = (len(gots), off)
                else:
                    part = got[:, off:off + size].reshape((N_DEV,) + tuple(shape))
                    if l is None:
                        parts[n] = [part]
                    else:
                        parts.setdefault(n, [None] * DEPTH)[l] = part
                off += size
            gots.append(got)
        return gots, where, parts


def kernel(x, p, positions, g_mix, w_in, q_norm, w_uq, kv_norm, w_ukv, w_pool, pool_scale, ssd_conv_w, ssd_conv_b,
           ssd_dt_bias, ssd_a_log, ssd_d, ssd_norm, lru_conv_w, lru_conv_b, lru_w_a, lru_b_a, lru_w_i, lru_b_i,
           lru_lambda, w_branch, w_out, g_mlp, w_ff1, w_ff2, g_ple, w_ple_gate, w_ple, g_final, loss_target, m_g_mix,
           m_w_in, m_q_norm, m_w_uq, m_kv_norm, m_w_ukv, m_w_pool, m_pool_scale, m_ssd_conv_w, m_ssd_conv_b,
           m_ssd_dt_bias, m_ssd_a_log, m_ssd_d, m_ssd_norm, m_lru_conv_w, m_lru_conv_b, m_lru_w_a, m_lru_b_a,
           m_lru_w_i, m_lru_b_i, m_lru_lambda, m_w_branch, m_w_out, m_g_mlp, m_w_ff1, m_w_ff2, m_g_ple, m_w_ple_gate,
           m_w_ple, m_g_final, v_g_mix, v_w_in, v_q_norm, v_w_uq, v_kv_norm, v_w_ukv, v_w_pool, v_pool_scale,
           v_ssd_conv_w, v_ssd_conv_b, v_ssd_dt_bias, v_ssd_a_log, v_ssd_d, v_ssd_norm, v_lru_conv_w, v_lru_conv_b,
           v_lru_w_a, v_lru_b_a, v_lru_w_i, v_lru_b_i, v_lru_lambda, v_w_branch, v_w_out, v_g_mlp, v_w_ff1, v_w_ff2,
           v_g_ple, v_w_ple_gate, v_w_ple, v_g_final):
    given = dict(locals())
    wts = {n: given[n] for n in WEIGHTS}
    ctx = _Exchanges(wts)
    loss, grad_x = local_step(x[0], p[:, 0], positions[0], loss_target[0], wts, ctx)

    def update(parts):
        out = {}
        for n, eight in parts.items():
            step = adamw_columns if n == "w_in" else adamw
            out[n] = step(f"adamw_{n}", eight, wts[n], given["m_" + n], given["v_" + n])
        return out

    outs = update(ctx.collect(("C", "B"), grad_x))
    late = outs["w_ff1"][1]
    outs.update(update(ctx.collect(("A",), late)))
    gots, where, parts = ctx.collect_small(late)
    outs.update(update(parts))
    names = sorted(where)
    rows = lambda a: a[None] if a.ndim == 1 else a
    res = adamw_packed("adamw_vectors", gots, [where[n] for n in names], [rows(wts[n]) for n in names],
                       [rows(given["m_" + n]) for n in names], [rows(given["v_" + n]) for n in names])
    for n, four in zip(names, res):
        outs[n] = [a[0] for a in four] if wts[n].ndim == 1 else four
    loss = lax.psum(loss, AXES)
    return (loss, grad_x[None], *[outs[n][0] for n in WEIGHTS], *[outs[n][1] for n in WEIGHTS],
            *[outs[n][2] for n in WEIGHTS], *[outs[n][3] for n in WEIGHTS])
```

```python
import functools

import jax
import jax.numpy as jnp
from jax import lax
from jax.experimental import pallas as pl
from jax.experimental.pallas import tpu as pltpu

f32 = jnp.float32
bf16 = jnp.bfloat16

D_MODEL = 1024
MIX = 512
N_HEADS = 8
QK_NOPE, QK_ROPE, V_HEAD = 64, 32, 64
Q_LORA, KV_LORA = 384, 256
ROPE_THETA = 10000.0
POOL_WINDOWS = (2, 4, 8, 16)
SSD_CHUNK = 128
SSD_CHUNKS_PER_TILE = 2
SSD_XBC = 768
CONV_W = 4
LRU_C = 8.0
EPS = 1e-6
IN_COLS = 7592
ADAM_LR, ADAM_B1, ADAM_B2, ADAM_EPS, ADAM_WD, ADAM_STEP = 0.001, 0.9, 0.999, 1e-08, 0.01, 10

LANE = 128
HALO = 8
POOL_HALO = 16
VMEM_LIMIT = 56 * 1024 * 1024
MATMUL_MAX_K_TILE = 4096
MATMUL_ACC_PASS_WEIGHT = 0.3
MATMUL_VMEM_BUDGET = 40 * 1024 * 1024
N_DEV = 8
AXES = ("x", "y", "c")

U_COLS = 8192
U_GATES, U_POOL, U_Z, U_LG, U_LX, U_XBC, U_CQ, U_KR, U_CKV, U_DT = (
    (0, 4096), (4096, 512), (4608, 512), (5120, 512), (5632, 512), (6144, 768),
    (6912, 384), (7296, 128), (7424, 256), (7680, 128))
KR_LANE = 64
U_DTYPE = bf16


def _cp(sem):
    return pltpu.CompilerParams(dimension_semantics=sem, vmem_limit_bytes=VMEM_LIMIT)


def _pick(dim, pref):
    if dim <= pref:
        return dim
    t = pref
    while t >= LANE:
        if dim % t == 0:
            return t
        t -= LANE
    t = pref
    while dim % t:
        t -= 8
    return t


@functools.partial(jax.custom_vjp, nondiff_argnums=(1,))
def shift_down(x, k):
    row = lax.broadcasted_iota(jnp.int32, x.shape, 0)
    return jnp.where(row >= k, pltpu.roll(x, k, 0), 0.0)


def _shift_down_fwd(x, k):
    return shift_down(x, k), None


def _shift_down_bwd(k, _, g):
    r = g.shape[0]
    row = lax.broadcasted_iota(jnp.int32, g.shape, 0)
    return (jnp.where(row < r - k, pltpu.roll(g, r - k, 0), 0.0),)


shift_down.defvjp(_shift_down_fwd, _shift_down_bwd)


def _tile_spec(tm, width, cb, n=None):
    if n is None:
        return pl.BlockSpec((tm, width), lambda i: (i, cb))
    return pl.BlockSpec((tm, width), lambda i: (n - 1 - i, cb))


def _const_spec(shape):
    nd = len(shape)
    return pl.BlockSpec(shape, lambda i: (0,) * nd)


def seq_fwd(name, f, params, tiles, carries, outs, tm):
    rows = tiles[0][0].shape[0]
    n = rows // tm
    np_, nt, no, nc = len(params), len(tiles), len(outs), len(carries)

    def body(*refs):
        p_refs = refs[:np_]
        t_refs = refs[np_:np_ + nt]
        o_refs = refs[np_ + nt:np_ + nt + no]
        s_refs = refs[np_ + nt + no:np_ + nt + no + nc]
        c_refs = refs[np_ + nt + no + nc:]
        i = pl.program_id(0)

        @pl.when(i == 0)
        def _():
            for c in c_refs:
                c[...] = jnp.zeros_like(c)

        cvals = [c[...] for c in c_refs]
        for s, c in zip(s_refs, cvals):
            s[0] = c
        o, newc = f(i, [r[...] for r in p_refs], cvals, [r[...].astype(f32) for r in t_refs])
        for r, v in zip(o_refs, o):
            r[...] = v.astype(r.dtype)
        for r, v in zip(c_refs, newc):
            r[...] = v

    in_specs = [_const_spec(p.shape) for p in params] + [_tile_spec(tm, w, cb) for (_, w, cb) in tiles]
    out_specs = [_tile_spec(tm, w, 0) for (w, _) in outs]
    out_specs += [pl.BlockSpec((1,) + tuple(c), lambda i, nd=len(c): (i,) + (0,) * nd) for c in carries]
    out_shape = [jax.ShapeDtypeStruct((rows, w), dt) for (w, dt) in outs]
    out_shape += [jax.ShapeDtypeStruct((n,) + tuple(c), f32) for c in carries]
    res = pl.pallas_call(
        body, name=name, grid=(n,), in_specs=in_specs, out_specs=out_specs, out_shape=out_shape,
        scratch_shapes=[pltpu.VMEM(tuple(c), f32) for c in carries],
        compiler_params=_cp(("arbitrary",)),
    )(*params, *[t[0] for t in tiles])
    return list(res[:no]), list(res[no:])


def seq_bwd(name, f, params, tiles, diff, saved, douts, gdtypes, tm):
    rows = tiles[0][0].shape[0]
    n = rows // tm
    np_, nt, nc, nd = len(params), len(tiles), len(saved), len(douts)
    didx = [k for k, d in enumerate(diff) if d]
    ng = len(didx)

    def body(*refs):
        p_refs = refs[:np_]
        t_refs = refs[np_:np_ + nt]
        s_refs = refs[np_ + nt:np_ + nt + nc]
        d_refs = refs[np_ + nt + nc:np_ + nt + nc + nd]
        pos = np_ + nt + nc + nd
        dp_refs = refs[pos:pos + np_]
        dt_refs = refs[pos + np_:pos + np_ + ng]
        dc_refs = refs[pos + np_ + ng:]
        i = pl.program_id(0)
        step = n - 1 - i

        @pl.when(i == 0)
        def _():
            for r in dp_refs:
                r[...] = jnp.zeros_like(r)
            for r in dc_refs:
                r[...] = jnp.zeros_like(r)

        pvals = [r[...] for r in p_refs]
        cvals = [r[0] for r in s_refs]
        xvals = [r[...].astype(f32) for r in t_refs]

        def fn(p, c, xd):
            x = list(xvals)
            for k, v in zip(didx, xd):
                x[k] = v
            return f(step, p, c, x)

        _, vjp = jax.vjp(fn, pvals, cvals, [xvals[k] for k in didx])
        dp, dc, dx = vjp(([r[...].astype(f32) for r in d_refs], [r[...] for r in dc_refs]))
        for r, v in zip(dp_refs, dp):
            r[...] += v
        for r, v in zip(dc_refs, dc):
            r[...] = v
        for r, v in zip(dt_refs, dx):
            r[...] = v.astype(r.dtype)

    in_specs = [_const_spec(p.shape) for p in params] + [_tile_spec(tm, w, cb, n) for (_, w, cb) in tiles]
    in_specs += [pl.BlockSpec((1,) + tuple(s.shape[1:]), lambda i, nd_=s.ndim - 1: (n - 1 - i,) + (0,) * nd_) for s in saved]
    in_specs += [_tile_spec(tm, d.shape[1], 0, n) for d in douts]
    args = list(params) + [t[0] for t in tiles] + list(saved) + list(douts)
    out_specs = [_const_spec(p.shape) for p in params] + [_tile_spec(tm, tiles[k][1], 0, n) for k in didx]
    out_shape = [jax.ShapeDtypeStruct(p.shape, f32) for p in params]
    out_shape += [jax.ShapeDtypeStruct((rows, tiles[k][1]), dt) for k, dt in zip(didx, gdtypes)]
    res = pl.pallas_call(
        body, name=name, grid=(n,), in_specs=in_specs, out_specs=out_specs, out_shape=out_shape,
        scratch_shapes=[pltpu.VMEM(tuple(s.shape[1:]), f32) for s in saved],
        compiler_params=_cp(("arbitrary",)),
    )(*args)
    return list(res[:np_]), list(res[np_:])


def _halvings(dim, lo, hi):
    t, out = _pick(dim, hi), []
    while t >= min(lo, dim) and dim % t == 0:
        out.append(t)
        if t % 2 or (t // 2) % 8:
            break
        t //= 2
    return out


def _matmul_tiles(m, n, k, a_item, b_item, per_out, max_tn=1024, whole_rows=False):
    def vmem_bytes(tm, tn, tk):
        acc = 4 if k // tk > 1 else 0
        return 2 * (tm * tk * a_item + tk * tn * b_item + tm * tn * per_out) + tm * tn * acc

    def traffic(tm, tn, tk):
        nk = k // tk
        return (m * k * a_item * (1 if nk == 1 else n // tn) + k * n * b_item * (m // tm)
                + (nk - 1) * m * n * 8 * MATMUL_ACC_PASS_WEIGHT)

    cands = [(traffic(tm, tn, tk), -tm * tn, tm, tn, tk)
             for tk in _halvings(k, 512, MATMUL_MAX_K_TILE) for tm in _halvings(m, 256, 4096)
             for tn in ([n] if whole_rows else _halvings(n, 512, min(1024, max_tn)))
             if vmem_bytes(tm, tn, tk) <= MATMUL_VMEM_BUDGET]
    return min(cands)[2:]


def matmul(name, a, b, *, ta=False, tb=False, outs=(f32,), epi=None, extras=(), rows=(), row_sums=0, deps=(),
           out_blocks=0):
    m, k = (a.shape[1], a.shape[0]) if ta else a.shape
    n = b.shape[0] if tb else b.shape[1]
    per_out = sum(jnp.dtype(dt).itemsize for dt in outs) + sum(e[0].dtype.itemsize for e in extras)
    whole_rows = bool(rows) or row_sums > 0
    tm, tn, tk = _matmul_tiles(m, n, k, a.dtype.itemsize, b.dtype.itemsize, per_out,
                               n // out_blocks if out_blocks else n, whole_rows)
    nk = k // tk
    ne, nr, nd, no = len(extras), len(rows), len(deps), len(outs)
    dims = (((0 if ta else 1,), (1 if tb else 0,)), ((), ()))

    def body(*refs):
        a_ref, b_ref = refs[0], refs[1]
        e_refs = refs[2:2 + ne]
        r_refs = refs[2 + ne:2 + ne + nr]
        o_refs = refs[2 + ne + nr + nd:2 + ne + nr + nd + no]
        s_refs = refs[2 + ne + nr + nd + no:2 + ne + nr + nd + no + row_sums]
        i, kk = pl.program_id(0), pl.program_id(2)
        part = lax.dot_general(a_ref[...].astype(bf16), b_ref[...].astype(bf16), dims, preferred_element_type=f32)

        def finish(total):
            res = (total,) if epi is None else epi(total, *[e[...] for e in e_refs], *[r[...] for r in r_refs])
            for r, v in zip(o_refs, res[:no]):
                r[...] = v.astype(r.dtype)
            for r, v in zip(s_refs, res[no:]):
                v8 = jnp.broadcast_to(v, r.shape)

                @pl.when(i == 0)
                def _(r=r, v8=v8):
                    r[...] = v8

                @pl.when(i > 0)
                def _(r=r, v8=v8):
                    r[...] += v8

        if nk == 1:
            finish(part)
            return
        acc = refs[-1]

        @pl.when(kk == 0)
        def _():
            acc[...] = part

        @pl.when(jnp.logical_and(kk > 0, kk < nk - 1))
        def _():
            acc[...] += part

        @pl.when(kk == nk - 1)
        def _():
            finish(acc[...] + part)

    a_spec = pl.BlockSpec((tk, tm), lambda i, j, q: (q, i)) if ta else pl.BlockSpec((tm, tk), lambda i, j, q: (i, q))
    b_spec = pl.BlockSpec((tn, tk), lambda i, j, q: (j, q)) if tb else pl.BlockSpec((tk, tn), lambda i, j, q: (q, j))
    def e_spec(e):
        if len(e) == 3:
            return pl.BlockSpec((tm, e[2]), lambda i, j, q, cb=e[1] // e[2]: (i, cb))
        assert e[1] % tn == 0
        return pl.BlockSpec((tm, tn), lambda i, j, q, off=e[1] // tn: (i, off + j))

    e_specs = [e_spec(e) for e in extras]
    r_specs = [pl.BlockSpec((1, tn), lambda i, j, q: (0, j)) for _ in rows]
    if out_blocks:
        per = n // out_blocks // tn
        out_spec = pl.BlockSpec((None, tm, tn), lambda i, j, q: (j // per, i, j % per))
        out_dims = (out_blocks, m, n // out_blocks)
    else:
        out_spec = pl.BlockSpec((tm, tn), lambda i, j, q: (i, j))
        out_dims = (m, n)
    res = pl.pallas_call(
        body, name=name, grid=(m // tm, n // tn, nk),
        in_specs=[a_spec, b_spec] + e_specs + r_specs + [pl.BlockSpec(memory_space=pl.ANY) for _ in deps],
        out_specs=[out_spec for _ in outs] + [pl.BlockSpec((8, tn), lambda i, j, q: (0, j))] * row_sums,
        out_shape=[jax.ShapeDtypeStruct(out_dims, dt) for dt in outs] + [jax.ShapeDtypeStruct((8, n), f32)] * row_sums,
        scratch_shapes=[pltpu.VMEM((tm, tn), f32)] if nk > 1 else [],
        compiler_params=_cp(("arbitrary" if row_sums else "parallel", "parallel", "arbitrary")),
    )(a, b, *[e[0] for e in extras], *rows, *deps)
    return res[0] if len(res) == 1 else tuple(res)


def merge_fwd(name, ys, wbs, u):
    rows, n_out = ys[0].shape[0], wbs[0].shape[1]
    nb = len(ys)
    tm, tn = _pick(rows, 512), _pick(n_out, 512)

    def body(*refs):
        y_refs, w_refs, g_refs = refs[:nb], refs[nb:2 * nb], refs[2 * nb:3 * nb]
        m_ref, p_refs = refs[3 * nb], refs[3 * nb + 1:]
        total = None
        for y_ref, w_ref, g_ref, p_ref in zip(y_refs, w_refs, g_refs, p_refs):
            pre = jnp.dot(y_ref[...], w_ref[...], preferred_element_type=f32)
            p_ref[...] = pre.astype(p_ref.dtype)
            term = jax.nn.sigmoid(g_ref[...].astype(f32)) * pre
            total = term if total is None else total + term
        m_ref[...] = total.astype(m_ref.dtype)

    in_specs = [pl.BlockSpec((tm, y.shape[1]), lambda i, j: (i, 0)) for y in ys]
    in_specs += [pl.BlockSpec((w.shape[0], tn), lambda i, j: (0, j)) for w in wbs]
    in_specs += [pl.BlockSpec((tm, tn), lambda i, j, off=n * (n_out // tn): (i, off + j)) for n in range(nb)]
    out_spec = pl.BlockSpec((tm, tn), lambda i, j: (i, j))
    res = pl.pallas_call(
        body, name=name, grid=(rows // tm, n_out // tn), in_specs=in_specs, out_specs=[out_spec] * (nb + 1),
        out_shape=[jax.ShapeDtypeStruct((rows, n_out), bf16)] * (nb + 1),
        compiler_params=_cp(("parallel", "parallel")),
    )(*ys, *wbs, *([u] * nb))
    return res[0], list(res[1:])


ATT_SCALE = (QK_NOPE + QK_ROPE) ** -0.5
LN2 = 0.6931471805599453
ATT_C = ATT_SCALE / LN2
NT = (((1,), (1,)), ((), ()))
TN = (((0,), (0,)), ((), ()))


def _causal(tq, tk):
    return lax.broadcasted_iota(jnp.int32, (tq, tk), 0) >= lax.broadcasted_iota(jnp.int32, (tq, tk), 1)


def _tri_pairs(n, by_column):
    if by_column:
        pairs = [(i, j) for j in range(n) for i in range(j, n)]
    else:
        pairs = [(i, j) for i in range(n) for j in range(i + 1)]
    return (jnp.asarray([a for a, _ in pairs], jnp.int32), jnp.asarray([b for _, b in pairs], jnp.int32))


FWD_HEADS_PER_STEP = 8
HEADS_PER_STEP = 4
HEAD_PAIR = HEADS_PER_STEP * LANE


def attn_fwd(q, k, v, t):
    rows = q.shape[0]
    n = rows // t
    it, jt = _tri_pairs(n, False)

    def body(it_ref, jt_ref, q_ref, k_ref, v_ref, o_ref, lse_ref, m_s, l_s, acc_s):
        s_id = pl.program_id(1)
        i, j = it_ref[s_id], jt_ref[s_id]

        @pl.when(j == 0)
        def _():
            m_s[...] = jnp.full_like(m_s, -jnp.inf)
            l_s[...] = jnp.zeros_like(l_s)
            acc_s[...] = jnp.zeros_like(acc_s)

        def step(diag):
            for hh in range(FWD_HEADS_PER_STEP):
                sl = slice(LANE * hh, LANE * (hh + 1))
                s = lax.dot_general(q_ref[:, sl], k_ref[:, sl], NT, preferred_element_type=f32)
                if diag:
                    s = jnp.where(_causal(t, t), s, -jnp.inf)
                m_prev = m_s[:, sl]
                m_new = jnp.maximum(m_prev, jnp.max(s, axis=1, keepdims=True))
                alpha = jnp.exp2(m_prev - m_new)
                p = jnp.exp2(s - m_new[:, :1])
                l_s[:, sl] = alpha * l_s[:, sl] + jnp.sum(p, axis=1, keepdims=True)
                acc_s[:, sl] = alpha * acc_s[:, sl] + jnp.dot(p.astype(bf16), v_ref[:, sl], preferred_element_type=f32)
                m_s[:, sl] = m_new

        pl.when(j < i)(lambda: step(False))

        @pl.when(j == i)
        def _():
            step(True)
            o_ref[...] = (acc_s[...] / l_s[...]).astype(o_ref.dtype)
            lse_ref[...] = m_s[...] + jnp.log2(l_s[...])

    width = FWD_HEADS_PER_STEP * LANE
    qs = pl.BlockSpec((t, width), lambda h, s, it_, jt_: (it_[s], h))
    ks = pl.BlockSpec((t, width), lambda h, s, it_, jt_: (jt_[s], h))
    hw = N_HEADS * LANE
    return pl.pallas_call(
        body, name="attn_fwd",
        grid_spec=pltpu.PrefetchScalarGridSpec(
            num_scalar_prefetch=2, grid=(hw // width, it.shape[0]), in_specs=[qs, ks, ks], out_specs=[qs, qs],
            scratch_shapes=[pltpu.VMEM((t, width), f32)] * 3),
        out_shape=[jax.ShapeDtypeStruct((rows, hw), bf16), jax.ShapeDtypeStruct((rows, hw), f32)],
        compiler_params=_cp(("parallel", "arbitrary")),
    )(it, jt, q, k, v)


def attn_bwd(q, k, v, do, o, lse, t):
    rows = q.shape[0]
    n = rows // t
    it, jt = _tri_pairs(n, True)

    def body(it_ref, jt_ref, q_ref, k_ref, v_ref, do_ref, o_ref, lse_ref, dq_ref, dk_ref, dv_ref, dk_s, dv_s):
        s_id = pl.program_id(1)
        i, j = it_ref[s_id], jt_ref[s_id]

        @pl.when(s_id == 0)
        def _():
            dq_ref[...] = jnp.zeros_like(dq_ref)

        @pl.when(i == j)
        def _():
            dk_s[...] = jnp.zeros_like(dk_s)
            dv_s[...] = jnp.zeros_like(dv_s)

        q_rows = pl.ds(pl.multiple_of(i * t, t), t)

        def step(diag):
            for hh in range(HEADS_PER_STEP):
                sl = slice(LANE * hh, LANE * (hh + 1))
                qh, kh, vh, doh = q_ref[:, sl], k_ref[:, sl], v_ref[:, sl], do_ref[:, sl]
                s = lax.dot_general(qh, kh, NT, preferred_element_type=f32)
                p = jnp.exp2(s - lse_ref[:, sl][:, :1])
                if diag:
                    p = jnp.where(_causal(t, t), p, 0.0)
                dp = lax.dot_general(doh, vh, NT, preferred_element_type=f32)
                delta = jnp.sum(doh.astype(f32) * o_ref[:, sl].astype(f32), axis=1, keepdims=True)
                ds = (p * (dp - delta) * LN2).astype(bf16)
                dv_s[:, sl] += lax.dot_general(p.astype(bf16), doh, TN, preferred_element_type=f32)
                dk_s[:, sl] += lax.dot_general(ds, qh, TN, preferred_element_type=f32)
                dq_ref[q_rows, sl] += jnp.dot(ds, kh, preferred_element_type=f32)

        pl.when(i > j)(lambda: step(False))
        pl.when(i == j)(lambda: step(True))

        @pl.when(i == n - 1)
        def _():
            dk_ref[...] = dk_s[...]
            dv_ref[...] = dv_s[...]

    qs = pl.BlockSpec((t, HEAD_PAIR), lambda h, s, it_, jt_: (it_[s], h))
    ks = pl.BlockSpec((t, HEAD_PAIR), lambda h, s, it_, jt_: (jt_[s], h))
    dqs = pl.BlockSpec((rows, HEAD_PAIR), lambda h, s, it_, jt_: (0, h))
    hw = N_HEADS * LANE
    return pl.pallas_call(
        body, name="attn_bwd",
        grid_spec=pltpu.PrefetchScalarGridSpec(
            num_scalar_prefetch=2, grid=(hw // HEAD_PAIR, it.shape[0]), in_specs=[qs, ks, ks, qs, qs, qs],
            out_specs=[dqs, ks, ks], scratch_shapes=[pltpu.VMEM((t, HEAD_PAIR), f32)] * 2),
        out_shape=[jax.ShapeDtypeStruct((rows, hw), f32)] * 3,
        compiler_params=_cp(("parallel", "arbitrary")),
    )(it, jt, q, k, v, do, o, lse)


def _steps(tm):
    k, out = 1, []
    while k < tm:
        out.append(k)
        k *= 2
    return out


def _gelu_gate(h, g):
    return h * jax.nn.gelu(g)


def scan_fwd(a, u, gate, tm):
    rows, ch = a.shape
    n = rows // tm

    def body(a_ref, u_ref, gt_ref, h_ref, y_ref, h_s):
        @pl.when(pl.program_id(0) == 0)
        def _():
            h_s[...] = jnp.zeros_like(h_s)

        av, bv = a_ref[...], u_ref[...]
        row = lax.broadcasted_iota(jnp.int32, av.shape, 0)
        for k in _steps(tm):
            a_sh = jnp.where(row >= k, pltpu.roll(av, k, 0), 1.0)
            b_sh = jnp.where(row >= k, pltpu.roll(bv, k, 0), 0.0)
            bv = av * b_sh + bv
            av = av * a_sh
        h = bv + av * h_s[HALO - 1:HALO, :]
        h_ref[...] = h
        y_ref[...] = _gelu_gate(h, gt_ref[...].astype(f32)).astype(y_ref.dtype)
        h_s[...] = h[tm - HALO:, :]

    spec = pl.BlockSpec((tm, ch), lambda i: (i, 0))
    gt_spec = pl.BlockSpec((tm, gate[1]), lambda i: (i, gate[2]))
    return pl.pallas_call(
        body, name="lru_scan_fwd", grid=(n,), in_specs=[spec, spec, gt_spec], out_specs=[spec, spec],
        out_shape=[jax.ShapeDtypeStruct((rows, ch), f32), jax.ShapeDtypeStruct((rows, ch), bf16)],
        scratch_shapes=[pltpu.VMEM((HALO, ch), f32)], compiler_params=_cp(("arbitrary",)),
    )(a, u, gate[0])


def scan_bwd(a, h, gate, dy, tm):
    rows, ch = a.shape
    n = rows // tm
    per = tm // HALO

    def body(a_ref, h_ref, hp_ref, gt_ref, dy_ref, da_ref, du_ref, dg_ref, g_s, a_s):
        i = pl.program_id(0)
        step = n - 1 - i

        @pl.when(i == 0)
        def _():
            g_s[...] = jnp.zeros_like(g_s)
            a_s[...] = jnp.zeros_like(a_s)

        _, vjp = jax.vjp(_gelu_gate, h_ref[...], gt_ref[...].astype(f32))
        dh, dgate = vjp(dy_ref[...].astype(f32))
        dg_ref[...] = dgate.astype(dg_ref.dtype)
        a0 = a_ref[...]
        row = lax.broadcasted_iota(jnp.int32, a0.shape, 0)
        av = jnp.where(row < tm - 1, pltpu.roll(a0, tm - 1, 0), a_s[0:1, :])
        bv = dh
        for k in _steps(tm):
            a_sh = jnp.where(row < tm - k, pltpu.roll(av, tm - k, 0), 1.0)
            b_sh = jnp.where(row < tm - k, pltpu.roll(bv, tm - k, 0), 0.0)
            bv = bv + av * b_sh
            av = av * a_sh
        g = bv + av * g_s[0:1, :]
        h_last = jnp.where(step > 0, hp_ref[HALO - 1:HALO, :], 0.0)
        h_prev = jnp.where(row >= 1, pltpu.roll(h_ref[...], 1, 0), h_last)
        du_ref[...] = g
        da_ref[...] = g * h_prev
        g_s[...] = g[0:HALO, :]
        a_s[...] = a0[0:HALO, :]

    spec = pl.BlockSpec((tm, ch), lambda i: (n - 1 - i, 0))
    hp_spec = pl.BlockSpec((HALO, ch), lambda i: (jnp.maximum((n - 1 - i) * per - 1, 0), 0))
    gt_spec = pl.BlockSpec((tm, gate[1]), lambda i: (n - 1 - i, gate[2]))
    return pl.pallas_call(
        body, name="lru_scan_bwd", grid=(n,), in_specs=[spec, spec, hp_spec, gt_spec, spec], out_specs=[spec, spec, spec],
        out_shape=[jax.ShapeDtypeStruct((rows, ch), f32)] * 2 + [jax.ShapeDtypeStruct((rows, ch), bf16)],
        scratch_shapes=[pltpu.VMEM((HALO, ch), f32)] * 2,
        compiler_params=_cp(("arbitrary",)),
    )(a, h, h, gate[0], dy)


def _rms(x, g):
    return x * lax.rsqrt(jnp.mean(x * x, axis=-1, keepdims=True) + EPS) * g


def f_rms(step, p, c, x):
    return [_rms(x[0], p[0])], []


def _rope_swap(x):
    lane = lax.broadcasted_iota(jnp.int32, x.shape, 1)
    half = QK_ROPE // 2
    sw = jnp.where(lane < KR_LANE + half, pltpu.roll(x, LANE - half, 1), pltpu.roll(x, half, 1))
    return jnp.where(jnp.logical_and(lane >= KR_LANE, lane < KR_LANE + QK_ROPE), sw, 0.0)


def _rope(x, cosf, sinf):
    return x * cosf + _rope_swap(x) * sinf


def _heads(x):
    return [x[:, LANE * h:LANE * (h + 1)] for h in range(x.shape[1] // LANE)]


def q_rope_epi(q, cosf, sinf):
    return (jnp.concatenate([_rope(b, cosf, sinf) * ATT_C for b in _heads(q)], axis=1),)


def k_rope_epi(kn, kr, cosf, sinf):
    kr_rot = _rope(kr.astype(f32), cosf, sinf)
    return (jnp.concatenate([b + kr_rot for b in _heads(kn)], axis=1),)


def rope_bwd(dqr, dkr, cosf, sinf):
    back = lambda g: g * cosf + _rope_swap(g * sinf)
    dq = jnp.concatenate([back(b) * ATT_C for b in _heads(dqr)], axis=1)
    dkrope = back(sum(_heads(dkr)))
    return dq, dkr, dkrope


def _conv(tail, x, w, b):
    xf = jnp.concatenate([tail, x], axis=0)
    acc = b + w[CONV_W - 1:CONV_W, :] * xf
    for k in range(CONV_W - 1):
        acc = acc + w[k:k + 1, :] * shift_down(xf, CONV_W - 1 - k)
    return acc[HALO:, :]


def f_pool(step, p, c, x):
    wp, sc = p
    (tail,) = c
    (u,) = x
    tm = u.shape[0]
    xf = jnp.concatenate([tail, u], axis=0)
    sums, s, w = [], xf, 1
    while w < POOL_WINDOWS[-1]:
        s = s + shift_down(s, w)
        w *= 2
        sums.append(s)
    t = step * tm + lax.broadcasted_iota(jnp.int32, (tm, 1), 0)
    ys = []
    for g, (w, s) in enumerate(zip(POOL_WINDOWS, sums)):
        sl = slice(LANE * g, LANE * (g + 1))
        cnt = jnp.minimum(t + 1, w).astype(f32)
        d = s[POOL_HALO:, sl] / cnt - u[:, sl]
        ys.append(jnp.dot(d.astype(bf16), wp[LANE * g:LANE * (g + 1), :].astype(bf16), preferred_element_type=f32))
    return [jnp.concatenate(ys, axis=1) * sc], [u[tm - POOL_HALO:, :]]


def f_ssd_tile(step, p, c, x):
    outs = []
    for k in range(x[0].shape[0] // SSD_CHUNK):
        o, c = f_ssd(step, p, c, [t[SSD_CHUNK * k:SSD_CHUNK * (k + 1), :] for t in x])
        outs.append(o[0])
    return [jnp.concatenate(outs, axis=0)], c


def f_ssd(step, p, c, x):
    conv_w, conv_b, dtb, alog, dsk, ng = p
    tail, s_in = c[0], c[1:]
    z, xbc, dt = x
    ln = z.shape[0]
    xc = jax.nn.silu(_conv(tail, xbc, conv_w, conv_b))
    xs, bb, cc = xc[:, :MIX], xc[:, MIX:MIX + LANE], xc[:, MIX + LANE:]
    dtv = jax.nn.softplus(dt + dtb[0:1, :])
    a = dtv * -jnp.exp(alog[0:1, :])
    ri = lax.broadcasted_iota(jnp.int32, (ln, ln), 0)
    ci = lax.broadcasted_iota(jnp.int32, (ln, ln), 1)
    tril = (ri >= ci).astype(f32)
    triu = (ri <= ci).astype(f32)
    hi = lax.Precision.HIGHEST
    a_cs = jnp.dot(tril, a, precision=hi, preferred_element_type=f32)
    a_cs_t = lax.dot_general(a, triu, TN, precision=hi, preferred_element_type=f32)
    a_tot = jnp.sum(a, axis=0, keepdims=True)
    lane = lax.broadcasted_iota(jnp.int32, (1, LANE), 1)
    half = [(lane < 64).astype(f32), (lane >= 64).astype(f32)]
    hrow = lax.broadcasted_iota(jnp.int32, (LANE, 1), 0)

    def head(v, h):
        return jnp.sum(v * (lane == h).astype(f32), axis=1, keepdims=True)

    def pair(v, j):
        return head(v, 2 * j) * half[0] + head(v, 2 * j + 1) * half[1]

    cg = [(cc * half[g]).astype(bf16) for g in range(2)]
    bg = [(bb * half[g]).astype(bf16) for g in range(2)]
    cb = [lax.dot_general(cg[g], bg[g], NT, preferred_element_type=f32) for g in range(2)]
    ys, s_out = [], []
    for j in range(4):
        g = j // 2
        xs_j = xs[:, LANE * j:LANE * (j + 1)]
        xj = xs_j * pair(dtv, j)
        yj = xs_j * pair(dsk[0:1, :], j)
        for hh in range(2):
            h = 2 * j + hh
            rowv = jnp.sum(a_cs_t * (hrow == h).astype(f32), axis=0, keepdims=True)
            lmat = jnp.exp(jnp.where(ri >= ci, head(a_cs, h) - rowv, -jnp.inf))
            yj = yj + jnp.dot((cb[g] * lmat).astype(bf16), (xj * half[hh]).astype(bf16), preferred_element_type=f32)
        acs = pair(a_cs, j)
        tot = pair(a_tot, j)
        yj = yj + jnp.exp(acs) * jnp.dot(cg[g], s_in[j].astype(bf16), preferred_element_type=f32)
        s_new = jnp.exp(tot) * s_in[j] + lax.dot_general(bg[g], (xj * jnp.exp(tot - acs)).astype(bf16), TN,
                                                         preferred_element_type=f32)
        ys.append(yj)
        s_out.append(s_new)
    y = jnp.concatenate(ys, axis=1) * jax.nn.silu(z)
    return [_rms(y, ng)], [xbc[ln - HALO:, :]] + s_out


def _neg_expm1(y):
    series = -y * (1.0 + y * (0.5 + y * (1.0 / 6 + y * (1.0 / 24 + y * (1.0 / 120)))))
    return jnp.where(y > -0.05, series, 1.0 - jnp.exp(y))


def f_lru_pre(step, p, c, x):
    cw, cb_, wa, ba, wi, bi, lam = p
    (tail,) = c
    (lx,) = x
    tm = lx.shape[0]
    xc = _conv(tail, lx, cw, cb_)
    xb = xc.astype(bf16)
    r = jax.nn.sigmoid(jnp.dot(xb, wa.astype(bf16), preferred_element_type=f32) + ba)
    it = jax.nn.sigmoid(jnp.dot(xb, wi.astype(bf16), preferred_element_type=f32) + bi)
    log_a = -LRU_C * r * jax.nn.softplus(-lam)
    mult = jnp.sqrt(_neg_expm1(2.0 * log_a))
    return [jnp.exp(log_a), xc * it * mult], [lx[tm - HALO:, :]]


def loss_head(x, tgt, g, tm):
    rows, d = x.shape
    n = rows // tm

    def body(x_ref, t_ref, g_ref, loss_ref, dx_ref, dg_ref):
        @pl.when(pl.program_id(0) == 0)
        def _():
            loss_ref[...] = jnp.zeros_like(loss_ref)
            dg_ref[...] = jnp.zeros_like(dg_ref)

        def fn(gv, xv):
            err = _rms(xv, gv) - t_ref[...]
            return 0.5 * jnp.sum(jnp.mean(err * err, axis=-1, keepdims=True))

        val, (dg, dx) = jax.value_and_grad(fn, argnums=(0, 1))(g_ref[...], x_ref[...])
        loss_ref[...] += val
        dg_ref[...] += dg
        dx_ref[...] = dx

    spec = pl.BlockSpec((tm, d), lambda i: (i, 0))
    return pl.pallas_call(
        body, name="loss_head", grid=(n,), in_specs=[spec, spec, _const_spec((1, d))],
        out_specs=[_const_spec((8, LANE)), spec, _const_spec((1, d))],
        out_shape=[jax.ShapeDtypeStruct((8, LANE), f32), jax.ShapeDtypeStruct((rows, d), f32),
                   jax.ShapeDtypeStruct((1, d), f32)],
        compiler_params=_cp(("arbitrary",)),
    )(x, tgt, g)


def ew(name, fn, ins, outs, tm):
    rows = ins[0][0].shape[0]
    ni = len(ins)

    def body(*refs):
        res = fn(*[r[...].astype(f32) for r in refs[:ni]])
        for r, v in zip(refs[ni:], res):
            r[...] = v.astype(r.dtype)

    return pl.pallas_call(
        body, name=name, grid=(rows // tm,), in_specs=[_tile_spec(tm, w, cb) for (_, w, cb) in ins],
        out_specs=[_tile_spec(tm, w, 0) for (w, _) in outs],
        out_shape=[jax.ShapeDtypeStruct((rows, w), dt) for (w, dt) in outs],
        compiler_params=_cp(("parallel",)),
    )(*[t[0] for t in ins])


def _peers():
    x, y, c = lax.axis_index("x"), lax.axis_index("y"), lax.axis_index("c")
    me = 4 * x + 2 * y + c
    out = []
    for k in range(1, N_DEV):
        px = 1 - x if k & 4 else x
        py = 1 - y if k & 2 else y
        pc = 1 - c if k & 1 else c
        out.append(((px, py, pc), 4 * px + 2 * py + pc))
    return me, out


_HBM = pl.BlockSpec(memory_space=pltpu.HBM)
_SEM = pl.BlockSpec(memory_space=pltpu.SEMAPHORE)
_EFFECT = pltpu.SideEffectType.DATAFLOW_SIDE_EFFECTING


def _remote(src_ref, land_ref, gather, me, pid, dev, send_sems, recv_sems, k, recv_side):
    return pltpu.make_async_remote_copy(
        src_ref=src_ref if gather else src_ref.at[pid], dst_ref=land_ref.at[pid if recv_side else me],
        send_sem=send_sems.at[k], recv_sem=recv_sems.at[k], device_id=dev, device_id_type=pl.DeviceIdType.MESH)


def _own(src_ref, land_ref, gather, me, sem):
    return pltpu.make_async_copy(src_ref if gather else src_ref.at[me], land_ref.at[me], sem)


def exchange_start(name, srcs, gather, deps=()):
    n, nd = len(srcs), len(deps)
    shapes = [(s.shape if gather else s.shape[1:]) for s in srcs]
    lands = [lax.empty((N_DEV,) + tuple(sh), s.dtype) for s, sh in zip(srcs, shapes)]

    def body(*refs):
        src_refs, land_refs = refs[:n], refs[n:2 * n]
        send_sems, recv_sems, own_sem = refs[2 * n + nd:2 * n + nd + 3]
        token = refs[-1]
        me, peers = _peers()
        for k, (dev, pid) in enumerate(peers):
            for s_ref, l_ref in zip(src_refs, land_refs):
                _remote(s_ref, l_ref, gather, me, pid, dev, send_sems, recv_sems, k, False).start()
        for s_ref, l_ref in zip(src_refs, land_refs):
            _own(s_ref, l_ref, gather, me, own_sem).start()
        token[...] = jnp.zeros_like(token)

    hbm = lambda a: pltpu.with_memory_space_constraint(a, pltpu.HBM)
    res = pl.pallas_call(
        body, name=name,
        out_shape=(pltpu.SemaphoreType.DMA((N_DEV - 1,)), pltpu.SemaphoreType.DMA((N_DEV - 1,)), pltpu.SemaphoreType.DMA(()),
                   *[pltpu.HBM(a.shape, a.dtype) for a in list(srcs) + lands], jax.ShapeDtypeStruct((8, LANE), f32)),
        in_specs=[_HBM] * (2 * n) + [pl.BlockSpec(memory_space=pl.ANY)] * nd,
        out_specs=(_SEM, _SEM, _SEM, *([_HBM] * (2 * n)), pl.BlockSpec(memory_space=pltpu.VMEM)),
        input_output_aliases={i: 3 + i for i in range(2 * n)},
        compiler_params=pltpu.CompilerParams(has_side_effects=_EFFECT),
    )(*[hbm(a) for a in list(srcs) + lands], *deps)
    return dict(sems=res[:3], srcs=list(res[3:3 + n]), lands=list(res[3 + n:3 + 2 * n]), token=res[-1], gather=gather)


def exchange_wait(name, h, afters):
    n, gather = len(h["srcs"]), h["gather"]

    def body(*refs):
        src_refs, land_refs = refs[:n], refs[n:2 * n]
        send_sems, recv_sems, own_sem = refs[2 * n:2 * n + 3]
        me, peers = _peers()
        for k, (dev, pid) in enumerate(peers):
            for s_ref, l_ref in zip(src_refs, land_refs):
                _remote(s_ref, l_ref, gather, me, pid, dev, send_sems, recv_sems, k, True).wait_recv()
        for k, (dev, pid) in enumerate(peers):
            for s_ref, l_ref in zip(src_refs, land_refs):
                _remote(s_ref, l_ref, gather, me, pid, dev, send_sems, recv_sems, k, False).wait_send()
        for s_ref, l_ref in zip(src_refs, land_refs):
            _own(s_ref, l_ref, gather, me, own_sem).wait()

    arrs = h["srcs"] + h["lands"]
    res = pl.pallas_call(
        body, name=name, out_shape=tuple(pltpu.HBM(a.shape, a.dtype) for a in arrs),
        in_specs=[_HBM] * (2 * n) + [_SEM, _SEM, _SEM] + [pl.BlockSpec(memory_space=pl.ANY)] * len(afters),
        out_specs=tuple([_HBM] * (2 * n)), input_output_aliases={i: i for i in range(2 * n)},
        compiler_params=pltpu.CompilerParams(has_side_effects=_EFFECT),
    )(*arrs, *h["sems"], *afters)
    return list(res[n:])


def _adam_update(g, w, m, v):
    mn = ADAM_B1 * m + (1.0 - ADAM_B1) * g
    vn = ADAM_B2 * v + (1.0 - ADAM_B2) * jnp.square(g)
    m_hat = mn / (1.0 - ADAM_B1 ** ADAM_STEP)
    v_hat = vn / (1.0 - ADAM_B2 ** ADAM_STEP)
    return -ADAM_LR * (m_hat / (jnp.sqrt(v_hat) + ADAM_EPS) + ADAM_WD * w), mn, vn


def _adamw_vectors(name, parts, w, m, v):
    nl = len(parts)

    def body(*refs):
        p_refs = refs[:nl]
        w_ref, m_ref, v_ref, g_ref, d_ref, nm_ref, nv_ref = refs[nl:]
        for ll, p_ref in enumerate(p_refs):
            row = slice(ll, ll + 1)
            g = p_ref[0:1, :]
            for i in range(1, N_DEV):
                g = g + p_ref[i:i + 1, :]
            delta, mn, vn = _adam_update(g, w_ref[row, :], m_ref[row, :], v_ref[row, :])
            g_ref[row, :] = g
            d_ref[row, :] = delta
            nm_ref[row, :] = mn
            nv_ref[row, :] = vn

    return list(pl.pallas_call(body, name=name, out_shape=[jax.ShapeDtypeStruct(w.shape, f32)] * 4)(*parts, w, m, v))


def adamw_packed(name, gots, where, ws, ms, vs):
    ng, npar = len(gots), len(ws)

    def body(*refs):
        g_refs = refs[:ng]
        w_refs, m_refs, v_refs = (refs[ng + k * npar:ng + (k + 1) * npar] for k in range(3))
        o_refs = refs[ng + 3 * npar:]
        for p in range(npar):
            width = w_refs[p].shape[1]
            for l, (which, off) in enumerate(where[p]):
                row = slice(l, l + 1)
                cols = slice(off, off + width)
                g = g_refs[which][0:1, cols]
                for i in range(1, N_DEV):
                    g = g + g_refs[which][i:i + 1, cols]
                delta, mn, vn = _adam_update(g, w_refs[p][row, :], m_refs[p][row, :], v_refs[p][row, :])
                for k, val in enumerate((g, delta, mn, vn)):
                    o_refs[4 * p + k][row, :] = val

    out_shape = [jax.ShapeDtypeStruct(w.shape, f32) for w in ws for _ in range(4)]
    res = pl.pallas_call(body, name=name, out_shape=out_shape, compiler_params=_cp(()))(*gots, *ws, *ms, *vs)
    return [list(res[4 * p:4 * p + 4]) for p in range(npar)]


def adamw_columns(name, parts, w, m, v):
    nl, kk, cc = w.shape
    view = lambda a: jnp.transpose(a, (2, 0, 1))
    tc = min(LANE, cc)

    def body(*refs):
        p_refs = refs[:nl]
        w_ref, m_ref, v_ref, g_ref, d_ref, nm_ref, nv_ref = refs[nl:]
        for l, p_ref in enumerate(p_refs):
            g = p_ref[0].astype(f32)
            for i in range(1, N_DEV):
                g = g + p_ref[i].astype(f32)
            g = g.T
            delta, mn, vn = _adam_update(g, w_ref[:, l, :], m_ref[:, l, :], v_ref[:, l, :])
            g_ref[:, l, :] = g
            d_ref[:, l, :] = delta
            nm_ref[:, l, :] = mn
            nv_ref[:, l, :] = vn

    p_spec = pl.BlockSpec((N_DEV, kk, tc), lambda j: (0, 0, j))
    w_spec = pl.BlockSpec((tc, nl, kk), lambda j: (j, 0, 0))
    res = pl.pallas_call(
        body, name=name, grid=(pl.cdiv(cc, tc),), in_specs=[p_spec] * nl + [w_spec] * 3, out_specs=[w_spec] * 4,
        out_shape=[jax.ShapeDtypeStruct((cc, nl, kk), f32)] * 4, compiler_params=_cp(("parallel",)),
    )(*parts, view(w), view(m), view(v))
    return [jnp.transpose(a, (1, 2, 0)) for a in res]


def adamw(name, parts, w, m, v):
    nl = len(parts)
    shape = w.shape[1:]
    c = shape[-1]
    r = 1
    for s in shape[:-1]:
        r *= s
    if r == 1:
        return _adamw_vectors(name, parts, w, m, v)
    tr = _pick(r, 256) if r % 8 == 0 else r
    nb = r // tr
    parts2 = [p.reshape(N_DEV, r, c) for p in parts]
    w2, m2, v2 = (a.reshape(nl, r, c) for a in (w, m, v))

    def body(*refs):
        p_refs = refs[:nl]
        w_ref, m_ref, v_ref, g_ref, d_ref, nm_ref, nv_ref = refs[nl:]
        layer = pl.program_id(0)
        for ll, p_ref in enumerate(p_refs):
            @pl.when(layer == ll)
            def _(p_ref=p_ref):
                g = p_ref[0].astype(f32)
                for i in range(1, N_DEV):
                    g = g + p_ref[i].astype(f32)
                delta, mn, vn = _adam_update(g, w_ref[0], m_ref[0], v_ref[0])
                g_ref[0] = g
                d_ref[0] = delta
                nm_ref[0] = mn
                nv_ref[0] = vn

    def p_spec(ll):
        return pl.BlockSpec((N_DEV, tr, c), lambda l, i: (0, jnp.where(l == ll, i, jnp.where(l > ll, nb - 1, 0)), 0))

    spec = pl.BlockSpec((1, tr, c), lambda l, i: (l, i, 0))
    res = pl.pallas_call(
        body, name=name, grid=(nl, nb), in_specs=[p_spec(ll) for ll in range(nl)] + [spec, spec, spec],
        out_specs=[spec] * 4, out_shape=[jax.ShapeDtypeStruct((nl, r, c), f32)] * 4,
        compiler_params=_cp(("arbitrary", "arbitrary")),
    )(*parts2, w2, m2, v2)
    return [a.reshape(w.shape) for a in res]


_IN_SPLITS = dict(cq=(0, 384), ckv=(384, 640), kr=(640, 672), pool=(672, 1184), z=(1184, 1696), xbc=(1696, 2464),
                  dt=(2464, 2472), lg=(2472, 2984), lx=(2984, 3496), gates=(3496, 7592))


W_IN_SHARD = IN_COLS // N_DEV

_PAD_ORDER = ("gates", "pool", "z", "lg", "lx", "xbc", "cq", KR_LANE, "kr", LANE - KR_LANE - QK_ROPE, "ckv", "dt",
              LANE - 8, U_COLS - U_DT[0] - LANE)
_SEGMENTS = ((0, U_CQ[0], 384), (384, U_CKV[0], 256), (640, U_KR[0] + KR_LANE, QK_ROPE), (672, U_POOL[0], 512),
             (1184, U_Z[0], 512), (1696, U_XBC[0], 768), (2464, U_DT[0], 8), (2472, U_LG[0], 512), (2984, U_LX[0], 512),
             (3496, 0, 4096))


def _pad_w_in(shards):
    rows = shards.shape[1]
    pieces = []
    for item in _PAD_ORDER:
        if isinstance(item, int):
            pieces.append(jnp.zeros((rows, item), shards.dtype))
            continue
        a, b = _IN_SPLITS[item]
        for d in range(a // W_IN_SHARD, (b - 1) // W_IN_SHARD + 1):
            lo, hi = max(a, d * W_IN_SHARD), min(b, (d + 1) * W_IN_SHARD)
            pieces.append(shards[d, :, lo - d * W_IN_SHARD:hi - d * W_IN_SHARD])
    return jnp.concatenate(pieces, axis=1)


def _w_in_blocks(g):
    blocks = []
    for d in range(N_DEV):
        a, b = d * W_IN_SHARD, (d + 1) * W_IN_SHARD
        pieces = []
        for ref, pad, width in _SEGMENTS:
            lo, hi = max(a, ref), min(b, ref + width)
            if lo < hi:
                pieces.append(g[:, pad + lo - ref:pad + hi - ref])
        blocks.append(jnp.concatenate(pieces, axis=1))
    return jnp.stack(blocks).astype(bf16)


def _head_pad_cols(w, per, lo, hi):
    k = w.shape[0]
    w = w.reshape(k, N_HEADS, per)[:, :, lo:hi]
    return jnp.pad(w, ((0, 0), (0, 0), (0, LANE - (hi - lo)))).reshape(k, N_HEADS * LANE)


def _head_unpad_cols(g, n):
    k = g.shape[0]
    return g.reshape(k, N_HEADS, LANE)[:, :, :n]


def _on_diagonal():
    i = lax.broadcasted_iota(jnp.int32, (8, 1, 8, 1), 0)
    j = lax.broadcasted_iota(jnp.int32, (8, 1, 8, 1), 2)
    return i == j


def _block_diag(w):
    w4 = jnp.broadcast_to(w[:, :, None, :], (8, 64, 8, 64))
    return jnp.where(_on_diagonal(), w4, 0.0).reshape(MIX, MIX)


def _block_diag_inv(g):
    return jnp.sum(jnp.where(_on_diagonal(), g.reshape(8, 64, 8, 64), 0.0), axis=2)


def _head8(v):
    return jnp.pad(v[None, :], ((0, 7), (0, LANE - v.shape[0])))


GROUPS = dict(A=("w_in",), B=("w_uq", "w_ukv", "ssd_conv_w", "lru_conv_w", "w_branch", "w_out"),
              C=("w_ff1", "w_ff2", "w_ple_gate", "w_ple"))


def _kernel_weights(grp, fw):
    if grp == "A":
        w_in = _pad_w_in(fw["w_in"])
        return dict(w_in=w_in, w_dt=w_in[:, U_DT[0]:U_DT[0] + LANE])
    if grp == "C":
        return dict(w_ff1=fw["w_ff1"], w_ff2=fw["w_ff2"], w_pg=fw["w_ple_gate"], w_ple=fw["w_ple"])
    wb = fw["w_branch"]
    wb0 = jnp.pad(wb[0].reshape(N_HEADS, V_HEAD, D_MODEL), ((0, 0), (0, LANE - V_HEAD), (0, 0))).reshape(N_HEADS * LANE, D_MODEL)
    return dict(
        w_uq=_head_pad_cols(fw["w_uq"], QK_NOPE + QK_ROPE, 0, QK_NOPE + QK_ROPE),
        w_uk=_head_pad_cols(fw["w_ukv"], QK_NOPE + V_HEAD, 0, QK_NOPE),
        w_uv=_head_pad_cols(fw["w_ukv"], QK_NOPE + V_HEAD, QK_NOPE, QK_NOPE + V_HEAD),
        wb=[wb0, wb[1], wb[2], wb[3]], w_out=fw["w_out"], ssd_conv_w=fw["ssd_conv_w"], lru_conv_w=fw["lru_conv_w"])


def _layer_params(sp, l):
    row = lambda n: sp[n][l][None, :]
    return dict(
        g_mix=row("g_mix"), q_norm=row("q_norm"), kv_norm=row("kv_norm"),
        pool=[sp["w_pool"][l].reshape(4 * LANE, LANE), row("pool_scale")],
        ssd=[None, row("ssd_conv_b"), _head8(sp["ssd_dt_bias"][l]), _head8(sp["ssd_a_log"][l]),
             _head8(sp["ssd_d"][l]), row("ssd_norm")],
        lru=[None, row("lru_conv_b"), _block_diag(sp["lru_w_a"][l]), row("lru_b_a"),
             _block_diag(sp["lru_w_i"][l]), row("lru_b_i"), row("lru_lambda")],
        g_mlp=row("g_mlp"), g_ple=row("g_ple"),
    )


_sig = jax.nn.sigmoid
_SSD_CARRY = [(HALO, SSD_XBC)] + [(LANE, LANE)] * 4


def _tiles(rows):
    return dict(tm=_pick(rows, 1024), ta=_pick(rows, 512), tp=_pick(rows, 512), tl=_pick(rows, 512), ts=_pick(rows, 256),
                tssd=_pick(rows, SSD_CHUNK * SSD_CHUNKS_PER_TILE))


def _mixer_tiles(u, dt32):
    return dict(
        cq=(u, 384, U_CQ[0] // 384), ckv=(u, 256, U_CKV[0] // 256), kr=(u, LANE, U_KR[0] // LANE),
        pool=(u, MIX, U_POOL[0] // MIX), z=(u, MIX, U_Z[0] // MIX), xbc=(u, SSD_XBC, U_XBC[0] // SSD_XBC),
        dt=(dt32, LANE, 0), lg=(u, MIX, U_LG[0] // MIX), lx=(u, MIX, U_LX[0] // MIX))


def _add_norm(acc, resid, g):
    x = acc + resid
    return x, _rms(x, g)


def _layer_fwd(x, h, p_bf, ctx, l, pr, g_next, cosf, sinf, early=()):
    rows = x.shape[0]
    ts = _tiles(rows)
    tm = ts["tm"]
    nm = lambda s: f"{s}_l{l}"
    r = dict(x=x)
    if h is None:
        (h,), _ = seq_fwd(nm("rms_in"), f_rms, [pr["g_mix"]], [(x, D_MODEL, 0)], [], [(D_MODEL, bf16)], tm)
    w = dict(_kernel_weights("A", ctx.weights(l, "A", [h, *early])))
    u = matmul(nm("w_in"), h, w["w_in"], outs=(U_DTYPE,))
    dt32 = matmul(nm("w_dt"), h, w["w_dt"])
    mt = _mixer_tiles(u, dt32)
    (cqn,), _ = seq_fwd(nm("rms_q"), f_rms, [pr["q_norm"]], [mt["cq"]], [], [(Q_LORA, bf16)], tm)
    (ckvn,), _ = seq_fwd(nm("rms_kv"), f_rms, [pr["kv_norm"]], [mt["ckv"]], [], [(KV_LORA, bf16)], tm)
    (yb,), pool_saved = seq_fwd(nm("pool"), f_pool, pr["pool"], [mt["pool"]], [(POOL_HALO, MIX)], [(MIX, bf16)], ts["tp"])
    w.update(_kernel_weights("B", ctx.weights(l, "B", yb)))
    pr = dict(pr, ssd=[w["ssd_conv_w"]] + pr["ssd"][1:], lru=[w["lru_conv_w"]] + pr["lru"][1:])
    tables = [(cosf, 0, LANE), (sinf, 0, LANE)]
    qr = matmul(nm("w_uq"), cqn, w["w_uq"], outs=(bf16,), epi=q_rope_epi, extras=tables)
    kr = matmul(nm("w_uk"), ckvn, w["w_uk"], outs=(bf16,), epi=k_rope_epi, extras=[(u, U_KR[0], LANE)] + tables)
    vb = matmul(nm("w_uv"), ckvn, w["w_uv"], outs=(bf16,))
    o, lse = attn_fwd(qr, kr, vb, ts["ta"])
    (yc,), ssd_saved = seq_fwd(nm("ssd"), f_ssd_tile, pr["ssd"], [mt["z"], mt["xbc"], mt["dt"]], _SSD_CARRY, [(MIX, bf16)],
                               ts["tssd"])
    (la, lu), lru_saved = seq_fwd(nm("lru_pre"), f_lru_pre, pr["lru"], [mt["lx"]], [(HALO, MIX)], [(MIX, f32), (MIX, f32)], ts["tl"])
    hh, yd = scan_fwd(la, lu, mt["lg"], ts["ts"])
    ys = [o, yb, yc, yd]
    m, pres = merge_fwd(nm("merge"), ys, w["wb"], u)
    x1, h2 = matmul(nm("w_out"), m, w["w_out"], outs=(f32, bf16), epi=_add_norm, extras=[(x, 0)], rows=[pr["g_mlp"]])
    w.update(_kernel_weights("C", ctx.weights(l, "C", h2)))
    a1, act = matmul(nm("ff1"), h2, w["w_ff1"], outs=(bf16, bf16), epi=lambda acc: (acc, jnp.square(jnp.maximum(acc, 0.0))))
    x2, h3 = matmul(nm("ff2"), act, w["w_ff2"], outs=(f32, bf16), epi=_add_norm, extras=[(x1, 0)], rows=[pr["g_ple"]])
    gl = matmul(nm("ple_gate"), h3, w["w_pg"])
    if g_next is None:
        x3, pe = matmul(nm("ple"), p_bf, w["w_ple"], outs=(f32, f32), epi=lambda acc, g, xr: (xr + acc * _sig(g), acc),
                        extras=[(gl, 0), (x2, 0)])
        h_next = None
    else:
        def ple_norm(acc, g, xr, gn):
            xo = xr + acc * _sig(g)
            return xo, acc, _rms(xo, gn)

        x3, pe, h_next = matmul(nm("ple"), p_bf, w["w_ple"], outs=(f32, f32, bf16), epi=ple_norm,
                                extras=[(gl, 0), (x2, 0)], rows=[g_next])
    r.update(h=h, u=u, cqn=cqn, ckvn=ckvn, vb=vb, qr=qr, kr=kr, o=o, lse=lse, ys=ys, pres=pres, m=m, x1=x1,
             h2=h2, a1=a1, act=act, x2=x2, h3=h3, gl=gl, pe=pe, p_bf=p_bf, pool_saved=pool_saved, ssd_saved=ssd_saved,
             lru_saved=lru_saved, la=la, hh=hh, w=w, pr=pr, dt32=dt32)
    return x3, h_next, r


def _norm_bwd(dh, x, resid, g):
    rs = lax.rsqrt(jnp.mean(x * x, axis=-1, keepdims=True) + EPS)
    xhat = x * rs
    dxn = dh * g
    dx = rs * (dxn - xhat * jnp.mean(dxn * xhat, axis=-1, keepdims=True)) + resid
    return dx, jnp.sum(dh * xhat, axis=0, keepdims=True)


def _gate_bwd(d, g, pre):
    s = _sig(g.astype(f32))
    return d * s, d * pre.astype(f32) * s * (1.0 - s)


def _layer_bwd(dx3, r, ctx, l, cosf, sinf, tok, extra_small):
    rows = dx3.shape[0]
    ts = _tiles(rows)
    tm = ts["tm"]
    nm = lambda s: f"{s}_l{l}"
    u, w, pr = r["u"], r["w"], r["pr"]
    mt = _mixer_tiles(u, r["dt32"])
    g = {}
    full = lambda a: (a, a.shape[1], 0)
    dpe, dgl = ew(nm("ple_bwd"), _gate_bwd, [full(dx3), full(r["gl"]), full(r["pe"])], [(D_MODEL, bf16)] * 2, tm)
    g["w_ple"] = matmul(nm("d_w_ple"), r["p_bf"], dpe, ta=True, outs=(bf16,), deps=[tok] if tok is not None else [])
    g["w_pg"] = matmul(nm("d_w_pg"), r["h3"], dgl, ta=True, outs=(bf16,))
    dx2, g["g_ple"] = matmul(nm("d_h3"), dgl, w["w_pg"], tb=True, epi=_norm_bwd, extras=[(r["x2"], 0), (dx3, 0)],
                             rows=[pr["g_ple"]], row_sums=1)
    da1 = matmul(nm("d_act"), dx2, w["w_ff2"], tb=True, outs=(bf16,),
                 epi=lambda acc, a: (acc * 2.0 * jnp.maximum(a, 0.0),), extras=[(r["a1"], 0)])
    g["w_ff2"] = matmul(nm("d_w_ff2"), r["act"], dx2, ta=True, outs=(bf16,))
    g["w_ff1"] = matmul(nm("d_w_ff1"), r["h2"], da1, ta=True, outs=(bf16,), out_blocks=N_DEV)
    tok = ctx.grads(l, "C", dict(w_ff1=g["w_ff1"], w_ff2=g["w_ff2"], w_ple_gate=g["w_pg"], w_ple=g["w_ple"]))
    dx1, g["g_mlp"] = matmul(nm("d_h2"), da1, w["w_ff1"], tb=True, epi=_norm_bwd, extras=[(r["x1"], 0), (dx2, 0)],
                             rows=[pr["g_mlp"]], row_sums=1, deps=[tok])
    def merge_bwd(dm, *gates_and_pres):
        both = [_gate_bwd(dm, gates_and_pres[n], gates_and_pres[4 + n]) for n in range(4)]
        return tuple(b[0] for b in both) + tuple(b[1] for b in both)

    res = matmul(nm("d_merged"), dx1, w["w_out"], tb=True, outs=(bf16,) * 8, epi=merge_bwd,
                 extras=[(u, D_MODEL * n) for n in range(4)] + [(pre, 0) for pre in r["pres"]])
    dpres, dgates = list(res[:4]), list(res[4:])
    g["w_out"] = matmul(nm("d_w_out"), r["m"], dx1, ta=True, outs=(bf16,))
    dys, g["wb"] = [], []
    for n in range(4):
        g["wb"].append(matmul(nm(f"d_w_branch{n}"), r["ys"][n], dpres[n], ta=True, outs=(bf16,)))
        dys.append(matmul(nm(f"d_y{n}"), dpres[n], w["wb"][n], tb=True, outs=(bf16 if n == 0 else f32,)))
    dqr, dkr_, dv = attn_bwd(r["qr"], r["kr"], r["vb"], dys[0], r["o"], r["lse"], ts["ta"])
    hw = N_HEADS * LANE
    dq, dkn, dkrope = ew(nm("rope_bwd"), rope_bwd, [full(dqr), full(dkr_), full(cosf), full(sinf)],
                         [(hw, bf16), (hw, bf16), (LANE, bf16)], tm)
    g["w_uq"] = matmul(nm("d_w_uq"), r["cqn"], dq, ta=True, outs=(bf16,))
    g["w_uk"] = matmul(nm("d_w_uk"), r["ckvn"], dkn, ta=True, outs=(bf16,))
    g["w_uv"] = matmul(nm("d_w_uv"), r["ckvn"], dv, ta=True, outs=(bf16,))
    dcqn = matmul(nm("d_cqn"), dq, w["w_uq"], tb=True)
    dckvn = matmul(nm("d_ckvn_k"), dkn, w["w_uk"], tb=True)
    dckvn = matmul(nm("d_ckvn_v"), dv, w["w_uv"], tb=True, epi=lambda acc, prev: (acc + prev,), extras=[(dckvn, 0)])
    (g["q_norm"],), (dcq,) = seq_bwd(nm("rms_q_bwd"), f_rms, [pr["q_norm"]], [mt["cq"]], [True], [], [dcqn], [bf16], tm)
    (g["kv_norm"],), (dckv,) = seq_bwd(nm("rms_kv_bwd"), f_rms, [pr["kv_norm"]], [mt["ckv"]], [True], [], [dckvn], [bf16], tm)
    g["pool"], (dpool,) = seq_bwd(nm("pool_bwd"), f_pool, pr["pool"], [mt["pool"]], [True], r["pool_saved"], [dys[1]],
                                  [bf16], ts["tp"])
    g["ssd"], (dz, dxbc, ddt) = seq_bwd(nm("ssd_bwd"), f_ssd_tile, pr["ssd"], [mt["z"], mt["xbc"], mt["dt"]], [True] * 3,
                                        r["ssd_saved"], [dys[2]], [bf16] * 3, ts["tssd"])
    da, du, dlg = scan_bwd(r["la"], r["hh"], mt["lg"], dys[3], ts["ts"])
    g["lru"], (dlx,) = seq_bwd(nm("lru_pre_bwd"), f_lru_pre, pr["lru"], [mt["lx"]], [True], r["lru_saved"], [da, du],
                               [bf16], ts["tl"])
    dk = _head_unpad_cols(g["w_uk"], QK_NOPE)
    dv_ = _head_unpad_cols(g["w_uv"], V_HEAD)
    wb0 = g["wb"][0].reshape(N_HEADS, LANE, D_MODEL)[:, :V_HEAD].reshape(MIX, D_MODEL)
    ssd, lru, pool = g["ssd"], g["lru"], g["pool"]
    tok = ctx.grads(l, "B", dict(
        w_uq=_head_unpad_cols(g["w_uq"], QK_NOPE + QK_ROPE).reshape(Q_LORA, -1),
        w_ukv=jnp.concatenate([dk, dv_], axis=2).reshape(KV_LORA, -1), ssd_conv_w=ssd[0], lru_conv_w=lru[0],
        w_branch=jnp.stack([wb0, g["wb"][1], g["wb"][2], g["wb"][3]]), w_out=g["w_out"]))
    du_p = jnp.concatenate(dgates + [dpool, dz, dlg, dlx, dxbc, dcq, dkrope, dckv, ddt,
                                     jnp.zeros((rows, U_COLS - U_DT[0] - LANE), bf16)], axis=1)
    small = dict(
        q_norm=g["q_norm"][0], kv_norm=g["kv_norm"][0],
        w_pool=pool[0].reshape(4, LANE, LANE), pool_scale=pool[1][0],
        ssd_conv_b=ssd[1][0], ssd_dt_bias=ssd[2][0, :8], ssd_a_log=ssd[3][0, :8], ssd_d=ssd[4][0, :8], ssd_norm=ssd[5][0],
        lru_conv_b=lru[1][0], lru_w_a=_block_diag_inv(lru[2]), lru_b_a=lru[3][0], lru_w_i=_block_diag_inv(lru[4]),
        lru_b_i=lru[5][0], lru_lambda=lru[6][0], g_mlp=g["g_mlp"][0], g_ple=g["g_ple"][0])
    tok_small = ctx.small(f"l{l}", [(n, l, small[n]) for n in SMALL if n in small] + extra_small)
    g_w_in = matmul(nm("d_w_in"), r["h"], du_p, ta=True, outs=(bf16,), deps=[tok, tok_small])
    tok = ctx.grads(l, "A", dict(w_in=_w_in_blocks(g_w_in)))
    dx, g_mix = matmul(nm("d_h"), du_p, w["w_in"], tb=True, epi=_norm_bwd, extras=[(r["x"], 0), (dx1, 0)],
                       rows=[pr["g_mix"]], row_sums=1, deps=[tok])
    return dx, tok, ("g_mix", l, g_mix[0])


def _rope_tables(positions):
    inv = 1.0 / (ROPE_THETA ** (jnp.arange(0, QK_ROPE, 2, dtype=f32) / QK_ROPE))
    ang = positions.astype(f32)[:, None] * inv
    cos, sin = jnp.cos(ang), jnp.sin(ang)
    rows = positions.shape[0]
    pad = jnp.zeros((rows, LANE - KR_LANE - QK_ROPE), f32)
    cosf = jnp.concatenate([jnp.ones((rows, KR_LANE), f32), cos, cos, pad], axis=1)
    sinf = jnp.concatenate([jnp.zeros((rows, KR_LANE), f32), -sin, sin, pad], axis=1)
    return cosf, sinf


WEIGHTS = ['g_mix', 'w_in', 'q_norm', 'w_uq', 'kv_norm', 'w_ukv', 'w_pool', 'pool_scale', 'ssd_conv_w', 'ssd_conv_b',
           'ssd_dt_bias', 'ssd_a_log', 'ssd_d', 'ssd_norm', 'lru_conv_w', 'lru_conv_b', 'lru_w_a', 'lru_b_a', 'lru_w_i',
           'lru_b_i', 'lru_lambda', 'w_branch', 'w_out', 'g_mlp', 'w_ff1', 'w_ff2', 'g_ple', 'w_ple_gate', 'w_ple', 'g_final']
SHARDED = dict(w_in=2, w_uq=2, w_ukv=2, ssd_conv_w=2, lru_conv_w=2, w_branch=3, w_out=1, w_ff1=2, w_ff2=1,
               w_ple_gate=1, w_ple=2)
F32_PAYLOAD = ("ssd_conv_w", "lru_conv_w")
DEPTH = 2


SMALL = [n for n in WEIGHTS if n not in SHARDED and n != "g_final"]


def local_step(x, p, positions, tgt, sp, ctx):
    cosf, sinf = _rope_tables(positions)
    prs = [_layer_params(sp, l) for l in range(DEPTH)]
    p_bf = [p[l].astype(bf16) for l in range(DEPTH)]
    early = [cosf, sinf, *p_bf] + [a for pr in prs for v in pr.values() for a in (v if isinstance(v, list) else [v])
                                   if a is not None]
    res, h = [], None
    for l in range(DEPTH):
        g_next = prs[l + 1]["g_mix"] if l + 1 < DEPTH else None
        x, h, r = _layer_fwd(x, h, p_bf[l], ctx, l, prs[l], g_next, cosf, sinf, early if l == 0 else ())
        res.append(r)
    loss8, dx, dgf = loss_head(x, tgt, sp["g_final"][None, :], _tiles(x.shape[0])["tm"])
    tok = None
    pending = ("g_final", None, dgf[0])
    for l in reversed(range(DEPTH)):
        dx, tok, pending = _layer_bwd(dx, res[l], ctx, l, cosf, sinf, tok, [pending])
    ctx.small("last", [pending])
    return loss8[0, 0], dx


def _payload(name, w):
    return w if name in F32_PAYLOAD else w.astype(bf16)


def _blocks(name, g):
    ax = SHARDED[name] - 1
    shape = list(g.shape)
    shape[ax:ax + 1] = [N_DEV, shape[ax] // N_DEV]
    return _payload(name, jnp.moveaxis(g.reshape(shape), ax, 0))


def _assemble(name, shards):
    ax = SHARDED[name] - 1
    shape = list(shards.shape[1:])
    shape[ax] *= N_DEV
    return jnp.moveaxis(shards, 0, ax).reshape(shape)


class _Exchanges:
    def __init__(self, wts):
        self.wts = wts
        self.ag, self.rs, self.sm = {}, {}, {}
        tok = None
        for l in range(DEPTH):
            for grp, names in GROUPS.items():
                h = exchange_start(f"ag_start_{grp}{l}", [_payload(n, wts[n][l]) for n in names], True,
                                   deps=[] if tok is None else [tok])
                tok = h["token"]
                self.ag[(l, grp)] = h
        self.all_started = tok

    def weights(self, l, grp, after):
        afters = list(after) if isinstance(after, (list, tuple)) else [after]
        if (l, grp) == (0, "A"):
            afters.append(self.all_started)
        got = exchange_wait(f"ag_wait_{grp}{l}", self.ag[(l, grp)], afters)
        out = {}
        for n, a in zip(GROUPS[grp], got):
            out[n] = a if n == "w_in" else _assemble(n, a)
        return out

    def grads(self, l, grp, g):
        cut = lambda n: g[n].ndim == self.wts[n].ndim
        h = exchange_start(f"rs_start_{grp}{l}", [g[n] if cut(n) else _blocks(n, g[n]) for n in GROUPS[grp]], False)
        self.rs[(l, grp)] = h
        return h["token"]

    def small(self, tag, entries):
        entries = sorted(entries, key=lambda e: e[2].size % LANE != 0)
        flat = jnp.concatenate([a.reshape(-1) for _, _, a in entries])
        flat = jnp.pad(flat, (0, (-flat.shape[0]) % (8 * LANE))).reshape(-1, LANE)
        h = exchange_start(f"small_start_{tag}", [flat], True)
        self.sm[tag] = (h, [(n, l, a.shape) for n, l, a in entries])
        return h["token"]

    def collect(self, groups, after):
        parts = {}
        for grp in groups:
            for l in reversed(range(DEPTH)):
                got = exchange_wait(f"rs_wait_{grp}{l}", self.rs[(l, grp)], [after])
                for n, a in zip(GROUPS[grp], got):
                    parts.setdefault(n, [None] * DEPTH)[l] = a
        return parts

    def collect_small(self, after):
        gots, where, parts = [], {}, {}
        for tag, (h, layout) in self.sm.items():
            (got,) = exchange_wait(f"small_wait_{tag}", h, [after])
            got = got.reshape(N_DEV, -1)
            off = 0
            for n, l, shape in layout:
                size = 1
                for d in shape:
                    size *= d
                if len(shape) == 1 and size % LANE == 0 and off % LANE == 0:
                    where.setdefault(n, [None] * (1 if l is None else DEPTH))[l or 0] = (len(gots), off)
                else:
                    part = got[:, off:off + size].reshape((N_DEV,) + tuple(shape))
                    if l is None:
                        parts[n] = [part]
                    else:
                        parts.setdefault(n, [None] * DEPTH)[l] = part
                off += size
            gots.append(got)
        return gots, where, parts


def kernel(x, p, positions, g_mix, w_in, q_norm, w_uq, kv_norm, w_ukv, w_pool, pool_scale, ssd_conv_w, ssd_conv_b,
           ssd_dt_bias, ssd_a_log, ssd_d, ssd_norm, lru_conv_w, lru_conv_b, lru_w_a, lru_b_a, lru_w_i, lru_b_i,
           lru_lambda, w_branch, w_out, g_mlp, w_ff1, w_ff2, g_ple, w_ple_gate, w_ple, g_final, loss_target, m_g_mix,
           m_w_in, m_q_norm, m_w_uq, m_kv_norm, m_w_ukv, m_w_pool, m_pool_scale, m_ssd_conv_w, m_ssd_conv_b,
           m_ssd_dt_bias, m_ssd_a_log, m_ssd_d, m_ssd_norm, m_lru_conv_w, m_lru_conv_b, m_lru_w_a, m_lru_b_a,
           m_lru_w_i, m_lru_b_i, m_lru_lambda, m_w_branch, m_w_out, m_g_mlp, m_w_ff1, m_w_ff2, m_g_ple, m_w_ple_gate,
           m_w_ple, m_g_final, v_g_mix, v_w_in, v_q_norm, v_w_uq, v_kv_norm, v_w_ukv, v_w_pool, v_pool_scale,
           v_ssd_conv_w, v_ssd_conv_b, v_ssd_dt_bias, v_ssd_a_log, v_ssd_d, v_ssd_norm, v_lru_conv_w, v_lru_conv_b,
           v_lru_w_a, v_lru_b_a, v_lru_w_i, v_lru_b_i, v_lru_lambda, v_w_branch, v_w_out, v_g_mlp, v_w_ff1, v_w_ff2,
           v_g_ple, v_w_ple_gate, v_w_ple, v_g_final):
    given = dict(locals())
    wts = {n: given[n] for n in WEIGHTS}
    ctx = _Exchanges(wts)
    loss, grad_x = local_step(x[0], p[:, 0], positions[0], loss_target[0], wts, ctx)

    def update(parts):
        out = {}
        for n, eight in parts.items():
            step = adamw_columns if n == "w_in" else adamw
            out[n] = step(f"adamw_{n}", eight, wts[n], given["m_" + n], given["v_" + n])
        return out

    outs = update(ctx.collect(("C", "B"), grad_x))
    late = outs["w_ff1"][1]
    outs.update(update(ctx.collect(("A",), late)))
    gots, where, parts = ctx.collect_small(late)
    outs.update(update(parts))
    names = sorted(where)
    rows = lambda a: a[None] if a.ndim == 1 else a
    res = adamw_packed("adamw_vectors", gots, [where[n] for n in names], [rows(wts[n]) for n in names],
                       [rows(given["m_" + n]) for n in names], [rows(given["v_" + n]) for n in names])
    for n, four in zip(names, res):
        outs[n] = [a[0] for a in four] if wts[n].ndim == 1 else four
    loss = lax.psum(loss, AXES)
    return (loss, grad_x[None], *[outs[n][0] for n in WEIGHTS], *[outs[n][1] for n in WEIGHTS],
            *[outs[n][2] for n in WEIGHTS], *[outs[n][3] for n in WEIGHTS])
```

```python
import functools

import jax
import jax.numpy as jnp
from jax import lax
from jax.experimental import pallas as pl
from jax.experimental.pallas import tpu as pltpu

f32 = jnp.float32
bf16 = jnp.bfloat16

D_MODEL = 1024
MIX = 512
N_HEADS = 8
QK_NOPE, QK_ROPE, V_HEAD = 64, 32, 64
Q_LORA, KV_LORA = 384, 256
ROPE_THETA = 10000.0
POOL_WINDOWS = (2, 4, 8, 16)
SSD_CHUNK = 128
SSD_CHUNKS_PER_TILE = 2
SSD_XBC = 768
CONV_W = 4
LRU_C = 8.0
EPS = 1e-6
IN_COLS = 7592
ADAM_LR, ADAM_B1, ADAM_B2, ADAM_EPS, ADAM_WD, ADAM_STEP = 0.001, 0.9, 0.999, 1e-08, 0.01, 10

LANE = 128
HALO = 8
POOL_HALO = 16
VMEM_LIMIT = 56 * 1024 * 1024
MATMUL_MAX_K_TILE = 4096
MATMUL_ACC_PASS_WEIGHT = 0.3
MATMUL_VMEM_BUDGET = 40 * 1024 * 1024
N_DEV = 8
AXES = ("x", "y", "c")

U_COLS = 7680
U_GATES, U_POOL, U_Z, U_LG, U_LX, U_XBC, U_CQ, U_KR, U_CKV = (
    (0, 4096), (4096, 512), (4608, 512), (5120, 512), (5632, 512), (6144, 768), (6912, 384), (7296, 128), (7424, 256))
KR_LANE = 64
DT_LANES = 8
U_DTYPE = bf16


def _cp(sem):
    return pltpu.CompilerParams(dimension_semantics=sem, vmem_limit_bytes=VMEM_LIMIT)


def _pick(dim, pref):
    if dim <= pref:
        return dim
    t = pref
    while t >= LANE:
        if dim % t == 0:
            return t
        t -= LANE
    t = pref
    while dim % t:
        t -= 8
    return t


@functools.partial(jax.custom_vjp, nondiff_argnums=(1,))
def shift_down(x, k):
    row = lax.broadcasted_iota(jnp.int32, x.shape, 0)
    return jnp.where(row >= k, pltpu.roll(x, k, 0), 0.0)


def _shift_down_fwd(x, k):
    return shift_down(x, k), None


def _shift_down_bwd(k, _, g):
    r = g.shape[0]
    row = lax.broadcasted_iota(jnp.int32, g.shape, 0)
    return (jnp.where(row < r - k, pltpu.roll(g, r - k, 0), 0.0),)


shift_down.defvjp(_shift_down_fwd, _shift_down_bwd)


def _tile_spec(tm, width, cb, n=None):
    if n is None:
        return pl.BlockSpec((tm, width), lambda i: (i, cb))
    return pl.BlockSpec((tm, width), lambda i: (n - 1 - i, cb))


def _const_spec(shape):
    nd = len(shape)
    return pl.BlockSpec(shape, lambda i: (0,) * nd)


def seq_fwd(name, f, params, tiles, carries, outs, tm):
    rows = tiles[0][0].shape[0]
    n = rows // tm
    np_, nt, no, nc = len(params), len(tiles), len(outs), len(carries)

    def body(*refs):
        p_refs = refs[:np_]
        t_refs = refs[np_:np_ + nt]
        o_refs = refs[np_ + nt:np_ + nt + no]
        s_refs = refs[np_ + nt + no:np_ + nt + no + nc]
        c_refs = refs[np_ + nt + no + nc:]
        i = pl.program_id(0)

        @pl.when(i == 0)
        def _():
            for c in c_refs:
                c[...] = jnp.zeros_like(c)

        cvals = [c[...] for c in c_refs]
        for s, c in zip(s_refs, cvals):
            s[0] = c
        o, newc = f(i, [r[...] for r in p_refs], cvals, [r[...].astype(f32) for r in t_refs])
        for r, v in zip(o_refs, o):
            r[...] = v.astype(r.dtype)
        for r, v in zip(c_refs, newc):
            r[...] = v

    in_specs = [_const_spec(p.shape) for p in params] + [_tile_spec(tm, w, cb) for (_, w, cb) in tiles]
    out_specs = [_tile_spec(tm, w, 0) for (w, _) in outs]
    out_specs += [pl.BlockSpec((1,) + tuple(c), lambda i, nd=len(c): (i,) + (0,) * nd) for c in carries]
    out_shape = [jax.ShapeDtypeStruct((rows, w), dt) for (w, dt) in outs]
    out_shape += [jax.ShapeDtypeStruct((n,) + tuple(c), f32) for c in carries]
    res = pl.pallas_call(
        body, name=name, grid=(n,), in_specs=in_specs, out_specs=out_specs, out_shape=out_shape,
        scratch_shapes=[pltpu.VMEM(tuple(c), f32) for c in carries],
        compiler_params=_cp(("arbitrary",)),
    )(*params, *[t[0] for t in tiles])
    return list(res[:no]), list(res[no:])


def seq_bwd(name, f, params, tiles, diff, saved, douts, gdtypes, tm):
    rows = tiles[0][0].shape[0]
    n = rows // tm
    np_, nt, nc, nd = len(params), len(tiles), len(saved), len(douts)
    didx = [k for k, d in enumerate(diff) if d]
    ng = len(didx)

    def body(*refs):
        p_refs = refs[:np_]
        t_refs = refs[np_:np_ + nt]
        s_refs = refs[np_ + nt:np_ + nt + nc]
        d_refs = refs[np_ + nt + nc:np_ + nt + nc + nd]
        pos = np_ + nt + nc + nd
        dp_refs = refs[pos:pos + np_]
        dt_refs = refs[pos + np_:pos + np_ + ng]
        dc_refs = refs[pos + np_ + ng:]
        i = pl.program_id(0)
        step = n - 1 - i

        @pl.when(i == 0)
        def _():
            for r in dp_refs:
                r[...] = jnp.zeros_like(r)
            for r in dc_refs:
                r[...] = jnp.zeros_like(r)

        pvals = [r[...] for r in p_refs]
        cvals = [r[0] for r in s_refs]
        xvals = [r[...].astype(f32) for r in t_refs]

        def fn(p, c, xd):
            x = list(xvals)
            for k, v in zip(didx, xd):
                x[k] = v
            return f(step, p, c, x)

        _, vjp = jax.vjp(fn, pvals, cvals, [xvals[k] for k in didx])
        dp, dc, dx = vjp(([r[...].astype(f32) for r in d_refs], [r[...] for r in dc_refs]))
        for r, v in zip(dp_refs, dp):
            r[...] += v
        for r, v in zip(dc_refs, dc):
            r[...] = v
        for r, v in zip(dt_refs, dx):
            r[...] = v.astype(r.dtype)

    in_specs = [_const_spec(p.shape) for p in params] + [_tile_spec(tm, w, cb, n) for (_, w, cb) in tiles]
    in_specs += [pl.BlockSpec((1,) + tuple(s.shape[1:]), lambda i, nd_=s.ndim - 1: (n - 1 - i,) + (0,) * nd_) for s in saved]
    in_specs += [_tile_spec(tm, d.shape[1], 0, n) for d in douts]
    args = list(params) + [t[0] for t in tiles] + list(saved) + list(douts)
    out_specs = [_const_spec(p.shape) for p in params] + [_tile_spec(tm, tiles[k][1], 0, n) for k in didx]
    out_shape = [jax.ShapeDtypeStruct(p.shape, f32) for p in params]
    out_shape += [jax.ShapeDtypeStruct((rows, tiles[k][1]), dt) for k, dt in zip(didx, gdtypes)]
    res = pl.pallas_call(
        body, name=name, grid=(n,), in_specs=in_specs, out_specs=out_specs, out_shape=out_shape,
        scratch_shapes=[pltpu.VMEM(tuple(s.shape[1:]), f32) for s in saved],
        compiler_params=_cp(("arbitrary",)),
    )(*args)
    return list(res[:np_]), list(res[np_:])


def _halvings(dim, lo, hi):
    t, out = _pick(dim, hi), []
    while t >= min(lo, dim) and dim % t == 0:
        out.append(t)
        if t % 2 or (t // 2) % LANE:
            break
        t //= 2
    return out


def _matmul_tiles(m, n, k, a_item, b_item, per_out, max_tn=1024, whole_rows=False):
    def vmem_bytes(tm, tn, tk):
        acc = 4 if k // tk > 1 else 0
        return 2 * (tm * tk * a_item + tk * tn * b_item + tm * tn * per_out) + tm * tn * acc

    def traffic(tm, tn, tk):
        nk = k // tk
        return (m * k * a_item * (1 if nk == 1 else n // tn) + k * n * b_item * (m // tm)
                + (nk - 1) * m * n * 8 * MATMUL_ACC_PASS_WEIGHT)

    cands = [(traffic(tm, tn, tk), -tm * tn, tm, tn, tk)
             for tk in _halvings(k, 512, MATMUL_MAX_K_TILE) for tm in _halvings(m, 256, 4096)
             for tn in ([n] if whole_rows else _halvings(n, 512, min(1024, max_tn)))
             if vmem_bytes(tm, tn, tk) <= MATMUL_VMEM_BUDGET]
    return min(cands)[2:]


def matmul(name, a, b, *, ta=False, tb=False, outs=(f32,), epi=None, extras=(), rows=(), row_sums=0, deps=(),
           out_blocks=0):
    m, k = (a.shape[1], a.shape[0]) if ta else a.shape
    n = b.shape[0] if tb else b.shape[1]
    per_out = sum(jnp.dtype(dt).itemsize for dt in outs) + sum(e[0].dtype.itemsize for e in extras)
    whole_rows = bool(rows) or row_sums > 0
    tm, tn, tk = _matmul_tiles(m, n, k, a.dtype.itemsize, b.dtype.itemsize, per_out,
                               n // out_blocks if out_blocks else n, whole_rows)
    nk = k // tk
    ne, nr, nd, no = len(extras), len(rows), len(deps), len(outs)
    dims = (((0 if ta else 1,), (1 if tb else 0,)), ((), ()))

    def body(*refs):
        a_ref, b_ref = refs[0], refs[1]
        e_refs = refs[2:2 + ne]
        r_refs = refs[2 + ne:2 + ne + nr]
        o_refs = refs[2 + ne + nr + nd:2 + ne + nr + nd + no]
        s_refs = refs[2 + ne + nr + nd + no:2 + ne + nr + nd + no + row_sums]
        i, kk = pl.program_id(0), pl.program_id(2)
        part = lax.dot_general(a_ref[...].astype(bf16), b_ref[...].astype(bf16), dims, preferred_element_type=f32)

        def finish(total):
            res = (total,) if epi is None else epi(total, *[e[...] for e in e_refs], *[r[...] for r in r_refs])
            for r, v in zip(o_refs, res[:no]):
                r[...] = v.astype(r.dtype)
            for r, v in zip(s_refs, res[no:]):
                v8 = jnp.broadcast_to(v, r.shape)

                @pl.when(i == 0)
                def _(r=r, v8=v8):
                    r[...] = v8

                @pl.when(i > 0)
                def _(r=r, v8=v8):
                    r[...] += v8

        if nk == 1:
            finish(part)
            return
        acc = refs[-1]

        @pl.when(kk == 0)
        def _():
            acc[...] = part

        @pl.when(jnp.logical_and(kk > 0, kk < nk - 1))
        def _():
            acc[...] += part

        @pl.when(kk == nk - 1)
        def _():
            finish(acc[...] + part)

    a_spec = pl.BlockSpec((tk, tm), lambda i, j, q: (q, i)) if ta else pl.BlockSpec((tm, tk), lambda i, j, q: (i, q))
    b_spec = pl.BlockSpec((tn, tk), lambda i, j, q: (j, q)) if tb else pl.BlockSpec((tk, tn), lambda i, j, q: (q, j))
    def e_spec(e):
        if len(e) == 3:
            return pl.BlockSpec((tm, e[2]), lambda i, j, q, cb=e[1] // e[2]: (i, cb))
        assert e[1] % tn == 0
        return pl.BlockSpec((tm, tn), lambda i, j, q, off=e[1] // tn: (i, off + j))

    e_specs = [e_spec(e) for e in extras]
    r_specs = [pl.BlockSpec((1, tn), lambda i, j, q: (0, j)) for _ in rows]
    if out_blocks:
        per = n // out_blocks // tn
        out_spec = pl.BlockSpec((None, tm, tn), lambda i, j, q: (j // per, i, j % per))
        out_dims = (out_blocks, m, n // out_blocks)
    else:
        out_spec = pl.BlockSpec((tm, tn), lambda i, j, q: (i, j))
        out_dims = (m, n)
    res = pl.pallas_call(
        body, name=name, grid=(m // tm, n // tn, nk),
        in_specs=[a_spec, b_spec] + e_specs + r_specs + [pl.BlockSpec(memory_space=pl.ANY) for _ in deps],
        out_specs=[out_spec for _ in outs] + [pl.BlockSpec((8, tn), lambda i, j, q: (0, j))] * row_sums,
        out_shape=[jax.ShapeDtypeStruct(out_dims, dt) for dt in outs] + [jax.ShapeDtypeStruct((8, n), f32)] * row_sums,
        scratch_shapes=[pltpu.VMEM((tm, tn), f32)] if nk > 1 else [],
        compiler_params=_cp(("arbitrary" if row_sums else "parallel", "parallel", "arbitrary")),
    )(a, b, *[e[0] for e in extras], *rows, *deps)
    return res[0] if len(res) == 1 else tuple(res)


def merge_fwd(name, ys, wbs, u):
    rows, n_out = ys[0].shape[0], wbs[0].shape[1]
    nb = len(ys)
    tm, tn = _pick(rows, 512), _pick(n_out, 512)

    def body(*refs):
        y_refs, w_refs, g_refs = refs[:nb], refs[nb:2 * nb], refs[2 * nb:3 * nb]
        m_ref, p_refs = refs[3 * nb], refs[3 * nb + 1:]
        total = None
        for y_ref, w_ref, g_ref, p_ref in zip(y_refs, w_refs, g_refs, p_refs):
            pre = jnp.dot(y_ref[...], w_ref[...], preferred_element_type=f32)
            p_ref[...] = pre.astype(p_ref.dtype)
            term = jax.nn.sigmoid(g_ref[...].astype(f32)) * pre
            total = term if total is None else total + term
        m_ref[...] = total.astype(m_ref.dtype)

    in_specs = [pl.BlockSpec((tm, y.shape[1]), lambda i, j: (i, 0)) for y in ys]
    in_specs += [pl.BlockSpec((w.shape[0], tn), lambda i, j: (0, j)) for w in wbs]
    in_specs += [pl.BlockSpec((tm, tn), lambda i, j, off=n * (n_out // tn): (i, off + j)) for n in range(nb)]
    out_spec = pl.BlockSpec((tm, tn), lambda i, j: (i, j))
    res = pl.pallas_call(
        body, name=name, grid=(rows // tm, n_out // tn), in_specs=in_specs, out_specs=[out_spec] * (nb + 1),
        out_shape=[jax.ShapeDtypeStruct((rows, n_out), bf16)] * (nb + 1),
        compiler_params=_cp(("parallel", "parallel")),
    )(*ys, *wbs, *([u] * nb))
    return res[0], list(res[1:])


ATT_SCALE = (QK_NOPE + QK_ROPE) ** -0.5
LN2 = 0.6931471805599453
ATT_C = ATT_SCALE / LN2
NT = (((1,), (1,)), ((), ()))
TN = (((0,), (0,)), ((), ()))


def _causal(tq, tk):
    return lax.broadcasted_iota(jnp.int32, (tq, tk), 0) >= lax.broadcasted_iota(jnp.int32, (tq, tk), 1)


def _tri_pairs(n, by_column):
    if by_column:
        pairs = [(i, j) for j in range(n) for i in range(j, n)]
    else:
        pairs = [(i, j) for i in range(n) for j in range(i + 1)]
    return (jnp.asarray([a for a, _ in pairs], jnp.int32), jnp.asarray([b for _, b in pairs], jnp.int32))


FWD_HEADS_PER_STEP = 8
HEADS_PER_STEP = 4
HEAD_PAIR = HEADS_PER_STEP * LANE


def attn_fwd(q, k, v, t):
    rows = q.shape[0]
    n = rows // t
    it, jt = _tri_pairs(n, False)

    def body(it_ref, jt_ref, q_ref, k_ref, v_ref, o_ref, lse_ref, m_s, l_s, acc_s):
        s_id = pl.program_id(1)
        i, j = it_ref[s_id], jt_ref[s_id]

        @pl.when(j == 0)
        def _():
            m_s[...] = jnp.full_like(m_s, -jnp.inf)
            l_s[...] = jnp.zeros_like(l_s)
            acc_s[...] = jnp.zeros_like(acc_s)

        def step(diag):
            for hh in range(FWD_HEADS_PER_STEP):
                sl = slice(LANE * hh, LANE * (hh + 1))
                s = lax.dot_general(q_ref[:, sl], k_ref[:, sl], NT, preferred_element_type=f32)
                if diag:
                    s = jnp.where(_causal(t, t), s, -jnp.inf)
                m_prev = m_s[:, sl]
                m_new = jnp.maximum(m_prev, jnp.max(s, axis=1, keepdims=True))
                alpha = jnp.exp2(m_prev - m_new)
                p = jnp.exp2(s - m_new[:, :1])
                l_s[:, sl] = alpha * l_s[:, sl] + jnp.sum(p, axis=1, keepdims=True)
                acc_s[:, sl] = alpha * acc_s[:, sl] + jnp.dot(p.astype(bf16), v_ref[:, sl], preferred_element_type=f32)
                m_s[:, sl] = m_new

        pl.when(j < i)(lambda: step(False))

        @pl.when(j == i)
        def _():
            step(True)
            o_ref[...] = (acc_s[...] / l_s[...]).astype(o_ref.dtype)
            lse_ref[...] = m_s[...] + jnp.log2(l_s[...])

    width = FWD_HEADS_PER_STEP * LANE
    qs = pl.BlockSpec((t, width), lambda h, s, it_, jt_: (it_[s], h))
    ks = pl.BlockSpec((t, width), lambda h, s, it_, jt_: (jt_[s], h))
    hw = N_HEADS * LANE
    return pl.pallas_call(
        body, name="attn_fwd",
        grid_spec=pltpu.PrefetchScalarGridSpec(
            num_scalar_prefetch=2, grid=(hw // width, it.shape[0]), in_specs=[qs, ks, ks], out_specs=[qs, qs],
            scratch_shapes=[pltpu.VMEM((t, width), f32)] * 3),
        out_shape=[jax.ShapeDtypeStruct((rows, hw), bf16), jax.ShapeDtypeStruct((rows, hw), f32)],
        compiler_params=_cp(("parallel", "arbitrary")),
    )(it, jt, q, k, v)


def attn_bwd(q, k, v, do, o, lse, t):
    rows = q.shape[0]
    n = rows // t
    it, jt = _tri_pairs(n, True)

    def body(it_ref, jt_ref, q_ref, k_ref, v_ref, do_ref, o_ref, lse_ref, dq_ref, dk_ref, dv_ref, dk_s, dv_s):
        s_id = pl.program_id(1)
        i, j = it_ref[s_id], jt_ref[s_id]

        @pl.when(s_id == 0)
        def _():
            dq_ref[...] = jnp.zeros_like(dq_ref)

        @pl.when(i == j)
        def _():
            dk_s[...] = jnp.zeros_like(dk_s)
            dv_s[...] = jnp.zeros_like(dv_s)

        q_rows = pl.ds(pl.multiple_of(i * t, t), t)

        def step(diag):
            for hh in range(HEADS_PER_STEP):
                sl = slice(LANE * hh, LANE * (hh + 1))
                qh, kh, vh, doh = q_ref[:, sl], k_ref[:, sl], v_ref[:, sl], do_ref[:, sl]
                s = lax.dot_general(qh, kh, NT, preferred_element_type=f32)
                p = jnp.exp2(s - lse_ref[:, sl][:, :1])
                if diag:
                    p = jnp.where(_causal(t, t), p, 0.0)
                dp = lax.dot_general(doh, vh, NT, preferred_element_type=f32)
                delta = jnp.sum(doh.astype(f32) * o_ref[:, sl].astype(f32), axis=1, keepdims=True)
                ds = (p * (dp - delta) * LN2).astype(bf16)
                dv_s[:, sl] += lax.dot_general(p.astype(bf16), doh, TN, preferred_element_type=f32)
                dk_s[:, sl] += lax.dot_general(ds, qh, TN, preferred_element_type=f32)
                dq_ref[q_rows, sl] += jnp.dot(ds, kh, preferred_element_type=f32)

        pl.when(i > j)(lambda: step(False))
        pl.when(i == j)(lambda: step(True))

        @pl.when(i == n - 1)
        def _():
            dk_ref[...] = dk_s[...]
            dv_ref[...] = dv_s[...]

    qs = pl.BlockSpec((t, HEAD_PAIR), lambda h, s, it_, jt_: (it_[s], h))
    ks = pl.BlockSpec((t, HEAD_PAIR), lambda h, s, it_, jt_: (jt_[s], h))
    dqs = pl.BlockSpec((rows, HEAD_PAIR), lambda h, s, it_, jt_: (0, h))
    hw = N_HEADS * LANE
    return pl.pallas_call(
        body, name="attn_bwd",
        grid_spec=pltpu.PrefetchScalarGridSpec(
            num_scalar_prefetch=2, grid=(hw // HEAD_PAIR, it.shape[0]), in_specs=[qs, ks, ks, qs, qs, qs],
            out_specs=[dqs, ks, ks], scratch_shapes=[pltpu.VMEM((t, HEAD_PAIR), f32)] * 2),
        out_shape=[jax.ShapeDtypeStruct((rows, hw), f32)] * 3,
        compiler_params=_cp(("parallel", "arbitrary")),
    )(it, jt, q, k, v, do, o, lse)


def _steps(tm):
    k, out = 1, []
    while k < tm:
        out.append(k)
        k *= 2
    return out


def _gelu_gate(h, g):
    return h * jax.nn.gelu(g)


def scan_fwd(a, u, gate, tm):
    rows, ch = a.shape
    n = rows // tm

    def body(a_ref, u_ref, gt_ref, h_ref, y_ref, h_s):
        @pl.when(pl.program_id(0) == 0)
        def _():
            h_s[...] = jnp.zeros_like(h_s)

        av, bv = a_ref[...], u_ref[...]
        row = lax.broadcasted_iota(jnp.int32, av.shape, 0)
        for k in _steps(tm):
            a_sh = jnp.where(row >= k, pltpu.roll(av, k, 0), 1.0)
            b_sh = jnp.where(row >= k, pltpu.roll(bv, k, 0), 0.0)
            bv = av * b_sh + bv
            av = av * a_sh
        h = bv + av * h_s[HALO - 1:HALO, :]
        h_ref[...] = h
        y_ref[...] = _gelu_gate(h, gt_ref[...].astype(f32)).astype(y_ref.dtype)
        h_s[...] = h[tm - HALO:, :]

    spec = pl.BlockSpec((tm, ch), lambda i: (i, 0))
    gt_spec = pl.BlockSpec((tm, gate[1]), lambda i: (i, gate[2]))
    return pl.pallas_call(
        body, name="lru_scan_fwd", grid=(n,), in_specs=[spec, spec, gt_spec], out_specs=[spec, spec],
        out_shape=[jax.ShapeDtypeStruct((rows, ch), f32), jax.ShapeDtypeStruct((rows, ch), bf16)],
        scratch_shapes=[pltpu.VMEM((HALO, ch), f32)], compiler_params=_cp(("arbitrary",)),
    )(a, u, gate[0])


def scan_bwd(a, h, gate, dy, tm):
    rows, ch = a.shape
    n = rows // tm
    per = tm // HALO

    def body(a_ref, h_ref, hp_ref, gt_ref, dy_ref, da_ref, du_ref, dg_ref, g_s, a_s):
        i = pl.program_id(0)
        step = n - 1 - i

        @pl.when(i == 0)
        def _():
            g_s[...] = jnp.zeros_like(g_s)
            a_s[...] = jnp.zeros_like(a_s)

        _, vjp = jax.vjp(_gelu_gate, h_ref[...], gt_ref[...].astype(f32))
        dh, dgate = vjp(dy_ref[...].astype(f32))
        dg_ref[...] = dgate.astype(dg_ref.dtype)
        a0 = a_ref[...]
        row = lax.broadcasted_iota(jnp.int32, a0.shape, 0)
        av = jnp.where(row < tm - 1, pltpu.roll(a0, tm - 1, 0), a_s[0:1, :])
        bv = dh
        for k in _steps(tm):
            a_sh = jnp.where(row < tm - k, pltpu.roll(av, tm - k, 0), 1.0)
            b_sh = jnp.where(row < tm - k, pltpu.roll(bv, tm - k, 0), 0.0)
            bv = bv + av * b_sh
            av = av * a_sh
        g = bv + av * g_s[0:1, :]
        h_last = jnp.where(step > 0, hp_ref[HALO - 1:HALO, :], 0.0)
        h_prev = jnp.where(row >= 1, pltpu.roll(h_ref[...], 1, 0), h_last)
        du_ref[...] = g
        da_ref[...] = g * h_prev
        g_s[...] = g[0:HALO, :]
        a_s[...] = a0[0:HALO, :]

    spec = pl.BlockSpec((tm, ch), lambda i: (n - 1 - i, 0))
    hp_spec = pl.BlockSpec((HALO, ch), lambda i: (jnp.maximum((n - 1 - i) * per - 1, 0), 0))
    gt_spec = pl.BlockSpec((tm, gate[1]), lambda i: (n - 1 - i, gate[2]))
    return pl.pallas_call(
        body, name="lru_scan_bwd", grid=(n,), in_specs=[spec, spec, hp_spec, gt_spec, spec], out_specs=[spec, spec, spec],
        out_shape=[jax.ShapeDtypeStruct((rows, ch), f32)] * 2 + [jax.ShapeDtypeStruct((rows, ch), bf16)],
        scratch_shapes=[pltpu.VMEM((HALO, ch), f32)] * 2,
        compiler_params=_cp(("arbitrary",)),
    )(a, h, h, gate[0], dy)


def _rms(x, g):
    return x * lax.rsqrt(jnp.mean(x * x, axis=-1, keepdims=True) + EPS) * g


def f_rms(step, p, c, x):
    return [_rms(x[0], p[0])], []


def _rope_lanes(shape):
    lane = lax.broadcasted_iota(jnp.int32, shape, 1)
    return jnp.logical_and(lane >= KR_LANE, lane < KR_LANE + QK_ROPE)


def _rope_swap(x):
    lane = lax.broadcasted_iota(jnp.int32, x.shape, 1)
    half = QK_ROPE // 2
    sw = jnp.where(lane < KR_LANE + half, pltpu.roll(x, LANE - half, 1), pltpu.roll(x, half, 1))
    return jnp.where(_rope_lanes(x.shape), sw, 0.0)


def _rope(x, cosf, sinf):
    return x * cosf + _rope_swap(x) * sinf


def _heads(x):
    return [x[:, LANE * h:LANE * (h + 1)] for h in range(x.shape[1] // LANE)]


def q_rope_epi(q, cosf, sinf):
    return (jnp.concatenate([_rope(b, cosf, sinf) * ATT_C for b in _heads(q)], axis=1),)


def k_rope_epi(kn, kr, cosf, sinf):
    kr_rot = _rope(jnp.where(_rope_lanes(kr.shape), kr.astype(f32), 0.0), cosf, sinf)
    return (jnp.concatenate([b + kr_rot for b in _heads(kn)], axis=1),)


def rope_bwd(dqr, dkr, cosf, sinf, ddt):
    back = lambda g: g * cosf + _rope_swap(g * sinf)
    dq = jnp.concatenate([back(b) * ATT_C for b in _heads(dqr)], axis=1)
    dkrope = jnp.where(_rope_lanes(ddt.shape), back(sum(_heads(dkr))), 0.0) + ddt
    return dq, dkr, dkrope


def _conv(tail, x, w, b):
    xf = jnp.concatenate([tail, x], axis=0)
    acc = b + w[CONV_W - 1:CONV_W, :] * xf
    for k in range(CONV_W - 1):
        acc = acc + w[k:k + 1, :] * shift_down(xf, CONV_W - 1 - k)
    return acc[HALO:, :]


def f_pool(step, p, c, x):
    wp, sc = p
    (tail,) = c
    (u,) = x
    tm = u.shape[0]
    xf = jnp.concatenate([tail, u], axis=0)
    sums, s, w = [], xf, 1
    while w < POOL_WINDOWS[-1]:
        s = s + shift_down(s, w)
        w *= 2
        sums.append(s)
    t = step * tm + lax.broadcasted_iota(jnp.int32, (tm, 1), 0)
    ys = []
    for g, (w, s) in enumerate(zip(POOL_WINDOWS, sums)):
        sl = slice(LANE * g, LANE * (g + 1))
        cnt = jnp.minimum(t + 1, w).astype(f32)
        d = s[POOL_HALO:, sl] / cnt - u[:, sl]
        ys.append(jnp.dot(d.astype(bf16), wp[LANE * g:LANE * (g + 1), :].astype(bf16), preferred_element_type=f32))
    return [jnp.concatenate(ys, axis=1) * sc], [u[tm - POOL_HALO:, :]]


def f_ssd_tile(step, p, c, x):
    outs = []
    for k in range(x[0].shape[0] // SSD_CHUNK):
        o, c = f_ssd(step, p, c, [t[SSD_CHUNK * k:SSD_CHUNK * (k + 1), :] for t in x])
        outs.append(o[0])
    return [jnp.concatenate(outs, axis=0)], c


def f_ssd(step, p, c, x):
    conv_w, conv_b, dtb, alog, dsk, ng = p
    tail, s_in = c[0], c[1:]
    z, xbc, dt = x
    ln = z.shape[0]
    xc = jax.nn.silu(_conv(tail, xbc, conv_w, conv_b))
    xs, bb, cc = xc[:, :MIX], xc[:, MIX:MIX + LANE], xc[:, MIX + LANE:]
    dtv = jax.nn.softplus(dt + dtb[0:1, :])
    a = dtv * -jnp.exp(alog[0:1, :])
    ri = lax.broadcasted_iota(jnp.int32, (ln, ln), 0)
    ci = lax.broadcasted_iota(jnp.int32, (ln, ln), 1)
    tril = (ri >= ci).astype(f32)
    triu = (ri <= ci).astype(f32)
    hi = lax.Precision.HIGHEST
    a_cs = jnp.dot(tril, a, precision=hi, preferred_element_type=f32)
    a_cs_t = lax.dot_general(a, triu, TN, precision=hi, preferred_element_type=f32)
    a_tot = jnp.sum(a, axis=0, keepdims=True)
    lane = lax.broadcasted_iota(jnp.int32, (1, LANE), 1)
    half = [(lane < 64).astype(f32), (lane >= 64).astype(f32)]
    hrow = lax.broadcasted_iota(jnp.int32, (LANE, 1), 0)

    def head(v, h):
        return jnp.sum(v * (lane == h).astype(f32), axis=1, keepdims=True)

    def pair(v, j):
        return head(v, 2 * j) * half[0] + head(v, 2 * j + 1) * half[1]

    cg = [(cc * half[g]).astype(bf16) for g in range(2)]
    bg = [(bb * half[g]).astype(bf16) for g in range(2)]
    cb = [lax.dot_general(cg[g], bg[g], NT, preferred_element_type=f32) for g in range(2)]
    ys, s_out = [], []
    for j in range(4):
        g = j // 2
        xs_j = xs[:, LANE * j:LANE * (j + 1)]
        xj = xs_j * pair(dtv, j)
        yj = xs_j * pair(dsk[0:1, :], j)
        for hh in range(2):
            h = 2 * j + hh
            rowv = jnp.sum(a_cs_t * (hrow == h).astype(f32), axis=0, keepdims=True)
            lmat = jnp.exp(jnp.where(ri >= ci, head(a_cs, h) - rowv, -jnp.inf))
            yj = yj + jnp.dot((cb[g] * lmat).astype(bf16), (xj * half[hh]).astype(bf16), preferred_element_type=f32)
        acs = pair(a_cs, j)
        tot = pair(a_tot, j)
        yj = yj + jnp.exp(acs) * jnp.dot(cg[g], s_in[j].astype(bf16), preferred_element_type=f32)
        s_new = jnp.exp(tot) * s_in[j] + lax.dot_general(bg[g], (xj * jnp.exp(tot - acs)).astype(bf16), TN,
                                                         preferred_element_type=f32)
        ys.append(yj)
        s_out.append(s_new)
    y = jnp.concatenate(ys, axis=1) * jax.nn.silu(z)
    return [_rms(y, ng)], [xbc[ln - HALO:, :]] + s_out


def _neg_expm1(y):
    series = -y * (1.0 + y * (0.5 + y * (1.0 / 6 + y * (1.0 / 24 + y * (1.0 / 120)))))
    return jnp.where(y > -0.05, series, 1.0 - jnp.exp(y))


def f_lru_pre(step, p, c, x):
    cw, cb_, wa, ba, wi, bi, lam = p
    (tail,) = c
    (lx,) = x
    tm = lx.shape[0]
    xc = _conv(tail, lx, cw, cb_)
    xb = xc.astype(bf16)
    r = jax.nn.sigmoid(jnp.dot(xb, wa.astype(bf16), preferred_element_type=f32) + ba)
    it = jax.nn.sigmoid(jnp.dot(xb, wi.astype(bf16), preferred_element_type=f32) + bi)
    log_a = -LRU_C * r * jax.nn.softplus(-lam)
    mult = jnp.sqrt(_neg_expm1(2.0 * log_a))
    return [jnp.exp(log_a), xc * it * mult], [lx[tm - HALO:, :]]


def loss_head(x, tgt, g, tm):
    rows, d = x.shape
    n = rows // tm

    def body(x_ref, t_ref, g_ref, loss_ref, dx_ref, dg_ref):
        @pl.when(pl.program_id(0) == 0)
        def _():
            loss_ref[...] = jnp.zeros_like(loss_ref)
            dg_ref[...] = jnp.zeros_like(dg_ref)

        def fn(gv, xv):
            err = _rms(xv, gv) - t_ref[...]
            return 0.5 * jnp.sum(jnp.mean(err * err, axis=-1, keepdims=True))

        val, (dg, dx) = jax.value_and_grad(fn, argnums=(0, 1))(g_ref[...], x_ref[...])
        loss_ref[...] += val
        dg_ref[...] += dg
        dx_ref[...] = dx

    spec = pl.BlockSpec((tm, d), lambda i: (i, 0))
    return pl.pallas_call(
        body, name="loss_head", grid=(n,), in_specs=[spec, spec, _const_spec((1, d))],
        out_specs=[_const_spec((8, LANE)), spec, _const_spec((1, d))],
        out_shape=[jax.ShapeDtypeStruct((8, LANE), f32), jax.ShapeDtypeStruct((rows, d), f32),
                   jax.ShapeDtypeStruct((1, d), f32)],
        compiler_params=_cp(("arbitrary",)),
    )(x, tgt, g)


def ew(name, fn, ins, outs, tm):
    rows = ins[0][0].shape[0]
    ni = len(ins)

    def body(*refs):
        res = fn(*[r[...].astype(f32) for r in refs[:ni]])
        for r, v in zip(refs[ni:], res):
            r[...] = v.astype(r.dtype)

    return pl.pallas_call(
        body, name=name, grid=(rows // tm,), in_specs=[_tile_spec(tm, w, cb) for (_, w, cb) in ins],
        out_specs=[_tile_spec(tm, w, 0) for (w, _) in outs],
        out_shape=[jax.ShapeDtypeStruct((rows, w), dt) for (w, dt) in outs],
        compiler_params=_cp(("parallel",)),
    )(*[t[0] for t in ins])


def _peers():
    x, y, c = lax.axis_index("x"), lax.axis_index("y"), lax.axis_index("c")
    me = 4 * x + 2 * y + c
    out = []
    for k in range(1, N_DEV):
        px = 1 - x if k & 4 else x
        py = 1 - y if k & 2 else y
        pc = 1 - c if k & 1 else c
        out.append(((px, py, pc), 4 * px + 2 * py + pc))
    return me, out


_HBM = pl.BlockSpec(memory_space=pltpu.HBM)
_SEM = pl.BlockSpec(memory_space=pltpu.SEMAPHORE)
_EFFECT = pltpu.SideEffectType.DATAFLOW_SIDE_EFFECTING


def _remote(src_ref, land_ref, gather, me, pid, dev, send_sems, recv_sems, k, recv_side):
    return pltpu.make_async_remote_copy(
        src_ref=src_ref if gather else src_ref.at[pid], dst_ref=land_ref.at[pid if recv_side else me],
        send_sem=send_sems.at[k], recv_sem=recv_sems.at[k], device_id=dev, device_id_type=pl.DeviceIdType.MESH)


def _own(src_ref, land_ref, gather, me, sem):
    return pltpu.make_async_copy(src_ref if gather else src_ref.at[me], land_ref.at[me], sem)


def exchange_start(name, srcs, gather, deps=()):
    n, nd = len(srcs), len(deps)
    shapes = [(s.shape if gather else s.shape[1:]) for s in srcs]
    lands = [lax.empty((N_DEV,) + tuple(sh), s.dtype) for s, sh in zip(srcs, shapes)]

    def body(*refs):
        src_refs, land_refs = refs[:n], refs[n:2 * n]
        send_sems, recv_sems, own_sem = refs[2 * n + nd:2 * n + nd + 3]
        token = refs[-1]
        me, peers = _peers()
        for k, (dev, pid) in enumerate(peers):
            for s_ref, l_ref in zip(src_refs, land_refs):
                _remote(s_ref, l_ref, gather, me, pid, dev, send_sems, recv_sems, k, False).start()
        for s_ref, l_ref in zip(src_refs, land_refs):
            _own(s_ref, l_ref, gather, me, own_sem).start()
        token[...] = jnp.zeros_like(token)

    hbm = lambda a: pltpu.with_memory_space_constraint(a, pltpu.HBM)
    res = pl.pallas_call(
        body, name=name,
        out_shape=(pltpu.SemaphoreType.DMA((N_DEV - 1,)), pltpu.SemaphoreType.DMA((N_DEV - 1,)), pltpu.SemaphoreType.DMA(()),
                   *[pltpu.HBM(a.shape, a.dtype) for a in list(srcs) + lands], jax.ShapeDtypeStruct((8, LANE), f32)),
        in_specs=[_HBM] * (2 * n) + [pl.BlockSpec(memory_space=pl.ANY)] * nd,
        out_specs=(_SEM, _SEM, _SEM, *([_HBM] * (2 * n)), pl.BlockSpec(memory_space=pltpu.VMEM)),
        input_output_aliases={i: 3 + i for i in range(2 * n)},
        compiler_params=pltpu.CompilerParams(has_side_effects=_EFFECT),
    )(*[hbm(a) for a in list(srcs) + lands], *deps)
    return dict(sems=res[:3], srcs=list(res[3:3 + n]), lands=list(res[3 + n:3 + 2 * n]), token=res[-1], gather=gather)


def exchange_wait(name, h, afters):
    n, gather = len(h["srcs"]), h["gather"]

    def body(*refs):
        src_refs, land_refs = refs[:n], refs[n:2 * n]
        send_sems, recv_sems, own_sem = refs[2 * n:2 * n + 3]
        me, peers = _peers()
        for k, (dev, pid) in enumerate(peers):
            for s_ref, l_ref in zip(src_refs, land_refs):
                _remote(s_ref, l_ref, gather, me, pid, dev, send_sems, recv_sems, k, True).wait_recv()
        for k, (dev, pid) in enumerate(peers):
            for s_ref, l_ref in zip(src_refs, land_refs):
                _remote(s_ref, l_ref, gather, me, pid, dev, send_sems, recv_sems, k, False).wait_send()
        for s_ref, l_ref in zip(src_refs, land_refs):
            _own(s_ref, l_ref, gather, me, own_sem).wait()

    arrs = h["srcs"] + h["lands"]
    res = pl.pallas_call(
        body, name=name, out_shape=tuple(pltpu.HBM(a.shape, a.dtype) for a in arrs),
        in_specs=[_HBM] * (2 * n) + [_SEM, _SEM, _SEM] + [pl.BlockSpec(memory_space=pl.ANY)] * len(afters),
        out_specs=tuple([_HBM] * (2 * n)), input_output_aliases={i: i for i in range(2 * n)},
        compiler_params=pltpu.CompilerParams(has_side_effects=_EFFECT),
    )(*arrs, *h["sems"], *afters)
    return list(res[n:])


def _adam_update(g, w, m, v):
    mn = ADAM_B1 * m + (1.0 - ADAM_B1) * g
    vn = ADAM_B2 * v + (1.0 - ADAM_B2) * jnp.square(g)
    m_hat = mn / (1.0 - ADAM_B1 ** ADAM_STEP)
    v_hat = vn / (1.0 - ADAM_B2 ** ADAM_STEP)
    return -ADAM_LR * (m_hat / (jnp.sqrt(v_hat) + ADAM_EPS) + ADAM_WD * w), mn, vn


def _adamw_vectors(name, parts, w, m, v):
    nl = len(parts)

    def body(*refs):
        p_refs = refs[:nl]
        w_ref, m_ref, v_ref, g_ref, d_ref, nm_ref, nv_ref = refs[nl:]
        for ll, p_ref in enumerate(p_refs):
            row = slice(ll, ll + 1)
            g = p_ref[0:1, :]
            for i in range(1, N_DEV):
                g = g + p_ref[i:i + 1, :]
            delta, mn, vn = _adam_update(g, w_ref[row, :], m_ref[row, :], v_ref[row, :])
            g_ref[row, :] = g
            d_ref[row, :] = delta
            nm_ref[row, :] = mn
            nv_ref[row, :] = vn

    return list(pl.pallas_call(body, name=name, out_shape=[jax.ShapeDtypeStruct(w.shape, f32)] * 4)(*parts, w, m, v))


def adamw_packed(name, gots, where, ws, ms, vs):
    ng, npar = len(gots), len(ws)

    def body(*refs):
        g_refs = refs[:ng]
        w_refs, m_refs, v_refs = (refs[ng + k * npar:ng + (k + 1) * npar] for k in range(3))
        o_refs = refs[ng + 3 * npar:]
        for p in range(npar):
            width = w_refs[p].shape[1]
            for l, (which, off) in enumerate(where[p]):
                row = slice(l, l + 1)
                cols = slice(off, off + width)
                g = g_refs[which][0:1, cols]
                for i in range(1, N_DEV):
                    g = g + g_refs[which][i:i + 1, cols]
                delta, mn, vn = _adam_update(g, w_refs[p][row, :], m_refs[p][row, :], v_refs[p][row, :])
                for k, val in enumerate((g, delta, mn, vn)):
                    o_refs[4 * p + k][row, :] = val

    out_shape = [jax.ShapeDtypeStruct(w.shape, f32) for w in ws for _ in range(4)]
    res = pl.pallas_call(body, name=name, out_shape=out_shape, compiler_params=_cp(()))(*gots, *ws, *ms, *vs)
    return [list(res[4 * p:4 * p + 4]) for p in range(npar)]


def adamw_columns(name, parts, w, m, v):
    nl, kk, cc = w.shape
    view = lambda a: jnp.transpose(a, (2, 0, 1))
    tc = min(LANE, cc)

    def body(*refs):
        p_refs = refs[:nl]
        w_ref, m_ref, v_ref, g_ref, d_ref, nm_ref, nv_ref = refs[nl:]
        for l, p_ref in enumerate(p_refs):
            g = p_ref[0].astype(f32)
            for i in range(1, N_DEV):
                g = g + p_ref[i].astype(f32)
            g = g.T
            delta, mn, vn = _adam_update(g, w_ref[:, l, :], m_ref[:, l, :], v_ref[:, l, :])
            g_ref[:, l, :] = g
            d_ref[:, l, :] = delta
            nm_ref[:, l, :] = mn
            nv_ref[:, l, :] = vn

    p_spec = pl.BlockSpec((N_DEV, kk, tc), lambda j: (0, 0, j))
    w_spec = pl.BlockSpec((tc, nl, kk), lambda j: (j, 0, 0))
    res = pl.pallas_call(
        body, name=name, grid=(pl.cdiv(cc, tc),), in_specs=[p_spec] * nl + [w_spec] * 3, out_specs=[w_spec] * 4,
        out_shape=[jax.ShapeDtypeStruct((cc, nl, kk), f32)] * 4, compiler_params=_cp(("parallel",)),
    )(*parts, view(w), view(m), view(v))
    return [jnp.transpose(a, (1, 2, 0)) for a in res]


def adamw(name, parts, w, m, v):
    nl = len(parts)
    shape = w.shape[1:]
    c = shape[-1]
    r = 1
    for s in shape[:-1]:
        r *= s
    if r == 1:
        return _adamw_vectors(name, parts, w, m, v)
    tr = _pick(r, 256) if r % 8 == 0 else r
    nb = r // tr
    parts2 = [p.reshape(N_DEV, r, c) for p in parts]
    w2, m2, v2 = (a.reshape(nl, r, c) for a in (w, m, v))

    def body(*refs):
        p_refs = refs[:nl]
        w_ref, m_ref, v_ref, g_ref, d_ref, nm_ref, nv_ref = refs[nl:]
        layer = pl.program_id(0)
        for ll, p_ref in enumerate(p_refs):
            @pl.when(layer == ll)
            def _(p_ref=p_ref):
                g = p_ref[0].astype(f32)
                for i in range(1, N_DEV):
                    g = g + p_ref[i].astype(f32)
                delta, mn, vn = _adam_update(g, w_ref[0], m_ref[0], v_ref[0])
                g_ref[0] = g
                d_ref[0] = delta
                nm_ref[0] = mn
                nv_ref[0] = vn

    def p_spec(ll):
        return pl.BlockSpec((N_DEV, tr, c), lambda l, i: (0, jnp.where(l == ll, i, jnp.where(l > ll, nb - 1, 0)), 0))

    spec = pl.BlockSpec((1, tr, c), lambda l, i: (l, i, 0))
    res = pl.pallas_call(
        body, name=name, grid=(nl, nb), in_specs=[p_spec(ll) for ll in range(nl)] + [spec, spec, spec],
        out_specs=[spec] * 4, out_shape=[jax.ShapeDtypeStruct((nl, r, c), f32)] * 4,
        compiler_params=_cp(("arbitrary", "arbitrary")),
    )(*parts2, w2, m2, v2)
    return [a.reshape(w.shape) for a in res]


_IN_SPLITS = dict(cq=(0, 384), ckv=(384, 640), kr=(640, 672), pool=(672, 1184), z=(1184, 1696), xbc=(1696, 2464),
                  dt=(2464, 2472), lg=(2472, 2984), lx=(2984, 3496), gates=(3496, 7592))


W_IN_SHARD = IN_COLS // N_DEV

_PAD_ORDER = ("gates", "pool", "z", "lg", "lx", "xbc", "cq", "dt", KR_LANE - DT_LANES, "kr", LANE - KR_LANE - QK_ROPE, "ckv")
_SEGMENTS = ((0, U_CQ[0], 384), (384, U_CKV[0], 256), (640, U_KR[0] + KR_LANE, QK_ROPE), (672, U_POOL[0], 512),
             (1184, U_Z[0], 512), (1696, U_XBC[0], 768), (2464, U_KR[0], DT_LANES), (2472, U_LG[0], 512), (2984, U_LX[0], 512),
             (3496, 0, 4096))


def _pad_w_in(shards):
    rows = shards.shape[1]
    pieces = []
    for item in _PAD_ORDER:
        if isinstance(item, int):
            pieces.append(jnp.zeros((rows, item), shards.dtype))
            continue
        a, b = _IN_SPLITS[item]
        for d in range(a // W_IN_SHARD, (b - 1) // W_IN_SHARD + 1):
            lo, hi = max(a, d * W_IN_SHARD), min(b, (d + 1) * W_IN_SHARD)
            pieces.append(shards[d, :, lo - d * W_IN_SHARD:hi - d * W_IN_SHARD])
    return jnp.concatenate(pieces, axis=1)


def _w_in_blocks(g):
    blocks = []
    for d in range(N_DEV):
        a, b = d * W_IN_SHARD, (d + 1) * W_IN_SHARD
        pieces = []
        for ref, pad, width in _SEGMENTS:
            lo, hi = max(a, ref), min(b, ref + width)
            if lo < hi:
                pieces.append(g[:, pad + lo - ref:pad + hi - ref])
        blocks.append(jnp.concatenate(pieces, axis=1))
    return jnp.stack(blocks).astype(bf16)


def _head_pad_cols(w, per, lo, hi):
    k = w.shape[0]
    w = w.reshape(k, N_HEADS, per)[:, :, lo:hi]
    return jnp.pad(w, ((0, 0), (0, 0), (0, LANE - (hi - lo)))).reshape(k, N_HEADS * LANE)


def _head_unpad_cols(g, n):
    k = g.shape[0]
    return g.reshape(k, N_HEADS, LANE)[:, :, :n]


def _on_diagonal():
    i = lax.broadcasted_iota(jnp.int32, (8, 1, 8, 1), 0)
    j = lax.broadcasted_iota(jnp.int32, (8, 1, 8, 1), 2)
    return i == j


def _block_diag(w):
    w4 = jnp.broadcast_to(w[:, :, None, :], (8, 64, 8, 64))
    return jnp.where(_on_diagonal(), w4, 0.0).reshape(MIX, MIX)


def _block_diag_inv(g):
    return jnp.sum(jnp.where(_on_diagonal(), g.reshape(8, 64, 8, 64), 0.0), axis=2)


def _head8(v):
    return jnp.pad(v[None, :], ((0, 7), (0, LANE - v.shape[0])))


GROUPS = dict(A=("w_in",), B=("w_uq", "w_ukv", "ssd_conv_w", "lru_conv_w", "w_branch", "w_out"),
              C=("w_ff1", "w_ff2", "w_ple_gate", "w_ple"))


def _kernel_weights(grp, fw):
    if grp == "A":
        w_in = _pad_w_in(fw["w_in"])
        return dict(w_in=w_in, w_dt=w_in[:, U_KR[0]:U_KR[0] + LANE])
    if grp == "C":
        return dict(w_ff1=fw["w_ff1"], w_ff2=fw["w_ff2"], w_pg=fw["w_ple_gate"], w_ple=fw["w_ple"])
    wb = fw["w_branch"]
    wb0 = jnp.pad(wb[0].reshape(N_HEADS, V_HEAD, D_MODEL), ((0, 0), (0, LANE - V_HEAD), (0, 0))).reshape(N_HEADS * LANE, D_MODEL)
    return dict(
        w_uq=_head_pad_cols(fw["w_uq"], QK_NOPE + QK_ROPE, 0, QK_NOPE + QK_ROPE),
        w_uk=_head_pad_cols(fw["w_ukv"], QK_NOPE + V_HEAD, 0, QK_NOPE),
        w_uv=_head_pad_cols(fw["w_ukv"], QK_NOPE + V_HEAD, QK_NOPE, QK_NOPE + V_HEAD),
        wb=[wb0, wb[1], wb[2], wb[3]], w_out=fw["w_out"], ssd_conv_w=fw["ssd_conv_w"], lru_conv_w=fw["lru_conv_w"])


def _layer_params(sp, l):
    row = lambda n: sp[n][l][None, :]
    return dict(
        g_mix=row("g_mix"), q_norm=row("q_norm"), kv_norm=row("kv_norm"),
        pool=[sp["w_pool"][l].reshape(4 * LANE, LANE), row("pool_scale")],
        ssd=[None, row("ssd_conv_b"), _head8(sp["ssd_dt_bias"][l]), _head8(sp["ssd_a_log"][l]),
             _head8(sp["ssd_d"][l]), row("ssd_norm")],
        lru=[None, row("lru_conv_b"), _block_diag(sp["lru_w_a"][l]), row("lru_b_a"),
             _block_diag(sp["lru_w_i"][l]), row("lru_b_i"), row("lru_lambda")],
        g_mlp=row("g_mlp"), g_ple=row("g_ple"),
    )


_sig = jax.nn.sigmoid
_SSD_CARRY = [(HALO, SSD_XBC)] + [(LANE, LANE)] * 4


def _tiles(rows):
    return dict(tm=_pick(rows, 1024), ta=_pick(rows, 512), tp=_pick(rows, 512), tl=_pick(rows, 512), ts=_pick(rows, 256),
                tssd=_pick(rows, SSD_CHUNK * SSD_CHUNKS_PER_TILE))


def _mixer_tiles(u, dt32):
    return dict(
        cq=(u, 384, U_CQ[0] // 384), ckv=(u, 256, U_CKV[0] // 256), kr=(u, LANE, U_KR[0] // LANE),
        pool=(u, MIX, U_POOL[0] // MIX), z=(u, MIX, U_Z[0] // MIX), xbc=(u, SSD_XBC, U_XBC[0] // SSD_XBC),
        dt=(dt32, LANE, 0), lg=(u, MIX, U_LG[0] // MIX), lx=(u, MIX, U_LX[0] // MIX))


def _add_norm(acc, resid, g):
    x = acc + resid
    return x, _rms(x, g)


def _layer_fwd(x, h, p_bf, ctx, l, pr, g_next, cosf, sinf, early=()):
    rows = x.shape[0]
    ts = _tiles(rows)
    tm = ts["tm"]
    nm = lambda s: f"{s}_l{l}"
    r = dict(x=x)
    if h is None:
        (h,), _ = seq_fwd(nm("rms_in"), f_rms, [pr["g_mix"]], [(x, D_MODEL, 0)], [], [(D_MODEL, bf16)], tm)
    w = dict(_kernel_weights("A", ctx.weights(l, "A", [h, *early])))
    u = matmul(nm("w_in"), h, w["w_in"], outs=(U_DTYPE,))
    dt32 = matmul(nm("w_dt"), h, w["w_dt"])
    mt = _mixer_tiles(u, dt32)
    (cqn,), _ = seq_fwd(nm("rms_q"), f_rms, [pr["q_norm"]], [mt["cq"]], [], [(Q_LORA, bf16)], tm)
    (ckvn,), _ = seq_fwd(nm("rms_kv"), f_rms, [pr["kv_norm"]], [mt["ckv"]], [], [(KV_LORA, bf16)], tm)
    (yb,), pool_saved = seq_fwd(nm("pool"), f_pool, pr["pool"], [mt["pool"]], [(POOL_HALO, MIX)], [(MIX, bf16)], ts["tp"])
    w.update(_kernel_weights("B", ctx.weights(l, "B", yb)))
    pr = dict(pr, ssd=[w["ssd_conv_w"]] + pr["ssd"][1:], lru=[w["lru_conv_w"]] + pr["lru"][1:])
    tables = [(cosf, 0, LANE), (sinf, 0, LANE)]
    qr = matmul(nm("w_uq"), cqn, w["w_uq"], outs=(bf16,), epi=q_rope_epi, extras=tables)
    kr = matmul(nm("w_uk"), ckvn, w["w_uk"], outs=(bf16,), epi=k_rope_epi, extras=[(u, U_KR[0], LANE)] + tables)
    vb = matmul(nm("w_uv"), ckvn, w["w_uv"], outs=(bf16,))
    o, lse = attn_fwd(qr, kr, vb, ts["ta"])
    (yc,), ssd_saved = seq_fwd(nm("ssd"), f_ssd_tile, pr["ssd"], [mt["z"], mt["xbc"], mt["dt"]], _SSD_CARRY, [(MIX, bf16)],
                               ts["tssd"])
    (la, lu), lru_saved = seq_fwd(nm("lru_pre"), f_lru_pre, pr["lru"], [mt["lx"]], [(HALO, MIX)], [(MIX, f32), (MIX, f32)], ts["tl"])
    hh, yd = scan_fwd(la, lu, mt["lg"], ts["ts"])
    ys = [o, yb, yc, yd]
    m, pres = merge_fwd(nm("merge"), ys, w["wb"], u)
    x1, h2 = matmul(nm("w_out"), m, w["w_out"], outs=(f32, bf16), epi=_add_norm, extras=[(x, 0)], rows=[pr["g_mlp"]])
    w.update(_kernel_weights("C", ctx.weights(l, "C", h2)))
    a1, act = matmul(nm("ff1"), h2, w["w_ff1"], outs=(bf16, bf16), epi=lambda acc: (acc, jnp.square(jnp.maximum(acc, 0.0))))
    x2, h3 = matmul(nm("ff2"), act, w["w_ff2"], outs=(f32, bf16), epi=_add_norm, extras=[(x1, 0)], rows=[pr["g_ple"]])
    gl = matmul(nm("ple_gate"), h3, w["w_pg"])
    if g_next is None:
        x3, pe = matmul(nm("ple"), p_bf, w["w_ple"], outs=(f32, f32), epi=lambda acc, g, xr: (xr + acc * _sig(g), acc),
                        extras=[(gl, 0), (x2, 0)])
        h_next = None
    else:
        def ple_norm(acc, g, xr, gn):
            xo = xr + acc * _sig(g)
            return xo, acc, _rms(xo, gn)

        x3, pe, h_next = matmul(nm("ple"), p_bf, w["w_ple"], outs=(f32, f32, bf16), epi=ple_norm,
                                extras=[(gl, 0), (x2, 0)], rows=[g_next])
    r.update(h=h, u=u, cqn=cqn, ckvn=ckvn, vb=vb, qr=qr, kr=kr, o=o, lse=lse, ys=ys, pres=pres, m=m, x1=x1,
             h2=h2, a1=a1, act=act, x2=x2, h3=h3, gl=gl, pe=pe, p_bf=p_bf, pool_saved=pool_saved, ssd_saved=ssd_saved,
             lru_saved=lru_saved, la=la, hh=hh, w=w, pr=pr, dt32=dt32)
    return x3, h_next, r


def _norm_bwd(dh, x, resid, g):
    rs = lax.rsqrt(jnp.mean(x * x, axis=-1, keepdims=True) + EPS)
    xhat = x * rs
    dxn = dh * g
    dx = rs * (dxn - xhat * jnp.mean(dxn * xhat, axis=-1, keepdims=True)) + resid
    return dx, jnp.sum(dh * xhat, axis=0, keepdims=True)


def _gate_bwd(d, g, pre):
    s = _sig(g.astype(f32))
    return d * s, d * pre.astype(f32) * s * (1.0 - s)


def _layer_bwd(dx3, r, ctx, l, cosf, sinf, tok, extra_small):
    rows = dx3.shape[0]
    ts = _tiles(rows)
    tm = ts["tm"]
    nm = lambda s: f"{s}_l{l}"
    u, w, pr = r["u"], r["w"], r["pr"]
    mt = _mixer_tiles(u, r["dt32"])
    g = {}
    full = lambda a: (a, a.shape[1], 0)
    dpe, dgl = ew(nm("ple_bwd"), _gate_bwd, [full(dx3), full(r["gl"]), full(r["pe"])], [(D_MODEL, bf16)] * 2, tm)
    g["w_ple"] = matmul(nm("d_w_ple"), r["p_bf"], dpe, ta=True, outs=(bf16,), deps=[tok] if tok is not None else [])
    g["w_pg"] = matmul(nm("d_w_pg"), r["h3"], dgl, ta=True, outs=(bf16,))
    dx2, g["g_ple"] = matmul(nm("d_h3"), dgl, w["w_pg"], tb=True, epi=_norm_bwd, extras=[(r["x2"], 0), (dx3, 0)],
                             rows=[pr["g_ple"]], row_sums=1)
    da1 = matmul(nm("d_act"), dx2, w["w_ff2"], tb=True, outs=(bf16,),
                 epi=lambda acc, a: (acc * 2.0 * jnp.maximum(a, 0.0),), extras=[(r["a1"], 0)])
    g["w_ff2"] = matmul(nm("d_w_ff2"), r["act"], dx2, ta=True, outs=(bf16,))
    g["w_ff1"] = matmul(nm("d_w_ff1"), r["h2"], da1, ta=True, outs=(bf16,), out_blocks=N_DEV)
    tok = ctx.grads(l, "C", dict(w_ff1=g["w_ff1"], w_ff2=g["w_ff2"], w_ple_gate=g["w_pg"], w_ple=g["w_ple"]))
    dx1, g["g_mlp"] = matmul(nm("d_h2"), da1, w["w_ff1"], tb=True, epi=_norm_bwd, extras=[(r["x1"], 0), (dx2, 0)],
                             rows=[pr["g_mlp"]], row_sums=1, deps=[tok])
    def merge_bwd(dm, *gates_and_pres):
        both = [_gate_bwd(dm, gates_and_pres[n], gates_and_pres[4 + n]) for n in range(4)]
        return tuple(b[0] for b in both) + tuple(b[1] for b in both)

    res = matmul(nm("d_merged"), dx1, w["w_out"], tb=True, outs=(bf16,) * 8, epi=merge_bwd,
                 extras=[(u, D_MODEL * n) for n in range(4)] + [(pre, 0) for pre in r["pres"]])
    dpres, dgates = list(res[:4]), list(res[4:])
    g["w_out"] = matmul(nm("d_w_out"), r["m"], dx1, ta=True, outs=(bf16,))
    dys, g["wb"] = [], []
    for n in range(4):
        g["wb"].append(matmul(nm(f"d_w_branch{n}"), r["ys"][n], dpres[n], ta=True, outs=(bf16,)))
        dys.append(matmul(nm(f"d_y{n}"), dpres[n], w["wb"][n], tb=True, outs=(bf16 if n == 0 else f32,)))
    g["ssd"], (dz, dxbc, ddt) = seq_bwd(nm("ssd_bwd"), f_ssd_tile, pr["ssd"], [mt["z"], mt["xbc"], mt["dt"]], [True] * 3,
                                        r["ssd_saved"], [dys[2]], [bf16] * 3, ts["tssd"])
    dqr, dkr_, dv = attn_bwd(r["qr"], r["kr"], r["vb"], dys[0], r["o"], r["lse"], ts["ta"])
    hw = N_HEADS * LANE
    dq, dkn, dkrope = ew(nm("rope_bwd"), rope_bwd, [full(dqr), full(dkr_), full(cosf), full(sinf), full(ddt)],
                         [(hw, bf16), (hw, bf16), (LANE, bf16)], tm)
    g["w_uq"] = matmul(nm("d_w_uq"), r["cqn"], dq, ta=True, outs=(bf16,))
    g["w_uk"] = matmul(nm("d_w_uk"), r["ckvn"], dkn, ta=True, outs=(bf16,))
    g["w_uv"] = matmul(nm("d_w_uv"), r["ckvn"], dv, ta=True, outs=(bf16,))
    dcqn = matmul(nm("d_cqn"), dq, w["w_uq"], tb=True)
    dckvn = matmul(nm("d_ckvn_k"), dkn, w["w_uk"], tb=True)
    dckvn = matmul(nm("d_ckvn_v"), dv, w["w_uv"], tb=True, epi=lambda acc, prev: (acc + prev,), extras=[(dckvn, 0)])
    (g["q_norm"],), (dcq,) = seq_bwd(nm("rms_q_bwd"), f_rms, [pr["q_norm"]], [mt["cq"]], [True], [], [dcqn], [bf16], tm)
    (g["kv_norm"],), (dckv,) = seq_bwd(nm("rms_kv_bwd"), f_rms, [pr["kv_norm"]], [mt["ckv"]], [True], [], [dckvn], [bf16], tm)
    g["pool"], (dpool,) = seq_bwd(nm("pool_bwd"), f_pool, pr["pool"], [mt["pool"]], [True], r["pool_saved"], [dys[1]],
                                  [bf16], ts["tp"])
    da, du, dlg = scan_bwd(r["la"], r["hh"], mt["lg"], dys[3], ts["ts"])
    g["lru"], (dlx,) = seq_bwd(nm("lru_pre_bwd"), f_lru_pre, pr["lru"], [mt["lx"]], [True], r["lru_saved"], [da, du],
                               [bf16], ts["tl"])
    dk = _head_unpad_cols(g["w_uk"], QK_NOPE)
    dv_ = _head_unpad_cols(g["w_uv"], V_HEAD)
    wb0 = g["wb"][0].reshape(N_HEADS, LANE, D_MODEL)[:, :V_HEAD].reshape(MIX, D_MODEL)
    ssd, lru, pool = g["ssd"], g["lru"], g["pool"]
    tok = ctx.grads(l, "B", dict(
        w_uq=_head_unpad_cols(g["w_uq"], QK_NOPE + QK_ROPE).reshape(Q_LORA, -1),
        w_ukv=jnp.concatenate([dk, dv_], axis=2).reshape(KV_LORA, -1), ssd_conv_w=ssd[0], lru_conv_w=lru[0],
        w_branch=jnp.stack([wb0, g["wb"][1], g["wb"][2], g["wb"][3]]), w_out=g["w_out"]))
    du_p = jnp.concatenate(dgates + [dpool, dz, dlg, dlx, dxbc, dcq, dkrope, dckv], axis=1)
    small = dict(
        q_norm=g["q_norm"][0], kv_norm=g["kv_norm"][0],
        w_pool=pool[0].reshape(4, LANE, LANE), pool_scale=pool[1][0],
        ssd_conv_b=ssd[1][0], ssd_dt_bias=ssd[2][0, :8], ssd_a_log=ssd[3][0, :8], ssd_d=ssd[4][0, :8], ssd_norm=ssd[5][0],
        lru_conv_b=lru[1][0], lru_w_a=_block_diag_inv(lru[2]), lru_b_a=lru[3][0], lru_w_i=_block_diag_inv(lru[4]),
        lru_b_i=lru[5][0], lru_lambda=lru[6][0], g_mlp=g["g_mlp"][0], g_ple=g["g_ple"][0])
    tok_small = ctx.small(f"l{l}", [(n, l, small[n]) for n in SMALL if n in small] + extra_small)
    g_w_in = matmul(nm("d_w_in"), r["h"], du_p, ta=True, outs=(bf16,), deps=[tok, tok_small])
    tok = ctx.grads(l, "A", dict(w_in=_w_in_blocks(g_w_in)))
    dx, g_mix = matmul(nm("d_h"), du_p, w["w_in"], tb=True, epi=_norm_bwd, extras=[(r["x"], 0), (dx1, 0)],
                       rows=[pr["g_mix"]], row_sums=1, deps=[tok])
    return dx, tok, ("g_mix", l, g_mix[0])


def _rope_tables(positions):
    inv = 1.0 / (ROPE_THETA ** (jnp.arange(0, QK_ROPE, 2, dtype=f32) / QK_ROPE))
    ang = positions.astype(f32)[:, None] * inv
    cos, sin = jnp.cos(ang), jnp.sin(ang)
    rows = positions.shape[0]
    pad = jnp.zeros((rows, LANE - KR_LANE - QK_ROPE), f32)
    cosf = jnp.concatenate([jnp.ones((rows, KR_LANE), f32), cos, cos, pad], axis=1)
    sinf = jnp.concatenate([jnp.zeros((rows, KR_LANE), f32), -sin, sin, pad], axis=1)
    return cosf, sinf


WEIGHTS = ['g_mix', 'w_in', 'q_norm', 'w_uq', 'kv_norm', 'w_ukv', 'w_pool', 'pool_scale', 'ssd_conv_w', 'ssd_conv_b',
           'ssd_dt_bias', 'ssd_a_log', 'ssd_d', 'ssd_norm', 'lru_conv_w', 'lru_conv_b', 'lru_w_a', 'lru_b_a', 'lru_w_i',
           'lru_b_i', 'lru_lambda', 'w_branch', 'w_out', 'g_mlp', 'w_ff1', 'w_ff2', 'g_ple', 'w_ple_gate', 'w_ple', 'g_final']
SHARDED = dict(w_in=2, w_uq=2, w_ukv=2, ssd_conv_w=2, lru_conv_w=2, w_branch=3, w_out=1, w_ff1=2, w_ff2=1,
               w_ple_gate=1, w_ple=2)
F32_PAYLOAD = ("ssd_conv_w", "lru_conv_w")
DEPTH = 2


SMALL = [n for n in WEIGHTS if n not in SHARDED and n != "g_final"]


def local_step(x, p, positions, tgt, sp, ctx):
    cosf, sinf = _rope_tables(positions)
    prs = [_layer_params(sp, l) for l in range(DEPTH)]
    p_bf = [p[l].astype(bf16) for l in range(DEPTH)]
    early = [cosf, sinf, *p_bf] + [a for pr in prs for v in pr.values() for a in (v if isinstance(v, list) else [v])
                                   if a is not None]
    res, h = [], None
    for l in range(DEPTH):
        g_next = prs[l + 1]["g_mix"] if l + 1 < DEPTH else None
        x, h, r = _layer_fwd(x, h, p_bf[l], ctx, l, prs[l], g_next, cosf, sinf, early if l == 0 else ())
        res.append(r)
    loss8, dx, dgf = loss_head(x, tgt, sp["g_final"][None, :], _tiles(x.shape[0])["tm"])
    tok = None
    pending = ("g_final", None, dgf[0])
    for l in reversed(range(DEPTH)):
        dx, tok, pending = _layer_bwd(dx, res[l], ctx, l, cosf, sinf, tok, [pending])
    ctx.small("last", [pending])
    return loss8[0, 0], dx


def _payload(name, w):
    return w if name in F32_PAYLOAD else w.astype(bf16)


def _blocks(name, g):
    ax = SHARDED[name] - 1
    shape = list(g.shape)
    shape[ax:ax + 1] = [N_DEV, shape[ax] // N_DEV]
    return _payload(name, jnp.moveaxis(g.reshape(shape), ax, 0))


def _assemble(name, shards):
    ax = SHARDED[name] - 1
    shape = list(shards.shape[1:])
    shape[ax] *= N_DEV
    return jnp.moveaxis(shards, 0, ax).reshape(shape)


class _Exchanges:
    def __init__(self, wts):
        self.wts = wts
        self.ag, self.rs, self.sm = {}, {}, {}
        tok = None
        for l in range(DEPTH):
            for grp, names in GROUPS.items():
                h = exchange_start(f"ag_start_{grp}{l}", [_payload(n, wts[n][l]) for n in names], True,
                                   deps=[] if tok is None else [tok])
                tok = h["token"]
                self.ag[(l, grp)] = h
        self.all_started = tok

    def weights(self, l, grp, after):
        afters = list(after) if isinstance(after, (list, tuple)) else [after]
        if (l, grp) == (0, "A"):
            afters.append(self.all_started)
        got = exchange_wait(f"ag_wait_{grp}{l}", self.ag[(l, grp)], afters)
        out = {}
        for n, a in zip(GROUPS[grp], got):
            out[n] = a if n == "w_in" else _assemble(n, a)
        return out

    def grads(self, l, grp, g):
        cut = lambda n: g[n].ndim == self.wts[n].ndim
        h = exchange_start(f"rs_start_{grp}{l}", [g[n] if cut(n) else _blocks(n, g[n]) for n in GROUPS[grp]], False)
        self.rs[(l, grp)] = h
        return h["token"]

    def small(self, tag, entries):
        entries = sorted(entries, key=lambda e: e[2].size % LANE != 0)
        flat = jnp.concatenate([a.reshape(-1) for _, _, a in entries])
        flat = jnp.pad(flat, (0, (-flat.shape[0]) % (8 * LANE))).reshape(-1, LANE)
        h = exchange_start(f"small_start_{tag}", [flat], True)
        self.sm[tag] = (h, [(n, l, a.shape) for n, l, a in entries])
        return h["token"]

    def collect(self, groups, after):
        parts = {}
        for grp in groups:
            for l in reversed(range(DEPTH)):
                got = exchange_wait(f"rs_wait_{grp}{l}", self.rs[(l, grp)], [after])
                for n, a in zip(GROUPS[grp], got):
                    parts.setdefault(n, [None] * DEPTH)[l] = a
        return parts

    def collect_small(self, after):
        gots, where, parts = [], {}, {}
        for tag, (h, layout) in self.sm.items():
            (got,) = exchange_wait(f"small_wait_{tag}", h, [after])
            got = got.reshape(N_DEV, -1)
            off = 0
            for n, l, shape in layout:
                size = 1
                for d in shape:
                    size *= d
                if len(shape) == 1 and size % LANE == 0 and off % LANE == 0:
                    where.setdefault(n, [None] * (1 if l is None else DEPTH))[l or 0] = (len(gots), off)
                else:
                    part = got[:, off:off + size].reshape((N_DEV,) + tuple(shape))
                    if l is None:
                        parts[n] = [part]
                    else:
                        parts.setdefault(n, [None] * DEPTH)[l] = part
                off += size
            gots.append(got)
        return gots, where, parts


def kernel(x, p, positions, g_mix, w_in, q_norm, w_uq, kv_norm, w_ukv, w_pool, pool_scale, ssd_conv_w, ssd_conv_b,
           ssd_dt_bias, ssd_a_log, ssd_d, ssd_norm, lru_conv_w, lru_conv_b, lru_w_a, lru_b_a, lru_w_i, lru_b_i,
           lru_lambda, w_branch, w_out, g_mlp, w_ff1, w_ff2, g_ple, w_ple_gate, w_ple, g_final, loss_target, m_g_mix,
           m_w_in, m_q_norm, m_w_uq, m_kv_norm, m_w_ukv, m_w_pool, m_pool_scale, m_ssd_conv_w, m_ssd_conv_b,
           m_ssd_dt_bias, m_ssd_a_log, m_ssd_d, m_ssd_norm, m_lru_conv_w, m_lru_conv_b, m_lru_w_a, m_lru_b_a,
           m_lru_w_i, m_lru_b_i, m_lru_lambda, m_w_branch, m_w_out, m_g_mlp, m_w_ff1, m_w_ff2, m_g_ple, m_w_ple_gate,
           m_w_ple, m_g_final, v_g_mix, v_w_in, v_q_norm, v_w_uq, v_kv_norm, v_w_ukv, v_w_pool, v_pool_scale,
           v_ssd_conv_w, v_ssd_conv_b, v_ssd_dt_bias, v_ssd_a_log, v_ssd_d, v_ssd_norm, v_lru_conv_w, v_lru_conv_b,
           v_lru_w_a, v_lru_b_a, v_lru_w_i, v_lru_b_i, v_lru_lambda, v_w_branch, v_w_out, v_g_mlp, v_w_ff1, v_w_ff2,
           v_g_ple, v_w_ple_gate, v_w_ple, v_g_final):
    given = dict(locals())
    wts = {n: given[n] for n in WEIGHTS}
    ctx = _Exchanges(wts)
    loss, grad_x = local_step(x[0], p[:, 0], positions[0], loss_target[0], wts, ctx)

    def update(parts):
        out = {}
        for n, eight in parts.items():
            step = adamw_columns if n == "w_in" else adamw
            out[n] = step(f"adamw_{n}", eight, wts[n], given["m_" + n], given["v_" + n])
        return out

    outs = update(ctx.collect(("C", "B"), grad_x))
    late = outs["w_ff1"][1]
    outs.update(update(ctx.collect(("A",), late)))
    gots, where, parts = ctx.collect_small(late)
    outs.update(update(parts))
    names = sorted(where)
    rows = lambda a: a[None] if a.ndim == 1 else a
    res = adamw_packed("adamw_vectors", gots, [where[n] for n in names], [rows(wts[n]) for n in names],
                       [rows(given["m_" + n]) for n in names], [rows(given["v_" + n]) for n in names])
    for n, four in zip(names, res):
        outs[n] = [a[0] for a in four] if wts[n].ndim == 1 else four
    loss = lax.psum(loss, AXES)
    return (loss, grad_x[None], *[outs[n][0] for n in WEIGHTS], *[outs[n][1] for n in WEIGHTS],
            *[outs[n][2] for n in WEIGHTS], *[outs[n][3] for n in WEIGHTS])
```

```python
import functools

import jax
import jax.numpy as jnp
from jax import lax
from jax.experimental import pallas as pl
from jax.experimental.pallas import tpu as pltpu

f32 = jnp.float32
bf16 = jnp.bfloat16

D_MODEL = 1024
MIX = 512
N_HEADS = 8
QK_NOPE, QK_ROPE, V_HEAD = 64, 32, 64
Q_LORA, KV_LORA = 384, 256
ROPE_THETA = 10000.0
POOL_WINDOWS = (2, 4, 8, 16)
SSD_CHUNK = 128
SSD_CHUNKS_PER_TILE = 2
SSD_XBC = 768
CONV_W = 4
LRU_C = 8.0
EPS = 1e-6
IN_COLS = 7592
ADAM_LR, ADAM_B1, ADAM_B2, ADAM_EPS, ADAM_WD, ADAM_STEP = 0.001, 0.9, 0.999, 1e-08, 0.01, 10

LANE = 128
HALO = 8
POOL_HALO = 16
VMEM_LIMIT = 56 * 1024 * 1024
MATMUL_MAX_K_TILE = 4096
MATMUL_ACC_PASS_WEIGHT = 0.3
MATMUL_VMEM_BUDGET = 40 * 1024 * 1024
N_DEV = 8
AXES = ("x", "y", "c")

U_COLS = 7680
U_GATES, U_POOL, U_Z, U_LG, U_LX, U_XBC, U_CQ, U_KR, U_CKV = (
    (0, 4096), (4096, 512), (4608, 512), (5120, 512), (5632, 512), (6144, 768), (6912, 384), (7296, 128), (7424, 256))
KR_LANE = 64
DT_LANES = 8
U_DTYPE = bf16


def _cp(sem):
    return pltpu.CompilerParams(dimension_semantics=sem, vmem_limit_bytes=VMEM_LIMIT)


def _pick(dim, pref):
    if dim <= pref:
        return dim
    t = pref
    while t >= LANE:
        if dim % t == 0:
            return t
        t -= LANE
    t = pref
    while dim % t:
        t -= 8
    return t


@functools.partial(jax.custom_vjp, nondiff_argnums=(1,))
def shift_down(x, k):
    row = lax.broadcasted_iota(jnp.int32, x.shape, 0)
    return jnp.where(row >= k, pltpu.roll(x, k, 0), 0.0)


def _shift_down_fwd(x, k):
    return shift_down(x, k), None


def _shift_down_bwd(k, _, g):
    r = g.shape[0]
    row = lax.broadcasted_iota(jnp.int32, g.shape, 0)
    return (jnp.where(row < r - k, pltpu.roll(g, r - k, 0), 0.0),)


shift_down.defvjp(_shift_down_fwd, _shift_down_bwd)


def _tile_spec(tm, width, cb, n=None):
    if n is None:
        return pl.BlockSpec((tm, width), lambda i: (i, cb))
    return pl.BlockSpec((tm, width), lambda i: (n - 1 - i, cb))


def _const_spec(shape):
    nd = len(shape)
    return pl.BlockSpec(shape, lambda i: (0,) * nd)


def seq_fwd(name, f, params, tiles, carries, outs, tm):
    rows = tiles[0][0].shape[0]
    n = rows // tm
    np_, nt, no, nc = len(params), len(tiles), len(outs), len(carries)

    def body(*refs):
        p_refs = refs[:np_]
        t_refs = refs[np_:np_ + nt]
        o_refs = refs[np_ + nt:np_ + nt + no]
        s_refs = refs[np_ + nt + no:np_ + nt + no + nc]
        c_refs = refs[np_ + nt + no + nc:]
        i = pl.program_id(0)

        @pl.when(i == 0)
        def _():
            for c in c_refs:
                c[...] = jnp.zeros_like(c)

        cvals = [c[...] for c in c_refs]
        for s, c in zip(s_refs, cvals):
            s[0] = c
        o, newc = f(i, [r[...] for r in p_refs], cvals, [r[...].astype(f32) for r in t_refs])
        for r, v in zip(o_refs, o):
            r[...] = v.astype(r.dtype)
        for r, v in zip(c_refs, newc):
            r[...] = v

    in_specs = [_const_spec(p.shape) for p in params] + [_tile_spec(tm, w, cb) for (_, w, cb) in tiles]
    out_specs = [_tile_spec(tm, w, 0) for (w, _) in outs]
    out_specs += [pl.BlockSpec((1,) + tuple(c), lambda i, nd=len(c): (i,) + (0,) * nd) for c in carries]
    out_shape = [jax.ShapeDtypeStruct((rows, w), dt) for (w, dt) in outs]
    out_shape += [jax.ShapeDtypeStruct((n,) + tuple(c), f32) for c in carries]
    res = pl.pallas_call(
        body, name=name, grid=(n,), in_specs=in_specs, out_specs=out_specs, out_shape=out_shape,
        scratch_shapes=[pltpu.VMEM(tuple(c), f32) for c in carries],
        compiler_params=_cp(("arbitrary",)),
    )(*params, *[t[0] for t in tiles])
    return list(res[:no]), list(res[no:])


def seq_bwd(name, f, params, tiles, diff, saved, douts, gdtypes, tm):
    rows = tiles[0][0].shape[0]
    n = rows // tm
    np_, nt, nc, nd = len(params), len(tiles), len(saved), len(douts)
    didx = [k for k, d in enumerate(diff) if d]
    ng = len(didx)

    def body(*refs):
        p_refs = refs[:np_]
        t_refs = refs[np_:np_ + nt]
        s_refs = refs[np_ + nt:np_ + nt + nc]
        d_refs = refs[np_ + nt + nc:np_ + nt + nc + nd]
        pos = np_ + nt + nc + nd
        dp_refs = refs[pos:pos + np_]
        dt_refs = refs[pos + np_:pos + np_ + ng]
        dc_refs = refs[pos + np_ + ng:]
        i = pl.program_id(0)
        step = n - 1 - i

        @pl.when(i == 0)
        def _():
            for r in dp_refs:
                r[...] = jnp.zeros_like(r)
            for r in dc_refs:
                r[...] = jnp.zeros_like(r)

        pvals = [r[...] for r in p_refs]
        cvals = [r[0] for r in s_refs]
        xvals = [r[...].astype(f32) for r in t_refs]

        def fn(p, c, xd):
            x = list(xvals)
            for k, v in zip(didx, xd):
                x[k] = v
            return f(step, p, c, x)

        _, vjp = jax.vjp(fn, pvals, cvals, [xvals[k] for k in didx])
        dp, dc, dx = vjp(([r[...].astype(f32) for r in d_refs], [r[...] for r in dc_refs]))
        for r, v in zip(dp_refs, dp):
            r[...] += v
        for r, v in zip(dc_refs, dc):
            r[...] = v
        for r, v in zip(dt_refs, dx):
            r[...] = v.astype(r.dtype)

    in_specs = [_const_spec(p.shape) for p in params] + [_tile_spec(tm, w, cb, n) for (_, w, cb) in tiles]
    in_specs += [pl.BlockSpec((1,) + tuple(s.shape[1:]), lambda i, nd_=s.ndim - 1: (n - 1 - i,) + (0,) * nd_) for s in saved]
    in_specs += [_tile_spec(tm, d.shape[1], 0, n) for d in douts]
    args = list(params) + [t[0] for t in tiles] + list(saved) + list(douts)
    out_specs = [_const_spec(p.shape) for p in params] + [_tile_spec(tm, tiles[k][1], 0, n) for k in didx]
    out_shape = [jax.ShapeDtypeStruct(p.shape, f32) for p in params]
    out_shape += [jax.ShapeDtypeStruct((rows, tiles[k][1]), dt) for k, dt in zip(didx, gdtypes)]
    res = pl.pallas_call(
        body, name=name, grid=(n,), in_specs=in_specs, out_specs=out_specs, out_shape=out_shape,
        scratch_shapes=[pltpu.VMEM(tuple(s.shape[1:]), f32) for s in saved],
        compiler_params=_cp(("arbitrary",)),
    )(*args)
    return list(res[:np_]), list(res[np_:])


def _halvings(dim, lo, hi):
    t, out = _pick(dim, hi), []
    while t >= min(lo, dim) and dim % t == 0:
        out.append(t)
        if t % 2 or (t // 2) % LANE:
            break
        t //= 2
    return out


def _matmul_tiles(m, n, k, a_item, b_item, per_out, max_tn=1024, whole_rows=False):
    def vmem_bytes(tm, tn, tk):
        acc = 4 if k // tk > 1 else 0
        return 2 * (tm * tk * a_item + tk * tn * b_item + tm * tn * per_out) + tm * tn * acc

    def traffic(tm, tn, tk):
        nk = k // tk
        return (m * k * a_item * (1 if nk == 1 else n // tn) + k * n * b_item * (m // tm)
                + (nk - 1) * m * n * 8 * MATMUL_ACC_PASS_WEIGHT)

    cands = [(traffic(tm, tn, tk), -tm * tn, tm, tn, tk)
             for tk in _halvings(k, 512, MATMUL_MAX_K_TILE) for tm in _halvings(m, 256, 4096)
             for tn in ([n] if whole_rows else _halvings(n, 512, min(1024, max_tn)))
             if vmem_bytes(tm, tn, tk) <= MATMUL_VMEM_BUDGET]
    return min(cands)[2:]


def matmul(name, a, b, *, ta=False, tb=False, outs=(f32,), epi=None, extras=(), rows=(), row_sums=0, deps=(),
           out_blocks=0):
    m, k = (a.shape[1], a.shape[0]) if ta else a.shape
    n = b.shape[0] if tb else b.shape[1]
    per_out = sum(jnp.dtype(dt).itemsize for dt in outs) + sum(e[0].dtype.itemsize for e in extras)
    whole_rows = bool(rows) or row_sums > 0
    tm, tn, tk = _matmul_tiles(m, n, k, a.dtype.itemsize, b.dtype.itemsize, per_out,
                               n // out_blocks if out_blocks else n, whole_rows)
    nk = k // tk
    ne, nr, nd, no = len(extras), len(rows), len(deps), len(outs)
    dims = (((0 if ta else 1,), (1 if tb else 0,)), ((), ()))

    def body(*refs):
        a_ref, b_ref = refs[0], refs[1]
        e_refs = refs[2:2 + ne]
        r_refs = refs[2 + ne:2 + ne + nr]
        o_refs = refs[2 + ne + nr + nd:2 + ne + nr + nd + no]
        s_refs = refs[2 + ne + nr + nd + no:2 + ne + nr + nd + no + row_sums]
        i, kk = pl.program_id(0), pl.program_id(2)
        part = lax.dot_general(a_ref[...].astype(bf16), b_ref[...].astype(bf16), dims, preferred_element_type=f32)

        def finish(total):
            res = (total,) if epi is None else epi(total, *[e[...] for e in e_refs], *[r[...] for r in r_refs])
            for r, v in zip(o_refs, res[:no]):
                r[...] = v.astype(r.dtype)
            for r, v in zip(s_refs, res[no:]):
                v8 = jnp.broadcast_to(v, r.shape)

                @pl.when(i == 0)
                def _(r=r, v8=v8):
                    r[...] = v8

                @pl.when(i > 0)
                def _(r=r, v8=v8):
                    r[...] += v8

        if nk == 1:
            finish(part)
            return
        acc = refs[-1]

        @pl.when(kk == 0)
        def _():
            acc[...] = part

        @pl.when(jnp.logical_and(kk > 0, kk < nk - 1))
        def _():
            acc[...] += part

        @pl.when(kk == nk - 1)
        def _():
            finish(acc[...] + part)

    a_spec = pl.BlockSpec((tk, tm), lambda i, j, q: (q, i)) if ta else pl.BlockSpec((tm, tk), lambda i, j, q: (i, q))
    b_spec = pl.BlockSpec((tn, tk), lambda i, j, q: (j, q)) if tb else pl.BlockSpec((tk, tn), lambda i, j, q: (q, j))
    def e_spec(e):
        if len(e) == 3:
            return pl.BlockSpec((tm, e[2]), lambda i, j, q, cb=e[1] // e[2]: (i, cb))
        assert e[1] % tn == 0
        return pl.BlockSpec((tm, tn), lambda i, j, q, off=e[1] // tn: (i, off + j))

    e_specs = [e_spec(e) for e in extras]
    r_specs = [pl.BlockSpec((1, tn), lambda i, j, q: (0, j)) for _ in rows]
    if out_blocks:
        per = n // out_blocks // tn
        out_spec = pl.BlockSpec((None, tm, tn), lambda i, j, q: (j // per, i, j % per))
        out_dims = (out_blocks, m, n // out_blocks)
    else:
        out_spec = pl.BlockSpec((tm, tn), lambda i, j, q: (i, j))
        out_dims = (m, n)
    res = pl.pallas_call(
        body, name=name, grid=(m // tm, n // tn, nk),
        in_specs=[a_spec, b_spec] + e_specs + r_specs + [pl.BlockSpec(memory_space=pl.ANY) for _ in deps],
        out_specs=[out_spec for _ in outs] + [pl.BlockSpec((8, tn), lambda i, j, q: (0, j))] * row_sums,
        out_shape=[jax.ShapeDtypeStruct(out_dims, dt) for dt in outs] + [jax.ShapeDtypeStruct((8, n), f32)] * row_sums,
        scratch_shapes=[pltpu.VMEM((tm, tn), f32)] if nk > 1 else [],
        compiler_params=_cp(("arbitrary" if row_sums else "parallel", "parallel", "arbitrary")),
    )(a, b, *[e[0] for e in extras], *rows, *deps)
    return res[0] if len(res) == 1 else tuple(res)


def merge_fwd(name, ys, wbs, u):
    rows, n_out = ys[0].shape[0], wbs[0].shape[1]
    nb = len(ys)
    tm, tn = _pick(rows, 1024), _pick(n_out, 512)

    def body(*refs):
        y_refs, w_refs, g_refs = refs[:nb], refs[nb:2 * nb], refs[2 * nb:3 * nb]
        m_ref, p_refs = refs[3 * nb], refs[3 * nb + 1:]
        total = None
        for y_ref, w_ref, g_ref, p_ref in zip(y_refs, w_refs, g_refs, p_refs):
            pre = jnp.dot(y_ref[...], w_ref[...], preferred_element_type=f32)
            p_ref[...] = pre.astype(p_ref.dtype)
            term = jax.nn.sigmoid(g_ref[...].astype(f32)) * pre
            total = term if total is None else total + term
        m_ref[...] = total.astype(m_ref.dtype)

    in_specs = [pl.BlockSpec((tm, y.shape[1]), lambda i, j: (i, 0)) for y in ys]
    in_specs += [pl.BlockSpec((w.shape[0], tn), lambda i, j: (0, j)) for w in wbs]
    in_specs += [pl.BlockSpec((tm, tn), lambda i, j, off=n * (n_out // tn): (i, off + j)) for n in range(nb)]
    out_spec = pl.BlockSpec((tm, tn), lambda i, j: (i, j))
    res = pl.pallas_call(
        body, name=name, grid=(rows // tm, n_out // tn), in_specs=in_specs, out_specs=[out_spec] * (nb + 1),
        out_shape=[jax.ShapeDtypeStruct((rows, n_out), bf16)] * (nb + 1),
        compiler_params=_cp(("parallel", "parallel")),
    )(*ys, *wbs, *([u] * nb))
    return res[0], list(res[1:])


ATT_SCALE = (QK_NOPE + QK_ROPE) ** -0.5
LN2 = 0.6931471805599453
ATT_C = ATT_SCALE / LN2
NT = (((1,), (1,)), ((), ()))
TN = (((0,), (0,)), ((), ()))


def _causal(tq, tk):
    return lax.broadcasted_iota(jnp.int32, (tq, tk), 0) >= lax.broadcasted_iota(jnp.int32, (tq, tk), 1)


def _tri_pairs(n, by_column):
    if by_column:
        pairs = [(i, j) for j in range(n) for i in range(j, n)]
    else:
        pairs = [(i, j) for i in range(n) for j in range(i + 1)]
    return (jnp.asarray([a for a, _ in pairs], jnp.int32), jnp.asarray([b for _, b in pairs], jnp.int32))


FWD_HEADS_PER_STEP = 8
HEADS_PER_STEP = 4
HEAD_PAIR = HEADS_PER_STEP * LANE


def attn_fwd(q, k, v, t):
    rows = q.shape[0]
    n = rows // t
    it, jt = _tri_pairs(n, False)

    def body(it_ref, jt_ref, q_ref, k_ref, v_ref, o_ref, lse_ref, m_s, l_s, acc_s):
        s_id = pl.program_id(1)
        i, j = it_ref[s_id], jt_ref[s_id]

        @pl.when(j == 0)
        def _():
            m_s[...] = jnp.full_like(m_s, -jnp.inf)
            l_s[...] = jnp.zeros_like(l_s)
            acc_s[...] = jnp.zeros_like(acc_s)

        def step(diag):
            for hh in range(FWD_HEADS_PER_STEP):
                sl = slice(LANE * hh, LANE * (hh + 1))
                s = lax.dot_general(q_ref[:, sl], k_ref[:, sl], NT, preferred_element_type=f32)
                if diag:
                    s = jnp.where(_causal(t, t), s, -jnp.inf)
                m_prev = m_s[:, sl]
                m_new = jnp.maximum(m_prev, jnp.max(s, axis=1, keepdims=True))
                alpha = jnp.exp2(m_prev - m_new)
                p = jnp.exp2(s - m_new[:, :1])
                l_s[:, sl] = alpha * l_s[:, sl] + jnp.sum(p, axis=1, keepdims=True)
                acc_s[:, sl] = alpha * acc_s[:, sl] + jnp.dot(p.astype(bf16), v_ref[:, sl], preferred_element_type=f32)
                m_s[:, sl] = m_new

        pl.when(j < i)(lambda: step(False))

        @pl.when(j == i)
        def _():
            step(True)
            o_ref[...] = (acc_s[...] / l_s[...]).astype(o_ref.dtype)
            lse_ref[...] = m_s[...] + jnp.log2(l_s[...])

    width = FWD_HEADS_PER_STEP * LANE
    qs = pl.BlockSpec((t, width), lambda h, s, it_, jt_: (it_[s], h))
    ks = pl.BlockSpec((t, width), lambda h, s, it_, jt_: (jt_[s], h))
    hw = N_HEADS * LANE
    return pl.pallas_call(
        body, name="attn_fwd",
        grid_spec=pltpu.PrefetchScalarGridSpec(
            num_scalar_prefetch=2, grid=(hw // width, it.shape[0]), in_specs=[qs, ks, ks], out_specs=[qs, qs],
            scratch_shapes=[pltpu.VMEM((t, width), f32)] * 3),
        out_shape=[jax.ShapeDtypeStruct((rows, hw), bf16), jax.ShapeDtypeStruct((rows, hw), f32)],
        compiler_params=_cp(("parallel", "arbitrary")),
    )(it, jt, q, k, v)


def attn_bwd(q, k, v, do, o, lse, t):
    rows = q.shape[0]
    n = rows // t
    it, jt = _tri_pairs(n, True)

    def body(it_ref, jt_ref, q_ref, k_ref, v_ref, do_ref, o_ref, lse_ref, dq_ref, dk_ref, dv_ref, dk_s, dv_s):
        s_id = pl.program_id(1)
        i, j = it_ref[s_id], jt_ref[s_id]

        @pl.when(s_id == 0)
        def _():
            dq_ref[...] = jnp.zeros_like(dq_ref)

        @pl.when(i == j)
        def _():
            dk_s[...] = jnp.zeros_like(dk_s)
            dv_s[...] = jnp.zeros_like(dv_s)

        q_rows = pl.ds(pl.multiple_of(i * t, t), t)

        def step(diag):
            for hh in range(HEADS_PER_STEP):
                sl = slice(LANE * hh, LANE * (hh + 1))
                qh, kh, vh, doh = q_ref[:, sl], k_ref[:, sl], v_ref[:, sl], do_ref[:, sl]
                s = lax.dot_general(qh, kh, NT, preferred_element_type=f32)
                p = jnp.exp2(s - lse_ref[:, sl][:, :1])
                if diag:
                    p = jnp.where(_causal(t, t), p, 0.0)
                dp = lax.dot_general(doh, vh, NT, preferred_element_type=f32)
                delta = jnp.sum(doh.astype(f32) * o_ref[:, sl].astype(f32), axis=1, keepdims=True)
                ds = (p * (dp - delta) * LN2).astype(bf16)
                dv_s[:, sl] += lax.dot_general(p.astype(bf16), doh, TN, preferred_element_type=f32)
                dk_s[:, sl] += lax.dot_general(ds, qh, TN, preferred_element_type=f32)
                dq_ref[q_rows, sl] += jnp.dot(ds, kh, preferred_element_type=f32)

        pl.when(i > j)(lambda: step(False))
        pl.when(i == j)(lambda: step(True))

        @pl.when(i == n - 1)
        def _():
            dk_ref[...] = dk_s[...]
            dv_ref[...] = dv_s[...]

    qs = pl.BlockSpec((t, HEAD_PAIR), lambda h, s, it_, jt_: (it_[s], h))
    ks = pl.BlockSpec((t, HEAD_PAIR), lambda h, s, it_, jt_: (jt_[s], h))
    dqs = pl.BlockSpec((rows, HEAD_PAIR), lambda h, s, it_, jt_: (0, h))
    hw = N_HEADS * LANE
    return pl.pallas_call(
        body, name="attn_bwd",
        grid_spec=pltpu.PrefetchScalarGridSpec(
            num_scalar_prefetch=2, grid=(hw // HEAD_PAIR, it.shape[0]), in_specs=[qs, ks, ks, qs, qs, qs],
            out_specs=[dqs, ks, ks], scratch_shapes=[pltpu.VMEM((t, HEAD_PAIR), f32)] * 2),
        out_shape=[jax.ShapeDtypeStruct((rows, hw), f32)] * 3,
        compiler_params=_cp(("parallel", "arbitrary")),
    )(it, jt, q, k, v, do, o, lse)


def _steps(tm):
    k, out = 1, []
    while k < tm:
        out.append(k)
        k *= 2
    return out


def _gelu_gate(h, g):
    return h * jax.nn.gelu(g)


def scan_fwd(a, u, gate, tm):
    rows, ch = a.shape
    n = rows // tm

    def body(a_ref, u_ref, gt_ref, h_ref, y_ref, h_s):
        @pl.when(pl.program_id(0) == 0)
        def _():
            h_s[...] = jnp.zeros_like(h_s)

        av, bv = a_ref[...], u_ref[...]
        row = lax.broadcasted_iota(jnp.int32, av.shape, 0)
        for k in _steps(tm):
            a_sh = jnp.where(row >= k, pltpu.roll(av, k, 0), 1.0)
            b_sh = jnp.where(row >= k, pltpu.roll(bv, k, 0), 0.0)
            bv = av * b_sh + bv
            av = av * a_sh
        h = bv + av * h_s[HALO - 1:HALO, :]
        h_ref[...] = h
        y_ref[...] = _gelu_gate(h, gt_ref[...].astype(f32)).astype(y_ref.dtype)
        h_s[...] = h[tm - HALO:, :]

    spec = pl.BlockSpec((tm, ch), lambda i: (i, 0))
    gt_spec = pl.BlockSpec((tm, gate[1]), lambda i: (i, gate[2]))
    return pl.pallas_call(
        body, name="lru_scan_fwd", grid=(n,), in_specs=[spec, spec, gt_spec], out_specs=[spec, spec],
        out_shape=[jax.ShapeDtypeStruct((rows, ch), f32), jax.ShapeDtypeStruct((rows, ch), bf16)],
        scratch_shapes=[pltpu.VMEM((HALO, ch), f32)], compiler_params=_cp(("arbitrary",)),
    )(a, u, gate[0])


def scan_bwd(a, h, gate, dy, tm):
    rows, ch = a.shape
    n = rows // tm
    per = tm // HALO

    def body(a_ref, h_ref, hp_ref, gt_ref, dy_ref, da_ref, du_ref, dg_ref, g_s, a_s):
        i = pl.program_id(0)
        step = n - 1 - i

        @pl.when(i == 0)
        def _():
            g_s[...] = jnp.zeros_like(g_s)
            a_s[...] = jnp.zeros_like(a_s)

        _, vjp = jax.vjp(_gelu_gate, h_ref[...], gt_ref[...].astype(f32))
        dh, dgate = vjp(dy_ref[...].astype(f32))
        dg_ref[...] = dgate.astype(dg_ref.dtype)
        a0 = a_ref[...]
        row = lax.broadcasted_iota(jnp.int32, a0.shape, 0)
        av = jnp.where(row < tm - 1, pltpu.roll(a0, tm - 1, 0), a_s[0:1, :])
        bv = dh
        for k in _steps(tm):
            a_sh = jnp.where(row < tm - k, pltpu.roll(av, tm - k, 0), 1.0)
            b_sh = jnp.where(row < tm - k, pltpu.roll(bv, tm - k, 0), 0.0)
            bv = bv + av * b_sh
            av = av * a_sh
        g = bv + av * g_s[0:1, :]
        h_last = jnp.where(step > 0, hp_ref[HALO - 1:HALO, :], 0.0)
        h_prev = jnp.where(row >= 1, pltpu.roll(h_ref[...], 1, 0), h_last)
        du_ref[...] = g
        da_ref[...] = g * h_prev
        g_s[...] = g[0:HALO, :]
        a_s[...] = a0[0:HALO, :]

    spec = pl.BlockSpec((tm, ch), lambda i: (n - 1 - i, 0))
    hp_spec = pl.BlockSpec((HALO, ch), lambda i: (jnp.maximum((n - 1 - i) * per - 1, 0), 0))
    gt_spec = pl.BlockSpec((tm, gate[1]), lambda i: (n - 1 - i, gate[2]))
    return pl.pallas_call(
        body, name="lru_scan_bwd", grid=(n,), in_specs=[spec, spec, hp_spec, gt_spec, spec], out_specs=[spec, spec, spec],
        out_shape=[jax.ShapeDtypeStruct((rows, ch), f32)] * 2 + [jax.ShapeDtypeStruct((rows, ch), bf16)],
        scratch_shapes=[pltpu.VMEM((HALO, ch), f32)] * 2,
        compiler_params=_cp(("arbitrary",)),
    )(a, h, h, gate[0], dy)


def _rms(x, g):
    return x * lax.rsqrt(jnp.mean(x * x, axis=-1, keepdims=True) + EPS) * g


def f_rms(step, p, c, x):
    return [_rms(x[0], p[0])], []


def _rope_lanes(shape):
    lane = lax.broadcasted_iota(jnp.int32, shape, 1)
    return jnp.logical_and(lane >= KR_LANE, lane < KR_LANE + QK_ROPE)


def _rope_swap(x):
    lane = lax.broadcasted_iota(jnp.int32, x.shape, 1)
    half = QK_ROPE // 2
    sw = jnp.where(lane < KR_LANE + half, pltpu.roll(x, LANE - half, 1), pltpu.roll(x, half, 1))
    return jnp.where(_rope_lanes(x.shape), sw, 0.0)


def _rope(x, cosf, sinf):
    return x * cosf + _rope_swap(x) * sinf


def _heads(x):
    return [x[:, LANE * h:LANE * (h + 1)] for h in range(x.shape[1] // LANE)]


def q_rope_epi(q, cosf, sinf):
    return (jnp.concatenate([_rope(b, cosf, sinf) * ATT_C for b in _heads(q)], axis=1),)


def k_rope_epi(kn, kr, cosf, sinf):
    kr_rot = _rope(jnp.where(_rope_lanes(kr.shape), kr.astype(f32), 0.0), cosf, sinf)
    return (jnp.concatenate([b + kr_rot for b in _heads(kn)], axis=1),)


def rope_bwd(dqr, dkr, cosf, sinf, ddt):
    back = lambda g: g * cosf + _rope_swap(g * sinf)
    dq = jnp.concatenate([back(b) * ATT_C for b in _heads(dqr)], axis=1)
    dkrope = jnp.where(_rope_lanes(ddt.shape), back(sum(_heads(dkr))), 0.0) + ddt
    return dq, dkr, dkrope


def _conv(tail, x, w, b):
    xf = jnp.concatenate([tail, x], axis=0)
    acc = b + w[CONV_W - 1:CONV_W, :] * xf
    for k in range(CONV_W - 1):
        acc = acc + w[k:k + 1, :] * shift_down(xf, CONV_W - 1 - k)
    return acc[HALO:, :]


def f_pool(step, p, c, x):
    wp, sc = p
    (tail,) = c
    (u,) = x
    tm = u.shape[0]
    xf = jnp.concatenate([tail, u], axis=0)
    sums, s, w = [], xf, 1
    while w < POOL_WINDOWS[-1]:
        s = s + shift_down(s, w)
        w *= 2
        sums.append(s)
    t = step * tm + lax.broadcasted_iota(jnp.int32, (tm, 1), 0)
    ys = []
    for g, (w, s) in enumerate(zip(POOL_WINDOWS, sums)):
        sl = slice(LANE * g, LANE * (g + 1))
        cnt = jnp.minimum(t + 1, w).astype(f32)
        d = s[POOL_HALO:, sl] / cnt - u[:, sl]
        ys.append(jnp.dot(d.astype(bf16), wp[LANE * g:LANE * (g + 1), :].astype(bf16), preferred_element_type=f32))
    return [jnp.concatenate(ys, axis=1) * sc], [u[tm - POOL_HALO:, :]]


def f_ssd_tile(step, p, c, x):
    outs = []
    for k in range(x[0].shape[0] // SSD_CHUNK):
        o, c = f_ssd(step, p, c, [t[SSD_CHUNK * k:SSD_CHUNK * (k + 1), :] for t in x])
        outs.append(o[0])
    return [jnp.concatenate(outs, axis=0)], c


def f_ssd(step, p, c, x):
    conv_w, conv_b, dtb, alog, dsk, ng = p
    tail, s_in = c[0], c[1:]
    z, xbc, dt = x
    ln = z.shape[0]
    xc = jax.nn.silu(_conv(tail, xbc, conv_w, conv_b))
    xs, bb, cc = xc[:, :MIX], xc[:, MIX:MIX + LANE], xc[:, MIX + LANE:]
    dtv = jax.nn.softplus(dt + dtb[0:1, :])
    a = dtv * -jnp.exp(alog[0:1, :])
    ri = lax.broadcasted_iota(jnp.int32, (ln, ln), 0)
    ci = lax.broadcasted_iota(jnp.int32, (ln, ln), 1)
    tril = (ri >= ci).astype(f32)
    triu = (ri <= ci).astype(f32)
    hi = lax.Precision.HIGHEST
    a_cs = jnp.dot(tril, a, precision=hi, preferred_element_type=f32)
    a_cs_t = lax.dot_general(a, triu, TN, precision=hi, preferred_element_type=f32)
    a_tot = jnp.sum(a, axis=0, keepdims=True)
    lane = lax.broadcasted_iota(jnp.int32, (1, LANE), 1)
    half = [(lane < 64).astype(f32), (lane >= 64).astype(f32)]
    hrow = lax.broadcasted_iota(jnp.int32, (LANE, 1), 0)

    def head(v, h):
        return jnp.sum(v * (lane == h).astype(f32), axis=1, keepdims=True)

    def pair(v, j):
        return head(v, 2 * j) * half[0] + head(v, 2 * j + 1) * half[1]

    cg = [(cc * half[g]).astype(bf16) for g in range(2)]
    bg = [(bb * half[g]).astype(bf16) for g in range(2)]
    cb = [lax.dot_general(cg[g], bg[g], NT, preferred_element_type=f32) for g in range(2)]
    ys, s_out = [], []
    for j in range(4):
        g = j // 2
        xs_j = xs[:, LANE * j:LANE * (j + 1)]
        xj = xs_j * pair(dtv, j)
        yj = xs_j * pair(dsk[0:1, :], j)
        for hh in range(2):
            h = 2 * j + hh
            rowv = jnp.sum(a_cs_t * (hrow == h).astype(f32), axis=0, keepdims=True)
            lmat = jnp.exp(jnp.where(ri >= ci, head(a_cs, h) - rowv, -jnp.inf))
            yj = yj + jnp.dot((cb[g] * lmat).astype(bf16), (xj * half[hh]).astype(bf16), preferred_element_type=f32)
        acs = pair(a_cs, j)
        tot = pair(a_tot, j)
        yj = yj + jnp.exp(acs) * jnp.dot(cg[g], s_in[j].astype(bf16), preferred_element_type=f32)
        s_new = jnp.exp(tot) * s_in[j] + lax.dot_general(bg[g], (xj * jnp.exp(tot - acs)).astype(bf16), TN,
                                                         preferred_element_type=f32)
        ys.append(yj)
        s_out.append(s_new)
    y = jnp.concatenate(ys, axis=1) * jax.nn.silu(z)
    return [_rms(y, ng)], [xbc[ln - HALO:, :]] + s_out


def _neg_expm1(y):
    series = -y * (1.0 + y * (0.5 + y * (1.0 / 6 + y * (1.0 / 24 + y * (1.0 / 120)))))
    return jnp.where(y > -0.05, series, 1.0 - jnp.exp(y))


def f_lru_pre(step, p, c, x):
    cw, cb_, wa, ba, wi, bi, lam = p
    (tail,) = c
    (lx,) = x
    tm = lx.shape[0]
    xc = _conv(tail, lx, cw, cb_)
    xb = xc.astype(bf16)
    r = jax.nn.sigmoid(jnp.dot(xb, wa.astype(bf16), preferred_element_type=f32) + ba)
    it = jax.nn.sigmoid(jnp.dot(xb, wi.astype(bf16), preferred_element_type=f32) + bi)
    log_a = -LRU_C * r * jax.nn.softplus(-lam)
    mult = jnp.sqrt(_neg_expm1(2.0 * log_a))
    return [jnp.exp(log_a), xc * it * mult], [lx[tm - HALO:, :]]


def loss_head(x, tgt, g, tm):
    rows, d = x.shape
    n = rows // tm

    def body(x_ref, t_ref, g_ref, loss_ref, dx_ref, dg_ref):
        @pl.when(pl.program_id(0) == 0)
        def _():
            loss_ref[...] = jnp.zeros_like(loss_ref)
            dg_ref[...] = jnp.zeros_like(dg_ref)

        def fn(gv, xv):
            err = _rms(xv, gv) - t_ref[...]
            return 0.5 * jnp.sum(jnp.mean(err * err, axis=-1, keepdims=True))

        val, (dg, dx) = jax.value_and_grad(fn, argnums=(0, 1))(g_ref[...], x_ref[...])
        loss_ref[...] += val
        dg_ref[...] += dg
        dx_ref[...] = dx

    spec = pl.BlockSpec((tm, d), lambda i: (i, 0))
    return pl.pallas_call(
        body, name="loss_head", grid=(n,), in_specs=[spec, spec, _const_spec((1, d))],
        out_specs=[_const_spec((8, LANE)), spec, _const_spec((1, d))],
        out_shape=[jax.ShapeDtypeStruct((8, LANE), f32), jax.ShapeDtypeStruct((rows, d), f32),
                   jax.ShapeDtypeStruct((1, d), f32)],
        compiler_params=_cp(("arbitrary",)),
    )(x, tgt, g)


def ew(name, fn, ins, outs, tm):
    rows = ins[0][0].shape[0]
    ni = len(ins)

    def body(*refs):
        res = fn(*[r[...].astype(f32) for r in refs[:ni]])
        for r, v in zip(refs[ni:], res):
            r[...] = v.astype(r.dtype)

    return pl.pallas_call(
        body, name=name, grid=(rows // tm,), in_specs=[_tile_spec(tm, w, cb) for (_, w, cb) in ins],
        out_specs=[_tile_spec(tm, w, 0) for (w, _) in outs],
        out_shape=[jax.ShapeDtypeStruct((rows, w), dt) for (w, dt) in outs],
        compiler_params=_cp(("parallel",)),
    )(*[t[0] for t in ins])


def _peers():
    x, y, c = lax.axis_index("x"), lax.axis_index("y"), lax.axis_index("c")
    me = 4 * x + 2 * y + c
    out = []
    for k in range(1, N_DEV):
        px = 1 - x if k & 4 else x
        py = 1 - y if k & 2 else y
        pc = 1 - c if k & 1 else c
        out.append(((px, py, pc), 4 * px + 2 * py + pc))
    return me, out


_HBM = pl.BlockSpec(memory_space=pltpu.HBM)
_SEM = pl.BlockSpec(memory_space=pltpu.SEMAPHORE)
_EFFECT = pltpu.SideEffectType.DATAFLOW_SIDE_EFFECTING


def _remote(src_ref, land_ref, gather, me, pid, dev, send_sems, recv_sems, k, recv_side):
    return pltpu.make_async_remote_copy(
        src_ref=src_ref if gather else src_ref.at[pid], dst_ref=land_ref.at[pid if recv_side else me],
        send_sem=send_sems.at[k], recv_sem=recv_sems.at[k], device_id=dev, device_id_type=pl.DeviceIdType.MESH)


def _own(src_ref, land_ref, gather, me, sem):
    return pltpu.make_async_copy(src_ref if gather else src_ref.at[me], land_ref.at[me], sem)


def exchange_start(name, srcs, gather, deps=()):
    n, nd = len(srcs), len(deps)
    shapes = [(s.shape if gather else s.shape[1:]) for s in srcs]
    lands = [lax.empty((N_DEV,) + tuple(sh), s.dtype) for s, sh in zip(srcs, shapes)]

    def body(*refs):
        src_refs, land_refs = refs[:n], refs[n:2 * n]
        send_sems, recv_sems, own_sem = refs[2 * n + nd:2 * n + nd + 3]
        token = refs[-1]
        me, peers = _peers()
        for k, (dev, pid) in enumerate(peers):
            for s_ref, l_ref in zip(src_refs, land_refs):
                _remote(s_ref, l_ref, gather, me, pid, dev, send_sems, recv_sems, k, False).start()
        for s_ref, l_ref in zip(src_refs, land_refs):
            _own(s_ref, l_ref, gather, me, own_sem).start()
        token[...] = jnp.zeros_like(token)

    hbm = lambda a: pltpu.with_memory_space_constraint(a, pltpu.HBM)
    res = pl.pallas_call(
        body, name=name,
        out_shape=(pltpu.SemaphoreType.DMA((N_DEV - 1,)), pltpu.SemaphoreType.DMA((N_DEV - 1,)), pltpu.SemaphoreType.DMA(()),
                   *[pltpu.HBM(a.shape, a.dtype) for a in list(srcs) + lands], jax.ShapeDtypeStruct((8, LANE), f32)),
        in_specs=[_HBM] * (2 * n) + [pl.BlockSpec(memory_space=pl.ANY)] * nd,
        out_specs=(_SEM, _SEM, _SEM, *([_HBM] * (2 * n)), pl.BlockSpec(memory_space=pltpu.VMEM)),
        input_output_aliases={i: 3 + i for i in range(2 * n)},
        compiler_params=pltpu.CompilerParams(has_side_effects=_EFFECT),
    )(*[hbm(a) for a in list(srcs) + lands], *deps)
    return dict(sems=res[:3], srcs=list(res[3:3 + n]), lands=list(res[3 + n:3 + 2 * n]), token=res[-1], gather=gather)


def exchange_wait(name, h, afters):
    n, gather = len(h["srcs"]), h["gather"]

    def body(*refs):
        src_refs, land_refs = refs[:n], refs[n:2 * n]
        send_sems, recv_sems, own_sem = refs[2 * n:2 * n + 3]
        me, peers = _peers()
        for k, (dev, pid) in enumerate(peers):
            for s_ref, l_ref in zip(src_refs, land_refs):
                _remote(s_ref, l_ref, gather, me, pid, dev, send_sems, recv_sems, k, True).wait_recv()
        for k, (dev, pid) in enumerate(peers):
            for s_ref, l_ref in zip(src_refs, land_refs):
                _remote(s_ref, l_ref, gather, me, pid, dev, send_sems, recv_sems, k, False).wait_send()
        for s_ref, l_ref in zip(src_refs, land_refs):
            _own(s_ref, l_ref, gather, me, own_sem).wait()

    arrs = h["srcs"] + h["lands"]
    res = pl.pallas_call(
        body, name=name, out_shape=tuple(pltpu.HBM(a.shape, a.dtype) for a in arrs),
        in_specs=[_HBM] * (2 * n) + [_SEM, _SEM, _SEM] + [pl.BlockSpec(memory_space=pl.ANY)] * len(afters),
        out_specs=tuple([_HBM] * (2 * n)), input_output_aliases={i: i for i in range(2 * n)},
        compiler_params=pltpu.CompilerParams(has_side_effects=_EFFECT),
    )(*arrs, *h["sems"], *afters)
    return list(res[n:])


def _adam_update(g, w, m, v):
    mn = ADAM_B1 * m + (1.0 - ADAM_B1) * g
    vn = ADAM_B2 * v + (1.0 - ADAM_B2) * jnp.square(g)
    m_hat = mn / (1.0 - ADAM_B1 ** ADAM_STEP)
    v_hat = vn / (1.0 - ADAM_B2 ** ADAM_STEP)
    return -ADAM_LR * (m_hat / (jnp.sqrt(v_hat) + ADAM_EPS) + ADAM_WD * w), mn, vn


def _adamw_vectors(name, parts, w, m, v):
    nl = len(parts)

    def body(*refs):
        p_refs = refs[:nl]
        w_ref, m_ref, v_ref, g_ref, d_ref, nm_ref, nv_ref = refs[nl:]
        for ll, p_ref in enumerate(p_refs):
            row = slice(ll, ll + 1)
            g = p_ref[0:1, :]
            for i in range(1, N_DEV):
                g = g + p_ref[i:i + 1, :]
            delta, mn, vn = _adam_update(g, w_ref[row, :], m_ref[row, :], v_ref[row, :])
            g_ref[row, :] = g
            d_ref[row, :] = delta
            nm_ref[row, :] = mn
            nv_ref[row, :] = vn

    return list(pl.pallas_call(body, name=name, out_shape=[jax.ShapeDtypeStruct(w.shape, f32)] * 4)(*parts, w, m, v))


def adamw_packed(name, gots, where, ws, ms, vs):
    ng, npar = len(gots), len(ws)

    def body(*refs):
        g_refs = refs[:ng]
        w_refs, m_refs, v_refs = (refs[ng + k * npar:ng + (k + 1) * npar] for k in range(3))
        o_refs = refs[ng + 3 * npar:]
        for p in range(npar):
            width = w_refs[p].shape[1]
            for l, (which, off) in enumerate(where[p]):
                row = slice(l, l + 1)
                cols = slice(off, off + width)
                g = g_refs[which][0:1, cols]
                for i in range(1, N_DEV):
                    g = g + g_refs[which][i:i + 1, cols]
                delta, mn, vn = _adam_update(g, w_refs[p][row, :], m_refs[p][row, :], v_refs[p][row, :])
                for k, val in enumerate((g, delta, mn, vn)):
                    o_refs[4 * p + k][row, :] = val

    out_shape = [jax.ShapeDtypeStruct(w.shape, f32) for w in ws for _ in range(4)]
    res = pl.pallas_call(body, name=name, out_shape=out_shape, compiler_params=_cp(()))(*gots, *ws, *ms, *vs)
    return [list(res[4 * p:4 * p + 4]) for p in range(npar)]


def adamw_columns(name, parts, w, m, v):
    nl, kk, cc = w.shape
    view = lambda a: jnp.transpose(a, (2, 0, 1))
    tc = min(LANE, cc)

    def body(*refs):
        p_refs = refs[:nl]
        w_ref, m_ref, v_ref, g_ref, d_ref, nm_ref, nv_ref = refs[nl:]
        for l, p_ref in enumerate(p_refs):
            g = p_ref[0].astype(f32)
            for i in range(1, N_DEV):
                g = g + p_ref[i].astype(f32)
            g = g.T
            delta, mn, vn = _adam_update(g, w_ref[:, l, :], m_ref[:, l, :], v_ref[:, l, :])
            g_ref[:, l, :] = g
            d_ref[:, l, :] = delta
            nm_ref[:, l, :] = mn
            nv_ref[:, l, :] = vn

    p_spec = pl.BlockSpec((N_DEV, kk, tc), lambda j: (0, 0, j))
    w_spec = pl.BlockSpec((tc, nl, kk), lambda j: (j, 0, 0))
    res = pl.pallas_call(
        body, name=name, grid=(pl.cdiv(cc, tc),), in_specs=[p_spec] * nl + [w_spec] * 3, out_specs=[w_spec] * 4,
        out_shape=[jax.ShapeDtypeStruct((cc, nl, kk), f32)] * 4, compiler_params=_cp(("parallel",)),
    )(*parts, view(w), view(m), view(v))
    return [jnp.transpose(a, (1, 2, 0)) for a in res]


def adamw(name, parts, w, m, v):
    nl = len(parts)
    shape = w.shape[1:]
    c = shape[-1]
    r = 1
    for s in shape[:-1]:
        r *= s
    if r == 1:
        return _adamw_vectors(name, parts, w, m, v)
    tr = _pick(r, 256) if r % 8 == 0 else r
    nb = r // tr
    parts2 = [p.reshape(N_DEV, r, c) for p in parts]
    w2, m2, v2 = (a.reshape(nl, r, c) for a in (w, m, v))

    def body(*refs):
        p_refs = refs[:nl]
        w_ref, m_ref, v_ref, g_ref, d_ref, nm_ref, nv_ref = refs[nl:]
        layer = pl.program_id(0)
        for ll, p_ref in enumerate(p_refs):
            @pl.when(layer == ll)
            def _(p_ref=p_ref):
                g = p_ref[0].astype(f32)
                for i in range(1, N_DEV):
                    g = g + p_ref[i].astype(f32)
                delta, mn, vn = _adam_update(g, w_ref[0], m_ref[0], v_ref[0])
                g_ref[0] = g
                d_ref[0] = delta
                nm_ref[0] = mn
                nv_ref[0] = vn

    def p_spec(ll):
        return pl.BlockSpec((N_DEV, tr, c), lambda l, i: (0, jnp.where(l == ll, i, jnp.where(l > ll, nb - 1, 0)), 0))

    spec = pl.BlockSpec((1, tr, c), lambda l, i: (l, i, 0))
    res = pl.pallas_call(
        body, name=name, grid=(nl, nb), in_specs=[p_spec(ll) for ll in range(nl)] + [spec, spec, spec],
        out_specs=[spec] * 4, out_shape=[jax.ShapeDtypeStruct((nl, r, c), f32)] * 4,
        compiler_params=_cp(("arbitrary", "arbitrary")),
    )(*parts2, w2, m2, v2)
    return [a.reshape(w.shape) for a in res]


_IN_SPLITS = dict(cq=(0, 384), ckv=(384, 640), kr=(640, 672), pool=(672, 1184), z=(1184, 1696), xbc=(1696, 2464),
                  dt=(2464, 2472), lg=(2472, 2984), lx=(2984, 3496), gates=(3496, 7592))


W_IN_SHARD = IN_COLS // N_DEV

_PAD_ORDER = ("gates", "pool", "z", "lg", "lx", "xbc", "cq", "dt", KR_LANE - DT_LANES, "kr", LANE - KR_LANE - QK_ROPE, "ckv")
_SEGMENTS = ((0, U_CQ[0], 384), (384, U_CKV[0], 256), (640, U_KR[0] + KR_LANE, QK_ROPE), (672, U_POOL[0], 512),
             (1184, U_Z[0], 512), (1696, U_XBC[0], 768), (2464, U_KR[0], DT_LANES), (2472, U_LG[0], 512), (2984, U_LX[0], 512),
             (3496, 0, 4096))


def _pad_w_in(shards):
    rows = shards.shape[1]
    pieces = []
    for item in _PAD_ORDER:
        if isinstance(item, int):
            pieces.append(jnp.zeros((rows, item), shards.dtype))
            continue
        a, b = _IN_SPLITS[item]
        for d in range(a // W_IN_SHARD, (b - 1) // W_IN_SHARD + 1):
            lo, hi = max(a, d * W_IN_SHARD), min(b, (d + 1) * W_IN_SHARD)
            pieces.append(shards[d, :, lo - d * W_IN_SHARD:hi - d * W_IN_SHARD])
    return jnp.concatenate(pieces, axis=1)


def _w_in_blocks(g):
    blocks = []
    for d in range(N_DEV):
        a, b = d * W_IN_SHARD, (d + 1) * W_IN_SHARD
        pieces = []
        for ref, pad, width in _SEGMENTS:
            lo, hi = max(a, ref), min(b, ref + width)
            if lo < hi:
                pieces.append(g[:, pad + lo - ref:pad + hi - ref])
        blocks.append(jnp.concatenate(pieces, axis=1))
    return jnp.stack(blocks).astype(bf16)


def _head_pad_cols(w, per, lo, hi):
    k = w.shape[0]
    w = w.reshape(k, N_HEADS, per)[:, :, lo:hi]
    return jnp.pad(w, ((0, 0), (0, 0), (0, LANE - (hi - lo)))).reshape(k, N_HEADS * LANE)


def _head_unpad_cols(g, n):
    k = g.shape[0]
    return g.reshape(k, N_HEADS, LANE)[:, :, :n]


def _on_diagonal():
    i = lax.broadcasted_iota(jnp.int32, (8, 1, 8, 1), 0)
    j = lax.broadcasted_iota(jnp.int32, (8, 1, 8, 1), 2)
    return i == j


def _block_diag(w):
    w4 = jnp.broadcast_to(w[:, :, None, :], (8, 64, 8, 64))
    return jnp.where(_on_diagonal(), w4, 0.0).reshape(MIX, MIX)


def _block_diag_inv(g):
    return jnp.sum(jnp.where(_on_diagonal(), g.reshape(8, 64, 8, 64), 0.0), axis=2)


def _head8(v):
    return jnp.pad(v[None, :], ((0, 7), (0, LANE - v.shape[0])))


GROUPS = dict(A=("w_in",), B=("w_uq", "w_ukv", "ssd_conv_w", "lru_conv_w", "w_branch", "w_out"),
              C=("w_ff1", "w_ff2", "w_ple_gate", "w_ple"))


def _kernel_weights(grp, fw):
    if grp == "A":
        w_in = _pad_w_in(fw["w_in"])
        return dict(w_in=w_in, w_dt=w_in[:, U_KR[0]:U_KR[0] + LANE])
    if grp == "C":
        return dict(w_ff1=fw["w_ff1"], w_ff2=fw["w_ff2"], w_pg=fw["w_ple_gate"], w_ple=fw["w_ple"])
    wb = fw["w_branch"]
    wb0 = jnp.pad(wb[0].reshape(N_HEADS, V_HEAD, D_MODEL), ((0, 0), (0, LANE - V_HEAD), (0, 0))).reshape(N_HEADS * LANE, D_MODEL)
    return dict(
        w_uq=_head_pad_cols(fw["w_uq"], QK_NOPE + QK_ROPE, 0, QK_NOPE + QK_ROPE),
        w_uk=_head_pad_cols(fw["w_ukv"], QK_NOPE + V_HEAD, 0, QK_NOPE),
        w_uv=_head_pad_cols(fw["w_ukv"], QK_NOPE + V_HEAD, QK_NOPE, QK_NOPE + V_HEAD),
        wb=[wb0, wb[1], wb[2], wb[3]], w_out=fw["w_out"], ssd_conv_w=fw["ssd_conv_w"], lru_conv_w=fw["lru_conv_w"])


def _layer_params(sp, l):
    row = lambda n: sp[n][l][None, :]
    return dict(
        g_mix=row("g_mix"), q_norm=row("q_norm"), kv_norm=row("kv_norm"),
        pool=[sp["w_pool"][l].reshape(4 * LANE, LANE), row("pool_scale")],
        ssd=[None, row("ssd_conv_b"), _head8(sp["ssd_dt_bias"][l]), _head8(sp["ssd_a_log"][l]),
             _head8(sp["ssd_d"][l]), row("ssd_norm")],
        lru=[None, row("lru_conv_b"), _block_diag(sp["lru_w_a"][l]), row("lru_b_a"),
             _block_diag(sp["lru_w_i"][l]), row("lru_b_i"), row("lru_lambda")],
        g_mlp=row("g_mlp"), g_ple=row("g_ple"),
    )


_sig = jax.nn.sigmoid
_SSD_CARRY = [(HALO, SSD_XBC)] + [(LANE, LANE)] * 4


def _tiles(rows):
    return dict(tm=_pick(rows, 1024), ta=_pick(rows, 512), tp=_pick(rows, 512), tl=_pick(rows, 512), ts=_pick(rows, 256),
                tssd=_pick(rows, SSD_CHUNK * SSD_CHUNKS_PER_TILE))


def _mixer_tiles(u, dt32):
    return dict(
        cq=(u, 384, U_CQ[0] // 384), ckv=(u, 256, U_CKV[0] // 256), kr=(u, LANE, U_KR[0] // LANE),
        pool=(u, MIX, U_POOL[0] // MIX), z=(u, MIX, U_Z[0] // MIX), xbc=(u, SSD_XBC, U_XBC[0] // SSD_XBC),
        dt=(dt32, LANE, 0), lg=(u, MIX, U_LG[0] // MIX), lx=(u, MIX, U_LX[0] // MIX))


def _add_norm(acc, resid, g):
    x = acc + resid
    return x, _rms(x, g)


def _layer_fwd(x, h, p_bf, ctx, l, pr, g_next, cosf, sinf, early=()):
    rows = x.shape[0]
    ts = _tiles(rows)
    tm = ts["tm"]
    nm = lambda s: f"{s}_l{l}"
    r = dict(x=x)
    if h is None:
        (h,), _ = seq_fwd(nm("rms_in"), f_rms, [pr["g_mix"]], [(x, D_MODEL, 0)], [], [(D_MODEL, bf16)], tm)
    w = dict(_kernel_weights("A", ctx.weights(l, "A", [h, *early])))
    u = matmul(nm("w_in"), h, w["w_in"], outs=(U_DTYPE,))
    dt32 = matmul(nm("w_dt"), h, w["w_dt"])
    mt = _mixer_tiles(u, dt32)
    (cqn,), _ = seq_fwd(nm("rms_q"), f_rms, [pr["q_norm"]], [mt["cq"]], [], [(Q_LORA, bf16)], tm)
    (ckvn,), _ = seq_fwd(nm("rms_kv"), f_rms, [pr["kv_norm"]], [mt["ckv"]], [], [(KV_LORA, bf16)], tm)
    (yb,), pool_saved = seq_fwd(nm("pool"), f_pool, pr["pool"], [mt["pool"]], [(POOL_HALO, MIX)], [(MIX, bf16)], ts["tp"])
    w.update(_kernel_weights("B", ctx.weights(l, "B", yb)))
    pr = dict(pr, ssd=[w["ssd_conv_w"]] + pr["ssd"][1:], lru=[w["lru_conv_w"]] + pr["lru"][1:])
    tables = [(cosf, 0, LANE), (sinf, 0, LANE)]
    qr = matmul(nm("w_uq"), cqn, w["w_uq"], outs=(bf16,), epi=q_rope_epi, extras=tables)
    kr = matmul(nm("w_uk"), ckvn, w["w_uk"], outs=(bf16,), epi=k_rope_epi, extras=[(u, U_KR[0], LANE)] + tables)
    vb = matmul(nm("w_uv"), ckvn, w["w_uv"], outs=(bf16,))
    o, lse = attn_fwd(qr, kr, vb, ts["ta"])
    (yc,), ssd_saved = seq_fwd(nm("ssd"), f_ssd_tile, pr["ssd"], [mt["z"], mt["xbc"], mt["dt"]], _SSD_CARRY, [(MIX, bf16)],
                               ts["tssd"])
    (la, lu), lru_saved = seq_fwd(nm("lru_pre"), f_lru_pre, pr["lru"], [mt["lx"]], [(HALO, MIX)], [(MIX, f32), (MIX, f32)], ts["tl"])
    hh, yd = scan_fwd(la, lu, mt["lg"], ts["ts"])
    ys = [o, yb, yc, yd]
    m, pres = merge_fwd(nm("merge"), ys, w["wb"], u)
    x1, h2 = matmul(nm("w_out"), m, w["w_out"], outs=(f32, bf16), epi=_add_norm, extras=[(x, 0)], rows=[pr["g_mlp"]])
    w.update(_kernel_weights("C", ctx.weights(l, "C", h2)))
    a1, act = matmul(nm("ff1"), h2, w["w_ff1"], outs=(bf16, bf16), epi=lambda acc: (acc, jnp.square(jnp.maximum(acc, 0.0))))
    x2, h3 = matmul(nm("ff2"), act, w["w_ff2"], outs=(f32, bf16), epi=_add_norm, extras=[(x1, 0)], rows=[pr["g_ple"]])
    gl = matmul(nm("ple_gate"), h3, w["w_pg"])
    if g_next is None:
        x3, pe = matmul(nm("ple"), p_bf, w["w_ple"], outs=(f32, bf16), epi=lambda acc, g, xr: (xr + acc * _sig(g), acc),
                        extras=[(gl, 0), (x2, 0)])
        h_next = None
    else:
        def ple_norm(acc, g, xr, gn):
            xo = xr + acc * _sig(g)
            return xo, acc, _rms(xo, gn)

        x3, pe, h_next = matmul(nm("ple"), p_bf, w["w_ple"], outs=(f32, bf16, bf16), epi=ple_norm,
                                extras=[(gl, 0), (x2, 0)], rows=[g_next])
    r.update(h=h, u=u, cqn=cqn, ckvn=ckvn, vb=vb, qr=qr, kr=kr, o=o, lse=lse, ys=ys, pres=pres, m=m, x1=x1,
             h2=h2, a1=a1, act=act, x2=x2, h3=h3, gl=gl, pe=pe, p_bf=p_bf, pool_saved=pool_saved, ssd_saved=ssd_saved,
             lru_saved=lru_saved, la=la, hh=hh, w=w, pr=pr, dt32=dt32)
    return x3, h_next, r


def _norm_bwd(dh, x, resid, g):
    rs = lax.rsqrt(jnp.mean(x * x, axis=-1, keepdims=True) + EPS)
    xhat = x * rs
    dxn = dh * g
    dx = rs * (dxn - xhat * jnp.mean(dxn * xhat, axis=-1, keepdims=True)) + resid
    return dx, jnp.sum(dh * xhat, axis=0, keepdims=True)


def _gate_bwd(d, g, pre):
    s = _sig(g.astype(f32))
    return d * s, d * pre.astype(f32) * s * (1.0 - s)


def _layer_bwd(dx3, r, ctx, l, cosf, sinf, tok, extra_small):
    rows = dx3.shape[0]
    ts = _tiles(rows)
    tm = ts["tm"]
    nm = lambda s: f"{s}_l{l}"
    u, w, pr = r["u"], r["w"], r["pr"]
    mt = _mixer_tiles(u, r["dt32"])
    g = {}
    full = lambda a: (a, a.shape[1], 0)
    dpe, dgl = ew(nm("ple_bwd"), _gate_bwd, [full(dx3), full(r["gl"]), full(r["pe"])], [(D_MODEL, bf16)] * 2, tm)
    g["w_ple"] = matmul(nm("d_w_ple"), r["p_bf"], dpe, ta=True, outs=(bf16,), deps=[tok] if tok is not None else [])
    g["w_pg"] = matmul(nm("d_w_pg"), r["h3"], dgl, ta=True, outs=(bf16,))
    dx2, g["g_ple"] = matmul(nm("d_h3"), dgl, w["w_pg"], tb=True, epi=_norm_bwd, extras=[(r["x2"], 0), (dx3, 0)],
                             rows=[pr["g_ple"]], row_sums=1)
    da1 = matmul(nm("d_act"), dx2, w["w_ff2"], tb=True, outs=(bf16,),
                 epi=lambda acc, a: (acc * 2.0 * jnp.maximum(a, 0.0),), extras=[(r["a1"], 0)])
    g["w_ff2"] = matmul(nm("d_w_ff2"), r["act"], dx2, ta=True, outs=(bf16,))
    g["w_ff1"] = matmul(nm("d_w_ff1"), r["h2"], da1, ta=True, outs=(bf16,), out_blocks=N_DEV)
    tok = ctx.grads(l, "C", dict(w_ff1=g["w_ff1"], w_ff2=g["w_ff2"], w_ple_gate=g["w_pg"], w_ple=g["w_ple"]))
    dx1, g["g_mlp"] = matmul(nm("d_h2"), da1, w["w_ff1"], tb=True, epi=_norm_bwd, extras=[(r["x1"], 0), (dx2, 0)],
                             rows=[pr["g_mlp"]], row_sums=1, deps=[tok])
    def merge_bwd(dm, *gates_and_pres):
        both = [_gate_bwd(dm, gates_and_pres[n], gates_and_pres[4 + n]) for n in range(4)]
        return tuple(b[0] for b in both) + tuple(b[1] for b in both)

    res = matmul(nm("d_merged"), dx1, w["w_out"], tb=True, outs=(bf16,) * 8, epi=merge_bwd,
                 extras=[(u, D_MODEL * n) for n in range(4)] + [(pre, 0) for pre in r["pres"]])
    dpres, dgates = list(res[:4]), list(res[4:])
    g["w_out"] = matmul(nm("d_w_out"), r["m"], dx1, ta=True, outs=(bf16,))
    dys, g["wb"] = [], []
    for n in range(4):
        g["wb"].append(matmul(nm(f"d_w_branch{n}"), r["ys"][n], dpres[n], ta=True, outs=(bf16,)))
        dys.append(matmul(nm(f"d_y{n}"), dpres[n], w["wb"][n], tb=True, outs=(bf16 if n == 0 else f32,)))
    g["ssd"], (dz, dxbc, ddt) = seq_bwd(nm("ssd_bwd"), f_ssd_tile, pr["ssd"], [mt["z"], mt["xbc"], mt["dt"]], [True] * 3,
                                        r["ssd_saved"], [dys[2]], [bf16] * 3, ts["tssd"])
    dqr, dkr_, dv = attn_bwd(r["qr"], r["kr"], r["vb"], dys[0], r["o"], r["lse"], ts["ta"])
    hw = N_HEADS * LANE
    dq, dkn, dkrope = ew(nm("rope_bwd"), rope_bwd, [full(dqr), full(dkr_), full(cosf), full(sinf), full(ddt)],
                         [(hw, bf16), (hw, bf16), (LANE, bf16)], tm)
    g["w_uq"] = matmul(nm("d_w_uq"), r["cqn"], dq, ta=True, outs=(bf16,))
    g["w_uk"] = matmul(nm("d_w_uk"), r["ckvn"], dkn, ta=True, outs=(bf16,))
    g["w_uv"] = matmul(nm("d_w_uv"), r["ckvn"], dv, ta=True, outs=(bf16,))
    dcqn = matmul(nm("d_cqn"), dq, w["w_uq"], tb=True)
    dckvn = matmul(nm("d_ckvn_k"), dkn, w["w_uk"], tb=True)
    dckvn = matmul(nm("d_ckvn_v"), dv, w["w_uv"], tb=True, epi=lambda acc, prev: (acc + prev,), extras=[(dckvn, 0)])
    (g["q_norm"],), (dcq,) = seq_bwd(nm("rms_q_bwd"), f_rms, [pr["q_norm"]], [mt["cq"]], [True], [], [dcqn], [bf16], tm)
    (g["kv_norm"],), (dckv,) = seq_bwd(nm("rms_kv_bwd"), f_rms, [pr["kv_norm"]], [mt["ckv"]], [True], [], [dckvn], [bf16], tm)
    g["pool"], (dpool,) = seq_bwd(nm("pool_bwd"), f_pool, pr["pool"], [mt["pool"]], [True], r["pool_saved"], [dys[1]],
                                  [bf16], ts["tp"])
    da, du, dlg = scan_bwd(r["la"], r["hh"], mt["lg"], dys[3], ts["ts"])
    g["lru"], (dlx,) = seq_bwd(nm("lru_pre_bwd"), f_lru_pre, pr["lru"], [mt["lx"]], [True], r["lru_saved"], [da, du],
                               [bf16], ts["tl"])
    dk = _head_unpad_cols(g["w_uk"], QK_NOPE)
    dv_ = _head_unpad_cols(g["w_uv"], V_HEAD)
    wb0 = g["wb"][0].reshape(N_HEADS, LANE, D_MODEL)[:, :V_HEAD].reshape(MIX, D_MODEL)
    ssd, lru, pool = g["ssd"], g["lru"], g["pool"]
    tok = ctx.grads(l, "B", dict(
        w_uq=_head_unpad_cols(g["w_uq"], QK_NOPE + QK_ROPE).reshape(Q_LORA, -1),
        w_ukv=jnp.concatenate([dk, dv_], axis=2).reshape(KV_LORA, -1), ssd_conv_w=ssd[0], lru_conv_w=lru[0],
        w_branch=jnp.stack([wb0, g["wb"][1], g["wb"][2], g["wb"][3]]), w_out=g["w_out"]))
    du_p = jnp.concatenate(dgates + [dpool, dz, dlg, dlx, dxbc, dcq, dkrope, dckv], axis=1)
    small = dict(
        q_norm=g["q_norm"][0], kv_norm=g["kv_norm"][0],
        w_pool=pool[0].reshape(4, LANE, LANE), pool_scale=pool[1][0],
        ssd_conv_b=ssd[1][0], ssd_dt_bias=ssd[2][0, :8], ssd_a_log=ssd[3][0, :8], ssd_d=ssd[4][0, :8], ssd_norm=ssd[5][0],
        lru_conv_b=lru[1][0], lru_w_a=_block_diag_inv(lru[2]), lru_b_a=lru[3][0], lru_w_i=_block_diag_inv(lru[4]),
        lru_b_i=lru[5][0], lru_lambda=lru[6][0], g_mlp=g["g_mlp"][0], g_ple=g["g_ple"][0])
    tok_small = ctx.small(f"l{l}", [(n, l, small[n]) for n in SMALL if n in small] + extra_small)
    g_w_in = matmul(nm("d_w_in"), r["h"], du_p, ta=True, outs=(bf16,), deps=[tok, tok_small])
    tok = ctx.grads(l, "A", dict(w_in=_w_in_blocks(g_w_in)))
    dx, g_mix = matmul(nm("d_h"), du_p, w["w_in"], tb=True, epi=_norm_bwd, extras=[(r["x"], 0), (dx1, 0)],
                       rows=[pr["g_mix"]], row_sums=1, deps=[tok])
    return dx, tok, ("g_mix", l, g_mix[0])


def _rope_tables(positions):
    inv = 1.0 / (ROPE_THETA ** (jnp.arange(0, QK_ROPE, 2, dtype=f32) / QK_ROPE))
    ang = positions.astype(f32)[:, None] * inv
    cos, sin = jnp.cos(ang), jnp.sin(ang)
    rows = positions.shape[0]
    pad = jnp.zeros((rows, LANE - KR_LANE - QK_ROPE), f32)
    cosf = jnp.concatenate([jnp.ones((rows, KR_LANE), f32), cos, cos, pad], axis=1)
    sinf = jnp.concatenate([jnp.zeros((rows, KR_LANE), f32), -sin, sin, pad], axis=1)
    return cosf, sinf


WEIGHTS = ['g_mix', 'w_in', 'q_norm', 'w_uq', 'kv_norm', 'w_ukv', 'w_pool', 'pool_scale', 'ssd_conv_w', 'ssd_conv_b',
           'ssd_dt_bias', 'ssd_a_log', 'ssd_d', 'ssd_norm', 'lru_conv_w', 'lru_conv_b', 'lru_w_a', 'lru_b_a', 'lru_w_i',
           'lru_b_i', 'lru_lambda', 'w_branch', 'w_out', 'g_mlp', 'w_ff1', 'w_ff2', 'g_ple', 'w_ple_gate', 'w_ple', 'g_final']
SHARDED = dict(w_in=2, w_uq=2, w_ukv=2, ssd_conv_w=2, lru_conv_w=2, w_branch=3, w_out=1, w_ff1=2, w_ff2=1,
               w_ple_gate=1, w_ple=2)
F32_PAYLOAD = ("ssd_conv_w", "lru_conv_w")
DEPTH = 2


SMALL = [n for n in WEIGHTS if n not in SHARDED and n != "g_final"]


def local_step(x, p, positions, tgt, sp, ctx):
    cosf, sinf = _rope_tables(positions)
    prs = [_layer_params(sp, l) for l in range(DEPTH)]
    p_bf = [p[l].astype(bf16) for l in range(DEPTH)]
    early = [cosf, sinf, *p_bf] + [a for pr in prs for v in pr.values() for a in (v if isinstance(v, list) else [v])
                                   if a is not None]
    res, h = [], None
    for l in range(DEPTH):
        g_next = prs[l + 1]["g_mix"] if l + 1 < DEPTH else None
        x, h, r = _layer_fwd(x, h, p_bf[l], ctx, l, prs[l], g_next, cosf, sinf, early if l == 0 else ())
        res.append(r)
    loss8, dx, dgf = loss_head(x, tgt, sp["g_final"][None, :], _tiles(x.shape[0])["tm"])
    tok = None
    pending = ("g_final", None, dgf[0])
    for l in reversed(range(DEPTH)):
        dx, tok, pending = _layer_bwd(dx, res[l], ctx, l, cosf, sinf, tok, [pending])
    ctx.small("last", [pending])
    return loss8[0, 0], dx


def _payload(name, w):
    return w if name in F32_PAYLOAD else w.astype(bf16)


def _blocks(name, g):
    ax = SHARDED[name] - 1
    shape = list(g.shape)
    shape[ax:ax + 1] = [N_DEV, shape[ax] // N_DEV]
    return _payload(name, jnp.moveaxis(g.reshape(shape), ax, 0))


def _assemble(name, shards):
    ax = SHARDED[name] - 1
    shape = list(shards.shape[1:])
    shape[ax] *= N_DEV
    return jnp.moveaxis(shards, 0, ax).reshape(shape)


class _Exchanges:
    def __init__(self, wts):
        self.wts = wts
        self.ag, self.rs, self.sm = {}, {}, {}
        tok = None
        for l in range(DEPTH):
            for grp, names in GROUPS.items():
                h = exchange_start(f"ag_start_{grp}{l}", [_payload(n, wts[n][l]) for n in names], True,
                                   deps=[] if tok is None else [tok])
                tok = h["token"]
                self.ag[(l, grp)] = h
        self.all_started = tok

    def weights(self, l, grp, after):
        afters = list(after) if isinstance(after, (list, tuple)) else [after]
        if (l, grp) == (0, "A"):
            afters.append(self.all_started)
        got = exchange_wait(f"ag_wait_{grp}{l}", self.ag[(l, grp)], afters)
        out = {}
        for n, a in zip(GROUPS[grp], got):
            out[n] = a if n == "w_in" else _assemble(n, a)
        return out

    def grads(self, l, grp, g):
        cut = lambda n: g[n].ndim == self.wts[n].ndim
        h = exchange_start(f"rs_start_{grp}{l}", [g[n] if cut(n) else _blocks(n, g[n]) for n in GROUPS[grp]], False)
        self.rs[(l, grp)] = h
        return h["token"]

    def small(self, tag, entries):
        entries = sorted(entries, key=lambda e: e[2].size % LANE != 0)
        flat = jnp.concatenate([a.reshape(-1) for _, _, a in entries])
        flat = jnp.pad(flat, (0, (-flat.shape[0]) % (8 * LANE))).reshape(-1, LANE)
        h = exchange_start(f"small_start_{tag}", [flat], True)
        self.sm[tag] = (h, [(n, l, a.shape) for n, l, a in entries])
        return h["token"]

    def collect(self, groups, after):
        parts = {}
        for grp in groups:
            for l in reversed(range(DEPTH)):
                got = exchange_wait(f"rs_wait_{grp}{l}", self.rs[(l, grp)], [after])
                for n, a in zip(GROUPS[grp], got):
                    parts.setdefault(n, [None] * DEPTH)[l] = a
        return parts

    def collect_small(self, after):
        gots, where, parts = [], {}, {}
        for tag, (h, layout) in self.sm.items():
            (got,) = exchange_wait(f"small_wait_{tag}", h, [after])
            got = got.reshape(N_DEV, -1)
            off = 0
            for n, l, shape in layout:
                size = 1
                for d in shape:
                    size *= d
                if len(shape) == 1 and size % LANE == 0 and off % LANE == 0:
                    where.setdefault(n, [None] * (1 if l is None else DEPTH))[l or 0] = (len(gots), off)
                else:
                    part = got[:, off:off + size].reshape((N_DEV,) + tuple(shape))
                    if l is None:
                        parts[n] = [part]
                    else:
                        parts.setdefault(n, [None] * DEPTH)[l] = part
                off += size
            gots.append(got)
        return gots, where, parts


def kernel(x, p, positions, g_mix, w_in, q_norm, w_uq, kv_norm, w_ukv, w_pool, pool_scale, ssd_conv_w, ssd_conv_b,
           ssd_dt_bias, ssd_a_log, ssd_d, ssd_norm, lru_conv_w, lru_conv_b, lru_w_a, lru_b_a, lru_w_i, lru_b_i,
           lru_lambda, w_branch, w_out, g_mlp, w_ff1, w_ff2, g_ple, w_ple_gate, w_ple, g_final, loss_target, m_g_mix,
           m_w_in, m_q_norm, m_w_uq, m_kv_norm, m_w_ukv, m_w_pool, m_pool_scale, m_ssd_conv_w, m_ssd_conv_b,
           m_ssd_dt_bias, m_ssd_a_log, m_ssd_d, m_ssd_norm, m_lru_conv_w, m_lru_conv_b, m_lru_w_a, m_lru_b_a,
           m_lru_w_i, m_lru_b_i, m_lru_lambda, m_w_branch, m_w_out, m_g_mlp, m_w_ff1, m_w_ff2, m_g_ple, m_w_ple_gate,
           m_w_ple, m_g_final, v_g_mix, v_w_in, v_q_norm, v_w_uq, v_kv_norm, v_w_ukv, v_w_pool, v_pool_scale,
           v_ssd_conv_w, v_ssd_conv_b, v_ssd_dt_bias, v_ssd_a_log, v_ssd_d, v_ssd_norm, v_lru_conv_w, v_lru_conv_b,
           v_lru_w_a, v_lru_b_a, v_lru_w_i, v_lru_b_i, v_lru_lambda, v_w_branch, v_w_out, v_g_mlp, v_w_ff1, v_w_ff2,
           v_g_ple, v_w_ple_gate, v_w_ple, v_g_final):
    given = dict(locals())
    wts = {n: given[n] for n in WEIGHTS}
    ctx = _Exchanges(wts)
    loss, grad_x = local_step(x[0], p[:, 0], positions[0], loss_target[0], wts, ctx)

    def update(parts):
        out = {}
        for n, eight in parts.items():
            step = adamw_columns if n == "w_in" else adamw
            out[n] = step(f"adamw_{n}", eight, wts[n], given["m_" + n], given["v_" + n])
        return out

    outs = update(ctx.collect(("C", "B"), grad_x))
    late = outs["w_ff1"][1]
    outs.update(update(ctx.collect(("A",), late)))
    gots, where, parts = ctx.collect_small(late)
    outs.update(update(parts))
    names = sorted(where)
    rows = lambda a: a[None] if a.ndim == 1 else a
    res = adamw_packed("adamw_vectors", gots, [where[n] for n in names], [rows(wts[n]) for n in names],
                       [rows(given["m_" + n]) for n in names], [rows(given["v_" + n]) for n in names])
    for n, four in zip(names, res):
        outs[n] = [a[0] for a in four] if wts[n].ndim == 1 else four
    loss = lax.psum(loss, AXES)
    return (loss, grad_x[None], *[outs[n][0] for n in WEIGHTS], *[outs[n][1] for n in WEIGHTS],
            *[outs[n][2] for n in WEIGHTS], *[outs[n][3] for n in WEIGHTS])
```

```python
import functools

import jax
import jax.numpy as jnp
from jax import lax
from jax.experimental import pallas as pl
from jax.experimental.pallas import tpu as pltpu

f32 = jnp.float32
bf16 = jnp.bfloat16

D_MODEL = 1024
MIX = 512
N_HEADS = 8
QK_NOPE, QK_ROPE, V_HEAD = 64, 32, 64
Q_LORA, KV_LORA = 384, 256
ROPE_THETA = 10000.0
POOL_WINDOWS = (2, 4, 8, 16)
SSD_CHUNK = 128
SSD_CHUNKS_PER_TILE = 2
SSD_XBC = 768
CONV_W = 4
LRU_C = 8.0
EPS = 1e-6
IN_COLS = 7592
ADAM_LR, ADAM_B1, ADAM_B2, ADAM_EPS, ADAM_WD, ADAM_STEP = 0.001, 0.9, 0.999, 1e-08, 0.01, 10

LANE = 128
HALO = 8
POOL_HALO = 16
VMEM_LIMIT = 56 * 1024 * 1024
MATMUL_MAX_K_TILE = 4096
MATMUL_ACC_PASS_WEIGHT = 0.3
MATMUL_VMEM_BUDGET = 40 * 1024 * 1024
N_DEV = 8
AXES = ("x", "y", "c")

U_COLS = 7680
U_GATES, U_POOL, U_Z, U_LG, U_LX, U_XBC, U_CQ, U_KR, U_CKV = (
    (0, 4096), (4096, 512), (4608, 512), (5120, 512), (5632, 512), (6144, 768), (6912, 384), (7296, 128), (7424, 256))
KR_LANE = 64
DT_LANES = 8
U_DTYPE = bf16


def _cp(sem):
    return pltpu.CompilerParams(dimension_semantics=sem, vmem_limit_bytes=VMEM_LIMIT)


def _pick(dim, pref):
    if dim <= pref:
        return dim
    t = pref
    while t >= LANE:
        if dim % t == 0:
            return t
        t -= LANE
    t = pref
    while dim % t:
        t -= 8
    return t


@functools.partial(jax.custom_vjp, nondiff_argnums=(1,))
def shift_down(x, k):
    row = lax.broadcasted_iota(jnp.int32, x.shape, 0)
    return jnp.where(row >= k, pltpu.roll(x, k, 0), 0.0)


def _shift_down_fwd(x, k):
    return shift_down(x, k), None


def _shift_down_bwd(k, _, g):
    r = g.shape[0]
    row = lax.broadcasted_iota(jnp.int32, g.shape, 0)
    return (jnp.where(row < r - k, pltpu.roll(g, r - k, 0), 0.0),)


shift_down.defvjp(_shift_down_fwd, _shift_down_bwd)


def _tile_spec(tm, width, cb, n=None):
    if n is None:
        return pl.BlockSpec((tm, width), lambda i: (i, cb))
    return pl.BlockSpec((tm, width), lambda i: (n - 1 - i, cb))


def _const_spec(shape):
    nd = len(shape)
    return pl.BlockSpec(shape, lambda i: (0,) * nd)


def seq_fwd(name, f, params, tiles, carries, outs, tm):
    rows = tiles[0][0].shape[0]
    n = rows // tm
    np_, nt, no, nc = len(params), len(tiles), len(outs), len(carries)

    def body(*refs):
        p_refs = refs[:np_]
        t_refs = refs[np_:np_ + nt]
        o_refs = refs[np_ + nt:np_ + nt + no]
        s_refs = refs[np_ + nt + no:np_ + nt + no + nc]
        c_refs = refs[np_ + nt + no + nc:]
        i = pl.program_id(0)

        @pl.when(i == 0)
        def _():
            for c in c_refs:
                c[...] = jnp.zeros_like(c)

        cvals = [c[...] for c in c_refs]
        for s, c in zip(s_refs, cvals):
            s[0] = c
        o, newc = f(i, [r[...] for r in p_refs], cvals, [r[...].astype(f32) for r in t_refs])
        for r, v in zip(o_refs, o):
            r[...] = v.astype(r.dtype)
        for r, v in zip(c_refs, newc):
            r[...] = v

    in_specs = [_const_spec(p.shape) for p in params] + [_tile_spec(tm, w, cb) for (_, w, cb) in tiles]
    out_specs = [_tile_spec(tm, w, 0) for (w, _) in outs]
    out_specs += [pl.BlockSpec((1,) + tuple(c), lambda i, nd=len(c): (i,) + (0,) * nd) for c in carries]
    out_shape = [jax.ShapeDtypeStruct((rows, w), dt) for (w, dt) in outs]
    out_shape += [jax.ShapeDtypeStruct((n,) + tuple(c), f32) for c in carries]
    res = pl.pallas_call(
        body, name=name, grid=(n,), in_specs=in_specs, out_specs=out_specs, out_shape=out_shape,
        scratch_shapes=[pltpu.VMEM(tuple(c), f32) for c in carries],
        compiler_params=_cp(("arbitrary",)),
    )(*params, *[t[0] for t in tiles])
    return list(res[:no]), list(res[no:])


def seq_bwd(name, f, params, tiles, diff, saved, douts, gdtypes, tm):
    rows = tiles[0][0].shape[0]
    n = rows // tm
    np_, nt, nc, nd = len(params), len(tiles), len(saved), len(douts)
    didx = [k for k, d in enumerate(diff) if d]
    ng = len(didx)

    def body(*refs):
        p_refs = refs[:np_]
        t_refs = refs[np_:np_ + nt]
        s_refs = refs[np_ + nt:np_ + nt + nc]
        d_refs = refs[np_ + nt + nc:np_ + nt + nc + nd]
        pos = np_ + nt + nc + nd
        dp_refs = refs[pos:pos + np_]
        dt_refs = refs[pos + np_:pos + np_ + ng]
        dc_refs = refs[pos + np_ + ng:]
        i = pl.program_id(0)
        step = n - 1 - i

        @pl.when(i == 0)
        def _():
            for r in dp_refs:
                r[...] = jnp.zeros_like(r)
            for r in dc_refs:
                r[...] = jnp.zeros_like(r)

        pvals = [r[...] for r in p_refs]
        cvals = [r[0] for r in s_refs]
        xvals = [r[...].astype(f32) for r in t_refs]

        def fn(p, c, xd):
            x = list(xvals)
            for k, v in zip(didx, xd):
                x[k] = v
            return f(step, p, c, x)

        _, vjp = jax.vjp(fn, pvals, cvals, [xvals[k] for k in didx])
        dp, dc, dx = vjp(([r[...].astype(f32) for r in d_refs], [r[...] for r in dc_refs]))
        for r, v in zip(dp_refs, dp):
            r[...] += v
        for r, v in zip(dc_refs, dc):
            r[...] = v
        for r, v in zip(dt_refs, dx):
            r[...] = v.astype(r.dtype)

    in_specs = [_const_spec(p.shape) for p in params] + [_tile_spec(tm, w, cb, n) for (_, w, cb) in tiles]
    in_specs += [pl.BlockSpec((1,) + tuple(s.shape[1:]), lambda i, nd_=s.ndim - 1: (n - 1 - i,) + (0,) * nd_) for s in saved]
    in_specs += [_tile_spec(tm, d.shape[1], 0, n) for d in douts]
    args = list(params) + [t[0] for t in tiles] + list(saved) + list(douts)
    out_specs = [_const_spec(p.shape) for p in params] + [_tile_spec(tm, tiles[k][1], 0, n) for k in didx]
    out_shape = [jax.ShapeDtypeStruct(p.shape, f32) for p in params]
    out_shape += [jax.ShapeDtypeStruct((rows, tiles[k][1]), dt) for k, dt in zip(didx, gdtypes)]
    res = pl.pallas_call(
        body, name=name, grid=(n,), in_specs=in_specs, out_specs=out_specs, out_shape=out_shape,
        scratch_shapes=[pltpu.VMEM(tuple(s.shape[1:]), f32) for s in saved],
        compiler_params=_cp(("arbitrary",)),
    )(*args)
    return list(res[:np_]), list(res[np_:])


def _halvings(dim, lo, hi):
    t, out = _pick(dim, hi), []
    while t >= min(lo, dim) and dim % t == 0:
        out.append(t)
        if t % 2 or (t // 2) % LANE:
            break
        t //= 2
    return out


def _matmul_tiles(m, n, k, a_item, b_item, per_out, max_tn=1024, whole_rows=False):
    def vmem_bytes(tm, tn, tk):
        acc = 4 if k // tk > 1 else 0
        return 2 * (tm * tk * a_item + tk * tn * b_item + tm * tn * per_out) + tm * tn * acc

    def traffic(tm, tn, tk):
        nk = k // tk
        return (m * k * a_item * (1 if nk == 1 else n // tn) + k * n * b_item * (m // tm)
                + (nk - 1) * m * n * 8 * MATMUL_ACC_PASS_WEIGHT)

    cands = [(traffic(tm, tn, tk), -tm * tn, tm, tn, tk)
             for tk in _halvings(k, 512, MATMUL_MAX_K_TILE) for tm in _halvings(m, 256, 4096)
             for tn in ([n] if whole_rows else _halvings(n, 512, min(1024, max_tn)))
             if vmem_bytes(tm, tn, tk) <= MATMUL_VMEM_BUDGET]
    return min(cands)[2:]


def matmul(name, a, b, *, ta=False, tb=False, outs=(f32,), epi=None, extras=(), rows=(), row_sums=0, deps=(),
           out_blocks=0):
    m, k = (a.shape[1], a.shape[0]) if ta else a.shape
    n = b.shape[0] if tb else b.shape[1]
    per_out = sum(jnp.dtype(dt).itemsize for dt in outs) + sum(e[0].dtype.itemsize for e in extras)
    whole_rows = bool(rows) or row_sums > 0
    tm, tn, tk = _matmul_tiles(m, n, k, a.dtype.itemsize, b.dtype.itemsize, per_out,
                               n // out_blocks if out_blocks else n, whole_rows)
    nk = k // tk
    ne, nr, nd, no = len(extras), len(rows), len(deps), len(outs)
    dims = (((0 if ta else 1,), (1 if tb else 0,)), ((), ()))

    def body(*refs):
        a_ref, b_ref = refs[0], refs[1]
        e_refs = refs[2:2 + ne]
        r_refs = refs[2 + ne:2 + ne + nr]
        o_refs = refs[2 + ne + nr + nd:2 + ne + nr + nd + no]
        s_refs = refs[2 + ne + nr + nd + no:2 + ne + nr + nd + no + row_sums]
        i, kk = pl.program_id(0), pl.program_id(2)
        part = lax.dot_general(a_ref[...].astype(bf16), b_ref[...].astype(bf16), dims, preferred_element_type=f32)

        def finish(total):
            res = (total,) if epi is None else epi(total, *[e[...] for e in e_refs], *[r[...] for r in r_refs])
            for r, v in zip(o_refs, res[:no]):
                r[...] = v.astype(r.dtype)
            for r, v in zip(s_refs, res[no:]):
                v8 = jnp.broadcast_to(v, r.shape)

                @pl.when(i == 0)
                def _(r=r, v8=v8):
                    r[...] = v8

                @pl.when(i > 0)
                def _(r=r, v8=v8):
                    r[...] += v8

        if nk == 1:
            finish(part)
            return
        acc = refs[-1]

        @pl.when(kk == 0)
        def _():
            acc[...] = part

        @pl.when(jnp.logical_and(kk > 0, kk < nk - 1))
        def _():
            acc[...] += part

        @pl.when(kk == nk - 1)
        def _():
            finish(acc[...] + part)

    a_spec = pl.BlockSpec((tk, tm), lambda i, j, q: (q, i)) if ta else pl.BlockSpec((tm, tk), lambda i, j, q: (i, q))
    b_spec = pl.BlockSpec((tn, tk), lambda i, j, q: (j, q)) if tb else pl.BlockSpec((tk, tn), lambda i, j, q: (q, j))
    def e_spec(e):
        if len(e) == 3:
            return pl.BlockSpec((tm, e[2]), lambda i, j, q, cb=e[1] // e[2]: (i, cb))
        assert e[1] % tn == 0
        return pl.BlockSpec((tm, tn), lambda i, j, q, off=e[1] // tn: (i, off + j))

    e_specs = [e_spec(e) for e in extras]
    r_specs = [pl.BlockSpec((1, tn), lambda i, j, q: (0, j)) for _ in rows]
    if out_blocks:
        per = n // out_blocks // tn
        out_spec = pl.BlockSpec((None, tm, tn), lambda i, j, q: (j // per, i, j % per))
        out_dims = (out_blocks, m, n // out_blocks)
    else:
        out_spec = pl.BlockSpec((tm, tn), lambda i, j, q: (i, j))
        out_dims = (m, n)
    res = pl.pallas_call(
        body, name=name, grid=(m // tm, n // tn, nk),
        in_specs=[a_spec, b_spec] + e_specs + r_specs + [pl.BlockSpec(memory_space=pl.ANY) for _ in deps],
        out_specs=[out_spec for _ in outs] + [pl.BlockSpec((8, tn), lambda i, j, q: (0, j))] * row_sums,
        out_shape=[jax.ShapeDtypeStruct(out_dims, dt) for dt in outs] + [jax.ShapeDtypeStruct((8, n), f32)] * row_sums,
        scratch_shapes=[pltpu.VMEM((tm, tn), f32)] if nk > 1 else [],
        compiler_params=_cp(("arbitrary" if row_sums else "parallel", "parallel", "arbitrary")),
    )(a, b, *[e[0] for e in extras], *rows, *deps)
    return res[0] if len(res) == 1 else tuple(res)


def merge_fwd(name, ys, wbs, u):
    rows, n_out = ys[0].shape[0], wbs[0].shape[1]
    nb = len(ys)
    tm, tn = _pick(rows, 1024), _pick(n_out, 512)

    def body(*refs):
        y_refs, w_refs, g_refs = refs[:nb], refs[nb:2 * nb], refs[2 * nb:3 * nb]
        m_ref, p_refs = refs[3 * nb], refs[3 * nb + 1:]
        total = None
        for y_ref, w_ref, g_ref, p_ref in zip(y_refs, w_refs, g_refs, p_refs):
            pre = jnp.dot(y_ref[...], w_ref[...], preferred_element_type=f32)
            p_ref[...] = pre.astype(p_ref.dtype)
            term = jax.nn.sigmoid(g_ref[...].astype(f32)) * pre
            total = term if total is None else total + term
        m_ref[...] = total.astype(m_ref.dtype)

    in_specs = [pl.BlockSpec((tm, y.shape[1]), lambda i, j: (i, 0)) for y in ys]
    in_specs += [pl.BlockSpec((w.shape[0], tn), lambda i, j: (0, j)) for w in wbs]
    in_specs += [pl.BlockSpec((tm, tn), lambda i, j, off=n * (n_out // tn): (i, off + j)) for n in range(nb)]
    out_spec = pl.BlockSpec((tm, tn), lambda i, j: (i, j))
    res = pl.pallas_call(
        body, name=name, grid=(rows // tm, n_out // tn), in_specs=in_specs, out_specs=[out_spec] * (nb + 1),
        out_shape=[jax.ShapeDtypeStruct((rows, n_out), bf16)] * (nb + 1),
        compiler_params=_cp(("parallel", "parallel")),
    )(*ys, *wbs, *([u] * nb))
    return res[0], list(res[1:])


ATT_SCALE = (QK_NOPE + QK_ROPE) ** -0.5
LN2 = 0.6931471805599453
ATT_C = ATT_SCALE / LN2
NT = (((1,), (1,)), ((), ()))
TN = (((0,), (0,)), ((), ()))


def _causal(tq, tk):
    return lax.broadcasted_iota(jnp.int32, (tq, tk), 0) >= lax.broadcasted_iota(jnp.int32, (tq, tk), 1)


def _tri_pairs(n, by_column):
    if by_column:
        pairs = [(i, j) for j in range(n) for i in range(j, n)]
    else:
        pairs = [(i, j) for i in range(n) for j in range(i + 1)]
    return (jnp.asarray([a for a, _ in pairs], jnp.int32), jnp.asarray([b for _, b in pairs], jnp.int32))


FWD_HEADS_PER_STEP = 8
HEADS_PER_STEP = 4
HEAD_PAIR = HEADS_PER_STEP * LANE


def attn_fwd(q, k, v, t):
    rows = q.shape[0]
    n = rows // t
    it, jt = _tri_pairs(n, False)

    def body(it_ref, jt_ref, q_ref, k_ref, v_ref, o_ref, lse_ref, m_s, l_s, acc_s):
        s_id = pl.program_id(1)
        i, j = it_ref[s_id], jt_ref[s_id]

        @pl.when(j == 0)
        def _():
            m_s[...] = jnp.full_like(m_s, -jnp.inf)
            l_s[...] = jnp.zeros_like(l_s)
            acc_s[...] = jnp.zeros_like(acc_s)

        def step(diag):
            for hh in range(FWD_HEADS_PER_STEP):
                sl = slice(LANE * hh, LANE * (hh + 1))
                s = lax.dot_general(q_ref[:, sl], k_ref[:, sl], NT, preferred_element_type=f32)
                if diag:
                    s = jnp.where(_causal(t, t), s, -jnp.inf)
                m_prev = m_s[:, sl]
                m_new = jnp.maximum(m_prev, jnp.max(s, axis=1, keepdims=True))
                alpha = jnp.exp2(m_prev - m_new)
                p = jnp.exp2(s - m_new[:, :1])
                l_s[:, sl] = alpha * l_s[:, sl] + jnp.sum(p, axis=1, keepdims=True)
                acc_s[:, sl] = alpha * acc_s[:, sl] + jnp.dot(p.astype(bf16), v_ref[:, sl], preferred_element_type=f32)
                m_s[:, sl] = m_new

        pl.when(j < i)(lambda: step(False))

        @pl.when(j == i)
        def _():
            step(True)
            o_ref[...] = (acc_s[...] / l_s[...]).astype(o_ref.dtype)
            lse = m_s[...] + jnp.log2(l_s[...])
            lane = lax.broadcasted_iota(jnp.int32, (1, LANE), 1)
            lse_ref[...] = sum(jnp.where(lane == h, lse[:, LANE * h:LANE * h + 1], 0.0) for h in range(N_HEADS))

    width = FWD_HEADS_PER_STEP * LANE
    assert width == N_HEADS * LANE
    qs = pl.BlockSpec((t, width), lambda h, s, it_, jt_: (it_[s], h))
    ks = pl.BlockSpec((t, width), lambda h, s, it_, jt_: (jt_[s], h))
    ls = pl.BlockSpec((t, LANE), lambda h, s, it_, jt_: (it_[s], 0))
    hw = N_HEADS * LANE
    return pl.pallas_call(
        body, name="attn_fwd",
        grid_spec=pltpu.PrefetchScalarGridSpec(
            num_scalar_prefetch=2, grid=(hw // width, it.shape[0]), in_specs=[qs, ks, ks], out_specs=[qs, ls],
            scratch_shapes=[pltpu.VMEM((t, width), f32)] * 3),
        out_shape=[jax.ShapeDtypeStruct((rows, hw), bf16), jax.ShapeDtypeStruct((rows, LANE), f32)],
        compiler_params=_cp(("parallel", "arbitrary")),
    )(it, jt, q, k, v)


def attn_bwd(q, k, v, do, o, lse, t):
    rows = q.shape[0]
    n = rows // t
    it, jt = _tri_pairs(n, True)

    def body(it_ref, jt_ref, q_ref, k_ref, v_ref, do_ref, o_ref, lse_ref, dq_ref, dk_ref, dv_ref, dk_s, dv_s):
        s_id = pl.program_id(1)
        i, j = it_ref[s_id], jt_ref[s_id]

        @pl.when(s_id == 0)
        def _():
            dq_ref[...] = jnp.zeros_like(dq_ref)

        @pl.when(i == j)
        def _():
            dk_s[...] = jnp.zeros_like(dk_s)
            dv_s[...] = jnp.zeros_like(dv_s)

        q_rows = pl.ds(pl.multiple_of(i * t, t), t)
        head0 = pl.program_id(0) * HEADS_PER_STEP

        def step(diag):
            for hh in range(HEADS_PER_STEP):
                sl = slice(LANE * hh, LANE * (hh + 1))
                qh, kh, vh, doh = q_ref[:, sl], k_ref[:, sl], v_ref[:, sl], do_ref[:, sl]
                s = lax.dot_general(qh, kh, NT, preferred_element_type=f32)
                lane = lax.broadcasted_iota(jnp.int32, (1, LANE), 1)
                p = jnp.exp2(s - jnp.sum(jnp.where(lane == head0 + hh, lse_ref[...], 0.0), axis=1, keepdims=True))
                if diag:
                    p = jnp.where(_causal(t, t), p, 0.0)
                dp = lax.dot_general(doh, vh, NT, preferred_element_type=f32)
                delta = jnp.sum(doh.astype(f32) * o_ref[:, sl].astype(f32), axis=1, keepdims=True)
                ds = (p * (dp - delta) * LN2).astype(bf16)
                dv_s[:, sl] += lax.dot_general(p.astype(bf16), doh, TN, preferred_element_type=f32)
                dk_s[:, sl] += lax.dot_general(ds, qh, TN, preferred_element_type=f32)
                dq_ref[q_rows, sl] += jnp.dot(ds, kh, preferred_element_type=f32)

        pl.when(i > j)(lambda: step(False))
        pl.when(i == j)(lambda: step(True))

        @pl.when(i == n - 1)
        def _():
            dk_ref[...] = dk_s[...]
            dv_ref[...] = dv_s[...]

    qs = pl.BlockSpec((t, HEAD_PAIR), lambda h, s, it_, jt_: (it_[s], h))
    ks = pl.BlockSpec((t, HEAD_PAIR), lambda h, s, it_, jt_: (jt_[s], h))
    dqs = pl.BlockSpec((rows, HEAD_PAIR), lambda h, s, it_, jt_: (0, h))
    ls = pl.BlockSpec((t, LANE), lambda h, s, it_, jt_: (it_[s], 0))
    hw = N_HEADS * LANE
    return pl.pallas_call(
        body, name="attn_bwd",
        grid_spec=pltpu.PrefetchScalarGridSpec(
            num_scalar_prefetch=2, grid=(hw // HEAD_PAIR, it.shape[0]), in_specs=[qs, ks, ks, qs, qs, ls],
            out_specs=[dqs, ks, ks], scratch_shapes=[pltpu.VMEM((t, HEAD_PAIR), f32)] * 2),
        out_shape=[jax.ShapeDtypeStruct((rows, hw), f32)] * 3,
        compiler_params=_cp(("parallel", "arbitrary")),
    )(it, jt, q, k, v, do, o, lse)


def _steps(tm):
    k, out = 1, []
    while k < tm:
        out.append(k)
        k *= 2
    return out


def _gelu_gate(h, g):
    return h * jax.nn.gelu(g)


def scan_fwd(a, u, gate, tm):
    rows, ch = a.shape
    n = rows // tm

    def body(a_ref, u_ref, gt_ref, h_ref, y_ref, h_s):
        @pl.when(pl.program_id(0) == 0)
        def _():
            h_s[...] = jnp.zeros_like(h_s)

        av, bv = a_ref[...], u_ref[...]
        row = lax.broadcasted_iota(jnp.int32, av.shape, 0)
        for k in _steps(tm):
            a_sh = jnp.where(row >= k, pltpu.roll(av, k, 0), 1.0)
            b_sh = jnp.where(row >= k, pltpu.roll(bv, k, 0), 0.0)
            bv = av * b_sh + bv
            av = av * a_sh
        h = bv + av * h_s[HALO - 1:HALO, :]
        h_ref[...] = h
        y_ref[...] = _gelu_gate(h, gt_ref[...].astype(f32)).astype(y_ref.dtype)
        h_s[...] = h[tm - HALO:, :]

    spec = pl.BlockSpec((tm, ch), lambda i: (i, 0))
    gt_spec = pl.BlockSpec((tm, gate[1]), lambda i: (i, gate[2]))
    return pl.pallas_call(
        body, name="lru_scan_fwd", grid=(n,), in_specs=[spec, spec, gt_spec], out_specs=[spec, spec],
        out_shape=[jax.ShapeDtypeStruct((rows, ch), f32), jax.ShapeDtypeStruct((rows, ch), bf16)],
        scratch_shapes=[pltpu.VMEM((HALO, ch), f32)], compiler_params=_cp(("arbitrary",)),
    )(a, u, gate[0])


def scan_bwd(a, h, gate, dy, tm):
    rows, ch = a.shape
    n = rows // tm
    per = tm // HALO

    def body(a_ref, h_ref, hp_ref, gt_ref, dy_ref, da_ref, du_ref, dg_ref, g_s, a_s):
        i = pl.program_id(0)
        step = n - 1 - i

        @pl.when(i == 0)
        def _():
            g_s[...] = jnp.zeros_like(g_s)
            a_s[...] = jnp.zeros_like(a_s)

        _, vjp = jax.vjp(_gelu_gate, h_ref[...], gt_ref[...].astype(f32))
        dh, dgate = vjp(dy_ref[...].astype(f32))
        dg_ref[...] = dgate.astype(dg_ref.dtype)
        a0 = a_ref[...]
        row = lax.broadcasted_iota(jnp.int32, a0.shape, 0)
        av = jnp.where(row < tm - 1, pltpu.roll(a0, tm - 1, 0), a_s[0:1, :])
        bv = dh
        for k in _steps(tm):
            a_sh = jnp.where(row < tm - k, pltpu.roll(av, tm - k, 0), 1.0)
            b_sh = jnp.where(row < tm - k, pltpu.roll(bv, tm - k, 0), 0.0)
            bv = bv + av * b_sh
            av = av * a_sh
        g = bv + av * g_s[0:1, :]
        h_last = jnp.where(step > 0, hp_ref[HALO - 1:HALO, :], 0.0)
        h_prev = jnp.where(row >= 1, pltpu.roll(h_ref[...], 1, 0), h_last)
        du_ref[...] = g
        da_ref[...] = g * h_prev
        g_s[...] = g[0:HALO, :]
        a_s[...] = a0[0:HALO, :]

    spec = pl.BlockSpec((tm, ch), lambda i: (n - 1 - i, 0))
    hp_spec = pl.BlockSpec((HALO, ch), lambda i: (jnp.maximum((n - 1 - i) * per - 1, 0), 0))
    gt_spec = pl.BlockSpec((tm, gate[1]), lambda i: (n - 1 - i, gate[2]))
    return pl.pallas_call(
        body, name="lru_scan_bwd", grid=(n,), in_specs=[spec, spec, hp_spec, gt_spec, spec], out_specs=[spec, spec, spec],
        out_shape=[jax.ShapeDtypeStruct((rows, ch), f32)] * 2 + [jax.ShapeDtypeStruct((rows, ch), bf16)],
        scratch_shapes=[pltpu.VMEM((HALO, ch), f32)] * 2,
        compiler_params=_cp(("arbitrary",)),
    )(a, h, h, gate[0], dy)


def _rms(x, g):
    return x * lax.rsqrt(jnp.mean(x * x, axis=-1, keepdims=True) + EPS) * g


def f_rms(step, p, c, x):
    return [_rms(x[0], p[0])], []


def _rope_lanes(shape):
    lane = lax.broadcasted_iota(jnp.int32, shape, 1)
    return jnp.logical_and(lane >= KR_LANE, lane < KR_LANE + QK_ROPE)


def _rope_swap(x):
    lane = lax.broadcasted_iota(jnp.int32, x.shape, 1)
    half = QK_ROPE // 2
    sw = jnp.where(lane < KR_LANE + half, pltpu.roll(x, LANE - half, 1), pltpu.roll(x, half, 1))
    return jnp.where(_rope_lanes(x.shape), sw, 0.0)


def _rope(x, cosf, sinf):
    return x * cosf + _rope_swap(x) * sinf


def _heads(x):
    return [x[:, LANE * h:LANE * (h + 1)] for h in range(x.shape[1] // LANE)]


def q_rope_epi(q, cosf, sinf):
    return (jnp.concatenate([_rope(b, cosf, sinf) * ATT_C for b in _heads(q)], axis=1),)


def k_rope_epi(kn, kr, cosf, sinf):
    kr_rot = _rope(jnp.where(_rope_lanes(kr.shape), kr.astype(f32), 0.0), cosf, sinf)
    return (jnp.concatenate([b + kr_rot for b in _heads(kn)], axis=1),)


def rope_bwd(dqr, dkr, cosf, sinf, ddt):
    back = lambda g: g * cosf + _rope_swap(g * sinf)
    dq = jnp.concatenate([back(b) * ATT_C for b in _heads(dqr)], axis=1)
    dkrope = jnp.where(_rope_lanes(ddt.shape), back(sum(_heads(dkr))), 0.0) + ddt
    return dq, dkr, dkrope


def _conv(tail, x, w, b):
    xf = jnp.concatenate([tail, x], axis=0)
    acc = b + w[CONV_W - 1:CONV_W, :] * xf
    for k in range(CONV_W - 1):
        acc = acc + w[k:k + 1, :] * shift_down(xf, CONV_W - 1 - k)
    return acc[HALO:, :]


def f_pool(step, p, c, x):
    wp, sc = p
    (tail,) = c
    (u,) = x
    tm = u.shape[0]
    xf = jnp.concatenate([tail, u], axis=0)
    sums, s, w = [], xf, 1
    while w < POOL_WINDOWS[-1]:
        s = s + shift_down(s, w)
        w *= 2
        sums.append(s)
    t = step * tm + lax.broadcasted_iota(jnp.int32, (tm, 1), 0)
    ys = []
    for g, (w, s) in enumerate(zip(POOL_WINDOWS, sums)):
        sl = slice(LANE * g, LANE * (g + 1))
        cnt = jnp.minimum(t + 1, w).astype(f32)
        d = s[POOL_HALO:, sl] / cnt - u[:, sl]
        ys.append(jnp.dot(d.astype(bf16), wp[LANE * g:LANE * (g + 1), :].astype(bf16), preferred_element_type=f32))
    return [jnp.concatenate(ys, axis=1) * sc], [u[tm - POOL_HALO:, :]]


def f_ssd_tile(step, p, c, x):
    outs = []
    for k in range(x[0].shape[0] // SSD_CHUNK):
        o, c = f_ssd(step, p, c, [t[SSD_CHUNK * k:SSD_CHUNK * (k + 1), :] for t in x])
        outs.append(o[0])
    return [jnp.concatenate(outs, axis=0)], c


def f_ssd(step, p, c, x):
    conv_w, conv_b, dtb, alog, dsk, ng = p
    tail, s_in = c[0], c[1:]
    z, xbc, dt = x
    ln = z.shape[0]
    xc = jax.nn.silu(_conv(tail, xbc, conv_w, conv_b))
    xs, bb, cc = xc[:, :MIX], xc[:, MIX:MIX + LANE], xc[:, MIX + LANE:]
    dtv = jax.nn.softplus(dt + dtb[0:1, :])
    a = dtv * -jnp.exp(alog[0:1, :])
    ri = lax.broadcasted_iota(jnp.int32, (ln, ln), 0)
    ci = lax.broadcasted_iota(jnp.int32, (ln, ln), 1)
    tril = (ri >= ci).astype(f32)
    triu = (ri <= ci).astype(f32)
    hi = lax.Precision.HIGHEST
    a_cs = jnp.dot(tril, a, precision=hi, preferred_element_type=f32)
    a_cs_t = lax.dot_general(a, triu, TN, precision=hi, preferred_element_type=f32)
    a_tot = jnp.sum(a, axis=0, keepdims=True)
    lane = lax.broadcasted_iota(jnp.int32, (1, LANE), 1)
    half = [(lane < 64).astype(f32), (lane >= 64).astype(f32)]
    hrow = lax.broadcasted_iota(jnp.int32, (LANE, 1), 0)

    def head(v, h):
        return jnp.sum(v * (lane == h).astype(f32), axis=1, keepdims=True)

    def pair(v, j):
        return head(v, 2 * j) * half[0] + head(v, 2 * j + 1) * half[1]

    cg = [(cc * half[g]).astype(bf16) for g in range(2)]
    bg = [(bb * half[g]).astype(bf16) for g in range(2)]
    cb = [lax.dot_general(cg[g], bg[g], NT, preferred_element_type=f32) for g in range(2)]
    ys, s_out = [], []
    for j in range(4):
        g = j // 2
        xs_j = xs[:, LANE * j:LANE * (j + 1)]
        xj = xs_j * pair(dtv, j)
        yj = xs_j * pair(dsk[0:1, :], j)
        for hh in range(2):
            h = 2 * j + hh
            rowv = jnp.sum(a_cs_t * (hrow == h).astype(f32), axis=0, keepdims=True)
            lmat = jnp.exp(jnp.where(ri >= ci, head(a_cs, h) - rowv, -jnp.inf))
            yj = yj + jnp.dot((cb[g] * lmat).astype(bf16), (xj * half[hh]).astype(bf16), preferred_element_type=f32)
        acs = pair(a_cs, j)
        tot = pair(a_tot, j)
        yj = yj + jnp.exp(acs) * jnp.dot(cg[g], s_in[j].astype(bf16), preferred_element_type=f32)
        s_new = jnp.exp(tot) * s_in[j] + lax.dot_general(bg[g], (xj * jnp.exp(tot - acs)).astype(bf16), TN,
                                                         preferred_element_type=f32)
        ys.append(yj)
        s_out.append(s_new)
    y = jnp.concatenate(ys, axis=1) * jax.nn.silu(z)
    return [_rms(y, ng)], [xbc[ln - HALO:, :]] + s_out


def _neg_expm1(y):
    series = -y * (1.0 + y * (0.5 + y * (1.0 / 6 + y * (1.0 / 24 + y * (1.0 / 120)))))
    return jnp.where(y > -0.05, series, 1.0 - jnp.exp(y))


def f_lru_pre(step, p, c, x):
    cw, cb_, wa, ba, wi, bi, lam = p
    (tail,) = c
    (lx,) = x
    tm = lx.shape[0]
    xc = _conv(tail, lx, cw, cb_)
    xb = xc.astype(bf16)
    r = jax.nn.sigmoid(jnp.dot(xb, wa.astype(bf16), preferred_element_type=f32) + ba)
    it = jax.nn.sigmoid(jnp.dot(xb, wi.astype(bf16), preferred_element_type=f32) + bi)
    log_a = -LRU_C * r * jax.nn.softplus(-lam)
    mult = jnp.sqrt(_neg_expm1(2.0 * log_a))
    return [jnp.exp(log_a), xc * it * mult], [lx[tm - HALO:, :]]


def loss_head(x, tgt, g, tm):
    rows, d = x.shape
    n = rows // tm

    def body(x_ref, t_ref, g_ref, loss_ref, dx_ref, dg_ref):
        @pl.when(pl.program_id(0) == 0)
        def _():
            loss_ref[...] = jnp.zeros_like(loss_ref)
            dg_ref[...] = jnp.zeros_like(dg_ref)

        def fn(gv, xv):
            err = _rms(xv, gv) - t_ref[...]
            return 0.5 * jnp.sum(jnp.mean(err * err, axis=-1, keepdims=True))

        val, (dg, dx) = jax.value_and_grad(fn, argnums=(0, 1))(g_ref[...], x_ref[...])
        loss_ref[...] += val
        dg_ref[...] += dg
        dx_ref[...] = dx

    spec = pl.BlockSpec((tm, d), lambda i: (i, 0))
    return pl.pallas_call(
        body, name="loss_head", grid=(n,), in_specs=[spec, spec, _const_spec((1, d))],
        out_specs=[_const_spec((8, LANE)), spec, _const_spec((1, d))],
        out_shape=[jax.ShapeDtypeStruct((8, LANE), f32), jax.ShapeDtypeStruct((rows, d), f32),
                   jax.ShapeDtypeStruct((1, d), f32)],
        compiler_params=_cp(("arbitrary",)),
    )(x, tgt, g)


def ew(name, fn, ins, outs, tm):
    rows = ins[0][0].shape[0]
    ni = len(ins)

    def body(*refs):
        res = fn(*[r[...].astype(f32) for r in refs[:ni]])
        for r, v in zip(refs[ni:], res):
            r[...] = v.astype(r.dtype)

    return pl.pallas_call(
        body, name=name, grid=(rows // tm,), in_specs=[_tile_spec(tm, w, cb) for (_, w, cb) in ins],
        out_specs=[_tile_spec(tm, w, 0) for (w, _) in outs],
        out_shape=[jax.ShapeDtypeStruct((rows, w), dt) for (w, dt) in outs],
        compiler_params=_cp(("parallel",)),
    )(*[t[0] for t in ins])


def _peers():
    x, y, c = lax.axis_index("x"), lax.axis_index("y"), lax.axis_index("c")
    me = 4 * x + 2 * y + c
    out = []
    for k in range(1, N_DEV):
        px = 1 - x if k & 4 else x
        py = 1 - y if k & 2 else y
        pc = 1 - c if k & 1 else c
        out.append(((px, py, pc), 4 * px + 2 * py + pc))
    return me, out


_HBM = pl.BlockSpec(memory_space=pltpu.HBM)
_SEM = pl.BlockSpec(memory_space=pltpu.SEMAPHORE)
_EFFECT = pltpu.SideEffectType.DATAFLOW_SIDE_EFFECTING


def _remote(src_ref, land_ref, gather, me, pid, dev, send_sems, recv_sems, k, recv_side):
    return pltpu.make_async_remote_copy(
        src_ref=src_ref if gather else src_ref.at[pid], dst_ref=land_ref.at[pid if recv_side else me],
        send_sem=send_sems.at[k], recv_sem=recv_sems.at[k], device_id=dev, device_id_type=pl.DeviceIdType.MESH)


def _own(src_ref, land_ref, gather, me, sem):
    return pltpu.make_async_copy(src_ref if gather else src_ref.at[me], land_ref.at[me], sem)


def exchange_start(name, srcs, gather, deps=()):
    n, nd = len(srcs), len(deps)
    shapes = [(s.shape if gather else s.shape[1:]) for s in srcs]
    lands = [lax.empty((N_DEV,) + tuple(sh), s.dtype) for s, sh in zip(srcs, shapes)]

    def body(*refs):
        src_refs, land_refs = refs[:n], refs[n:2 * n]
        send_sems, recv_sems, own_sem = refs[2 * n + nd:2 * n + nd + 3]
        token = refs[-1]
        me, peers = _peers()
        for k, (dev, pid) in enumerate(peers):
            for s_ref, l_ref in zip(src_refs, land_refs):
                _remote(s_ref, l_ref, gather, me, pid, dev, send_sems, recv_sems, k, False).start()
        for s_ref, l_ref in zip(src_refs, land_refs):
            _own(s_ref, l_ref, gather, me, own_sem).start()
        token[...] = jnp.zeros_like(token)

    hbm = lambda a: pltpu.with_memory_space_constraint(a, pltpu.HBM)
    res = pl.pallas_call(
        body, name=name,
        out_shape=(pltpu.SemaphoreType.DMA((N_DEV - 1,)), pltpu.SemaphoreType.DMA((N_DEV - 1,)), pltpu.SemaphoreType.DMA(()),
                   *[pltpu.HBM(a.shape, a.dtype) for a in list(srcs) + lands], jax.ShapeDtypeStruct((8, LANE), f32)),
        in_specs=[_HBM] * (2 * n) + [pl.BlockSpec(memory_space=pl.ANY)] * nd,
        out_specs=(_SEM, _SEM, _SEM, *([_HBM] * (2 * n)), pl.BlockSpec(memory_space=pltpu.VMEM)),
        input_output_aliases={i: 3 + i for i in range(2 * n)},
        compiler_params=pltpu.CompilerParams(has_side_effects=_EFFECT),
    )(*[hbm(a) for a in list(srcs) + lands], *deps)
    return dict(sems=res[:3], srcs=list(res[3:3 + n]), lands=list(res[3 + n:3 + 2 * n]), token=res[-1], gather=gather)


def exchange_wait(name, h, afters):
    n, gather = len(h["srcs"]), h["gather"]

    def body(*refs):
        src_refs, land_refs = refs[:n], refs[n:2 * n]
        send_sems, recv_sems, own_sem = refs[2 * n:2 * n + 3]
        me, peers = _peers()
        for k, (dev, pid) in enumerate(peers):
            for s_ref, l_ref in zip(src_refs, land_refs):
                _remote(s_ref, l_ref, gather, me, pid, dev, send_sems, recv_sems, k, True).wait_recv()
        for k, (dev, pid) in enumerate(peers):
            for s_ref, l_ref in zip(src_refs, land_refs):
                _remote(s_ref, l_ref, gather, me, pid, dev, send_sems, recv_sems, k, False).wait_send()
        for s_ref, l_ref in zip(src_refs, land_refs):
            _own(s_ref, l_ref, gather, me, own_sem).wait()

    arrs = h["srcs"] + h["lands"]
    res = pl.pallas_call(
        body, name=name, out_shape=tuple(pltpu.HBM(a.shape, a.dtype) for a in arrs),
        in_specs=[_HBM] * (2 * n) + [_SEM, _SEM, _SEM] + [pl.BlockSpec(memory_space=pl.ANY)] * len(afters),
        out_specs=tuple([_HBM] * (2 * n)), input_output_aliases={i: i for i in range(2 * n)},
        compiler_params=pltpu.CompilerParams(has_side_effects=_EFFECT),
    )(*arrs, *h["sems"], *afters)
    return list(res[n:])


def _adam_update(g, w, m, v):
    mn = ADAM_B1 * m + (1.0 - ADAM_B1) * g
    vn = ADAM_B2 * v + (1.0 - ADAM_B2) * jnp.square(g)
    m_hat = mn / (1.0 - ADAM_B1 ** ADAM_STEP)
    v_hat = vn / (1.0 - ADAM_B2 ** ADAM_STEP)
    return -ADAM_LR * (m_hat / (jnp.sqrt(v_hat) + ADAM_EPS) + ADAM_WD * w), mn, vn


def _adamw_vectors(name, parts, w, m, v):
    nl = len(parts)

    def body(*refs):
        p_refs = refs[:nl]
        w_ref, m_ref, v_ref, g_ref, d_ref, nm_ref, nv_ref = refs[nl:]
        for ll, p_ref in enumerate(p_refs):
            row = slice(ll, ll + 1)
            g = p_ref[0:1, :]
            for i in range(1, N_DEV):
                g = g + p_ref[i:i + 1, :]
            delta, mn, vn = _adam_update(g, w_ref[row, :], m_ref[row, :], v_ref[row, :])
            g_ref[row, :] = g
            d_ref[row, :] = delta
            nm_ref[row, :] = mn
            nv_ref[row, :] = vn

    return list(pl.pallas_call(body, name=name, out_shape=[jax.ShapeDtypeStruct(w.shape, f32)] * 4)(*parts, w, m, v))


def adamw_packed(name, gots, where, ws, ms, vs):
    ng, npar = len(gots), len(ws)

    def body(*refs):
        g_refs = refs[:ng]
        w_refs, m_refs, v_refs = (refs[ng + k * npar:ng + (k + 1) * npar] for k in range(3))
        o_refs = refs[ng + 3 * npar:]
        for p in range(npar):
            width = w_refs[p].shape[1]
            for l, (which, off) in enumerate(where[p]):
                row = slice(l, l + 1)
                cols = slice(off, off + width)
                g = g_refs[which][0:1, cols]
                for i in range(1, N_DEV):
                    g = g + g_refs[which][i:i + 1, cols]
                delta, mn, vn = _adam_update(g, w_refs[p][row, :], m_refs[p][row, :], v_refs[p][row, :])
                for k, val in enumerate((g, delta, mn, vn)):
                    o_refs[4 * p + k][row, :] = val

    out_shape = [jax.ShapeDtypeStruct(w.shape, f32) for w in ws for _ in range(4)]
    res = pl.pallas_call(body, name=name, out_shape=out_shape, compiler_params=_cp(()))(*gots, *ws, *ms, *vs)
    return [list(res[4 * p:4 * p + 4]) for p in range(npar)]


def adamw_columns(name, parts, w, m, v):
    nl, kk, cc = w.shape
    view = lambda a: jnp.transpose(a, (2, 0, 1))
    tc = min(LANE, cc)

    def body(*refs):
        p_refs = refs[:nl]
        w_ref, m_ref, v_ref, g_ref, d_ref, nm_ref, nv_ref = refs[nl:]
        for l, p_ref in enumerate(p_refs):
            g = p_ref[0].astype(f32)
            for i in range(1, N_DEV):
                g = g + p_ref[i].astype(f32)
            g = g.T
            delta, mn, vn = _adam_update(g, w_ref[:, l, :], m_ref[:, l, :], v_ref[:, l, :])
            g_ref[:, l, :] = g
            d_ref[:, l, :] = delta
            nm_ref[:, l, :] = mn
            nv_ref[:, l, :] = vn

    p_spec = pl.BlockSpec((N_DEV, kk, tc), lambda j: (0, 0, j))
    w_spec = pl.BlockSpec((tc, nl, kk), lambda j: (j, 0, 0))
    res = pl.pallas_call(
        body, name=name, grid=(pl.cdiv(cc, tc),), in_specs=[p_spec] * nl + [w_spec] * 3, out_specs=[w_spec] * 4,
        out_shape=[jax.ShapeDtypeStruct((cc, nl, kk), f32)] * 4, compiler_params=_cp(("parallel",)),
    )(*parts, view(w), view(m), view(v))
    return [jnp.transpose(a, (1, 2, 0)) for a in res]


def adamw(name, parts, w, m, v):
    nl = len(parts)
    shape = w.shape[1:]
    c = shape[-1]
    r = 1
    for s in shape[:-1]:
        r *= s
    if r == 1:
        return _adamw_vectors(name, parts, w, m, v)
    tr = _pick(r, 256) if r % 8 == 0 else r
    nb = r // tr
    parts2 = [p.reshape(N_DEV, r, c) for p in parts]
    w2, m2, v2 = (a.reshape(nl, r, c) for a in (w, m, v))

    def body(*refs):
        p_refs = refs[:nl]
        w_ref, m_ref, v_ref, g_ref, d_ref, nm_ref, nv_ref = refs[nl:]
        layer = pl.program_id(0)
        for ll, p_ref in enumerate(p_refs):
            @pl.when(layer == ll)
            def _(p_ref=p_ref):
                g = p_ref[0].astype(f32)
                for i in range(1, N_DEV):
                    g = g + p_ref[i].astype(f32)
                delta, mn, vn = _adam_update(g, w_ref[0], m_ref[0], v_ref[0])
                g_ref[0] = g
                d_ref[0] = delta
                nm_ref[0] = mn
                nv_ref[0] = vn

    def p_spec(ll):
        return pl.BlockSpec((N_DEV, tr, c), lambda l, i: (0, jnp.where(l == ll, i, jnp.where(l > ll, nb - 1, 0)), 0))

    spec = pl.BlockSpec((1, tr, c), lambda l, i: (l, i, 0))
    res = pl.pallas_call(
        body, name=name, grid=(nl, nb), in_specs=[p_spec(ll) for ll in range(nl)] + [spec, spec, spec],
        out_specs=[spec] * 4, out_shape=[jax.ShapeDtypeStruct((nl, r, c), f32)] * 4,
        compiler_params=_cp(("arbitrary", "arbitrary")),
    )(*parts2, w2, m2, v2)
    return [a.reshape(w.shape) for a in res]


_IN_SPLITS = dict(cq=(0, 384), ckv=(384, 640), kr=(640, 672), pool=(672, 1184), z=(1184, 1696), xbc=(1696, 2464),
                  dt=(2464, 2472), lg=(2472, 2984), lx=(2984, 3496), gates=(3496, 7592))


W_IN_SHARD = IN_COLS // N_DEV

_PAD_ORDER = ("gates", "pool", "z", "lg", "lx", "xbc", "cq", "dt", KR_LANE - DT_LANES, "kr", LANE - KR_LANE - QK_ROPE, "ckv")
_SEGMENTS = ((0, U_CQ[0], 384), (384, U_CKV[0], 256), (640, U_KR[0] + KR_LANE, QK_ROPE), (672, U_POOL[0], 512),
             (1184, U_Z[0], 512), (1696, U_XBC[0], 768), (2464, U_KR[0], DT_LANES), (2472, U_LG[0], 512), (2984, U_LX[0], 512),
             (3496, 0, 4096))


def _pad_w_in(shards):
    rows = shards.shape[1]
    pieces = []
    for item in _PAD_ORDER:
        if isinstance(item, int):
            pieces.append(jnp.zeros((rows, item), shards.dtype))
            continue
        a, b = _IN_SPLITS[item]
        for d in range(a // W_IN_SHARD, (b - 1) // W_IN_SHARD + 1):
            lo, hi = max(a, d * W_IN_SHARD), min(b, (d + 1) * W_IN_SHARD)
            pieces.append(shards[d, :, lo - d * W_IN_SHARD:hi - d * W_IN_SHARD])
    return jnp.concatenate(pieces, axis=1)


def _w_in_blocks(g):
    blocks = []
    for d in range(N_DEV):
        a, b = d * W_IN_SHARD, (d + 1) * W_IN_SHARD
        pieces = []
        for ref, pad, width in _SEGMENTS:
            lo, hi = max(a, ref), min(b, ref + width)
            if lo < hi:
                pieces.append(g[:, pad + lo - ref:pad + hi - ref])
        blocks.append(jnp.concatenate(pieces, axis=1))
    return jnp.stack(blocks).astype(bf16)


def _head_pad_cols(w, per, lo, hi):
    k = w.shape[0]
    w = w.reshape(k, N_HEADS, per)[:, :, lo:hi]
    return jnp.pad(w, ((0, 0), (0, 0), (0, LANE - (hi - lo)))).reshape(k, N_HEADS * LANE)


def _head_unpad_cols(g, n):
    k = g.shape[0]
    return g.reshape(k, N_HEADS, LANE)[:, :, :n]


def _on_diagonal():
    i = lax.broadcasted_iota(jnp.int32, (8, 1, 8, 1), 0)
    j = lax.broadcasted_iota(jnp.int32, (8, 1, 8, 1), 2)
    return i == j


def _block_diag(w):
    w4 = jnp.broadcast_to(w[:, :, None, :], (8, 64, 8, 64))
    return jnp.where(_on_diagonal(), w4, 0.0).reshape(MIX, MIX)


def _block_diag_inv(g):
    return jnp.sum(jnp.where(_on_diagonal(), g.reshape(8, 64, 8, 64), 0.0), axis=2)


def _head8(v):
    return jnp.pad(v[None, :], ((0, 7), (0, LANE - v.shape[0])))


GROUPS = dict(A=("w_in",), B=("w_uq", "w_ukv", "ssd_conv_w", "lru_conv_w", "w_branch", "w_out"),
              C=("w_ff1", "w_ff2", "w_ple_gate", "w_ple"))


def _kernel_weights(grp, fw):
    if grp == "A":
        w_in = _pad_w_in(fw["w_in"])
        return dict(w_in=w_in, w_dt=w_in[:, U_KR[0]:U_KR[0] + LANE])
    if grp == "C":
        return dict(w_ff1=fw["w_ff1"], w_ff2=fw["w_ff2"], w_pg=fw["w_ple_gate"], w_ple=fw["w_ple"])
    wb = fw["w_branch"]
    wb0 = jnp.pad(wb[0].reshape(N_HEADS, V_HEAD, D_MODEL), ((0, 0), (0, LANE - V_HEAD), (0, 0))).reshape(N_HEADS * LANE, D_MODEL)
    return dict(
        w_uq=_head_pad_cols(fw["w_uq"], QK_NOPE + QK_ROPE, 0, QK_NOPE + QK_ROPE),
        w_uk=_head_pad_cols(fw["w_ukv"], QK_NOPE + V_HEAD, 0, QK_NOPE),
        w_uv=_head_pad_cols(fw["w_ukv"], QK_NOPE + V_HEAD, QK_NOPE, QK_NOPE + V_HEAD),
        wb=[wb0, wb[1], wb[2], wb[3]], w_out=fw["w_out"], ssd_conv_w=fw["ssd_conv_w"], lru_conv_w=fw["lru_conv_w"])


def _layer_params(sp, l):
    row = lambda n: sp[n][l][None, :]
    return dict(
        g_mix=row("g_mix"), q_norm=row("q_norm"), kv_norm=row("kv_norm"),
        pool=[sp["w_pool"][l].reshape(4 * LANE, LANE), row("pool_scale")],
        ssd=[None, row("ssd_conv_b"), _head8(sp["ssd_dt_bias"][l]), _head8(sp["ssd_a_log"][l]),
             _head8(sp["ssd_d"][l]), row("ssd_norm")],
        lru=[None, row("lru_conv_b"), _block_diag(sp["lru_w_a"][l]), row("lru_b_a"),
             _block_diag(sp["lru_w_i"][l]), row("lru_b_i"), row("lru_lambda")],
        g_mlp=row("g_mlp"), g_ple=row("g_ple"),
    )


_sig = jax.nn.sigmoid
_SSD_CARRY = [(HALO, SSD_XBC)] + [(LANE, LANE)] * 4


def _tiles(rows):
    return dict(tm=_pick(rows, 1024), ta=_pick(rows, 512), tp=_pick(rows, 512), tl=_pick(rows, 512), ts=_pick(rows, 256),
                tssd=_pick(rows, SSD_CHUNK * SSD_CHUNKS_PER_TILE))


def _mixer_tiles(u, dt32):
    return dict(
        cq=(u, 384, U_CQ[0] // 384), ckv=(u, 256, U_CKV[0] // 256), kr=(u, LANE, U_KR[0] // LANE),
        pool=(u, MIX, U_POOL[0] // MIX), z=(u, MIX, U_Z[0] // MIX), xbc=(u, SSD_XBC, U_XBC[0] // SSD_XBC),
        dt=(dt32, LANE, 0), lg=(u, MIX, U_LG[0] // MIX), lx=(u, MIX, U_LX[0] // MIX))


def _add_norm(acc, resid, g):
    x = acc + resid
    return x, _rms(x, g)


def _layer_fwd(x, h, p_bf, ctx, l, pr, g_next, cosf, sinf, early=()):
    rows = x.shape[0]
    ts = _tiles(rows)
    tm = ts["tm"]
    nm = lambda s: f"{s}_l{l}"
    r = dict(x=x)
    if h is None:
        (h,), _ = seq_fwd(nm("rms_in"), f_rms, [pr["g_mix"]], [(x, D_MODEL, 0)], [], [(D_MODEL, bf16)], tm)
    w = dict(_kernel_weights("A", ctx.weights(l, "A", [h, *early])))
    u = matmul(nm("w_in"), h, w["w_in"], outs=(U_DTYPE,))
    dt32 = matmul(nm("w_dt"), h, w["w_dt"])
    mt = _mixer_tiles(u, dt32)
    (cqn,), _ = seq_fwd(nm("rms_q"), f_rms, [pr["q_norm"]], [mt["cq"]], [], [(Q_LORA, bf16)], tm)
    (ckvn,), _ = seq_fwd(nm("rms_kv"), f_rms, [pr["kv_norm"]], [mt["ckv"]], [], [(KV_LORA, bf16)], tm)
    (yb,), pool_saved = seq_fwd(nm("pool"), f_pool, pr["pool"], [mt["pool"]], [(POOL_HALO, MIX)], [(MIX, bf16)], ts["tp"])
    w.update(_kernel_weights("B", ctx.weights(l, "B", yb)))
    pr = dict(pr, ssd=[w["ssd_conv_w"]] + pr["ssd"][1:], lru=[w["lru_conv_w"]] + pr["lru"][1:])
    tables = [(cosf, 0, LANE), (sinf, 0, LANE)]
    qr = matmul(nm("w_uq"), cqn, w["w_uq"], outs=(bf16,), epi=q_rope_epi, extras=tables)
    kr = matmul(nm("w_uk"), ckvn, w["w_uk"], outs=(bf16,), epi=k_rope_epi, extras=[(u, U_KR[0], LANE)] + tables)
    vb = matmul(nm("w_uv"), ckvn, w["w_uv"], outs=(bf16,))
    o, lse = attn_fwd(qr, kr, vb, ts["ta"])
    (yc,), ssd_saved = seq_fwd(nm("ssd"), f_ssd_tile, pr["ssd"], [mt["z"], mt["xbc"], mt["dt"]], _SSD_CARRY, [(MIX, bf16)],
                               ts["tssd"])
    (la, lu), lru_saved = seq_fwd(nm("lru_pre"), f_lru_pre, pr["lru"], [mt["lx"]], [(HALO, MIX)], [(MIX, f32), (MIX, f32)], ts["tl"])
    hh, yd = scan_fwd(la, lu, mt["lg"], ts["ts"])
    ys = [o, yb, yc, yd]
    m, pres = merge_fwd(nm("merge"), ys, w["wb"], u)
    x1, h2 = matmul(nm("w_out"), m, w["w_out"], outs=(f32, bf16), epi=_add_norm, extras=[(x, 0)], rows=[pr["g_mlp"]])
    w.update(_kernel_weights("C", ctx.weights(l, "C", h2)))
    a1, act = matmul(nm("ff1"), h2, w["w_ff1"], outs=(bf16, bf16), epi=lambda acc: (acc, jnp.square(jnp.maximum(acc, 0.0))))
    x2, h3 = matmul(nm("ff2"), act, w["w_ff2"], outs=(f32, bf16), epi=_add_norm, extras=[(x1, 0)], rows=[pr["g_ple"]])
    gl = matmul(nm("ple_gate"), h3, w["w_pg"])
    if g_next is None:
        x3, pe = matmul(nm("ple"), p_bf, w["w_ple"], outs=(f32, bf16), epi=lambda acc, g, xr: (xr + acc * _sig(g), acc),
                        extras=[(gl, 0), (x2, 0)])
        h_next = None
    else:
        def ple_norm(acc, g, xr, gn):
            xo = xr + acc * _sig(g)
            return xo, acc, _rms(xo, gn)

        x3, pe, h_next = matmul(nm("ple"), p_bf, w["w_ple"], outs=(f32, bf16, bf16), epi=ple_norm,
                                extras=[(gl, 0), (x2, 0)], rows=[g_next])
    r.update(h=h, u=u, cqn=cqn, ckvn=ckvn, vb=vb, qr=qr, kr=kr, o=o, lse=lse, ys=ys, pres=pres, m=m, x1=x1,
             h2=h2, a1=a1, act=act, x2=x2, h3=h3, gl=gl, pe=pe, p_bf=p_bf, pool_saved=pool_saved, ssd_saved=ssd_saved,
             lru_saved=lru_saved, la=la, hh=hh, w=w, pr=pr, dt32=dt32)
    return x3, h_next, r


def _norm_bwd(dh, x, resid, g):
    rs = lax.rsqrt(jnp.mean(x * x, axis=-1, keepdims=True) + EPS)
    xhat = x * rs
    dxn = dh * g
    dx = rs * (dxn - xhat * jnp.mean(dxn * xhat, axis=-1, keepdims=True)) + resid
    return dx, jnp.sum(dh * xhat, axis=0, keepdims=True)


def _gate_bwd(d, g, pre):
    s = _sig(g.astype(f32))
    return d * s, d * pre.astype(f32) * s * (1.0 - s)


def _layer_bwd(dx3, r, ctx, l, cosf, sinf, tok, extra_small):
    rows = dx3.shape[0]
    ts = _tiles(rows)
    tm = ts["tm"]
    nm = lambda s: f"{s}_l{l}"
    u, w, pr = r["u"], r["w"], r["pr"]
    mt = _mixer_tiles(u, r["dt32"])
    g = {}
    full = lambda a: (a, a.shape[1], 0)
    dpe, dgl = ew(nm("ple_bwd"), _gate_bwd, [full(dx3), full(r["gl"]), full(r["pe"])], [(D_MODEL, bf16)] * 2, tm)
    g["w_ple"] = matmul(nm("d_w_ple"), r["p_bf"], dpe, ta=True, outs=(bf16,), deps=[tok] if tok is not None else [])
    g["w_pg"] = matmul(nm("d_w_pg"), r["h3"], dgl, ta=True, outs=(bf16,))
    dx2, g["g_ple"] = matmul(nm("d_h3"), dgl, w["w_pg"], tb=True, epi=_norm_bwd, extras=[(r["x2"], 0), (dx3, 0)],
                             rows=[pr["g_ple"]], row_sums=1)
    da1 = matmul(nm("d_act"), dx2, w["w_ff2"], tb=True, outs=(bf16,),
                 epi=lambda acc, a: (acc * 2.0 * jnp.maximum(a, 0.0),), extras=[(r["a1"], 0)])
    g["w_ff2"] = matmul(nm("d_w_ff2"), r["act"], dx2, ta=True, outs=(bf16,))
    g["w_ff1"] = matmul(nm("d_w_ff1"), r["h2"], da1, ta=True, outs=(bf16,), out_blocks=N_DEV)
    tok = ctx.grads(l, "C", dict(w_ff1=g["w_ff1"], w_ff2=g["w_ff2"], w_ple_gate=g["w_pg"], w_ple=g["w_ple"]))
    dx1, g["g_mlp"] = matmul(nm("d_h2"), da1, w["w_ff1"], tb=True, epi=_norm_bwd, extras=[(r["x1"], 0), (dx2, 0)],
                             rows=[pr["g_mlp"]], row_sums=1, deps=[tok])
    def merge_bwd(dm, *gates_and_pres):
        both = [_gate_bwd(dm, gates_and_pres[n], gates_and_pres[4 + n]) for n in range(4)]
        return tuple(b[0] for b in both) + tuple(b[1] for b in both)

    res = matmul(nm("d_merged"), dx1, w["w_out"], tb=True, outs=(bf16,) * 8, epi=merge_bwd,
                 extras=[(u, D_MODEL * n) for n in range(4)] + [(pre, 0) for pre in r["pres"]])
    dpres, dgates = list(res[:4]), list(res[4:])
    g["w_out"] = matmul(nm("d_w_out"), r["m"], dx1, ta=True, outs=(bf16,))
    dys, g["wb"] = [], []
    for n in range(4):
        g["wb"].append(matmul(nm(f"d_w_branch{n}"), r["ys"][n], dpres[n], ta=True, outs=(bf16,)))
        dys.append(matmul(nm(f"d_y{n}"), dpres[n], w["wb"][n], tb=True, outs=(bf16 if n == 0 else f32,)))
    g["ssd"], (dz, dxbc, ddt) = seq_bwd(nm("ssd_bwd"), f_ssd_tile, pr["ssd"], [mt["z"], mt["xbc"], mt["dt"]], [True] * 3,
                                        r["ssd_saved"], [dys[2]], [bf16] * 3, ts["tssd"])
    dqr, dkr_, dv = attn_bwd(r["qr"], r["kr"], r["vb"], dys[0], r["o"], r["lse"], ts["ta"])
    hw = N_HEADS * LANE
    dq, dkn, dkrope = ew(nm("rope_bwd"), rope_bwd, [full(dqr), full(dkr_), full(cosf), full(sinf), full(ddt)],
                         [(hw, bf16), (hw, bf16), (LANE, bf16)], tm)
    g["w_uq"] = matmul(nm("d_w_uq"), r["cqn"], dq, ta=True, outs=(bf16,))
    g["w_uk"] = matmul(nm("d_w_uk"), r["ckvn"], dkn, ta=True, outs=(bf16,))
    g["w_uv"] = matmul(nm("d_w_uv"), r["ckvn"], dv, ta=True, outs=(bf16,))
    dcqn = matmul(nm("d_cqn"), dq, w["w_uq"], tb=True)
    dckvn = matmul(nm("d_ckvn_k"), dkn, w["w_uk"], tb=True)
    dckvn = matmul(nm("d_ckvn_v"), dv, w["w_uv"], tb=True, epi=lambda acc, prev: (acc + prev,), extras=[(dckvn, 0)])
    (g["q_norm"],), (dcq,) = seq_bwd(nm("rms_q_bwd"), f_rms, [pr["q_norm"]], [mt["cq"]], [True], [], [dcqn], [bf16], tm)
    (g["kv_norm"],), (dckv,) = seq_bwd(nm("rms_kv_bwd"), f_rms, [pr["kv_norm"]], [mt["ckv"]], [True], [], [dckvn], [bf16], tm)
    g["pool"], (dpool,) = seq_bwd(nm("pool_bwd"), f_pool, pr["pool"], [mt["pool"]], [True], r["pool_saved"], [dys[1]],
                                  [bf16], ts["tp"])
    da, du, dlg = scan_bwd(r["la"], r["hh"], mt["lg"], dys[3], ts["ts"])
    g["lru"], (dlx,) = seq_bwd(nm("lru_pre_bwd"), f_lru_pre, pr["lru"], [mt["lx"]], [True], r["lru_saved"], [da, du],
                               [bf16], ts["tl"])
    dk = _head_unpad_cols(g["w_uk"], QK_NOPE)
    dv_ = _head_unpad_cols(g["w_uv"], V_HEAD)
    wb0 = g["wb"][0].reshape(N_HEADS, LANE, D_MODEL)[:, :V_HEAD].reshape(MIX, D_MODEL)
    ssd, lru, pool = g["ssd"], g["lru"], g["pool"]
    tok = ctx.grads(l, "B", dict(
        w_uq=_head_unpad_cols(g["w_uq"], QK_NOPE + QK_ROPE).reshape(Q_LORA, -1),
        w_ukv=jnp.concatenate([dk, dv_], axis=2).reshape(KV_LORA, -1), ssd_conv_w=ssd[0], lru_conv_w=lru[0],
        w_branch=jnp.stack([wb0, g["wb"][1], g["wb"][2], g["wb"][3]]), w_out=g["w_out"]))
    du_p = jnp.concatenate(dgates + [dpool, dz, dlg, dlx, dxbc, dcq, dkrope, dckv], axis=1)
    small = dict(
        q_norm=g["q_norm"][0], kv_norm=g["kv_norm"][0],
        w_pool=pool[0].reshape(4, LANE, LANE), pool_scale=pool[1][0],
        ssd_conv_b=ssd[1][0], ssd_dt_bias=ssd[2][0, :8], ssd_a_log=ssd[3][0, :8], ssd_d=ssd[4][0, :8], ssd_norm=ssd[5][0],
        lru_conv_b=lru[1][0], lru_w_a=_block_diag_inv(lru[2]), lru_b_a=lru[3][0], lru_w_i=_block_diag_inv(lru[4]),
        lru_b_i=lru[5][0], lru_lambda=lru[6][0], g_mlp=g["g_mlp"][0], g_ple=g["g_ple"][0])
    tok_small = ctx.small(f"l{l}", [(n, l, small[n]) for n in SMALL if n in small] + extra_small)
    g_w_in = matmul(nm("d_w_in"), r["h"], du_p, ta=True, outs=(bf16,), deps=[tok, tok_small])
    tok = ctx.grads(l, "A", dict(w_in=_w_in_blocks(g_w_in)))
    dx, g_mix = matmul(nm("d_h"), du_p, w["w_in"], tb=True, epi=_norm_bwd, extras=[(r["x"], 0), (dx1, 0)],
                       rows=[pr["g_mix"]], row_sums=1, deps=[tok])
    return dx, tok, ("g_mix", l, g_mix[0])


def _rope_tables(positions):
    inv = 1.0 / (ROPE_THETA ** (jnp.arange(0, QK_ROPE, 2, dtype=f32) / QK_ROPE))
    ang = positions.astype(f32)[:, None] * inv
    cos, sin = jnp.cos(ang), jnp.sin(ang)
    rows = positions.shape[0]
    pad = jnp.zeros((rows, LANE - KR_LANE - QK_ROPE), f32)
    cosf = jnp.concatenate([jnp.ones((rows, KR_LANE), f32), cos, cos, pad], axis=1)
    sinf = jnp.concatenate([jnp.zeros((rows, KR_LANE), f32), -sin, sin, pad], axis=1)
    return cosf, sinf


WEIGHTS = ['g_mix', 'w_in', 'q_norm', 'w_uq', 'kv_norm', 'w_ukv', 'w_pool', 'pool_scale', 'ssd_conv_w', 'ssd_conv_b',
           'ssd_dt_bias', 'ssd_a_log', 'ssd_d', 'ssd_norm', 'lru_conv_w', 'lru_conv_b', 'lru_w_a', 'lru_b_a', 'lru_w_i',
           'lru_b_i', 'lru_lambda', 'w_branch', 'w_out', 'g_mlp', 'w_ff1', 'w_ff2', 'g_ple', 'w_ple_gate', 'w_ple', 'g_final']
SHARDED = dict(w_in=2, w_uq=2, w_ukv=2, ssd_conv_w=2, lru_conv_w=2, w_branch=3, w_out=1, w_ff1=2, w_ff2=1,
               w_ple_gate=1, w_ple=2)
F32_PAYLOAD = ("ssd_conv_w", "lru_conv_w")
DEPTH = 2


SMALL = [n for n in WEIGHTS if n not in SHARDED and n != "g_final"]


def local_step(x, p, positions, tgt, sp, ctx):
    cosf, sinf = _rope_tables(positions)
    prs = [_layer_params(sp, l) for l in range(DEPTH)]
    p_bf = [p[l].astype(bf16) for l in range(DEPTH)]
    early = [cosf, sinf, *p_bf] + [a for pr in prs for v in pr.values() for a in (v if isinstance(v, list) else [v])
                                   if a is not None]
    res, h = [], None
    for l in range(DEPTH):
        g_next = prs[l + 1]["g_mix"] if l + 1 < DEPTH else None
        x, h, r = _layer_fwd(x, h, p_bf[l], ctx, l, prs[l], g_next, cosf, sinf, early if l == 0 else ())
        res.append(r)
    loss8, dx, dgf = loss_head(x, tgt, sp["g_final"][None, :], _tiles(x.shape[0])["tm"])
    tok = None
    pending = ("g_final", None, dgf[0])
    for l in reversed(range(DEPTH)):
        dx, tok, pending = _layer_bwd(dx, res[l], ctx, l, cosf, sinf, tok, [pending])
    ctx.small("last", [pending])
    return loss8[0, 0], dx


def _payload(name, w):
    return w if name in F32_PAYLOAD else w.astype(bf16)


def _blocks(name, g):
    ax = SHARDED[name] - 1
    shape = list(g.shape)
    shape[ax:ax + 1] = [N_DEV, shape[ax] // N_DEV]
    return _payload(name, jnp.moveaxis(g.reshape(shape), ax, 0))


def _assemble(name, shards):
    ax = SHARDED[name] - 1
    shape = list(shards.shape[1:])
    shape[ax] *= N_DEV
    return jnp.moveaxis(shards, 0, ax).reshape(shape)


class _Exchanges:
    def __init__(self, wts):
        self.wts = wts
        self.ag, self.rs, self.sm = {}, {}, {}
        tok = None
        for l in range(DEPTH):
            for grp, names in GROUPS.items():
                h = exchange_start(f"ag_start_{grp}{l}", [_payload(n, wts[n][l]) for n in names], True,
                                   deps=[] if tok is None else [tok])
                tok = h["token"]
                self.ag[(l, grp)] = h
        self.all_started = tok

    def weights(self, l, grp, after):
        afters = list(after) if isinstance(after, (list, tuple)) else [after]
        if (l, grp) == (0, "A"):
            afters.append(self.all_started)
        got = exchange_wait(f"ag_wait_{grp}{l}", self.ag[(l, grp)], afters)
        out = {}
        for n, a in zip(GROUPS[grp], got):
            out[n] = a if n == "w_in" else _assemble(n, a)
        return out

    def grads(self, l, grp, g):
        cut = lambda n: g[n].ndim == self.wts[n].ndim
        h = exchange_start(f"rs_start_{grp}{l}", [g[n] if cut(n) else _blocks(n, g[n]) for n in GROUPS[grp]], False)
        self.rs[(l, grp)] = h
        return h["token"]

    def small(self, tag, entries):
        entries = sorted(entries, key=lambda e: e[2].size % LANE != 0)
        flat = jnp.concatenate([a.reshape(-1) for _, _, a in entries])
        flat = jnp.pad(flat, (0, (-flat.shape[0]) % (8 * LANE))).reshape(-1, LANE)
        h = exchange_start(f"small_start_{tag}", [flat], True)
        self.sm[tag] = (h, [(n, l, a.shape) for n, l, a in entries])
        return h["token"]

    def collect(self, groups, after):
        parts = {}
        for grp in groups:
            for l in reversed(range(DEPTH)):
                got = exchange_wait(f"rs_wait_{grp}{l}", self.rs[(l, grp)], [after])
                for n, a in zip(GROUPS[grp], got):
                    parts.setdefault(n, [None] * DEPTH)[l] = a
        return parts

    def collect_small(self, after):
        gots, where, parts = [], {}, {}
        for tag, (h, layout) in self.sm.items():
            (got,) = exchange_wait(f"small_wait_{tag}", h, [after])
            got = got.reshape(N_DEV, -1)
            off = 0
            for n, l, shape in layout:
                size = 1
                for d in shape:
                    size *= d
                if len(shape) == 1 and size % LANE == 0 and off % LANE == 0:
                    where.setdefault(n, [None] * (1 if l is None else DEPTH))[l or 0] = (len(gots), off)
                else:
                    part = got[:, off:off + size].reshape((N_DEV,) + tuple(shape))
                    if l is None:
                        parts[n] = [part]
                    else:
                        parts.setdefault(n, [None] * DEPTH)[l] = part
                off += size
            gots.append(got)
        return gots, where, parts


def kernel(x, p, positions, g_mix, w_in, q_norm, w_uq, kv_norm, w_ukv, w_pool, pool_scale, ssd_conv_w, ssd_conv_b,
           ssd_dt_bias, ssd_a_log, ssd_d, ssd_norm, lru_conv_w, lru_conv_b, lru_w_a, lru_b_a, lru_w_i, lru_b_i,
           lru_lambda, w_branch, w_out, g_mlp, w_ff1, w_ff2, g_ple, w_ple_gate, w_ple, g_final, loss_target, m_g_mix,
           m_w_in, m_q_norm, m_w_uq, m_kv_norm, m_w_ukv, m_w_pool, m_pool_scale, m_ssd_conv_w, m_ssd_conv_b,
           m_ssd_dt_bias, m_ssd_a_log, m_ssd_d, m_ssd_norm, m_lru_conv_w, m_lru_conv_b, m_lru_w_a, m_lru_b_a,
           m_lru_w_i, m_lru_b_i, m_lru_lambda, m_w_branch, m_w_out, m_g_mlp, m_w_ff1, m_w_ff2, m_g_ple, m_w_ple_gate,
           m_w_ple, m_g_final, v_g_mix, v_w_in, v_q_norm, v_w_uq, v_kv_norm, v_w_ukv, v_w_pool, v_pool_scale,
           v_ssd_conv_w, v_ssd_conv_b, v_ssd_dt_bias, v_ssd_a_log, v_ssd_d, v_ssd_norm, v_lru_conv_w, v_lru_conv_b,
           v_lru_w_a, v_lru_b_a, v_lru_w_i, v_lru_b_i, v_lru_lambda, v_w_branch, v_w_out, v_g_mlp, v_w_ff1, v_w_ff2,
           v_g_ple, v_w_ple_gate, v_w_ple, v_g_final):
    given = dict(locals())
    wts = {n: given[n] for n in WEIGHTS}
    ctx = _Exchanges(wts)
    loss, grad_x = local_step(x[0], p[:, 0], positions[0], loss_target[0], wts, ctx)

    def update(parts):
        out = {}
        for n, eight in parts.items():
            step = adamw_columns if n == "w_in" else adamw
            out[n] = step(f"adamw_{n}", eight, wts[n], given["m_" + n], given["v_" + n])
        return out

    outs = update(ctx.collect(("C", "B"), grad_x))
    late = outs["w_ff1"][1]
    outs.update(update(ctx.collect(("A",), late)))
    gots, where, parts = ctx.collect_small(late)
    outs.update(update(parts))
    names = sorted(where)
    rows = lambda a: a[None] if a.ndim == 1 else a
    res = adamw_packed("adamw_vectors", gots, [where[n] for n in names], [rows(wts[n]) for n in names],
                       [rows(given["m_" + n]) for n in names], [rows(given["v_" + n]) for n in names])
    for n, four in zip(names, res):
        outs[n] = [a[0] for a in four] if wts[n].ndim == 1 else four
    loss = lax.psum(loss, AXES)
    return (loss, grad_x[None], *[outs[n][0] for n in WEIGHTS], *[outs[n][1] for n in WEIGHTS],
            *[outs[n][2] for n in WEIGHTS], *[outs[n][3] for n in WEIGHTS])
```

```python
import functools

import jax
import jax.numpy as jnp
from jax import lax
from jax.experimental import pallas as pl
from jax.experimental.pallas import tpu as pltpu

f32 = jnp.float32
bf16 = jnp.bfloat16

D_MODEL = 1024
MIX = 512
N_HEADS = 8
QK_NOPE, QK_ROPE, V_HEAD = 64, 32, 64
Q_LORA, KV_LORA = 384, 256
ROPE_THETA = 10000.0
POOL_WINDOWS = (2, 4, 8, 16)
SSD_CHUNK = 128
SSD_CHUNKS_PER_TILE = 2
SSD_XBC = 768
CONV_W = 4
LRU_C = 8.0
EPS = 1e-6
IN_COLS = 7592
ADAM_LR, ADAM_B1, ADAM_B2, ADAM_EPS, ADAM_WD, ADAM_STEP = 0.001, 0.9, 0.999, 1e-08, 0.01, 10

LANE = 128
HALO = 8
POOL_HALO = 16
VMEM_LIMIT = 56 * 1024 * 1024
MATMUL_MAX_K_TILE = 4096
MATMUL_ACC_PASS_WEIGHT = 0.3
MATMUL_VMEM_BUDGET = 40 * 1024 * 1024
N_DEV = 8
AXES = ("x", "y", "c")

U_COLS = 7680
U_GATES, U_POOL, U_Z, U_LG, U_LX, U_XBC, U_CQ, U_KR, U_CKV = (
    (0, 4096), (4096, 512), (4608, 512), (5120, 512), (5632, 512), (6144, 768), (6912, 384), (7296, 128), (7424, 256))
KR_LANE = 64
DT_LANES = 8
U_DTYPE = bf16


def _cp(sem):
    return pltpu.CompilerParams(dimension_semantics=sem, vmem_limit_bytes=VMEM_LIMIT)


def _pick(dim, pref):
    if dim <= pref:
        return dim
    t = pref
    while t >= LANE:
        if dim % t == 0:
            return t
        t -= LANE
    t = pref
    while dim % t:
        t -= 8
    return t


@functools.partial(jax.custom_vjp, nondiff_argnums=(1,))
def shift_down(x, k):
    row = lax.broadcasted_iota(jnp.int32, x.shape, 0)
    return jnp.where(row >= k, pltpu.roll(x, k, 0), 0.0)


def _shift_down_fwd(x, k):
    return shift_down(x, k), None


def _shift_down_bwd(k, _, g):
    r = g.shape[0]
    row = lax.broadcasted_iota(jnp.int32, g.shape, 0)
    return (jnp.where(row < r - k, pltpu.roll(g, r - k, 0), 0.0),)


shift_down.defvjp(_shift_down_fwd, _shift_down_bwd)


def _tile_spec(tm, width, cb, n=None):
    if n is None:
        return pl.BlockSpec((tm, width), lambda i: (i, cb))
    return pl.BlockSpec((tm, width), lambda i: (n - 1 - i, cb))


def _const_spec(shape):
    nd = len(shape)
    return pl.BlockSpec(shape, lambda i: (0,) * nd)


def seq_fwd(name, f, params, tiles, carries, outs, tm):
    rows = tiles[0][0].shape[0]
    n = rows // tm
    np_, nt, no, nc = len(params), len(tiles), len(outs), len(carries)

    def body(*refs):
        p_refs = refs[:np_]
        t_refs = refs[np_:np_ + nt]
        o_refs = refs[np_ + nt:np_ + nt + no]
        s_refs = refs[np_ + nt + no:np_ + nt + no + nc]
        c_refs = refs[np_ + nt + no + nc:]
        i = pl.program_id(0)

        @pl.when(i == 0)
        def _():
            for c in c_refs:
                c[...] = jnp.zeros_like(c)

        cvals = [c[...] for c in c_refs]
        for s, c in zip(s_refs, cvals):
            s[0] = c
        o, newc = f(i, [r[...] for r in p_refs], cvals, [r[...].astype(f32) for r in t_refs])
        for r, v in zip(o_refs, o):
            r[...] = v.astype(r.dtype)
        for r, v in zip(c_refs, newc):
            r[...] = v

    in_specs = [_const_spec(p.shape) for p in params] + [_tile_spec(tm, w, cb) for (_, w, cb) in tiles]
    out_specs = [_tile_spec(tm, w, 0) for (w, _) in outs]
    out_specs += [pl.BlockSpec((1,) + tuple(c), lambda i, nd=len(c): (i,) + (0,) * nd) for c in carries]
    out_shape = [jax.ShapeDtypeStruct((rows, w), dt) for (w, dt) in outs]
    out_shape += [jax.ShapeDtypeStruct((n,) + tuple(c), f32) for c in carries]
    res = pl.pallas_call(
        body, name=name, grid=(n,), in_specs=in_specs, out_specs=out_specs, out_shape=out_shape,
        scratch_shapes=[pltpu.VMEM(tuple(c), f32) for c in carries],
        compiler_params=_cp(("arbitrary",)),
    )(*params, *[t[0] for t in tiles])
    return list(res[:no]), list(res[no:])


def seq_bwd(name, f, params, tiles, diff, saved, douts, gdtypes, tm):
    rows = tiles[0][0].shape[0]
    n = rows // tm
    np_, nt, nc, nd = len(params), len(tiles), len(saved), len(douts)
    didx = [k for k, d in enumerate(diff) if d]
    ng = len(didx)

    def body(*refs):
        p_refs = refs[:np_]
        t_refs = refs[np_:np_ + nt]
        s_refs = refs[np_ + nt:np_ + nt + nc]
        d_refs = refs[np_ + nt + nc:np_ + nt + nc + nd]
        pos = np_ + nt + nc + nd
        dp_refs = refs[pos:pos + np_]
        dt_refs = refs[pos + np_:pos + np_ + ng]
        dc_refs = refs[pos + np_ + ng:]
        i = pl.program_id(0)
        step = n - 1 - i

        @pl.when(i == 0)
        def _():
            for r in dp_refs:
                r[...] = jnp.zeros_like(r)
            for r in dc_refs:
                r[...] = jnp.zeros_like(r)

        pvals = [r[...] for r in p_refs]
        cvals = [r[0] for r in s_refs]
        xvals = [r[...].astype(f32) for r in t_refs]

        def fn(p, c, xd):
            x = list(xvals)
            for k, v in zip(didx, xd):
                x[k] = v
            return f(step, p, c, x)

        _, vjp = jax.vjp(fn, pvals, cvals, [xvals[k] for k in didx])
        dp, dc, dx = vjp(([r[...].astype(f32) for r in d_refs], [r[...] for r in dc_refs]))
        for r, v in zip(dp_refs, dp):
            r[...] += v
        for r, v in zip(dc_refs, dc):
            r[...] = v
        for r, v in zip(dt_refs, dx):
            r[...] = v.astype(r.dtype)

    in_specs = [_const_spec(p.shape) for p in params] + [_tile_spec(tm, w, cb, n) for (_, w, cb) in tiles]
    in_specs += [pl.BlockSpec((1,) + tuple(s.shape[1:]), lambda i, nd_=s.ndim - 1: (n - 1 - i,) + (0,) * nd_) for s in saved]
    in_specs += [_tile_spec(tm, d.shape[1], 0, n) for d in douts]
    args = list(params) + [t[0] for t in tiles] + list(saved) + list(douts)
    out_specs = [_const_spec(p.shape) for p in params] + [_tile_spec(tm, tiles[k][1], 0, n) for k in didx]
    out_shape = [jax.ShapeDtypeStruct(p.shape, f32) for p in params]
    out_shape += [jax.ShapeDtypeStruct((rows, tiles[k][1]), dt) for k, dt in zip(didx, gdtypes)]
    res = pl.pallas_call(
        body, name=name, grid=(n,), in_specs=in_specs, out_specs=out_specs, out_shape=out_shape,
        scratch_shapes=[pltpu.VMEM(tuple(s.shape[1:]), f32) for s in saved],
        compiler_params=_cp(("arbitrary",)),
    )(*args)
    return list(res[:np_]), list(res[np_:])


def _halvings(dim, lo, hi):
    t, out = _pick(dim, hi), []
    while t >= min(lo, dim) and dim % t == 0:
        out.append(t)
        if t % 2 or (t // 2) % LANE:
            break
        t //= 2
    return out


def _matmul_tiles(m, n, k, a_item, b_item, per_out, max_tn=1024, whole_rows=False):
    def vmem_bytes(tm, tn, tk):
        acc = 4 if k // tk > 1 else 0
        return 2 * (tm * tk * a_item + tk * tn * b_item + tm * tn * per_out) + tm * tn * acc

    def traffic(tm, tn, tk):
        nk = k // tk
        return (m * k * a_item * (1 if nk == 1 else n // tn) + k * n * b_item * (m // tm)
                + (nk - 1) * m * n * 8 * MATMUL_ACC_PASS_WEIGHT)

    cands = [(traffic(tm, tn, tk), -tm * tn, tm, tn, tk)
             for tk in _halvings(k, 512, MATMUL_MAX_K_TILE) for tm in _halvings(m, 256, 4096)
             for tn in ([n] if whole_rows else _halvings(n, 512, min(1024, max_tn)))
             if vmem_bytes(tm, tn, tk) <= MATMUL_VMEM_BUDGET]
    return min(cands)[2:]


def matmul(name, a, b, *, ta=False, tb=False, outs=(f32,), epi=None, extras=(), rows=(), row_sums=0, deps=(),
           out_blocks=0):
    m, k = (a.shape[1], a.shape[0]) if ta else a.shape
    n = b.shape[0] if tb else b.shape[1]
    per_out = sum(jnp.dtype(dt).itemsize for dt in outs) + sum(e[0].dtype.itemsize for e in extras)
    whole_rows = bool(rows) or row_sums > 0
    tm, tn, tk = _matmul_tiles(m, n, k, a.dtype.itemsize, b.dtype.itemsize, per_out,
                               n // out_blocks if out_blocks else n, whole_rows)
    nk = k // tk
    ne, nr, nd, no = len(extras), len(rows), len(deps), len(outs)
    dims = (((0 if ta else 1,), (1 if tb else 0,)), ((), ()))

    def body(*refs):
        a_ref, b_ref = refs[0], refs[1]
        e_refs = refs[2:2 + ne]
        r_refs = refs[2 + ne:2 + ne + nr]
        o_refs = refs[2 + ne + nr + nd:2 + ne + nr + nd + no]
        s_refs = refs[2 + ne + nr + nd + no:2 + ne + nr + nd + no + row_sums]
        i, kk = pl.program_id(0), pl.program_id(2)
        part = lax.dot_general(a_ref[...].astype(bf16), b_ref[...].astype(bf16), dims, preferred_element_type=f32)

        def finish(total):
            res = (total,) if epi is None else epi(total, *[e[...] for e in e_refs], *[r[...] for r in r_refs])
            for r, v in zip(o_refs, res[:no]):
                r[...] = v.astype(r.dtype)
            for r, v in zip(s_refs, res[no:]):
                v8 = jnp.broadcast_to(v, r.shape)

                @pl.when(i == 0)
                def _(r=r, v8=v8):
                    r[...] = v8

                @pl.when(i > 0)
                def _(r=r, v8=v8):
                    r[...] += v8

        if nk == 1:
            finish(part)
            return
        acc = refs[-1]

        @pl.when(kk == 0)
        def _():
            acc[...] = part

        @pl.when(jnp.logical_and(kk > 0, kk < nk - 1))
        def _():
            acc[...] += part

        @pl.when(kk == nk - 1)
        def _():
            finish(acc[...] + part)

    a_spec = pl.BlockSpec((tk, tm), lambda i, j, q: (q, i)) if ta else pl.BlockSpec((tm, tk), lambda i, j, q: (i, q))
    b_spec = pl.BlockSpec((tn, tk), lambda i, j, q: (j, q)) if tb else pl.BlockSpec((tk, tn), lambda i, j, q: (q, j))
    def e_spec(e):
        if len(e) == 3:
            return pl.BlockSpec((tm, e[2]), lambda i, j, q, cb=e[1] // e[2]: (i, cb))
        assert e[1] % tn == 0
        return pl.BlockSpec((tm, tn), lambda i, j, q, off=e[1] // tn: (i, off + j))

    e_specs = [e_spec(e) for e in extras]
    r_specs = [pl.BlockSpec((1, tn), lambda i, j, q: (0, j)) for _ in rows]
    if out_blocks:
        per = n // out_blocks // tn
        out_spec = pl.BlockSpec((None, tm, tn), lambda i, j, q: (j // per, i, j % per))
        out_dims = (out_blocks, m, n // out_blocks)
    else:
        out_spec = pl.BlockSpec((tm, tn), lambda i, j, q: (i, j))
        out_dims = (m, n)
    res = pl.pallas_call(
        body, name=name, grid=(m // tm, n // tn, nk),
        in_specs=[a_spec, b_spec] + e_specs + r_specs + [pl.BlockSpec(memory_space=pl.ANY) for _ in deps],
        out_specs=[out_spec for _ in outs] + [pl.BlockSpec((8, tn), lambda i, j, q: (0, j))] * row_sums,
        out_shape=[jax.ShapeDtypeStruct(out_dims, dt) for dt in outs] + [jax.ShapeDtypeStruct((8, n), f32)] * row_sums,
        scratch_shapes=[pltpu.VMEM((tm, tn), f32)] if nk > 1 else [],
        compiler_params=_cp(("arbitrary" if row_sums else "parallel", "parallel", "arbitrary")),
    )(a, b, *[e[0] for e in extras], *rows, *deps)
    return res[0] if len(res) == 1 else tuple(res)


def merge_fwd(name, ys, wbs, u):
    rows, n_out = ys[0].shape[0], wbs[0].shape[1]
    nb = len(ys)
    tm, tn = _pick(rows, 1024), _pick(n_out, 512)

    def body(*refs):
        y_refs, w_refs, g_refs = refs[:nb], refs[nb:2 * nb], refs[2 * nb:3 * nb]
        m_ref, p_refs = refs[3 * nb], refs[3 * nb + 1:]
        total = None
        for y_ref, w_ref, g_ref, p_ref in zip(y_refs, w_refs, g_refs, p_refs):
            pre = jnp.dot(y_ref[...], w_ref[...], preferred_element_type=f32)
            p_ref[...] = pre.astype(p_ref.dtype)
            term = jax.nn.sigmoid(g_ref[...].astype(f32)) * pre
            total = term if total is None else total + term
        m_ref[...] = total.astype(m_ref.dtype)

    in_specs = [pl.BlockSpec((tm, y.shape[1]), lambda i, j: (i, 0)) for y in ys]
    in_specs += [pl.BlockSpec((w.shape[0], tn), lambda i, j: (0, j)) for w in wbs]
    in_specs += [pl.BlockSpec((tm, tn), lambda i, j, off=n * (n_out // tn): (i, off + j)) for n in range(nb)]
    out_spec = pl.BlockSpec((tm, tn), lambda i, j: (i, j))
    res = pl.pallas_call(
        body, name=name, grid=(rows // tm, n_out // tn), in_specs=in_specs, out_specs=[out_spec] * (nb + 1),
        out_shape=[jax.ShapeDtypeStruct((rows, n_out), bf16)] * (nb + 1),
        compiler_params=_cp(("parallel", "parallel")),
    )(*ys, *wbs, *([u] * nb))
    return res[0], list(res[1:])


ATT_SCALE = (QK_NOPE + QK_ROPE) ** -0.5
LN2 = 0.6931471805599453
ATT_C = ATT_SCALE / LN2
NT = (((1,), (1,)), ((), ()))
TN = (((0,), (0,)), ((), ()))


def _causal(tq, tk):
    return lax.broadcasted_iota(jnp.int32, (tq, tk), 0) >= lax.broadcasted_iota(jnp.int32, (tq, tk), 1)


def _tri_pairs(n, by_column):
    if by_column:
        pairs = [(i, j) for j in range(n) for i in range(j, n)]
    else:
        pairs = [(i, j) for i in range(n) for j in range(i + 1)]
    return (jnp.asarray([a for a, _ in pairs], jnp.int32), jnp.asarray([b for _, b in pairs], jnp.int32))


FWD_HEADS_PER_STEP = 8
HEADS_PER_STEP = 8
HEAD_PAIR = HEADS_PER_STEP * LANE


def attn_fwd(q, k, v, t):
    rows = q.shape[0]
    n = rows // t
    it, jt = _tri_pairs(n, False)

    def body(it_ref, jt_ref, q_ref, k_ref, v_ref, o_ref, lse_ref, m_s, l_s, acc_s):
        s_id = pl.program_id(1)
        i, j = it_ref[s_id], jt_ref[s_id]

        @pl.when(j == 0)
        def _():
            m_s[...] = jnp.full_like(m_s, -jnp.inf)
            l_s[...] = jnp.zeros_like(l_s)
            acc_s[...] = jnp.zeros_like(acc_s)

        def step(diag):
            for hh in range(FWD_HEADS_PER_STEP):
                sl = slice(LANE * hh, LANE * (hh + 1))
                s = lax.dot_general(q_ref[:, sl], k_ref[:, sl], NT, preferred_element_type=f32)
                if diag:
                    s = jnp.where(_causal(t, t), s, -jnp.inf)
                m_prev = m_s[:, sl]
                m_new = jnp.maximum(m_prev, jnp.max(s, axis=1, keepdims=True))
                alpha = jnp.exp2(m_prev - m_new)
                p = jnp.exp2(s - m_new[:, :1])
                l_s[:, sl] = alpha * l_s[:, sl] + jnp.sum(p, axis=1, keepdims=True)
                acc_s[:, sl] = alpha * acc_s[:, sl] + jnp.dot(p.astype(bf16), v_ref[:, sl], preferred_element_type=f32)
                m_s[:, sl] = m_new

        pl.when(j < i)(lambda: step(False))

        @pl.when(j == i)
        def _():
            step(True)
            o_ref[...] = (acc_s[...] / l_s[...]).astype(o_ref.dtype)
            lse_ref[...] = m_s[...] + jnp.log2(l_s[...])

    width = FWD_HEADS_PER_STEP * LANE
    qs = pl.BlockSpec((t, width), lambda h, s, it_, jt_: (it_[s], h))
    ks = pl.BlockSpec((t, width), lambda h, s, it_, jt_: (jt_[s], h))
    hw = N_HEADS * LANE
    return pl.pallas_call(
        body, name="attn_fwd",
        grid_spec=pltpu.PrefetchScalarGridSpec(
            num_scalar_prefetch=2, grid=(hw // width, it.shape[0]), in_specs=[qs, ks, ks], out_specs=[qs, qs],
            scratch_shapes=[pltpu.VMEM((t, width), f32)] * 3),
        out_shape=[jax.ShapeDtypeStruct((rows, hw), bf16), jax.ShapeDtypeStruct((rows, hw), f32)],
        compiler_params=_cp(("parallel", "arbitrary")),
    )(it, jt, q, k, v)


def attn_bwd(q, k, v, do, o, lse, t):
    rows = q.shape[0]
    n = rows // t
    it, jt = _tri_pairs(n, True)

    def body(it_ref, jt_ref, q_ref, k_ref, v_ref, do_ref, o_ref, lse_ref, dq_ref, dk_ref, dv_ref, dk_s, dv_s):
        s_id = pl.program_id(1)
        i, j = it_ref[s_id], jt_ref[s_id]

        @pl.when(s_id == 0)
        def _():
            dq_ref[...] = jnp.zeros_like(dq_ref)

        @pl.when(i == j)
        def _():
            dk_s[...] = jnp.zeros_like(dk_s)
            dv_s[...] = jnp.zeros_like(dv_s)

        q_rows = pl.ds(pl.multiple_of(i * t, t), t)

        def step(diag):
            for hh in range(HEADS_PER_STEP):
                sl = slice(LANE * hh, LANE * (hh + 1))
                qh, kh, vh, doh = q_ref[:, sl], k_ref[:, sl], v_ref[:, sl], do_ref[:, sl]
                s = lax.dot_general(qh, kh, NT, preferred_element_type=f32)
                p = jnp.exp2(s - lse_ref[:, sl][:, :1])
                if diag:
                    p = jnp.where(_causal(t, t), p, 0.0)
                dp = lax.dot_general(doh, vh, NT, preferred_element_type=f32)
                delta = jnp.sum(doh.astype(f32) * o_ref[:, sl].astype(f32), axis=1, keepdims=True)
                ds = (p * (dp - delta) * LN2).astype(bf16)
                dv_s[:, sl] += lax.dot_general(p.astype(bf16), doh, TN, preferred_element_type=f32)
                dk_s[:, sl] += lax.dot_general(ds, qh, TN, preferred_element_type=f32)
                dq_ref[q_rows, sl] += jnp.dot(ds, kh, preferred_element_type=f32)

        pl.when(i > j)(lambda: step(False))
        pl.when(i == j)(lambda: step(True))

        @pl.when(i == n - 1)
        def _():
            dk_ref[...] = dk_s[...]
            dv_ref[...] = dv_s[...]

    qs = pl.BlockSpec((t, HEAD_PAIR), lambda h, s, it_, jt_: (it_[s], h))
    ks = pl.BlockSpec((t, HEAD_PAIR), lambda h, s, it_, jt_: (jt_[s], h))
    dqs = pl.BlockSpec((rows, HEAD_PAIR), lambda h, s, it_, jt_: (0, h), pipeline_mode=pl.Buffered(1))
    hw = N_HEADS * LANE
    return pl.pallas_call(
        body, name="attn_bwd",
        grid_spec=pltpu.PrefetchScalarGridSpec(
            num_scalar_prefetch=2, grid=(hw // HEAD_PAIR, it.shape[0]), in_specs=[qs, ks, ks, qs, qs, qs],
            out_specs=[dqs, ks, ks], scratch_shapes=[pltpu.VMEM((t, HEAD_PAIR), f32)] * 2),
        out_shape=[jax.ShapeDtypeStruct((rows, hw), f32)] * 3,
        compiler_params=_cp(("parallel", "arbitrary")),
    )(it, jt, q, k, v, do, o, lse)


def _steps(tm):
    k, out = 1, []
    while k < tm:
        out.append(k)
        k *= 2
    return out


def _gelu_gate(h, g):
    return h * jax.nn.gelu(g)


def scan_fwd(a, u, gate, tm):
    rows, ch = a.shape
    n = rows // tm

    def body(a_ref, u_ref, gt_ref, h_ref, y_ref, h_s):
        @pl.when(pl.program_id(0) == 0)
        def _():
            h_s[...] = jnp.zeros_like(h_s)

        av, bv = a_ref[...], u_ref[...]
        row = lax.broadcasted_iota(jnp.int32, av.shape, 0)
        for k in _steps(tm):
            a_sh = jnp.where(row >= k, pltpu.roll(av, k, 0), 1.0)
            b_sh = jnp.where(row >= k, pltpu.roll(bv, k, 0), 0.0)
            bv = av * b_sh + bv
            av = av * a_sh
        h = bv + av * h_s[HALO - 1:HALO, :]
        h_ref[...] = h
        y_ref[...] = _gelu_gate(h, gt_ref[...].astype(f32)).astype(y_ref.dtype)
        h_s[...] = h[tm - HALO:, :]

    spec = pl.BlockSpec((tm, ch), lambda i: (i, 0))
    gt_spec = pl.BlockSpec((tm, gate[1]), lambda i: (i, gate[2]))
    return pl.pallas_call(
        body, name="lru_scan_fwd", grid=(n,), in_specs=[spec, spec, gt_spec], out_specs=[spec, spec],
        out_shape=[jax.ShapeDtypeStruct((rows, ch), f32), jax.ShapeDtypeStruct((rows, ch), bf16)],
        scratch_shapes=[pltpu.VMEM((HALO, ch), f32)], compiler_params=_cp(("arbitrary",)),
    )(a, u, gate[0])


def scan_bwd(a, h, gate, dy, tm):
    rows, ch = a.shape
    n = rows // tm
    per = tm // HALO

    def body(a_ref, h_ref, hp_ref, gt_ref, dy_ref, da_ref, du_ref, dg_ref, g_s, a_s):
        i = pl.program_id(0)
        step = n - 1 - i

        @pl.when(i == 0)
        def _():
            g_s[...] = jnp.zeros_like(g_s)
            a_s[...] = jnp.zeros_like(a_s)

        _, vjp = jax.vjp(_gelu_gate, h_ref[...], gt_ref[...].astype(f32))
        dh, dgate = vjp(dy_ref[...].astype(f32))
        dg_ref[...] = dgate.astype(dg_ref.dtype)
        a0 = a_ref[...]
        row = lax.broadcasted_iota(jnp.int32, a0.shape, 0)
        av = jnp.where(row < tm - 1, pltpu.roll(a0, tm - 1, 0), a_s[0:1, :])
        bv = dh
        for k in _steps(tm):
            a_sh = jnp.where(row < tm - k, pltpu.roll(av, tm - k, 0), 1.0)
            b_sh = jnp.where(row < tm - k, pltpu.roll(bv, tm - k, 0), 0.0)
            bv = bv + av * b_sh
            av = av * a_sh
        g = bv + av * g_s[0:1, :]
        h_last = jnp.where(step > 0, hp_ref[HALO - 1:HALO, :], 0.0)
        h_prev = jnp.where(row >= 1, pltpu.roll(h_ref[...], 1, 0), h_last)
        du_ref[...] = g
        da_ref[...] = g * h_prev
        g_s[...] = g[0:HALO, :]
        a_s[...] = a0[0:HALO, :]

    spec = pl.BlockSpec((tm, ch), lambda i: (n - 1 - i, 0))
    hp_spec = pl.BlockSpec((HALO, ch), lambda i: (jnp.maximum((n - 1 - i) * per - 1, 0), 0))
    gt_spec = pl.BlockSpec((tm, gate[1]), lambda i: (n - 1 - i, gate[2]))
    return pl.pallas_call(
        body, name="lru_scan_bwd", grid=(n,), in_specs=[spec, spec, hp_spec, gt_spec, spec], out_specs=[spec, spec, spec],
        out_shape=[jax.ShapeDtypeStruct((rows, ch), f32)] * 2 + [jax.ShapeDtypeStruct((rows, ch), bf16)],
        scratch_shapes=[pltpu.VMEM((HALO, ch), f32)] * 2,
        compiler_params=_cp(("arbitrary",)),
    )(a, h, h, gate[0], dy)


def _rms(x, g):
    return x * lax.rsqrt(jnp.mean(x * x, axis=-1, keepdims=True) + EPS) * g


def f_rms(step, p, c, x):
    return [_rms(x[0], p[0])], []


def _rope_lanes(shape):
    lane = lax.broadcasted_iota(jnp.int32, shape, 1)
    return jnp.logical_and(lane >= KR_LANE, lane < KR_LANE + QK_ROPE)


def _rope_swap(x):
    lane = lax.broadcasted_iota(jnp.int32, x.shape, 1)
    half = QK_ROPE // 2
    sw = jnp.where(lane < KR_LANE + half, pltpu.roll(x, LANE - half, 1), pltpu.roll(x, half, 1))
    return jnp.where(_rope_lanes(x.shape), sw, 0.0)


def _rope(x, cosf, sinf):
    return x * cosf + _rope_swap(x) * sinf


def _heads(x):
    return [x[:, LANE * h:LANE * (h + 1)] for h in range(x.shape[1] // LANE)]


def q_rope_epi(q, cosf, sinf):
    return (jnp.concatenate([_rope(b, cosf, sinf) * ATT_C for b in _heads(q)], axis=1),)


def k_rope_epi(kn, kr, cosf, sinf):
    kr_rot = _rope(jnp.where(_rope_lanes(kr.shape), kr.astype(f32), 0.0), cosf, sinf)
    return (jnp.concatenate([b + kr_rot for b in _heads(kn)], axis=1),)


def rope_bwd(dqr, dkr, cosf, sinf, ddt):
    back = lambda g: g * cosf + _rope_swap(g * sinf)
    dq = jnp.concatenate([back(b) * ATT_C for b in _heads(dqr)], axis=1)
    dkrope = jnp.where(_rope_lanes(ddt.shape), back(sum(_heads(dkr))), 0.0) + ddt
    return dq, dkr, dkrope


def _conv(tail, x, w, b):
    xf = jnp.concatenate([tail, x], axis=0)
    acc = b + w[CONV_W - 1:CONV_W, :] * xf
    for k in range(CONV_W - 1):
        acc = acc + w[k:k + 1, :] * shift_down(xf, CONV_W - 1 - k)
    return acc[HALO:, :]


def f_pool(step, p, c, x):
    wp, sc = p
    (tail,) = c
    (u,) = x
    tm = u.shape[0]
    xf = jnp.concatenate([tail, u], axis=0)
    sums, s, w = [], xf, 1
    while w < POOL_WINDOWS[-1]:
        s = s + shift_down(s, w)
        w *= 2
        sums.append(s)
    t = step * tm + lax.broadcasted_iota(jnp.int32, (tm, 1), 0)
    ys = []
    for g, (w, s) in enumerate(zip(POOL_WINDOWS, sums)):
        sl = slice(LANE * g, LANE * (g + 1))
        cnt = jnp.minimum(t + 1, w).astype(f32)
        d = s[POOL_HALO:, sl] / cnt - u[:, sl]
        ys.append(jnp.dot(d.astype(bf16), wp[LANE * g:LANE * (g + 1), :].astype(bf16), preferred_element_type=f32))
    return [jnp.concatenate(ys, axis=1) * sc], [u[tm - POOL_HALO:, :]]


def f_ssd_tile(step, p, c, x):
    outs = []
    for k in range(x[0].shape[0] // SSD_CHUNK):
        o, c = f_ssd(step, p, c, [t[SSD_CHUNK * k:SSD_CHUNK * (k + 1), :] for t in x])
        outs.append(o[0])
    return [jnp.concatenate(outs, axis=0)], c


def f_ssd(step, p, c, x):
    conv_w, conv_b, dtb, alog, dsk, ng = p
    tail, s_in = c[0], c[1:]
    z, xbc, dt = x
    ln = z.shape[0]
    xc = jax.nn.silu(_conv(tail, xbc, conv_w, conv_b))
    xs, bb, cc = xc[:, :MIX], xc[:, MIX:MIX + LANE], xc[:, MIX + LANE:]
    dtv = jax.nn.softplus(dt + dtb[0:1, :])
    a = dtv * -jnp.exp(alog[0:1, :])
    ri = lax.broadcasted_iota(jnp.int32, (ln, ln), 0)
    ci = lax.broadcasted_iota(jnp.int32, (ln, ln), 1)
    tril = (ri >= ci).astype(f32)
    triu = (ri <= ci).astype(f32)
    hi = lax.Precision.HIGHEST
    a_cs = jnp.dot(tril, a, precision=hi, preferred_element_type=f32)
    a_cs_t = lax.dot_general(a, triu, TN, precision=hi, preferred_element_type=f32)
    a_tot = jnp.sum(a, axis=0, keepdims=True)
    lane = lax.broadcasted_iota(jnp.int32, (1, LANE), 1)
    half = [(lane < 64).astype(f32), (lane >= 64).astype(f32)]
    hrow = lax.broadcasted_iota(jnp.int32, (LANE, 1), 0)

    def head(v, h):
        return jnp.sum(v * (lane == h).astype(f32), axis=1, keepdims=True)

    def pair(v, j):
        return head(v, 2 * j) * half[0] + head(v, 2 * j + 1) * half[1]

    cg = [(cc * half[g]).astype(bf16) for g in range(2)]
    bg = [(bb * half[g]).astype(bf16) for g in range(2)]
    cb = [lax.dot_general(cg[g], bg[g], NT, preferred_element_type=f32) for g in range(2)]
    ys, s_out = [], []
    for j in range(4):
        g = j // 2
        xs_j = xs[:, LANE * j:LANE * (j + 1)]
        xj = xs_j * pair(dtv, j)
        yj = xs_j * pair(dsk[0:1, :], j)
        for hh in range(2):
            h = 2 * j + hh
            rowv = jnp.sum(a_cs_t * (hrow == h).astype(f32), axis=0, keepdims=True)
            lmat = jnp.exp(jnp.where(ri >= ci, head(a_cs, h) - rowv, -jnp.inf))
            yj = yj + jnp.dot((cb[g] * lmat).astype(bf16), (xj * half[hh]).astype(bf16), preferred_element_type=f32)
        acs = pair(a_cs, j)
        tot = pair(a_tot, j)
        yj = yj + jnp.exp(acs) * jnp.dot(cg[g], s_in[j].astype(bf16), preferred_element_type=f32)
        s_new = jnp.exp(tot) * s_in[j] + lax.dot_general(bg[g], (xj * jnp.exp(tot - acs)).astype(bf16), TN,
                                                         preferred_element_type=f32)
        ys.append(yj)
        s_out.append(s_new)
    y = jnp.concatenate(ys, axis=1) * jax.nn.silu(z)
    return [_rms(y, ng)], [xbc[ln - HALO:, :]] + s_out


def _neg_expm1(y):
    series = -y * (1.0 + y * (0.5 + y * (1.0 / 6 + y * (1.0 / 24 + y * (1.0 / 120)))))
    return jnp.where(y > -0.05, series, 1.0 - jnp.exp(y))


def f_lru_pre(step, p, c, x):
    cw, cb_, wa, ba, wi, bi, lam = p
    (tail,) = c
    (lx,) = x
    tm = lx.shape[0]
    xc = _conv(tail, lx, cw, cb_)
    xb = xc.astype(bf16)
    r = jax.nn.sigmoid(jnp.dot(xb, wa.astype(bf16), preferred_element_type=f32) + ba)
    it = jax.nn.sigmoid(jnp.dot(xb, wi.astype(bf16), preferred_element_type=f32) + bi)
    log_a = -LRU_C * r * jax.nn.softplus(-lam)
    mult = jnp.sqrt(_neg_expm1(2.0 * log_a))
    return [jnp.exp(log_a), xc * it * mult], [lx[tm - HALO:, :]]


def loss_head(x, tgt, g, tm):
    rows, d = x.shape
    n = rows // tm

    def body(x_ref, t_ref, g_ref, loss_ref, dx_ref, dg_ref):
        @pl.when(pl.program_id(0) == 0)
        def _():
            loss_ref[...] = jnp.zeros_like(loss_ref)
            dg_ref[...] = jnp.zeros_like(dg_ref)

        def fn(gv, xv):
            err = _rms(xv, gv) - t_ref[...]
            return 0.5 * jnp.sum(jnp.mean(err * err, axis=-1, keepdims=True))

        val, (dg, dx) = jax.value_and_grad(fn, argnums=(0, 1))(g_ref[...], x_ref[...])
        loss_ref[...] += val
        dg_ref[...] += dg
        dx_ref[...] = dx

    spec = pl.BlockSpec((tm, d), lambda i: (i, 0))
    return pl.pallas_call(
        body, name="loss_head", grid=(n,), in_specs=[spec, spec, _const_spec((1, d))],
        out_specs=[_const_spec((8, LANE)), spec, _const_spec((1, d))],
        out_shape=[jax.ShapeDtypeStruct((8, LANE), f32), jax.ShapeDtypeStruct((rows, d), f32),
                   jax.ShapeDtypeStruct((1, d), f32)],
        compiler_params=_cp(("arbitrary",)),
    )(x, tgt, g)


def ew(name, fn, ins, outs, tm):
    rows = ins[0][0].shape[0]
    ni = len(ins)

    def body(*refs):
        res = fn(*[r[...].astype(f32) for r in refs[:ni]])
        for r, v in zip(refs[ni:], res):
            r[...] = v.astype(r.dtype)

    return pl.pallas_call(
        body, name=name, grid=(rows // tm,), in_specs=[_tile_spec(tm, w, cb) for (_, w, cb) in ins],
        out_specs=[_tile_spec(tm, w, 0) for (w, _) in outs],
        out_shape=[jax.ShapeDtypeStruct((rows, w), dt) for (w, dt) in outs],
        compiler_params=_cp(("parallel",)),
    )(*[t[0] for t in ins])


def _peers():
    x, y, c = lax.axis_index("x"), lax.axis_index("y"), lax.axis_index("c")
    me = 4 * x + 2 * y + c
    out = []
    for k in range(1, N_DEV):
        px = 1 - x if k & 4 else x
        py = 1 - y if k & 2 else y
        pc = 1 - c if k & 1 else c
        out.append(((px, py, pc), 4 * px + 2 * py + pc))
    return me, out


_HBM = pl.BlockSpec(memory_space=pltpu.HBM)
_SEM = pl.BlockSpec(memory_space=pltpu.SEMAPHORE)
_EFFECT = pltpu.SideEffectType.DATAFLOW_SIDE_EFFECTING


def _remote(src_ref, land_ref, gather, me, pid, dev, send_sems, recv_sems, k, recv_side):
    return pltpu.make_async_remote_copy(
        src_ref=src_ref if gather else src_ref.at[pid], dst_ref=land_ref.at[pid if recv_side else me],
        send_sem=send_sems.at[k], recv_sem=recv_sems.at[k], device_id=dev, device_id_type=pl.DeviceIdType.MESH)


def _own(src_ref, land_ref, gather, me, sem):
    return pltpu.make_async_copy(src_ref if gather else src_ref.at[me], land_ref.at[me], sem)


def exchange_start(name, srcs, gather, deps=()):
    n, nd = len(srcs), len(deps)
    shapes = [(s.shape if gather else s.shape[1:]) for s in srcs]
    lands = [lax.empty((N_DEV,) + tuple(sh), s.dtype) for s, sh in zip(srcs, shapes)]

    def body(*refs):
        src_refs, land_refs = refs[:n], refs[n:2 * n]
        send_sems, recv_sems, own_sem = refs[2 * n + nd:2 * n + nd + 3]
        token = refs[-1]
        me, peers = _peers()
        for k, (dev, pid) in enumerate(peers):
            for s_ref, l_ref in zip(src_refs, land_refs):
                _remote(s_ref, l_ref, gather, me, pid, dev, send_sems, recv_sems, k, False).start()
        for s_ref, l_ref in zip(src_refs, land_refs):
            _own(s_ref, l_ref, gather, me, own_sem).start()
        token[...] = jnp.zeros_like(token)

    hbm = lambda a: pltpu.with_memory_space_constraint(a, pltpu.HBM)
    res = pl.pallas_call(
        body, name=name,
        out_shape=(pltpu.SemaphoreType.DMA((N_DEV - 1,)), pltpu.SemaphoreType.DMA((N_DEV - 1,)), pltpu.SemaphoreType.DMA(()),
                   *[pltpu.HBM(a.shape, a.dtype) for a in list(srcs) + lands], jax.ShapeDtypeStruct((8, LANE), f32)),
        in_specs=[_HBM] * (2 * n) + [pl.BlockSpec(memory_space=pl.ANY)] * nd,
        out_specs=(_SEM, _SEM, _SEM, *([_HBM] * (2 * n)), pl.BlockSpec(memory_space=pltpu.VMEM)),
        input_output_aliases={i: 3 + i for i in range(2 * n)},
        compiler_params=pltpu.CompilerParams(has_side_effects=_EFFECT),
    )(*[hbm(a) for a in list(srcs) + lands], *deps)
    return dict(sems=res[:3], srcs=list(res[3:3 + n]), lands=list(res[3 + n:3 + 2 * n]), token=res[-1], gather=gather)


def exchange_wait(name, h, afters):
    n, gather = len(h["srcs"]), h["gather"]

    def body(*refs):
        src_refs, land_refs = refs[:n], refs[n:2 * n]
        send_sems, recv_sems, own_sem = refs[2 * n:2 * n + 3]
        me, peers = _peers()
        for k, (dev, pid) in enumerate(peers):
            for s_ref, l_ref in zip(src_refs, land_refs):
                _remote(s_ref, l_ref, gather, me, pid, dev, send_sems, recv_sems, k, True).wait_recv()
        for k, (dev, pid) in enumerate(peers):
            for s_ref, l_ref in zip(src_refs, land_refs):
                _remote(s_ref, l_ref, gather, me, pid, dev, send_sems, recv_sems, k, False).wait_send()
        for s_ref, l_ref in zip(src_refs, land_refs):
            _own(s_ref, l_ref, gather, me, own_sem).wait()

    arrs = h["srcs"] + h["lands"]
    res = pl.pallas_call(
        body, name=name, out_shape=tuple(pltpu.HBM(a.shape, a.dtype) for a in arrs),
        in_specs=[_HBM] * (2 * n) + [_SEM, _SEM, _SEM] + [pl.BlockSpec(memory_space=pl.ANY)] * len(afters),
        out_specs=tuple([_HBM] * (2 * n)), input_output_aliases={i: i for i in range(2 * n)},
        compiler_params=pltpu.CompilerParams(has_side_effects=_EFFECT),
    )(*arrs, *h["sems"], *afters)
    return list(res[n:])


def _adam_update(g, w, m, v):
    mn = ADAM_B1 * m + (1.0 - ADAM_B1) * g
    vn = ADAM_B2 * v + (1.0 - ADAM_B2) * jnp.square(g)
    m_hat = mn / (1.0 - ADAM_B1 ** ADAM_STEP)
    v_hat = vn / (1.0 - ADAM_B2 ** ADAM_STEP)
    return -ADAM_LR * (m_hat / (jnp.sqrt(v_hat) + ADAM_EPS) + ADAM_WD * w), mn, vn


def _adamw_vectors(name, parts, w, m, v):
    nl = len(parts)

    def body(*refs):
        p_refs = refs[:nl]
        w_ref, m_ref, v_ref, g_ref, d_ref, nm_ref, nv_ref = refs[nl:]
        for ll, p_ref in enumerate(p_refs):
            row = slice(ll, ll + 1)
            g = p_ref[0:1, :]
            for i in range(1, N_DEV):
                g = g + p_ref[i:i + 1, :]
            delta, mn, vn = _adam_update(g, w_ref[row, :], m_ref[row, :], v_ref[row, :])
            g_ref[row, :] = g
            d_ref[row, :] = delta
            nm_ref[row, :] = mn
            nv_ref[row, :] = vn

    return list(pl.pallas_call(body, name=name, out_shape=[jax.ShapeDtypeStruct(w.shape, f32)] * 4)(*parts, w, m, v))


def adamw_packed(name, gots, where, ws, ms, vs):
    ng, npar = len(gots), len(ws)

    def body(*refs):
        g_refs = refs[:ng]
        w_refs, m_refs, v_refs = (refs[ng + k * npar:ng + (k + 1) * npar] for k in range(3))
        o_refs = refs[ng + 3 * npar:]
        for p in range(npar):
            width = w_refs[p].shape[1]
            for l, (which, off) in enumerate(where[p]):
                row = slice(l, l + 1)
                cols = slice(off, off + width)
                g = g_refs[which][0:1, cols]
                for i in range(1, N_DEV):
                    g = g + g_refs[which][i:i + 1, cols]
                delta, mn, vn = _adam_update(g, w_refs[p][row, :], m_refs[p][row, :], v_refs[p][row, :])
                for k, val in enumerate((g, delta, mn, vn)):
                    o_refs[4 * p + k][row, :] = val

    out_shape = [jax.ShapeDtypeStruct(w.shape, f32) for w in ws for _ in range(4)]
    res = pl.pallas_call(body, name=name, out_shape=out_shape, compiler_params=_cp(()))(*gots, *ws, *ms, *vs)
    return [list(res[4 * p:4 * p + 4]) for p in range(npar)]


def adamw_columns(name, parts, w, m, v):
    nl, kk, cc = w.shape
    view = lambda a: jnp.transpose(a, (2, 0, 1))
    tc = min(LANE, cc)

    def body(*refs):
        p_refs = refs[:nl]
        w_ref, m_ref, v_ref, g_ref, d_ref, nm_ref, nv_ref = refs[nl:]
        for l, p_ref in enumerate(p_refs):
            g = p_ref[0].astype(f32)
            for i in range(1, N_DEV):
                g = g + p_ref[i].astype(f32)
            g = g.T
            delta, mn, vn = _adam_update(g, w_ref[:, l, :], m_ref[:, l, :], v_ref[:, l, :])
            g_ref[:, l, :] = g
            d_ref[:, l, :] = delta
            nm_ref[:, l, :] = mn
            nv_ref[:, l, :] = vn

    p_spec = pl.BlockSpec((N_DEV, kk, tc), lambda j: (0, 0, j))
    w_spec = pl.BlockSpec((tc, nl, kk), lambda j: (j, 0, 0))
    res = pl.pallas_call(
        body, name=name, grid=(pl.cdiv(cc, tc),), in_specs=[p_spec] * nl + [w_spec] * 3, out_specs=[w_spec] * 4,
        out_shape=[jax.ShapeDtypeStruct((cc, nl, kk), f32)] * 4, compiler_params=_cp(("parallel",)),
    )(*parts, view(w), view(m), view(v))
    return [jnp.transpose(a, (1, 2, 0)) for a in res]


def adamw(name, parts, w, m, v):
    nl = len(parts)
    shape = w.shape[1:]
    c = shape[-1]
    r = 1
    for s in shape[:-1]:
        r *= s
    if r == 1:
        return _adamw_vectors(name, parts, w, m, v)
    tr = _pick(r, 256) if r % 8 == 0 else r
    nb = r // tr
    parts2 = [p.reshape(N_DEV, r, c) for p in parts]
    w2, m2, v2 = (a.reshape(nl, r, c) for a in (w, m, v))

    def body(*refs):
        p_refs = refs[:nl]
        w_ref, m_ref, v_ref, g_ref, d_ref, nm_ref, nv_ref = refs[nl:]
        layer = pl.program_id(0)
        for ll, p_ref in enumerate(p_refs):
            @pl.when(layer == ll)
            def _(p_ref=p_ref):
                g = p_ref[0].astype(f32)
                for i in range(1, N_DEV):
                    g = g + p_ref[i].astype(f32)
                delta, mn, vn = _adam_update(g, w_ref[0], m_ref[0], v_ref[0])
                g_ref[0] = g
                d_ref[0] = delta
                nm_ref[0] = mn
                nv_ref[0] = vn

    def p_spec(ll):
        return pl.BlockSpec((N_DEV, tr, c), lambda l, i: (0, jnp.where(l == ll, i, jnp.where(l > ll, nb - 1, 0)), 0))

    spec = pl.BlockSpec((1, tr, c), lambda l, i: (l, i, 0))
    res = pl.pallas_call(
        body, name=name, grid=(nl, nb), in_specs=[p_spec(ll) for ll in range(nl)] + [spec, spec, spec],
        out_specs=[spec] * 4, out_shape=[jax.ShapeDtypeStruct((nl, r, c), f32)] * 4,
        compiler_params=_cp(("arbitrary", "arbitrary")),
    )(*parts2, w2, m2, v2)
    return [a.reshape(w.shape) for a in res]


_IN_SPLITS = dict(cq=(0, 384), ckv=(384, 640), kr=(640, 672), pool=(672, 1184), z=(1184, 1696), xbc=(1696, 2464),
                  dt=(2464, 2472), lg=(2472, 2984), lx=(2984, 3496), gates=(3496, 7592))


W_IN_SHARD = IN_COLS // N_DEV

_PAD_ORDER = ("gates", "pool", "z", "lg", "lx", "xbc", "cq", "dt", KR_LANE - DT_LANES, "kr", LANE - KR_LANE - QK_ROPE, "ckv")
_SEGMENTS = ((0, U_CQ[0], 384), (384, U_CKV[0], 256), (640, U_KR[0] + KR_LANE, QK_ROPE), (672, U_POOL[0], 512),
             (1184, U_Z[0], 512), (1696, U_XBC[0], 768), (2464, U_KR[0], DT_LANES), (2472, U_LG[0], 512), (2984, U_LX[0], 512),
             (3496, 0, 4096))


def _pad_w_in(shards):
    rows = shards.shape[1]
    pieces = []
    for item in _PAD_ORDER:
        if isinstance(item, int):
            pieces.append(jnp.zeros((rows, item), shards.dtype))
            continue
        a, b = _IN_SPLITS[item]
        for d in range(a // W_IN_SHARD, (b - 1) // W_IN_SHARD + 1):
            lo, hi = max(a, d * W_IN_SHARD), min(b, (d + 1) * W_IN_SHARD)
            pieces.append(shards[d, :, lo - d * W_IN_SHARD:hi - d * W_IN_SHARD])
    return jnp.concatenate(pieces, axis=1)


def _w_in_blocks(g):
    blocks = []
    for d in range(N_DEV):
        a, b = d * W_IN_SHARD, (d + 1) * W_IN_SHARD
        pieces = []
        for ref, pad, width in _SEGMENTS:
            lo, hi = max(a, ref), min(b, ref + width)
            if lo < hi:
                pieces.append(g[:, pad + lo - ref:pad + hi - ref])
        blocks.append(jnp.concatenate(pieces, axis=1))
    return jnp.stack(blocks).astype(bf16)


def _head_pad_cols(w, per, lo, hi):
    k = w.shape[0]
    w = w.reshape(k, N_HEADS, per)[:, :, lo:hi]
    return jnp.pad(w, ((0, 0), (0, 0), (0, LANE - (hi - lo)))).reshape(k, N_HEADS * LANE)


def _head_unpad_cols(g, n):
    k = g.shape[0]
    return g.reshape(k, N_HEADS, LANE)[:, :, :n]


def _on_diagonal():
    i = lax.broadcasted_iota(jnp.int32, (8, 1, 8, 1), 0)
    j = lax.broadcasted_iota(jnp.int32, (8, 1, 8, 1), 2)
    return i == j


def _block_diag(w):
    w4 = jnp.broadcast_to(w[:, :, None, :], (8, 64, 8, 64))
    return jnp.where(_on_diagonal(), w4, 0.0).reshape(MIX, MIX)


def _block_diag_inv(g):
    return jnp.sum(jnp.where(_on_diagonal(), g.reshape(8, 64, 8, 64), 0.0), axis=2)


def _head8(v):
    return jnp.pad(v[None, :], ((0, 7), (0, LANE - v.shape[0])))


GROUPS = dict(A=("w_in",), B=("w_uq", "w_ukv", "ssd_conv_w", "lru_conv_w", "w_branch", "w_out"),
              C=("w_ff1", "w_ff2", "w_ple_gate", "w_ple"))


def _kernel_weights(grp, fw):
    if grp == "A":
        w_in = _pad_w_in(fw["w_in"])
        return dict(w_in=w_in, w_dt=w_in[:, U_KR[0]:U_KR[0] + LANE])
    if grp == "C":
        return dict(w_ff1=fw["w_ff1"], w_ff2=fw["w_ff2"], w_pg=fw["w_ple_gate"], w_ple=fw["w_ple"])
    wb = fw["w_branch"]
    wb0 = jnp.pad(wb[0].reshape(N_HEADS, V_HEAD, D_MODEL), ((0, 0), (0, LANE - V_HEAD), (0, 0))).reshape(N_HEADS * LANE, D_MODEL)
    return dict(
        w_uq=_head_pad_cols(fw["w_uq"], QK_NOPE + QK_ROPE, 0, QK_NOPE + QK_ROPE),
        w_uk=_head_pad_cols(fw["w_ukv"], QK_NOPE + V_HEAD, 0, QK_NOPE),
        w_uv=_head_pad_cols(fw["w_ukv"], QK_NOPE + V_HEAD, QK_NOPE, QK_NOPE + V_HEAD),
        wb=[wb0, wb[1], wb[2], wb[3]], w_out=fw["w_out"], ssd_conv_w=fw["ssd_conv_w"], lru_conv_w=fw["lru_conv_w"])


def _layer_params(sp, l):
    row = lambda n: sp[n][l][None, :]
    return dict(
        g_mix=row("g_mix"), q_norm=row("q_norm"), kv_norm=row("kv_norm"),
        pool=[sp["w_pool"][l].reshape(4 * LANE, LANE), row("pool_scale")],
        ssd=[None, row("ssd_conv_b"), _head8(sp["ssd_dt_bias"][l]), _head8(sp["ssd_a_log"][l]),
             _head8(sp["ssd_d"][l]), row("ssd_norm")],
        lru=[None, row("lru_conv_b"), _block_diag(sp["lru_w_a"][l]), row("lru_b_a"),
             _block_diag(sp["lru_w_i"][l]), row("lru_b_i"), row("lru_lambda")],
        g_mlp=row("g_mlp"), g_ple=row("g_ple"),
    )


_sig = jax.nn.sigmoid
_SSD_CARRY = [(HALO, SSD_XBC)] + [(LANE, LANE)] * 4


def _tiles(rows):
    return dict(tm=_pick(rows, 1024), ta=_pick(rows, 512), tp=_pick(rows, 512), tl=_pick(rows, 512), ts=_pick(rows, 256),
                tssd=_pick(rows, SSD_CHUNK * SSD_CHUNKS_PER_TILE))


def _mixer_tiles(u, dt32):
    return dict(
        cq=(u, 384, U_CQ[0] // 384), ckv=(u, 256, U_CKV[0] // 256), kr=(u, LANE, U_KR[0] // LANE),
        pool=(u, MIX, U_POOL[0] // MIX), z=(u, MIX, U_Z[0] // MIX), xbc=(u, SSD_XBC, U_XBC[0] // SSD_XBC),
        dt=(dt32, LANE, 0), lg=(u, MIX, U_LG[0] // MIX), lx=(u, MIX, U_LX[0] // MIX))


def _add_norm(acc, resid, g):
    x = acc + resid
    return x, _rms(x, g)


def _layer_fwd(x, h, p_bf, ctx, l, pr, g_next, cosf, sinf, early=()):
    rows = x.shape[0]
    ts = _tiles(rows)
    tm = ts["tm"]
    nm = lambda s: f"{s}_l{l}"
    r = dict(x=x)
    if h is None:
        (h,), _ = seq_fwd(nm("rms_in"), f_rms, [pr["g_mix"]], [(x, D_MODEL, 0)], [], [(D_MODEL, bf16)], tm)
    w = dict(_kernel_weights("A", ctx.weights(l, "A", [h, *early])))
    u = matmul(nm("w_in"), h, w["w_in"], outs=(U_DTYPE,))
    dt32 = matmul(nm("w_dt"), h, w["w_dt"])
    mt = _mixer_tiles(u, dt32)
    (cqn,), _ = seq_fwd(nm("rms_q"), f_rms, [pr["q_norm"]], [mt["cq"]], [], [(Q_LORA, bf16)], tm)
    (ckvn,), _ = seq_fwd(nm("rms_kv"), f_rms, [pr["kv_norm"]], [mt["ckv"]], [], [(KV_LORA, bf16)], tm)
    (yb,), pool_saved = seq_fwd(nm("pool"), f_pool, pr["pool"], [mt["pool"]], [(POOL_HALO, MIX)], [(MIX, bf16)], ts["tp"])
    w.update(_kernel_weights("B", ctx.weights(l, "B", yb)))
    pr = dict(pr, ssd=[w["ssd_conv_w"]] + pr["ssd"][1:], lru=[w["lru_conv_w"]] + pr["lru"][1:])
    tables = [(cosf, 0, LANE), (sinf, 0, LANE)]
    qr = matmul(nm("w_uq"), cqn, w["w_uq"], outs=(bf16,), epi=q_rope_epi, extras=tables)
    kr = matmul(nm("w_uk"), ckvn, w["w_uk"], outs=(bf16,), epi=k_rope_epi, extras=[(u, U_KR[0], LANE)] + tables)
    vb = matmul(nm("w_uv"), ckvn, w["w_uv"], outs=(bf16,))
    o, lse = attn_fwd(qr, kr, vb, ts["ta"])
    (yc,), ssd_saved = seq_fwd(nm("ssd"), f_ssd_tile, pr["ssd"], [mt["z"], mt["xbc"], mt["dt"]], _SSD_CARRY, [(MIX, bf16)],
                               ts["tssd"])
    (la, lu), lru_saved = seq_fwd(nm("lru_pre"), f_lru_pre, pr["lru"], [mt["lx"]], [(HALO, MIX)], [(MIX, f32), (MIX, f32)], ts["tl"])
    hh, yd = scan_fwd(la, lu, mt["lg"], ts["ts"])
    ys = [o, yb, yc, yd]
    m, pres = merge_fwd(nm("merge"), ys, w["wb"], u)
    x1, h2 = matmul(nm("w_out"), m, w["w_out"], outs=(f32, bf16), epi=_add_norm, extras=[(x, 0)], rows=[pr["g_mlp"]])
    w.update(_kernel_weights("C", ctx.weights(l, "C", h2)))
    a1, act = matmul(nm("ff1"), h2, w["w_ff1"], outs=(bf16, bf16), epi=lambda acc: (acc, jnp.square(jnp.maximum(acc, 0.0))))
    x2, h3 = matmul(nm("ff2"), act, w["w_ff2"], outs=(f32, bf16), epi=_add_norm, extras=[(x1, 0)], rows=[pr["g_ple"]])
    gl = matmul(nm("ple_gate"), h3, w["w_pg"])
    if g_next is None:
        x3, pe = matmul(nm("ple"), p_bf, w["w_ple"], outs=(f32, bf16), epi=lambda acc, g, xr: (xr + acc * _sig(g), acc),
                        extras=[(gl, 0), (x2, 0)])
        h_next = None
    else:
        def ple_norm(acc, g, xr, gn):
            xo = xr + acc * _sig(g)
            return xo, acc, _rms(xo, gn)

        x3, pe, h_next = matmul(nm("ple"), p_bf, w["w_ple"], outs=(f32, bf16, bf16), epi=ple_norm,
                                extras=[(gl, 0), (x2, 0)], rows=[g_next])
    r.update(h=h, u=u, cqn=cqn, ckvn=ckvn, vb=vb, qr=qr, kr=kr, o=o, lse=lse, ys=ys, pres=pres, m=m, x1=x1,
             h2=h2, a1=a1, act=act, x2=x2, h3=h3, gl=gl, pe=pe, p_bf=p_bf, pool_saved=pool_saved, ssd_saved=ssd_saved,
             lru_saved=lru_saved, la=la, hh=hh, w=w, pr=pr, dt32=dt32)
    return x3, h_next, r


def _norm_bwd(dh, x, resid, g):
    rs = lax.rsqrt(jnp.mean(x * x, axis=-1, keepdims=True) + EPS)
    xhat = x * rs
    dxn = dh * g
    dx = rs * (dxn - xhat * jnp.mean(dxn * xhat, axis=-1, keepdims=True)) + resid
    return dx, jnp.sum(dh * xhat, axis=0, keepdims=True)


def _gate_bwd(d, g, pre):
    s = _sig(g.astype(f32))
    return d * s, d * pre.astype(f32) * s * (1.0 - s)


def _layer_bwd(dx3, r, ctx, l, cosf, sinf, tok, extra_small):
    rows = dx3.shape[0]
    ts = _tiles(rows)
    tm = ts["tm"]
    nm = lambda s: f"{s}_l{l}"
    u, w, pr = r["u"], r["w"], r["pr"]
    mt = _mixer_tiles(u, r["dt32"])
    g = {}
    full = lambda a: (a, a.shape[1], 0)
    dpe, dgl = ew(nm("ple_bwd"), _gate_bwd, [full(dx3), full(r["gl"]), full(r["pe"])], [(D_MODEL, bf16)] * 2, tm)
    g["w_ple"] = matmul(nm("d_w_ple"), r["p_bf"], dpe, ta=True, outs=(bf16,), deps=[tok] if tok is not None else [])
    g["w_pg"] = matmul(nm("d_w_pg"), r["h3"], dgl, ta=True, outs=(bf16,))
    dx2, g["g_ple"] = matmul(nm("d_h3"), dgl, w["w_pg"], tb=True, epi=_norm_bwd, extras=[(r["x2"], 0), (dx3, 0)],
                             rows=[pr["g_ple"]], row_sums=1)
    da1 = matmul(nm("d_act"), dx2, w["w_ff2"], tb=True, outs=(bf16,),
                 epi=lambda acc, a: (acc * 2.0 * jnp.maximum(a, 0.0),), extras=[(r["a1"], 0)])
    g["w_ff2"] = matmul(nm("d_w_ff2"), r["act"], dx2, ta=True, outs=(bf16,))
    g["w_ff1"] = matmul(nm("d_w_ff1"), r["h2"], da1, ta=True, outs=(bf16,), out_blocks=N_DEV)
    tok = ctx.grads(l, "C", dict(w_ff1=g["w_ff1"], w_ff2=g["w_ff2"], w_ple_gate=g["w_pg"], w_ple=g["w_ple"]))
    dx1, g["g_mlp"] = matmul(nm("d_h2"), da1, w["w_ff1"], tb=True, epi=_norm_bwd, extras=[(r["x1"], 0), (dx2, 0)],
                             rows=[pr["g_mlp"]], row_sums=1, deps=[tok])
    def merge_bwd(dm, *gates_and_pres):
        both = [_gate_bwd(dm, gates_and_pres[n], gates_and_pres[4 + n]) for n in range(4)]
        return tuple(b[0] for b in both) + tuple(b[1] for b in both)

    res = matmul(nm("d_merged"), dx1, w["w_out"], tb=True, outs=(bf16,) * 8, epi=merge_bwd,
                 extras=[(u, D_MODEL * n) for n in range(4)] + [(pre, 0) for pre in r["pres"]])
    dpres, dgates = list(res[:4]), list(res[4:])
    g["w_out"] = matmul(nm("d_w_out"), r["m"], dx1, ta=True, outs=(bf16,))
    dys, g["wb"] = [], []
    for n in range(4):
        g["wb"].append(matmul(nm(f"d_w_branch{n}"), r["ys"][n], dpres[n], ta=True, outs=(bf16,)))
        dys.append(matmul(nm(f"d_y{n}"), dpres[n], w["wb"][n], tb=True, outs=(bf16 if n == 0 else f32,)))
    g["ssd"], (dz, dxbc, ddt) = seq_bwd(nm("ssd_bwd"), f_ssd_tile, pr["ssd"], [mt["z"], mt["xbc"], mt["dt"]], [True] * 3,
                                        r["ssd_saved"], [dys[2]], [bf16] * 3, ts["tssd"])
    dqr, dkr_, dv = attn_bwd(r["qr"], r["kr"], r["vb"], dys[0], r["o"], r["lse"], ts["ta"])
    hw = N_HEADS * LANE
    dq, dkn, dkrope = ew(nm("rope_bwd"), rope_bwd, [full(dqr), full(dkr_), full(cosf), full(sinf), full(ddt)],
                         [(hw, bf16), (hw, bf16), (LANE, bf16)], tm)
    g["w_uq"] = matmul(nm("d_w_uq"), r["cqn"], dq, ta=True, outs=(bf16,))
    g["w_uk"] = matmul(nm("d_w_uk"), r["ckvn"], dkn, ta=True, outs=(bf16,))
    g["w_uv"] = matmul(nm("d_w_uv"), r["ckvn"], dv, ta=True, outs=(bf16,))
    dcqn = matmul(nm("d_cqn"), dq, w["w_uq"], tb=True)
    dckvn = matmul(nm("d_ckvn_k"), dkn, w["w_uk"], tb=True)
    dckvn = matmul(nm("d_ckvn_v"), dv, w["w_uv"], tb=True, epi=lambda acc, prev: (acc + prev,), extras=[(dckvn, 0)])
    (g["q_norm"],), (dcq,) = seq_bwd(nm("rms_q_bwd"), f_rms, [pr["q_norm"]], [mt["cq"]], [True], [], [dcqn], [bf16], tm)
    (g["kv_norm"],), (dckv,) = seq_bwd(nm("rms_kv_bwd"), f_rms, [pr["kv_norm"]], [mt["ckv"]], [True], [], [dckvn], [bf16], tm)
    g["pool"], (dpool,) = seq_bwd(nm("pool_bwd"), f_pool, pr["pool"], [mt["pool"]], [True], r["pool_saved"], [dys[1]],
                                  [bf16], ts["tp"])
    da, du, dlg = scan_bwd(r["la"], r["hh"], mt["lg"], dys[3], ts["ts"])
    g["lru"], (dlx,) = seq_bwd(nm("lru_pre_bwd"), f_lru_pre, pr["lru"], [mt["lx"]], [True], r["lru_saved"], [da, du],
                               [bf16], ts["tl"])
    dk = _head_unpad_cols(g["w_uk"], QK_NOPE)
    dv_ = _head_unpad_cols(g["w_uv"], V_HEAD)
    wb0 = g["wb"][0].reshape(N_HEADS, LANE, D_MODEL)[:, :V_HEAD].reshape(MIX, D_MODEL)
    ssd, lru, pool = g["ssd"], g["lru"], g["pool"]
    tok = ctx.grads(l, "B", dict(
        w_uq=_head_unpad_cols(g["w_uq"], QK_NOPE + QK_ROPE).reshape(Q_LORA, -1),
        w_ukv=jnp.concatenate([dk, dv_], axis=2).reshape(KV_LORA, -1), ssd_conv_w=ssd[0], lru_conv_w=lru[0],
        w_branch=jnp.stack([wb0, g["wb"][1], g["wb"][2], g["wb"][3]]), w_out=g["w_out"]))
    du_p = jnp.concatenate(dgates + [dpool, dz, dlg, dlx, dxbc, dcq, dkrope, dckv], axis=1)
    small = dict(
        q_norm=g["q_norm"][0], kv_norm=g["kv_norm"][0],
        w_pool=pool[0].reshape(4, LANE, LANE), pool_scale=pool[1][0],
        ssd_conv_b=ssd[1][0], ssd_dt_bias=ssd[2][0, :8], ssd_a_log=ssd[3][0, :8], ssd_d=ssd[4][0, :8], ssd_norm=ssd[5][0],
        lru_conv_b=lru[1][0], lru_w_a=_block_diag_inv(lru[2]), lru_b_a=lru[3][0], lru_w_i=_block_diag_inv(lru[4]),
        lru_b_i=lru[5][0], lru_lambda=lru[6][0], g_mlp=g["g_mlp"][0], g_ple=g["g_ple"][0])
    tok_small = ctx.small(f"l{l}", [(n, l, small[n]) for n in SMALL if n in small] + extra_small)
    g_w_in = matmul(nm("d_w_in"), r["h"], du_p, ta=True, outs=(bf16,), deps=[tok, tok_small])
    tok = ctx.grads(l, "A", dict(w_in=_w_in_blocks(g_w_in)))
    dx, g_mix = matmul(nm("d_h"), du_p, w["w_in"], tb=True, epi=_norm_bwd, extras=[(r["x"], 0), (dx1, 0)],
                       rows=[pr["g_mix"]], row_sums=1, deps=[tok])
    return dx, tok, ("g_mix", l, g_mix[0])


def _rope_tables(positions):
    inv = 1.0 / (ROPE_THETA ** (jnp.arange(0, QK_ROPE, 2, dtype=f32) / QK_ROPE))
    ang = positions.astype(f32)[:, None] * inv
    cos, sin = jnp.cos(ang), jnp.sin(ang)
    rows = positions.shape[0]
    pad = jnp.zeros((rows, LANE - KR_LANE - QK_ROPE), f32)
    cosf = jnp.concatenate([jnp.ones((rows, KR_LANE), f32), cos, cos, pad], axis=1)
    sinf = jnp.concatenate([jnp.zeros((rows, KR_LANE), f32), -sin, sin, pad], axis=1)
    return cosf, sinf


WEIGHTS = ['g_mix', 'w_in', 'q_norm', 'w_uq', 'kv_norm', 'w_ukv', 'w_pool', 'pool_scale', 'ssd_conv_w', 'ssd_conv_b',
           'ssd_dt_bias', 'ssd_a_log', 'ssd_d', 'ssd_norm', 'lru_conv_w', 'lru_conv_b', 'lru_w_a', 'lru_b_a', 'lru_w_i',
           'lru_b_i', 'lru_lambda', 'w_branch', 'w_out', 'g_mlp', 'w_ff1', 'w_ff2', 'g_ple', 'w_ple_gate', 'w_ple', 'g_final']
SHARDED = dict(w_in=2, w_uq=2, w_ukv=2, ssd_conv_w=2, lru_conv_w=2, w_branch=3, w_out=1, w_ff1=2, w_ff2=1,
               w_ple_gate=1, w_ple=2)
F32_PAYLOAD = ("ssd_conv_w", "lru_conv_w")
DEPTH = 2


SMALL = [n for n in WEIGHTS if n not in SHARDED and n != "g_final"]


def local_step(x, p, positions, tgt, sp, ctx):
    cosf, sinf = _rope_tables(positions)
    prs = [_layer_params(sp, l) for l in range(DEPTH)]
    p_bf = [p[l].astype(bf16) for l in range(DEPTH)]
    early = [cosf, sinf, *p_bf] + [a for pr in prs for v in pr.values() for a in (v if isinstance(v, list) else [v])
                                   if a is not None]
    res, h = [], None
    for l in range(DEPTH):
        g_next = prs[l + 1]["g_mix"] if l + 1 < DEPTH else None
        x, h, r = _layer_fwd(x, h, p_bf[l], ctx, l, prs[l], g_next, cosf, sinf, early if l == 0 else ())
        res.append(r)
    loss8, dx, dgf = loss_head(x, tgt, sp["g_final"][None, :], _tiles(x.shape[0])["tm"])
    tok = None
    pending = ("g_final", None, dgf[0])
    for l in reversed(range(DEPTH)):
        dx, tok, pending = _layer_bwd(dx, res[l], ctx, l, cosf, sinf, tok, [pending])
    ctx.small("last", [pending])
    return loss8[0, 0], dx


def _payload(name, w):
    return w if name in F32_PAYLOAD else w.astype(bf16)


def _blocks(name, g):
    ax = SHARDED[name] - 1
    shape = list(g.shape)
    shape[ax:ax + 1] = [N_DEV, shape[ax] // N_DEV]
    return _payload(name, jnp.moveaxis(g.reshape(shape), ax, 0))


def _assemble(name, shards):
    ax = SHARDED[name] - 1
    shape = list(shards.shape[1:])
    shape[ax] *= N_DEV
    return jnp.moveaxis(shards, 0, ax).reshape(shape)


class _Exchanges:
    def __init__(self, wts):
        self.wts = wts
        self.ag, self.rs, self.sm = {}, {}, {}
        tok = None
        for l in range(DEPTH):
            for grp, names in GROUPS.items():
                h = exchange_start(f"ag_start_{grp}{l}", [_payload(n, wts[n][l]) for n in names], True,
                                   deps=[] if tok is None else [tok])
                tok = h["token"]
                self.ag[(l, grp)] = h
        self.all_started = tok

    def weights(self, l, grp, after):
        afters = list(after) if isinstance(after, (list, tuple)) else [after]
        if (l, grp) == (0, "A"):
            afters.append(self.all_started)
        got = exchange_wait(f"ag_wait_{grp}{l}", self.ag[(l, grp)], afters)
        out = {}
        for n, a in zip(GROUPS[grp], got):
            out[n] = a if n == "w_in" else _assemble(n, a)
        return out

    def grads(self, l, grp, g):
        cut = lambda n: g[n].ndim == self.wts[n].ndim
        h = exchange_start(f"rs_start_{grp}{l}", [g[n] if cut(n) else _blocks(n, g[n]) for n in GROUPS[grp]], False)
        self.rs[(l, grp)] = h
        return h["token"]

    def small(self, tag, entries):
        entries = sorted(entries, key=lambda e: e[2].size % LANE != 0)
        flat = jnp.concatenate([a.reshape(-1) for _, _, a in entries])
        flat = jnp.pad(flat, (0, (-flat.shape[0]) % (8 * LANE))).reshape(-1, LANE)
        h = exchange_start(f"small_start_{tag}", [flat], True)
        self.sm[tag] = (h, [(n, l, a.shape) for n, l, a in entries])
        return h["token"]

    def collect(self, groups, after):
        parts = {}
        for grp in groups:
            for l in reversed(range(DEPTH)):
                got = exchange_wait(f"rs_wait_{grp}{l}", self.rs[(l, grp)], [after])
                for n, a in zip(GROUPS[grp], got):
                    parts.setdefault(n, [None] * DEPTH)[l] = a
        return parts

    def collect_small(self, after):
        gots, where, parts = [], {}, {}
        for tag, (h, layout) in self.sm.items():
            (got,) = exchange_wait(f"small_wait_{tag}", h, [after])
            got = got.reshape(N_DEV, -1)
            off = 0
            for n, l, shape in layout:
                size = 1
                for d in shape:
                    size *= d
                if len(shape) == 1 and size % LANE == 0 and off % LANE == 0:
                    where.setdefault(n, [None] * (1 if l is None else DEPTH))[l or 0] = (len(gots), off)
                else:
                    part = got[:, off:off + size].reshape((N_DEV,) + tuple(shape))
                    if l is None:
                        parts[n] = [part]
                    else:
                        parts.setdefault(n, [None] * DEPTH)[l] = part
                off += size
            gots.append(got)
        return gots, where, parts


def kernel(x, p, positions, g_mix, w_in, q_norm, w_uq, kv_norm, w_ukv, w_pool, pool_scale, ssd_conv_w, ssd_conv_b,
           ssd_dt_bias, ssd_a_log, ssd_d, ssd_norm, lru_conv_w, lru_conv_b, lru_w_a, lru_b_a, lru_w_i, lru_b_i,
           lru_lambda, w_branch, w_out, g_mlp, w_ff1, w_ff2, g_ple, w_ple_gate, w_ple, g_final, loss_target, m_g_mix,
           m_w_in, m_q_norm, m_w_uq, m_kv_norm, m_w_ukv, m_w_pool, m_pool_scale, m_ssd_conv_w, m_ssd_conv_b,
           m_ssd_dt_bias, m_ssd_a_log, m_ssd_d, m_ssd_norm, m_lru_conv_w, m_lru_conv_b, m_lru_w_a, m_lru_b_a,
           m_lru_w_i, m_lru_b_i, m_lru_lambda, m_w_branch, m_w_out, m_g_mlp, m_w_ff1, m_w_ff2, m_g_ple, m_w_ple_gate,
           m_w_ple, m_g_final, v_g_mix, v_w_in, v_q_norm, v_w_uq, v_kv_norm, v_w_ukv, v_w_pool, v_pool_scale,
           v_ssd_conv_w, v_ssd_conv_b, v_ssd_dt_bias, v_ssd_a_log, v_ssd_d, v_ssd_norm, v_lru_conv_w, v_lru_conv_b,
           v_lru_w_a, v_lru_b_a, v_lru_w_i, v_lru_b_i, v_lru_lambda, v_w_branch, v_w_out, v_g_mlp, v_w_ff1, v_w_ff2,
           v_g_ple, v_w_ple_gate, v_w_ple, v_g_final):
    given = dict(locals())
    wts = {n: given[n] for n in WEIGHTS}
    ctx = _Exchanges(wts)
    loss, grad_x = local_step(x[0], p[:, 0], positions[0], loss_target[0], wts, ctx)

    def update(parts):
        out = {}
        for n, eight in parts.items():
            step = adamw_columns if n == "w_in" else adamw
            out[n] = step(f"adamw_{n}", eight, wts[n], given["m_" + n], given["v_" + n])
        return out

    outs = update(ctx.collect(("C", "B"), grad_x))
    late = outs["w_ff1"][1]
    outs.update(update(ctx.collect(("A",), late)))
    gots, where, parts = ctx.collect_small(late)
    outs.update(update(parts))
    names = sorted(where)
    rows = lambda a: a[None] if a.ndim == 1 else a
    res = adamw_packed("adamw_vectors", gots, [where[n] for n in names], [rows(wts[n]) for n in names],
                       [rows(given["m_" + n]) for n in names], [rows(given["v_" + n]) for n in names])
    for n, four in zip(names, res):
        outs[n] = [a[0] for a in four] if wts[n].ndim == 1 else four
    loss = lax.psum(loss, AXES)
    return (loss, grad_x[None], *[outs[n][0] for n in WEIGHTS], *[outs[n][1] for n in WEIGHTS],
            *[outs[n][2] for n in WEIGHTS], *[outs[n][3] for n in WEIGHTS])
```
